```python
import math
import jax, jax.numpy as jnp
from jax import lax
import numpy as np

D_MODEL = 1024
BATCH = 8
SEQ = 2048
DEPTH = 4

N_MIXERS = 4
N_A = (DEPTH + 3) // 4
N_B = (DEPTH + 2) // 4
N_C = (DEPTH + 1) // 4
N_D = DEPTH // 4
EPS = 1e-6
NEG_INF = -1e30

SSM_WIDTH = D_MODEL
SSM_GROUP = 16
SSM_GROUPS = SSM_WIDTH // SSM_GROUP
SSM_STATE = 64
DT_MIN = 1e-3
DT_MAX = 1e-1

HEAD_DIM = 64
SWA_HEADS = D_MODEL // HEAD_DIM
SWA_KV_HEADS = SWA_HEADS // 8
SWA_WIDTH = SWA_HEADS * HEAD_DIM
WINDOW = 128

REL_BUCKETS = 32
REL_MAX_DIST = 128

MLA_HEADS = 16
MLA_NOPE = 64
MLA_ROPE = 32
MLA_V = 64
MLA_KV_RANK = 256
MLA_Q_RANK = 768
MLA_WIDTH = MLA_HEADS * MLA_V
ROPE_BASE = 10000.0
Q_BLOCK = 128

SGU_WIDTH = D_MODEL
SGU_CHUNK = 128
SGU_GROUPS = 16
SGU_GROUP_DIM = SGU_WIDTH // SGU_GROUPS

kernel_name = 'hybrid_interleaved_s5_swa_mla_sgu'


def rmsnorm(x, g):
    xf = x.astype(jnp.float32)
    y = xf * lax.rsqrt(jnp.mean(xf * xf, axis=-1, keepdims=True) + EPS)
    return (y * g.astype(jnp.float32)).astype(x.dtype)


def layernorm(x, g, b):
    xf = x.astype(jnp.float32)
    mu = jnp.mean(xf, axis=-1, keepdims=True)
    var = jnp.mean(jnp.square(xf - mu), axis=-1, keepdims=True)
    y = (xf - mu) * lax.rsqrt(var + EPS) * g.astype(jnp.float32) + b.astype(jnp.float32)
    return y.astype(x.dtype)


def _ssm_combine(left, right):
    a1r, a1i, b1r, b1i = left
    a2r, a2i, b2r, b2i = right
    return (a2r * a1r - a2i * a1i,
            a2r * a1i + a2i * a1r,
            a2r * b1r - a2i * b1i + b2r,
            a2r * b1i + a2i * b1r + b2i)


def s5_mixer(u, lam_re, lam_im, log_dt, b_re, b_im, c_re, c_im, d_skip, w_glu, b_glu):
    f32 = jnp.float32
    bsz, L, _ = u.shape
    ug = u.astype(f32).reshape(bsz, L, SSM_GROUPS, SSM_GROUP)
    lr = lam_re.astype(f32)
    li = lam_im.astype(f32)
    dt = jnp.exp(log_dt.astype(f32))[:, None]
    mag = jnp.exp(lr * dt)
    ab_re = mag * jnp.cos(li * dt)
    ab_im = mag * jnp.sin(li * dt)
    den = lr * lr + li * li
    nr = ab_re - 1.0
    f_re = (nr * lr + ab_im * li) / den
    f_im = (ab_im * lr - nr * li) / den
    br = b_re.astype(f32)
    bi = b_im.astype(f32)
    bb_re = f_re[..., None] * br - f_im[..., None] * bi
    bb_im = f_re[..., None] * bi + f_im[..., None] * br
    bu_re = jnp.einsum('blgh,gph->blgp', ug, bb_re)
    bu_im = jnp.einsum('blgh,gph->blgp', ug, bb_im)
    a_re = jnp.broadcast_to(ab_re, (1, L) + ab_re.shape)
    a_im = jnp.broadcast_to(ab_im, (1, L) + ab_im.shape)
    _, _, s_re, s_im = lax.associative_scan(_ssm_combine, (a_re, a_im, bu_re, bu_im), axis=1)
    y = (jnp.einsum('blgp,ghp->blgh', s_re, c_re.astype(f32))
         - jnp.einsum('blgp,ghp->blgh', s_im, c_im.astype(f32)))
    y = y.reshape(bsz, L, SSM_WIDTH) + d_skip.astype(f32) * u.astype(f32)
    y = jax.nn.gelu(y).astype(u.dtype)
    return y * jax.nn.sigmoid(y @ w_glu + b_glu)


def s5_branch(h, w_in, lam_re, lam_im, log_dt, b_re, b_im, c_re, c_im, d_skip, w_glu, b_glu, w_out):
    u, z = jnp.split(h @ w_in, [SSM_WIDTH], axis=-1)
    y = s5_mixer(u, lam_re, lam_im, log_dt, b_re, b_im, c_re, c_im, d_skip, w_glu, b_glu)
    return (y * jax.nn.silu(z)) @ w_out


def t5_bucket(dist):
    max_exact = REL_BUCKETS // 2
    dist_f = jnp.maximum(dist, 1).astype(jnp.float32)
    large = max_exact + (jnp.log(dist_f / max_exact) / math.log(REL_MAX_DIST / max_exact)
                         * (REL_BUCKETS - max_exact)).astype(jnp.int32)
    large = jnp.minimum(large, REL_BUCKETS - 1)
    return jnp.where(dist < max_exact, dist, large)


def sliding_window_attention(q, k, v, sinks, rel_bias):
    bsz, L = q.shape[0], q.shape[1]
    nb = L // WINDOW
    grp = SWA_HEADS // SWA_KV_HEADS
    qb = q.reshape(bsz, nb, WINDOW, SWA_KV_HEADS, grp, HEAD_DIM)

    def band(t):
        prev = jnp.pad(t, ((0, 0), (WINDOW, 0), (0, 0), (0, 0)))[:, :L]
        shp = (bsz, nb, WINDOW, SWA_KV_HEADS, HEAD_DIM)
        return jnp.concatenate([prev.reshape(shp), t.reshape(shp)], axis=2)

    kb = band(k)
    vb = band(v)
    s = jnp.einsum('bnqhgd,bnkhd->bnhgqk', qb, kb).astype(jnp.float32) * (HEAD_DIM ** -0.5)
    qi = jnp.arange(WINDOW)[:, None]
    kj = jnp.arange(2 * WINDOW)[None, :]
    dist = qi + WINDOW - kj
    blk = jnp.arange(nb)[:, None, None]
    valid = (dist >= 0) & (dist < WINDOW) & (blk * WINDOW + kj - WINDOW >= 0)
    bias = rel_bias[t5_bucket(jnp.maximum(dist, 0))]
    bias = jnp.transpose(bias, (2, 0, 1)).reshape(SWA_KV_HEADS, grp, WINDOW, 2 * WINDOW).astype(jnp.float32)
    s = jnp.where(valid[None, :, None, None], s + bias, NEG_INF)
    sink = jnp.broadcast_to(sinks.astype(jnp.float32).reshape(SWA_KV_HEADS, grp, 1, 1), s.shape[:-1] + (1,))
    p = jax.nn.softmax(jnp.concatenate([s, sink], axis=-1), axis=-1)[..., :-1]
    o = jnp.einsum('bnhgqk,bnkhd->bnqhgd', p.astype(v.dtype), vb)
    return o.reshape(bsz, L, SWA_WIDTH)


def swa_branch(h, w_in, sinks, w_out, rel_bias):
    bsz, L, _ = h.shape
    kv_w = SWA_KV_HEADS * HEAD_DIM
    q, k, v, z = jnp.split(h @ w_in, [SWA_WIDTH, SWA_WIDTH + kv_w, SWA_WIDTH + 2 * kv_w], axis=-1)
    q = q.reshape(bsz, L, SWA_HEADS, HEAD_DIM)
    k = k.reshape(bsz, L, SWA_KV_HEADS, HEAD_DIM)
    v = v.reshape(bsz, L, SWA_KV_HEADS, HEAD_DIM)
    o = sliding_window_attention(q, k, v, sinks, rel_bias)
    return (o * jax.nn.silu(z)) @ w_out


def rope_tables(L):
    inv = ROPE_BASE ** (-jnp.arange(0, MLA_ROPE, 2, dtype=jnp.float32) / MLA_ROPE)
    ang = jnp.arange(L, dtype=jnp.float32)[:, None] * inv[None, :]
    return jnp.cos(ang), jnp.sin(ang)


def apply_rope(x, cos, sin):
    xf = x.astype(jnp.float32)
    x1, x2 = jnp.split(xf, 2, axis=-1)
    return jnp.concatenate([x1 * cos - x2 * sin, x2 * cos + x1 * sin], axis=-1).astype(x.dtype)


def causal_block_attention(q, k, v):
    bsz, L, H, dk = q.shape
    nb = L // Q_BLOCK
    scale = dk ** -0.5
    qb = q.reshape(bsz, nb, Q_BLOCK, H, dk).transpose(1, 0, 2, 3, 4)
    kpos = jnp.arange(L)

    def one_block(args):
        qi, n = args
        s = jnp.einsum('bqhd,bkhd->bhqk', qi, k).astype(jnp.float32) * scale
        qpos = n * Q_BLOCK + jnp.arange(Q_BLOCK)
        s = jnp.where(kpos[None, :] <= qpos[:, None], s, NEG_INF)
        p = jax.nn.softmax(s, axis=-1).astype(v.dtype)
        return jnp.einsum('bhqk,bkhd->bqhd', p, v)

    o = lax.map(one_block, (qb, jnp.arange(nb)))
    return o.transpose(1, 0, 2, 3, 4).reshape(bsz, L, H, v.shape[-1])


def mla_branch(h, w_in, q_norm, kv_norm, w_uq, w_ukv, w_out):
    bsz, L, _ = h.shape
    c_q, c_kv, k_rope, z = jnp.split(
        h @ w_in, [MLA_Q_RANK, MLA_Q_RANK + MLA_KV_RANK, MLA_Q_RANK + MLA_KV_RANK + MLA_ROPE], axis=-1)
    q = (rmsnorm(c_q, q_norm) @ w_uq).reshape(bsz, L, MLA_HEADS, MLA_NOPE + MLA_ROPE)
    kv = (rmsnorm(c_kv, kv_norm) @ w_ukv).reshape(bsz, L, MLA_HEADS, MLA_NOPE + MLA_V)
    cos, sin = rope_tables(L)
    q = jnp.concatenate([q[..., :MLA_NOPE], apply_rope(q[..., MLA_NOPE:], cos[:, None], sin[:, None])], axis=-1)
    k_rope = apply_rope(k_rope, cos, sin)
    k = jnp.concatenate([kv[..., :MLA_NOPE],
                         jnp.broadcast_to(k_rope[:, :, None, :], (bsz, L, MLA_HEADS, MLA_ROPE))], axis=-1)
    o = causal_block_attention(q, k, kv[..., MLA_NOPE:])
    return (o.reshape(bsz, L, MLA_WIDTH) * jax.nn.silu(z)) @ w_out


def sgu_branch(h, w_in, ln_g, ln_b, w_s, b_s, w_out):
    bsz, L, _ = h.shape
    uv, z = jnp.split(h @ w_in, [2 * SGU_WIDTH], axis=-1)
    u, v = jnp.split(jax.nn.gelu(uv), 2, axis=-1)
    v = layernorm(v, ln_g, ln_b).reshape(bsz, L // SGU_CHUNK, SGU_CHUNK, SGU_GROUPS, SGU_GROUP_DIM)
    tril = jnp.tril(jnp.ones((SGU_CHUNK, SGU_CHUNK), dtype=bool))
    w = jnp.where(tril[None], w_s, 0.0)
    s = jnp.einsum('gts,bnsgc->bntgc', w, v) + b_s.T[:, :, None]
    s = s.reshape(bsz, L, SGU_WIDTH)
    return (u * s * jax.nn.silu(z)) @ w_out


def setup_inputs(seed: int = 0) -> dict:
    key = jax.random.key(seed)
    ks = iter(jax.random.split(key, 40))
    f32 = jnp.float32

    def nrm(shape, scale):
        return jax.random.normal(next(ks), shape, f32) * scale

    x = nrm((BATCH, SEQ, D_MODEL), 1.0)
    pre_norm = 1.0 + nrm((DEPTH, D_MODEL), 0.05)
    post_norm = 1.0 + nrm((DEPTH, D_MODEL), 0.05)
    rel_bias = nrm((REL_BUCKETS, SWA_HEADS), 0.5)
    a_w_in = nrm((N_A, D_MODEL, 2 * SSM_WIDTH), D_MODEL ** -0.5)
    n_idx = jnp.arange(SSM_STATE, dtype=f32)
    a_lam_re = -0.5 + nrm((N_A, SSM_GROUPS, SSM_STATE), 0.01)
    a_lam_im = jnp.pi * n_idx + nrm((N_A, SSM_GROUPS, SSM_STATE), 0.01)
    a_log_dt = jax.random.uniform(next(ks), (N_A, SSM_GROUPS), f32, math.log(DT_MIN), math.log(DT_MAX))
    a_b_re = nrm((N_A, SSM_GROUPS, SSM_STATE, SSM_GROUP), (2 * SSM_GROUP) ** -0.5)
    a_b_im = nrm((N_A, SSM_GROUPS, SSM_STATE, SSM_GROUP), (2 * SSM_GROUP) ** -0.5)
    a_c_re = nrm((N_A, SSM_GROUPS, SSM_GROUP, SSM_STATE), SSM_STATE ** -0.5)
    a_c_im = nrm((N_A, SSM_GROUPS, SSM_GROUP, SSM_STATE), SSM_STATE ** -0.5)
    a_d = nrm((N_A, SSM_WIDTH), 1.0)
    a_w_glu = nrm((N_A, SSM_WIDTH, SSM_WIDTH), SSM_WIDTH ** -0.5)
    a_b_glu = nrm((N_A, SSM_WIDTH), 0.02)
    a_w_out = nrm((N_A, SSM_WIDTH, D_MODEL), SSM_WIDTH ** -0.5)
    b_w_in = nrm((N_B, D_MODEL, 2 * SWA_WIDTH + 2 * SWA_KV_HEADS * HEAD_DIM), D_MODEL ** -0.5)
    b_sinks = nrm((N_B, SWA_HEADS), 1.0)
    b_w_out = nrm((N_B, SWA_WIDTH, D_MODEL), SWA_WIDTH ** -0.5)
    c_w_in = nrm((N_C, D_MODEL, MLA_Q_RANK + MLA_KV_RANK + MLA_ROPE + MLA_WIDTH), D_MODEL ** -0.5)
    c_q_norm = 1.0 + nrm((N_C, MLA_Q_RANK), 0.05)
    c_kv_norm = 1.0 + nrm((N_C, MLA_KV_RANK), 0.05)
    c_w_uq = nrm((N_C, MLA_Q_RANK, MLA_HEADS * (MLA_NOPE + MLA_ROPE)), MLA_Q_RANK ** -0.5)
    c_w_ukv = nrm((N_C, MLA_KV_RANK, MLA_HEADS * (MLA_NOPE + MLA_V)), MLA_KV_RANK ** -0.5)
    c_w_out = nrm((N_C, MLA_WIDTH, D_MODEL), MLA_WIDTH ** -0.5)
    d_w_in = nrm((N_D, D_MODEL, 3 * SGU_WIDTH), D_MODEL ** -0.5)
    d_ln_g = 1.0 + nrm((N_D, SGU_WIDTH), 0.05)
    d_ln_b = nrm((N_D, SGU_WIDTH), 0.02)
    d_w_s = nrm((N_D, SGU_GROUPS, SGU_CHUNK, SGU_CHUNK), 0.5 * SGU_CHUNK ** -0.5)
    d_b_s = 1.0 + nrm((N_D, SGU_GROUPS, SGU_CHUNK), 0.1)
    d_w_out = nrm((N_D, SGU_WIDTH, D_MODEL), SGU_WIDTH ** -0.5)
    return {'x': x, 'pre_norm': pre_norm, 'post_norm': post_norm, 'rel_bias': rel_bias,
            'a_w_in': a_w_in, 'a_lam_re': a_lam_re, 'a_lam_im': a_lam_im, 'a_log_dt': a_log_dt,
            'a_b_re': a_b_re, 'a_b_im': a_b_im, 'a_c_re': a_c_re, 'a_c_im': a_c_im, 'a_d': a_d,
            'a_w_glu': a_w_glu, 'a_b_glu': a_b_glu, 'a_w_out': a_w_out,
            'b_w_in': b_w_in, 'b_sinks': b_sinks, 'b_w_out': b_w_out,
            'c_w_in': c_w_in, 'c_q_norm': c_q_norm, 'c_kv_norm': c_kv_norm, 'c_w_uq': c_w_uq,
            'c_w_ukv': c_w_ukv, 'c_w_out': c_w_out,
            'd_w_in': d_w_in, 'd_ln_g': d_ln_g, 'd_ln_b': d_ln_b, 'd_w_s': d_w_s, 'd_b_s': d_b_s,
            'd_w_out': d_w_out}


def reference(x, pre_norm, post_norm, rel_bias,
              a_w_in, a_lam_re, a_lam_im, a_log_dt, a_b_re, a_b_im, a_c_re, a_c_im, a_d,
              a_w_glu, a_b_glu, a_w_out,
              b_w_in, b_sinks, b_w_out,
              c_w_in, c_q_norm, c_kv_norm, c_w_uq, c_w_ukv, c_w_out,
              d_w_in, d_ln_g, d_ln_b, d_w_s, d_b_s, d_w_out):
    for i in range(DEPTH):
        kind = i % N_MIXERS
        j = i // N_MIXERS
        h = rmsnorm(x, pre_norm[i])
        if kind == 0:
            y = s5_branch(h, a_w_in[j], a_lam_re[j], a_lam_im[j], a_log_dt[j], a_b_re[j], a_b_im[j],
                          a_c_re[j], a_c_im[j], a_d[j], a_w_glu[j], a_b_glu[j], a_w_out[j])
        elif kind == 1:
            y = swa_branch(h, b_w_in[j], b_sinks[j], b_w_out[j], rel_bias)
        elif kind == 2:
            y = mla_branch(h, c_w_in[j], c_q_norm[j], c_kv_norm[j], c_w_uq[j], c_w_ukv[j], c_w_out[j])
        else:
            y = sgu_branch(h, d_w_in[j], d_ln_g[j], d_ln_b[j], d_w_s[j], d_b_s[j], d_w_out[j])
        x = x + rmsnorm(y, post_norm[i])
    return x
```

```python
import functools
import math

import jax
import jax.numpy as jnp
import numpy as np
from jax import lax
from jax.experimental import pallas as pl
from jax.experimental.pallas import tpu as pltpu

F32 = jnp.float32
BF16 = jnp.bfloat16

D_MODEL = 1024
EPS = 1e-6
NEG_INF = -1e30
LANES = 128
HALF = LANES // 2

SSM_GROUP = 16
SSM_STATE = 64
S5_CH_BLOCK = LANES
S5_GROUPS_PER_BLOCK = S5_CH_BLOCK // SSM_GROUP
S5_STATE_BLOCK = S5_GROUPS_PER_BLOCK * SSM_STATE
S5_T = 64

HEAD_DIM = 64
SWA_HEADS = 16
SWA_KV_HEADS = 2
SWA_GROUP = SWA_HEADS // SWA_KV_HEADS
WINDOW = 128
REL_BUCKETS = 32
REL_MAX_DIST = 128

MLA_HEADS = 16
MLA_NOPE = 64
MLA_ROPE = 32
MLA_V = 64
MLA_KV_RANK = 256
MLA_Q_RANK = 768
ROPE_BASE = 10000.0
MLA_TQ = 256
MLA_TK = 256

SGU_CHUNK = 128
SGU_GROUPS = 16

VMEM_LIMIT = 56 * 1024 * 1024


def _cparams(sem):
    return pltpu.CompilerParams(dimension_semantics=sem, vmem_limit_bytes=VMEM_LIMIT)


def _rms(x, g):
    return x * lax.rsqrt(jnp.mean(x * x, axis=-1, keepdims=True) + EPS) * g


def _dot(a, b):
    return jnp.dot(a, b, preferred_element_type=F32)


def _dot_nt(a, b):
    return lax.dot_general(a, b, (((1,), (1,)), ((), ())), preferred_element_type=F32)


def _full(shape):
    n = len(shape)
    return pl.BlockSpec(shape, lambda *_: (0,) * n)


def _pre_kernel(x_ref, g_ref, w_ref, *out_refs, splits, scales):
    hb = _rms(x_ref[0], g_ref[...]).astype(BF16)
    off = 0
    for o_ref, width, scale in zip(out_refs, splits, scales):
        r = _dot(hb, w_ref[:, off:off + width])
        if scale != 1.0:
            r = r * scale
        o_ref[0] = r.astype(o_ref.dtype)
        off += width


def _pre(x, g, w, splits, dtypes, scales, tm, name):
    bsz, L, d = x.shape
    kern = functools.partial(_pre_kernel, splits=tuple(splits), scales=tuple(scales))
    return pl.pallas_call(
        kern,
        out_shape=[jax.ShapeDtypeStruct((bsz, L, s), dt) for s, dt in zip(splits, dtypes)],
        grid=(bsz, L // tm),
        in_specs=[pl.BlockSpec((1, tm, d), lambda b, i: (b, i, 0)),
                  _full((1, d)), _full(w.shape)],
        out_specs=[pl.BlockSpec((1, tm, s), lambda b, i: (b, i, 0)) for s in splits],
        compiler_params=_cparams(("parallel", "parallel")),
        name=name,
    )(x, g.reshape(1, d), w)


def _post_kernel(o_ref, z_ref, x_ref, w_ref, g_ref, out_ref):
    o = o_ref[0].astype(F32) * jax.nn.silu(z_ref[0])
    r = _dot(o.astype(BF16), w_ref[...])
    out_ref[0] = x_ref[0] + _rms(r, g_ref[...])


def _post(o, z, x, w, g, tm, name):
    bsz, L, d = x.shape
    width = o.shape[-1]
    return pl.pallas_call(
        _post_kernel,
        out_shape=jax.ShapeDtypeStruct(x.shape, x.dtype),
        grid=(bsz, L // tm),
        in_specs=[pl.BlockSpec((1, tm, width), lambda b, i: (b, i, 0)),
                  pl.BlockSpec((1, tm, width), lambda b, i: (b, i, 0)),
                  pl.BlockSpec((1, tm, d), lambda b, i: (b, i, 0)),
                  _full(w.shape), _full((1, d))],
        out_specs=pl.BlockSpec((1, tm, d), lambda b, i: (b, i, 0)),
        compiler_params=_cparams(("parallel", "parallel")),
        name=name,
    )(o, z, x, w, g.reshape(1, d))


def _s5_pre_kernel(x_ref, g_ref, w_ref, u_ref, z_ref, *, tt):
    bsz = x_ref.shape[0]
    width = u_ref.shape[1] // bsz
    x = x_ref[...].reshape(bsz * tt, x_ref.shape[2])
    hb = _rms(x, g_ref[...]).astype(BF16)
    u = _dot(hb, w_ref[:, :width])
    z = _dot(hb, w_ref[:, width:])
    for b in range(bsz):
        u_ref[:, b * width:(b + 1) * width] = u[b * tt:(b + 1) * tt]
        z_ref[:, b * width:(b + 1) * width] = z[b * tt:(b + 1) * tt]


def _s5_scan_kernel(u_ref, bb_ref, cc_ref, ar_ref, ai_ref, d_ref, y_ref, s_ref, carry_ref, *, tsteps):
    nblk = bb_ref.shape[0]
    sb = S5_STATE_BLOCK
    rows = carry_ref.shape[1]

    @pl.when(pl.program_id(0) == 0)
    def _():
        carry_ref[...] = jnp.zeros_like(carry_ref)

    for i in range(nblk):
        ub = u_ref[:, i * LANES:(i + 1) * LANES]
        s_ref[...] = _dot(ub.astype(BF16), bb_ref[i])
        ar = ar_ref[i]
        ai = ai_ref[i]

        def step(t, st):
            sr, si = st
            r0 = pl.multiple_of(t * rows, rows)
            xr = s_ref[pl.ds(r0, rows), 0:sb]
            xi = s_ref[pl.ds(r0, rows), sb:2 * sb]
            nr = ar * sr - ai * si + xr
            ni = ar * si + ai * sr + xi
            s_ref[pl.ds(r0, rows), 0:sb] = nr
            s_ref[pl.ds(r0, rows), sb:2 * sb] = ni
            return nr, ni

        sr, si = lax.fori_loop(0, tsteps, step,
                               (carry_ref[i, :, 0:sb], carry_ref[i, :, sb:2 * sb]), unroll=8)
        carry_ref[i, :, 0:sb] = sr
        carry_ref[i, :, sb:2 * sb] = si
        y = _dot(s_ref[...].astype(BF16), cc_ref[i]) + d_ref[:, i * LANES:(i + 1) * LANES] * ub
        y_ref[:, i * LANES:(i + 1) * LANES] = jax.nn.gelu(y)


def _s5_post_kernel(y_ref, z_ref, x_ref, wg_ref, bg_ref, wo_ref, g_ref, out_ref, *, tt):
    bsz = x_ref.shape[0]
    width = y_ref.shape[1] // bsz
    y = jnp.concatenate([y_ref[:, b * width:(b + 1) * width] for b in range(bsz)], axis=0)
    z = jnp.concatenate([z_ref[:, b * width:(b + 1) * width] for b in range(bsz)], axis=0)
    gate = jax.nn.sigmoid(_dot(y.astype(BF16), wg_ref[...]) + bg_ref[...])
    o = y * gate * jax.nn.silu(z)
    r = _dot(o.astype(BF16), wo_ref[...])
    x = x_ref[...].reshape(bsz * tt, x_ref.shape[2])
    out_ref[...] = (x + _rms(r, g_ref[...])).reshape(out_ref.shape)


def _s5_discretize(lam_re, lam_im, log_dt, b_re, b_im):
    dt = jnp.exp(log_dt)[:, None]
    mag = jnp.exp(lam_re * dt)
    ab_re = mag * jnp.cos(lam_im * dt)
    ab_im = mag * jnp.sin(lam_im * dt)
    den = lam_re * lam_re + lam_im * lam_im
    nr = ab_re - 1.0
    f_re = (nr * lam_re + ab_im * lam_im) / den
    f_im = (ab_im * lam_re - nr * lam_im) / den
    bb_re = f_re[..., None] * b_re - f_im[..., None] * b_im
    bb_im = f_re[..., None] * b_im + f_im[..., None] * b_re
    return ab_re, ab_im, bb_re, bb_im


def _s5_layer(x, pre_g, post_g, w_in, lam_re, lam_im, log_dt, b_re, b_im, c_re, c_im, d_skip,
              w_glu, b_glu, w_out):
    bsz, L, d = x.shape
    width = w_in.shape[1] // 2
    nblk = width // S5_CH_BLOCK
    gpb = S5_GROUPS_PER_BLOCK
    tt = S5_T
    rows = bsz * tt

    u, z = pl.pallas_call(
        functools.partial(_s5_pre_kernel, tt=tt),
        out_shape=[jax.ShapeDtypeStruct((L, bsz * width), F32)] * 2,
        grid=(L // tt,),
        in_specs=[pl.BlockSpec((bsz, tt, d), lambda i: (0, i, 0)),
                  _full((1, d)), _full(w_in.shape)],
        out_specs=[pl.BlockSpec((tt, bsz * width), lambda i: (i, 0))] * 2,
        compiler_params=_cparams(("parallel",)),
        name="s5_pre",
    )(x, pre_g.reshape(1, d), w_in.astype(BF16))

    ab_re, ab_im, bb_re, bb_im = _s5_discretize(lam_re, lam_im, log_dt, b_re, b_im)
    eye = jnp.eye(gpb, dtype=F32)

    def pack_b(bb):
        t = bb.reshape(nblk, gpb, SSM_STATE, SSM_GROUP)
        return jnp.einsum('igph,gk->ikhgp', t, eye).reshape(nblk, S5_CH_BLOCK, S5_STATE_BLOCK)

    def pack_c(cc):
        t = cc.reshape(nblk, gpb, SSM_GROUP, SSM_STATE)
        return jnp.einsum('ighp,gk->igpkh', t, eye).reshape(nblk, S5_STATE_BLOCK, S5_CH_BLOCK)

    bb = jnp.concatenate([pack_b(bb_re), pack_b(bb_im)], axis=2).astype(BF16)
    cc = jnp.concatenate([pack_c(c_re), -pack_c(c_im)], axis=1).astype(BF16)
    ar = jnp.broadcast_to(ab_re.reshape(nblk, 1, S5_STATE_BLOCK), (nblk, bsz, S5_STATE_BLOCK))
    ai = jnp.broadcast_to(ab_im.reshape(nblk, 1, S5_STATE_BLOCK), (nblk, bsz, S5_STATE_BLOCK))

    y = pl.pallas_call(
        functools.partial(_s5_scan_kernel, tsteps=tt),
        out_shape=jax.ShapeDtypeStruct((L * bsz, width), F32),
        grid=(L // tt,),
        in_specs=[pl.BlockSpec((rows, width), lambda i: (i, 0)),
                  _full(bb.shape), _full(cc.shape), _full(ar.shape), _full(ai.shape),
                  _full((1, width))],
        out_specs=pl.BlockSpec((rows, width), lambda i: (i, 0)),
        scratch_shapes=[pltpu.VMEM((rows, 2 * S5_STATE_BLOCK), F32),
                        pltpu.VMEM((nblk, bsz, 2 * S5_STATE_BLOCK), F32)],
        compiler_params=_cparams(("arbitrary",)),
        name="s5_scan",
    )(u.reshape(L * bsz, width), bb, cc, ar, ai, d_skip.reshape(1, width))

    return pl.pallas_call(
        functools.partial(_s5_post_kernel, tt=tt),
        out_shape=jax.ShapeDtypeStruct(x.shape, x.dtype),
        grid=(L // tt,),
        in_specs=[pl.BlockSpec((tt, bsz * width), lambda i: (i, 0)),
                  pl.BlockSpec((tt, bsz * width), lambda i: (i, 0)),
                  pl.BlockSpec((bsz, tt, d), lambda i: (0, i, 0)),
                  _full(w_glu.shape), _full((1, width)), _full(w_out.shape), _full((1, d))],
        out_specs=pl.BlockSpec((bsz, tt, d), lambda i: (0, i, 0)),
        compiler_params=_cparams(("parallel",)),
        name="s5_post",
    )(y.reshape(L, bsz * width), z, x, w_glu.astype(BF16), b_glu.reshape(1, width),
      w_out.astype(BF16), post_g.reshape(1, d))


def _t5_bucket_table():
    qi = np.arange(WINDOW)[:, None]
    kj = np.arange(2 * WINDOW)[None, :]
    dist = qi + WINDOW - kj
    valid = (dist >= 0) & (dist < WINDOW)
    dpos = jnp.maximum(jnp.asarray(dist), 0)
    max_exact = REL_BUCKETS // 2
    dist_f = jnp.maximum(dpos, 1).astype(F32)
    large = max_exact + (jnp.log(dist_f / max_exact) / math.log(REL_MAX_DIST / max_exact)
                         * (REL_BUCKETS - max_exact)).astype(jnp.int32)
    large = jnp.minimum(large, REL_BUCKETS - 1)
    return jnp.where(dpos < max_exact, dpos, large), jnp.asarray(valid)


def _swa_kernel(q_ref, kvp_ref, kvc_ref, bias_ref, sink_ref, o_ref):
    n = pl.program_id(1)
    lane = lax.broadcasted_iota(jnp.int32, (1, LANES), 1)
    lo = lane < HALF
    col = lax.broadcasted_iota(jnp.int32, (1, 2 * WINDOW), 1)
    pen = jnp.where(jnp.logical_and(n == 0, col < WINDOW), NEG_INF, 0.0).astype(F32)
    kv = jnp.concatenate([kvp_ref[0], kvc_ref[0]], axis=0)
    for h in range(SWA_KV_HEADS):
        k2 = kv[:, h * LANES:(h + 1) * LANES]
        v2 = kv[:, (SWA_KV_HEADS + h) * LANES:(SWA_KV_HEADS + h + 1) * LANES]
        parts = []
        for g in range(SWA_GROUP):
            head = h * SWA_GROUP + g
            qb = q_ref[0, :, (head // 2) * LANES:(head // 2 + 1) * LANES]
            keep = lo if head % 2 == 0 else jnp.logical_not(lo)
            parts.append(jnp.where(keep, qb, jnp.zeros_like(qb)))
        qs = jnp.concatenate(parts, axis=0)
        s = _dot_nt(qs, k2) + bias_ref[h] + pen
        sink = sink_ref[h]
        m = jnp.maximum(jnp.max(s, axis=-1, keepdims=True), sink)
        p = jnp.exp(s - m)
        denom = jnp.sum(p, axis=-1, keepdims=True) + jnp.exp(sink - m)
        o = _dot(p.astype(BF16), v2) * (1.0 / denom)
        for jj in range(SWA_GROUP // 2):
            a = o[(2 * jj) * WINDOW:(2 * jj + 1) * WINDOW]
            b = o[(2 * jj + 1) * WINDOW:(2 * jj + 2) * WINDOW]
            blk = h * (SWA_GROUP // 2) + jj
            o_ref[0, :, blk * LANES:(blk + 1) * LANES] = jnp.where(lo, a, b)


def _swa_layer(x, pre_g, post_g, w_in, sinks, w_out, rel_bias):
    bsz, L, d = x.shape
    width = SWA_HEADS * HEAD_DIM
    kvw = SWA_KV_HEADS * HEAD_DIM
    nb = L // WINDOW
    dup = np.concatenate([np.tile(np.arange(HEAD_DIM), 2) + hh * HEAD_DIM for hh in range(SWA_KV_HEADS)])
    cols = np.concatenate([np.arange(width), width + dup, width + kvw + dup,
                           width + 2 * kvw + np.arange(width)])
    w = w_in[:, cols].astype(BF16)
    q, kv, z = _pre(x, pre_g, w, [width, 4 * kvw, width], [BF16, BF16, F32],
                    [HEAD_DIM ** -0.5, 1.0, 1.0], 512, "swa_pre")

    bucket, valid = _t5_bucket_table()
    bias = jnp.transpose(rel_bias[bucket], (2, 0, 1)).astype(F32)
    bias = jnp.where(valid[None], bias, NEG_INF).reshape(SWA_KV_HEADS, SWA_GROUP * WINDOW, 2 * WINDOW)
    sink = jnp.broadcast_to(sinks.astype(F32).reshape(SWA_KV_HEADS, SWA_GROUP, 1, 1),
                            (SWA_KV_HEADS, SWA_GROUP, WINDOW, 1)).reshape(SWA_KV_HEADS, SWA_GROUP * WINDOW, 1)

    o = pl.pallas_call(
        _swa_kernel,
        out_shape=jax.ShapeDtypeStruct((bsz, L, width), F32),
        grid=(bsz, nb),
        in_specs=[pl.BlockSpec((1, WINDOW, width), lambda b, n: (b, n, 0)),
                  pl.BlockSpec((1, WINDOW, 4 * kvw), lambda b, n: (b, jnp.maximum(n - 1, 0), 0)),
                  pl.BlockSpec((1, WINDOW, 4 * kvw), lambda b, n: (b, n, 0)),
                  _full(bias.shape), _full(sink.shape)],
        out_specs=pl.BlockSpec((1, WINDOW, width), lambda b, n: (b, n, 0)),
        compiler_params=_cparams(("parallel", "parallel")),
        name="swa_attn",
    )(q, kv, kv, bias, sink)
    return _post(o, z, x, w_out.astype(BF16), post_g, 512, "swa_post")


def _mla_pre_kernel(x_ref, g_ref, w_ref, qn_ref, kvn_ref, wq_ref, wkv_ref, cq_ref, sq_ref, ck_ref, sk_ref,
                    oqn_ref, oqr_ref, okn_ref, okr_ref, ov_ref, oz_ref, *, scale):
    nope = MLA_HEADS * MLA_NOPE
    rope = MLA_HEADS * MLA_ROPE
    vw = MLA_HEADS * MLA_V
    hb = _rms(x_ref[0], g_ref[...]).astype(BF16)
    o1 = MLA_Q_RANK
    o2 = o1 + MLA_KV_RANK
    o3 = o2 + vw
    cq = _dot(hb, w_ref[:, :o1])
    ckv = _dot(hb, w_ref[:, o1:o2])
    oz_ref[0] = _dot(hb, w_ref[:, o2:o3])
    kr = _dot(hb, w_ref[:, o3:o3 + LANES])
    krs = _dot(hb, w_ref[:, o3 + LANES:o3 + 2 * LANES])
    okr_ref[0] = (kr * ck_ref[...] + krs * sk_ref[...]).astype(BF16)
    cqb = _rms(cq, qn_ref[...]).astype(BF16)
    oqn_ref[0] = (_dot(cqb, wq_ref[:, :nope]) * scale).astype(BF16)
    qr = _dot(cqb, wq_ref[:, nope:nope + rope])
    qrs = _dot(cqb, wq_ref[:, nope + rope:nope + 2 * rope])
    oqr_ref[0] = ((qr * cq_ref[...] + qrs * sq_ref[...]) * scale).astype(BF16)
    ckb = _rms(ckv, kvn_ref[...]).astype(BF16)
    okn_ref[0] = _dot(ckb, wkv_ref[:, :nope]).astype(BF16)
    ov_ref[0] = _dot(ckb, wkv_ref[:, nope:nope + vw]).astype(BF16)


def _mla_attn_kernel(qn_ref, qr_ref, kn_ref, kr_ref, v_ref, o_ref, acc_ref, m_ref, l_ref):
    tq = qn_ref.shape[1]
    tk = MLA_TK
    i = pl.program_id(1)
    lane = lax.broadcasted_iota(jnp.int32, (1, LANES), 1)
    lo = lane < HALF
    row = lax.broadcasted_iota(jnp.int32, (tq, tk), 0)
    colk = lax.broadcasted_iota(jnp.int32, (tq, tk), 1)
    causal = colk <= row
    causal2 = jnp.concatenate([causal, causal], axis=0)

    for p in range(MLA_HEADS // 2):
        qn = qn_ref[0, :, p * LANES:(p + 1) * LANES]
        rblk = (2 * p * MLA_ROPE) // LANES
        qr = qr_ref[0, :, rblk * LANES:(rblk + 1) * LANES]
        r0 = (2 * p * MLA_ROPE) % LANES
        zn = jnp.zeros_like(qn)
        rm0 = jnp.logical_and(lane >= r0, lane < r0 + MLA_ROPE)
        rm1 = jnp.logical_and(lane >= r0 + MLA_ROPE, lane < r0 + 2 * MLA_ROPE)
        q0 = jnp.concatenate([jnp.where(lo, qn, zn), jnp.where(rm0, qr, zn)], axis=1)
        q1 = jnp.concatenate([jnp.where(lo, zn, qn), jnp.where(rm1, qr, zn)], axis=1)
        qs = jnp.concatenate([q0, q1], axis=0)

        m_ref[...] = jnp.full(m_ref.shape, NEG_INF, F32)
        l_ref[...] = jnp.zeros(l_ref.shape, F32)
        acc_ref[...] = jnp.zeros(acc_ref.shape, F32)

        def kv_step(j, masked):
            ks = pl.multiple_of(j * tk, tk)
            kc = jnp.concatenate([kn_ref[0, pl.ds(ks, tk), p * LANES:(p + 1) * LANES],
                                  kr_ref[0, pl.ds(ks, tk), :]], axis=1)
            s = _dot_nt(qs, kc)
            if masked:
                s = jnp.where(causal2, s, NEG_INF)
            m_prev = m_ref[...]
            m_new = jnp.maximum(m_prev, jnp.max(s, axis=-1, keepdims=True))
            alpha = jnp.exp(m_prev - m_new)
            pr = jnp.exp(s - m_new)
            l_ref[...] = alpha * l_ref[...] + jnp.sum(pr, axis=-1, keepdims=True)
            pv = _dot(pr.astype(BF16), v_ref[0, pl.ds(ks, tk), p * LANES:(p + 1) * LANES])
            acc_ref[...] = alpha * acc_ref[...] + pv
            m_ref[...] = m_new

        def body(j, c):
            kv_step(j, False)
            return c

        lax.fori_loop(0, i, body, 0)
        kv_step(i, True)
        a = acc_ref[...] * (1.0 / l_ref[...])
        o_ref[0, :, p * LANES:(p + 1) * LANES] = jnp.where(lo, a[:tq], a[tq:])


def _mla_layer(x, pre_g, post_g, w_in, q_norm, kv_norm, w_uq, w_ukv, w_out):
    bsz, L, d = x.shape
    H = MLA_HEADS
    dq = MLA_NOPE + MLA_ROPE
    nope = H * MLA_NOPE
    rope = H * MLA_ROPE
    vw = H * MLA_V
    half = MLA_ROPE // 2
    o_kr = MLA_Q_RANK + MLA_KV_RANK
    o_z = o_kr + MLA_ROPE
    lane_d = np.arange(LANES) % MLA_ROPE
    cols_in = np.concatenate([np.arange(o_kr), o_z + np.arange(vw),
                              o_kr + lane_d, o_kr + (lane_d + half) % MLA_ROPE])
    w1 = w_in[:, cols_in].astype(BF16)
    hh = np.arange(H)[:, None]
    cols_n = (hh * dq + np.arange(MLA_NOPE)[None]).reshape(-1)
    cols_r = (hh * dq + MLA_NOPE + np.arange(MLA_ROPE)[None]).reshape(-1)
    cols_rs = (hh * dq + MLA_NOPE + (np.arange(MLA_ROPE)[None] + half) % MLA_ROPE).reshape(-1)
    wq = w_uq[:, np.concatenate([cols_n, cols_r, cols_rs])].astype(BF16)
    dkv = MLA_NOPE + MLA_V
    cols_kn = (hh * dkv + np.arange(MLA_NOPE)[None]).reshape(-1)
    cols_v = (hh * dkv + MLA_NOPE + np.arange(MLA_V)[None]).reshape(-1)
    wkv = w_ukv[:, np.concatenate([cols_kn, cols_v])].astype(BF16)
    inv = ROPE_BASE ** (-jnp.arange(0, MLA_ROPE, 2, dtype=F32) / MLA_ROPE)
    ang = jnp.arange(L, dtype=F32)[:, None] * inv[None, :]
    cos, sin = jnp.cos(ang), jnp.sin(ang)
    cos32 = jnp.concatenate([cos, cos], axis=1)
    sin32 = jnp.concatenate([-sin, sin], axis=1)
    cos_q, sin_q = jnp.tile(cos32, (1, H)), jnp.tile(sin32, (1, H))
    cos_k, sin_k = cos_q[:, :LANES], sin_q[:, :LANES]

    tm = 512
    tok = lambda w_: pl.BlockSpec((1, tm, w_), lambda b, i: (b, i, 0))
    tab = lambda w_: pl.BlockSpec((tm, w_), lambda b, i: (i, 0))
    qn, qr, kn, kr, v, z = pl.pallas_call(
        functools.partial(_mla_pre_kernel, scale=dq ** -0.5),
        out_shape=[jax.ShapeDtypeStruct((bsz, L, nope), BF16),
                   jax.ShapeDtypeStruct((bsz, L, rope), BF16),
                   jax.ShapeDtypeStruct((bsz, L, nope), BF16),
                   jax.ShapeDtypeStruct((bsz, L, LANES), BF16),
                   jax.ShapeDtypeStruct((bsz, L, vw), BF16),
                   jax.ShapeDtypeStruct((bsz, L, vw), F32)],
        grid=(bsz, L // tm),
        in_specs=[tok(d), _full((1, d)), _full(w1.shape), _full((1, MLA_Q_RANK)), _full((1, MLA_KV_RANK)),
                  _full(wq.shape), _full(wkv.shape), tab(rope), tab(rope), tab(LANES), tab(LANES)],
        out_specs=[tok(nope), tok(rope), tok(nope), tok(LANES), tok(vw), tok(vw)],
        compiler_params=_cparams(("parallel", "parallel")),
        name="mla_pre",
    )(x, pre_g.reshape(1, d), w1, q_norm.reshape(1, -1), kv_norm.reshape(1, -1), wq, wkv,
      cos_q, sin_q, cos_k, sin_k)

    tq = MLA_TQ
    qspec = lambda w_: pl.BlockSpec((1, tq, w_), lambda b, i: (b, i, 0))
    kspec = lambda w_: pl.BlockSpec((1, L, w_), lambda b, i: (b, 0, 0))
    o = pl.pallas_call(
        _mla_attn_kernel,
        out_shape=jax.ShapeDtypeStruct((bsz, L, vw), F32),
        grid=(bsz, L // tq),
        in_specs=[qspec(nope), qspec(rope), kspec(nope), kspec(LANES), kspec(vw)],
        out_specs=qspec(vw),
        scratch_shapes=[pltpu.VMEM((2 * tq, LANES), F32),
                        pltpu.VMEM((2 * tq, 1), F32),
                        pltpu.VMEM((2 * tq, 1), F32)],
        compiler_params=_cparams(("parallel", "arbitrary")),
        name="mla_attn",
    )(qn, qr, kn, kr, v)
    return _post(o, z, x, w_out.astype(BF16), post_g, 512, "mla_post")


def _sgu_kernel(x_ref, g_ref, w_ref, lng_ref, lnb_ref, ws_ref, bs_ref, wo_ref, pg_ref, out_ref, s_ref):
    width = wo_ref.shape[0]
    tm = x_ref.shape[1]
    lane = lax.broadcasted_iota(jnp.int32, (1, LANES), 1)
    lo = lane < HALF
    x = x_ref[0]
    hb = _rms(x, g_ref[...]).astype(BF16)
    v = jax.nn.gelu(_dot(hb, w_ref[:, width:2 * width]))
    mu = jnp.mean(v, axis=-1, keepdims=True)
    vc = v - mu
    var = jnp.mean(vc * vc, axis=-1, keepdims=True)
    vb = (vc * lax.rsqrt(var + EPS) * lng_ref[...] + lnb_ref[...]).astype(BF16)
    for c in range(tm // SGU_CHUNK):
        for jj in range(width // LANES):
            blk = vb[c * SGU_CHUNK:(c + 1) * SGU_CHUNK, jj * LANES:(jj + 1) * LANES]
            r = _dot(ws_ref[jj], blk)
            s_ref[c * SGU_CHUNK:(c + 1) * SGU_CHUNK, jj * LANES:(jj + 1) * LANES] = (
                jnp.where(lo, r[:SGU_CHUNK], r[SGU_CHUNK:]) + bs_ref[jj])
    u = jax.nn.gelu(_dot(hb, w_ref[:, :width]))
    z = _dot(hb, w_ref[:, 2 * width:])
    o = u * s_ref[...] * jax.nn.silu(z)
    r = _dot(o.astype(BF16), wo_ref[...])
    out_ref[0] = x + _rms(r, pg_ref[...])


def _sgu_layer(x, pre_g, post_g, w_in, ln_g, ln_b, w_s, b_s, w_out):
    bsz, L, d = x.shape
    width = w_out.shape[0]
    T = SGU_CHUNK
    gd = width // SGU_GROUPS
    tril = jnp.tril(jnp.ones((T, T), dtype=bool))
    ws = jnp.where(tril[None], w_s, 0.0).reshape(SGU_GROUPS // 2, 2 * T, T).astype(BF16)
    bs = jnp.repeat(b_s.astype(F32).T, gd, axis=1)
    bs = bs.reshape(T, width // LANES, LANES).transpose(1, 0, 2)
    tm = 512
    return pl.pallas_call(
        _sgu_kernel,
        out_shape=jax.ShapeDtypeStruct(x.shape, x.dtype),
        grid=(bsz, L // tm),
        in_specs=[pl.BlockSpec((1, tm, d), lambda b, i: (b, i, 0)),
                  _full((1, d)), _full(w_in.shape), _full((1, width)), _full((1, width)),
                  _full(ws.shape), _full(bs.shape), _full(w_out.shape), _full((1, d))],
        out_specs=pl.BlockSpec((1, tm, d), lambda b, i: (b, i, 0)),
        scratch_shapes=[pltpu.VMEM((tm, width), F32)],
        compiler_params=_cparams(("parallel", "parallel")),
        name="sgu",
    )(x, pre_g.reshape(1, d), w_in.astype(BF16), ln_g.reshape(1, width), ln_b.reshape(1, width),
      ws, bs, w_out.astype(BF16), post_g.reshape(1, d))


def kernel(x, pre_norm, post_norm, rel_bias, a_w_in, a_lam_re, a_lam_im, a_log_dt, a_b_re, a_b_im, a_c_re, a_c_im, a_d, a_w_glu, a_b_glu, a_w_out, b_w_in, b_sinks, b_w_out, c_w_in, c_q_norm, c_kv_norm, c_w_uq, c_w_ukv, c_w_out, d_w_in, d_ln_g, d_ln_b, d_w_s, d_b_s, d_w_out):
    depth = pre_norm.shape[0]
    for i in range(depth):
        kind, j = i % 4, i // 4
        if kind == 0:
            x = _s5_layer(x, pre_norm[i], post_norm[i], a_w_in[j], a_lam_re[j], a_lam_im[j], a_log_dt[j],
                          a_b_re[j], a_b_im[j], a_c_re[j], a_c_im[j], a_d[j], a_w_glu[j], a_b_glu[j],
                          a_w_out[j])
        elif kind == 1:
            x = _swa_layer(x, pre_norm[i], post_norm[i], b_w_in[j], b_sinks[j], b_w_out[j], rel_bias)
        elif kind == 2:
            x = _mla_layer(x, pre_norm[i], post_norm[i], c_w_in[j], c_q_norm[j], c_kv_norm[j], c_w_uq[j],
                           c_w_ukv[j], c_w_out[j])
        else:
            x = _sgu_layer(x, pre_norm[i], post_norm[i], d_w_in[j], d_ln_g[j], d_ln_b[j], d_w_s[j],
                           d_b_s[j], d_w_out[j])
    return x
```

```python
import functools
import math

import jax
import jax.numpy as jnp
import numpy as np
from jax import lax
from jax.experimental import pallas as pl
from jax.experimental.pallas import tpu as pltpu

F32 = jnp.float32
BF16 = jnp.bfloat16

D_MODEL = 1024
EPS = 1e-6
NEG_INF = -1e30
LANES = 128
HALF = LANES // 2

SSM_GROUP = 16
SSM_STATE = 64
S5_CH_BLOCK = LANES
S5_GROUPS_PER_BLOCK = S5_CH_BLOCK // SSM_GROUP
S5_STATE_BLOCK = S5_GROUPS_PER_BLOCK * SSM_STATE
S5_T = 64

HEAD_DIM = 64
SWA_HEADS = 16
SWA_KV_HEADS = 2
SWA_GROUP = SWA_HEADS // SWA_KV_HEADS
WINDOW = 128
REL_BUCKETS = 32
REL_MAX_DIST = 128

MLA_HEADS = 16
MLA_NOPE = 64
MLA_ROPE = 32
MLA_V = 64
MLA_KV_RANK = 256
MLA_Q_RANK = 768
ROPE_BASE = 10000.0
MLA_TQ = 256
MLA_TK = 256

SGU_CHUNK = 128
SGU_GROUPS = 16

VMEM_LIMIT = 56 * 1024 * 1024


def _cparams(sem):
    return pltpu.CompilerParams(dimension_semantics=sem, vmem_limit_bytes=VMEM_LIMIT)


def _rms(x, g):
    return x * lax.rsqrt(jnp.mean(x * x, axis=-1, keepdims=True) + EPS) * g


def _dot(a, b):
    return jnp.dot(a, b, preferred_element_type=F32)


def _dot_nt(a, b):
    return lax.dot_general(a, b, (((1,), (1,)), ((), ())), preferred_element_type=F32)


def _full(shape):
    n = len(shape)
    return pl.BlockSpec(shape, lambda *_: (0,) * n)


def _pre_kernel(x_ref, g_ref, w_ref, *out_refs, splits, scales):
    hb = _rms(x_ref[0], g_ref[...]).astype(BF16)
    off = 0
    for o_ref, width, scale in zip(out_refs, splits, scales):
        r = _dot(hb, w_ref[:, off:off + width])
        if scale != 1.0:
            r = r * scale
        o_ref[0] = r.astype(o_ref.dtype)
        off += width


def _pre(x, g, w, splits, dtypes, scales, tm, name):
    bsz, L, d = x.shape
    kern = functools.partial(_pre_kernel, splits=tuple(splits), scales=tuple(scales))
    return pl.pallas_call(
        kern,
        out_shape=[jax.ShapeDtypeStruct((bsz, L, s), dt) for s, dt in zip(splits, dtypes)],
        grid=(bsz, L // tm),
        in_specs=[pl.BlockSpec((1, tm, d), lambda b, i: (b, i, 0)),
                  _full((1, d)), _full(w.shape)],
        out_specs=[pl.BlockSpec((1, tm, s), lambda b, i: (b, i, 0)) for s in splits],
        compiler_params=_cparams(("parallel", "parallel")),
        name=name,
    )(x, g.reshape(1, d), w)


def _post_kernel(o_ref, z_ref, x_ref, w_ref, g_ref, out_ref):
    o = o_ref[0].astype(F32) * jax.nn.silu(z_ref[0])
    r = _dot(o.astype(BF16), w_ref[...])
    out_ref[0] = x_ref[0] + _rms(r, g_ref[...])


def _post(o, z, x, w, g, tm, name):
    bsz, L, d = x.shape
    width = o.shape[-1]
    return pl.pallas_call(
        _post_kernel,
        out_shape=jax.ShapeDtypeStruct(x.shape, x.dtype),
        grid=(bsz, L // tm),
        in_specs=[pl.BlockSpec((1, tm, width), lambda b, i: (b, i, 0)),
                  pl.BlockSpec((1, tm, width), lambda b, i: (b, i, 0)),
                  pl.BlockSpec((1, tm, d), lambda b, i: (b, i, 0)),
                  _full(w.shape), _full((1, d))],
        out_specs=pl.BlockSpec((1, tm, d), lambda b, i: (b, i, 0)),
        compiler_params=_cparams(("parallel", "parallel")),
        name=name,
    )(o, z, x, w, g.reshape(1, d))


def _s5_pre_kernel(x_ref, g_ref, w_ref, u_ref, z_ref, *, tt):
    bsz = x_ref.shape[0]
    width = u_ref.shape[1] // bsz
    x = x_ref[...].reshape(bsz * tt, x_ref.shape[2])
    hb = _rms(x, g_ref[...]).astype(BF16)
    u = _dot(hb, w_ref[:, :width])
    z = _dot(hb, w_ref[:, width:])
    for b in range(bsz):
        u_ref[:, b * width:(b + 1) * width] = u[b * tt:(b + 1) * tt]
        z_ref[:, b * width:(b + 1) * width] = z[b * tt:(b + 1) * tt]


def _s5_scan_kernel(u_ref, bb_ref, cc_ref, ar_ref, ai_ref, d_ref, y_ref, s_ref, carry_ref, *, tsteps):
    nblk = bb_ref.shape[0]
    sb = S5_STATE_BLOCK
    rows = carry_ref.shape[1]

    @pl.when(pl.program_id(0) == 0)
    def _():
        carry_ref[...] = jnp.zeros_like(carry_ref)

    for i in range(nblk):
        ub = u_ref[:, i * LANES:(i + 1) * LANES]
        s_ref[...] = _dot(ub.astype(BF16), bb_ref[i])
        ar = ar_ref[i]
        ai = ai_ref[i]

        def step(t, st):
            sr, si = st
            r0 = pl.multiple_of(t * rows, rows)
            xr = s_ref[pl.ds(r0, rows), 0:sb]
            xi = s_ref[pl.ds(r0, rows), sb:2 * sb]
            nr = ar * sr - ai * si + xr
            ni = ar * si + ai * sr + xi
            s_ref[pl.ds(r0, rows), 0:sb] = nr
            s_ref[pl.ds(r0, rows), sb:2 * sb] = ni
            return nr, ni

        sr, si = lax.fori_loop(0, tsteps, step,
                               (carry_ref[i, :, 0:sb], carry_ref[i, :, sb:2 * sb]), unroll=8)
        carry_ref[i, :, 0:sb] = sr
        carry_ref[i, :, sb:2 * sb] = si
        y = _dot(s_ref[...].astype(BF16), cc_ref[i]) + d_ref[:, i * LANES:(i + 1) * LANES] * ub
        y_ref[:, i * LANES:(i + 1) * LANES] = jax.nn.gelu(y)


def _s5_post_kernel(y_ref, z_ref, x_ref, wg_ref, bg_ref, wo_ref, g_ref, out_ref, *, tt):
    bsz = x_ref.shape[0]
    width = y_ref.shape[1] // bsz
    y = jnp.concatenate([y_ref[:, b * width:(b + 1) * width] for b in range(bsz)], axis=0)
    z = jnp.concatenate([z_ref[:, b * width:(b + 1) * width] for b in range(bsz)], axis=0)
    gate = jax.nn.sigmoid(_dot(y.astype(BF16), wg_ref[...]) + bg_ref[...])
    o = y * gate * jax.nn.silu(z)
    r = _dot(o.astype(BF16), wo_ref[...])
    x = x_ref[...].reshape(bsz * tt, x_ref.shape[2])
    out_ref[...] = (x + _rms(r, g_ref[...])).reshape(out_ref.shape)


def _s5_discretize(lam_re, lam_im, log_dt, b_re, b_im):
    dt = jnp.exp(log_dt)[:, None]
    mag = jnp.exp(lam_re * dt)
    ab_re = mag * jnp.cos(lam_im * dt)
    ab_im = mag * jnp.sin(lam_im * dt)
    den = lam_re * lam_re + lam_im * lam_im
    nr = ab_re - 1.0
    f_re = (nr * lam_re + ab_im * lam_im) / den
    f_im = (ab_im * lam_re - nr * lam_im) / den
    bb_re = f_re[..., None] * b_re - f_im[..., None] * b_im
    bb_im = f_re[..., None] * b_im + f_im[..., None] * b_re
    return ab_re, ab_im, bb_re, bb_im


def _s5_layer(x, pre_g, post_g, w_in, lam_re, lam_im, log_dt, b_re, b_im, c_re, c_im, d_skip,
              w_glu, b_glu, w_out):
    bsz, L, d = x.shape
    width = w_in.shape[1] // 2
    nblk = width // S5_CH_BLOCK
    gpb = S5_GROUPS_PER_BLOCK
    tt = S5_T
    rows = bsz * tt

    u, z = pl.pallas_call(
        functools.partial(_s5_pre_kernel, tt=tt),
        out_shape=[jax.ShapeDtypeStruct((L, bsz * width), F32)] * 2,
        grid=(L // tt,),
        in_specs=[pl.BlockSpec((bsz, tt, d), lambda i: (0, i, 0)),
                  _full((1, d)), _full(w_in.shape)],
        out_specs=[pl.BlockSpec((tt, bsz * width), lambda i: (i, 0))] * 2,
        compiler_params=_cparams(("parallel",)),
        name="s5_pre",
    )(x, pre_g.reshape(1, d), w_in.astype(BF16))

    ab_re, ab_im, bb_re, bb_im = _s5_discretize(lam_re, lam_im, log_dt, b_re, b_im)
    eye = jnp.eye(gpb, dtype=F32)

    def pack_b(bb):
        t = bb.reshape(nblk, gpb, SSM_STATE, SSM_GROUP)
        return jnp.einsum('igph,gk->ikhgp', t, eye).reshape(nblk, S5_CH_BLOCK, S5_STATE_BLOCK)

    def pack_c(cc):
        t = cc.reshape(nblk, gpb, SSM_GROUP, SSM_STATE)
        return jnp.einsum('ighp,gk->igpkh', t, eye).reshape(nblk, S5_STATE_BLOCK, S5_CH_BLOCK)

    bb = jnp.concatenate([pack_b(bb_re), pack_b(bb_im)], axis=2).astype(BF16)
    cc = jnp.concatenate([pack_c(c_re), -pack_c(c_im)], axis=1).astype(BF16)
    ar = jnp.broadcast_to(ab_re.reshape(nblk, 1, S5_STATE_BLOCK), (nblk, bsz, S5_STATE_BLOCK))
    ai = jnp.broadcast_to(ab_im.reshape(nblk, 1, S5_STATE_BLOCK), (nblk, bsz, S5_STATE_BLOCK))

    y = pl.pallas_call(
        functools.partial(_s5_scan_kernel, tsteps=tt),
        out_shape=jax.ShapeDtypeStruct((L * bsz, width), F32),
        grid=(L // tt,),
        in_specs=[pl.BlockSpec((rows, width), lambda i: (i, 0)),
                  _full(bb.shape), _full(cc.shape), _full(ar.shape), _full(ai.shape),
                  _full((1, width))],
        out_specs=pl.BlockSpec((rows, width), lambda i: (i, 0)),
        scratch_shapes=[pltpu.VMEM((rows, 2 * S5_STATE_BLOCK), F32),
                        pltpu.VMEM((nblk, bsz, 2 * S5_STATE_BLOCK), F32)],
        compiler_params=_cparams(("arbitrary",)),
        name="s5_scan",
    )(u.reshape(L * bsz, width), bb, cc, ar, ai, d_skip.reshape(1, width))

    return pl.pallas_call(
        functools.partial(_s5_post_kernel, tt=tt),
        out_shape=jax.ShapeDtypeStruct(x.shape, x.dtype),
        grid=(L // tt,),
        in_specs=[pl.BlockSpec((tt, bsz * width), lambda i: (i, 0)),
                  pl.BlockSpec((tt, bsz * width), lambda i: (i, 0)),
                  pl.BlockSpec((bsz, tt, d), lambda i: (0, i, 0)),
                  _full(w_glu.shape), _full((1, width)), _full(w_out.shape), _full((1, d))],
        out_specs=pl.BlockSpec((bsz, tt, d), lambda i: (0, i, 0)),
        compiler_params=_cparams(("parallel",)),
        name="s5_post",
    )(y.reshape(L, bsz * width), z, x, w_glu.astype(BF16), b_glu.reshape(1, width),
      w_out.astype(BF16), post_g.reshape(1, d))


def _t5_bucket_table():
    qi = np.arange(WINDOW)[:, None]
    kj = np.arange(2 * WINDOW)[None, :]
    dist = qi + WINDOW - kj
    valid = (dist >= 0) & (dist < WINDOW)
    dpos = jnp.maximum(jnp.asarray(dist), 0)
    max_exact = REL_BUCKETS // 2
    dist_f = jnp.maximum(dpos, 1).astype(F32)
    large = max_exact + (jnp.log(dist_f / max_exact) / math.log(REL_MAX_DIST / max_exact)
                         * (REL_BUCKETS - max_exact)).astype(jnp.int32)
    large = jnp.minimum(large, REL_BUCKETS - 1)
    return jnp.where(dpos < max_exact, dpos, large), jnp.asarray(valid)


def _swa_kernel(q_ref, kvp_ref, kvc_ref, bias_ref, sink_ref, o_ref):
    n = pl.program_id(1)
    lane = lax.broadcasted_iota(jnp.int32, (1, LANES), 1)
    lo = lane < HALF
    col = lax.broadcasted_iota(jnp.int32, (1, 2 * WINDOW), 1)
    pen = jnp.where(jnp.logical_and(n == 0, col < WINDOW), NEG_INF, 0.0).astype(F32)
    kv = jnp.concatenate([kvp_ref[0], kvc_ref[0]], axis=0)
    for h in range(SWA_KV_HEADS):
        k2 = kv[:, h * LANES:(h + 1) * LANES]
        v2 = kv[:, (SWA_KV_HEADS + h) * LANES:(SWA_KV_HEADS + h + 1) * LANES]
        parts = []
        for g in range(SWA_GROUP):
            head = h * SWA_GROUP + g
            qb = q_ref[0, :, (head // 2) * LANES:(head // 2 + 1) * LANES]
            keep = lo if head % 2 == 0 else jnp.logical_not(lo)
            parts.append(jnp.where(keep, qb, jnp.zeros_like(qb)))
        qs = jnp.concatenate(parts, axis=0)
        s = _dot_nt(qs, k2) + bias_ref[h] + pen
        sink = sink_ref[h]
        m = jnp.maximum(jnp.max(s, axis=-1, keepdims=True), sink)
        p = jnp.exp(s - m)
        denom = jnp.sum(p, axis=-1, keepdims=True) + jnp.exp(sink - m)
        o = _dot(p.astype(BF16), v2) * (1.0 / denom)
        for jj in range(SWA_GROUP // 2):
            a = o[(2 * jj) * WINDOW:(2 * jj + 1) * WINDOW]
            b = o[(2 * jj + 1) * WINDOW:(2 * jj + 2) * WINDOW]
            blk = h * (SWA_GROUP // 2) + jj
            o_ref[0, :, blk * LANES:(blk + 1) * LANES] = jnp.where(lo, a, b)


def _swa_layer(x, pre_g, post_g, w_in, sinks, w_out, rel_bias):
    bsz, L, d = x.shape
    width = SWA_HEADS * HEAD_DIM
    kvw = SWA_KV_HEADS * HEAD_DIM
    nb = L // WINDOW
    dup = np.concatenate([np.tile(np.arange(HEAD_DIM), 2) + hh * HEAD_DIM for hh in range(SWA_KV_HEADS)])
    cols = np.concatenate([np.arange(width), width + dup, width + kvw + dup,
                           width + 2 * kvw + np.arange(width)])
    w = w_in[:, cols].astype(BF16)
    q, kv, z = _pre(x, pre_g, w, [width, 4 * kvw, width], [BF16, BF16, F32],
                    [HEAD_DIM ** -0.5, 1.0, 1.0], 512, "swa_pre")

    bucket, valid = _t5_bucket_table()
    bias = jnp.transpose(rel_bias[bucket], (2, 0, 1)).astype(F32)
    bias = jnp.where(valid[None], bias, NEG_INF).reshape(SWA_KV_HEADS, SWA_GROUP * WINDOW, 2 * WINDOW)
    sink = jnp.broadcast_to(sinks.astype(F32).reshape(SWA_KV_HEADS, SWA_GROUP, 1, 1),
                            (SWA_KV_HEADS, SWA_GROUP, WINDOW, 1)).reshape(SWA_KV_HEADS, SWA_GROUP * WINDOW, 1)

    o = pl.pallas_call(
        _swa_kernel,
        out_shape=jax.ShapeDtypeStruct((bsz, L, width), F32),
        grid=(bsz, nb),
        in_specs=[pl.BlockSpec((1, WINDOW, width), lambda b, n: (b, n, 0)),
                  pl.BlockSpec((1, WINDOW, 4 * kvw), lambda b, n: (b, jnp.maximum(n - 1, 0), 0)),
                  pl.BlockSpec((1, WINDOW, 4 * kvw), lambda b, n: (b, n, 0)),
                  _full(bias.shape), _full(sink.shape)],
        out_specs=pl.BlockSpec((1, WINDOW, width), lambda b, n: (b, n, 0)),
        compiler_params=_cparams(("parallel", "parallel")),
        name="swa_attn",
    )(q, kv, kv, bias, sink)
    return _post(o, z, x, w_out.astype(BF16), post_g, 512, "swa_post")


def _mla_pre_kernel(x_ref, g_ref, w_ref, qn_ref, kvn_ref, wq_ref, wkv_ref, wvt_ref, cq_ref, sq_ref, ck_ref, sk_ref,
                    oqn_ref, oqr_ref, okn_ref, okr_ref, ov_ref, oz_ref, *, scale):
    nope = MLA_HEADS * MLA_NOPE
    rope = MLA_HEADS * MLA_ROPE
    vw = MLA_HEADS * MLA_V
    hb = _rms(x_ref[0], g_ref[...]).astype(BF16)
    o1 = MLA_Q_RANK
    o2 = o1 + MLA_KV_RANK
    o3 = o2 + vw
    cq = _dot(hb, w_ref[:, :o1])
    ckv = _dot(hb, w_ref[:, o1:o2])
    oz_ref[0] = _dot(hb, w_ref[:, o2:o3])
    kr = _dot(hb, w_ref[:, o3:o3 + LANES])
    krs = _dot(hb, w_ref[:, o3 + LANES:o3 + 2 * LANES])
    okr_ref[0] = (kr * ck_ref[...] + krs * sk_ref[...]).astype(BF16)
    cqb = _rms(cq, qn_ref[...]).astype(BF16)
    oqn_ref[0] = (_dot_nt(wq_ref[:nope], cqb) * scale).astype(BF16)
    qr = _dot_nt(wq_ref[nope:nope + rope], cqb)
    qrs = _dot_nt(wq_ref[nope + rope:nope + 2 * rope], cqb)
    oqr_ref[0] = ((qr * cq_ref[...] + qrs * sq_ref[...]) * scale).astype(BF16)
    ckb = _rms(ckv, kvn_ref[...]).astype(BF16)
    okn_ref[0] = _dot(ckb, wkv_ref[:, :nope]).astype(BF16)
    vt = _dot_nt(wvt_ref[...], ckb).astype(BF16)
    tk = ov_ref.shape[3]
    for c in range(ov_ref.shape[1]):
        ov_ref[0, c] = vt[:, c * tk:(c + 1) * tk]


def _mla_attn_kernel(qn_ref, qr_ref, kn_ref, kr_ref, v_ref, o_ref, qs_ref, acc_ref, m_ref, l_ref):
    tq = qn_ref.shape[2]
    tk = v_ref.shape[3]
    npairs = MLA_HEADS // 2
    i = pl.program_id(1)
    krow = lax.broadcasted_iota(jnp.int32, (tk, 2 * tq), 0)
    qcol = lax.broadcasted_iota(jnp.int32, (tk, 2 * tq), 1)
    causal = krow <= jnp.where(qcol >= tq, qcol - tq, qcol)

    zn = jnp.zeros((MLA_NOPE, tq), BF16)
    zr = jnp.zeros((LANES - MLA_ROPE, tq), BF16)
    for p in range(npairs):
        qn = qn_ref[0, p * LANES:(p + 1) * LANES, :]
        r0 = 2 * p * MLA_ROPE
        c0 = jnp.concatenate([qn[:MLA_NOPE], zn, qr_ref[0, r0:r0 + MLA_ROPE, :], zr], axis=0)
        c1 = jnp.concatenate([zn, qn[MLA_NOPE:], qr_ref[0, r0 + MLA_ROPE:r0 + 2 * MLA_ROPE, :], zr], axis=0)
        qs_ref[p] = jnp.concatenate([c0, c1], axis=1)

    m_ref[...] = jnp.full(m_ref.shape, NEG_INF, F32)
    l_ref[...] = jnp.zeros(l_ref.shape, F32)
    acc_ref[...] = jnp.zeros(acc_ref.shape, F32)

    def kv_step(j, masked):
        ks = pl.multiple_of(j * tk, tk)
        kr = kr_ref[0, pl.ds(ks, tk), :]

        def scores(p):
            kc = jnp.concatenate([kn_ref[0, pl.ds(ks, tk), p * LANES:(p + 1) * LANES], kr], axis=1)
            return _dot(kc, qs_ref[p])

        s_next = scores(0)
        for p in range(npairs):
            s = s_next
            if p + 1 < npairs:
                s_next = scores(p + 1)
            if masked:
                s = jnp.where(causal, s, NEG_INF)
            m_prev = m_ref[p]
            m_new = jnp.maximum(m_prev, jnp.max(s, axis=0, keepdims=True))
            alpha = jnp.exp2(m_prev - m_new)
            pr = jnp.exp2(s - m_new)
            l_ref[p] = alpha * l_ref[p] + jnp.sum(pr, axis=0, keepdims=True)
            pv = _dot(v_ref[0, j, p * LANES:(p + 1) * LANES, :], pr.astype(BF16))
            acc_ref[p] = alpha * acc_ref[p] + pv
            m_ref[p] = m_new

    def body(j, c):
        kv_step(j, False)
        return c

    lax.fori_loop(0, i, body, 0)
    kv_step(i, True)
    for p in range(npairs):
        a = acc_ref[p] * (1.0 / l_ref[p])
        ot = jnp.concatenate([a[:MLA_V, :tq], a[MLA_V:, tq:]], axis=0)
        o_ref[0, :, p * LANES:(p + 1) * LANES] = ot.T


def _mla_layer(x, pre_g, post_g, w_in, q_norm, kv_norm, w_uq, w_ukv, w_out):
    bsz, L, d = x.shape
    H = MLA_HEADS
    dq = MLA_NOPE + MLA_ROPE
    nope = H * MLA_NOPE
    rope = H * MLA_ROPE
    vw = H * MLA_V
    half = MLA_ROPE // 2
    o_kr = MLA_Q_RANK + MLA_KV_RANK
    o_z = o_kr + MLA_ROPE
    lane_d = np.arange(LANES) % MLA_ROPE
    cols_in = np.concatenate([np.arange(o_kr), o_z + np.arange(vw),
                              o_kr + lane_d, o_kr + (lane_d + half) % MLA_ROPE])
    w1 = w_in[:, cols_in].astype(BF16)
    hh = np.arange(H)[:, None]
    cols_n = (hh * dq + np.arange(MLA_NOPE)[None]).reshape(-1)
    cols_r = (hh * dq + MLA_NOPE + np.arange(MLA_ROPE)[None]).reshape(-1)
    cols_rs = (hh * dq + MLA_NOPE + (np.arange(MLA_ROPE)[None] + half) % MLA_ROPE).reshape(-1)
    wqt = w_uq[:, np.concatenate([cols_n, cols_r, cols_rs])].T.astype(BF16)
    dkv = MLA_NOPE + MLA_V
    cols_kn = (hh * dkv + np.arange(MLA_NOPE)[None]).reshape(-1)
    cols_v = (hh * dkv + MLA_NOPE + np.arange(MLA_V)[None]).reshape(-1)
    wkn = w_ukv[:, cols_kn].astype(BF16)
    wvt = w_ukv[:, cols_v].T.astype(BF16)
    inv = ROPE_BASE ** (-jnp.arange(0, MLA_ROPE, 2, dtype=F32) / MLA_ROPE)
    ang = jnp.arange(L, dtype=F32)[:, None] * inv[None, :]
    cos, sin = jnp.cos(ang), jnp.sin(ang)
    cos32 = jnp.concatenate([cos, cos], axis=1)
    sin32 = jnp.concatenate([-sin, sin], axis=1)
    cos_k, sin_k = jnp.tile(cos32, (1, LANES // MLA_ROPE)), jnp.tile(sin32, (1, LANES // MLA_ROPE))
    cos_q, sin_q = jnp.tile(cos32, (1, H)).T, jnp.tile(sin32, (1, H)).T

    tm = 512
    tk = MLA_TK
    tok = lambda w_: pl.BlockSpec((1, tm, w_), lambda b, i: (b, i, 0))
    tokt = lambda w_: pl.BlockSpec((1, w_, tm), lambda b, i: (b, 0, i))
    scale = dq ** -0.5 * math.log2(math.e)
    qn, qr, kn, kr, v, z = pl.pallas_call(
        functools.partial(_mla_pre_kernel, scale=scale),
        out_shape=[jax.ShapeDtypeStruct((bsz, nope, L), BF16),
                   jax.ShapeDtypeStruct((bsz, rope, L), BF16),
                   jax.ShapeDtypeStruct((bsz, L, nope), BF16),
                   jax.ShapeDtypeStruct((bsz, L, LANES), BF16),
                   jax.ShapeDtypeStruct((bsz, L // tk, vw, tk), BF16),
                   jax.ShapeDtypeStruct((bsz, L, vw), F32)],
        grid=(bsz, L // tm),
        in_specs=[tok(d), _full((1, d)), _full(w1.shape), _full((1, MLA_Q_RANK)), _full((1, MLA_KV_RANK)),
                  _full(wqt.shape), _full(wkn.shape), _full(wvt.shape),
                  pl.BlockSpec((rope, tm), lambda b, i: (0, i)), pl.BlockSpec((rope, tm), lambda b, i: (0, i)),
                  pl.BlockSpec((tm, LANES), lambda b, i: (i, 0)), pl.BlockSpec((tm, LANES), lambda b, i: (i, 0))],
        out_specs=[tokt(nope), tokt(rope), tok(nope), tok(LANES),
                   pl.BlockSpec((1, tm // tk, vw, tk), lambda b, i: (b, i, 0, 0)), tok(vw)],
        compiler_params=_cparams(("parallel", "parallel")),
        name="mla_pre",
    )(x, pre_g.reshape(1, d), w1, q_norm.reshape(1, -1), kv_norm.reshape(1, -1), wqt, wkn, wvt,
      cos_q, sin_q, cos_k, sin_k)

    tq = MLA_TQ
    npairs = H // 2
    qspec = lambda w_: pl.BlockSpec((1, w_, tq), lambda b, i: (b, 0, i))
    kspec = lambda w_: pl.BlockSpec((1, L, w_), lambda b, i: (b, 0, 0))
    o = pl.pallas_call(
        _mla_attn_kernel,
        out_shape=jax.ShapeDtypeStruct((bsz, L, vw), F32),
        grid=(bsz, L // tq),
        in_specs=[qspec(nope), qspec(rope), kspec(nope), kspec(LANES),
                  pl.BlockSpec((1, L // tk, vw, tk), lambda b, i: (b, 0, 0, 0))],
        out_specs=pl.BlockSpec((1, tq, vw), lambda b, i: (b, i, 0)),
        scratch_shapes=[pltpu.VMEM((npairs, 2 * LANES, 2 * tq), BF16),
                        pltpu.VMEM((npairs, LANES, 2 * tq), F32),
                        pltpu.VMEM((npairs, 1, 2 * tq), F32),
                        pltpu.VMEM((npairs, 1, 2 * tq), F32)],
        compiler_params=_cparams(("parallel", "arbitrary")),
        name="mla_attn",
    )(qn, qr, kn, kr, v)
    return _post(o, z, x, w_out.astype(BF16), post_g, 512, "mla_post")


def _sgu_kernel(x_ref, g_ref, w_ref, lng_ref, lnb_ref, ws_ref, bs_ref, wo_ref, pg_ref, out_ref, s_ref):
    width = wo_ref.shape[0]
    tm = x_ref.shape[1]
    lane = lax.broadcasted_iota(jnp.int32, (1, LANES), 1)
    lo = lane < HALF
    x = x_ref[0]
    hb = _rms(x, g_ref[...]).astype(BF16)
    v = jax.nn.gelu(_dot(hb, w_ref[:, width:2 * width]))
    mu = jnp.mean(v, axis=-1, keepdims=True)
    vc = v - mu
    var = jnp.mean(vc * vc, axis=-1, keepdims=True)
    vb = (vc * lax.rsqrt(var + EPS) * lng_ref[...] + lnb_ref[...]).astype(BF16)
    for c in range(tm // SGU_CHUNK):
        for jj in range(width // LANES):
            blk = vb[c * SGU_CHUNK:(c + 1) * SGU_CHUNK, jj * LANES:(jj + 1) * LANES]
            r = _dot(ws_ref[jj], blk)
            s_ref[c * SGU_CHUNK:(c + 1) * SGU_CHUNK, jj * LANES:(jj + 1) * LANES] = (
                jnp.where(lo, r[:SGU_CHUNK], r[SGU_CHUNK:]) + bs_ref[jj])
    u = jax.nn.gelu(_dot(hb, w_ref[:, :width]))
    z = _dot(hb, w_ref[:, 2 * width:])
    o = u * s_ref[...] * jax.nn.silu(z)
    r = _dot(o.astype(BF16), wo_ref[...])
    out_ref[0] = x + _rms(r, pg_ref[...])


def _sgu_layer(x, pre_g, post_g, w_in, ln_g, ln_b, w_s, b_s, w_out):
    bsz, L, d = x.shape
    width = w_out.shape[0]
    T = SGU_CHUNK
    gd = width // SGU_GROUPS
    tril = jnp.tril(jnp.ones((T, T), dtype=bool))
    ws = jnp.where(tril[None], w_s, 0.0).reshape(SGU_GROUPS // 2, 2 * T, T).astype(BF16)
    bs = jnp.repeat(b_s.astype(F32).T, gd, axis=1)
    bs = bs.reshape(T, width // LANES, LANES).transpose(1, 0, 2)
    tm = 512
    return pl.pallas_call(
        _sgu_kernel,
        out_shape=jax.ShapeDtypeStruct(x.shape, x.dtype),
        grid=(bsz, L // tm),
        in_specs=[pl.BlockSpec((1, tm, d), lambda b, i: (b, i, 0)),
                  _full((1, d)), _full(w_in.shape), _full((1, width)), _full((1, width)),
                  _full(ws.shape), _full(bs.shape), _full(w_out.shape), _full((1, d))],
        out_specs=pl.BlockSpec((1, tm, d), lambda b, i: (b, i, 0)),
        scratch_shapes=[pltpu.VMEM((tm, width), F32)],
        compiler_params=_cparams(("parallel", "parallel")),
        name="sgu",
    )(x, pre_g.reshape(1, d), w_in.astype(BF16), ln_g.reshape(1, width), ln_b.reshape(1, width),
      ws, bs, w_out.astype(BF16), post_g.reshape(1, d))


def kernel(x, pre_norm, post_norm, rel_bias, a_w_in, a_lam_re, a_lam_im, a_log_dt, a_b_re, a_b_im, a_c_re, a_c_im, a_d, a_w_glu, a_b_glu, a_w_out, b_w_in, b_sinks, b_w_out, c_w_in, c_q_norm, c_kv_norm, c_w_uq, c_w_ukv, c_w_out, d_w_in, d_ln_g, d_ln_b, d_w_s, d_b_s, d_w_out):
    depth = pre_norm.shape[0]
    for i in range(depth):
        kind, j = i % 4, i // 4
        if kind == 0:
            x = _s5_layer(x, pre_norm[i], post_norm[i], a_w_in[j], a_lam_re[j], a_lam_im[j], a_log_dt[j],
                          a_b_re[j], a_b_im[j], a_c_re[j], a_c_im[j], a_d[j], a_w_glu[j], a_b_glu[j],
                          a_w_out[j])
        elif kind == 1:
            x = _swa_layer(x, pre_norm[i], post_norm[i], b_w_in[j], b_sinks[j], b_w_out[j], rel_bias)
        elif kind == 2:
            x = _mla_layer(x, pre_norm[i], post_norm[i], c_w_in[j], c_q_norm[j], c_kv_norm[j], c_w_uq[j],
                           c_w_ukv[j], c_w_out[j])
        else:
            x = _sgu_layer(x, pre_norm[i], post_norm[i], d_w_in[j], d_ln_g[j], d_ln_b[j], d_w_s[j],
                           d_b_s[j], d_w_out[j])
    return x
```

```python
import functools
import math

import jax
import jax.numpy as jnp
import numpy as np
from jax import lax
from jax.experimental import pallas as pl
from jax.experimental.pallas import tpu as pltpu

F32 = jnp.float32
BF16 = jnp.bfloat16

D_MODEL = 1024
EPS = 1e-6
NEG_INF = -1e30
LANES = 128
HALF = LANES // 2

SSM_GROUP = 16
SSM_STATE = 64
S5_CH_BLOCK = LANES
S5_GROUPS_PER_BLOCK = S5_CH_BLOCK // SSM_GROUP
S5_STATE_BLOCK = S5_GROUPS_PER_BLOCK * SSM_STATE
S5_T = 64

HEAD_DIM = 64
SWA_HEADS = 16
SWA_KV_HEADS = 2
SWA_GROUP = SWA_HEADS // SWA_KV_HEADS
WINDOW = 128
REL_BUCKETS = 32
REL_MAX_DIST = 128

MLA_HEADS = 16
MLA_NOPE = 64
MLA_ROPE = 32
MLA_V = 64
MLA_KV_RANK = 256
MLA_Q_RANK = 768
ROPE_BASE = 10000.0
MLA_TQ = 256
MLA_TK = 256

SGU_CHUNK = 128
SGU_GROUPS = 16

VMEM_LIMIT = 56 * 1024 * 1024


def _cparams(sem):
    return pltpu.CompilerParams(dimension_semantics=sem, vmem_limit_bytes=VMEM_LIMIT)


def _rms(x, g):
    return x * lax.rsqrt(jnp.mean(x * x, axis=-1, keepdims=True) + EPS) * g


def _dot(a, b):
    return jnp.dot(a, b, preferred_element_type=F32)


def _dot_nt(a, b):
    return lax.dot_general(a, b, (((1,), (1,)), ((), ())), preferred_element_type=F32)


def _full(shape):
    n = len(shape)
    return pl.BlockSpec(shape, lambda *_: (0,) * n)


def _pre_kernel(x_ref, g_ref, w_ref, *out_refs, splits, scales):
    hb = _rms(x_ref[0], g_ref[...]).astype(BF16)
    off = 0
    for o_ref, width, scale in zip(out_refs, splits, scales):
        r = _dot(hb, w_ref[:, off:off + width])
        if scale != 1.0:
            r = r * scale
        o_ref[0] = r.astype(o_ref.dtype)
        off += width


def _pre(x, g, w, splits, dtypes, scales, tm, name):
    bsz, L, d = x.shape
    kern = functools.partial(_pre_kernel, splits=tuple(splits), scales=tuple(scales))
    return pl.pallas_call(
        kern,
        out_shape=[jax.ShapeDtypeStruct((bsz, L, s), dt) for s, dt in zip(splits, dtypes)],
        grid=(bsz, L // tm),
        in_specs=[pl.BlockSpec((1, tm, d), lambda b, i: (b, i, 0)),
                  _full((1, d)), _full(w.shape)],
        out_specs=[pl.BlockSpec((1, tm, s), lambda b, i: (b, i, 0)) for s in splits],
        compiler_params=_cparams(("parallel", "parallel")),
        name=name,
    )(x, g.reshape(1, d), w)


def _post_kernel(o_ref, z_ref, x_ref, w_ref, g_ref, out_ref):
    o = o_ref[0].astype(F32) * jax.nn.silu(z_ref[0])
    r = _dot(o.astype(BF16), w_ref[...])
    out_ref[0] = x_ref[0] + _rms(r, g_ref[...])


def _post(o, z, x, w, g, tm, name):
    bsz, L, d = x.shape
    width = o.shape[-1]
    return pl.pallas_call(
        _post_kernel,
        out_shape=jax.ShapeDtypeStruct(x.shape, x.dtype),
        grid=(bsz, L // tm),
        in_specs=[pl.BlockSpec((1, tm, width), lambda b, i: (b, i, 0)),
                  pl.BlockSpec((1, tm, width), lambda b, i: (b, i, 0)),
                  pl.BlockSpec((1, tm, d), lambda b, i: (b, i, 0)),
                  _full(w.shape), _full((1, d))],
        out_specs=pl.BlockSpec((1, tm, d), lambda b, i: (b, i, 0)),
        compiler_params=_cparams(("parallel", "parallel")),
        name=name,
    )(o, z, x, w, g.reshape(1, d))


def _s5_pre_kernel(x_ref, g_ref, w_ref, u_ref, z_ref, *, tt):
    bsz = x_ref.shape[0]
    width = u_ref.shape[1] // bsz
    x = x_ref[...].reshape(bsz * tt, x_ref.shape[2])
    hb = _rms(x, g_ref[...]).astype(BF16)
    u = _dot(hb, w_ref[:, :width])
    z = _dot(hb, w_ref[:, width:])
    for b in range(bsz):
        u_ref[:, b * width:(b + 1) * width] = u[b * tt:(b + 1) * tt]
        z_ref[:, b * width:(b + 1) * width] = z[b * tt:(b + 1) * tt]


def _s5_scan_kernel(u_ref, bb_ref, cc_ref, ar_ref, ai_ref, d_ref, y_ref, s_ref, carry_ref, *, tsteps):
    nblk = bb_ref.shape[0]
    sb = S5_STATE_BLOCK
    rows = carry_ref.shape[1]

    @pl.when(pl.program_id(0) == 0)
    def _():
        carry_ref[...] = jnp.zeros_like(carry_ref)

    for i in range(nblk):
        ub = u_ref[:, i * LANES:(i + 1) * LANES]
        s_ref[...] = _dot(ub.astype(BF16), bb_ref[i])
        ar = ar_ref[i]
        ai = ai_ref[i]

        def step(t, st):
            sr, si = st
            r0 = pl.multiple_of(t * rows, rows)
            xr = s_ref[pl.ds(r0, rows), 0:sb]
            xi = s_ref[pl.ds(r0, rows), sb:2 * sb]
            nr = ar * sr - ai * si + xr
            ni = ar * si + ai * sr + xi
            s_ref[pl.ds(r0, rows), 0:sb] = nr
            s_ref[pl.ds(r0, rows), sb:2 * sb] = ni
            return nr, ni

        sr, si = lax.fori_loop(0, tsteps, step,
                               (carry_ref[i, :, 0:sb], carry_ref[i, :, sb:2 * sb]), unroll=8)
        carry_ref[i, :, 0:sb] = sr
        carry_ref[i, :, sb:2 * sb] = si
        y = _dot(s_ref[...].astype(BF16), cc_ref[i]) + d_ref[:, i * LANES:(i + 1) * LANES] * ub
        y_ref[:, i * LANES:(i + 1) * LANES] = jax.nn.gelu(y)


def _s5_post_kernel(y_ref, z_ref, x_ref, wg_ref, bg_ref, wo_ref, g_ref, out_ref, *, tt):
    bsz = x_ref.shape[0]
    width = y_ref.shape[1] // bsz
    y = jnp.concatenate([y_ref[:, b * width:(b + 1) * width] for b in range(bsz)], axis=0)
    z = jnp.concatenate([z_ref[:, b * width:(b + 1) * width] for b in range(bsz)], axis=0)
    gate = jax.nn.sigmoid(_dot(y.astype(BF16), wg_ref[...]) + bg_ref[...])
    o = y * gate * jax.nn.silu(z)
    r = _dot(o.astype(BF16), wo_ref[...])
    x = x_ref[...].reshape(bsz * tt, x_ref.shape[2])
    out_ref[...] = (x + _rms(r, g_ref[...])).reshape(out_ref.shape)


def _s5_discretize(lam_re, lam_im, log_dt, b_re, b_im):
    dt = jnp.exp(log_dt)[:, None]
    mag = jnp.exp(lam_re * dt)
    ab_re = mag * jnp.cos(lam_im * dt)
    ab_im = mag * jnp.sin(lam_im * dt)
    den = lam_re * lam_re + lam_im * lam_im
    nr = ab_re - 1.0
    f_re = (nr * lam_re + ab_im * lam_im) / den
    f_im = (ab_im * lam_re - nr * lam_im) / den
    bb_re = f_re[..., None] * b_re - f_im[..., None] * b_im
    bb_im = f_re[..., None] * b_im + f_im[..., None] * b_re
    return ab_re, ab_im, bb_re, bb_im


def _s5_layer(x, pre_g, post_g, w_in, lam_re, lam_im, log_dt, b_re, b_im, c_re, c_im, d_skip,
              w_glu, b_glu, w_out):
    bsz, L, d = x.shape
    width = w_in.shape[1] // 2
    nblk = width // S5_CH_BLOCK
    gpb = S5_GROUPS_PER_BLOCK
    tt = S5_T
    rows = bsz * tt

    u, z = pl.pallas_call(
        functools.partial(_s5_pre_kernel, tt=tt),
        out_shape=[jax.ShapeDtypeStruct((L, bsz * width), F32)] * 2,
        grid=(L // tt,),
        in_specs=[pl.BlockSpec((bsz, tt, d), lambda i: (0, i, 0)),
                  _full((1, d)), _full(w_in.shape)],
        out_specs=[pl.BlockSpec((tt, bsz * width), lambda i: (i, 0))] * 2,
        compiler_params=_cparams(("parallel",)),
        name="s5_pre",
    )(x, pre_g.reshape(1, d), w_in.astype(BF16))

    ab_re, ab_im, bb_re, bb_im = _s5_discretize(lam_re, lam_im, log_dt, b_re, b_im)
    eye = jnp.eye(gpb, dtype=F32)

    def pack_b(bb):
        t = bb.reshape(nblk, gpb, SSM_STATE, SSM_GROUP)
        return jnp.einsum('igph,gk->ikhgp', t, eye).reshape(nblk, S5_CH_BLOCK, S5_STATE_BLOCK)

    def pack_c(cc):
        t = cc.reshape(nblk, gpb, SSM_GROUP, SSM_STATE)
        return jnp.einsum('ighp,gk->igpkh', t, eye).reshape(nblk, S5_STATE_BLOCK, S5_CH_BLOCK)

    bb = jnp.concatenate([pack_b(bb_re), pack_b(bb_im)], axis=2).astype(BF16)
    cc = jnp.concatenate([pack_c(c_re), -pack_c(c_im)], axis=1).astype(BF16)
    ar = jnp.broadcast_to(ab_re.reshape(nblk, 1, S5_STATE_BLOCK), (nblk, bsz, S5_STATE_BLOCK))
    ai = jnp.broadcast_to(ab_im.reshape(nblk, 1, S5_STATE_BLOCK), (nblk, bsz, S5_STATE_BLOCK))

    y = pl.pallas_call(
        functools.partial(_s5_scan_kernel, tsteps=tt),
        out_shape=jax.ShapeDtypeStruct((L * bsz, width), F32),
        grid=(L // tt,),
        in_specs=[pl.BlockSpec((rows, width), lambda i: (i, 0)),
                  _full(bb.shape), _full(cc.shape), _full(ar.shape), _full(ai.shape),
                  _full((1, width))],
        out_specs=pl.BlockSpec((rows, width), lambda i: (i, 0)),
        scratch_shapes=[pltpu.VMEM((rows, 2 * S5_STATE_BLOCK), F32),
                        pltpu.VMEM((nblk, bsz, 2 * S5_STATE_BLOCK), F32)],
        compiler_params=_cparams(("arbitrary",)),
        name="s5_scan",
    )(u.reshape(L * bsz, width), bb, cc, ar, ai, d_skip.reshape(1, width))

    return pl.pallas_call(
        functools.partial(_s5_post_kernel, tt=tt),
        out_shape=jax.ShapeDtypeStruct(x.shape, x.dtype),
        grid=(L // tt,),
        in_specs=[pl.BlockSpec((tt, bsz * width), lambda i: (i, 0)),
                  pl.BlockSpec((tt, bsz * width), lambda i: (i, 0)),
                  pl.BlockSpec((bsz, tt, d), lambda i: (0, i, 0)),
                  _full(w_glu.shape), _full((1, width)), _full(w_out.shape), _full((1, d))],
        out_specs=pl.BlockSpec((bsz, tt, d), lambda i: (0, i, 0)),
        compiler_params=_cparams(("parallel",)),
        name="s5_post",
    )(y.reshape(L, bsz * width), z, x, w_glu.astype(BF16), b_glu.reshape(1, width),
      w_out.astype(BF16), post_g.reshape(1, d))


def _t5_bucket_table():
    qi = np.arange(WINDOW)[:, None]
    kj = np.arange(2 * WINDOW)[None, :]
    dist = qi + WINDOW - kj
    valid = (dist >= 0) & (dist < WINDOW)
    dpos = jnp.maximum(jnp.asarray(dist), 0)
    max_exact = REL_BUCKETS // 2
    dist_f = jnp.maximum(dpos, 1).astype(F32)
    large = max_exact + (jnp.log(dist_f / max_exact) / math.log(REL_MAX_DIST / max_exact)
                         * (REL_BUCKETS - max_exact)).astype(jnp.int32)
    large = jnp.minimum(large, REL_BUCKETS - 1)
    return jnp.where(dpos < max_exact, dpos, large), jnp.asarray(valid)


def _swa_pre_kernel(x_ref, g_ref, wqt_ref, wk_ref, wvt_ref, wz_ref, qt_ref, k_ref, vt_ref, z_ref, *, scale):
    hb = _rms(x_ref[0], g_ref[...]).astype(BF16)
    qt_ref[0] = (_dot_nt(wqt_ref[...], hb) * scale).astype(BF16)
    k_ref[0] = _dot(hb, wk_ref[...]).astype(BF16)
    vt_ref[0] = _dot_nt(wvt_ref[...], hb).astype(BF16)
    z_ref[0] = _dot(hb, wz_ref[...])


def _swa_kernel(qt_ref, kp_ref, kc_ref, vtp_ref, vtc_ref, bias_ref, sink_ref, o_ref, ot_ref):
    W = WINDOW
    n = pl.program_id(1)
    first = (n == 0).astype(jnp.int32)
    k = jnp.concatenate([kp_ref[0], kc_ref[0]], axis=0)
    vt = jnp.concatenate([vtp_ref[0], vtc_ref[0]], axis=1)
    zq = jnp.zeros((HEAD_DIM, SWA_GROUP * W), BF16)

    def scores(h):
        qh = jnp.concatenate([qt_ref[0, (h * SWA_GROUP + g) * HEAD_DIM:(h * SWA_GROUP + g + 1) * HEAD_DIM, :]
                              for g in range(SWA_GROUP)], axis=1)
        qz = jnp.concatenate([qh, zq] if h == 0 else [zq, qh], axis=0)
        return _dot(k, qz)

    raw = [scores(h) for h in range(SWA_KV_HEADS)]
    for h in range(SWA_KV_HEADS):
        heads = [h * SWA_GROUP + g for g in range(SWA_GROUP)]
        s = raw[h] + bias_ref[first, h]
        sink = sink_ref[h]
        m = jnp.maximum(jnp.max(s, axis=0, keepdims=True), sink)
        p = jnp.exp2(s - m)
        denom = jnp.sum(p, axis=0, keepdims=True) + jnp.exp2(sink - m)
        o = _dot(vt, p.astype(BF16))
        oh = o[h * HEAD_DIM:(h + 1) * HEAD_DIM] * (1.0 / denom)
        for g, hd in enumerate(heads):
            ot_ref[hd * HEAD_DIM:(hd + 1) * HEAD_DIM, :] = oh[:, g * W:(g + 1) * W]
    o_ref[0] = ot_ref[...].T


def _swa_layer(x, pre_g, post_g, w_in, sinks, w_out, rel_bias):
    bsz, L, d = x.shape
    width = SWA_HEADS * HEAD_DIM
    kvw = SWA_KV_HEADS * HEAD_DIM
    W = WINDOW
    nb = L // W
    log2e = math.log2(math.e)
    tm = 512
    tok = lambda w_: pl.BlockSpec((1, tm, w_), lambda b, i: (b, i, 0))
    tokt = lambda w_: pl.BlockSpec((1, w_, tm), lambda b, i: (b, 0, i))
    wqt = w_in[:, :width].T.astype(BF16)
    wk = w_in[:, width:width + kvw].astype(BF16)
    wvt = w_in[:, width + kvw:width + 2 * kvw].T.astype(BF16)
    wz = w_in[:, width + 2 * kvw:].astype(BF16)
    qt, k, vt, z = pl.pallas_call(
        functools.partial(_swa_pre_kernel, scale=HEAD_DIM ** -0.5 * log2e),
        out_shape=[jax.ShapeDtypeStruct((bsz, width, L), BF16),
                   jax.ShapeDtypeStruct((bsz, L, kvw), BF16),
                   jax.ShapeDtypeStruct((bsz, kvw, L), BF16),
                   jax.ShapeDtypeStruct((bsz, L, width), F32)],
        grid=(bsz, L // tm),
        in_specs=[tok(d), _full((1, d)), _full(wqt.shape), _full(wk.shape), _full(wvt.shape), _full(wz.shape)],
        out_specs=[tokt(width), tok(kvw), tokt(kvw), tok(width)],
        compiler_params=_cparams(("parallel", "parallel")),
        name="swa_pre",
    )(x, pre_g.reshape(1, d), wqt, wk, wvt, wz)

    bucket, valid = _t5_bucket_table()
    bias = jnp.transpose(rel_bias[bucket], (2, 1, 0)).astype(F32) * log2e
    valid_t = valid.T[None]
    has_prev = (jnp.arange(2 * W) >= W)[None, :, None]
    variants = [jnp.where(valid_t, bias, NEG_INF),
                jnp.where(jnp.logical_and(valid_t, has_prev), bias, NEG_INF)]
    bias_t = jnp.stack([v.reshape(SWA_KV_HEADS, SWA_GROUP, 2 * W, W).transpose(0, 2, 1, 3)
                        .reshape(SWA_KV_HEADS, 2 * W, SWA_GROUP * W) for v in variants])
    sink = jnp.repeat(sinks.astype(F32) * log2e, W).reshape(SWA_KV_HEADS, 1, SWA_GROUP * W)

    o = pl.pallas_call(
        _swa_kernel,
        out_shape=jax.ShapeDtypeStruct((bsz, L, width), F32),
        grid=(bsz, nb),
        in_specs=[pl.BlockSpec((1, width, W), lambda b, n: (b, 0, n)),
                  pl.BlockSpec((1, W, kvw), lambda b, n: (b, jnp.maximum(n - 1, 0), 0)),
                  pl.BlockSpec((1, W, kvw), lambda b, n: (b, n, 0)),
                  pl.BlockSpec((1, kvw, W), lambda b, n: (b, 0, jnp.maximum(n - 1, 0))),
                  pl.BlockSpec((1, kvw, W), lambda b, n: (b, 0, n)),
                  _full(bias_t.shape), _full(sink.shape)],
        out_specs=pl.BlockSpec((1, W, width), lambda b, n: (b, n, 0)),
        scratch_shapes=[pltpu.VMEM((width, W), F32)],
        compiler_params=_cparams(("parallel", "parallel")),
        name="swa_attn",
    )(qt, k, k, vt, vt, bias_t, sink)
    return _post(o, z, x, w_out.astype(BF16), post_g, 512, "swa_post")


def _mla_pre_kernel(x_ref, g_ref, w_ref, qn_ref, kvn_ref, wq_ref, wkv_ref, wvt_ref, cq_ref, sq_ref, ck_ref, sk_ref,
                    oqn_ref, oqr_ref, okn_ref, okr_ref, ov_ref, oz_ref, *, scale):
    nope = MLA_HEADS * MLA_NOPE
    rope = MLA_HEADS * MLA_ROPE
    vw = MLA_HEADS * MLA_V
    hb = _rms(x_ref[0], g_ref[...]).astype(BF16)
    o1 = MLA_Q_RANK
    o2 = o1 + MLA_KV_RANK
    o3 = o2 + vw
    cq = _dot(hb, w_ref[:, :o1])
    ckv = _dot(hb, w_ref[:, o1:o2])
    oz_ref[0] = _dot(hb, w_ref[:, o2:o3])
    kr = _dot(hb, w_ref[:, o3:o3 + LANES])
    krs = _dot(hb, w_ref[:, o3 + LANES:o3 + 2 * LANES])
    okr_ref[0] = (kr * ck_ref[...] + krs * sk_ref[...]).astype(BF16)
    cqb = _rms(cq, qn_ref[...]).astype(BF16)
    oqn_ref[0] = (_dot_nt(wq_ref[:nope], cqb) * scale).astype(BF16)
    qr = _dot_nt(wq_ref[nope:nope + rope], cqb)
    qrs = _dot_nt(wq_ref[nope + rope:nope + 2 * rope], cqb)
    oqr_ref[0] = ((qr * cq_ref[...] + qrs * sq_ref[...]) * scale).astype(BF16)
    ckb = _rms(ckv, kvn_ref[...]).astype(BF16)
    okn_ref[0] = _dot(ckb, wkv_ref[:, :nope]).astype(BF16)
    vt = _dot_nt(wvt_ref[...], ckb).astype(BF16)
    tk = ov_ref.shape[3]
    for c in range(ov_ref.shape[1]):
        ov_ref[0, c] = vt[:, c * tk:(c + 1) * tk]


def _mla_attn_kernel(qn_ref, qr_ref, kn_ref, kr_ref, v_ref, o_ref, qs_ref, acc_ref, m_ref, l_ref):
    tq = qn_ref.shape[2]
    tk = v_ref.shape[3]
    npairs = MLA_HEADS // 2
    i = pl.program_id(1)
    krow = lax.broadcasted_iota(jnp.int32, (tk, 2 * tq), 0)
    qcol = lax.broadcasted_iota(jnp.int32, (tk, 2 * tq), 1)
    causal = krow <= jnp.where(qcol >= tq, qcol - tq, qcol)

    zn = jnp.zeros((MLA_NOPE, tq), BF16)
    zr = jnp.zeros((LANES - MLA_ROPE, tq), BF16)
    for p in range(npairs):
        qn = qn_ref[0, p * LANES:(p + 1) * LANES, :]
        r0 = 2 * p * MLA_ROPE
        c0 = jnp.concatenate([qn[:MLA_NOPE], zn, qr_ref[0, r0:r0 + MLA_ROPE, :], zr], axis=0)
        c1 = jnp.concatenate([zn, qn[MLA_NOPE:], qr_ref[0, r0 + MLA_ROPE:r0 + 2 * MLA_ROPE, :], zr], axis=0)
        qs_ref[p] = jnp.concatenate([c0, c1], axis=1)

    m_ref[...] = jnp.full(m_ref.shape, NEG_INF, F32)
    l_ref[...] = jnp.zeros(l_ref.shape, F32)
    acc_ref[...] = jnp.zeros(acc_ref.shape, F32)

    def kv_step(j, masked):
        ks = pl.multiple_of(j * tk, tk)
        kr = kr_ref[0, pl.ds(ks, tk), :]

        def scores(p):
            kc = jnp.concatenate([kn_ref[0, pl.ds(ks, tk), p * LANES:(p + 1) * LANES], kr], axis=1)
            return _dot(kc, qs_ref[p])

        s_next = scores(0)
        for p in range(npairs):
            s = s_next
            if p + 1 < npairs:
                s_next = scores(p + 1)
            if masked:
                s = jnp.where(causal, s, NEG_INF)
            m_prev = m_ref[p]
            m_new = jnp.maximum(m_prev, jnp.max(s, axis=0, keepdims=True))
            alpha = jnp.exp2(m_prev - m_new)
            pr = jnp.exp2(s - m_new)
            l_ref[p] = alpha * l_ref[p] + jnp.sum(pr, axis=0, keepdims=True)
            pv = _dot(v_ref[0, j, p * LANES:(p + 1) * LANES, :], pr.astype(BF16))
            acc_ref[p] = alpha * acc_ref[p] + pv
            m_ref[p] = m_new

    def body(j, c):
        kv_step(j, False)
        return c

    lax.fori_loop(0, i, body, 0)
    kv_step(i, True)
    for p in range(npairs):
        a = acc_ref[p] * (1.0 / l_ref[p])
        ot = jnp.concatenate([a[:MLA_V, :tq], a[MLA_V:, tq:]], axis=0)
        o_ref[0, :, p * LANES:(p + 1) * LANES] = ot.T


def _mla_layer(x, pre_g, post_g, w_in, q_norm, kv_norm, w_uq, w_ukv, w_out):
    bsz, L, d = x.shape
    H = MLA_HEADS
    dq = MLA_NOPE + MLA_ROPE
    nope = H * MLA_NOPE
    rope = H * MLA_ROPE
    vw = H * MLA_V
    half = MLA_ROPE // 2
    o_kr = MLA_Q_RANK + MLA_KV_RANK
    o_z = o_kr + MLA_ROPE
    w_kr = w_in[:, o_kr:o_z]
    w_krs = jnp.concatenate([w_kr[:, half:], w_kr[:, :half]], axis=1)
    reps = LANES // MLA_ROPE
    w1 = jnp.concatenate([w_in[:, :o_kr], w_in[:, o_z:]] + [w_kr] * reps + [w_krs] * reps, axis=1).astype(BF16)
    wq3 = w_uq.reshape(MLA_Q_RANK, H, dq)
    wq_r = wq3[:, :, MLA_NOPE:]
    wq_rs = jnp.concatenate([wq_r[:, :, half:], wq_r[:, :, :half]], axis=2)
    wqt = jnp.concatenate([wq3[:, :, :MLA_NOPE].reshape(MLA_Q_RANK, nope), wq_r.reshape(MLA_Q_RANK, rope),
                           wq_rs.reshape(MLA_Q_RANK, rope)], axis=1).T.astype(BF16)
    wkv3 = w_ukv.reshape(MLA_KV_RANK, H, MLA_NOPE + MLA_V)
    wkn = wkv3[:, :, :MLA_NOPE].reshape(MLA_KV_RANK, nope).astype(BF16)
    wvt = wkv3[:, :, MLA_NOPE:].reshape(MLA_KV_RANK, vw).T.astype(BF16)
    inv = ROPE_BASE ** (-jnp.arange(0, MLA_ROPE, 2, dtype=F32) / MLA_ROPE)
    ang = jnp.arange(L, dtype=F32)[:, None] * inv[None, :]
    cos, sin = jnp.cos(ang), jnp.sin(ang)
    cos32 = jnp.concatenate([cos, cos], axis=1)
    sin32 = jnp.concatenate([-sin, sin], axis=1)
    cos_k, sin_k = jnp.tile(cos32, (1, LANES // MLA_ROPE)), jnp.tile(sin32, (1, LANES // MLA_ROPE))
    cos_q, sin_q = jnp.tile(cos32, (1, H)).T, jnp.tile(sin32, (1, H)).T

    tm = 512
    tk = MLA_TK
    tok = lambda w_: pl.BlockSpec((1, tm, w_), lambda b, i: (b, i, 0))
    tokt = lambda w_: pl.BlockSpec((1, w_, tm), lambda b, i: (b, 0, i))
    scale = dq ** -0.5 * math.log2(math.e)
    qn, qr, kn, kr, v, z = pl.pallas_call(
        functools.partial(_mla_pre_kernel, scale=scale),
        out_shape=[jax.ShapeDtypeStruct((bsz, nope, L), BF16),
                   jax.ShapeDtypeStruct((bsz, rope, L), BF16),
                   jax.ShapeDtypeStruct((bsz, L, nope), BF16),
                   jax.ShapeDtypeStruct((bsz, L, LANES), BF16),
                   jax.ShapeDtypeStruct((bsz, L // tk, vw, tk), BF16),
                   jax.ShapeDtypeStruct((bsz, L, vw), F32)],
        grid=(bsz, L // tm),
        in_specs=[tok(d), _full((1, d)), _full(w1.shape), _full((1, MLA_Q_RANK)), _full((1, MLA_KV_RANK)),
                  _full(wqt.shape), _full(wkn.shape), _full(wvt.shape),
                  pl.BlockSpec((rope, tm), lambda b, i: (0, i)), pl.BlockSpec((rope, tm), lambda b, i: (0, i)),
                  pl.BlockSpec((tm, LANES), lambda b, i: (i, 0)), pl.BlockSpec((tm, LANES), lambda b, i: (i, 0))],
        out_specs=[tokt(nope), tokt(rope), tok(nope), tok(LANES),
                   pl.BlockSpec((1, tm // tk, vw, tk), lambda b, i: (b, i, 0, 0)), tok(vw)],
        compiler_params=_cparams(("parallel", "parallel")),
        name="mla_pre",
    )(x, pre_g.reshape(1, d), w1, q_norm.reshape(1, -1), kv_norm.reshape(1, -1), wqt, wkn, wvt,
      cos_q, sin_q, cos_k, sin_k)

    tq = MLA_TQ
    npairs = H // 2
    qspec = lambda w_: pl.BlockSpec((1, w_, tq), lambda b, i: (b, 0, i))
    kspec = lambda w_: pl.BlockSpec((1, L, w_), lambda b, i: (b, 0, 0))
    o = pl.pallas_call(
        _mla_attn_kernel,
        out_shape=jax.ShapeDtypeStruct((bsz, L, vw), F32),
        grid=(bsz, L // tq),
        in_specs=[qspec(nope), qspec(rope), kspec(nope), kspec(LANES),
                  pl.BlockSpec((1, L // tk, vw, tk), lambda b, i: (b, 0, 0, 0))],
        out_specs=pl.BlockSpec((1, tq, vw), lambda b, i: (b, i, 0)),
        scratch_shapes=[pltpu.VMEM((npairs, 2 * LANES, 2 * tq), BF16),
                        pltpu.VMEM((npairs, LANES, 2 * tq), F32),
                        pltpu.VMEM((npairs, 1, 2 * tq), F32),
                        pltpu.VMEM((npairs, 1, 2 * tq), F32)],
        compiler_params=_cparams(("parallel", "arbitrary")),
        name="mla_attn",
    )(qn, qr, kn, kr, v)
    return _post(o, z, x, w_out.astype(BF16), post_g, 512, "mla_post")


def _sgu_kernel(x_ref, g_ref, w_ref, lng_ref, lnb_ref, ws_ref, bs_ref, wo_ref, pg_ref, out_ref, s_ref):
    width = wo_ref.shape[0]
    tm = x_ref.shape[1]
    lane = lax.broadcasted_iota(jnp.int32, (1, LANES), 1)
    lo = lane < HALF
    x = x_ref[0]
    hb = _rms(x, g_ref[...]).astype(BF16)
    v = jax.nn.gelu(_dot(hb, w_ref[:, width:2 * width]))
    mu = jnp.mean(v, axis=-1, keepdims=True)
    vc = v - mu
    var = jnp.mean(vc * vc, axis=-1, keepdims=True)
    vb = (vc * lax.rsqrt(var + EPS) * lng_ref[...] + lnb_ref[...]).astype(BF16)
    for c in range(tm // SGU_CHUNK):
        for jj in range(width // LANES):
            blk = vb[c * SGU_CHUNK:(c + 1) * SGU_CHUNK, jj * LANES:(jj + 1) * LANES]
            r = _dot(ws_ref[jj], blk)
            s_ref[c * SGU_CHUNK:(c + 1) * SGU_CHUNK, jj * LANES:(jj + 1) * LANES] = (
                jnp.where(lo, r[:SGU_CHUNK], r[SGU_CHUNK:]) + bs_ref[jj])
    u = jax.nn.gelu(_dot(hb, w_ref[:, :width]))
    z = _dot(hb, w_ref[:, 2 * width:])
    o = u * s_ref[...] * jax.nn.silu(z)
    r = _dot(o.astype(BF16), wo_ref[...])
    out_ref[0] = x + _rms(r, pg_ref[...])


def _sgu_layer(x, pre_g, post_g, w_in, ln_g, ln_b, w_s, b_s, w_out):
    bsz, L, d = x.shape
    width = w_out.shape[0]
    T = SGU_CHUNK
    gd = width // SGU_GROUPS
    tril = jnp.tril(jnp.ones((T, T), dtype=bool))
    ws = jnp.where(tril[None], w_s, 0.0).reshape(SGU_GROUPS // 2, 2 * T, T).astype(BF16)
    bs = jnp.repeat(b_s.astype(F32).T, gd, axis=1)
    bs = bs.reshape(T, width // LANES, LANES).transpose(1, 0, 2)
    tm = 512
    return pl.pallas_call(
        _sgu_kernel,
        out_shape=jax.ShapeDtypeStruct(x.shape, x.dtype),
        grid=(bsz, L // tm),
        in_specs=[pl.BlockSpec((1, tm, d), lambda b, i: (b, i, 0)),
                  _full((1, d)), _full(w_in.shape), _full((1, width)), _full((1, width)),
                  _full(ws.shape), _full(bs.shape), _full(w_out.shape), _full((1, d))],
        out_specs=pl.BlockSpec((1, tm, d), lambda b, i: (b, i, 0)),
        scratch_shapes=[pltpu.VMEM((tm, width), F32)],
        compiler_params=_cparams(("parallel", "parallel")),
        name="sgu",
    )(x, pre_g.reshape(1, d), w_in.astype(BF16), ln_g.reshape(1, width), ln_b.reshape(1, width),
      ws, bs, w_out.astype(BF16), post_g.reshape(1, d))


def kernel(x, pre_norm, post_norm, rel_bias, a_w_in, a_lam_re, a_lam_im, a_log_dt, a_b_re, a_b_im, a_c_re, a_c_im, a_d, a_w_glu, a_b_glu, a_w_out, b_w_in, b_sinks, b_w_out, c_w_in, c_q_norm, c_kv_norm, c_w_uq, c_w_ukv, c_w_out, d_w_in, d_ln_g, d_ln_b, d_w_s, d_b_s, d_w_out):
    depth = pre_norm.shape[0]
    for i in range(depth):
        kind, j = i % 4, i // 4
        if kind == 0:
            x = _s5_layer(x, pre_norm[i], post_norm[i], a_w_in[j], a_lam_re[j], a_lam_im[j], a_log_dt[j],
                          a_b_re[j], a_b_im[j], a_c_re[j], a_c_im[j], a_d[j], a_w_glu[j], a_b_glu[j],
                          a_w_out[j])
        elif kind == 1:
            x = _swa_layer(x, pre_norm[i], post_norm[i], b_w_in[j], b_sinks[j], b_w_out[j], rel_bias)
        elif kind == 2:
            x = _mla_layer(x, pre_norm[i], post_norm[i], c_w_in[j], c_q_norm[j], c_kv_norm[j], c_w_uq[j],
                           c_w_ukv[j], c_w_out[j])
        else:
            x = _sgu_layer(x, pre_norm[i], post_norm[i], d_w_in[j], d_ln_g[j], d_ln_b[j], d_w_s[j],
                           d_b_s[j], d_w_out[j])
    return x
```

```python
import functools
import math

import jax
import jax.numpy as jnp
import numpy as np
from jax import lax
from jax.experimental import pallas as pl
from jax.experimental.pallas import tpu as pltpu

F32 = jnp.float32
BF16 = jnp.bfloat16

D_MODEL = 1024
EPS = 1e-6
NEG_INF = -1e30
LANES = 128
HALF = LANES // 2

SSM_GROUP = 16
SSM_STATE = 64
S5_CH_BLOCK = LANES
S5_GROUPS_PER_BLOCK = S5_CH_BLOCK // SSM_GROUP
S5_STATE_BLOCK = S5_GROUPS_PER_BLOCK * SSM_STATE
S5_T = 64

HEAD_DIM = 64
SWA_HEADS = 16
SWA_KV_HEADS = 2
SWA_GROUP = SWA_HEADS // SWA_KV_HEADS
WINDOW = 128
REL_BUCKETS = 32
REL_MAX_DIST = 128

MLA_HEADS = 16
MLA_NOPE = 64
MLA_ROPE = 32
MLA_V = 64
MLA_KV_RANK = 256
MLA_Q_RANK = 768
ROPE_BASE = 10000.0
MLA_TQ = 256
MLA_TK = 256

SGU_CHUNK = 128
SGU_GROUPS = 16

VMEM_LIMIT = 56 * 1024 * 1024


def _cparams(sem):
    return pltpu.CompilerParams(dimension_semantics=sem, vmem_limit_bytes=VMEM_LIMIT)


def _rms(x, g):
    return x * lax.rsqrt(jnp.mean(x * x, axis=-1, keepdims=True) + EPS) * g


def _dot(a, b):
    return jnp.dot(a, b, preferred_element_type=F32)


def _dot_nt(a, b):
    return lax.dot_general(a, b, (((1,), (1,)), ((), ())), preferred_element_type=F32)


def _full(shape):
    n = len(shape)
    return pl.BlockSpec(shape, lambda *_: (0,) * n)


def _pre_kernel(x_ref, g_ref, w_ref, *out_refs, splits, scales):
    hb = _rms(x_ref[0], g_ref[...]).astype(BF16)
    off = 0
    for o_ref, width, scale in zip(out_refs, splits, scales):
        r = _dot(hb, w_ref[:, off:off + width])
        if scale != 1.0:
            r = r * scale
        o_ref[0] = r.astype(o_ref.dtype)
        off += width


def _pre(x, g, w, splits, dtypes, scales, tm, name):
    bsz, L, d = x.shape
    kern = functools.partial(_pre_kernel, splits=tuple(splits), scales=tuple(scales))
    return pl.pallas_call(
        kern,
        out_shape=[jax.ShapeDtypeStruct((bsz, L, s), dt) for s, dt in zip(splits, dtypes)],
        grid=(bsz, L // tm),
        in_specs=[pl.BlockSpec((1, tm, d), lambda b, i: (b, i, 0)),
                  _full((1, d)), _full(w.shape)],
        out_specs=[pl.BlockSpec((1, tm, s), lambda b, i: (b, i, 0)) for s in splits],
        compiler_params=_cparams(("parallel", "parallel")),
        name=name,
    )(x, g.reshape(1, d), w)


def _post_kernel(o_ref, z_ref, x_ref, w_ref, g_ref, out_ref):
    o = o_ref[0].astype(F32) * jax.nn.silu(z_ref[0])
    r = _dot(o.astype(BF16), w_ref[...])
    out_ref[0] = x_ref[0] + _rms(r, g_ref[...])


def _post(o, z, x, w, g, tm, name):
    bsz, L, d = x.shape
    width = o.shape[-1]
    return pl.pallas_call(
        _post_kernel,
        out_shape=jax.ShapeDtypeStruct(x.shape, x.dtype),
        grid=(bsz, L // tm),
        in_specs=[pl.BlockSpec((1, tm, width), lambda b, i: (b, i, 0)),
                  pl.BlockSpec((1, tm, width), lambda b, i: (b, i, 0)),
                  pl.BlockSpec((1, tm, d), lambda b, i: (b, i, 0)),
                  _full(w.shape), _full((1, d))],
        out_specs=pl.BlockSpec((1, tm, d), lambda b, i: (b, i, 0)),
        compiler_params=_cparams(("parallel", "parallel")),
        name=name,
    )(o, z, x, w, g.reshape(1, d))


def _s5_pre_kernel(x_ref, g_ref, w_ref, perm_ref, u_ref, z_ref, *, tt):
    bsz = x_ref.shape[0]
    width = u_ref.shape[1]
    x = x_ref[...].reshape(bsz * tt, x_ref.shape[2])
    hb = _rms(x, g_ref[...]).astype(BF16)
    hb = _dot(perm_ref[...], hb).astype(BF16)
    u_ref[...] = _dot(hb, w_ref[:, :width])
    z_ref[...] = _dot(hb, w_ref[:, width:])


def _s5_scan_kernel(u_ref, bb_ref, cc_ref, ar_ref, ai_ref, d_ref, y_ref, s_ref, carry_ref, *, tsteps):
    nblk = bb_ref.shape[0]
    sb = S5_STATE_BLOCK
    rows = carry_ref.shape[1]

    @pl.when(pl.program_id(0) == 0)
    def _():
        carry_ref[...] = jnp.zeros_like(carry_ref)

    for i in range(nblk):
        ub = u_ref[:, i * LANES:(i + 1) * LANES]
        s_ref[...] = _dot(ub.astype(BF16), bb_ref[i])
        ar = ar_ref[i]
        ai = ai_ref[i]

        def step(t, st):
            sr, si = st
            r0 = pl.multiple_of(t * rows, rows)
            xr = s_ref[pl.ds(r0, rows), 0:sb]
            xi = s_ref[pl.ds(r0, rows), sb:2 * sb]
            nr = ar * sr - ai * si + xr
            ni = ar * si + ai * sr + xi
            s_ref[pl.ds(r0, rows), 0:sb] = nr
            s_ref[pl.ds(r0, rows), sb:2 * sb] = ni
            return nr, ni

        sr, si = lax.fori_loop(0, tsteps, step,
                               (carry_ref[i, :, 0:sb], carry_ref[i, :, sb:2 * sb]), unroll=8)
        carry_ref[i, :, 0:sb] = sr
        carry_ref[i, :, sb:2 * sb] = si
        y = _dot(s_ref[...].astype(BF16), cc_ref[i]) + d_ref[:, i * LANES:(i + 1) * LANES] * ub
        y_ref[:, i * LANES:(i + 1) * LANES] = jax.nn.gelu(y)


def _s5_post_kernel(y_ref, z_ref, x_ref, wg_ref, bg_ref, wo_ref, g_ref, perm_ref, out_ref, *, tt):
    bsz = x_ref.shape[0]
    y = y_ref[...]
    gate = jax.nn.sigmoid(_dot(y.astype(BF16), wg_ref[...]) + bg_ref[...])
    o = y * gate * jax.nn.silu(z_ref[...])
    ob = _dot(perm_ref[...], o.astype(BF16)).astype(BF16)
    r = _dot(ob, wo_ref[...])
    x = x_ref[...].reshape(bsz * tt, x_ref.shape[2])
    out_ref[...] = (x + _rms(r, g_ref[...])).reshape(out_ref.shape)


def _s5_discretize(lam_re, lam_im, log_dt, b_re, b_im):
    dt = jnp.exp(log_dt)[:, None]
    mag = jnp.exp(lam_re * dt)
    ab_re = mag * jnp.cos(lam_im * dt)
    ab_im = mag * jnp.sin(lam_im * dt)
    den = lam_re * lam_re + lam_im * lam_im
    nr = ab_re - 1.0
    f_re = (nr * lam_re + ab_im * lam_im) / den
    f_im = (ab_im * lam_re - nr * lam_im) / den
    bb_re = f_re[..., None] * b_re - f_im[..., None] * b_im
    bb_im = f_re[..., None] * b_im + f_im[..., None] * b_re
    return ab_re, ab_im, bb_re, bb_im


def _s5_layer(x, pre_g, post_g, w_in, lam_re, lam_im, log_dt, b_re, b_im, c_re, c_im, d_skip,
              w_glu, b_glu, w_out):
    bsz, L, d = x.shape
    width = w_in.shape[1] // 2
    nblk = width // S5_CH_BLOCK
    gpb = S5_GROUPS_PER_BLOCK
    tt = S5_T
    rows = bsz * tt

    src = (np.arange(rows) % bsz) * tt + np.arange(rows) // bsz
    perm_np = np.zeros((rows, rows), np.float32)
    perm_np[np.arange(rows), src] = 1.0
    perm = jnp.asarray(perm_np, BF16)
    perm_t = jnp.asarray(perm_np.T, BF16)

    u, z = pl.pallas_call(
        functools.partial(_s5_pre_kernel, tt=tt),
        out_shape=[jax.ShapeDtypeStruct((L * bsz, width), F32)] * 2,
        grid=(L // tt,),
        in_specs=[pl.BlockSpec((bsz, tt, d), lambda i: (0, i, 0)),
                  _full((1, d)), _full(w_in.shape), _full(perm.shape)],
        out_specs=[pl.BlockSpec((rows, width), lambda i: (i, 0))] * 2,
        compiler_params=_cparams(("parallel",)),
        name="s5_pre",
    )(x, pre_g.reshape(1, d), w_in.astype(BF16), perm)

    ab_re, ab_im, bb_re, bb_im = _s5_discretize(lam_re, lam_im, log_dt, b_re, b_im)
    eye = jnp.eye(gpb, dtype=F32)

    def pack_b(bb):
        t = bb.reshape(nblk, gpb, SSM_STATE, SSM_GROUP)
        return jnp.einsum('igph,gk->ikhgp', t, eye).reshape(nblk, S5_CH_BLOCK, S5_STATE_BLOCK)

    def pack_c(cc):
        t = cc.reshape(nblk, gpb, SSM_GROUP, SSM_STATE)
        return jnp.einsum('ighp,gk->igpkh', t, eye).reshape(nblk, S5_STATE_BLOCK, S5_CH_BLOCK)

    bb = jnp.concatenate([pack_b(bb_re), pack_b(bb_im)], axis=2).astype(BF16)
    cc = jnp.concatenate([pack_c(c_re), -pack_c(c_im)], axis=1).astype(BF16)
    ar = jnp.broadcast_to(ab_re.reshape(nblk, 1, S5_STATE_BLOCK), (nblk, bsz, S5_STATE_BLOCK))
    ai = jnp.broadcast_to(ab_im.reshape(nblk, 1, S5_STATE_BLOCK), (nblk, bsz, S5_STATE_BLOCK))

    y = pl.pallas_call(
        functools.partial(_s5_scan_kernel, tsteps=tt),
        out_shape=jax.ShapeDtypeStruct((L * bsz, width), F32),
        grid=(L // tt,),
        in_specs=[pl.BlockSpec((rows, width), lambda i: (i, 0)),
                  _full(bb.shape), _full(cc.shape), _full(ar.shape), _full(ai.shape),
                  _full((1, width))],
        out_specs=pl.BlockSpec((rows, width), lambda i: (i, 0)),
        scratch_shapes=[pltpu.VMEM((rows, 2 * S5_STATE_BLOCK), F32),
                        pltpu.VMEM((nblk, bsz, 2 * S5_STATE_BLOCK), F32)],
        compiler_params=_cparams(("arbitrary",)),
        name="s5_scan",
    )(u, bb, cc, ar, ai, d_skip.reshape(1, width))

    return pl.pallas_call(
        functools.partial(_s5_post_kernel, tt=tt),
        out_shape=jax.ShapeDtypeStruct(x.shape, x.dtype),
        grid=(L // tt,),
        in_specs=[pl.BlockSpec((rows, width), lambda i: (i, 0)),
                  pl.BlockSpec((rows, width), lambda i: (i, 0)),
                  pl.BlockSpec((bsz, tt, d), lambda i: (0, i, 0)),
                  _full(w_glu.shape), _full((1, width)), _full(w_out.shape), _full((1, d)),
                  _full(perm_t.shape)],
        out_specs=pl.BlockSpec((bsz, tt, d), lambda i: (0, i, 0)),
        compiler_params=_cparams(("parallel",)),
        name="s5_post",
    )(y, z, x, w_glu.astype(BF16), b_glu.reshape(1, width),
      w_out.astype(BF16), post_g.reshape(1, d), perm_t)


def _swa_bias(rel_bias):
    W = WINDOW
    n = 4 * W
    dist = 2 * W - jnp.arange(n)
    valid = jnp.logical_and(dist >= 0, dist < W)
    dpos = jnp.maximum(dist, 0)
    max_exact = REL_BUCKETS // 2
    dist_f = jnp.maximum(dpos, 1).astype(F32)
    large = max_exact + (jnp.log(dist_f / max_exact) / math.log(REL_MAX_DIST / max_exact)
                         * (REL_BUCKETS - max_exact)).astype(jnp.int32)
    large = jnp.minimum(large, REL_BUCKETS - 1)
    bucket = jnp.where(dpos < max_exact, dpos, large)
    vec = jnp.where(valid[:, None], rel_bias[bucket].astype(F32), NEG_INF).T
    skew = jnp.tile(vec, (1, W))[:, :W * (n - 1)].reshape(vec.shape[0], W, n - 1)
    return skew[:, :, W:3 * W]


def _swa_pre_kernel(x_ref, g_ref, wqt_ref, wk_ref, wvt_ref, wz_ref, qt_ref, k_ref, vt_ref, z_ref, *, scale):
    hb = _rms(x_ref[0], g_ref[...]).astype(BF16)
    qt_ref[0] = (_dot_nt(wqt_ref[...], hb) * scale).astype(BF16)
    k_ref[0] = _dot(hb, wk_ref[...]).astype(BF16)
    vt_ref[0] = _dot_nt(wvt_ref[...], hb).astype(BF16)
    z_ref[0] = _dot(hb, wz_ref[...])


def _swa_kernel(qt_ref, kp_ref, kc_ref, vtp_ref, vtc_ref, bias_ref, sink_ref, o_ref, ot_ref):
    W = WINDOW
    n = pl.program_id(1)
    first = (n == 0).astype(jnp.int32)
    k = jnp.concatenate([kp_ref[0], kc_ref[0]], axis=0)
    vt = jnp.concatenate([vtp_ref[0], vtc_ref[0]], axis=1)
    zq = jnp.zeros((HEAD_DIM, SWA_GROUP * W), BF16)

    def scores(h):
        qh = jnp.concatenate([qt_ref[0, (h * SWA_GROUP + g) * HEAD_DIM:(h * SWA_GROUP + g + 1) * HEAD_DIM, :]
                              for g in range(SWA_GROUP)], axis=1)
        qz = jnp.concatenate([qh, zq] if h == 0 else [zq, qh], axis=0)
        return _dot(k, qz)

    raw = [scores(h) for h in range(SWA_KV_HEADS)]
    for h in range(SWA_KV_HEADS):
        heads = [h * SWA_GROUP + g for g in range(SWA_GROUP)]
        s = raw[h] + bias_ref[first, h]
        sink = sink_ref[h]
        m = jnp.maximum(jnp.max(s, axis=0, keepdims=True), sink)
        p = jnp.exp2(s - m)
        denom = jnp.sum(p, axis=0, keepdims=True) + jnp.exp2(sink - m)
        o = _dot(vt, p.astype(BF16))
        oh = o[h * HEAD_DIM:(h + 1) * HEAD_DIM] * (1.0 / denom)
        for g, hd in enumerate(heads):
            ot_ref[hd * HEAD_DIM:(hd + 1) * HEAD_DIM, :] = oh[:, g * W:(g + 1) * W]
    o_ref[0] = ot_ref[...].T


def _swa_layer(x, pre_g, post_g, w_in, sinks, w_out, rel_bias):
    bsz, L, d = x.shape
    width = SWA_HEADS * HEAD_DIM
    kvw = SWA_KV_HEADS * HEAD_DIM
    W = WINDOW
    nb = L // W
    log2e = math.log2(math.e)
    tm = 512
    tok = lambda w_: pl.BlockSpec((1, tm, w_), lambda b, i: (b, i, 0))
    tokt = lambda w_: pl.BlockSpec((1, w_, tm), lambda b, i: (b, 0, i))
    wqt = w_in[:, :width].T.astype(BF16)
    wk = w_in[:, width:width + kvw].astype(BF16)
    wvt = w_in[:, width + kvw:width + 2 * kvw].T.astype(BF16)
    wz = w_in[:, width + 2 * kvw:].astype(BF16)
    qt, k, vt, z = pl.pallas_call(
        functools.partial(_swa_pre_kernel, scale=HEAD_DIM ** -0.5 * log2e),
        out_shape=[jax.ShapeDtypeStruct((bsz, width, L), BF16),
                   jax.ShapeDtypeStruct((bsz, L, kvw), BF16),
                   jax.ShapeDtypeStruct((bsz, kvw, L), BF16),
                   jax.ShapeDtypeStruct((bsz, L, width), F32)],
        grid=(bsz, L // tm),
        in_specs=[tok(d), _full((1, d)), _full(wqt.shape), _full(wk.shape), _full(wvt.shape), _full(wz.shape)],
        out_specs=[tokt(width), tok(kvw), tokt(kvw), tok(width)],
        compiler_params=_cparams(("parallel", "parallel")),
        name="swa_pre",
    )(x, pre_g.reshape(1, d), wqt, wk, wvt, wz)

    bias = jnp.transpose(_swa_bias(rel_bias.astype(F32) * log2e), (0, 2, 1))
    has_prev = (jnp.arange(2 * W) >= W)[None, :, None]
    variants = [bias,
                jnp.where(has_prev, bias, NEG_INF)]
    bias_t = jnp.stack([v.reshape(SWA_KV_HEADS, SWA_GROUP, 2 * W, W).transpose(0, 2, 1, 3)
                        .reshape(SWA_KV_HEADS, 2 * W, SWA_GROUP * W) for v in variants])
    sink = jnp.repeat(sinks.astype(F32) * log2e, W).reshape(SWA_KV_HEADS, 1, SWA_GROUP * W)

    o = pl.pallas_call(
        _swa_kernel,
        out_shape=jax.ShapeDtypeStruct((bsz, L, width), F32),
        grid=(bsz, nb),
        in_specs=[pl.BlockSpec((1, width, W), lambda b, n: (b, 0, n)),
                  pl.BlockSpec((1, W, kvw), lambda b, n: (b, jnp.maximum(n - 1, 0), 0)),
                  pl.BlockSpec((1, W, kvw), lambda b, n: (b, n, 0)),
                  pl.BlockSpec((1, kvw, W), lambda b, n: (b, 0, jnp.maximum(n - 1, 0))),
                  pl.BlockSpec((1, kvw, W), lambda b, n: (b, 0, n)),
                  _full(bias_t.shape), _full(sink.shape)],
        out_specs=pl.BlockSpec((1, W, width), lambda b, n: (b, n, 0)),
        scratch_shapes=[pltpu.VMEM((width, W), F32)],
        compiler_params=_cparams(("parallel", "parallel")),
        name="swa_attn",
    )(qt, k, k, vt, vt, bias_t, sink)
    return _post(o, z, x, w_out.astype(BF16), post_g, 512, "swa_post")


def _mla_pre_kernel(x_ref, g_ref, w_ref, qn_ref, kvn_ref, wq_ref, wkv_ref, wvt_ref, cq_ref, sq_ref, ck_ref, sk_ref,
                    oqn_ref, oqr_ref, okn_ref, okr_ref, ov_ref, oz_ref, *, scale):
    nope = MLA_HEADS * MLA_NOPE
    rope = MLA_HEADS * MLA_ROPE
    vw = MLA_HEADS * MLA_V
    hb = _rms(x_ref[0], g_ref[...]).astype(BF16)
    o1 = MLA_Q_RANK
    o2 = o1 + MLA_KV_RANK
    o3 = o2 + vw
    cq = _dot(hb, w_ref[:, :o1])
    ckv = _dot(hb, w_ref[:, o1:o2])
    oz_ref[0] = _dot(hb, w_ref[:, o2:o3])
    kr = _dot(hb, w_ref[:, o3:o3 + LANES])
    krs = _dot(hb, w_ref[:, o3 + LANES:o3 + 2 * LANES])
    okr_ref[0] = (kr * ck_ref[...] + krs * sk_ref[...]).astype(BF16)
    cqb = _rms(cq, qn_ref[...]).astype(BF16)
    oqn_ref[0] = (_dot_nt(wq_ref[:nope], cqb) * scale).astype(BF16)
    qr = _dot_nt(wq_ref[nope:nope + rope], cqb)
    qrs = _dot_nt(wq_ref[nope + rope:nope + 2 * rope], cqb)
    oqr_ref[0] = ((qr * cq_ref[...] + qrs * sq_ref[...]) * scale).astype(BF16)
    ckb = _rms(ckv, kvn_ref[...]).astype(BF16)
    okn_ref[0] = _dot(ckb, wkv_ref[:, :nope]).astype(BF16)
    vt = _dot_nt(wvt_ref[...], ckb).astype(BF16)
    tk = ov_ref.shape[3]
    for c in range(ov_ref.shape[1]):
        ov_ref[0, c] = vt[:, c * tk:(c + 1) * tk]


def _mla_attn_kernel(qn_ref, qr_ref, kn_ref, kr_ref, v_ref, o_ref, qs_ref, acc_ref, m_ref, l_ref):
    tq = qn_ref.shape[2]
    tk = v_ref.shape[3]
    npairs = MLA_HEADS // 2
    i = pl.program_id(1)
    krow = lax.broadcasted_iota(jnp.int32, (tk, 2 * tq), 0)
    qcol = lax.broadcasted_iota(jnp.int32, (tk, 2 * tq), 1)
    causal = krow <= jnp.where(qcol >= tq, qcol - tq, qcol)

    zn = jnp.zeros((MLA_NOPE, tq), BF16)
    zr = jnp.zeros((LANES - MLA_ROPE, tq), BF16)
    for p in range(npairs):
        qn = qn_ref[0, p * LANES:(p + 1) * LANES, :]
        r0 = 2 * p * MLA_ROPE
        c0 = jnp.concatenate([qn[:MLA_NOPE], zn, qr_ref[0, r0:r0 + MLA_ROPE, :], zr], axis=0)
        c1 = jnp.concatenate([zn, qn[MLA_NOPE:], qr_ref[0, r0 + MLA_ROPE:r0 + 2 * MLA_ROPE, :], zr], axis=0)
        qs_ref[p] = jnp.concatenate([c0, c1], axis=1)

    m_ref[...] = jnp.full(m_ref.shape, NEG_INF, F32)
    l_ref[...] = jnp.zeros(l_ref.shape, F32)
    acc_ref[...] = jnp.zeros(acc_ref.shape, F32)

    def kv_step(j, masked):
        ks = pl.multiple_of(j * tk, tk)
        kr = kr_ref[0, pl.ds(ks, tk), :]

        def scores(p):
            kc = jnp.concatenate([kn_ref[0, pl.ds(ks, tk), p * LANES:(p + 1) * LANES], kr], axis=1)
            return _dot(kc, qs_ref[p])

        s_next = scores(0)
        for p in range(npairs):
            s = s_next
            if p + 1 < npairs:
                s_next = scores(p + 1)
            if masked:
                s = jnp.where(causal, s, NEG_INF)
            m_prev = m_ref[p]
            m_new = jnp.maximum(m_prev, jnp.max(s, axis=0, keepdims=True))
            alpha = jnp.exp2(m_prev - m_new)
            pr = jnp.exp2(s - m_new)
            l_ref[p] = alpha * l_ref[p] + jnp.sum(pr, axis=0, keepdims=True)
            pv = _dot(v_ref[0, j, p * LANES:(p + 1) * LANES, :], pr.astype(BF16))
            acc_ref[p] = alpha * acc_ref[p] + pv
            m_ref[p] = m_new

    def body(j, c):
        kv_step(j, False)
        return c

    lax.fori_loop(0, i, body, 0)
    kv_step(i, True)
    for p in range(npairs):
        a = acc_ref[p] * (1.0 / l_ref[p])
        ot = jnp.concatenate([a[:MLA_V, :tq], a[MLA_V:, tq:]], axis=0)
        o_ref[0, :, p * LANES:(p + 1) * LANES] = ot.T


def _mla_layer(x, pre_g, post_g, w_in, q_norm, kv_norm, w_uq, w_ukv, w_out):
    bsz, L, d = x.shape
    H = MLA_HEADS
    dq = MLA_NOPE + MLA_ROPE
    nope = H * MLA_NOPE
    rope = H * MLA_ROPE
    vw = H * MLA_V
    half = MLA_ROPE // 2
    o_kr = MLA_Q_RANK + MLA_KV_RANK
    o_z = o_kr + MLA_ROPE
    w_kr = w_in[:, o_kr:o_z]
    w_krs = jnp.concatenate([w_kr[:, half:], w_kr[:, :half]], axis=1)
    reps = LANES // MLA_ROPE
    w1 = jnp.concatenate([w_in[:, :o_kr], w_in[:, o_z:]] + [w_kr] * reps + [w_krs] * reps, axis=1).astype(BF16)
    wq3 = w_uq.reshape(MLA_Q_RANK, H, dq)
    wq_r = wq3[:, :, MLA_NOPE:]
    wq_rs = jnp.concatenate([wq_r[:, :, half:], wq_r[:, :, :half]], axis=2)
    wqt = jnp.concatenate([wq3[:, :, :MLA_NOPE].reshape(MLA_Q_RANK, nope), wq_r.reshape(MLA_Q_RANK, rope),
                           wq_rs.reshape(MLA_Q_RANK, rope)], axis=1).T.astype(BF16)
    wkv3 = w_ukv.reshape(MLA_KV_RANK, H, MLA_NOPE + MLA_V)
    wkn = wkv3[:, :, :MLA_NOPE].reshape(MLA_KV_RANK, nope).astype(BF16)
    wvt = wkv3[:, :, MLA_NOPE:].reshape(MLA_KV_RANK, vw).T.astype(BF16)
    inv = ROPE_BASE ** (-jnp.arange(0, MLA_ROPE, 2, dtype=F32) / MLA_ROPE)
    ang = jnp.arange(L, dtype=F32)[:, None] * inv[None, :]
    cos, sin = jnp.cos(ang), jnp.sin(ang)
    cos32 = jnp.concatenate([cos, cos], axis=1)
    sin32 = jnp.concatenate([-sin, sin], axis=1)
    cos_k, sin_k = jnp.tile(cos32, (1, LANES // MLA_ROPE)), jnp.tile(sin32, (1, LANES // MLA_ROPE))
    cos_q, sin_q = jnp.tile(cos32, (1, H)).T, jnp.tile(sin32, (1, H)).T

    tm = 512
    tk = MLA_TK
    tok = lambda w_: pl.BlockSpec((1, tm, w_), lambda b, i: (b, i, 0))
    tokt = lambda w_: pl.BlockSpec((1, w_, tm), lambda b, i: (b, 0, i))
    scale = dq ** -0.5 * math.log2(math.e)
    qn, qr, kn, kr, v, z = pl.pallas_call(
        functools.partial(_mla_pre_kernel, scale=scale),
        out_shape=[jax.ShapeDtypeStruct((bsz, nope, L), BF16),
                   jax.ShapeDtypeStruct((bsz, rope, L), BF16),
                   jax.ShapeDtypeStruct((bsz, L, nope), BF16),
                   jax.ShapeDtypeStruct((bsz, L, LANES), BF16),
                   jax.ShapeDtypeStruct((bsz, L // tk, vw, tk), BF16),
                   jax.ShapeDtypeStruct((bsz, L, vw), F32)],
        grid=(bsz, L // tm),
        in_specs=[tok(d), _full((1, d)), _full(w1.shape), _full((1, MLA_Q_RANK)), _full((1, MLA_KV_RANK)),
                  _full(wqt.shape), _full(wkn.shape), _full(wvt.shape),
                  pl.BlockSpec((rope, tm), lambda b, i: (0, i)), pl.BlockSpec((rope, tm), lambda b, i: (0, i)),
                  pl.BlockSpec((tm, LANES), lambda b, i: (i, 0)), pl.BlockSpec((tm, LANES), lambda b, i: (i, 0))],
        out_specs=[tokt(nope), tokt(rope), tok(nope), tok(LANES),
                   pl.BlockSpec((1, tm // tk, vw, tk), lambda b, i: (b, i, 0, 0)), tok(vw)],
        compiler_params=_cparams(("parallel", "parallel")),
        name="mla_pre",
    )(x, pre_g.reshape(1, d), w1, q_norm.reshape(1, -1), kv_norm.reshape(1, -1), wqt, wkn, wvt,
      cos_q, sin_q, cos_k, sin_k)

    tq = MLA_TQ
    npairs = H // 2
    qspec = lambda w_: pl.BlockSpec((1, w_, tq), lambda b, i: (b, 0, i))
    kspec = lambda w_: pl.BlockSpec((1, L, w_), lambda b, i: (b, 0, 0))
    o = pl.pallas_call(
        _mla_attn_kernel,
        out_shape=jax.ShapeDtypeStruct((bsz, L, vw), F32),
        grid=(bsz, L // tq),
        in_specs=[qspec(nope), qspec(rope), kspec(nope), kspec(LANES),
                  pl.BlockSpec((1, L // tk, vw, tk), lambda b, i: (b, 0, 0, 0))],
        out_specs=pl.BlockSpec((1, tq, vw), lambda b, i: (b, i, 0)),
        scratch_shapes=[pltpu.VMEM((npairs, 2 * LANES, 2 * tq), BF16),
                        pltpu.VMEM((npairs, LANES, 2 * tq), F32),
                        pltpu.VMEM((npairs, 1, 2 * tq), F32),
                        pltpu.VMEM((npairs, 1, 2 * tq), F32)],
        compiler_params=_cparams(("parallel", "arbitrary")),
        name="mla_attn",
    )(qn, qr, kn, kr, v)
    return _post(o, z, x, w_out.astype(BF16), post_g, 512, "mla_post")


def _sgu_kernel(x_ref, g_ref, w_ref, lng_ref, lnb_ref, ws_ref, bs_ref, wo_ref, pg_ref, out_ref, s_ref):
    width = wo_ref.shape[0]
    tm = x_ref.shape[1]
    lane = lax.broadcasted_iota(jnp.int32, (1, LANES), 1)
    lo = lane < HALF
    x = x_ref[0]
    hb = _rms(x, g_ref[...]).astype(BF16)
    v = jax.nn.gelu(_dot(hb, w_ref[:, width:2 * width]))
    mu = jnp.mean(v, axis=-1, keepdims=True)
    vc = v - mu
    var = jnp.mean(vc * vc, axis=-1, keepdims=True)
    vb = (vc * lax.rsqrt(var + EPS) * lng_ref[...] + lnb_ref[...]).astype(BF16)
    for c in range(tm // SGU_CHUNK):
        for jj in range(width // LANES):
            blk = vb[c * SGU_CHUNK:(c + 1) * SGU_CHUNK, jj * LANES:(jj + 1) * LANES]
            r = _dot(ws_ref[jj], blk)
            s_ref[c * SGU_CHUNK:(c + 1) * SGU_CHUNK, jj * LANES:(jj + 1) * LANES] = (
                jnp.where(lo, r[:SGU_CHUNK], r[SGU_CHUNK:]) + bs_ref[jj])
    u = jax.nn.gelu(_dot(hb, w_ref[:, :width]))
    z = _dot(hb, w_ref[:, 2 * width:])
    o = u * s_ref[...] * jax.nn.silu(z)
    r = _dot(o.astype(BF16), wo_ref[...])
    out_ref[0] = x + _rms(r, pg_ref[...])


def _sgu_layer(x, pre_g, post_g, w_in, ln_g, ln_b, w_s, b_s, w_out):
    bsz, L, d = x.shape
    width = w_out.shape[0]
    T = SGU_CHUNK
    gd = width // SGU_GROUPS
    tril = jnp.tril(jnp.ones((T, T), dtype=bool))
    ws = jnp.where(tril[None], w_s, 0.0).reshape(SGU_GROUPS // 2, 2 * T, T).astype(BF16)
    bs = jnp.repeat(b_s.astype(F32).T, gd, axis=1)
    bs = bs.reshape(T, width // LANES, LANES).transpose(1, 0, 2)
    tm = 512
    return pl.pallas_call(
        _sgu_kernel,
        out_shape=jax.ShapeDtypeStruct(x.shape, x.dtype),
        grid=(bsz, L // tm),
        in_specs=[pl.BlockSpec((1, tm, d), lambda b, i: (b, i, 0)),
                  _full((1, d)), _full(w_in.shape), _full((1, width)), _full((1, width)),
                  _full(ws.shape), _full(bs.shape), _full(w_out.shape), _full((1, d))],
        out_specs=pl.BlockSpec((1, tm, d), lambda b, i: (b, i, 0)),
        scratch_shapes=[pltpu.VMEM((tm, width), F32)],
        compiler_params=_cparams(("parallel", "parallel")),
        name="sgu",
    )(x, pre_g.reshape(1, d), w_in.astype(BF16), ln_g.reshape(1, width), ln_b.reshape(1, width),
      ws, bs, w_out.astype(BF16), post_g.reshape(1, d))


def kernel(x, pre_norm, post_norm, rel_bias, a_w_in, a_lam_re, a_lam_im, a_log_dt, a_b_re, a_b_im, a_c_re, a_c_im, a_d, a_w_glu, a_b_glu, a_w_out, b_w_in, b_sinks, b_w_out, c_w_in, c_q_norm, c_kv_norm, c_w_uq, c_w_ukv, c_w_out, d_w_in, d_ln_g, d_ln_b, d_w_s, d_b_s, d_w_out):
    depth = pre_norm.shape[0]
    for i in range(depth):
        kind, j = i % 4, i // 4
        if kind == 0:
            x = _s5_layer(x, pre_norm[i], post_norm[i], a_w_in[j], a_lam_re[j], a_lam_im[j], a_log_dt[j],
                          a_b_re[j], a_b_im[j], a_c_re[j], a_c_im[j], a_d[j], a_w_glu[j], a_b_glu[j],
                          a_w_out[j])
        elif kind == 1:
            x = _swa_layer(x, pre_norm[i], post_norm[i], b_w_in[j], b_sinks[j], b_w_out[j], rel_bias)
        elif kind == 2:
            x = _mla_layer(x, pre_norm[i], post_norm[i], c_w_in[j], c_q_norm[j], c_kv_norm[j], c_w_uq[j],
                           c_w_ukv[j], c_w_out[j])
        else:
            x = _sgu_layer(x, pre_norm[i], post_norm[i], d_w_in[j], d_ln_g[j], d_ln_b[j], d_w_s[j],
                           d_b_s[j], d_w_out[j])
    return x
```

```python
import functools
import math

import jax
import jax.numpy as jnp
import numpy as np
from jax import lax
from jax.experimental import pallas as pl
from jax.experimental.pallas import tpu as pltpu

F32 = jnp.float32
BF16 = jnp.bfloat16

D_MODEL = 1024
EPS = 1e-6
NEG_INF = -1e30
LANES = 128
HALF = LANES // 2

SSM_GROUP = 16
SSM_STATE = 64
S5_CH_BLOCK = LANES
S5_GROUPS_PER_BLOCK = S5_CH_BLOCK // SSM_GROUP
S5_STATE_BLOCK = S5_GROUPS_PER_BLOCK * SSM_STATE
S5_T = 64

HEAD_DIM = 64
SWA_HEADS = 16
SWA_KV_HEADS = 2
SWA_GROUP = SWA_HEADS // SWA_KV_HEADS
WINDOW = 128
REL_BUCKETS = 32
REL_MAX_DIST = 128

MLA_HEADS = 16
MLA_NOPE = 64
MLA_ROPE = 32
MLA_V = 64
MLA_KV_RANK = 256
MLA_Q_RANK = 768
ROPE_BASE = 10000.0
MLA_TQ = 256
MLA_TK = 256
MLA_LOOKAHEAD = 4

SGU_CHUNK = 128
SGU_GROUPS = 16

VMEM_LIMIT = 56 * 1024 * 1024


def _cparams(sem):
    return pltpu.CompilerParams(dimension_semantics=sem, vmem_limit_bytes=VMEM_LIMIT)


def _rms(x, g):
    return x * lax.rsqrt(jnp.mean(x * x, axis=-1, keepdims=True) + EPS) * g


def _dot(a, b):
    return jnp.dot(a, b, preferred_element_type=F32)


def _dot_nt(a, b):
    return lax.dot_general(a, b, (((1,), (1,)), ((), ())), preferred_element_type=F32)


def _full(shape):
    n = len(shape)
    return pl.BlockSpec(shape, lambda *_: (0,) * n)


def _pre_kernel(x_ref, g_ref, w_ref, *out_refs, splits, scales):
    hb = _rms(x_ref[0], g_ref[...]).astype(BF16)
    off = 0
    for o_ref, width, scale in zip(out_refs, splits, scales):
        r = _dot(hb, w_ref[:, off:off + width])
        if scale != 1.0:
            r = r * scale
        o_ref[0] = r.astype(o_ref.dtype)
        off += width


def _pre(x, g, w, splits, dtypes, scales, tm, name):
    bsz, L, d = x.shape
    kern = functools.partial(_pre_kernel, splits=tuple(splits), scales=tuple(scales))
    return pl.pallas_call(
        kern,
        out_shape=[jax.ShapeDtypeStruct((bsz, L, s), dt) for s, dt in zip(splits, dtypes)],
        grid=(bsz, L // tm),
        in_specs=[pl.BlockSpec((1, tm, d), lambda b, i: (b, i, 0)),
                  _full((1, d)), _full(w.shape)],
        out_specs=[pl.BlockSpec((1, tm, s), lambda b, i: (b, i, 0)) for s in splits],
        compiler_params=_cparams(("parallel", "parallel")),
        name=name,
    )(x, g.reshape(1, d), w)


def _post_kernel(o_ref, z_ref, x_ref, w_ref, g_ref, out_ref):
    o = o_ref[0].astype(F32) * jax.nn.silu(z_ref[0])
    r = _dot(o.astype(BF16), w_ref[...])
    out_ref[0] = x_ref[0] + _rms(r, g_ref[...])


def _post(o, z, x, w, g, tm, name):
    bsz, L, d = x.shape
    width = o.shape[-1]
    return pl.pallas_call(
        _post_kernel,
        out_shape=jax.ShapeDtypeStruct(x.shape, x.dtype),
        grid=(bsz, L // tm),
        in_specs=[pl.BlockSpec((1, tm, width), lambda b, i: (b, i, 0)),
                  pl.BlockSpec((1, tm, width), lambda b, i: (b, i, 0)),
                  pl.BlockSpec((1, tm, d), lambda b, i: (b, i, 0)),
                  _full(w.shape), _full((1, d))],
        out_specs=pl.BlockSpec((1, tm, d), lambda b, i: (b, i, 0)),
        compiler_params=_cparams(("parallel", "parallel")),
        name=name,
    )(o, z, x, w, g.reshape(1, d))


def _s5_pre_kernel(x_ref, g_ref, w_ref, perm_ref, u_ref, z_ref, *, tt):
    bsz = x_ref.shape[0]
    width = u_ref.shape[1]
    x = x_ref[...].reshape(bsz * tt, x_ref.shape[2])
    hb = _rms(x, g_ref[...]).astype(BF16)
    hb = _dot(perm_ref[...], hb).astype(BF16)
    u_ref[...] = _dot(hb, w_ref[:, :width])
    z_ref[...] = _dot(hb, w_ref[:, width:])


def _s5_scan_kernel(u_ref, bb_ref, cc_ref, ar_ref, ai_ref, d_ref, y_ref, s_ref, carry_ref, *, tsteps):
    nblk = bb_ref.shape[0]
    sb = S5_STATE_BLOCK
    rows = carry_ref.shape[1]

    @pl.when(pl.program_id(0) == 0)
    def _():
        carry_ref[...] = jnp.zeros_like(carry_ref)

    def project_in(i):
        s_ref[i % 2] = _dot(u_ref[:, i * LANES:(i + 1) * LANES].astype(BF16), bb_ref[i])

    project_in(0)
    for i in range(nblk):
        if i + 1 < nblk:
            project_in(i + 1)
        buf = s_ref.at[i % 2]
        ar = ar_ref[i]
        ai = ai_ref[i]
        sr = carry_ref[i, :, 0:sb]
        si = carry_ref[i, :, sb:2 * sb]
        for t in range(tsteps):
            r0 = t * rows
            nr = ar * sr - ai * si + buf[r0:r0 + rows, 0:sb]
            ni = ar * si + ai * sr + buf[r0:r0 + rows, sb:2 * sb]
            buf[r0:r0 + rows, 0:sb] = nr
            buf[r0:r0 + rows, sb:2 * sb] = ni
            sr, si = nr, ni
        carry_ref[i, :, 0:sb] = sr
        carry_ref[i, :, sb:2 * sb] = si
        ub = u_ref[:, i * LANES:(i + 1) * LANES]
        y = _dot(buf[...].astype(BF16), cc_ref[i]) + d_ref[:, i * LANES:(i + 1) * LANES] * ub
        y_ref[:, i * LANES:(i + 1) * LANES] = jax.nn.gelu(y)


def _s5_post_kernel(y_ref, z_ref, x_ref, wg_ref, bg_ref, wo_ref, g_ref, perm_ref, out_ref, *, tt):
    bsz = x_ref.shape[0]
    y = y_ref[...]
    gate = jax.nn.sigmoid(_dot(y.astype(BF16), wg_ref[...]) + bg_ref[...])
    o = y * gate * jax.nn.silu(z_ref[...])
    ob = _dot(perm_ref[...], o.astype(BF16)).astype(BF16)
    r = _dot(ob, wo_ref[...])
    x = x_ref[...].reshape(bsz * tt, x_ref.shape[2])
    out_ref[...] = (x + _rms(r, g_ref[...])).reshape(out_ref.shape)


def _s5_discretize(lam_re, lam_im, log_dt, b_re, b_im):
    dt = jnp.exp(log_dt)[:, None]
    mag = jnp.exp(lam_re * dt)
    ab_re = mag * jnp.cos(lam_im * dt)
    ab_im = mag * jnp.sin(lam_im * dt)
    den = lam_re * lam_re + lam_im * lam_im
    nr = ab_re - 1.0
    f_re = (nr * lam_re + ab_im * lam_im) / den
    f_im = (ab_im * lam_re - nr * lam_im) / den
    bb_re = f_re[..., None] * b_re - f_im[..., None] * b_im
    bb_im = f_re[..., None] * b_im + f_im[..., None] * b_re
    return ab_re, ab_im, bb_re, bb_im


def _s5_layer(x, pre_g, post_g, w_in, lam_re, lam_im, log_dt, b_re, b_im, c_re, c_im, d_skip,
              w_glu, b_glu, w_out):
    bsz, L, d = x.shape
    width = w_in.shape[1] // 2
    nblk = width // S5_CH_BLOCK
    gpb = S5_GROUPS_PER_BLOCK
    tt = S5_T
    rows = bsz * tt

    src = (np.arange(rows) % bsz) * tt + np.arange(rows) // bsz
    perm_np = np.zeros((rows, rows), np.float32)
    perm_np[np.arange(rows), src] = 1.0
    perm = jnp.asarray(perm_np, BF16)
    perm_t = jnp.asarray(perm_np.T, BF16)

    u, z = pl.pallas_call(
        functools.partial(_s5_pre_kernel, tt=tt),
        out_shape=[jax.ShapeDtypeStruct((L * bsz, width), F32)] * 2,
        grid=(L // tt,),
        in_specs=[pl.BlockSpec((bsz, tt, d), lambda i: (0, i, 0)),
                  _full((1, d)), _full(w_in.shape), _full(perm.shape)],
        out_specs=[pl.BlockSpec((rows, width), lambda i: (i, 0))] * 2,
        compiler_params=_cparams(("parallel",)),
        name="s5_pre",
    )(x, pre_g.reshape(1, d), w_in.astype(BF16), perm)

    ab_re, ab_im, bb_re, bb_im = _s5_discretize(lam_re, lam_im, log_dt, b_re, b_im)
    eye = jnp.eye(gpb, dtype=F32)

    def pack_b(bb):
        t = bb.reshape(nblk, gpb, SSM_STATE, SSM_GROUP)
        return jnp.einsum('igph,gk->ikhgp', t, eye).reshape(nblk, S5_CH_BLOCK, S5_STATE_BLOCK)

    def pack_c(cc):
        t = cc.reshape(nblk, gpb, SSM_GROUP, SSM_STATE)
        return jnp.einsum('ighp,gk->igpkh', t, eye).reshape(nblk, S5_STATE_BLOCK, S5_CH_BLOCK)

    bb = jnp.concatenate([pack_b(bb_re), pack_b(bb_im)], axis=2).astype(BF16)
    cc = jnp.concatenate([pack_c(c_re), -pack_c(c_im)], axis=1).astype(BF16)
    ar = jnp.broadcast_to(ab_re.reshape(nblk, 1, S5_STATE_BLOCK), (nblk, bsz, S5_STATE_BLOCK))
    ai = jnp.broadcast_to(ab_im.reshape(nblk, 1, S5_STATE_BLOCK), (nblk, bsz, S5_STATE_BLOCK))

    y = pl.pallas_call(
        functools.partial(_s5_scan_kernel, tsteps=tt),
        out_shape=jax.ShapeDtypeStruct((L * bsz, width), F32),
        grid=(L // tt,),
        in_specs=[pl.BlockSpec((rows, width), lambda i: (i, 0)),
                  _full(bb.shape), _full(cc.shape), _full(ar.shape), _full(ai.shape),
                  _full((1, width))],
        out_specs=pl.BlockSpec((rows, width), lambda i: (i, 0)),
        scratch_shapes=[pltpu.VMEM((2, rows, 2 * S5_STATE_BLOCK), F32),
                        pltpu.VMEM((nblk, bsz, 2 * S5_STATE_BLOCK), F32)],
        compiler_params=_cparams(("arbitrary",)),
        name="s5_scan",
    )(u, bb, cc, ar, ai, d_skip.reshape(1, width))

    return pl.pallas_call(
        functools.partial(_s5_post_kernel, tt=tt),
        out_shape=jax.ShapeDtypeStruct(x.shape, x.dtype),
        grid=(L // tt,),
        in_specs=[pl.BlockSpec((rows, width), lambda i: (i, 0)),
                  pl.BlockSpec((rows, width), lambda i: (i, 0)),
                  pl.BlockSpec((bsz, tt, d), lambda i: (0, i, 0)),
                  _full(w_glu.shape), _full((1, width)), _full(w_out.shape), _full((1, d)),
                  _full(perm_t.shape)],
        out_specs=pl.BlockSpec((bsz, tt, d), lambda i: (0, i, 0)),
        compiler_params=_cparams(("parallel",)),
        name="s5_post",
    )(y, z, x, w_glu.astype(BF16), b_glu.reshape(1, width),
      w_out.astype(BF16), post_g.reshape(1, d), perm_t)


def _swa_bias(rel_bias):
    W = WINDOW
    n = 4 * W
    dist = 2 * W - jnp.arange(n)
    valid = jnp.logical_and(dist >= 0, dist < W)
    dpos = jnp.maximum(dist, 0)
    max_exact = REL_BUCKETS // 2
    dist_f = jnp.maximum(dpos, 1).astype(F32)
    large = max_exact + (jnp.log(dist_f / max_exact) / math.log(REL_MAX_DIST / max_exact)
                         * (REL_BUCKETS - max_exact)).astype(jnp.int32)
    large = jnp.minimum(large, REL_BUCKETS - 1)
    bucket = jnp.where(dpos < max_exact, dpos, large)
    vec = jnp.where(valid[:, None], rel_bias[bucket].astype(F32), NEG_INF).T
    skew = jnp.tile(vec, (1, W))[:, :W * (n - 1)].reshape(vec.shape[0], W, n - 1)
    return skew[:, :, W:3 * W]


def _swa_pre_kernel(x_ref, g_ref, wqt_ref, wk_ref, wvt_ref, wz_ref, qt_ref, k_ref, vt_ref, z_ref, *, scale):
    hb = _rms(x_ref[0], g_ref[...]).astype(BF16)
    qt_ref[0] = (_dot_nt(wqt_ref[...], hb) * scale).astype(BF16)
    k_ref[0] = _dot(hb, wk_ref[...]).astype(BF16)
    vt_ref[0] = _dot_nt(wvt_ref[...], hb).astype(BF16)
    z_ref[0] = _dot(hb, wz_ref[...])


def _swa_kernel(qt_ref, kp_ref, kc_ref, vtp_ref, vtc_ref, bias_ref, sink_ref, o_ref, ot_ref):
    W = WINDOW
    n = pl.program_id(1)
    first = (n == 0).astype(jnp.int32)
    k = jnp.concatenate([kp_ref[0], kc_ref[0]], axis=0)
    vt = jnp.concatenate([vtp_ref[0], vtc_ref[0]], axis=1)
    zq = jnp.zeros((HEAD_DIM, SWA_GROUP * W), BF16)

    def scores(h):
        qh = jnp.concatenate([qt_ref[0, (h * SWA_GROUP + g) * HEAD_DIM:(h * SWA_GROUP + g + 1) * HEAD_DIM, :]
                              for g in range(SWA_GROUP)], axis=1)
        qz = jnp.concatenate([qh, zq] if h == 0 else [zq, qh], axis=0)
        return _dot(k, qz)

    raw = [scores(h) for h in range(SWA_KV_HEADS)]
    for h in range(SWA_KV_HEADS):
        heads = [h * SWA_GROUP + g for g in range(SWA_GROUP)]
        s = raw[h] + bias_ref[first, h]
        sink = sink_ref[h]
        m = jnp.maximum(jnp.max(s, axis=0, keepdims=True), sink)
        p = jnp.exp2(s - m)
        denom = jnp.sum(p, axis=0, keepdims=True) + jnp.exp2(sink - m)
        o = _dot(vt, p.astype(BF16))
        oh = o[h * HEAD_DIM:(h + 1) * HEAD_DIM] * (1.0 / denom)
        for g, hd in enumerate(heads):
            ot_ref[hd * HEAD_DIM:(hd + 1) * HEAD_DIM, :] = oh[:, g * W:(g + 1) * W]
    o_ref[0] = ot_ref[...].T


def _swa_layer(x, pre_g, post_g, w_in, sinks, w_out, rel_bias):
    bsz, L, d = x.shape
    width = SWA_HEADS * HEAD_DIM
    kvw = SWA_KV_HEADS * HEAD_DIM
    W = WINDOW
    nb = L // W
    log2e = math.log2(math.e)
    tm = 512
    tok = lambda w_: pl.BlockSpec((1, tm, w_), lambda b, i: (b, i, 0))
    tokt = lambda w_: pl.BlockSpec((1, w_, tm), lambda b, i: (b, 0, i))
    wqt = w_in[:, :width].T.astype(BF16)
    wk = w_in[:, width:width + kvw].astype(BF16)
    wvt = w_in[:, width + kvw:width + 2 * kvw].T.astype(BF16)
    wz = w_in[:, width + 2 * kvw:].astype(BF16)
    qt, k, vt, z = pl.pallas_call(
        functools.partial(_swa_pre_kernel, scale=HEAD_DIM ** -0.5 * log2e),
        out_shape=[jax.ShapeDtypeStruct((bsz, width, L), BF16),
                   jax.ShapeDtypeStruct((bsz, L, kvw), BF16),
                   jax.ShapeDtypeStruct((bsz, kvw, L), BF16),
                   jax.ShapeDtypeStruct((bsz, L, width), F32)],
        grid=(bsz, L // tm),
        in_specs=[tok(d), _full((1, d)), _full(wqt.shape), _full(wk.shape), _full(wvt.shape), _full(wz.shape)],
        out_specs=[tokt(width), tok(kvw), tokt(kvw), tok(width)],
        compiler_params=_cparams(("parallel", "parallel")),
        name="swa_pre",
    )(x, pre_g.reshape(1, d), wqt, wk, wvt, wz)

    bias = jnp.transpose(_swa_bias(rel_bias.astype(F32) * log2e), (0, 2, 1))
    has_prev = (jnp.arange(2 * W) >= W)[None, :, None]
    variants = [bias,
                jnp.where(has_prev, bias, NEG_INF)]
    bias_t = jnp.stack([v.reshape(SWA_KV_HEADS, SWA_GROUP, 2 * W, W).transpose(0, 2, 1, 3)
                        .reshape(SWA_KV_HEADS, 2 * W, SWA_GROUP * W) for v in variants])
    sink = jnp.repeat(sinks.astype(F32) * log2e, W).reshape(SWA_KV_HEADS, 1, SWA_GROUP * W)

    o = pl.pallas_call(
        _swa_kernel,
        out_shape=jax.ShapeDtypeStruct((bsz, L, width), F32),
        grid=(bsz, nb),
        in_specs=[pl.BlockSpec((1, width, W), lambda b, n: (b, 0, n)),
                  pl.BlockSpec((1, W, kvw), lambda b, n: (b, jnp.maximum(n - 1, 0), 0)),
                  pl.BlockSpec((1, W, kvw), lambda b, n: (b, n, 0)),
                  pl.BlockSpec((1, kvw, W), lambda b, n: (b, 0, jnp.maximum(n - 1, 0))),
                  pl.BlockSpec((1, kvw, W), lambda b, n: (b, 0, n)),
                  _full(bias_t.shape), _full(sink.shape)],
        out_specs=pl.BlockSpec((1, W, width), lambda b, n: (b, n, 0)),
        scratch_shapes=[pltpu.VMEM((width, W), F32)],
        compiler_params=_cparams(("parallel", "parallel")),
        name="swa_attn",
    )(qt, k, k, vt, vt, bias_t, sink)
    return _post(o, z, x, w_out.astype(BF16), post_g, 512, "swa_post")


def _mla_pre_kernel(x_ref, g_ref, w_ref, qn_ref, kvn_ref, wq_ref, wkv_ref, wvt_ref, cq_ref, sq_ref, ck_ref, sk_ref,
                    oqn_ref, oqr_ref, okn_ref, okr_ref, ov_ref, oz_ref, *, scale):
    nope = MLA_HEADS * MLA_NOPE
    rope = MLA_HEADS * MLA_ROPE
    vw = MLA_HEADS * MLA_V
    hb = _rms(x_ref[0], g_ref[...]).astype(BF16)
    o1 = MLA_Q_RANK
    o2 = o1 + MLA_KV_RANK
    o3 = o2 + vw
    cq = _dot(hb, w_ref[:, :o1])
    ckv = _dot(hb, w_ref[:, o1:o2])
    oz_ref[0] = _dot(hb, w_ref[:, o2:o3])
    kr = _dot(hb, w_ref[:, o3:o3 + LANES])
    krs = _dot(hb, w_ref[:, o3 + LANES:o3 + 2 * LANES])
    okr_ref[0] = (kr * ck_ref[...] + krs * sk_ref[...]).astype(BF16)
    cqb = _rms(cq, qn_ref[...]).astype(BF16)
    oqn_ref[0] = (_dot_nt(wq_ref[:nope], cqb) * scale).astype(BF16)
    qr = _dot_nt(wq_ref[nope:nope + rope], cqb)
    qrs = _dot_nt(wq_ref[nope + rope:nope + 2 * rope], cqb)
    oqr_ref[0] = ((qr * cq_ref[...] + qrs * sq_ref[...]) * scale).astype(BF16)
    ckb = _rms(ckv, kvn_ref[...]).astype(BF16)
    okn_ref[0] = _dot(ckb, wkv_ref[:, :nope]).astype(BF16)
    vt = _dot_nt(wvt_ref[...], ckb).astype(BF16)
    tk = ov_ref.shape[3]
    for c in range(ov_ref.shape[1]):
        ov_ref[0, c] = vt[:, c * tk:(c + 1) * tk]


def _mla_attn_kernel(qn_ref, qr_ref, kn_ref, kr_ref, v_ref, o_ref, qs_ref, acc_ref, m_ref):
    tq = qn_ref.shape[2]
    tk = v_ref.shape[3]
    npairs = MLA_HEADS // 2
    i = pl.program_id(1)
    krow = lax.broadcasted_iota(jnp.int32, (tk, 2 * tq), 0)
    qcol = lax.broadcasted_iota(jnp.int32, (tk, 2 * tq), 1)
    causal = krow <= jnp.where(qcol >= tq, qcol - tq, qcol)

    zn = jnp.zeros((MLA_NOPE, tq), BF16)
    zr = jnp.zeros((LANES - MLA_ROPE, tq), BF16)
    for p in range(npairs):
        qn = qn_ref[0, p * LANES:(p + 1) * LANES, :]
        r0 = 2 * p * MLA_ROPE
        c0 = jnp.concatenate([qn[:MLA_NOPE], zn, qr_ref[0, r0:r0 + MLA_ROPE, :], zr], axis=0)
        c1 = jnp.concatenate([zn, qn[MLA_NOPE:], qr_ref[0, r0 + MLA_ROPE:r0 + 2 * MLA_ROPE, :], zr], axis=0)
        qs_ref[p] = jnp.concatenate([c0, c1], axis=1)

    m_ref[...] = jnp.full(m_ref.shape, NEG_INF, F32)
    acc_ref[...] = jnp.zeros(acc_ref.shape, F32)
    ones = jnp.ones((acc_ref.shape[1] - LANES, tk), BF16)

    def kv_step(j, masked):
        ks = pl.multiple_of(j * tk, tk)
        kr = kr_ref[0, pl.ds(ks, tk), :]

        def scores(p):
            kc = jnp.concatenate([kn_ref[0, pl.ds(ks, tk), p * LANES:(p + 1) * LANES], kr], axis=1)
            return _dot(kc, qs_ref[p])

        pending = [scores(p) for p in range(MLA_LOOKAHEAD)]
        for p in range(npairs):
            s = pending.pop(0)
            if p + MLA_LOOKAHEAD < npairs:
                pending.append(scores(p + MLA_LOOKAHEAD))
            probs, alphas = [], []
            for c in range(2 * tq // LANES):
                sc = s[:, c * LANES:(c + 1) * LANES]
                if masked:
                    sc = jnp.where(causal[:, c * LANES:(c + 1) * LANES], sc, NEG_INF)
                m_prev = m_ref[p, :, c * LANES:(c + 1) * LANES]
                m_new = jnp.maximum(m_prev, jnp.max(sc, axis=0, keepdims=True))
                alphas.append(jnp.exp2(m_prev - m_new))
                probs.append(jnp.exp2(sc - m_new).astype(BF16))
                m_ref[p, :, c * LANES:(c + 1) * LANES] = m_new
            alpha = jnp.concatenate(alphas, axis=1)
            vones = jnp.concatenate([v_ref[0, j, p * LANES:(p + 1) * LANES, :], ones], axis=0)
            pv = _dot(vones, jnp.concatenate(probs, axis=1))
            acc_ref[p] = alpha * acc_ref[p] + pv

    def body(j, c):
        kv_step(j, False)
        return c

    lax.fori_loop(0, i, body, 0)
    kv_step(i, True)
    for p in range(npairs):
        a = acc_ref[p]
        a = a[:LANES] * (1.0 / a[LANES:LANES + 1])
        ot = jnp.concatenate([a[:MLA_V, :tq], a[MLA_V:, tq:]], axis=0)
        o_ref[0, :, p * LANES:(p + 1) * LANES] = ot.T


def _mla_layer(x, pre_g, post_g, w_in, q_norm, kv_norm, w_uq, w_ukv, w_out):
    bsz, L, d = x.shape
    H = MLA_HEADS
    dq = MLA_NOPE + MLA_ROPE
    nope = H * MLA_NOPE
    rope = H * MLA_ROPE
    vw = H * MLA_V
    half = MLA_ROPE // 2
    o_kr = MLA_Q_RANK + MLA_KV_RANK
    o_z = o_kr + MLA_ROPE
    w_kr = w_in[:, o_kr:o_z]
    w_krs = jnp.concatenate([w_kr[:, half:], w_kr[:, :half]], axis=1)
    reps = LANES // MLA_ROPE
    w1 = jnp.concatenate([w_in[:, :o_kr], w_in[:, o_z:]] + [w_kr] * reps + [w_krs] * reps, axis=1).astype(BF16)
    wq3 = w_uq.reshape(MLA_Q_RANK, H, dq)
    wq_r = wq3[:, :, MLA_NOPE:]
    wq_rs = jnp.concatenate([wq_r[:, :, half:], wq_r[:, :, :half]], axis=2)
    wqt = jnp.concatenate([wq3[:, :, :MLA_NOPE].reshape(MLA_Q_RANK, nope), wq_r.reshape(MLA_Q_RANK, rope),
                           wq_rs.reshape(MLA_Q_RANK, rope)], axis=1).T.astype(BF16)
    wkv3 = w_ukv.reshape(MLA_KV_RANK, H, MLA_NOPE + MLA_V)
    wkn = wkv3[:, :, :MLA_NOPE].reshape(MLA_KV_RANK, nope).astype(BF16)
    wvt = wkv3[:, :, MLA_NOPE:].reshape(MLA_KV_RANK, vw).T.astype(BF16)
    inv = ROPE_BASE ** (-jnp.arange(0, MLA_ROPE, 2, dtype=F32) / MLA_ROPE)
    ang = jnp.arange(L, dtype=F32)[:, None] * inv[None, :]
    cos, sin = jnp.cos(ang), jnp.sin(ang)
    cos32 = jnp.concatenate([cos, cos], axis=1)
    sin32 = jnp.concatenate([-sin, sin], axis=1)
    cos_k, sin_k = jnp.tile(cos32, (1, LANES // MLA_ROPE)), jnp.tile(sin32, (1, LANES // MLA_ROPE))
    cos_q, sin_q = jnp.tile(cos32, (1, H)).T, jnp.tile(sin32, (1, H)).T

    tm = 512
    tk = MLA_TK
    tok = lambda w_: pl.BlockSpec((1, tm, w_), lambda b, i: (b, i, 0))
    tokt = lambda w_: pl.BlockSpec((1, w_, tm), lambda b, i: (b, 0, i))
    scale = dq ** -0.5 * math.log2(math.e)
    qn, qr, kn, kr, v, z = pl.pallas_call(
        functools.partial(_mla_pre_kernel, scale=scale),
        out_shape=[jax.ShapeDtypeStruct((bsz, nope, L), BF16),
                   jax.ShapeDtypeStruct((bsz, rope, L), BF16),
                   jax.ShapeDtypeStruct((bsz, L, nope), BF16),
                   jax.ShapeDtypeStruct((bsz, L, LANES), BF16),
                   jax.ShapeDtypeStruct((bsz, L // tk, vw, tk), BF16),
                   jax.ShapeDtypeStruct((bsz, L, vw), F32)],
        grid=(bsz, L // tm),
        in_specs=[tok(d), _full((1, d)), _full(w1.shape), _full((1, MLA_Q_RANK)), _full((1, MLA_KV_RANK)),
                  _full(wqt.shape), _full(wkn.shape), _full(wvt.shape),
                  pl.BlockSpec((rope, tm), lambda b, i: (0, i)), pl.BlockSpec((rope, tm), lambda b, i: (0, i)),
                  pl.BlockSpec((tm, LANES), lambda b, i: (i, 0)), pl.BlockSpec((tm, LANES), lambda b, i: (i, 0))],
        out_specs=[tokt(nope), tokt(rope), tok(nope), tok(LANES),
                   pl.BlockSpec((1, tm // tk, vw, tk), lambda b, i: (b, i, 0, 0)), tok(vw)],
        compiler_params=_cparams(("parallel", "parallel")),
        name="mla_pre",
    )(x, pre_g.reshape(1, d), w1, q_norm.reshape(1, -1), kv_norm.reshape(1, -1), wqt, wkn, wvt,
      cos_q, sin_q, cos_k, sin_k)

    tq = MLA_TQ
    npairs = H // 2
    qspec = lambda w_: pl.BlockSpec((1, w_, tq), lambda b, i: (b, 0, i))
    kspec = lambda w_: pl.BlockSpec((1, L, w_), lambda b, i: (b, 0, 0))
    o = pl.pallas_call(
        _mla_attn_kernel,
        out_shape=jax.ShapeDtypeStruct((bsz, L, vw), F32),
        grid=(bsz, L // tq),
        in_specs=[qspec(nope), qspec(rope), kspec(nope), kspec(LANES),
                  pl.BlockSpec((1, L // tk, vw, tk), lambda b, i: (b, 0, 0, 0))],
        out_specs=pl.BlockSpec((1, tq, vw), lambda b, i: (b, i, 0)),
        scratch_shapes=[pltpu.VMEM((npairs, 2 * LANES, 2 * tq), BF16),
                        pltpu.VMEM((npairs, LANES + 16, 2 * tq), F32),
                        pltpu.VMEM((npairs, 1, 2 * tq), F32)],
        compiler_params=_cparams(("parallel", "arbitrary")),
        name="mla_attn",
    )(qn, qr, kn, kr, v)
    return _post(o, z, x, w_out.astype(BF16), post_g, 512, "mla_post")


def _sgu_kernel(x_ref, g_ref, w_ref, lng_ref, lnb_ref, ws_ref, bs_ref, wo_ref, pg_ref, out_ref, s_ref):
    width = wo_ref.shape[0]
    tm = x_ref.shape[1]
    lane = lax.broadcasted_iota(jnp.int32, (1, LANES), 1)
    lo = lane < HALF
    x = x_ref[0]
    hb = _rms(x, g_ref[...]).astype(BF16)
    v = jax.nn.gelu(_dot(hb, w_ref[:, width:2 * width]))
    mu = jnp.mean(v, axis=-1, keepdims=True)
    vc = v - mu
    var = jnp.mean(vc * vc, axis=-1, keepdims=True)
    vb = (vc * lax.rsqrt(var + EPS) * lng_ref[...] + lnb_ref[...]).astype(BF16)
    for c in range(tm // SGU_CHUNK):
        for jj in range(width // LANES):
            blk = vb[c * SGU_CHUNK:(c + 1) * SGU_CHUNK, jj * LANES:(jj + 1) * LANES]
            r = _dot(ws_ref[jj], blk)
            s_ref[c * SGU_CHUNK:(c + 1) * SGU_CHUNK, jj * LANES:(jj + 1) * LANES] = (
                jnp.where(lo, r[:SGU_CHUNK], r[SGU_CHUNK:]) + bs_ref[jj])
    u = jax.nn.gelu(_dot(hb, w_ref[:, :width]))
    z = _dot(hb, w_ref[:, 2 * width:])
    o = u * s_ref[...] * jax.nn.silu(z)
    r = _dot(o.astype(BF16), wo_ref[...])
    out_ref[0] = x + _rms(r, pg_ref[...])


def _sgu_layer(x, pre_g, post_g, w_in, ln_g, ln_b, w_s, b_s, w_out):
    bsz, L, d = x.shape
    width = w_out.shape[0]
    T = SGU_CHUNK
    gd = width // SGU_GROUPS
    tril = jnp.tril(jnp.ones((T, T), dtype=bool))
    ws = jnp.where(tril[None], w_s, 0.0).reshape(SGU_GROUPS // 2, 2 * T, T).astype(BF16)
    bs = jnp.repeat(b_s.astype(F32).T, gd, axis=1)
    bs = bs.reshape(T, width // LANES, LANES).transpose(1, 0, 2)
    tm = 512
    return pl.pallas_call(
        _sgu_kernel,
        out_shape=jax.ShapeDtypeStruct(x.shape, x.dtype),
        grid=(bsz, L // tm),
        in_specs=[pl.BlockSpec((1, tm, d), lambda b, i: (b, i, 0)),
                  _full((1, d)), _full(w_in.shape), _full((1, width)), _full((1, width)),
                  _full(ws.shape), _full(bs.shape), _full(w_out.shape), _full((1, d))],
        out_specs=pl.BlockSpec((1, tm, d), lambda b, i: (b, i, 0)),
        scratch_shapes=[pltpu.VMEM((tm, width), F32)],
        compiler_params=_cparams(("parallel", "parallel")),
        name="sgu",
    )(x, pre_g.reshape(1, d), w_in.astype(BF16), ln_g.reshape(1, width), ln_b.reshape(1, width),
      ws, bs, w_out.astype(BF16), post_g.reshape(1, d))


def kernel(x, pre_norm, post_norm, rel_bias, a_w_in, a_lam_re, a_lam_im, a_log_dt, a_b_re, a_b_im, a_c_re, a_c_im, a_d, a_w_glu, a_b_glu, a_w_out, b_w_in, b_sinks, b_w_out, c_w_in, c_q_norm, c_kv_norm, c_w_uq, c_w_ukv, c_w_out, d_w_in, d_ln_g, d_ln_b, d_w_s, d_b_s, d_w_out):
    depth = pre_norm.shape[0]
    for i in range(depth):
        kind, j = i % 4, i // 4
        if kind == 0:
            x = _s5_layer(x, pre_norm[i], post_norm[i], a_w_in[j], a_lam_re[j], a_lam_im[j], a_log_dt[j],
                          a_b_re[j], a_b_im[j], a_c_re[j], a_c_im[j], a_d[j], a_w_glu[j], a_b_glu[j],
                          a_w_out[j])
        elif kind == 1:
            x = _swa_layer(x, pre_norm[i], post_norm[i], b_w_in[j], b_sinks[j], b_w_out[j], rel_bias)
        elif kind == 2:
            x = _mla_layer(x, pre_norm[i], post_norm[i], c_w_in[j], c_q_norm[j], c_kv_norm[j], c_w_uq[j],
                           c_w_ukv[j], c_w_out[j])
        else:
            x = _sgu_layer(x, pre_norm[i], post_norm[i], d_w_in[j], d_ln_g[j], d_ln_b[j], d_w_s[j],
                           d_b_s[j], d_w_out[j])
    return x
```

```python
import functools
import math

import jax
import jax.numpy as jnp
import numpy as np
from jax import lax
from jax.experimental import pallas as pl
from jax.experimental.pallas import tpu as pltpu

F32 = jnp.float32
BF16 = jnp.bfloat16

D_MODEL = 1024
EPS = 1e-6
NEG_INF = -1e30
LANES = 128
HALF = LANES // 2

SSM_GROUP = 16
SSM_STATE = 64
S5_CH_BLOCK = LANES
S5_GROUPS_PER_BLOCK = S5_CH_BLOCK // SSM_GROUP
S5_STATE_BLOCK = S5_GROUPS_PER_BLOCK * SSM_STATE
S5_T = 64

HEAD_DIM = 64
SWA_HEADS = 16
SWA_KV_HEADS = 2
SWA_GROUP = SWA_HEADS // SWA_KV_HEADS
WINDOW = 128
SWA_WINDOWS_PER_STEP = 4
SWA_LOOKAHEAD = 2
SWA_UNIT_HEADS = 8
REL_BUCKETS = 32
REL_MAX_DIST = 128

MLA_HEADS = 16
MLA_NOPE = 64
MLA_ROPE = 32
MLA_V = 64
MLA_KV_RANK = 256
MLA_Q_RANK = 768
ROPE_BASE = 10000.0
MLA_TQ = 256
MLA_TK = 256
MLA_LOOKAHEAD = 4

SGU_CHUNK = 128
SGU_GROUPS = 16

VMEM_LIMIT = 56 * 1024 * 1024


def _cparams(sem):
    return pltpu.CompilerParams(dimension_semantics=sem, vmem_limit_bytes=VMEM_LIMIT)


def _rms(x, g):
    return x * lax.rsqrt(jnp.mean(x * x, axis=-1, keepdims=True) + EPS) * g


def _dot(a, b):
    return jnp.dot(a, b, preferred_element_type=F32)


def _dot_nt(a, b):
    return lax.dot_general(a, b, (((1,), (1,)), ((), ())), preferred_element_type=F32)


def _full(shape):
    n = len(shape)
    return pl.BlockSpec(shape, lambda *_: (0,) * n)


def _s5_pre_kernel(x_ref, g_ref, w_ref, perm_ref, u_ref, z_ref, *, tt):
    bsz = x_ref.shape[0]
    width = u_ref.shape[1]
    x = x_ref[...].reshape(bsz * tt, x_ref.shape[2])
    hb = _rms(x, g_ref[...]).astype(BF16)
    hb = _dot(perm_ref[...], hb).astype(BF16)
    u_ref[...] = _dot(hb, w_ref[:, :width])
    z_ref[...] = _dot(hb, w_ref[:, width:])


def _s5_scan_kernel(u_ref, bb_ref, cc_ref, ar_ref, ai_ref, d_ref, y_ref, s_ref, carry_ref, *, tsteps):
    nblk = bb_ref.shape[0]
    sb = S5_STATE_BLOCK
    rows = carry_ref.shape[1]

    @pl.when(pl.program_id(0) == 0)
    def _():
        carry_ref[...] = jnp.zeros_like(carry_ref)

    def project_in(i):
        s_ref[i % 2] = _dot(u_ref[:, i * LANES:(i + 1) * LANES].astype(BF16), bb_ref[i])

    project_in(0)
    for i in range(nblk):
        if i + 1 < nblk:
            project_in(i + 1)
        buf = s_ref.at[i % 2]
        ar = ar_ref[i]
        ai = ai_ref[i]
        sr = carry_ref[i, :, 0:sb]
        si = carry_ref[i, :, sb:2 * sb]
        for t in range(tsteps):
            r0 = t * rows
            nr = ar * sr - ai * si + buf[r0:r0 + rows, 0:sb]
            ni = ar * si + ai * sr + buf[r0:r0 + rows, sb:2 * sb]
            buf[r0:r0 + rows, 0:sb] = nr
            buf[r0:r0 + rows, sb:2 * sb] = ni
            sr, si = nr, ni
        carry_ref[i, :, 0:sb] = sr
        carry_ref[i, :, sb:2 * sb] = si
        ub = u_ref[:, i * LANES:(i + 1) * LANES]
        y = _dot(buf[...].astype(BF16), cc_ref[i]) + d_ref[:, i * LANES:(i + 1) * LANES] * ub
        y_ref[:, i * LANES:(i + 1) * LANES] = jax.nn.gelu(y)


def _s5_post_kernel(y_ref, z_ref, x_ref, wg_ref, bg_ref, wo_ref, g_ref, perm_ref, out_ref, *, tt):
    bsz = x_ref.shape[0]
    y = y_ref[...]
    gate = jax.nn.sigmoid(_dot(y.astype(BF16), wg_ref[...]) + bg_ref[...])
    o = y * gate * jax.nn.silu(z_ref[...])
    ob = _dot(perm_ref[...], o.astype(BF16)).astype(BF16)
    r = _dot(ob, wo_ref[...])
    x = x_ref[...].reshape(bsz * tt, x_ref.shape[2])
    out_ref[...] = (x + _rms(r, g_ref[...])).reshape(out_ref.shape)


def _s5_discretize(lam_re, lam_im, log_dt, b_re, b_im):
    dt = jnp.exp(log_dt)[:, None]
    mag = jnp.exp(lam_re * dt)
    ab_re = mag * jnp.cos(lam_im * dt)
    ab_im = mag * jnp.sin(lam_im * dt)
    den = lam_re * lam_re + lam_im * lam_im
    nr = ab_re - 1.0
    f_re = (nr * lam_re + ab_im * lam_im) / den
    f_im = (ab_im * lam_re - nr * lam_im) / den
    bb_re = f_re[..., None] * b_re - f_im[..., None] * b_im
    bb_im = f_re[..., None] * b_im + f_im[..., None] * b_re
    return ab_re, ab_im, bb_re, bb_im


def _s5_layer(x, pre_g, post_g, w_in, lam_re, lam_im, log_dt, b_re, b_im, c_re, c_im, d_skip,
              w_glu, b_glu, w_out):
    bsz, L, d = x.shape
    width = w_in.shape[1] // 2
    nblk = width // S5_CH_BLOCK
    gpb = S5_GROUPS_PER_BLOCK
    tt = S5_T
    rows = bsz * tt

    src = (np.arange(rows) % bsz) * tt + np.arange(rows) // bsz
    perm_np = np.zeros((rows, rows), np.float32)
    perm_np[np.arange(rows), src] = 1.0
    perm = jnp.asarray(perm_np, BF16)
    perm_t = jnp.asarray(perm_np.T, BF16)

    u, z = pl.pallas_call(
        functools.partial(_s5_pre_kernel, tt=tt),
        out_shape=[jax.ShapeDtypeStruct((L * bsz, width), F32)] * 2,
        grid=(L // tt,),
        in_specs=[pl.BlockSpec((bsz, tt, d), lambda i: (0, i, 0)),
                  _full((1, d)), _full(w_in.shape), _full(perm.shape)],
        out_specs=[pl.BlockSpec((rows, width), lambda i: (i, 0))] * 2,
        compiler_params=_cparams(("parallel",)),
        name="s5_pre",
    )(x, pre_g.reshape(1, d), w_in.astype(BF16), perm)

    ab_re, ab_im, bb_re, bb_im = _s5_discretize(lam_re, lam_im, log_dt, b_re, b_im)
    eye = jnp.eye(gpb, dtype=F32)

    def pack_b(bb):
        t = bb.reshape(nblk, gpb, SSM_STATE, SSM_GROUP)
        return jnp.einsum('igph,gk->ikhgp', t, eye).reshape(nblk, S5_CH_BLOCK, S5_STATE_BLOCK)

    def pack_c(cc):
        t = cc.reshape(nblk, gpb, SSM_GROUP, SSM_STATE)
        return jnp.einsum('ighp,gk->igpkh', t, eye).reshape(nblk, S5_STATE_BLOCK, S5_CH_BLOCK)

    bb = jnp.concatenate([pack_b(bb_re), pack_b(bb_im)], axis=2).astype(BF16)
    cc = jnp.concatenate([pack_c(c_re), -pack_c(c_im)], axis=1).astype(BF16)
    ar = jnp.broadcast_to(ab_re.reshape(nblk, 1, S5_STATE_BLOCK), (nblk, bsz, S5_STATE_BLOCK))
    ai = jnp.broadcast_to(ab_im.reshape(nblk, 1, S5_STATE_BLOCK), (nblk, bsz, S5_STATE_BLOCK))

    y = pl.pallas_call(
        functools.partial(_s5_scan_kernel, tsteps=tt),
        out_shape=jax.ShapeDtypeStruct((L * bsz, width), F32),
        grid=(L // tt,),
        in_specs=[pl.BlockSpec((rows, width), lambda i: (i, 0)),
                  _full(bb.shape), _full(cc.shape), _full(ar.shape), _full(ai.shape),
                  _full((1, width))],
        out_specs=pl.BlockSpec((rows, width), lambda i: (i, 0)),
        scratch_shapes=[pltpu.VMEM((2, rows, 2 * S5_STATE_BLOCK), F32),
                        pltpu.VMEM((nblk, bsz, 2 * S5_STATE_BLOCK), F32)],
        compiler_params=_cparams(("arbitrary",)),
        name="s5_scan",
    )(u, bb, cc, ar, ai, d_skip.reshape(1, width))

    return pl.pallas_call(
        functools.partial(_s5_post_kernel, tt=tt),
        out_shape=jax.ShapeDtypeStruct(x.shape, x.dtype),
        grid=(L // tt,),
        in_specs=[pl.BlockSpec((rows, width), lambda i: (i, 0)),
                  pl.BlockSpec((rows, width), lambda i: (i, 0)),
                  pl.BlockSpec((bsz, tt, d), lambda i: (0, i, 0)),
                  _full(w_glu.shape), _full((1, width)), _full(w_out.shape), _full((1, d)),
                  _full(perm_t.shape)],
        out_specs=pl.BlockSpec((bsz, tt, d), lambda i: (0, i, 0)),
        compiler_params=_cparams(("parallel",)),
        name="s5_post",
    )(y, z, x, w_glu.astype(BF16), b_glu.reshape(1, width),
      w_out.astype(BF16), post_g.reshape(1, d), perm_t)


def _swa_bias(rel_bias):
    W = WINDOW
    n = 4 * W
    dist = 2 * W - jnp.arange(n)
    valid = jnp.logical_and(dist >= 0, dist < W)
    dpos = jnp.maximum(dist, 0)
    max_exact = REL_BUCKETS // 2
    dist_f = jnp.maximum(dpos, 1).astype(F32)
    large = max_exact + (jnp.log(dist_f / max_exact) / math.log(REL_MAX_DIST / max_exact)
                         * (REL_BUCKETS - max_exact)).astype(jnp.int32)
    large = jnp.minimum(large, REL_BUCKETS - 1)
    bucket = jnp.where(dpos < max_exact, dpos, large)
    vec = jnp.where(valid[:, None], rel_bias[bucket].astype(F32), NEG_INF).T
    skew = jnp.tile(vec, (1, W))[:, :W * (n - 1)].reshape(vec.shape[0], W, n - 1)
    return skew[:, :, W:3 * W]


def _swa_pre_kernel(x_ref, g_ref, wqt_ref, wk_ref, wvt_ref, wz_ref, qt_ref, k_ref, vt_ref, z_ref, *, scale):
    hb = _rms(x_ref[0], g_ref[...]).astype(BF16)
    qt_ref[0] = (_dot_nt(wqt_ref[...], hb) * scale).astype(BF16)
    k_ref[0] = _dot(hb, wk_ref[...]).astype(BF16)
    vt_ref[0] = _dot_nt(wvt_ref[...], hb).astype(BF16)
    z_ref[0] = _dot(hb, wz_ref[...])


def _swa_kernel(qt_ref, kp_ref, kc_ref, vtp_ref, vtc_ref, bias_ref, sink_ref, z_ref, x_ref, wo_ref, g_ref,
                out_ref, ot_ref):
    W = WINDOW
    nwin = qt_ref.shape[2] // W
    step = pl.program_id(1)
    kall = jnp.concatenate([kp_ref[0], kc_ref[0]], axis=0)
    vtall = jnp.concatenate([vtp_ref[0], vtc_ref[0]], axis=1)
    nsub = SWA_UNIT_HEADS
    zq = jnp.zeros((HEAD_DIM, nsub * W), BF16)
    ones = jnp.ones((16, 2 * W), BF16)
    units = [(w, h, c) for w in range(nwin) for h in range(SWA_KV_HEADS) for c in range(SWA_GROUP // nsub)]

    def scores(w, h, c):
        hd0 = h * SWA_GROUP + c * nsub
        qh = jnp.concatenate([qt_ref[0, (hd0 + g) * HEAD_DIM:(hd0 + g + 1) * HEAD_DIM, w * W:(w + 1) * W]
                              for g in range(nsub)], axis=1)
        qz = jnp.concatenate([qh, zq] if h == 0 else [zq, qh], axis=0)
        return _dot(kall[w * W:(w + 2) * W], qz)

    pending = [scores(*u) for u in units[:SWA_LOOKAHEAD]]
    for idx, (w, h, c) in enumerate(units):
        raw = pending.pop(0)
        if idx + SWA_LOOKAHEAD < len(units):
            pending.append(scores(*units[idx + SWA_LOOKAHEAD]))
        cols = slice(c * nsub * W, (c + 1) * nsub * W)
        variant = (step == 0).astype(jnp.int32) if w == 0 else 0
        s = raw + bias_ref[variant, h, :, cols]
        sink = sink_ref[h, :, cols]
        m = jnp.maximum(jnp.max(s, axis=0, keepdims=True), sink)
        p = jnp.exp2(s - m).astype(BF16)
        vones = jnp.concatenate([vtall[h * HEAD_DIM:(h + 1) * HEAD_DIM, w * W:(w + 2) * W], ones], axis=0)
        o = _dot(vones, p)
        denom = o[HEAD_DIM:HEAD_DIM + 1] + jnp.exp2(sink - m)
        oh = o[:HEAD_DIM] * (1.0 / denom)
        for g in range(nsub):
            hd = h * SWA_GROUP + c * nsub + g
            ot_ref[hd * HEAD_DIM:(hd + 1) * HEAD_DIM, w * W:(w + 1) * W] = oh[:, g * W:(g + 1) * W]
    gated = ot_ref[...].T * jax.nn.silu(z_ref[0])
    r = _dot(gated.astype(BF16), wo_ref[...])
    out_ref[0] = x_ref[0] + _rms(r, g_ref[...])


def _swa_layer(x, pre_g, post_g, w_in, sinks, w_out, rel_bias):
    bsz, L, d = x.shape
    width = SWA_HEADS * HEAD_DIM
    kvw = SWA_KV_HEADS * HEAD_DIM
    W = WINDOW
    nb = L // W
    log2e = math.log2(math.e)
    tm = 512
    tok = lambda w_: pl.BlockSpec((1, tm, w_), lambda b, i: (b, i, 0))
    tokt = lambda w_: pl.BlockSpec((1, w_, tm), lambda b, i: (b, 0, i))
    wqt = w_in[:, :width].T.astype(BF16)
    wk = w_in[:, width:width + kvw].astype(BF16)
    wvt = w_in[:, width + kvw:width + 2 * kvw].T.astype(BF16)
    wz = w_in[:, width + 2 * kvw:].astype(BF16)
    qt, k, vt, z = pl.pallas_call(
        functools.partial(_swa_pre_kernel, scale=HEAD_DIM ** -0.5 * log2e),
        out_shape=[jax.ShapeDtypeStruct((bsz, width, L), BF16),
                   jax.ShapeDtypeStruct((bsz, L, kvw), BF16),
                   jax.ShapeDtypeStruct((bsz, kvw, L), BF16),
                   jax.ShapeDtypeStruct((bsz, L, width), F32)],
        grid=(bsz, L // tm),
        in_specs=[tok(d), _full((1, d)), _full(wqt.shape), _full(wk.shape), _full(wvt.shape), _full(wz.shape)],
        out_specs=[tokt(width), tok(kvw), tokt(kvw), tok(width)],
        compiler_params=_cparams(("parallel", "parallel")),
        name="swa_pre",
    )(x, pre_g.reshape(1, d), wqt, wk, wvt, wz)

    bias = jnp.transpose(_swa_bias(rel_bias.astype(F32) * log2e), (0, 2, 1))
    has_prev = (jnp.arange(2 * W) >= W)[None, :, None]
    variants = [bias,
                jnp.where(has_prev, bias, NEG_INF)]
    bias_t = jnp.stack([v.reshape(SWA_KV_HEADS, SWA_GROUP, 2 * W, W).transpose(0, 2, 1, 3)
                        .reshape(SWA_KV_HEADS, 2 * W, SWA_GROUP * W) for v in variants])
    sink = jnp.repeat(sinks.astype(F32) * log2e, W).reshape(SWA_KV_HEADS, 1, SWA_GROUP * W)

    nwin = SWA_WINDOWS_PER_STEP
    tq = nwin * W
    prev = lambda n: jnp.maximum(n * nwin - 1, 0)
    return pl.pallas_call(
        _swa_kernel,
        out_shape=jax.ShapeDtypeStruct(x.shape, x.dtype),
        grid=(bsz, L // tq),
        in_specs=[pl.BlockSpec((1, width, tq), lambda b, n: (b, 0, n)),
                  pl.BlockSpec((1, W, kvw), lambda b, n: (b, prev(n), 0)),
                  pl.BlockSpec((1, tq, kvw), lambda b, n: (b, n, 0)),
                  pl.BlockSpec((1, kvw, W), lambda b, n: (b, 0, prev(n))),
                  pl.BlockSpec((1, kvw, tq), lambda b, n: (b, 0, n)),
                  _full(bias_t.shape), _full(sink.shape),
                  pl.BlockSpec((1, tq, width), lambda b, n: (b, n, 0)),
                  pl.BlockSpec((1, tq, d), lambda b, n: (b, n, 0)),
                  _full(w_out.shape), _full((1, d))],
        out_specs=pl.BlockSpec((1, tq, d), lambda b, n: (b, n, 0)),
        scratch_shapes=[pltpu.VMEM((width, tq), F32)],
        compiler_params=_cparams(("parallel", "parallel")),
        name="swa_attn",
    )(qt, k, k, vt, vt, bias_t, sink, z, x, w_out.astype(BF16), post_g.reshape(1, d))


def _mla_pre_kernel(x_ref, g_ref, w_ref, qn_ref, kvn_ref, wq_ref, wkv_ref, wvt_ref, cq_ref, sq_ref, ck_ref, sk_ref,
                    oqn_ref, oqr_ref, okn_ref, okr_ref, ov_ref, oz_ref, *, scale):
    nope = MLA_HEADS * MLA_NOPE
    rope = MLA_HEADS * MLA_ROPE
    vw = MLA_HEADS * MLA_V
    hb = _rms(x_ref[0], g_ref[...]).astype(BF16)
    o1 = MLA_Q_RANK
    o2 = o1 + MLA_KV_RANK
    o3 = o2 + vw
    cq = _dot(hb, w_ref[:, :o1])
    ckv = _dot(hb, w_ref[:, o1:o2])
    oz_ref[0] = _dot(hb, w_ref[:, o2:o3])
    kr = _dot(hb, w_ref[:, o3:o3 + LANES])
    krs = _dot(hb, w_ref[:, o3 + LANES:o3 + 2 * LANES])
    okr_ref[0] = (kr * ck_ref[...] + krs * sk_ref[...]).astype(BF16)
    cqb = _rms(cq, qn_ref[...]).astype(BF16)
    oqn_ref[0] = (_dot_nt(wq_ref[:nope], cqb) * scale).astype(BF16)
    qr = _dot_nt(wq_ref[nope:nope + rope], cqb)
    qrs = _dot_nt(wq_ref[nope + rope:nope + 2 * rope], cqb)
    oqr_ref[0] = ((qr * cq_ref[...] + qrs * sq_ref[...]) * scale).astype(BF16)
    ckb = _rms(ckv, kvn_ref[...]).astype(BF16)
    okn_ref[0] = _dot(ckb, wkv_ref[:, :nope]).astype(BF16)
    vt = _dot_nt(wvt_ref[...], ckb).astype(BF16)
    tk = ov_ref.shape[3]
    for c in range(ov_ref.shape[1]):
        ov_ref[0, c] = vt[:, c * tk:(c + 1) * tk]


def _mla_attn_kernel(qn_ref, qr_ref, kn_ref, kr_ref, v_ref, z_ref, x_ref, wo_ref, g_ref, out_ref,
                     qs_ref, acc_ref, m_ref, o_ref):
    tq = qn_ref.shape[2]
    tk = v_ref.shape[3]
    npairs = MLA_HEADS // 2
    i = pl.program_id(1)
    krow = lax.broadcasted_iota(jnp.int32, (tk, 2 * tq), 0)
    qcol = lax.broadcasted_iota(jnp.int32, (tk, 2 * tq), 1)
    causal = krow <= jnp.where(qcol >= tq, qcol - tq, qcol)

    zn = jnp.zeros((MLA_NOPE, tq), BF16)
    zr = jnp.zeros((LANES - MLA_ROPE, tq), BF16)
    for p in range(npairs):
        qn = qn_ref[0, p * LANES:(p + 1) * LANES, :]
        r0 = 2 * p * MLA_ROPE
        c0 = jnp.concatenate([qn[:MLA_NOPE], zn, qr_ref[0, r0:r0 + MLA_ROPE, :], zr], axis=0)
        c1 = jnp.concatenate([zn, qn[MLA_NOPE:], qr_ref[0, r0 + MLA_ROPE:r0 + 2 * MLA_ROPE, :], zr], axis=0)
        qs_ref[p] = jnp.concatenate([c0, c1], axis=1)

    m_ref[...] = jnp.full(m_ref.shape, NEG_INF, F32)
    acc_ref[...] = jnp.zeros(acc_ref.shape, F32)
    ones = jnp.ones((acc_ref.shape[1] - LANES, tk), BF16)

    def kv_step(j, masked):
        ks = pl.multiple_of(j * tk, tk)
        kr = kr_ref[0, pl.ds(ks, tk), :]

        def scores(p):
            kc = jnp.concatenate([kn_ref[0, pl.ds(ks, tk), p * LANES:(p + 1) * LANES], kr], axis=1)
            return _dot(kc, qs_ref[p])

        pending = [scores(p) for p in range(MLA_LOOKAHEAD)]
        for p in range(npairs):
            s = pending.pop(0)
            if p + MLA_LOOKAHEAD < npairs:
                pending.append(scores(p + MLA_LOOKAHEAD))
            probs, alphas = [], []
            for c in range(2 * tq // LANES):
                sc = s[:, c * LANES:(c + 1) * LANES]
                if masked:
                    sc = jnp.where(causal[:, c * LANES:(c + 1) * LANES], sc, NEG_INF)
                m_prev = m_ref[p, :, c * LANES:(c + 1) * LANES]
                m_new = jnp.maximum(m_prev, jnp.max(sc, axis=0, keepdims=True))
                alphas.append(jnp.exp2(m_prev - m_new))
                probs.append(jnp.exp2(sc - m_new).astype(BF16))
                m_ref[p, :, c * LANES:(c + 1) * LANES] = m_new
            alpha = jnp.concatenate(alphas, axis=1)
            vones = jnp.concatenate([v_ref[0, j, p * LANES:(p + 1) * LANES, :], ones], axis=0)
            pv = _dot(vones, jnp.concatenate(probs, axis=1))
            acc_ref[p] = alpha * acc_ref[p] + pv

    def body(j, c):
        kv_step(j, False)
        return c

    lax.fori_loop(0, i, body, 0)
    kv_step(i, True)
    for p in range(npairs):
        a = acc_ref[p]
        a = a[:LANES] * (1.0 / a[LANES:LANES + 1])
        ot = jnp.concatenate([a[:MLA_V, :tq], a[MLA_V:, tq:]], axis=0)
        o_ref[:, p * LANES:(p + 1) * LANES] = ot.T
    gated = o_ref[...] * jax.nn.silu(z_ref[0])
    r = _dot(gated.astype(BF16), wo_ref[...])
    out_ref[0] = x_ref[0] + _rms(r, g_ref[...])


def _mla_layer(x, pre_g, post_g, w_in, q_norm, kv_norm, w_uq, w_ukv, w_out):
    bsz, L, d = x.shape
    H = MLA_HEADS
    dq = MLA_NOPE + MLA_ROPE
    nope = H * MLA_NOPE
    rope = H * MLA_ROPE
    vw = H * MLA_V
    half = MLA_ROPE // 2
    o_kr = MLA_Q_RANK + MLA_KV_RANK
    o_z = o_kr + MLA_ROPE
    w_kr = w_in[:, o_kr:o_z]
    w_krs = jnp.concatenate([w_kr[:, half:], w_kr[:, :half]], axis=1)
    reps = LANES // MLA_ROPE
    w1 = jnp.concatenate([w_in[:, :o_kr], w_in[:, o_z:]] + [w_kr] * reps + [w_krs] * reps, axis=1).astype(BF16)
    wq3 = w_uq.reshape(MLA_Q_RANK, H, dq)
    wq_r = wq3[:, :, MLA_NOPE:]
    wq_rs = jnp.concatenate([wq_r[:, :, half:], wq_r[:, :, :half]], axis=2)
    wqt = jnp.concatenate([wq3[:, :, :MLA_NOPE].reshape(MLA_Q_RANK, nope), wq_r.reshape(MLA_Q_RANK, rope),
                           wq_rs.reshape(MLA_Q_RANK, rope)], axis=1).T.astype(BF16)
    wkv3 = w_ukv.reshape(MLA_KV_RANK, H, MLA_NOPE + MLA_V)
    wkn = wkv3[:, :, :MLA_NOPE].reshape(MLA_KV_RANK, nope).astype(BF16)
    wvt = wkv3[:, :, MLA_NOPE:].reshape(MLA_KV_RANK, vw).T.astype(BF16)
    inv = ROPE_BASE ** (-jnp.arange(0, MLA_ROPE, 2, dtype=F32) / MLA_ROPE)
    ang = jnp.arange(L, dtype=F32)[:, None] * inv[None, :]
    cos, sin = jnp.cos(ang), jnp.sin(ang)
    cos32 = jnp.concatenate([cos, cos], axis=1)
    sin32 = jnp.concatenate([-sin, sin], axis=1)
    cos_k, sin_k = jnp.tile(cos32, (1, LANES // MLA_ROPE)), jnp.tile(sin32, (1, LANES // MLA_ROPE))
    cos_q, sin_q = jnp.tile(cos32, (1, H)).T, jnp.tile(sin32, (1, H)).T

    tm = 512
    tk = MLA_TK
    tok = lambda w_: pl.BlockSpec((1, tm, w_), lambda b, i: (b, i, 0))
    tokt = lambda w_: pl.BlockSpec((1, w_, tm), lambda b, i: (b, 0, i))
    scale = dq ** -0.5 * math.log2(math.e)
    qn, qr, kn, kr, v, z = pl.pallas_call(
        functools.partial(_mla_pre_kernel, scale=scale),
        out_shape=[jax.ShapeDtypeStruct((bsz, nope, L), BF16),
                   jax.ShapeDtypeStruct((bsz, rope, L), BF16),
                   jax.ShapeDtypeStruct((bsz, L, nope), BF16),
                   jax.ShapeDtypeStruct((bsz, L, LANES), BF16),
                   jax.ShapeDtypeStruct((bsz, L // tk, vw, tk), BF16),
                   jax.ShapeDtypeStruct((bsz, L, vw), F32)],
        grid=(bsz, L // tm),
        in_specs=[tok(d), _full((1, d)), _full(w1.shape), _full((1, MLA_Q_RANK)), _full((1, MLA_KV_RANK)),
                  _full(wqt.shape), _full(wkn.shape), _full(wvt.shape),
                  pl.BlockSpec((rope, tm), lambda b, i: (0, i)), pl.BlockSpec((rope, tm), lambda b, i: (0, i)),
                  pl.BlockSpec((tm, LANES), lambda b, i: (i, 0)), pl.BlockSpec((tm, LANES), lambda b, i: (i, 0))],
        out_specs=[tokt(nope), tokt(rope), tok(nope), tok(LANES),
                   pl.BlockSpec((1, tm // tk, vw, tk), lambda b, i: (b, i, 0, 0)), tok(vw)],
        compiler_params=_cparams(("parallel", "parallel")),
        name="mla_pre",
    )(x, pre_g.reshape(1, d), w1, q_norm.reshape(1, -1), kv_norm.reshape(1, -1), wqt, wkn, wvt,
      cos_q, sin_q, cos_k, sin_k)

    tq = MLA_TQ
    npairs = H // 2
    qspec = lambda w_: pl.BlockSpec((1, w_, tq), lambda b, i: (b, 0, i))
    kspec = lambda w_: pl.BlockSpec((1, L, w_), lambda b, i: (b, 0, 0))
    rowspec = lambda w_: pl.BlockSpec((1, tq, w_), lambda b, i: (b, i, 0))
    return pl.pallas_call(
        _mla_attn_kernel,
        out_shape=jax.ShapeDtypeStruct(x.shape, x.dtype),
        grid=(bsz, L // tq),
        in_specs=[qspec(nope), qspec(rope), kspec(nope), kspec(LANES),
                  pl.BlockSpec((1, L // tk, vw, tk), lambda b, i: (b, 0, 0, 0)),
                  rowspec(vw), rowspec(d), _full(w_out.shape), _full((1, d))],
        out_specs=rowspec(d),
        scratch_shapes=[pltpu.VMEM((npairs, 2 * LANES, 2 * tq), BF16),
                        pltpu.VMEM((npairs, LANES + 16, 2 * tq), F32),
                        pltpu.VMEM((npairs, 1, 2 * tq), F32),
                        pltpu.VMEM((tq, vw), F32)],
        compiler_params=_cparams(("parallel", "arbitrary")),
        name="mla_attn",
    )(qn, qr, kn, kr, v, z, x, w_out.astype(BF16), post_g.reshape(1, d))


def _sgu_kernel(x_ref, g_ref, w_ref, lng_ref, lnb_ref, ws_ref, bs_ref, wo_ref, pg_ref, out_ref, s_ref):
    width = wo_ref.shape[0]
    tm = x_ref.shape[1]
    lane = lax.broadcasted_iota(jnp.int32, (1, LANES), 1)
    lo = lane < HALF
    x = x_ref[0]
    hb = _rms(x, g_ref[...]).astype(BF16)
    v = jax.nn.gelu(_dot(hb, w_ref[:, width:2 * width]))
    mu = jnp.mean(v, axis=-1, keepdims=True)
    vc = v - mu
    var = jnp.mean(vc * vc, axis=-1, keepdims=True)
    vb = (vc * lax.rsqrt(var + EPS) * lng_ref[...] + lnb_ref[...]).astype(BF16)
    for c in range(tm // SGU_CHUNK):
        for jj in range(width // LANES):
            blk = vb[c * SGU_CHUNK:(c + 1) * SGU_CHUNK, jj * LANES:(jj + 1) * LANES]
            r = _dot(ws_ref[jj], blk)
            s_ref[c * SGU_CHUNK:(c + 1) * SGU_CHUNK, jj * LANES:(jj + 1) * LANES] = (
                jnp.where(lo, r[:SGU_CHUNK], r[SGU_CHUNK:]) + bs_ref[jj])
    u = jax.nn.gelu(_dot(hb, w_ref[:, :width]))
    z = _dot(hb, w_ref[:, 2 * width:])
    o = u * s_ref[...] * jax.nn.silu(z)
    r = _dot(o.astype(BF16), wo_ref[...])
    out_ref[0] = x + _rms(r, pg_ref[...])


def _sgu_layer(x, pre_g, post_g, w_in, ln_g, ln_b, w_s, b_s, w_out):
    bsz, L, d = x.shape
    width = w_out.shape[0]
    T = SGU_CHUNK
    gd = width // SGU_GROUPS
    tril = jnp.tril(jnp.ones((T, T), dtype=bool))
    ws = jnp.where(tril[None], w_s, 0.0).reshape(SGU_GROUPS // 2, 2 * T, T).astype(BF16)
    bs = jnp.repeat(b_s.astype(F32).T, gd, axis=1)
    bs = bs.reshape(T, width // LANES, LANES).transpose(1, 0, 2)
    tm = 512
    return pl.pallas_call(
        _sgu_kernel,
        out_shape=jax.ShapeDtypeStruct(x.shape, x.dtype),
        grid=(bsz, L // tm),
        in_specs=[pl.BlockSpec((1, tm, d), lambda b, i: (b, i, 0)),
                  _full((1, d)), _full(w_in.shape), _full((1, width)), _full((1, width)),
                  _full(ws.shape), _full(bs.shape), _full(w_out.shape), _full((1, d))],
        out_specs=pl.BlockSpec((1, tm, d), lambda b, i: (b, i, 0)),
        scratch_shapes=[pltpu.VMEM((tm, width), F32)],
        compiler_params=_cparams(("parallel", "parallel")),
        name="sgu",
    )(x, pre_g.reshape(1, d), w_in.astype(BF16), ln_g.reshape(1, width), ln_b.reshape(1, width),
      ws, bs, w_out.astype(BF16), post_g.reshape(1, d))


def kernel(x, pre_norm, post_norm, rel_bias, a_w_in, a_lam_re, a_lam_im, a_log_dt, a_b_re, a_b_im, a_c_re, a_c_im, a_d, a_w_glu, a_b_glu, a_w_out, b_w_in, b_sinks, b_w_out, c_w_in, c_q_norm, c_kv_norm, c_w_uq, c_w_ukv, c_w_out, d_w_in, d_ln_g, d_ln_b, d_w_s, d_b_s, d_w_out):
    depth = pre_norm.shape[0]
    for i in range(depth):
        kind, j = i % 4, i // 4
        if kind == 0:
            x = _s5_layer(x, pre_norm[i], post_norm[i], a_w_in[j], a_lam_re[j], a_lam_im[j], a_log_dt[j],
                          a_b_re[j], a_b_im[j], a_c_re[j], a_c_im[j], a_d[j], a_w_glu[j], a_b_glu[j],
                          a_w_out[j])
        elif kind == 1:
            x = _swa_layer(x, pre_norm[i], post_norm[i], b_w_in[j], b_sinks[j], b_w_out[j], rel_bias)
        elif kind == 2:
            x = _mla_layer(x, pre_norm[i], post_norm[i], c_w_in[j], c_q_norm[j], c_kv_norm[j], c_w_uq[j],
                           c_w_ukv[j], c_w_out[j])
        else:
            x = _sgu_layer(x, pre_norm[i], post_norm[i], d_w_in[j], d_ln_g[j], d_ln_b[j], d_w_s[j],
                           d_b_s[j], d_w_out[j])
    return x
```

```python
import functools
import math

import jax
import jax.numpy as jnp
import numpy as np
from jax import lax
from jax.experimental import pallas as pl
from jax.experimental.pallas import tpu as pltpu

F32 = jnp.float32
BF16 = jnp.bfloat16

D_MODEL = 1024
EPS = 1e-6
NEG_INF = -1e30
LANES = 128
HALF = LANES // 2

SSM_GROUP = 16
SSM_STATE = 64
S5_CH_BLOCK = LANES
S5_GROUPS_PER_BLOCK = S5_CH_BLOCK // SSM_GROUP
S5_STATE_BLOCK = S5_GROUPS_PER_BLOCK * SSM_STATE
S5_T = 64

HEAD_DIM = 64
SWA_HEADS = 16
SWA_KV_HEADS = 2
SWA_GROUP = SWA_HEADS // SWA_KV_HEADS
WINDOW = 128
SWA_WINDOWS_PER_STEP = 4
SWA_LOOKAHEAD = 2
SWA_UNIT_HEADS = 8
REL_BUCKETS = 32
REL_MAX_DIST = 128

MLA_HEADS = 16
MLA_NOPE = 64
MLA_ROPE = 32
MLA_V = 64
MLA_KV_RANK = 256
MLA_Q_RANK = 768
ROPE_BASE = 10000.0
MLA_TQ = 256
MLA_TK = 256
MLA_LOOKAHEAD = 4

SGU_CHUNK = 128
SGU_GROUPS = 16

VMEM_LIMIT = 56 * 1024 * 1024


def _cparams(sem):
    return pltpu.CompilerParams(dimension_semantics=sem, vmem_limit_bytes=VMEM_LIMIT)


def _rms(x, g):
    return x * lax.rsqrt(jnp.mean(x * x, axis=-1, keepdims=True) + EPS) * g


def _dot(a, b):
    return jnp.dot(a, b, preferred_element_type=F32)


def _dot_nt(a, b):
    return lax.dot_general(a, b, (((1,), (1,)), ((), ())), preferred_element_type=F32)


def _full(shape):
    n = len(shape)
    return pl.BlockSpec(shape, lambda *_: (0,) * n, pipeline_mode=pl.Buffered(1))


def _s5_kernel(x_ref, g_ref, w_ref, perm_ref, permt_ref, bb_ref, cc_ref, ar_ref, ai_ref, d_ref,
               wg_ref, bg_ref, wo_ref, pg_ref, out_ref, u_ref, z_ref, y_ref, s_ref, carry_ref, *, tt):
    bsz = x_ref.shape[0]
    width = wg_ref.shape[0]
    rows = bsz * tt
    nblk = bb_ref.shape[0]
    sb = S5_STATE_BLOCK

    @pl.when(pl.program_id(0) == 0)
    def _():
        carry_ref[...] = jnp.zeros_like(carry_ref)

    x = x_ref[...].reshape(rows, x_ref.shape[2])
    hb = _rms(x, g_ref[...]).astype(BF16)
    hb = _dot(perm_ref[...], hb).astype(BF16)
    u_ref[...] = _dot(hb, w_ref[:, :width])
    z_ref[...] = _dot(hb, w_ref[:, width:])

    def project_in(i):
        s_ref[i % 2] = _dot(u_ref[:, i * LANES:(i + 1) * LANES].astype(BF16), bb_ref[i])

    project_in(0)
    for i in range(nblk):
        if i + 1 < nblk:
            project_in(i + 1)
        buf = s_ref.at[i % 2]
        ar = ar_ref[i]
        ai = ai_ref[i]
        sr = carry_ref[i, :, 0:sb]
        si = carry_ref[i, :, sb:2 * sb]
        for t in range(tt):
            r0 = t * bsz
            nr = ar * sr - ai * si + buf[r0:r0 + bsz, 0:sb]
            ni = ar * si + ai * sr + buf[r0:r0 + bsz, sb:2 * sb]
            buf[r0:r0 + bsz, 0:sb] = nr
            buf[r0:r0 + bsz, sb:2 * sb] = ni
            sr, si = nr, ni
        carry_ref[i, :, 0:sb] = sr
        carry_ref[i, :, sb:2 * sb] = si
        ub = u_ref[:, i * LANES:(i + 1) * LANES]
        y = _dot(buf[...].astype(BF16), cc_ref[i]) + d_ref[:, i * LANES:(i + 1) * LANES] * ub
        y_ref[:, i * LANES:(i + 1) * LANES] = jax.nn.gelu(y)

    y = y_ref[...]
    gate = jax.nn.sigmoid(_dot(y.astype(BF16), wg_ref[...]) + bg_ref[...])
    o = y * gate * jax.nn.silu(z_ref[...])
    ob = _dot(permt_ref[...], o.astype(BF16)).astype(BF16)
    r = _dot(ob, wo_ref[...])
    out_ref[...] = (x + _rms(r, pg_ref[...])).reshape(out_ref.shape)


def _s5_discretize(lam_re, lam_im, log_dt, b_re, b_im):
    dt = jnp.exp(log_dt)[:, None]
    mag = jnp.exp(lam_re * dt)
    ab_re = mag * jnp.cos(lam_im * dt)
    ab_im = mag * jnp.sin(lam_im * dt)
    den = lam_re * lam_re + lam_im * lam_im
    nr = ab_re - 1.0
    f_re = (nr * lam_re + ab_im * lam_im) / den
    f_im = (ab_im * lam_re - nr * lam_im) / den
    bb_re = f_re[..., None] * b_re - f_im[..., None] * b_im
    bb_im = f_re[..., None] * b_im + f_im[..., None] * b_re
    return ab_re, ab_im, bb_re, bb_im


def _s5_layer(x, pre_g, post_g, w_in, lam_re, lam_im, log_dt, b_re, b_im, c_re, c_im, d_skip,
              w_glu, b_glu, w_out):
    bsz, L, d = x.shape
    width = w_in.shape[1] // 2
    nblk = width // S5_CH_BLOCK
    gpb = S5_GROUPS_PER_BLOCK
    tt = S5_T
    rows = bsz * tt

    src = (np.arange(rows) % bsz) * tt + np.arange(rows) // bsz
    perm_np = np.zeros((rows, rows), np.float32)
    perm_np[np.arange(rows), src] = 1.0
    perm = jnp.asarray(perm_np, BF16)
    perm_t = jnp.asarray(perm_np.T, BF16)

    ab_re, ab_im, bb_re, bb_im = _s5_discretize(lam_re, lam_im, log_dt, b_re, b_im)
    eye = jnp.eye(gpb, dtype=F32)

    def pack_b(bb):
        t = bb.reshape(nblk, gpb, SSM_STATE, SSM_GROUP)
        return jnp.einsum('igph,gk->ikhgp', t, eye).reshape(nblk, S5_CH_BLOCK, S5_STATE_BLOCK)

    def pack_c(cc):
        t = cc.reshape(nblk, gpb, SSM_GROUP, SSM_STATE)
        return jnp.einsum('ighp,gk->igpkh', t, eye).reshape(nblk, S5_STATE_BLOCK, S5_CH_BLOCK)

    bb = jnp.concatenate([pack_b(bb_re), pack_b(bb_im)], axis=2).astype(BF16)
    cc = jnp.concatenate([pack_c(c_re), -pack_c(c_im)], axis=1).astype(BF16)
    ar = jnp.broadcast_to(ab_re.reshape(nblk, 1, S5_STATE_BLOCK), (nblk, bsz, S5_STATE_BLOCK))
    ai = jnp.broadcast_to(ab_im.reshape(nblk, 1, S5_STATE_BLOCK), (nblk, bsz, S5_STATE_BLOCK))

    xspec = pl.BlockSpec((bsz, tt, d), lambda i: (0, i, 0))
    return pl.pallas_call(
        functools.partial(_s5_kernel, tt=tt),
        out_shape=jax.ShapeDtypeStruct(x.shape, x.dtype),
        grid=(L // tt,),
        in_specs=[xspec, _full((1, d)), _full(w_in.shape), _full(perm.shape), _full(perm_t.shape),
                  _full(bb.shape), _full(cc.shape), _full(ar.shape), _full(ai.shape), _full((1, width)),
                  _full(w_glu.shape), _full((1, width)), _full(w_out.shape), _full((1, d))],
        out_specs=xspec,
        scratch_shapes=[pltpu.VMEM((rows, width), F32),
                        pltpu.VMEM((rows, width), F32),
                        pltpu.VMEM((rows, width), F32),
                        pltpu.VMEM((2, rows, 2 * S5_STATE_BLOCK), F32),
                        pltpu.VMEM((nblk, bsz, 2 * S5_STATE_BLOCK), F32)],
        compiler_params=_cparams(("arbitrary",)),
        name="s5_layer",
    )(x, pre_g.reshape(1, d), w_in.astype(BF16), perm, perm_t, bb, cc, ar, ai, d_skip.reshape(1, width),
      w_glu.astype(BF16), b_glu.reshape(1, width), w_out.astype(BF16), post_g.reshape(1, d))


def _swa_bias(rel_bias):
    W = WINDOW
    n = 4 * W
    dist = 2 * W - jnp.arange(n)
    valid = jnp.logical_and(dist >= 0, dist < W)
    dpos = jnp.maximum(dist, 0)
    max_exact = REL_BUCKETS // 2
    dist_f = jnp.maximum(dpos, 1).astype(F32)
    large = max_exact + (jnp.log(dist_f / max_exact) / math.log(REL_MAX_DIST / max_exact)
                         * (REL_BUCKETS - max_exact)).astype(jnp.int32)
    large = jnp.minimum(large, REL_BUCKETS - 1)
    bucket = jnp.where(dpos < max_exact, dpos, large)
    vec = jnp.where(valid[:, None], rel_bias[bucket].astype(F32), NEG_INF).T
    skew = jnp.tile(vec, (1, W))[:, :W * (n - 1)].reshape(vec.shape[0], W, n - 1)
    return skew[:, :, W:3 * W]


def _swa_pre_kernel(x_ref, g_ref, wqt_ref, wk_ref, wvt_ref, wz_ref, qt_ref, k_ref, vt_ref, z_ref, *, scale):
    hb = _rms(x_ref[0], g_ref[...]).astype(BF16)
    qt_ref[0] = (_dot_nt(wqt_ref[...], hb) * scale).astype(BF16)
    k_ref[0] = _dot(hb, wk_ref[...]).astype(BF16)
    vt_ref[0] = _dot_nt(wvt_ref[...], hb).astype(BF16)
    z_ref[0] = _dot(hb, wz_ref[...])


def _swa_kernel(qt_ref, kp_ref, kc_ref, vtp_ref, vtc_ref, bias_ref, sink_ref, z_ref, x_ref, wo_ref, g_ref,
                out_ref, ot_ref):
    W = WINDOW
    nwin = qt_ref.shape[2] // W
    step = pl.program_id(1)
    kall = jnp.concatenate([kp_ref[0], kc_ref[0]], axis=0)
    vtall = jnp.concatenate([vtp_ref[0], vtc_ref[0]], axis=1)
    nsub = SWA_UNIT_HEADS
    zq = jnp.zeros((HEAD_DIM, nsub * W), BF16)
    ones = jnp.ones((16, 2 * W), BF16)
    units = [(w, h, c) for w in range(nwin) for h in range(SWA_KV_HEADS) for c in range(SWA_GROUP // nsub)]

    def scores(w, h, c):
        hd0 = h * SWA_GROUP + c * nsub
        qh = jnp.concatenate([qt_ref[0, (hd0 + g) * HEAD_DIM:(hd0 + g + 1) * HEAD_DIM, w * W:(w + 1) * W]
                              for g in range(nsub)], axis=1)
        qz = jnp.concatenate([qh, zq] if h == 0 else [zq, qh], axis=0)
        return _dot(kall[w * W:(w + 2) * W], qz)

    pending = [scores(*u) for u in units[:SWA_LOOKAHEAD]]
    for idx, (w, h, c) in enumerate(units):
        raw = pending.pop(0)
        if idx + SWA_LOOKAHEAD < len(units):
            pending.append(scores(*units[idx + SWA_LOOKAHEAD]))
        cols = slice(c * nsub * W, (c + 1) * nsub * W)
        variant = (step == 0).astype(jnp.int32) if w == 0 else 0
        s = raw + bias_ref[variant, h, :, cols]
        sink = sink_ref[h, :, cols]
        m = jnp.maximum(jnp.max(s, axis=0, keepdims=True), sink)
        p = jnp.exp2(s - m).astype(BF16)
        vones = jnp.concatenate([vtall[h * HEAD_DIM:(h + 1) * HEAD_DIM, w * W:(w + 2) * W], ones], axis=0)
        o = _dot(vones, p)
        denom = o[HEAD_DIM:HEAD_DIM + 1] + jnp.exp2(sink - m)
        oh = o[:HEAD_DIM] * (1.0 / denom)
        for g in range(nsub):
            hd = h * SWA_GROUP + c * nsub + g
            ot_ref[hd * HEAD_DIM:(hd + 1) * HEAD_DIM, w * W:(w + 1) * W] = oh[:, g * W:(g + 1) * W]
    gated = ot_ref[...].T * jax.nn.silu(z_ref[0])
    r = _dot(gated.astype(BF16), wo_ref[...])
    out_ref[0] = x_ref[0] + _rms(r, g_ref[...])


def _swa_layer(x, pre_g, post_g, w_in, sinks, w_out, rel_bias):
    bsz, L, d = x.shape
    width = SWA_HEADS * HEAD_DIM
    kvw = SWA_KV_HEADS * HEAD_DIM
    W = WINDOW
    nb = L // W
    log2e = math.log2(math.e)
    tm = 512
    tok = lambda w_: pl.BlockSpec((1, tm, w_), lambda b, i: (b, i, 0))
    tokt = lambda w_: pl.BlockSpec((1, w_, tm), lambda b, i: (b, 0, i))
    wqt = w_in[:, :width].T.astype(BF16)
    wk = w_in[:, width:width + kvw].astype(BF16)
    wvt = w_in[:, width + kvw:width + 2 * kvw].T.astype(BF16)
    wz = w_in[:, width + 2 * kvw:].astype(BF16)
    qt, k, vt, z = pl.pallas_call(
        functools.partial(_swa_pre_kernel, scale=HEAD_DIM ** -0.5 * log2e),
        out_shape=[jax.ShapeDtypeStruct((bsz, width, L), BF16),
                   jax.ShapeDtypeStruct((bsz, L, kvw), BF16),
                   jax.ShapeDtypeStruct((bsz, kvw, L), BF16),
                   jax.ShapeDtypeStruct((bsz, L, width), F32)],
        grid=(bsz, L // tm),
        in_specs=[tok(d), _full((1, d)), _full(wqt.shape), _full(wk.shape), _full(wvt.shape), _full(wz.shape)],
        out_specs=[tokt(width), tok(kvw), tokt(kvw), tok(width)],
        compiler_params=_cparams(("parallel", "parallel")),
        name="swa_pre",
    )(x, pre_g.reshape(1, d), wqt, wk, wvt, wz)

    bias = jnp.transpose(_swa_bias(rel_bias.astype(F32) * log2e), (0, 2, 1))
    has_prev = (jnp.arange(2 * W) >= W)[None, :, None]
    variants = [bias,
                jnp.where(has_prev, bias, NEG_INF)]
    bias_t = jnp.stack([v.reshape(SWA_KV_HEADS, SWA_GROUP, 2 * W, W).transpose(0, 2, 1, 3)
                        .reshape(SWA_KV_HEADS, 2 * W, SWA_GROUP * W) for v in variants])
    sink = jnp.repeat(sinks.astype(F32) * log2e, W).reshape(SWA_KV_HEADS, 1, SWA_GROUP * W)

    nwin = SWA_WINDOWS_PER_STEP
    tq = nwin * W
    prev = lambda n: jnp.maximum(n * nwin - 1, 0)
    return pl.pallas_call(
        _swa_kernel,
        out_shape=jax.ShapeDtypeStruct(x.shape, x.dtype),
        grid=(bsz, L // tq),
        in_specs=[pl.BlockSpec((1, width, tq), lambda b, n: (b, 0, n)),
                  pl.BlockSpec((1, W, kvw), lambda b, n: (b, prev(n), 0)),
                  pl.BlockSpec((1, tq, kvw), lambda b, n: (b, n, 0)),
                  pl.BlockSpec((1, kvw, W), lambda b, n: (b, 0, prev(n))),
                  pl.BlockSpec((1, kvw, tq), lambda b, n: (b, 0, n)),
                  _full(bias_t.shape), _full(sink.shape),
                  pl.BlockSpec((1, tq, width), lambda b, n: (b, n, 0)),
                  pl.BlockSpec((1, tq, d), lambda b, n: (b, n, 0)),
                  _full(w_out.shape), _full((1, d))],
        out_specs=pl.BlockSpec((1, tq, d), lambda b, n: (b, n, 0)),
        scratch_shapes=[pltpu.VMEM((width, tq), F32)],
        compiler_params=_cparams(("parallel", "parallel")),
        name="swa_attn",
    )(qt, k, k, vt, vt, bias_t, sink, z, x, w_out.astype(BF16), post_g.reshape(1, d))


def _mla_pre_kernel(x_ref, g_ref, w_ref, qn_ref, kvn_ref, wq_ref, wkv_ref, wvt_ref, cq_ref, sq_ref, ck_ref, sk_ref,
                    oqn_ref, oqr_ref, okn_ref, okr_ref, ov_ref, oz_ref, *, scale):
    nope = MLA_HEADS * MLA_NOPE
    rope = MLA_HEADS * MLA_ROPE
    vw = MLA_HEADS * MLA_V
    hb = _rms(x_ref[0], g_ref[...]).astype(BF16)
    o1 = MLA_Q_RANK
    o2 = o1 + MLA_KV_RANK
    o3 = o2 + vw
    cq = _dot(hb, w_ref[:, :o1])
    ckv = _dot(hb, w_ref[:, o1:o2])
    oz_ref[0] = _dot(hb, w_ref[:, o2:o3])
    kr = _dot(hb, w_ref[:, o3:o3 + LANES])
    krs = _dot(hb, w_ref[:, o3 + LANES:o3 + 2 * LANES])
    okr_ref[0] = (kr * ck_ref[...] + krs * sk_ref[...]).astype(BF16)
    cqb = _rms(cq, qn_ref[...]).astype(BF16)
    oqn_ref[0] = (_dot_nt(wq_ref[:nope], cqb) * scale).astype(BF16)
    qr = _dot_nt(wq_ref[nope:nope + rope], cqb)
    qrs = _dot_nt(wq_ref[nope + rope:nope + 2 * rope], cqb)
    oqr_ref[0] = ((qr * cq_ref[...] + qrs * sq_ref[...]) * scale).astype(BF16)
    ckb = _rms(ckv, kvn_ref[...]).astype(BF16)
    okn_ref[0] = _dot(ckb, wkv_ref[:, :nope]).astype(BF16)
    vt = _dot_nt(wvt_ref[...], ckb).astype(BF16)
    tk = ov_ref.shape[3]
    for c in range(ov_ref.shape[1]):
        ov_ref[0, c] = vt[:, c * tk:(c + 1) * tk]


def _mla_attn_kernel(qn_ref, qr_ref, kn_ref, kr_ref, v_ref, z_ref, x_ref, wo_ref, g_ref, out_ref,
                     qs_ref, acc_ref, m_ref, o_ref):
    tq = qn_ref.shape[2]
    tk = v_ref.shape[3]
    npairs = MLA_HEADS // 2
    i = pl.program_id(1)
    krow = lax.broadcasted_iota(jnp.int32, (tk, 2 * tq), 0)
    qcol = lax.broadcasted_iota(jnp.int32, (tk, 2 * tq), 1)
    causal = krow <= jnp.where(qcol >= tq, qcol - tq, qcol)

    zn = jnp.zeros((MLA_NOPE, tq), BF16)
    zr = jnp.zeros((LANES - MLA_ROPE, tq), BF16)
    for p in range(npairs):
        qn = qn_ref[0, p * LANES:(p + 1) * LANES, :]
        r0 = 2 * p * MLA_ROPE
        c0 = jnp.concatenate([qn[:MLA_NOPE], zn, qr_ref[0, r0:r0 + MLA_ROPE, :], zr], axis=0)
        c1 = jnp.concatenate([zn, qn[MLA_NOPE:], qr_ref[0, r0 + MLA_ROPE:r0 + 2 * MLA_ROPE, :], zr], axis=0)
        qs_ref[p] = jnp.concatenate([c0, c1], axis=1)

    m_ref[...] = jnp.full(m_ref.shape, NEG_INF, F32)
    acc_ref[...] = jnp.zeros(acc_ref.shape, F32)
    ones = jnp.ones((acc_ref.shape[1] - LANES, tk), BF16)

    def kv_step(j, masked):
        ks = pl.multiple_of(j * tk, tk)
        kr = kr_ref[0, pl.ds(ks, tk), :]

        def scores(p):
            kc = jnp.concatenate([kn_ref[0, pl.ds(ks, tk), p * LANES:(p + 1) * LANES], kr], axis=1)
            return _dot(kc, qs_ref[p])

        pending = [scores(p) for p in range(MLA_LOOKAHEAD)]
        for p in range(npairs):
            s = pending.pop(0)
            if p + MLA_LOOKAHEAD < npairs:
                pending.append(scores(p + MLA_LOOKAHEAD))
            probs, alphas = [], []
            for c in range(2 * tq // LANES):
                sc = s[:, c * LANES:(c + 1) * LANES]
                if masked:
                    sc = jnp.where(causal[:, c * LANES:(c + 1) * LANES], sc, NEG_INF)
                m_prev = m_ref[p, :, c * LANES:(c + 1) * LANES]
                m_new = jnp.maximum(m_prev, jnp.max(sc, axis=0, keepdims=True))
                alphas.append(jnp.exp2(m_prev - m_new))
                probs.append(jnp.exp2(sc - m_new).astype(BF16))
                m_ref[p, :, c * LANES:(c + 1) * LANES] = m_new
            alpha = jnp.concatenate(alphas, axis=1)
            vones = jnp.concatenate([v_ref[0, j, p * LANES:(p + 1) * LANES, :], ones], axis=0)
            pv = _dot(vones, jnp.concatenate(probs, axis=1))
            acc_ref[p] = alpha * acc_ref[p] + pv

    def body(j, c):
        kv_step(j, False)
        return c

    lax.fori_loop(0, i, body, 0)
    kv_step(i, True)
    for p in range(npairs):
        a = acc_ref[p]
        a = a[:LANES] * (1.0 / a[LANES:LANES + 1])
        ot = jnp.concatenate([a[:MLA_V, :tq], a[MLA_V:, tq:]], axis=0)
        o_ref[:, p * LANES:(p + 1) * LANES] = ot.T
    gated = o_ref[...] * jax.nn.silu(z_ref[0])
    r = _dot(gated.astype(BF16), wo_ref[...])
    out_ref[0] = x_ref[0] + _rms(r, g_ref[...])


def _mla_layer(x, pre_g, post_g, w_in, q_norm, kv_norm, w_uq, w_ukv, w_out):
    bsz, L, d = x.shape
    H = MLA_HEADS
    dq = MLA_NOPE + MLA_ROPE
    nope = H * MLA_NOPE
    rope = H * MLA_ROPE
    vw = H * MLA_V
    half = MLA_ROPE // 2
    o_kr = MLA_Q_RANK + MLA_KV_RANK
    o_z = o_kr + MLA_ROPE
    w_kr = w_in[:, o_kr:o_z]
    w_krs = jnp.concatenate([w_kr[:, half:], w_kr[:, :half]], axis=1)
    reps = LANES // MLA_ROPE
    w1 = jnp.concatenate([w_in[:, :o_kr], w_in[:, o_z:]] + [w_kr] * reps + [w_krs] * reps, axis=1).astype(BF16)
    wq3 = w_uq.reshape(MLA_Q_RANK, H, dq)
    wq_r = wq3[:, :, MLA_NOPE:]
    wq_rs = jnp.concatenate([wq_r[:, :, half:], wq_r[:, :, :half]], axis=2)
    wqt = jnp.concatenate([wq3[:, :, :MLA_NOPE].reshape(MLA_Q_RANK, nope), wq_r.reshape(MLA_Q_RANK, rope),
                           wq_rs.reshape(MLA_Q_RANK, rope)], axis=1).T.astype(BF16)
    wkv3 = w_ukv.reshape(MLA_KV_RANK, H, MLA_NOPE + MLA_V)
    wkn = wkv3[:, :, :MLA_NOPE].reshape(MLA_KV_RANK, nope).astype(BF16)
    wvt = wkv3[:, :, MLA_NOPE:].reshape(MLA_KV_RANK, vw).T.astype(BF16)
    inv = ROPE_BASE ** (-jnp.arange(0, MLA_ROPE, 2, dtype=F32) / MLA_ROPE)
    ang = jnp.arange(L, dtype=F32)[:, None] * inv[None, :]
    cos, sin = jnp.cos(ang), jnp.sin(ang)
    cos32 = jnp.concatenate([cos, cos], axis=1)
    sin32 = jnp.concatenate([-sin, sin], axis=1)
    cos_k, sin_k = jnp.tile(cos32, (1, LANES // MLA_ROPE)), jnp.tile(sin32, (1, LANES // MLA_ROPE))
    cos_q, sin_q = jnp.tile(cos32, (1, H)).T, jnp.tile(sin32, (1, H)).T

    tm = 512
    tk = MLA_TK
    tok = lambda w_: pl.BlockSpec((1, tm, w_), lambda b, i: (b, i, 0))
    tokt = lambda w_: pl.BlockSpec((1, w_, tm), lambda b, i: (b, 0, i))
    scale = dq ** -0.5 * math.log2(math.e)
    qn, qr, kn, kr, v, z = pl.pallas_call(
        functools.partial(_mla_pre_kernel, scale=scale),
        out_shape=[jax.ShapeDtypeStruct((bsz, nope, L), BF16),
                   jax.ShapeDtypeStruct((bsz, rope, L), BF16),
                   jax.ShapeDtypeStruct((bsz, L, nope), BF16),
                   jax.ShapeDtypeStruct((bsz, L, LANES), BF16),
                   jax.ShapeDtypeStruct((bsz, L // tk, vw, tk), BF16),
                   jax.ShapeDtypeStruct((bsz, L, vw), F32)],
        grid=(bsz, L // tm),
        in_specs=[tok(d), _full((1, d)), _full(w1.shape), _full((1, MLA_Q_RANK)), _full((1, MLA_KV_RANK)),
                  _full(wqt.shape), _full(wkn.shape), _full(wvt.shape),
                  pl.BlockSpec((rope, tm), lambda b, i: (0, i)), pl.BlockSpec((rope, tm), lambda b, i: (0, i)),
                  pl.BlockSpec((tm, LANES), lambda b, i: (i, 0)), pl.BlockSpec((tm, LANES), lambda b, i: (i, 0))],
        out_specs=[tokt(nope), tokt(rope), tok(nope), tok(LANES),
                   pl.BlockSpec((1, tm // tk, vw, tk), lambda b, i: (b, i, 0, 0)), tok(vw)],
        compiler_params=_cparams(("parallel", "parallel")),
        name="mla_pre",
    )(x, pre_g.reshape(1, d), w1, q_norm.reshape(1, -1), kv_norm.reshape(1, -1), wqt, wkn, wvt,
      cos_q, sin_q, cos_k, sin_k)

    tq = MLA_TQ
    npairs = H // 2
    qspec = lambda w_: pl.BlockSpec((1, w_, tq), lambda b, i: (b, 0, i))
    kspec = lambda w_: pl.BlockSpec((1, L, w_), lambda b, i: (b, 0, 0))
    rowspec = lambda w_: pl.BlockSpec((1, tq, w_), lambda b, i: (b, i, 0))
    return pl.pallas_call(
        _mla_attn_kernel,
        out_shape=jax.ShapeDtypeStruct(x.shape, x.dtype),
        grid=(bsz, L // tq),
        in_specs=[qspec(nope), qspec(rope), kspec(nope), kspec(LANES),
                  pl.BlockSpec((1, L // tk, vw, tk), lambda b, i: (b, 0, 0, 0)),
                  rowspec(vw), rowspec(d), _full(w_out.shape), _full((1, d))],
        out_specs=rowspec(d),
        scratch_shapes=[pltpu.VMEM((npairs, 2 * LANES, 2 * tq), BF16),
                        pltpu.VMEM((npairs, LANES + 16, 2 * tq), F32),
                        pltpu.VMEM((npairs, 1, 2 * tq), F32),
                        pltpu.VMEM((tq, vw), F32)],
        compiler_params=_cparams(("parallel", "arbitrary")),
        name="mla_attn",
    )(qn, qr, kn, kr, v, z, x, w_out.astype(BF16), post_g.reshape(1, d))


def _sgu_kernel(x_ref, g_ref, w_ref, lng_ref, lnb_ref, ws_ref, bs_ref, wo_ref, pg_ref, out_ref, s_ref):
    width = wo_ref.shape[0]
    tm = x_ref.shape[1]
    lane = lax.broadcasted_iota(jnp.int32, (1, LANES), 1)
    lo = lane < HALF
    x = x_ref[0]
    hb = _rms(x, g_ref[...]).astype(BF16)
    v = jax.nn.gelu(_dot(hb, w_ref[:, width:2 * width]))
    mu = jnp.mean(v, axis=-1, keepdims=True)
    vc = v - mu
    var = jnp.mean(vc * vc, axis=-1, keepdims=True)
    vb = (vc * lax.rsqrt(var + EPS) * lng_ref[...] + lnb_ref[...]).astype(BF16)
    for c in range(tm // SGU_CHUNK):
        for jj in range(width // LANES):
            blk = vb[c * SGU_CHUNK:(c + 1) * SGU_CHUNK, jj * LANES:(jj + 1) * LANES]
            r = _dot(ws_ref[jj], blk)
            s_ref[c * SGU_CHUNK:(c + 1) * SGU_CHUNK, jj * LANES:(jj + 1) * LANES] = (
                jnp.where(lo, r[:SGU_CHUNK], r[SGU_CHUNK:]) + bs_ref[jj])
    u = jax.nn.gelu(_dot(hb, w_ref[:, :width]))
    z = _dot(hb, w_ref[:, 2 * width:])
    o = u * s_ref[...] * jax.nn.silu(z)
    r = _dot(o.astype(BF16), wo_ref[...])
    out_ref[0] = x + _rms(r, pg_ref[...])


def _sgu_layer(x, pre_g, post_g, w_in, ln_g, ln_b, w_s, b_s, w_out):
    bsz, L, d = x.shape
    width = w_out.shape[0]
    T = SGU_CHUNK
    gd = width // SGU_GROUPS
    tril = jnp.tril(jnp.ones((T, T), dtype=bool))
    ws = jnp.where(tril[None], w_s, 0.0).reshape(SGU_GROUPS // 2, 2 * T, T).astype(BF16)
    bs = jnp.repeat(b_s.astype(F32).T, gd, axis=1)
    bs = bs.reshape(T, width // LANES, LANES).transpose(1, 0, 2)
    tm = 512
    return pl.pallas_call(
        _sgu_kernel,
        out_shape=jax.ShapeDtypeStruct(x.shape, x.dtype),
        grid=(bsz, L // tm),
        in_specs=[pl.BlockSpec((1, tm, d), lambda b, i: (b, i, 0)),
                  _full((1, d)), _full(w_in.shape), _full((1, width)), _full((1, width)),
                  _full(ws.shape), _full(bs.shape), _full(w_out.shape), _full((1, d))],
        out_specs=pl.BlockSpec((1, tm, d), lambda b, i: (b, i, 0)),
        scratch_shapes=[pltpu.VMEM((tm, width), F32)],
        compiler_params=_cparams(("parallel", "parallel")),
        name="sgu",
    )(x, pre_g.reshape(1, d), w_in.astype(BF16), ln_g.reshape(1, width), ln_b.reshape(1, width),
      ws, bs, w_out.astype(BF16), post_g.reshape(1, d))


def kernel(x, pre_norm, post_norm, rel_bias, a_w_in, a_lam_re, a_lam_im, a_log_dt, a_b_re, a_b_im, a_c_re, a_c_im, a_d, a_w_glu, a_b_glu, a_w_out, b_w_in, b_sinks, b_w_out, c_w_in, c_q_norm, c_kv_norm, c_w_uq, c_w_ukv, c_w_out, d_w_in, d_ln_g, d_ln_b, d_w_s, d_b_s, d_w_out):
    depth = pre_norm.shape[0]
    for i in range(depth):
        kind, j = i % 4, i // 4
        if kind == 0:
            x = _s5_layer(x, pre_norm[i], post_norm[i], a_w_in[j], a_lam_re[j], a_lam_im[j], a_log_dt[j],
                          a_b_re[j], a_b_im[j], a_c_re[j], a_c_im[j], a_d[j], a_w_glu[j], a_b_glu[j],
                          a_w_out[j])
        elif kind == 1:
            x = _swa_layer(x, pre_norm[i], post_norm[i], b_w_in[j], b_sinks[j], b_w_out[j], rel_bias)
        elif kind == 2:
            x = _mla_layer(x, pre_norm[i], post_norm[i], c_w_in[j], c_q_norm[j], c_kv_norm[j], c_w_uq[j],
                           c_w_ukv[j], c_w_out[j])
        else:
            x = _sgu_layer(x, pre_norm[i], post_norm[i], d_w_in[j], d_ln_g[j], d_ln_b[j], d_w_s[j],
                           d_b_s[j], d_w_out[j])
    return x
```

```python
import functools
import math

import jax
import jax.numpy as jnp
import numpy as np
from jax import lax
from jax.experimental import pallas as pl
from jax.experimental.pallas import tpu as pltpu

F32 = jnp.float32
BF16 = jnp.bfloat16

D_MODEL = 1024
EPS = 1e-6
NEG_INF = -1e30
LANES = 128
HALF = LANES // 2

SSM_GROUP = 16
SSM_STATE = 64
S5_CH_BLOCK = LANES
S5_GROUPS_PER_BLOCK = S5_CH_BLOCK // SSM_GROUP
S5_STATE_BLOCK = S5_GROUPS_PER_BLOCK * SSM_STATE
S5_T = 64

HEAD_DIM = 64
SWA_HEADS = 16
SWA_KV_HEADS = 2
SWA_GROUP = SWA_HEADS // SWA_KV_HEADS
WINDOW = 128
SWA_WINDOWS_PER_STEP = 4
SWA_LOOKAHEAD = 2
SWA_UNIT_HEADS = 8
REL_BUCKETS = 32
REL_MAX_DIST = 128

MLA_HEADS = 16
MLA_NOPE = 64
MLA_ROPE = 32
MLA_V = 64
MLA_KV_RANK = 256
MLA_Q_RANK = 768
ROPE_BASE = 10000.0
MLA_TQ = 256
MLA_TK = 256
MLA_LOOKAHEAD = 4

SGU_CHUNK = 128
SGU_GROUPS = 16

VMEM_LIMIT = 56 * 1024 * 1024


def _cparams(sem):
    return pltpu.CompilerParams(dimension_semantics=sem, vmem_limit_bytes=VMEM_LIMIT)


def _rms(x, g):
    return x * lax.rsqrt(jnp.mean(x * x, axis=-1, keepdims=True) + EPS) * g


def _dot(a, b):
    return jnp.dot(a, b, preferred_element_type=F32)


def _dot_nt(a, b):
    return lax.dot_general(a, b, (((1,), (1,)), ((), ())), preferred_element_type=F32)


def _full(shape):
    n = len(shape)
    return pl.BlockSpec(shape, lambda *_: (0,) * n, pipeline_mode=pl.Buffered(1))


def _s5_kernel(x_ref, g_ref, w_ref, perm_ref, permt_ref, bb_ref, cc_ref, ar_ref, ai_ref, d_ref,
               wg_ref, bg_ref, wo_ref, pg_ref, out_ref, u_ref, z_ref, y_ref, s_ref, carry_ref, *, tt):
    bsz = x_ref.shape[0]
    width = wg_ref.shape[0]
    rows = bsz * tt
    nblk = bb_ref.shape[0]
    sb = S5_STATE_BLOCK

    @pl.when(pl.program_id(0) == 0)
    def _():
        carry_ref[...] = jnp.zeros_like(carry_ref)

    x = x_ref[...].reshape(rows, x_ref.shape[2])
    hb = _rms(x, g_ref[...]).astype(BF16)
    hb = _dot(perm_ref[...], hb).astype(BF16)
    u_ref[...] = _dot(hb, w_ref[:, :width])
    z_ref[...] = _dot(hb, w_ref[:, width:])

    def project_in(i):
        s_ref[i % 2] = _dot(u_ref[:, i * LANES:(i + 1) * LANES].astype(BF16), bb_ref[i])

    project_in(0)
    for i in range(nblk):
        if i + 1 < nblk:
            project_in(i + 1)
        buf = s_ref.at[i % 2]
        ar = ar_ref[i]
        ai = ai_ref[i]
        sr = carry_ref[i, :, 0:sb]
        si = carry_ref[i, :, sb:2 * sb]
        for t in range(tt):
            r0 = t * bsz
            nr = ar * sr - ai * si + buf[r0:r0 + bsz, 0:sb]
            ni = ar * si + ai * sr + buf[r0:r0 + bsz, sb:2 * sb]
            buf[r0:r0 + bsz, 0:sb] = nr
            buf[r0:r0 + bsz, sb:2 * sb] = ni
            sr, si = nr, ni
        carry_ref[i, :, 0:sb] = sr
        carry_ref[i, :, sb:2 * sb] = si
        ub = u_ref[:, i * LANES:(i + 1) * LANES]
        y = _dot(buf[...].astype(BF16), cc_ref[i]) + d_ref[:, i * LANES:(i + 1) * LANES] * ub
        y_ref[:, i * LANES:(i + 1) * LANES] = jax.nn.gelu(y)

    y = y_ref[...]
    gate = jax.nn.sigmoid(_dot(y.astype(BF16), wg_ref[...]) + bg_ref[...])
    o = y * gate * jax.nn.silu(z_ref[...])
    ob = _dot(permt_ref[...], o.astype(BF16)).astype(BF16)
    r = _dot(ob, wo_ref[...])
    out_ref[...] = (x + _rms(r, pg_ref[...])).reshape(out_ref.shape)


def _s5_discretize(lam_re, lam_im, log_dt, b_re, b_im):
    dt = jnp.exp(log_dt)[:, None]
    mag = jnp.exp(lam_re * dt)
    ab_re = mag * jnp.cos(lam_im * dt)
    ab_im = mag * jnp.sin(lam_im * dt)
    den = lam_re * lam_re + lam_im * lam_im
    nr = ab_re - 1.0
    f_re = (nr * lam_re + ab_im * lam_im) / den
    f_im = (ab_im * lam_re - nr * lam_im) / den
    bb_re = f_re[..., None] * b_re - f_im[..., None] * b_im
    bb_im = f_re[..., None] * b_im + f_im[..., None] * b_re
    return ab_re, ab_im, bb_re, bb_im


def _s5_layer(x, pre_g, post_g, w_in, lam_re, lam_im, log_dt, b_re, b_im, c_re, c_im, d_skip,
              w_glu, b_glu, w_out):
    bsz, L, d = x.shape
    width = w_in.shape[1] // 2
    nblk = width // S5_CH_BLOCK
    gpb = S5_GROUPS_PER_BLOCK
    tt = S5_T
    rows = bsz * tt

    src = (np.arange(rows) % bsz) * tt + np.arange(rows) // bsz
    perm_np = np.zeros((rows, rows), np.float32)
    perm_np[np.arange(rows), src] = 1.0
    perm = jnp.asarray(perm_np, BF16)
    perm_t = jnp.asarray(perm_np.T, BF16)

    ab_re, ab_im, bb_re, bb_im = _s5_discretize(lam_re, lam_im, log_dt, b_re, b_im)
    eye = jnp.eye(gpb, dtype=F32)

    def pack_b(bb):
        t = bb.reshape(nblk, gpb, SSM_STATE, SSM_GROUP)
        return jnp.einsum('igph,gk->ikhgp', t, eye).reshape(nblk, S5_CH_BLOCK, S5_STATE_BLOCK)

    def pack_c(cc):
        t = cc.reshape(nblk, gpb, SSM_GROUP, SSM_STATE)
        return jnp.einsum('ighp,gk->igpkh', t, eye).reshape(nblk, S5_STATE_BLOCK, S5_CH_BLOCK)

    bb = jnp.concatenate([pack_b(bb_re), pack_b(bb_im)], axis=2).astype(BF16)
    cc = jnp.concatenate([pack_c(c_re), -pack_c(c_im)], axis=1).astype(BF16)
    ar = jnp.broadcast_to(ab_re.reshape(nblk, 1, S5_STATE_BLOCK), (nblk, bsz, S5_STATE_BLOCK))
    ai = jnp.broadcast_to(ab_im.reshape(nblk, 1, S5_STATE_BLOCK), (nblk, bsz, S5_STATE_BLOCK))

    xspec = pl.BlockSpec((bsz, tt, d), lambda i: (0, i, 0))
    return pl.pallas_call(
        functools.partial(_s5_kernel, tt=tt),
        out_shape=jax.ShapeDtypeStruct(x.shape, x.dtype),
        grid=(L // tt,),
        in_specs=[xspec, _full((1, d)), _full(w_in.shape), _full(perm.shape), _full(perm_t.shape),
                  _full(bb.shape), _full(cc.shape), _full(ar.shape), _full(ai.shape), _full((1, width)),
                  _full(w_glu.shape), _full((1, width)), _full(w_out.shape), _full((1, d))],
        out_specs=xspec,
        scratch_shapes=[pltpu.VMEM((rows, width), F32),
                        pltpu.VMEM((rows, width), F32),
                        pltpu.VMEM((rows, width), F32),
                        pltpu.VMEM((2, rows, 2 * S5_STATE_BLOCK), F32),
                        pltpu.VMEM((nblk, bsz, 2 * S5_STATE_BLOCK), F32)],
        compiler_params=_cparams(("arbitrary",)),
        name="s5_layer",
    )(x, pre_g.reshape(1, d), w_in.astype(BF16), perm, perm_t, bb, cc, ar, ai, d_skip.reshape(1, width),
      w_glu.astype(BF16), b_glu.reshape(1, width), w_out.astype(BF16), post_g.reshape(1, d))


def _swa_bias(rel_bias):
    W = WINDOW
    n = 4 * W
    dist = 2 * W - jnp.arange(n)
    valid = jnp.logical_and(dist >= 0, dist < W)
    dpos = jnp.maximum(dist, 0)
    max_exact = REL_BUCKETS // 2
    dist_f = jnp.maximum(dpos, 1).astype(F32)
    large = max_exact + (jnp.log(dist_f / max_exact) / math.log(REL_MAX_DIST / max_exact)
                         * (REL_BUCKETS - max_exact)).astype(jnp.int32)
    large = jnp.minimum(large, REL_BUCKETS - 1)
    bucket = jnp.where(dpos < max_exact, dpos, large)
    vec = jnp.where(valid[:, None], rel_bias[bucket].astype(F32), NEG_INF).T
    skew = jnp.tile(vec, (1, W))[:, :W * (n - 1)].reshape(vec.shape[0], W, n - 1)
    return skew[:, :, W:3 * W]


def _swa_pre_kernel(x_ref, g_ref, wqt_ref, wk_ref, wvt_ref, wz_ref, qt_ref, k_ref, vt_ref, z_ref, *, scale):
    hb = _rms(x_ref[0], g_ref[...]).astype(BF16)
    qt_ref[0] = (_dot_nt(wqt_ref[...], hb) * scale).astype(BF16)
    k_ref[0] = _dot(hb, wk_ref[...]).astype(BF16)
    vt_ref[0] = _dot_nt(wvt_ref[...], hb).astype(BF16)
    z_ref[0] = _dot(hb, wz_ref[...])


def _swa_kernel(qt_ref, kp_ref, kc_ref, vtp_ref, vtc_ref, bias_ref, sink_ref, z_ref, x_ref, wo_ref, g_ref,
                out_ref, ot_ref):
    W = WINDOW
    nwin = qt_ref.shape[2] // W
    step = pl.program_id(1)
    kall = jnp.concatenate([kp_ref[0], kc_ref[0]], axis=0)
    vtall = jnp.concatenate([vtp_ref[0], vtc_ref[0]], axis=1)
    nsub = SWA_UNIT_HEADS
    zq = jnp.zeros((HEAD_DIM, nsub * W), BF16)
    ones = jnp.ones((16, 2 * W), BF16)
    units = [(w, h, c) for w in range(nwin) for h in range(SWA_KV_HEADS) for c in range(SWA_GROUP // nsub)]

    def scores(w, h, c):
        hd0 = h * SWA_GROUP + c * nsub
        qh = jnp.concatenate([qt_ref[0, (hd0 + g) * HEAD_DIM:(hd0 + g + 1) * HEAD_DIM, w * W:(w + 1) * W]
                              for g in range(nsub)], axis=1)
        qz = jnp.concatenate([qh, zq] if h == 0 else [zq, qh], axis=0)
        return _dot(kall[w * W:(w + 2) * W], qz)

    pending = [scores(*u) for u in units[:SWA_LOOKAHEAD]]
    for idx, (w, h, c) in enumerate(units):
        raw = pending.pop(0)
        if idx + SWA_LOOKAHEAD < len(units):
            pending.append(scores(*units[idx + SWA_LOOKAHEAD]))
        cols = slice(c * nsub * W, (c + 1) * nsub * W)
        variant = (step == 0).astype(jnp.int32) if w == 0 else 0
        s = raw + bias_ref[variant, h, :, cols]
        sink = sink_ref[h, :, cols]
        m = jnp.maximum(jnp.max(s, axis=0, keepdims=True), sink)
        p = jnp.exp2(s - m).astype(BF16)
        vones = jnp.concatenate([vtall[h * HEAD_DIM:(h + 1) * HEAD_DIM, w * W:(w + 2) * W], ones], axis=0)
        o = _dot(vones, p)
        denom = o[HEAD_DIM:HEAD_DIM + 1] + jnp.exp2(sink - m)
        oh = o[:HEAD_DIM] * (1.0 / denom)
        for g in range(nsub):
            hd = h * SWA_GROUP + c * nsub + g
            ot_ref[hd * HEAD_DIM:(hd + 1) * HEAD_DIM, w * W:(w + 1) * W] = oh[:, g * W:(g + 1) * W]
    gated = ot_ref[...].T * jax.nn.silu(z_ref[0])
    r = _dot(gated.astype(BF16), wo_ref[...])
    out_ref[0] = x_ref[0] + _rms(r, g_ref[...])


def _swa_layer(x, pre_g, post_g, w_in, sinks, w_out, rel_bias):
    bsz, L, d = x.shape
    width = SWA_HEADS * HEAD_DIM
    kvw = SWA_KV_HEADS * HEAD_DIM
    W = WINDOW
    nb = L // W
    log2e = math.log2(math.e)
    tm = 512
    tok = lambda w_: pl.BlockSpec((1, tm, w_), lambda b, i: (b, i, 0))
    tokt = lambda w_: pl.BlockSpec((1, w_, tm), lambda b, i: (b, 0, i))
    wqt = w_in[:, :width].T.astype(BF16)
    wk = w_in[:, width:width + kvw].astype(BF16)
    wvt = w_in[:, width + kvw:width + 2 * kvw].T.astype(BF16)
    wz = w_in[:, width + 2 * kvw:].astype(BF16)
    qt, k, vt, z = pl.pallas_call(
        functools.partial(_swa_pre_kernel, scale=HEAD_DIM ** -0.5 * log2e),
        out_shape=[jax.ShapeDtypeStruct((bsz, width, L), BF16),
                   jax.ShapeDtypeStruct((bsz, L, kvw), BF16),
                   jax.ShapeDtypeStruct((bsz, kvw, L), BF16),
                   jax.ShapeDtypeStruct((bsz, L, width), F32)],
        grid=(bsz, L // tm),
        in_specs=[tok(d), _full((1, d)), _full(wqt.shape), _full(wk.shape), _full(wvt.shape), _full(wz.shape)],
        out_specs=[tokt(width), tok(kvw), tokt(kvw), tok(width)],
        compiler_params=_cparams(("parallel", "parallel")),
        name="swa_pre",
    )(x, pre_g.reshape(1, d), wqt, wk, wvt, wz)

    bias = jnp.transpose(_swa_bias(rel_bias.astype(F32) * log2e), (0, 2, 1))
    has_prev = (jnp.arange(2 * W) >= W)[None, :, None]
    variants = [bias,
                jnp.where(has_prev, bias, NEG_INF)]
    bias_t = jnp.stack([v.reshape(SWA_KV_HEADS, SWA_GROUP, 2 * W, W).transpose(0, 2, 1, 3)
                        .reshape(SWA_KV_HEADS, 2 * W, SWA_GROUP * W) for v in variants])
    sink = jnp.repeat(sinks.astype(F32) * log2e, W).reshape(SWA_KV_HEADS, 1, SWA_GROUP * W)

    nwin = SWA_WINDOWS_PER_STEP
    tq = nwin * W
    prev = lambda n: jnp.maximum(n * nwin - 1, 0)
    return pl.pallas_call(
        _swa_kernel,
        out_shape=jax.ShapeDtypeStruct(x.shape, x.dtype),
        grid=(bsz, L // tq),
        in_specs=[pl.BlockSpec((1, width, tq), lambda b, n: (b, 0, n)),
                  pl.BlockSpec((1, W, kvw), lambda b, n: (b, prev(n), 0)),
                  pl.BlockSpec((1, tq, kvw), lambda b, n: (b, n, 0)),
                  pl.BlockSpec((1, kvw, W), lambda b, n: (b, 0, prev(n))),
                  pl.BlockSpec((1, kvw, tq), lambda b, n: (b, 0, n)),
                  _full(bias_t.shape), _full(sink.shape),
                  pl.BlockSpec((1, tq, width), lambda b, n: (b, n, 0)),
                  pl.BlockSpec((1, tq, d), lambda b, n: (b, n, 0)),
                  _full(w_out.shape), _full((1, d))],
        out_specs=pl.BlockSpec((1, tq, d), lambda b, n: (b, n, 0)),
        scratch_shapes=[pltpu.VMEM((width, tq), F32)],
        compiler_params=_cparams(("parallel", "parallel")),
        name="swa_attn",
    )(qt, k, k, vt, vt, bias_t, sink, z, x, w_out.astype(BF16), post_g.reshape(1, d))


def _mla_pre_kernel(x_ref, g_ref, w_ref, qn_ref, kvn_ref, wq_ref, wkv_ref, wvt_ref, cq_ref, sq_ref, ck_ref, sk_ref,
                    oqn_ref, oqr_ref, okn_ref, okr_ref, ov_ref, oz_ref, *, scale):
    nope = MLA_HEADS * MLA_NOPE
    rope = MLA_HEADS * MLA_ROPE
    vw = MLA_HEADS * MLA_V
    hb = _rms(x_ref[0], g_ref[...]).astype(BF16)
    o1 = MLA_Q_RANK
    o2 = o1 + MLA_KV_RANK
    o3 = o2 + vw
    cq = _dot(hb, w_ref[:, :o1])
    ckv = _dot(hb, w_ref[:, o1:o2])
    oz_ref[0] = _dot(hb, w_ref[:, o2:o3])
    kr = _dot(hb, w_ref[:, o3:o3 + LANES])
    krs = _dot(hb, w_ref[:, o3 + LANES:o3 + 2 * LANES])
    okr_ref[0] = (kr * ck_ref[...] + krs * sk_ref[...]).astype(BF16)
    cqb = _rms(cq, qn_ref[...]).astype(BF16)
    oqn_ref[0] = (_dot_nt(wq_ref[:nope], cqb) * scale).astype(BF16)
    qr = _dot_nt(wq_ref[nope:nope + rope], cqb)
    hr = MLA_ROPE // 2
    qrs = jnp.concatenate([qr[h * MLA_ROPE + o:h * MLA_ROPE + o + hr]
                           for h in range(MLA_HEADS) for o in (hr, 0)], axis=0)
    oqr_ref[0] = ((qr * cq_ref[...] + qrs * sq_ref[...]) * scale).astype(BF16)
    ckb = _rms(ckv, kvn_ref[...]).astype(BF16)
    okn_ref[0] = _dot(ckb, wkv_ref[:, :nope]).astype(BF16)
    vt = _dot_nt(wvt_ref[...], ckb).astype(BF16)
    tk = ov_ref.shape[3]
    for c in range(ov_ref.shape[1]):
        ov_ref[0, c] = vt[:, c * tk:(c + 1) * tk]


def _mla_attn_kernel(qn_ref, qr_ref, kn_ref, kr_ref, v_ref, z_ref, x_ref, wo_ref, g_ref, out_ref,
                     qs_ref, acc_ref, m_ref, o_ref):
    tq = qn_ref.shape[2]
    tk = v_ref.shape[3]
    npairs = MLA_HEADS // 2
    i = pl.program_id(1)
    krow = lax.broadcasted_iota(jnp.int32, (tk, 2 * tq), 0)
    qcol = lax.broadcasted_iota(jnp.int32, (tk, 2 * tq), 1)
    causal = krow <= jnp.where(qcol >= tq, qcol - tq, qcol)

    zn = jnp.zeros((MLA_NOPE, tq), BF16)
    zr = jnp.zeros((LANES - MLA_ROPE, tq), BF16)
    for p in range(npairs):
        qn = qn_ref[0, p * LANES:(p + 1) * LANES, :]
        r0 = 2 * p * MLA_ROPE
        c0 = jnp.concatenate([qn[:MLA_NOPE], zn, qr_ref[0, r0:r0 + MLA_ROPE, :], zr], axis=0)
        c1 = jnp.concatenate([zn, qn[MLA_NOPE:], qr_ref[0, r0 + MLA_ROPE:r0 + 2 * MLA_ROPE, :], zr], axis=0)
        qs_ref[p] = jnp.concatenate([c0, c1], axis=1)

    m_ref[...] = jnp.full(m_ref.shape, NEG_INF, F32)
    acc_ref[...] = jnp.zeros(acc_ref.shape, F32)
    ones = jnp.ones((acc_ref.shape[1] - LANES, tk), BF16)

    def kv_step(j, masked):
        ks = pl.multiple_of(j * tk, tk)
        kr = kr_ref[0, pl.ds(ks, tk), :]

        def scores(p):
            kc = jnp.concatenate([kn_ref[0, pl.ds(ks, tk), p * LANES:(p + 1) * LANES], kr], axis=1)
            return [_dot(kc, qs_ref[p, :, c * 2 * LANES:(c + 1) * 2 * LANES]) for c in range(tq // LANES)]

        pending = [scores(p) for p in range(MLA_LOOKAHEAD)]
        late = []

        def flush():
            pp, alpha, pr = late.pop(0)
            vones = jnp.concatenate([v_ref[0, j, pp * LANES:(pp + 1) * LANES, :], ones], axis=0)
            acc_ref[pp] = alpha * acc_ref[pp] + _dot(vones, pr)

        for p in range(npairs):
            s = pending.pop(0)
            if p + MLA_LOOKAHEAD < npairs:
                pending.append(scores(p + MLA_LOOKAHEAD))
            probs, alphas = [], []
            for c in range(2 * tq // LANES):
                sc = s[c // 2][:, (c % 2) * LANES:(c % 2 + 1) * LANES]
                if masked:
                    sc = jnp.where(causal[:, c * LANES:(c + 1) * LANES], sc, NEG_INF)
                m_prev = m_ref[p, :, c * LANES:(c + 1) * LANES]
                m_new = jnp.maximum(m_prev, jnp.max(sc, axis=0, keepdims=True))
                alphas.append(jnp.exp2(m_prev - m_new))
                probs.append(jnp.exp2(sc - m_new).astype(BF16))
                m_ref[p, :, c * LANES:(c + 1) * LANES] = m_new
            if late:
                flush()
            late.append((p, jnp.concatenate(alphas, axis=1), jnp.concatenate(probs, axis=1)))
        flush()

    def body(j, c):
        kv_step(j, False)
        return c

    lax.fori_loop(0, i, body, 0)
    kv_step(i, True)
    for p in range(npairs):
        a = acc_ref[p]
        a = a[:LANES] * (1.0 / a[LANES:LANES + 1])
        ot = jnp.concatenate([a[:MLA_V, :tq], a[MLA_V:, tq:]], axis=0)
        o_ref[:, p * LANES:(p + 1) * LANES] = ot.T
    gated = o_ref[...] * jax.nn.silu(z_ref[0])
    r = _dot(gated.astype(BF16), wo_ref[...])
    out_ref[0] = x_ref[0] + _rms(r, g_ref[...])


def _mla_layer(x, pre_g, post_g, w_in, q_norm, kv_norm, w_uq, w_ukv, w_out):
    bsz, L, d = x.shape
    H = MLA_HEADS
    dq = MLA_NOPE + MLA_ROPE
    nope = H * MLA_NOPE
    rope = H * MLA_ROPE
    vw = H * MLA_V
    half = MLA_ROPE // 2
    o_kr = MLA_Q_RANK + MLA_KV_RANK
    o_z = o_kr + MLA_ROPE
    w_kr = w_in[:, o_kr:o_z]
    w_krs = jnp.concatenate([w_kr[:, half:], w_kr[:, :half]], axis=1)
    reps = LANES // MLA_ROPE
    w1 = jnp.concatenate([w_in[:, :o_kr], w_in[:, o_z:]] + [w_kr] * reps + [w_krs] * reps, axis=1).astype(BF16)
    wq3 = w_uq.reshape(MLA_Q_RANK, H, dq)
    wqt = jnp.concatenate([wq3[:, :, :MLA_NOPE].reshape(MLA_Q_RANK, nope),
                           wq3[:, :, MLA_NOPE:].reshape(MLA_Q_RANK, rope)], axis=1).T.astype(BF16)
    wkv3 = w_ukv.reshape(MLA_KV_RANK, H, MLA_NOPE + MLA_V)
    wkn = wkv3[:, :, :MLA_NOPE].reshape(MLA_KV_RANK, nope).astype(BF16)
    wvt = wkv3[:, :, MLA_NOPE:].reshape(MLA_KV_RANK, vw).T.astype(BF16)
    inv = ROPE_BASE ** (-jnp.arange(0, MLA_ROPE, 2, dtype=F32) / MLA_ROPE)
    ang = jnp.arange(L, dtype=F32)[:, None] * inv[None, :]
    cos, sin = jnp.cos(ang), jnp.sin(ang)
    cos32 = jnp.concatenate([cos, cos], axis=1)
    sin32 = jnp.concatenate([-sin, sin], axis=1)
    cos_k, sin_k = jnp.tile(cos32, (1, LANES // MLA_ROPE)), jnp.tile(sin32, (1, LANES // MLA_ROPE))
    cos_q, sin_q = jnp.tile(cos32, (1, H)).T, jnp.tile(sin32, (1, H)).T

    tm = 512
    tk = MLA_TK
    tok = lambda w_: pl.BlockSpec((1, tm, w_), lambda b, i: (b, i, 0))
    tokt = lambda w_: pl.BlockSpec((1, w_, tm), lambda b, i: (b, 0, i))
    scale = dq ** -0.5 * math.log2(math.e)
    qn, qr, kn, kr, v, z = pl.pallas_call(
        functools.partial(_mla_pre_kernel, scale=scale),
        out_shape=[jax.ShapeDtypeStruct((bsz, nope, L), BF16),
                   jax.ShapeDtypeStruct((bsz, rope, L), BF16),
                   jax.ShapeDtypeStruct((bsz, L, nope), BF16),
                   jax.ShapeDtypeStruct((bsz, L, LANES), BF16),
                   jax.ShapeDtypeStruct((bsz, L // tk, vw, tk), BF16),
                   jax.ShapeDtypeStruct((bsz, L, vw), F32)],
        grid=(bsz, L // tm),
        in_specs=[tok(d), _full((1, d)), _full(w1.shape), _full((1, MLA_Q_RANK)), _full((1, MLA_KV_RANK)),
                  _full(wqt.shape), _full(wkn.shape), _full(wvt.shape),
                  pl.BlockSpec((rope, tm), lambda b, i: (0, i)), pl.BlockSpec((rope, tm), lambda b, i: (0, i)),
                  pl.BlockSpec((tm, LANES), lambda b, i: (i, 0)), pl.BlockSpec((tm, LANES), lambda b, i: (i, 0))],
        out_specs=[tokt(nope), tokt(rope), tok(nope), tok(LANES),
                   pl.BlockSpec((1, tm // tk, vw, tk), lambda b, i: (b, i, 0, 0)), tok(vw)],
        compiler_params=_cparams(("parallel", "parallel")),
        name="mla_pre",
    )(x, pre_g.reshape(1, d), w1, q_norm.reshape(1, -1), kv_norm.reshape(1, -1), wqt, wkn, wvt,
      cos_q, sin_q, cos_k, sin_k)

    tq = MLA_TQ
    npairs = H // 2
    qspec = lambda w_: pl.BlockSpec((1, w_, tq), lambda b, i: (b, 0, i))
    kspec = lambda w_: pl.BlockSpec((1, L, w_), lambda b, i: (b, 0, 0))
    rowspec = lambda w_: pl.BlockSpec((1, tq, w_), lambda b, i: (b, i, 0))
    return pl.pallas_call(
        _mla_attn_kernel,
        out_shape=jax.ShapeDtypeStruct(x.shape, x.dtype),
        grid=(bsz, L // tq),
        in_specs=[qspec(nope), qspec(rope), kspec(nope), kspec(LANES),
                  pl.BlockSpec((1, L // tk, vw, tk), lambda b, i: (b, 0, 0, 0)),
                  rowspec(vw), rowspec(d), _full(w_out.shape), _full((1, d))],
        out_specs=rowspec(d),
        scratch_shapes=[pltpu.VMEM((npairs, 2 * LANES, 2 * tq), BF16),
                        pltpu.VMEM((npairs, LANES + 16, 2 * tq), F32),
                        pltpu.VMEM((npairs, 1, 2 * tq), F32),
                        pltpu.VMEM((tq, vw), F32)],
        compiler_params=_cparams(("parallel", "arbitrary")),
        name="mla_attn",
    )(qn, qr, kn, kr, v, z, x, w_out.astype(BF16), post_g.reshape(1, d))


def _sgu_kernel(x_ref, g_ref, w_ref, lng_ref, lnb_ref, ws_ref, bs_ref, wo_ref, pg_ref, out_ref, s_ref):
    width = wo_ref.shape[0]
    tm = x_ref.shape[1]
    lane = lax.broadcasted_iota(jnp.int32, (1, LANES), 1)
    lo = lane < HALF
    x = x_ref[0]
    hb = _rms(x, g_ref[...]).astype(BF16)
    v = jax.nn.gelu(_dot(hb, w_ref[:, width:2 * width]))
    mu = jnp.mean(v, axis=-1, keepdims=True)
    vc = v - mu
    var = jnp.mean(vc * vc, axis=-1, keepdims=True)
    vb = (vc * lax.rsqrt(var + EPS) * lng_ref[...] + lnb_ref[...]).astype(BF16)
    nchunk = tm // SGU_CHUNK
    for jj in range(width // LANES):
        blk = jnp.concatenate([vb[c * SGU_CHUNK:(c + 1) * SGU_CHUNK, jj * LANES:(jj + 1) * LANES]
                               for c in range(nchunk)], axis=1)
        r = _dot(ws_ref[jj], blk)
        for c in range(nchunk):
            s_ref[c * SGU_CHUNK:(c + 1) * SGU_CHUNK, jj * LANES:(jj + 1) * LANES] = (
                jnp.where(lo, r[:SGU_CHUNK, c * LANES:(c + 1) * LANES],
                          r[SGU_CHUNK:, c * LANES:(c + 1) * LANES]) + bs_ref[jj])
    u = jax.nn.gelu(_dot(hb, w_ref[:, :width]))
    z = _dot(hb, w_ref[:, 2 * width:])
    o = u * s_ref[...] * jax.nn.silu(z)
    r = _dot(o.astype(BF16), wo_ref[...])
    out_ref[0] = x + _rms(r, pg_ref[...])


def _sgu_layer(x, pre_g, post_g, w_in, ln_g, ln_b, w_s, b_s, w_out):
    bsz, L, d = x.shape
    width = w_out.shape[0]
    T = SGU_CHUNK
    gd = width // SGU_GROUPS
    tril = jnp.tril(jnp.ones((T, T), dtype=bool))
    ws = jnp.where(tril[None], w_s, 0.0).reshape(SGU_GROUPS // 2, 2 * T, T).astype(BF16)
    bs = jnp.repeat(b_s.astype(F32).T, gd, axis=1)
    bs = bs.reshape(T, width // LANES, LANES).transpose(1, 0, 2)
    tm = 512
    return pl.pallas_call(
        _sgu_kernel,
        out_shape=jax.ShapeDtypeStruct(x.shape, x.dtype),
        grid=(bsz, L // tm),
        in_specs=[pl.BlockSpec((1, tm, d), lambda b, i: (b, i, 0)),
                  _full((1, d)), _full(w_in.shape), _full((1, width)), _full((1, width)),
                  _full(ws.shape), _full(bs.shape), _full(w_out.shape), _full((1, d))],
        out_specs=pl.BlockSpec((1, tm, d), lambda b, i: (b, i, 0)),
        scratch_shapes=[pltpu.VMEM((tm, width), F32)],
        compiler_params=_cparams(("parallel", "parallel")),
        name="sgu",
    )(x, pre_g.reshape(1, d), w_in.astype(BF16), ln_g.reshape(1, width), ln_b.reshape(1, width),
      ws, bs, w_out.astype(BF16), post_g.reshape(1, d))


def kernel(x, pre_norm, post_norm, rel_bias, a_w_in, a_lam_re, a_lam_im, a_log_dt, a_b_re, a_b_im, a_c_re, a_c_im, a_d, a_w_glu, a_b_glu, a_w_out, b_w_in, b_sinks, b_w_out, c_w_in, c_q_norm, c_kv_norm, c_w_uq, c_w_ukv, c_w_out, d_w_in, d_ln_g, d_ln_b, d_w_s, d_b_s, d_w_out):
    depth = pre_norm.shape[0]
    for i in range(depth):
        kind, j = i % 4, i // 4
        if kind == 0:
            x = _s5_layer(x, pre_norm[i], post_norm[i], a_w_in[j], a_lam_re[j], a_lam_im[j], a_log_dt[j],
                          a_b_re[j], a_b_im[j], a_c_re[j], a_c_im[j], a_d[j], a_w_glu[j], a_b_glu[j],
                          a_w_out[j])
        elif kind == 1:
            x = _swa_layer(x, pre_norm[i], post_norm[i], b_w_in[j], b_sinks[j], b_w_out[j], rel_bias)
        elif kind == 2:
            x = _mla_layer(x, pre_norm[i], post_norm[i], c_w_in[j], c_q_norm[j], c_kv_norm[j], c_w_uq[j],
                           c_w_ukv[j], c_w_out[j])
        else:
            x = _sgu_layer(x, pre_norm[i], post_norm[i], d_w_in[j], d_ln_g[j], d_ln_b[j], d_w_s[j],
                           d_b_s[j], d_w_out[j])
    return x
```

```python
import functools
import math

import jax
import jax.numpy as jnp
import numpy as np
from jax import lax
from jax.experimental import pallas as pl
from jax.experimental.pallas import tpu as pltpu

F32 = jnp.float32
BF16 = jnp.bfloat16

D_MODEL = 1024
EPS = 1e-6
NEG_INF = -1e30
LANES = 128
HALF = LANES // 2

SSM_GROUP = 16
SSM_STATE = 64
S5_CH_BLOCK = LANES
S5_GROUPS_PER_BLOCK = S5_CH_BLOCK // SSM_GROUP
S5_STATE_BLOCK = S5_GROUPS_PER_BLOCK * SSM_STATE
S5_T = 64

HEAD_DIM = 64
SWA_HEADS = 16
SWA_KV_HEADS = 2
SWA_GROUP = SWA_HEADS // SWA_KV_HEADS
WINDOW = 128
SWA_WINDOWS_PER_STEP = 4
SWA_LOOKAHEAD = 2
SWA_UNIT_HEADS = 8
SWA_PV_DELAY = 1
REL_BUCKETS = 32
REL_MAX_DIST = 128

MLA_HEADS = 16
MLA_NOPE = 64
MLA_ROPE = 32
MLA_V = 64
MLA_KV_RANK = 256
MLA_Q_RANK = 768
ROPE_BASE = 10000.0
MLA_TQ = 256
MLA_TK = 256
MLA_LOOKAHEAD = 4
MLA_PV_DELAY = 2

SGU_CHUNK = 128
SGU_GROUPS = 16

VMEM_LIMIT = 56 * 1024 * 1024


def _cparams(sem):
    return pltpu.CompilerParams(dimension_semantics=sem, vmem_limit_bytes=VMEM_LIMIT)


def _rms(x, g):
    return x * lax.rsqrt(jnp.mean(x * x, axis=-1, keepdims=True) + EPS) * g


def _dot(a, b):
    return jnp.dot(a, b, preferred_element_type=F32)


def _dot_nt(a, b):
    return lax.dot_general(a, b, (((1,), (1,)), ((), ())), preferred_element_type=F32)


def _full(shape):
    n = len(shape)
    return pl.BlockSpec(shape, lambda *_: (0,) * n, pipeline_mode=pl.Buffered(1))


def _s5_kernel(x_ref, g_ref, w_ref, perm_ref, permt_ref, bb_ref, cc_ref, ar_ref, ai_ref, d_ref,
               wg_ref, bg_ref, wo_ref, pg_ref, out_ref, u_ref, z_ref, y_ref, s_ref, carry_ref, *, tt):
    bsz = x_ref.shape[0]
    width = wg_ref.shape[0]
    rows = bsz * tt
    nblk = bb_ref.shape[0]
    sb = S5_STATE_BLOCK

    @pl.when(pl.program_id(0) == 0)
    def _():
        carry_ref[...] = jnp.zeros_like(carry_ref)

    x = x_ref[...].reshape(rows, x_ref.shape[2])
    hb = _rms(x, g_ref[...]).astype(BF16)
    hb = _dot(perm_ref[...], hb).astype(BF16)
    u_ref[...] = _dot(hb, w_ref[:, :width])
    z_ref[...] = _dot(hb, w_ref[:, width:])

    nbuf = s_ref.shape[0]

    def project_in(i):
        s_ref[i % nbuf] = _dot(u_ref[:, i * LANES:(i + 1) * LANES].astype(BF16), bb_ref[i])

    def project_out(i):
        ub = u_ref[:, i * LANES:(i + 1) * LANES]
        y = _dot(s_ref[i % nbuf].astype(BF16), cc_ref[i]) + d_ref[:, i * LANES:(i + 1) * LANES] * ub
        y_ref[:, i * LANES:(i + 1) * LANES] = jax.nn.gelu(y)

    project_in(0)
    for i in range(nblk):
        if i + 1 < nblk:
            project_in(i + 1)
        buf = s_ref.at[i % nbuf]
        ar = ar_ref[i]
        ai = ai_ref[i]
        sr = carry_ref[i, :, 0:sb]
        si = carry_ref[i, :, sb:2 * sb]
        for t in range(tt):
            r0 = t * bsz
            nr = ar * sr - ai * si + buf[r0:r0 + bsz, 0:sb]
            ni = ar * si + ai * sr + buf[r0:r0 + bsz, sb:2 * sb]
            buf[r0:r0 + bsz, 0:sb] = nr
            buf[r0:r0 + bsz, sb:2 * sb] = ni
            sr, si = nr, ni
        carry_ref[i, :, 0:sb] = sr
        carry_ref[i, :, sb:2 * sb] = si
        project_out(i)

    y = y_ref[...]
    gate = jax.nn.sigmoid(_dot(y.astype(BF16), wg_ref[...]) + bg_ref[...])
    o = y * gate * jax.nn.silu(z_ref[...])
    ob = _dot(permt_ref[...], o.astype(BF16)).astype(BF16)
    r = _dot(ob, wo_ref[...])
    out_ref[...] = (x + _rms(r, pg_ref[...])).reshape(out_ref.shape)


def _s5_discretize(lam_re, lam_im, log_dt, b_re, b_im):
    dt = jnp.exp(log_dt)[:, None]
    mag = jnp.exp(lam_re * dt)
    ab_re = mag * jnp.cos(lam_im * dt)
    ab_im = mag * jnp.sin(lam_im * dt)
    den = lam_re * lam_re + lam_im * lam_im
    nr = ab_re - 1.0
    f_re = (nr * lam_re + ab_im * lam_im) / den
    f_im = (ab_im * lam_re - nr * lam_im) / den
    bb_re = f_re[..., None] * b_re - f_im[..., None] * b_im
    bb_im = f_re[..., None] * b_im + f_im[..., None] * b_re
    return ab_re, ab_im, bb_re, bb_im


def _s5_layer(x, pre_g, post_g, w_in, lam_re, lam_im, log_dt, b_re, b_im, c_re, c_im, d_skip,
              w_glu, b_glu, w_out):
    bsz, L, d = x.shape
    width = w_in.shape[1] // 2
    nblk = width // S5_CH_BLOCK
    gpb = S5_GROUPS_PER_BLOCK
    tt = S5_T
    rows = bsz * tt

    src = (np.arange(rows) % bsz) * tt + np.arange(rows) // bsz
    perm_np = np.zeros((rows, rows), np.float32)
    perm_np[np.arange(rows), src] = 1.0
    perm = jnp.asarray(perm_np, BF16)
    perm_t = jnp.asarray(perm_np.T, BF16)

    ab_re, ab_im, bb_re, bb_im = _s5_discretize(lam_re, lam_im, log_dt, b_re, b_im)
    eye = jnp.eye(gpb, dtype=F32)

    def pack_b(bb):
        t = bb.reshape(nblk, gpb, SSM_STATE, SSM_GROUP)
        return jnp.einsum('igph,gk->ikhgp', t, eye).reshape(nblk, S5_CH_BLOCK, S5_STATE_BLOCK)

    def pack_c(cc):
        t = cc.reshape(nblk, gpb, SSM_GROUP, SSM_STATE)
        return jnp.einsum('ighp,gk->igpkh', t, eye).reshape(nblk, S5_STATE_BLOCK, S5_CH_BLOCK)

    bb = jnp.concatenate([pack_b(bb_re), pack_b(bb_im)], axis=2).astype(BF16)
    cc = jnp.concatenate([pack_c(c_re), -pack_c(c_im)], axis=1).astype(BF16)
    ar = jnp.broadcast_to(ab_re.reshape(nblk, 1, S5_STATE_BLOCK), (nblk, bsz, S5_STATE_BLOCK))
    ai = jnp.broadcast_to(ab_im.reshape(nblk, 1, S5_STATE_BLOCK), (nblk, bsz, S5_STATE_BLOCK))

    xspec = pl.BlockSpec((bsz, tt, d), lambda i: (0, i, 0))
    return pl.pallas_call(
        functools.partial(_s5_kernel, tt=tt),
        out_shape=jax.ShapeDtypeStruct(x.shape, x.dtype),
        grid=(L // tt,),
        in_specs=[xspec, _full((1, d)), _full(w_in.shape), _full(perm.shape), _full(perm_t.shape),
                  _full(bb.shape), _full(cc.shape), _full(ar.shape), _full(ai.shape), _full((1, width)),
                  _full(w_glu.shape), _full((1, width)), _full(w_out.shape), _full((1, d))],
        out_specs=xspec,
        scratch_shapes=[pltpu.VMEM((rows, width), F32),
                        pltpu.VMEM((rows, width), F32),
                        pltpu.VMEM((rows, width), F32),
                        pltpu.VMEM((2, rows, 2 * S5_STATE_BLOCK), F32),
                        pltpu.VMEM((nblk, bsz, 2 * S5_STATE_BLOCK), F32)],
        compiler_params=_cparams(("arbitrary",)),
        name="s5_layer",
    )(x, pre_g.reshape(1, d), w_in.astype(BF16), perm, perm_t, bb, cc, ar, ai, d_skip.reshape(1, width),
      w_glu.astype(BF16), b_glu.reshape(1, width), w_out.astype(BF16), post_g.reshape(1, d))


def _swa_bias(rel_bias):
    W = WINDOW
    n = 4 * W
    dist = 2 * W - jnp.arange(n)
    valid = jnp.logical_and(dist >= 0, dist < W)
    dpos = jnp.maximum(dist, 0)
    max_exact = REL_BUCKETS // 2
    dist_f = jnp.maximum(dpos, 1).astype(F32)
    large = max_exact + (jnp.log(dist_f / max_exact) / math.log(REL_MAX_DIST / max_exact)
                         * (REL_BUCKETS - max_exact)).astype(jnp.int32)
    large = jnp.minimum(large, REL_BUCKETS - 1)
    bucket = jnp.where(dpos < max_exact, dpos, large)
    vec = jnp.where(valid[:, None], rel_bias[bucket].astype(F32), NEG_INF).T
    skew = jnp.tile(vec, (1, W))[:, :W * (n - 1)].reshape(vec.shape[0], W, n - 1)
    return skew[:, :, W:3 * W]


def _swa_pre_kernel(x_ref, g_ref, wqt_ref, wk_ref, wvt_ref, wz_ref, qt_ref, k_ref, vt_ref, z_ref, *, scale):
    hb = _rms(x_ref[0], g_ref[...]).astype(BF16)
    qt_ref[0] = (_dot_nt(wqt_ref[...], hb) * scale).astype(BF16)
    k_ref[0] = _dot(hb, wk_ref[...]).astype(BF16)
    vt_ref[0] = _dot_nt(wvt_ref[...], hb).astype(BF16)
    z_ref[0] = _dot(hb, wz_ref[...])


def _swa_kernel(qt_ref, kp_ref, kc_ref, vtp_ref, vtc_ref, bias_ref, sink_ref, z_ref, x_ref, wo_ref, g_ref,
                out_ref, ot_ref):
    W = WINDOW
    nwin = qt_ref.shape[2] // W
    step = pl.program_id(1)
    kall = jnp.concatenate([kp_ref[0], kc_ref[0]], axis=0)
    vtall = jnp.concatenate([vtp_ref[0], vtc_ref[0]], axis=1)
    nsub = SWA_UNIT_HEADS
    zq = jnp.zeros((HEAD_DIM, nsub * W), BF16)
    ones = jnp.ones((16, 2 * W), BF16)
    units = [(w, h, c) for w in range(nwin) for h in range(SWA_KV_HEADS) for c in range(SWA_GROUP // nsub)]

    def scores(w, h, c):
        hd0 = h * SWA_GROUP + c * nsub
        qh = jnp.concatenate([qt_ref[0, (hd0 + g) * HEAD_DIM:(hd0 + g + 1) * HEAD_DIM, w * W:(w + 1) * W]
                              for g in range(nsub)], axis=1)
        qz = jnp.concatenate([qh, zq] if h == 0 else [zq, qh], axis=0)
        return _dot(kall[w * W:(w + 2) * W], qz)

    pending = [scores(*u) for u in units[:SWA_LOOKAHEAD]]
    late = []

    def flush():
        (w, h, c), p, tail = late.pop(0)
        vones = jnp.concatenate([vtall[h * HEAD_DIM:(h + 1) * HEAD_DIM, w * W:(w + 2) * W], ones], axis=0)
        o = _dot(vones, p)
        oh = o[:HEAD_DIM] * (1.0 / (o[HEAD_DIM:HEAD_DIM + 1] + tail))
        for g in range(nsub):
            hd = h * SWA_GROUP + c * nsub + g
            ot_ref[hd * HEAD_DIM:(hd + 1) * HEAD_DIM, w * W:(w + 1) * W] = oh[:, g * W:(g + 1) * W]

    for idx, (w, h, c) in enumerate(units):
        raw = pending.pop(0)
        if idx + SWA_LOOKAHEAD < len(units):
            pending.append(scores(*units[idx + SWA_LOOKAHEAD]))
        cols = slice(c * nsub * W, (c + 1) * nsub * W)
        variant = (step == 0).astype(jnp.int32) if w == 0 else 0
        s = raw + bias_ref[variant, h, :, cols]
        sink = sink_ref[h, :, cols]
        m = jnp.maximum(jnp.max(s, axis=0, keepdims=True), sink)
        if len(late) == SWA_PV_DELAY:
            flush()
        late.append(((w, h, c), jnp.exp2(s - m).astype(BF16), jnp.exp2(sink - m)))
    while late:
        flush()
    gated = ot_ref[...].T * jax.nn.silu(z_ref[0])
    r = _dot(gated.astype(BF16), wo_ref[...])
    out_ref[0] = x_ref[0] + _rms(r, g_ref[...])


def _swa_layer(x, pre_g, post_g, w_in, sinks, w_out, rel_bias):
    bsz, L, d = x.shape
    width = SWA_HEADS * HEAD_DIM
    kvw = SWA_KV_HEADS * HEAD_DIM
    W = WINDOW
    nb = L // W
    log2e = math.log2(math.e)
    tm = 512
    tok = lambda w_: pl.BlockSpec((1, tm, w_), lambda b, i: (b, i, 0))
    tokt = lambda w_: pl.BlockSpec((1, w_, tm), lambda b, i: (b, 0, i))
    wqt = w_in[:, :width].T.astype(BF16)
    wk = w_in[:, width:width + kvw].astype(BF16)
    wvt = w_in[:, width + kvw:width + 2 * kvw].T.astype(BF16)
    wz = w_in[:, width + 2 * kvw:].astype(BF16)
    qt, k, vt, z = pl.pallas_call(
        functools.partial(_swa_pre_kernel, scale=HEAD_DIM ** -0.5 * log2e),
        out_shape=[jax.ShapeDtypeStruct((bsz, width, L), BF16),
                   jax.ShapeDtypeStruct((bsz, L, kvw), BF16),
                   jax.ShapeDtypeStruct((bsz, kvw, L), BF16),
                   jax.ShapeDtypeStruct((bsz, L, width), F32)],
        grid=(bsz, L // tm),
        in_specs=[tok(d), _full((1, d)), _full(wqt.shape), _full(wk.shape), _full(wvt.shape), _full(wz.shape)],
        out_specs=[tokt(width), tok(kvw), tokt(kvw), tok(width)],
        compiler_params=_cparams(("parallel", "parallel")),
        name="swa_pre",
    )(x, pre_g.reshape(1, d), wqt, wk, wvt, wz)

    bias = jnp.transpose(_swa_bias(rel_bias.astype(F32) * log2e), (0, 2, 1))
    has_prev = (jnp.arange(2 * W) >= W)[None, :, None]
    variants = [bias,
                jnp.where(has_prev, bias, NEG_INF)]
    bias_t = jnp.stack([v.reshape(SWA_KV_HEADS, SWA_GROUP, 2 * W, W).transpose(0, 2, 1, 3)
                        .reshape(SWA_KV_HEADS, 2 * W, SWA_GROUP * W) for v in variants])
    sink = jnp.repeat(sinks.astype(F32) * log2e, W).reshape(SWA_KV_HEADS, 1, SWA_GROUP * W)

    nwin = SWA_WINDOWS_PER_STEP
    tq = nwin * W
    prev = lambda n: jnp.maximum(n * nwin - 1, 0)
    return pl.pallas_call(
        _swa_kernel,
        out_shape=jax.ShapeDtypeStruct(x.shape, x.dtype),
        grid=(bsz, L // tq),
        in_specs=[pl.BlockSpec((1, width, tq), lambda b, n: (b, 0, n)),
                  pl.BlockSpec((1, W, kvw), lambda b, n: (b, prev(n), 0)),
                  pl.BlockSpec((1, tq, kvw), lambda b, n: (b, n, 0)),
                  pl.BlockSpec((1, kvw, W), lambda b, n: (b, 0, prev(n))),
                  pl.BlockSpec((1, kvw, tq), lambda b, n: (b, 0, n)),
                  _full(bias_t.shape), _full(sink.shape),
                  pl.BlockSpec((1, tq, width), lambda b, n: (b, n, 0)),
                  pl.BlockSpec((1, tq, d), lambda b, n: (b, n, 0)),
                  _full(w_out.shape), _full((1, d))],
        out_specs=pl.BlockSpec((1, tq, d), lambda b, n: (b, n, 0)),
        scratch_shapes=[pltpu.VMEM((width, tq), F32)],
        compiler_params=_cparams(("parallel", "parallel")),
        name="swa_attn",
    )(qt, k, k, vt, vt, bias_t, sink, z, x, w_out.astype(BF16), post_g.reshape(1, d))


def _mla_pre_kernel(x_ref, g_ref, w_ref, qn_ref, kvn_ref, wq_ref, wkv_ref, wvt_ref, cq_ref, sq_ref, ck_ref, sk_ref,
                    oqn_ref, oqr_ref, okn_ref, okr_ref, ov_ref, oz_ref, *, scale):
    nope = MLA_HEADS * MLA_NOPE
    rope = MLA_HEADS * MLA_ROPE
    vw = MLA_HEADS * MLA_V
    hb = _rms(x_ref[0], g_ref[...]).astype(BF16)
    o1 = MLA_Q_RANK
    o2 = o1 + MLA_KV_RANK
    o3 = o2 + vw
    cq = _dot(hb, w_ref[:, :o1])
    ckv = _dot(hb, w_ref[:, o1:o2])
    oz_ref[0] = _dot(hb, w_ref[:, o2:o3])
    kr = _dot(hb, w_ref[:, o3:o3 + LANES])
    krs = _dot(hb, w_ref[:, o3 + LANES:o3 + 2 * LANES])
    okr_ref[0] = (kr * ck_ref[...] + krs * sk_ref[...]).astype(BF16)
    cqb = _rms(cq, qn_ref[...]).astype(BF16)
    oqn_ref[0] = (_dot_nt(wq_ref[:nope], cqb) * scale).astype(BF16)
    qr = _dot_nt(wq_ref[nope:nope + rope], cqb)
    hr = MLA_ROPE // 2
    qrs = jnp.concatenate([qr[h * MLA_ROPE + o:h * MLA_ROPE + o + hr]
                           for h in range(MLA_HEADS) for o in (hr, 0)], axis=0)
    oqr_ref[0] = ((qr * cq_ref[...] + qrs * sq_ref[...]) * scale).astype(BF16)
    ckb = _rms(ckv, kvn_ref[...]).astype(BF16)
    okn_ref[0] = _dot(ckb, wkv_ref[:, :nope]).astype(BF16)
    vt = _dot_nt(wvt_ref[...], ckb).astype(BF16)
    tk = ov_ref.shape[3]
    for c in range(ov_ref.shape[1]):
        ov_ref[0, c] = vt[:, c * tk:(c + 1) * tk]


def _mla_attn_kernel(qn_ref, qr_ref, kn_ref, kr_ref, v_ref, z_ref, x_ref, wo_ref, g_ref, out_ref,
                     qs_ref, acc_ref, m_ref, o_ref):
    tq = qn_ref.shape[2]
    tk = v_ref.shape[3]
    npairs = MLA_HEADS // 2
    i = pl.program_id(1)
    krow = lax.broadcasted_iota(jnp.int32, (tk, 2 * tq), 0)
    qcol = lax.broadcasted_iota(jnp.int32, (tk, 2 * tq), 1)
    causal = krow <= jnp.where(qcol >= tq, qcol - tq, qcol)

    zn = jnp.zeros((MLA_NOPE, tq), BF16)
    zr = jnp.zeros((LANES - MLA_ROPE, tq), BF16)
    for p in range(npairs):
        qn = qn_ref[0, p * LANES:(p + 1) * LANES, :]
        r0 = 2 * p * MLA_ROPE
        c0 = jnp.concatenate([qn[:MLA_NOPE], zn, qr_ref[0, r0:r0 + MLA_ROPE, :], zr], axis=0)
        c1 = jnp.concatenate([zn, qn[MLA_NOPE:], qr_ref[0, r0 + MLA_ROPE:r0 + 2 * MLA_ROPE, :], zr], axis=0)
        qs_ref[p] = jnp.concatenate([c0, c1], axis=1)

    m_ref[...] = jnp.full(m_ref.shape, NEG_INF, F32)
    acc_ref[...] = jnp.zeros(acc_ref.shape, F32)
    ones = jnp.ones((acc_ref.shape[1] - LANES, tk), BF16)

    def kv_step(j, masked):
        ks = pl.multiple_of(j * tk, tk)
        kr = kr_ref[0, pl.ds(ks, tk), :]

        def scores(p):
            kc = jnp.concatenate([kn_ref[0, pl.ds(ks, tk), p * LANES:(p + 1) * LANES], kr], axis=1)
            return [_dot(kc, qs_ref[p, :, c * 2 * LANES:(c + 1) * 2 * LANES]) for c in range(tq // LANES)]

        pending = [scores(p) for p in range(MLA_LOOKAHEAD)]
        late = []

        def flush():
            pp, alpha, pr = late.pop(0)
            vones = jnp.concatenate([v_ref[0, j, pp * LANES:(pp + 1) * LANES, :], ones], axis=0)
            acc_ref[pp] = alpha * acc_ref[pp] + _dot(vones, pr)

        for p in range(npairs):
            s = pending.pop(0)
            if p + MLA_LOOKAHEAD < npairs:
                pending.append(scores(p + MLA_LOOKAHEAD))
            probs, alphas = [], []
            for c in range(2 * tq // LANES):
                sc = s[c // 2][:, (c % 2) * LANES:(c % 2 + 1) * LANES]
                if masked:
                    sc = jnp.where(causal[:, c * LANES:(c + 1) * LANES], sc, NEG_INF)
                m_prev = m_ref[p, :, c * LANES:(c + 1) * LANES]
                m_new = jnp.maximum(m_prev, jnp.max(sc, axis=0, keepdims=True))
                alphas.append(jnp.exp2(m_prev - m_new))
                probs.append(jnp.exp2(sc - m_new).astype(BF16))
                m_ref[p, :, c * LANES:(c + 1) * LANES] = m_new
            if len(late) == MLA_PV_DELAY:
                flush()
            late.append((p, jnp.concatenate(alphas, axis=1), jnp.concatenate(probs, axis=1)))
        while late:
            flush()

    def body(j, c):
        kv_step(j, False)
        return c

    lax.fori_loop(0, i, body, 0)
    kv_step(i, True)
    for p in range(npairs):
        a = acc_ref[p]
        a = a[:LANES] * (1.0 / a[LANES:LANES + 1])
        ot = jnp.concatenate([a[:MLA_V, :tq], a[MLA_V:, tq:]], axis=0)
        o_ref[:, p * LANES:(p + 1) * LANES] = ot.T
    gated = o_ref[...] * jax.nn.silu(z_ref[0])
    r = _dot(gated.astype(BF16), wo_ref[...])
    out_ref[0] = x_ref[0] + _rms(r, g_ref[...])


def _mla_layer(x, pre_g, post_g, w_in, q_norm, kv_norm, w_uq, w_ukv, w_out):
    bsz, L, d = x.shape
    H = MLA_HEADS
    dq = MLA_NOPE + MLA_ROPE
    nope = H * MLA_NOPE
    rope = H * MLA_ROPE
    vw = H * MLA_V
    half = MLA_ROPE // 2
    o_kr = MLA_Q_RANK + MLA_KV_RANK
    o_z = o_kr + MLA_ROPE
    w_kr = w_in[:, o_kr:o_z]
    w_krs = jnp.concatenate([w_kr[:, half:], w_kr[:, :half]], axis=1)
    reps = LANES // MLA_ROPE
    w1 = jnp.concatenate([w_in[:, :o_kr], w_in[:, o_z:]] + [w_kr] * reps + [w_krs] * reps, axis=1).astype(BF16)
    wq3 = w_uq.reshape(MLA_Q_RANK, H, dq)
    wqt = jnp.concatenate([wq3[:, :, :MLA_NOPE].reshape(MLA_Q_RANK, nope),
                           wq3[:, :, MLA_NOPE:].reshape(MLA_Q_RANK, rope)], axis=1).T.astype(BF16)
    wkv3 = w_ukv.reshape(MLA_KV_RANK, H, MLA_NOPE + MLA_V)
    wkn = wkv3[:, :, :MLA_NOPE].reshape(MLA_KV_RANK, nope).astype(BF16)
    wvt = wkv3[:, :, MLA_NOPE:].reshape(MLA_KV_RANK, vw).T.astype(BF16)
    inv = ROPE_BASE ** (-jnp.arange(0, MLA_ROPE, 2, dtype=F32) / MLA_ROPE)
    ang = jnp.arange(L, dtype=F32)[:, None] * inv[None, :]
    cos, sin = jnp.cos(ang), jnp.sin(ang)
    cos32 = jnp.concatenate([cos, cos], axis=1)
    sin32 = jnp.concatenate([-sin, sin], axis=1)
    cos_k, sin_k = jnp.tile(cos32, (1, LANES // MLA_ROPE)), jnp.tile(sin32, (1, LANES // MLA_ROPE))
    cos_q, sin_q = jnp.tile(cos32, (1, H)).T, jnp.tile(sin32, (1, H)).T

    tm = 512
    tk = MLA_TK
    tok = lambda w_: pl.BlockSpec((1, tm, w_), lambda b, i: (b, i, 0))
    tokt = lambda w_: pl.BlockSpec((1, w_, tm), lambda b, i: (b, 0, i))
    scale = dq ** -0.5 * math.log2(math.e)
    qn, qr, kn, kr, v, z = pl.pallas_call(
        functools.partial(_mla_pre_kernel, scale=scale),
        out_shape=[jax.ShapeDtypeStruct((bsz, nope, L), BF16),
                   jax.ShapeDtypeStruct((bsz, rope, L), BF16),
                   jax.ShapeDtypeStruct((bsz, L, nope), BF16),
                   jax.ShapeDtypeStruct((bsz, L, LANES), BF16),
                   jax.ShapeDtypeStruct((bsz, L // tk, vw, tk), BF16),
                   jax.ShapeDtypeStruct((bsz, L, vw), F32)],
        grid=(bsz, L // tm),
        in_specs=[tok(d), _full((1, d)), _full(w1.shape), _full((1, MLA_Q_RANK)), _full((1, MLA_KV_RANK)),
                  _full(wqt.shape), _full(wkn.shape), _full(wvt.shape),
                  pl.BlockSpec((rope, tm), lambda b, i: (0, i)), pl.BlockSpec((rope, tm), lambda b, i: (0, i)),
                  pl.BlockSpec((tm, LANES), lambda b, i: (i, 0)), pl.BlockSpec((tm, LANES), lambda b, i: (i, 0))],
        out_specs=[tokt(nope), tokt(rope), tok(nope), tok(LANES),
                   pl.BlockSpec((1, tm // tk, vw, tk), lambda b, i: (b, i, 0, 0)), tok(vw)],
        compiler_params=_cparams(("parallel", "parallel")),
        name="mla_pre",
    )(x, pre_g.reshape(1, d), w1, q_norm.reshape(1, -1), kv_norm.reshape(1, -1), wqt, wkn, wvt,
      cos_q, sin_q, cos_k, sin_k)

    tq = MLA_TQ
    npairs = H // 2
    qspec = lambda w_: pl.BlockSpec((1, w_, tq), lambda b, i: (b, 0, i))
    kspec = lambda w_: pl.BlockSpec((1, L, w_), lambda b, i: (b, 0, 0))
    rowspec = lambda w_: pl.BlockSpec((1, tq, w_), lambda b, i: (b, i, 0))
    return pl.pallas_call(
        _mla_attn_kernel,
        out_shape=jax.ShapeDtypeStruct(x.shape, x.dtype),
        grid=(bsz, L // tq),
        in_specs=[qspec(nope), qspec(rope), kspec(nope), kspec(LANES),
                  pl.BlockSpec((1, L // tk, vw, tk), lambda b, i: (b, 0, 0, 0)),
                  rowspec(vw), rowspec(d), _full(w_out.shape), _full((1, d))],
        out_specs=rowspec(d),
        scratch_shapes=[pltpu.VMEM((npairs, 2 * LANES, 2 * tq), BF16),
                        pltpu.VMEM((npairs, LANES + 16, 2 * tq), F32),
                        pltpu.VMEM((npairs, 1, 2 * tq), F32),
                        pltpu.VMEM((tq, vw), F32)],
        compiler_params=_cparams(("parallel", "arbitrary")),
        name="mla_attn",
    )(qn, qr, kn, kr, v, z, x, w_out.astype(BF16), post_g.reshape(1, d))


def _sgu_kernel(x_ref, g_ref, w_ref, lng_ref, lnb_ref, ws_ref, bs_ref, wo_ref, pg_ref, out_ref, s_ref):
    width = wo_ref.shape[0]
    tm = x_ref.shape[1]
    lane = lax.broadcasted_iota(jnp.int32, (1, LANES), 1)
    lo = lane < HALF
    x = x_ref[0]
    hb = _rms(x, g_ref[...]).astype(BF16)
    v = jax.nn.gelu(_dot(hb, w_ref[:, width:2 * width]))
    mu = jnp.mean(v, axis=-1, keepdims=True)
    vc = v - mu
    var = jnp.mean(vc * vc, axis=-1, keepdims=True)
    vb = (vc * lax.rsqrt(var + EPS) * lng_ref[...] + lnb_ref[...]).astype(BF16)
    nchunk = tm // SGU_CHUNK
    for jj in range(width // LANES):
        blk = jnp.concatenate([vb[c * SGU_CHUNK:(c + 1) * SGU_CHUNK, jj * LANES:(jj + 1) * LANES]
                               for c in range(nchunk)], axis=1)
        r = _dot(ws_ref[jj], blk)
        for c in range(nchunk):
            s_ref[c * SGU_CHUNK:(c + 1) * SGU_CHUNK, jj * LANES:(jj + 1) * LANES] = (
                jnp.where(lo, r[:SGU_CHUNK, c * LANES:(c + 1) * LANES],
                          r[SGU_CHUNK:, c * LANES:(c + 1) * LANES]) + bs_ref[jj])
    u = jax.nn.gelu(_dot(hb, w_ref[:, :width]))
    z = _dot(hb, w_ref[:, 2 * width:])
    o = u * s_ref[...] * jax.nn.silu(z)
    r = _dot(o.astype(BF16), wo_ref[...])
    out_ref[0] = x + _rms(r, pg_ref[...])


def _sgu_layer(x, pre_g, post_g, w_in, ln_g, ln_b, w_s, b_s, w_out):
    bsz, L, d = x.shape
    width = w_out.shape[0]
    T = SGU_CHUNK
    gd = width // SGU_GROUPS
    tril = jnp.tril(jnp.ones((T, T), dtype=bool))
    ws = jnp.where(tril[None], w_s, 0.0).reshape(SGU_GROUPS // 2, 2 * T, T).astype(BF16)
    bs = jnp.repeat(b_s.astype(F32).T, gd, axis=1)
    bs = bs.reshape(T, width // LANES, LANES).transpose(1, 0, 2)
    tm = 512
    return pl.pallas_call(
        _sgu_kernel,
        out_shape=jax.ShapeDtypeStruct(x.shape, x.dtype),
        grid=(bsz, L // tm),
        in_specs=[pl.BlockSpec((1, tm, d), lambda b, i: (b, i, 0)),
                  _full((1, d)), _full(w_in.shape), _full((1, width)), _full((1, width)),
                  _full(ws.shape), _full(bs.shape), _full(w_out.shape), _full((1, d))],
        out_specs=pl.BlockSpec((1, tm, d), lambda b, i: (b, i, 0)),
        scratch_shapes=[pltpu.VMEM((tm, width), F32)],
        compiler_params=_cparams(("parallel", "parallel")),
        name="sgu",
    )(x, pre_g.reshape(1, d), w_in.astype(BF16), ln_g.reshape(1, width), ln_b.reshape(1, width),
      ws, bs, w_out.astype(BF16), post_g.reshape(1, d))


def kernel(x, pre_norm, post_norm, rel_bias, a_w_in, a_lam_re, a_lam_im, a_log_dt, a_b_re, a_b_im, a_c_re, a_c_im, a_d, a_w_glu, a_b_glu, a_w_out, b_w_in, b_sinks, b_w_out, c_w_in, c_q_norm, c_kv_norm, c_w_uq, c_w_ukv, c_w_out, d_w_in, d_ln_g, d_ln_b, d_w_s, d_b_s, d_w_out):
    depth = pre_norm.shape[0]
    for i in range(depth):
        kind, j = i % 4, i // 4
        if kind == 0:
            x = _s5_layer(x, pre_norm[i], post_norm[i], a_w_in[j], a_lam_re[j], a_lam_im[j], a_log_dt[j],
                          a_b_re[j], a_b_im[j], a_c_re[j], a_c_im[j], a_d[j], a_w_glu[j], a_b_glu[j],
                          a_w_out[j])
        elif kind == 1:
            x = _swa_layer(x, pre_norm[i], post_norm[i], b_w_in[j], b_sinks[j], b_w_out[j], rel_bias)
        elif kind == 2:
            x = _mla_layer(x, pre_norm[i], post_norm[i], c_w_in[j], c_q_norm[j], c_kv_norm[j], c_w_uq[j],
                           c_w_ukv[j], c_w_out[j])
        else:
            x = _sgu_layer(x, pre_norm[i], post_norm[i], d_w_in[j], d_ln_g[j], d_ln_b[j], d_w_s[j],
                           d_b_s[j], d_w_out[j])
    return x
```

```python
import functools
import math

import jax
import jax.numpy as jnp
import numpy as np
from jax import lax
from jax.experimental import pallas as pl
from jax.experimental.pallas import tpu as pltpu

F32 = jnp.float32
BF16 = jnp.bfloat16

D_MODEL = 1024
EPS = 1e-6
NEG_INF = -1e30
LANES = 128
HALF = LANES // 2

SSM_GROUP = 16
SSM_STATE = 64
S5_CH_BLOCK = LANES
S5_GROUPS_PER_BLOCK = S5_CH_BLOCK // SSM_GROUP
S5_STATE_BLOCK = S5_GROUPS_PER_BLOCK * SSM_STATE
S5_T = 64

HEAD_DIM = 64
SWA_HEADS = 16
SWA_KV_HEADS = 2
SWA_GROUP = SWA_HEADS // SWA_KV_HEADS
WINDOW = 128
SWA_WINDOWS_PER_STEP = 4
SWA_LOOKAHEAD = 2
SWA_UNIT_HEADS = 8
SWA_PV_DELAY = 1
REL_BUCKETS = 32
REL_MAX_DIST = 128

MLA_HEADS = 16
MLA_NOPE = 64
MLA_ROPE = 32
MLA_V = 64
MLA_KV_RANK = 256
MLA_Q_RANK = 768
ROPE_BASE = 10000.0
MLA_TQ = 256
MLA_TK = 256
MLA_LOOKAHEAD = 4
MLA_PV_DELAY = 2

SGU_CHUNK = 128
SGU_GROUPS = 16
SGU_STACK = 4

VMEM_LIMIT = 56 * 1024 * 1024


def _cparams(sem):
    return pltpu.CompilerParams(dimension_semantics=sem, vmem_limit_bytes=VMEM_LIMIT)


def _rms(x, g):
    return x * lax.rsqrt(jnp.mean(x * x, axis=-1, keepdims=True) + EPS) * g


def _dot(a, b):
    return jnp.dot(a, b, preferred_element_type=F32)


def _dot_nt(a, b):
    return lax.dot_general(a, b, (((1,), (1,)), ((), ())), preferred_element_type=F32)


def _full(shape):
    n = len(shape)
    return pl.BlockSpec(shape, lambda *_: (0,) * n, pipeline_mode=pl.Buffered(1))


def _s5_kernel(x_ref, g_ref, w_ref, perm_ref, permt_ref, bb_ref, cc_ref, ar_ref, ai_ref, d_ref,
               wg_ref, bg_ref, wo_ref, pg_ref, out_ref, u_ref, z_ref, y_ref, s_ref, carry_ref, *, tt):
    bsz = x_ref.shape[0]
    width = wg_ref.shape[0]
    rows = bsz * tt
    nblk = bb_ref.shape[0]
    sb = S5_STATE_BLOCK

    @pl.when(pl.program_id(0) == 0)
    def _():
        carry_ref[...] = jnp.zeros_like(carry_ref)

    x = x_ref[...].reshape(rows, x_ref.shape[2])
    hb = _rms(x, g_ref[...]).astype(BF16)
    hb = _dot(perm_ref[...], hb).astype(BF16)
    u_ref[...] = _dot(hb, w_ref[:, :width])
    z_ref[...] = _dot(hb, w_ref[:, width:])

    nbuf = s_ref.shape[0]

    def project_in(i):
        s_ref[i % nbuf] = _dot(u_ref[:, i * LANES:(i + 1) * LANES].astype(BF16), bb_ref[i])

    def project_out(i):
        ub = u_ref[:, i * LANES:(i + 1) * LANES]
        y = _dot(s_ref[i % nbuf].astype(BF16), cc_ref[i]) + d_ref[:, i * LANES:(i + 1) * LANES] * ub
        y_ref[:, i * LANES:(i + 1) * LANES] = jax.nn.gelu(y)

    project_in(0)
    for i in range(nblk):
        if i + 1 < nblk:
            project_in(i + 1)
        buf = s_ref.at[i % nbuf]
        ar = ar_ref[i]
        ai = ai_ref[i]
        sr = carry_ref[i, :, 0:sb]
        si = carry_ref[i, :, sb:2 * sb]
        for t in range(tt):
            r0 = t * bsz
            nr = ar * sr - ai * si + buf[r0:r0 + bsz, 0:sb]
            ni = ar * si + ai * sr + buf[r0:r0 + bsz, sb:2 * sb]
            buf[r0:r0 + bsz, 0:sb] = nr
            buf[r0:r0 + bsz, sb:2 * sb] = ni
            sr, si = nr, ni
        carry_ref[i, :, 0:sb] = sr
        carry_ref[i, :, sb:2 * sb] = si
        project_out(i)

    y = y_ref[...]
    gate = jax.nn.sigmoid(_dot(y.astype(BF16), wg_ref[...]) + bg_ref[...])
    o = y * gate * jax.nn.silu(z_ref[...])
    ob = _dot(permt_ref[...], o.astype(BF16)).astype(BF16)
    r = _dot(ob, wo_ref[...])
    out_ref[...] = (x + _rms(r, pg_ref[...])).reshape(out_ref.shape)


def _s5_discretize(lam_re, lam_im, log_dt, b_re, b_im):
    dt = jnp.exp(log_dt)[:, None]
    mag = jnp.exp(lam_re * dt)
    ab_re = mag * jnp.cos(lam_im * dt)
    ab_im = mag * jnp.sin(lam_im * dt)
    den = lam_re * lam_re + lam_im * lam_im
    nr = ab_re - 1.0
    f_re = (nr * lam_re + ab_im * lam_im) / den
    f_im = (ab_im * lam_re - nr * lam_im) / den
    bb_re = f_re[..., None] * b_re - f_im[..., None] * b_im
    bb_im = f_re[..., None] * b_im + f_im[..., None] * b_re
    return ab_re, ab_im, bb_re, bb_im


def _s5_layer(x, pre_g, post_g, w_in, lam_re, lam_im, log_dt, b_re, b_im, c_re, c_im, d_skip,
              w_glu, b_glu, w_out):
    bsz, L, d = x.shape
    width = w_in.shape[1] // 2
    nblk = width // S5_CH_BLOCK
    gpb = S5_GROUPS_PER_BLOCK
    tt = S5_T
    rows = bsz * tt

    src = (np.arange(rows) % bsz) * tt + np.arange(rows) // bsz
    perm_np = np.zeros((rows, rows), np.float32)
    perm_np[np.arange(rows), src] = 1.0
    perm = jnp.asarray(perm_np, BF16)
    perm_t = jnp.asarray(perm_np.T, BF16)

    ab_re, ab_im, bb_re, bb_im = _s5_discretize(lam_re, lam_im, log_dt, b_re, b_im)
    eye = jnp.eye(gpb, dtype=F32)

    def pack_b(bb):
        t = bb.reshape(nblk, gpb, SSM_STATE, SSM_GROUP)
        return jnp.einsum('igph,gk->ikhgp', t, eye).reshape(nblk, S5_CH_BLOCK, S5_STATE_BLOCK)

    def pack_c(cc):
        t = cc.reshape(nblk, gpb, SSM_GROUP, SSM_STATE)
        return jnp.einsum('ighp,gk->igpkh', t, eye).reshape(nblk, S5_STATE_BLOCK, S5_CH_BLOCK)

    bb = jnp.concatenate([pack_b(bb_re), pack_b(bb_im)], axis=2).astype(BF16)
    cc = jnp.concatenate([pack_c(c_re), -pack_c(c_im)], axis=1).astype(BF16)
    ar = jnp.broadcast_to(ab_re.reshape(nblk, 1, S5_STATE_BLOCK), (nblk, bsz, S5_STATE_BLOCK))
    ai = jnp.broadcast_to(ab_im.reshape(nblk, 1, S5_STATE_BLOCK), (nblk, bsz, S5_STATE_BLOCK))

    xspec = pl.BlockSpec((bsz, tt, d), lambda i: (0, i, 0))
    return pl.pallas_call(
        functools.partial(_s5_kernel, tt=tt),
        out_shape=jax.ShapeDtypeStruct(x.shape, x.dtype),
        grid=(L // tt,),
        in_specs=[xspec, _full((1, d)), _full(w_in.shape), _full(perm.shape), _full(perm_t.shape),
                  _full(bb.shape), _full(cc.shape), _full(ar.shape), _full(ai.shape), _full((1, width)),
                  _full(w_glu.shape), _full((1, width)), _full(w_out.shape), _full((1, d))],
        out_specs=xspec,
        scratch_shapes=[pltpu.VMEM((rows, width), F32),
                        pltpu.VMEM((rows, width), F32),
                        pltpu.VMEM((rows, width), F32),
                        pltpu.VMEM((2, rows, 2 * S5_STATE_BLOCK), F32),
                        pltpu.VMEM((nblk, bsz, 2 * S5_STATE_BLOCK), F32)],
        compiler_params=_cparams(("arbitrary",)),
        name="s5_layer",
    )(x, pre_g.reshape(1, d), w_in.astype(BF16), perm, perm_t, bb, cc, ar, ai, d_skip.reshape(1, width),
      w_glu.astype(BF16), b_glu.reshape(1, width), w_out.astype(BF16), post_g.reshape(1, d))


def _swa_bias(rel_bias):
    W = WINDOW
    n = 4 * W
    dist = 2 * W - jnp.arange(n)
    valid = jnp.logical_and(dist >= 0, dist < W)
    dpos = jnp.maximum(dist, 0)
    max_exact = REL_BUCKETS // 2
    dist_f = jnp.maximum(dpos, 1).astype(F32)
    large = max_exact + (jnp.log(dist_f / max_exact) / math.log(REL_MAX_DIST / max_exact)
                         * (REL_BUCKETS - max_exact)).astype(jnp.int32)
    large = jnp.minimum(large, REL_BUCKETS - 1)
    bucket = jnp.where(dpos < max_exact, dpos, large)
    vec = jnp.where(valid[:, None], rel_bias[bucket].astype(F32), NEG_INF).T
    skew = jnp.tile(vec, (1, W))[:, :W * (n - 1)].reshape(vec.shape[0], W, n - 1)
    return skew[:, :, W:3 * W]


def _swa_pre_kernel(x_ref, g_ref, wqt_ref, wk_ref, wvt_ref, wz_ref, qt_ref, k_ref, vt_ref, z_ref, *, scale):
    hb = _rms(x_ref[0], g_ref[...]).astype(BF16)
    qt_ref[0] = (_dot_nt(wqt_ref[...], hb) * scale).astype(BF16)
    k_ref[0] = _dot(hb, wk_ref[...]).astype(BF16)
    vt_ref[0] = _dot_nt(wvt_ref[...], hb).astype(BF16)
    z_ref[0] = _dot(hb, wz_ref[...])


def _swa_kernel(qt_ref, kp_ref, kc_ref, vtp_ref, vtc_ref, bias_ref, sink_ref, z_ref, x_ref, wo_ref, g_ref,
                out_ref, ot_ref):
    W = WINDOW
    nwin = qt_ref.shape[2] // W
    step = pl.program_id(1)
    kall = jnp.concatenate([kp_ref[0], kc_ref[0]], axis=0)
    vtall = jnp.concatenate([vtp_ref[0], vtc_ref[0]], axis=1)
    nsub = SWA_UNIT_HEADS
    zq = jnp.zeros((HEAD_DIM, nsub * W), BF16)
    ones = jnp.ones((16, 2 * W), BF16)
    units = [(w, h, c) for w in range(nwin) for h in range(SWA_KV_HEADS) for c in range(SWA_GROUP // nsub)]

    def scores(w, h, c):
        hd0 = h * SWA_GROUP + c * nsub
        qh = jnp.concatenate([qt_ref[0, (hd0 + g) * HEAD_DIM:(hd0 + g + 1) * HEAD_DIM, w * W:(w + 1) * W]
                              for g in range(nsub)], axis=1)
        qz = jnp.concatenate([qh, zq] if h == 0 else [zq, qh], axis=0)
        return _dot(kall[w * W:(w + 2) * W], qz)

    pending = [scores(*u) for u in units[:SWA_LOOKAHEAD]]
    late = []

    def flush():
        (w, h, c), p, tail = late.pop(0)
        vones = jnp.concatenate([vtall[h * HEAD_DIM:(h + 1) * HEAD_DIM, w * W:(w + 2) * W], ones], axis=0)
        o = _dot(vones, p)
        oh = o[:HEAD_DIM] * (1.0 / (o[HEAD_DIM:HEAD_DIM + 1] + tail))
        for g in range(nsub):
            hd = h * SWA_GROUP + c * nsub + g
            ot_ref[hd * HEAD_DIM:(hd + 1) * HEAD_DIM, w * W:(w + 1) * W] = oh[:, g * W:(g + 1) * W]

    for idx, (w, h, c) in enumerate(units):
        raw = pending.pop(0)
        if idx + SWA_LOOKAHEAD < len(units):
            pending.append(scores(*units[idx + SWA_LOOKAHEAD]))
        cols = slice(c * nsub * W, (c + 1) * nsub * W)
        variant = (step == 0).astype(jnp.int32) if w == 0 else 0
        s = raw + bias_ref[variant, h, :, cols]
        sink = sink_ref[h, :, cols]
        m = jnp.maximum(jnp.max(s, axis=0, keepdims=True), sink)
        if len(late) == SWA_PV_DELAY:
            flush()
        late.append(((w, h, c), jnp.exp2(s - m).astype(BF16), jnp.exp2(sink - m)))
    while late:
        flush()
    gated = ot_ref[...].T * jax.nn.silu(z_ref[0])
    r = _dot(gated.astype(BF16), wo_ref[...])
    out_ref[0] = x_ref[0] + _rms(r, g_ref[...])


def _swa_layer(x, pre_g, post_g, w_in, sinks, w_out, rel_bias):
    bsz, L, d = x.shape
    width = SWA_HEADS * HEAD_DIM
    kvw = SWA_KV_HEADS * HEAD_DIM
    W = WINDOW
    nb = L // W
    log2e = math.log2(math.e)
    tm = 1024
    tok = lambda w_: pl.BlockSpec((1, tm, w_), lambda b, i: (b, i, 0))
    tokt = lambda w_: pl.BlockSpec((1, w_, tm), lambda b, i: (b, 0, i))
    wb = w_in.astype(BF16)
    wqt = wb[:, :width].T
    wk = wb[:, width:width + kvw]
    wvt = wb[:, width + kvw:width + 2 * kvw].T
    wz = wb[:, width + 2 * kvw:]
    qt, k, vt, z = pl.pallas_call(
        functools.partial(_swa_pre_kernel, scale=HEAD_DIM ** -0.5 * log2e),
        out_shape=[jax.ShapeDtypeStruct((bsz, width, L), BF16),
                   jax.ShapeDtypeStruct((bsz, L, kvw), BF16),
                   jax.ShapeDtypeStruct((bsz, kvw, L), BF16),
                   jax.ShapeDtypeStruct((bsz, L, width), F32)],
        grid=(bsz, L // tm),
        in_specs=[tok(d), _full((1, d)), _full(wqt.shape), _full(wk.shape), _full(wvt.shape), _full(wz.shape)],
        out_specs=[tokt(width), tok(kvw), tokt(kvw), tok(width)],
        compiler_params=_cparams(("parallel", "parallel")),
        name="swa_pre",
    )(x, pre_g.reshape(1, d), wqt, wk, wvt, wz)

    bias = jnp.transpose(_swa_bias(rel_bias.astype(F32) * log2e), (0, 2, 1))
    has_prev = (jnp.arange(2 * W) >= W)[None, :, None]
    variants = [bias,
                jnp.where(has_prev, bias, NEG_INF)]
    bias_t = jnp.stack([v.reshape(SWA_KV_HEADS, SWA_GROUP, 2 * W, W).transpose(0, 2, 1, 3)
                        .reshape(SWA_KV_HEADS, 2 * W, SWA_GROUP * W) for v in variants])
    sink = jnp.repeat(sinks.astype(F32) * log2e, W).reshape(SWA_KV_HEADS, 1, SWA_GROUP * W)

    nwin = SWA_WINDOWS_PER_STEP
    tq = nwin * W
    prev = lambda n: jnp.maximum(n * nwin - 1, 0)
    return pl.pallas_call(
        _swa_kernel,
        out_shape=jax.ShapeDtypeStruct(x.shape, x.dtype),
        grid=(bsz, L // tq),
        in_specs=[pl.BlockSpec((1, width, tq), lambda b, n: (b, 0, n)),
                  pl.BlockSpec((1, W, kvw), lambda b, n: (b, prev(n), 0)),
                  pl.BlockSpec((1, tq, kvw), lambda b, n: (b, n, 0)),
                  pl.BlockSpec((1, kvw, W), lambda b, n: (b, 0, prev(n))),
                  pl.BlockSpec((1, kvw, tq), lambda b, n: (b, 0, n)),
                  _full(bias_t.shape), _full(sink.shape),
                  pl.BlockSpec((1, tq, width), lambda b, n: (b, n, 0)),
                  pl.BlockSpec((1, tq, d), lambda b, n: (b, n, 0)),
                  _full(w_out.shape), _full((1, d))],
        out_specs=pl.BlockSpec((1, tq, d), lambda b, n: (b, n, 0)),
        scratch_shapes=[pltpu.VMEM((width, tq), F32)],
        compiler_params=_cparams(("parallel", "parallel")),
        name="swa_attn",
    )(qt, k, k, vt, vt, bias_t, sink, z, x, w_out.astype(BF16), post_g.reshape(1, d))


def _mla_pre_kernel(x_ref, g_ref, w_ref, qn_ref, kvn_ref, wq_ref, wkv_ref, wvt_ref, cq_ref, sq_ref, ck_ref, sk_ref,
                    oqn_ref, oqr_ref, okn_ref, okr_ref, ov_ref, oz_ref, *, scale):
    nope = MLA_HEADS * MLA_NOPE
    rope = MLA_HEADS * MLA_ROPE
    vw = MLA_HEADS * MLA_V
    hb = _rms(x_ref[0], g_ref[...]).astype(BF16)
    o1 = MLA_Q_RANK
    o2 = o1 + MLA_KV_RANK
    o3 = o2 + vw
    cq = _dot(hb, w_ref[:, :o1])
    ckv = _dot(hb, w_ref[:, o1:o2])
    oz_ref[0] = _dot(hb, w_ref[:, o2:o3])
    kr = _dot(hb, w_ref[:, o3:o3 + LANES])
    krs = _dot(hb, w_ref[:, o3 + LANES:o3 + 2 * LANES])
    okr_ref[0] = (kr * ck_ref[...] + krs * sk_ref[...]).astype(BF16)
    cqb = _rms(cq, qn_ref[...]).astype(BF16)
    oqn_ref[0] = (_dot_nt(wq_ref[:nope], cqb) * scale).astype(BF16)
    qr = _dot_nt(wq_ref[nope:nope + rope], cqb)
    hr = MLA_ROPE // 2
    qrs = jnp.concatenate([qr[h * MLA_ROPE + o:h * MLA_ROPE + o + hr]
                           for h in range(MLA_HEADS) for o in (hr, 0)], axis=0)
    oqr_ref[0] = ((qr * cq_ref[...] + qrs * sq_ref[...]) * scale).astype(BF16)
    ckb = _rms(ckv, kvn_ref[...]).astype(BF16)
    okn_ref[0] = _dot(ckb, wkv_ref[:, :nope]).astype(BF16)
    vt = _dot_nt(wvt_ref[...], ckb).astype(BF16)
    tk = ov_ref.shape[3]
    for c in range(ov_ref.shape[1]):
        ov_ref[0, c] = vt[:, c * tk:(c + 1) * tk]


def _mla_attn_kernel(qn_ref, qr_ref, kn_ref, kr_ref, v_ref, z_ref, x_ref, wo_ref, g_ref, out_ref,
                     qs_ref, acc_ref, m_ref, o_ref):
    tq = qn_ref.shape[2]
    tk = v_ref.shape[3]
    npairs = MLA_HEADS // 2
    i = pl.program_id(1)
    krow = lax.broadcasted_iota(jnp.int32, (tk, 2 * tq), 0)
    qcol = lax.broadcasted_iota(jnp.int32, (tk, 2 * tq), 1)
    causal = krow <= jnp.where(qcol >= tq, qcol - tq, qcol)

    zn = jnp.zeros((MLA_NOPE, tq), BF16)
    zr = jnp.zeros((LANES - MLA_ROPE, tq), BF16)
    for p in range(npairs):
        qn = qn_ref[0, p * LANES:(p + 1) * LANES, :]
        r0 = 2 * p * MLA_ROPE
        c0 = jnp.concatenate([qn[:MLA_NOPE], zn, qr_ref[0, r0:r0 + MLA_ROPE, :], zr], axis=0)
        c1 = jnp.concatenate([zn, qn[MLA_NOPE:], qr_ref[0, r0 + MLA_ROPE:r0 + 2 * MLA_ROPE, :], zr], axis=0)
        qs_ref[p] = jnp.concatenate([c0, c1], axis=1)

    m_ref[...] = jnp.full(m_ref.shape, NEG_INF, F32)
    acc_ref[...] = jnp.zeros(acc_ref.shape, F32)
    ones = jnp.ones((acc_ref.shape[1] - LANES, tk), BF16)

    def kv_step(j, masked):
        ks = pl.multiple_of(j * tk, tk)
        kr = kr_ref[0, pl.ds(ks, tk), :]

        def scores(p):
            kc = jnp.concatenate([kn_ref[0, pl.ds(ks, tk), p * LANES:(p + 1) * LANES], kr], axis=1)
            return [_dot(kc, qs_ref[p, :, c * 2 * LANES:(c + 1) * 2 * LANES]) for c in range(tq // LANES)]

        pending = [scores(p) for p in range(MLA_LOOKAHEAD)]
        late = []

        def flush():
            pp, alpha, pr = late.pop(0)
            vones = jnp.concatenate([v_ref[0, j, pp * LANES:(pp + 1) * LANES, :], ones], axis=0)
            acc_ref[pp] = alpha * acc_ref[pp] + _dot(vones, pr)

        for p in range(npairs):
            s = pending.pop(0)
            if p + MLA_LOOKAHEAD < npairs:
                pending.append(scores(p + MLA_LOOKAHEAD))
            probs, alphas = [], []
            for c in range(2 * tq // LANES):
                sc = s[c // 2][:, (c % 2) * LANES:(c % 2 + 1) * LANES]
                if masked:
                    sc = jnp.where(causal[:, c * LANES:(c + 1) * LANES], sc, NEG_INF)
                m_prev = m_ref[p, :, c * LANES:(c + 1) * LANES]
                m_new = jnp.maximum(m_prev, jnp.max(sc, axis=0, keepdims=True))
                alphas.append(jnp.exp2(m_prev - m_new))
                probs.append(jnp.exp2(sc - m_new).astype(BF16))
                m_ref[p, :, c * LANES:(c + 1) * LANES] = m_new
            if len(late) == MLA_PV_DELAY:
                flush()
            late.append((p, jnp.concatenate(alphas, axis=1), jnp.concatenate(probs, axis=1)))
        while late:
            flush()

    def body(j, c):
        kv_step(j, False)
        return c

    lax.fori_loop(0, i, body, 0)
    kv_step(i, True)
    for p in range(npairs):
        a = acc_ref[p]
        a = a[:LANES] * (1.0 / a[LANES:LANES + 1])
        ot = jnp.concatenate([a[:MLA_V, :tq], a[MLA_V:, tq:]], axis=0)
        o_ref[:, p * LANES:(p + 1) * LANES] = ot.T
    gated = o_ref[...] * jax.nn.silu(z_ref[0])
    r = _dot(gated.astype(BF16), wo_ref[...])
    out_ref[0] = x_ref[0] + _rms(r, g_ref[...])


def _mla_layer(x, pre_g, post_g, w_in, q_norm, kv_norm, w_uq, w_ukv, w_out):
    bsz, L, d = x.shape
    H = MLA_HEADS
    dq = MLA_NOPE + MLA_ROPE
    nope = H * MLA_NOPE
    rope = H * MLA_ROPE
    vw = H * MLA_V
    half = MLA_ROPE // 2
    o_kr = MLA_Q_RANK + MLA_KV_RANK
    o_z = o_kr + MLA_ROPE
    wb = w_in.astype(BF16)
    w_kr = wb[:, o_kr:o_z]
    w_krs = jnp.concatenate([w_kr[:, half:], w_kr[:, :half]], axis=1)
    reps = LANES // MLA_ROPE
    w1 = jnp.concatenate([wb[:, :o_kr], wb[:, o_z:]] + [w_kr] * reps + [w_krs] * reps, axis=1)
    wq3 = w_uq.astype(BF16).reshape(MLA_Q_RANK, H, dq)
    wqt = jnp.concatenate([wq3[:, :, :MLA_NOPE].reshape(MLA_Q_RANK, nope),
                           wq3[:, :, MLA_NOPE:].reshape(MLA_Q_RANK, rope)], axis=1).T
    wkv3 = w_ukv.astype(BF16).reshape(MLA_KV_RANK, H, MLA_NOPE + MLA_V)
    wkn = wkv3[:, :, :MLA_NOPE].reshape(MLA_KV_RANK, nope)
    wvt = wkv3[:, :, MLA_NOPE:].reshape(MLA_KV_RANK, vw).T
    inv = ROPE_BASE ** (-jnp.arange(0, MLA_ROPE, 2, dtype=F32) / MLA_ROPE)
    ang = jnp.arange(L, dtype=F32)[:, None] * inv[None, :]
    cos, sin = jnp.cos(ang), jnp.sin(ang)
    cos32 = jnp.concatenate([cos, cos], axis=1)
    sin32 = jnp.concatenate([-sin, sin], axis=1)
    cos_k, sin_k = jnp.tile(cos32, (1, LANES // MLA_ROPE)), jnp.tile(sin32, (1, LANES // MLA_ROPE))
    cos_q, sin_q = jnp.tile(cos32, (1, H)).T, jnp.tile(sin32, (1, H)).T

    tm = 1024
    tk = MLA_TK
    tok = lambda w_: pl.BlockSpec((1, tm, w_), lambda b, i: (b, i, 0))
    tokt = lambda w_: pl.BlockSpec((1, w_, tm), lambda b, i: (b, 0, i))
    scale = dq ** -0.5 * math.log2(math.e)
    qn, qr, kn, kr, v, z = pl.pallas_call(
        functools.partial(_mla_pre_kernel, scale=scale),
        out_shape=[jax.ShapeDtypeStruct((bsz, nope, L), BF16),
                   jax.ShapeDtypeStruct((bsz, rope, L), BF16),
                   jax.ShapeDtypeStruct((bsz, L, nope), BF16),
                   jax.ShapeDtypeStruct((bsz, L, LANES), BF16),
                   jax.ShapeDtypeStruct((bsz, L // tk, vw, tk), BF16),
                   jax.ShapeDtypeStruct((bsz, L, vw), F32)],
        grid=(bsz, L // tm),
        in_specs=[tok(d), _full((1, d)), _full(w1.shape), _full((1, MLA_Q_RANK)), _full((1, MLA_KV_RANK)),
                  _full(wqt.shape), _full(wkn.shape), _full(wvt.shape),
                  pl.BlockSpec((rope, tm), lambda b, i: (0, i)), pl.BlockSpec((rope, tm), lambda b, i: (0, i)),
                  pl.BlockSpec((tm, LANES), lambda b, i: (i, 0)), pl.BlockSpec((tm, LANES), lambda b, i: (i, 0))],
        out_specs=[tokt(nope), tokt(rope), tok(nope), tok(LANES),
                   pl.BlockSpec((1, tm // tk, vw, tk), lambda b, i: (b, i, 0, 0)), tok(vw)],
        compiler_params=_cparams(("parallel", "parallel")),
        name="mla_pre",
    )(x, pre_g.reshape(1, d), w1, q_norm.reshape(1, -1), kv_norm.reshape(1, -1), wqt, wkn, wvt,
      cos_q, sin_q, cos_k, sin_k)

    tq = MLA_TQ
    npairs = H // 2
    qspec = lambda w_: pl.BlockSpec((1, w_, tq), lambda b, i: (b, 0, i))
    kspec = lambda w_: pl.BlockSpec((1, L, w_), lambda b, i: (b, 0, 0))
    rowspec = lambda w_: pl.BlockSpec((1, tq, w_), lambda b, i: (b, i, 0))
    return pl.pallas_call(
        _mla_attn_kernel,
        out_shape=jax.ShapeDtypeStruct(x.shape, x.dtype),
        grid=(bsz, L // tq),
        in_specs=[qspec(nope), qspec(rope), kspec(nope), kspec(LANES),
                  pl.BlockSpec((1, L // tk, vw, tk), lambda b, i: (b, 0, 0, 0)),
                  rowspec(vw), rowspec(d), _full(w_out.shape), _full((1, d))],
        out_specs=rowspec(d),
        scratch_shapes=[pltpu.VMEM((npairs, 2 * LANES, 2 * tq), BF16),
                        pltpu.VMEM((npairs, LANES + 16, 2 * tq), F32),
                        pltpu.VMEM((npairs, 1, 2 * tq), F32),
                        pltpu.VMEM((tq, vw), F32)],
        compiler_params=_cparams(("parallel", "arbitrary")),
        name="mla_attn",
    )(qn, qr, kn, kr, v, z, x, w_out.astype(BF16), post_g.reshape(1, d))


def _sgu_kernel(x_ref, g_ref, w_ref, lng_ref, lnb_ref, ws_ref, bs_ref, wo_ref, pg_ref, out_ref, s_ref):
    width = wo_ref.shape[0]
    tm = x_ref.shape[1]
    lane = lax.broadcasted_iota(jnp.int32, (1, LANES), 1)
    lo = lane < HALF
    x = x_ref[0]
    hb = _rms(x, g_ref[...]).astype(BF16)
    v = jax.nn.gelu(_dot(hb, w_ref[:, width:2 * width]))
    mu = jnp.mean(v, axis=-1, keepdims=True)
    vc = v - mu
    var = jnp.mean(vc * vc, axis=-1, keepdims=True)
    vb = (vc * lax.rsqrt(var + EPS) * lng_ref[...] + lnb_ref[...]).astype(BF16)
    group = SGU_STACK
    for c0 in range(0, tm // SGU_CHUNK, group):
        for jj in range(width // LANES):
            blk = jnp.concatenate([vb[c * SGU_CHUNK:(c + 1) * SGU_CHUNK, jj * LANES:(jj + 1) * LANES]
                                   for c in range(c0, c0 + group)], axis=1)
            r = _dot(ws_ref[jj], blk)
            for k in range(group):
                c = c0 + k
                s_ref[c * SGU_CHUNK:(c + 1) * SGU_CHUNK, jj * LANES:(jj + 1) * LANES] = (
                    jnp.where(lo, r[:SGU_CHUNK, k * LANES:(k + 1) * LANES],
                              r[SGU_CHUNK:, k * LANES:(k + 1) * LANES]) + bs_ref[jj])
    u = jax.nn.gelu(_dot(hb, w_ref[:, :width]))
    z = _dot(hb, w_ref[:, 2 * width:])
    o = u * s_ref[...] * jax.nn.silu(z)
    r = _dot(o.astype(BF16), wo_ref[...])
    out_ref[0] = x + _rms(r, pg_ref[...])


def _sgu_layer(x, pre_g, post_g, w_in, ln_g, ln_b, w_s, b_s, w_out):
    bsz, L, d = x.shape
    width = w_out.shape[0]
    T = SGU_CHUNK
    gd = width // SGU_GROUPS
    tril = jnp.tril(jnp.ones((T, T), dtype=bool))
    ws = jnp.where(tril[None], w_s, 0.0).reshape(SGU_GROUPS // 2, 2 * T, T).astype(BF16)
    bs = jnp.repeat(b_s.astype(F32).T, gd, axis=1)
    bs = bs.reshape(T, width // LANES, LANES).transpose(1, 0, 2)
    tm = 1024
    return pl.pallas_call(
        _sgu_kernel,
        out_shape=jax.ShapeDtypeStruct(x.shape, x.dtype),
        grid=(bsz, L // tm),
        in_specs=[pl.BlockSpec((1, tm, d), lambda b, i: (b, i, 0)),
                  _full((1, d)), _full(w_in.shape), _full((1, width)), _full((1, width)),
                  _full(ws.shape), _full(bs.shape), _full(w_out.shape), _full((1, d))],
        out_specs=pl.BlockSpec((1, tm, d), lambda b, i: (b, i, 0)),
        scratch_shapes=[pltpu.VMEM((tm, width), F32)],
        compiler_params=_cparams(("parallel", "parallel")),
        name="sgu",
    )(x, pre_g.reshape(1, d), w_in.astype(BF16), ln_g.reshape(1, width), ln_b.reshape(1, width),
      ws, bs, w_out.astype(BF16), post_g.reshape(1, d))


def kernel(x, pre_norm, post_norm, rel_bias, a_w_in, a_lam_re, a_lam_im, a_log_dt, a_b_re, a_b_im, a_c_re, a_c_im, a_d, a_w_glu, a_b_glu, a_w_out, b_w_in, b_sinks, b_w_out, c_w_in, c_q_norm, c_kv_norm, c_w_uq, c_w_ukv, c_w_out, d_w_in, d_ln_g, d_ln_b, d_w_s, d_b_s, d_w_out):
    depth = pre_norm.shape[0]
    for i in range(depth):
        kind, j = i % 4, i // 4
        if kind == 0:
            x = _s5_layer(x, pre_norm[i], post_norm[i], a_w_in[j], a_lam_re[j], a_lam_im[j], a_log_dt[j],
                          a_b_re[j], a_b_im[j], a_c_re[j], a_c_im[j], a_d[j], a_w_glu[j], a_b_glu[j],
                          a_w_out[j])
        elif kind == 1:
            x = _swa_layer(x, pre_norm[i], post_norm[i], b_w_in[j], b_sinks[j], b_w_out[j], rel_bias)
        elif kind == 2:
            x = _mla_layer(x, pre_norm[i], post_norm[i], c_w_in[j], c_q_norm[j], c_kv_norm[j], c_w_uq[j],
                           c_w_ukv[j], c_w_out[j])
        else:
            x = _sgu_layer(x, pre_norm[i], post_norm[i], d_w_in[j], d_ln_g[j], d_ln_b[j], d_w_s[j],
                           d_b_s[j], d_w_out[j])
    return x
```

```python
import functools
import math

import jax
import jax.numpy as jnp
import numpy as np
from jax import lax
from jax.experimental import pallas as pl
from jax.experimental.pallas import tpu as pltpu

F32 = jnp.float32
BF16 = jnp.bfloat16

D_MODEL = 1024
EPS = 1e-6
NEG_INF = -1e30
LANES = 128
HALF = LANES // 2

SSM_GROUP = 16
SSM_STATE = 64
S5_CH_BLOCK = LANES
S5_GROUPS_PER_BLOCK = S5_CH_BLOCK // SSM_GROUP
S5_STATE_BLOCK = S5_GROUPS_PER_BLOCK * SSM_STATE
S5_T = 64

HEAD_DIM = 64
SWA_HEADS = 16
SWA_KV_HEADS = 2
SWA_GROUP = SWA_HEADS // SWA_KV_HEADS
WINDOW = 128
SWA_WINDOWS_PER_STEP = 4
SWA_LOOKAHEAD = 2
SWA_UNIT_HEADS = 8
SWA_PV_DELAY = 1
REL_BUCKETS = 32
REL_MAX_DIST = 128

MLA_HEADS = 16
MLA_NOPE = 64
MLA_ROPE = 32
MLA_V = 64
MLA_KV_RANK = 256
MLA_Q_RANK = 768
ROPE_BASE = 10000.0
MLA_TQ = 256
MLA_TK = 256
MLA_LOOKAHEAD = 4
MLA_PV_DELAY = 2

SGU_CHUNK = 128
SGU_GROUPS = 16
SGU_STACK = 4

VMEM_LIMIT = 56 * 1024 * 1024


def _cparams(sem):
    return pltpu.CompilerParams(dimension_semantics=sem, vmem_limit_bytes=VMEM_LIMIT)


def _rms(x, g):
    return x * lax.rsqrt(jnp.mean(x * x, axis=-1, keepdims=True) + EPS) * g


def _dot(a, b):
    return jnp.dot(a, b, preferred_element_type=F32)


def _dot_nt(a, b):
    return lax.dot_general(a, b, (((1,), (1,)), ((), ())), preferred_element_type=F32)


def _full(shape):
    n = len(shape)
    return pl.BlockSpec(shape, lambda *_: (0,) * n, pipeline_mode=pl.Buffered(1))


def _s5_kernel(x_ref, g_ref, w_ref, perm_ref, permt_ref, bb_ref, cc_ref, ar_ref, ai_ref, d_ref,
               wg_ref, bg_ref, wo_ref, pg_ref, out_ref, u_ref, z_ref, y_ref, s_ref, carry_ref, *, tt):
    bsz = x_ref.shape[0]
    width = wg_ref.shape[0]
    rows = bsz * tt
    nblk = bb_ref.shape[0]
    sb = S5_STATE_BLOCK

    @pl.when(pl.program_id(0) == 0)
    def _():
        carry_ref[...] = jnp.zeros_like(carry_ref)

    x = x_ref[...].reshape(rows, x_ref.shape[2])
    hb = _rms(x, g_ref[...]).astype(BF16)
    hb = _dot(perm_ref[...], hb).astype(BF16)
    u_ref[...] = _dot(hb, w_ref[:, :width])
    z_ref[...] = _dot(hb, w_ref[:, width:])

    nbuf = s_ref.shape[0]

    def project_in(i):
        s_ref[i % nbuf] = _dot(u_ref[:, i * LANES:(i + 1) * LANES].astype(BF16), bb_ref[i])

    def project_out(i):
        ub = u_ref[:, i * LANES:(i + 1) * LANES]
        y = _dot(s_ref[i % nbuf].astype(BF16), cc_ref[i]) + d_ref[:, i * LANES:(i + 1) * LANES] * ub
        y_ref[:, i * LANES:(i + 1) * LANES] = jax.nn.gelu(y)

    project_in(0)
    for i in range(nblk):
        if i + 1 < nblk:
            project_in(i + 1)
        buf = s_ref.at[i % nbuf]
        ar = ar_ref[i]
        ai = ai_ref[i]
        sr = carry_ref[i, :, 0:sb]
        si = carry_ref[i, :, sb:2 * sb]
        for t in range(tt):
            r0 = t * bsz
            nr = ar * sr - ai * si + buf[r0:r0 + bsz, 0:sb]
            ni = ar * si + ai * sr + buf[r0:r0 + bsz, sb:2 * sb]
            buf[r0:r0 + bsz, 0:sb] = nr
            buf[r0:r0 + bsz, sb:2 * sb] = ni
            sr, si = nr, ni
        carry_ref[i, :, 0:sb] = sr
        carry_ref[i, :, sb:2 * sb] = si
        project_out(i)

    y = y_ref[...]
    gate = jax.nn.sigmoid(_dot(y.astype(BF16), wg_ref[...]) + bg_ref[...])
    o = y * gate * jax.nn.silu(z_ref[...])
    ob = _dot(permt_ref[...], o.astype(BF16)).astype(BF16)
    r = _dot(ob, wo_ref[...])
    out_ref[...] = (x + _rms(r, pg_ref[...])).reshape(out_ref.shape)


def _s5_discretize(lam_re, lam_im, log_dt, b_re, b_im):
    dt = jnp.exp(log_dt)[:, None]
    mag = jnp.exp(lam_re * dt)
    ab_re = mag * jnp.cos(lam_im * dt)
    ab_im = mag * jnp.sin(lam_im * dt)
    den = lam_re * lam_re + lam_im * lam_im
    nr = ab_re - 1.0
    f_re = (nr * lam_re + ab_im * lam_im) / den
    f_im = (ab_im * lam_re - nr * lam_im) / den
    bb_re = f_re[..., None] * b_re - f_im[..., None] * b_im
    bb_im = f_re[..., None] * b_im + f_im[..., None] * b_re
    return ab_re, ab_im, bb_re, bb_im


def _s5_layer(x, pre_g, post_g, w_in, lam_re, lam_im, log_dt, b_re, b_im, c_re, c_im, d_skip,
              w_glu, b_glu, w_out):
    bsz, L, d = x.shape
    width = w_in.shape[1] // 2
    nblk = width // S5_CH_BLOCK
    gpb = S5_GROUPS_PER_BLOCK
    tt = S5_T
    rows = bsz * tt

    src = (np.arange(rows) % bsz) * tt + np.arange(rows) // bsz
    perm_np = np.zeros((rows, rows), np.float32)
    perm_np[np.arange(rows), src] = 1.0
    perm = jnp.asarray(perm_np, BF16)
    perm_t = jnp.asarray(perm_np.T, BF16)

    ab_re, ab_im, bb_re, bb_im = _s5_discretize(lam_re, lam_im, log_dt, b_re, b_im)
    eye = jnp.eye(gpb, dtype=F32)

    def pack_b(bb):
        t = bb.reshape(nblk, gpb, SSM_STATE, SSM_GROUP)
        return jnp.einsum('igph,gk->ikhgp', t, eye).reshape(nblk, S5_CH_BLOCK, S5_STATE_BLOCK)

    def pack_c(cc):
        t = cc.reshape(nblk, gpb, SSM_GROUP, SSM_STATE)
        return jnp.einsum('ighp,gk->igpkh', t, eye).reshape(nblk, S5_STATE_BLOCK, S5_CH_BLOCK)

    bb = jnp.concatenate([pack_b(bb_re), pack_b(bb_im)], axis=2).astype(BF16)
    cc = jnp.concatenate([pack_c(c_re), -pack_c(c_im)], axis=1).astype(BF16)
    ar = jnp.broadcast_to(ab_re.reshape(nblk, 1, S5_STATE_BLOCK), (nblk, bsz, S5_STATE_BLOCK))
    ai = jnp.broadcast_to(ab_im.reshape(nblk, 1, S5_STATE_BLOCK), (nblk, bsz, S5_STATE_BLOCK))

    xspec = pl.BlockSpec((bsz, tt, d), lambda i: (0, i, 0))
    return pl.pallas_call(
        functools.partial(_s5_kernel, tt=tt),
        out_shape=jax.ShapeDtypeStruct(x.shape, x.dtype),
        grid=(L // tt,),
        in_specs=[xspec, _full((1, d)), _full(w_in.shape), _full(perm.shape), _full(perm_t.shape),
                  _full(bb.shape), _full(cc.shape), _full(ar.shape), _full(ai.shape), _full((1, width)),
                  _full(w_glu.shape), _full((1, width)), _full(w_out.shape), _full((1, d))],
        out_specs=xspec,
        scratch_shapes=[pltpu.VMEM((rows, width), F32),
                        pltpu.VMEM((rows, width), F32),
                        pltpu.VMEM((rows, width), F32),
                        pltpu.VMEM((2, rows, 2 * S5_STATE_BLOCK), F32),
                        pltpu.VMEM((nblk, bsz, 2 * S5_STATE_BLOCK), F32)],
        compiler_params=_cparams(("arbitrary",)),
        name="s5_layer",
    )(x, pre_g.reshape(1, d), w_in.astype(BF16), perm, perm_t, bb, cc, ar, ai, d_skip.reshape(1, width),
      w_glu.astype(BF16), b_glu.reshape(1, width), w_out.astype(BF16), post_g.reshape(1, d))


def _swa_bias(rel_bias):
    W = WINDOW
    n = 4 * W
    dist = 2 * W - jnp.arange(n)
    valid = jnp.logical_and(dist >= 0, dist < W)
    dpos = jnp.maximum(dist, 0)
    max_exact = REL_BUCKETS // 2
    dist_f = jnp.maximum(dpos, 1).astype(F32)
    large = max_exact + (jnp.log(dist_f / max_exact) / math.log(REL_MAX_DIST / max_exact)
                         * (REL_BUCKETS - max_exact)).astype(jnp.int32)
    large = jnp.minimum(large, REL_BUCKETS - 1)
    bucket = jnp.where(dpos < max_exact, dpos, large)
    vec = jnp.where(valid[:, None], rel_bias[bucket].astype(F32), NEG_INF).T
    skew = jnp.tile(vec, (1, W))[:, :W * (n - 1)].reshape(vec.shape[0], W, n - 1)
    return skew[:, :, W:3 * W]


def _swa_pre_kernel(x_ref, g_ref, wqt_ref, wk_ref, wvt_ref, wz_ref, qt_ref, k_ref, vt_ref, z_ref, *, scale):
    hb = _rms(x_ref[0], g_ref[...]).astype(BF16)
    qt_ref[0] = (_dot_nt(wqt_ref[...], hb) * scale).astype(BF16)
    k_ref[0] = _dot(hb, wk_ref[...]).astype(BF16)
    vt_ref[0] = _dot_nt(wvt_ref[...], hb).astype(BF16)
    z_ref[0] = _dot(hb, wz_ref[...])


def _swa_kernel(qt_ref, kp_ref, kc_ref, vtp_ref, vtc_ref, bias_ref, sink_ref, z_ref, x_ref, wo_ref, g_ref,
                out_ref, ot_ref):
    W = WINDOW
    nwin = qt_ref.shape[2] // W
    step = pl.program_id(1)
    kall = jnp.concatenate([kp_ref[0], kc_ref[0]], axis=0)
    vtall = jnp.concatenate([vtp_ref[0], vtc_ref[0]], axis=1)
    nsub = SWA_UNIT_HEADS
    zq = jnp.zeros((HEAD_DIM, nsub * W), BF16)
    ones = jnp.ones((16, 2 * W), BF16)
    units = [(w, h, c) for w in range(nwin) for h in range(SWA_KV_HEADS) for c in range(SWA_GROUP // nsub)]

    def scores(w, h, c):
        hd0 = h * SWA_GROUP + c * nsub
        qh = jnp.concatenate([qt_ref[0, (hd0 + g) * HEAD_DIM:(hd0 + g + 1) * HEAD_DIM, w * W:(w + 1) * W]
                              for g in range(nsub)], axis=1)
        qz = jnp.concatenate([qh, zq] if h == 0 else [zq, qh], axis=0)
        return _dot(kall[w * W:(w + 2) * W], qz)

    pending = [scores(*u) for u in units[:SWA_LOOKAHEAD]]
    late = []

    def flush():
        (w, h, c), p, tail = late.pop(0)
        vones = jnp.concatenate([vtall[h * HEAD_DIM:(h + 1) * HEAD_DIM, w * W:(w + 2) * W], ones], axis=0)
        o = _dot(vones, p)
        oh = o[:HEAD_DIM] * (1.0 / (o[HEAD_DIM:HEAD_DIM + 1] + tail))
        for g in range(nsub):
            hd = h * SWA_GROUP + c * nsub + g
            ot_ref[hd * HEAD_DIM:(hd + 1) * HEAD_DIM, w * W:(w + 1) * W] = oh[:, g * W:(g + 1) * W]

    for idx, (w, h, c) in enumerate(units):
        raw = pending.pop(0)
        if idx + SWA_LOOKAHEAD < len(units):
            pending.append(scores(*units[idx + SWA_LOOKAHEAD]))
        cols = slice(c * nsub * W, (c + 1) * nsub * W)
        variant = (step == 0).astype(jnp.int32) if w == 0 else 0
        s = raw + bias_ref[variant, h, :, cols]
        sink = sink_ref[h, :, cols]
        m = jnp.maximum(jnp.max(s, axis=0, keepdims=True), sink)
        if len(late) == SWA_PV_DELAY:
            flush()
        late.append(((w, h, c), jnp.exp2(s - m).astype(BF16), jnp.exp2(sink - m)))
    while late:
        flush()
    gated = ot_ref[...].T * jax.nn.silu(z_ref[0])
    r = _dot(gated.astype(BF16), wo_ref[...])
    out_ref[0] = x_ref[0] + _rms(r, g_ref[...])


def _swa_layer(x, pre_g, post_g, w_in, sinks, w_out, rel_bias):
    bsz, L, d = x.shape
    width = SWA_HEADS * HEAD_DIM
    kvw = SWA_KV_HEADS * HEAD_DIM
    W = WINDOW
    nb = L // W
    log2e = math.log2(math.e)
    tm = 1024
    tok = lambda w_: pl.BlockSpec((1, tm, w_), lambda b, i: (b, i, 0))
    tokt = lambda w_: pl.BlockSpec((1, w_, tm), lambda b, i: (b, 0, i))
    wb = w_in.astype(BF16)
    wqt = wb[:, :width].T
    wk = wb[:, width:width + kvw]
    wvt = wb[:, width + kvw:width + 2 * kvw].T
    wz = wb[:, width + 2 * kvw:]
    qt, k, vt, z = pl.pallas_call(
        functools.partial(_swa_pre_kernel, scale=HEAD_DIM ** -0.5 * log2e),
        out_shape=[jax.ShapeDtypeStruct((bsz, width, L), BF16),
                   jax.ShapeDtypeStruct((bsz, L, kvw), BF16),
                   jax.ShapeDtypeStruct((bsz, kvw, L), BF16),
                   jax.ShapeDtypeStruct((bsz, L, width), F32)],
        grid=(bsz, L // tm),
        in_specs=[tok(d), _full((1, d)), _full(wqt.shape), _full(wk.shape), _full(wvt.shape), _full(wz.shape)],
        out_specs=[tokt(width), tok(kvw), tokt(kvw), tok(width)],
        compiler_params=_cparams(("parallel", "parallel")),
        name="swa_pre",
    )(x, pre_g.reshape(1, d), wqt, wk, wvt, wz)

    bias = jnp.transpose(_swa_bias(rel_bias.astype(F32) * log2e), (0, 2, 1))
    has_prev = (jnp.arange(2 * W) >= W)[None, :, None]
    variants = [bias,
                jnp.where(has_prev, bias, NEG_INF)]
    bias_t = jnp.stack([v.reshape(SWA_KV_HEADS, SWA_GROUP, 2 * W, W).transpose(0, 2, 1, 3)
                        .reshape(SWA_KV_HEADS, 2 * W, SWA_GROUP * W) for v in variants])
    sink = jnp.repeat(sinks.astype(F32) * log2e, W).reshape(SWA_KV_HEADS, 1, SWA_GROUP * W)

    nwin = SWA_WINDOWS_PER_STEP
    tq = nwin * W
    prev = lambda n: jnp.maximum(n * nwin - 1, 0)
    return pl.pallas_call(
        _swa_kernel,
        out_shape=jax.ShapeDtypeStruct(x.shape, x.dtype),
        grid=(bsz, L // tq),
        in_specs=[pl.BlockSpec((1, width, tq), lambda b, n: (b, 0, n)),
                  pl.BlockSpec((1, W, kvw), lambda b, n: (b, prev(n), 0)),
                  pl.BlockSpec((1, tq, kvw), lambda b, n: (b, n, 0)),
                  pl.BlockSpec((1, kvw, W), lambda b, n: (b, 0, prev(n))),
                  pl.BlockSpec((1, kvw, tq), lambda b, n: (b, 0, n)),
                  _full(bias_t.shape), _full(sink.shape),
                  pl.BlockSpec((1, tq, width), lambda b, n: (b, n, 0)),
                  pl.BlockSpec((1, tq, d), lambda b, n: (b, n, 0)),
                  _full(w_out.shape), _full((1, d))],
        out_specs=pl.BlockSpec((1, tq, d), lambda b, n: (b, n, 0)),
        scratch_shapes=[pltpu.VMEM((width, tq), F32)],
        compiler_params=_cparams(("parallel", "parallel")),
        name="swa_attn",
    )(qt, k, k, vt, vt, bias_t, sink, z, x, w_out.astype(BF16), post_g.reshape(1, d))


def _mla_pre_kernel(x_ref, g_ref, w_ref, qn_ref, kvn_ref, wq_ref, wkv_ref, wvt_ref, cq_ref, sq_ref, ck_ref, sk_ref,
                    oqn_ref, oqr_ref, okn_ref, okr_ref, ov_ref, oz_ref, *, scale):
    nope = MLA_HEADS * MLA_NOPE
    rope = MLA_HEADS * MLA_ROPE
    vw = MLA_HEADS * MLA_V
    hb = _rms(x_ref[0], g_ref[...]).astype(BF16)
    o1 = MLA_Q_RANK
    o2 = o1 + MLA_KV_RANK
    o3 = o2 + vw
    cq = _dot(hb, w_ref[:, :o1])
    ckv = _dot(hb, w_ref[:, o1:o2])
    oz_ref[0] = _dot(hb, w_ref[:, o2:o3])
    kr = _dot(hb, w_ref[:, o3:o3 + LANES])
    krs = _dot(hb, w_ref[:, o3 + LANES:o3 + 2 * LANES])
    okr_ref[0] = (kr * ck_ref[...] + krs * sk_ref[...]).astype(BF16)
    cqb = _rms(cq, qn_ref[...]).astype(BF16)
    oqn_ref[0] = (_dot_nt(wq_ref[:nope], cqb) * scale).astype(BF16)
    qr = _dot_nt(wq_ref[nope:nope + rope], cqb)
    hr = MLA_ROPE // 2
    qrs = jnp.concatenate([qr[h * MLA_ROPE + o:h * MLA_ROPE + o + hr]
                           for h in range(MLA_HEADS) for o in (hr, 0)], axis=0)
    oqr_ref[0] = ((qr * cq_ref[...] + qrs * sq_ref[...]) * scale).astype(BF16)
    ckb = _rms(ckv, kvn_ref[...]).astype(BF16)
    okn_ref[0] = _dot(ckb, wkv_ref[:, :nope]).astype(BF16)
    vt = _dot_nt(wvt_ref[...], ckb).astype(BF16)
    tk = ov_ref.shape[3]
    for c in range(ov_ref.shape[1]):
        ov_ref[0, c] = vt[:, c * tk:(c + 1) * tk]


def _mla_attn_kernel(qn_ref, qr_ref, kn_ref, kr_ref, v_ref, z_ref, x_ref, wo_ref, g_ref, out_ref,
                     qs_ref, acc_ref, m_ref, o_ref):
    tq = qn_ref.shape[2]
    tk = v_ref.shape[3]
    npairs = MLA_HEADS // 2
    i = pl.program_id(1)
    krow = lax.broadcasted_iota(jnp.int32, (tk, 2 * tq), 0)
    qcol = lax.broadcasted_iota(jnp.int32, (tk, 2 * tq), 1)
    causal = krow <= jnp.where(qcol >= tq, qcol - tq, qcol)

    zn = jnp.zeros((MLA_NOPE, tq), BF16)
    zr = jnp.zeros((LANES - MLA_ROPE, tq), BF16)
    for p in range(npairs):
        qn = qn_ref[0, p * LANES:(p + 1) * LANES, :]
        r0 = 2 * p * MLA_ROPE
        c0 = jnp.concatenate([qn[:MLA_NOPE], zn, qr_ref[0, r0:r0 + MLA_ROPE, :], zr], axis=0)
        c1 = jnp.concatenate([zn, qn[MLA_NOPE:], qr_ref[0, r0 + MLA_ROPE:r0 + 2 * MLA_ROPE, :], zr], axis=0)
        qs_ref[p] = jnp.concatenate([c0, c1], axis=1)

    m_ref[...] = jnp.full(m_ref.shape, NEG_INF, F32)
    acc_ref[...] = jnp.zeros(acc_ref.shape, F32)
    ones = jnp.ones((acc_ref.shape[1] - LANES, tk), BF16)

    def kv_steps(blocks):
        units = [(j, masked, p) for j, masked in blocks for p in range(npairs)]

        def scores(j, p):
            ks = pl.multiple_of(j * tk, tk)
            kc = jnp.concatenate([kn_ref[0, pl.ds(ks, tk), p * LANES:(p + 1) * LANES],
                                  kr_ref[0, pl.ds(ks, tk), :]], axis=1)
            return [_dot(kc, qs_ref[p, :, c * 2 * LANES:(c + 1) * 2 * LANES]) for c in range(tq // LANES)]

        pending = [scores(j, p) for j, _, p in units[:MLA_LOOKAHEAD]]
        late = []

        def flush():
            jj, pp, alpha, pr = late.pop(0)
            vones = jnp.concatenate([v_ref[0, jj, pp * LANES:(pp + 1) * LANES, :], ones], axis=0)
            acc_ref[pp] = alpha * acc_ref[pp] + _dot(vones, pr)

        for idx, (j, masked, p) in enumerate(units):
            s = pending.pop(0)
            if idx + MLA_LOOKAHEAD < len(units):
                nxt = units[idx + MLA_LOOKAHEAD]
                pending.append(scores(nxt[0], nxt[2]))
            probs, alphas = [], []
            for c in range(2 * tq // LANES):
                sc = s[c // 2][:, (c % 2) * LANES:(c % 2 + 1) * LANES]
                if masked:
                    sc = jnp.where(causal[:, c * LANES:(c + 1) * LANES], sc, NEG_INF)
                m_prev = m_ref[p, :, c * LANES:(c + 1) * LANES]
                m_new = jnp.maximum(m_prev, jnp.max(sc, axis=0, keepdims=True))
                alphas.append(jnp.exp2(m_prev - m_new))
                probs.append(jnp.exp2(sc - m_new).astype(BF16))
                m_ref[p, :, c * LANES:(c + 1) * LANES] = m_new
            if len(late) == MLA_PV_DELAY:
                flush()
            late.append((j, p, jnp.concatenate(alphas, axis=1), jnp.concatenate(probs, axis=1)))
        while late:
            flush()

    def body(jj, c):
        kv_steps([(2 * jj, False), (2 * jj + 1, False)])
        return c

    lax.fori_loop(0, i // 2, body, 0)

    @pl.when(i % 2 == 1)
    def _():
        kv_steps([(i - 1, False), (i, True)])

    @pl.when(i % 2 == 0)
    def _():
        kv_steps([(i, True)])
    for p in range(npairs):
        a = acc_ref[p]
        a = a[:LANES] * (1.0 / a[LANES:LANES + 1])
        ot = jnp.concatenate([a[:MLA_V, :tq], a[MLA_V:, tq:]], axis=0)
        o_ref[:, p * LANES:(p + 1) * LANES] = ot.T
    gated = o_ref[...] * jax.nn.silu(z_ref[0])
    r = _dot(gated.astype(BF16), wo_ref[...])
    out_ref[0] = x_ref[0] + _rms(r, g_ref[...])


def _mla_layer(x, pre_g, post_g, w_in, q_norm, kv_norm, w_uq, w_ukv, w_out):
    bsz, L, d = x.shape
    H = MLA_HEADS
    dq = MLA_NOPE + MLA_ROPE
    nope = H * MLA_NOPE
    rope = H * MLA_ROPE
    vw = H * MLA_V
    half = MLA_ROPE // 2
    o_kr = MLA_Q_RANK + MLA_KV_RANK
    o_z = o_kr + MLA_ROPE
    wb = w_in.astype(BF16)
    w_kr = wb[:, o_kr:o_z]
    w_krs = jnp.concatenate([w_kr[:, half:], w_kr[:, :half]], axis=1)
    reps = LANES // MLA_ROPE
    w1 = jnp.concatenate([wb[:, :o_kr], wb[:, o_z:]] + [w_kr] * reps + [w_krs] * reps, axis=1)
    wq3 = w_uq.astype(BF16).reshape(MLA_Q_RANK, H, dq)
    wqt = jnp.concatenate([wq3[:, :, :MLA_NOPE].reshape(MLA_Q_RANK, nope),
                           wq3[:, :, MLA_NOPE:].reshape(MLA_Q_RANK, rope)], axis=1).T
    wkv3 = w_ukv.astype(BF16).reshape(MLA_KV_RANK, H, MLA_NOPE + MLA_V)
    wkn = wkv3[:, :, :MLA_NOPE].reshape(MLA_KV_RANK, nope)
    wvt = wkv3[:, :, MLA_NOPE:].reshape(MLA_KV_RANK, vw).T
    inv = ROPE_BASE ** (-jnp.arange(0, MLA_ROPE, 2, dtype=F32) / MLA_ROPE)
    ang = jnp.arange(L, dtype=F32)[:, None] * inv[None, :]
    cos, sin = jnp.cos(ang), jnp.sin(ang)
    cos32 = jnp.concatenate([cos, cos], axis=1)
    sin32 = jnp.concatenate([-sin, sin], axis=1)
    cos_k, sin_k = jnp.tile(cos32, (1, LANES // MLA_ROPE)), jnp.tile(sin32, (1, LANES // MLA_ROPE))
    cos_q, sin_q = jnp.tile(cos32, (1, H)).T, jnp.tile(sin32, (1, H)).T

    tm = 1024
    tk = MLA_TK
    tok = lambda w_: pl.BlockSpec((1, tm, w_), lambda b, i: (b, i, 0))
    tokt = lambda w_: pl.BlockSpec((1, w_, tm), lambda b, i: (b, 0, i))
    scale = dq ** -0.5 * math.log2(math.e)
    qn, qr, kn, kr, v, z = pl.pallas_call(
        functools.partial(_mla_pre_kernel, scale=scale),
        out_shape=[jax.ShapeDtypeStruct((bsz, nope, L), BF16),
                   jax.ShapeDtypeStruct((bsz, rope, L), BF16),
                   jax.ShapeDtypeStruct((bsz, L, nope), BF16),
                   jax.ShapeDtypeStruct((bsz, L, LANES), BF16),
                   jax.ShapeDtypeStruct((bsz, L // tk, vw, tk), BF16),
                   jax.ShapeDtypeStruct((bsz, L, vw), F32)],
        grid=(bsz, L // tm),
        in_specs=[tok(d), _full((1, d)), _full(w1.shape), _full((1, MLA_Q_RANK)), _full((1, MLA_KV_RANK)),
                  _full(wqt.shape), _full(wkn.shape), _full(wvt.shape),
                  pl.BlockSpec((rope, tm), lambda b, i: (0, i)), pl.BlockSpec((rope, tm), lambda b, i: (0, i)),
                  pl.BlockSpec((tm, LANES), lambda b, i: (i, 0)), pl.BlockSpec((tm, LANES), lambda b, i: (i, 0))],
        out_specs=[tokt(nope), tokt(rope), tok(nope), tok(LANES),
                   pl.BlockSpec((1, tm // tk, vw, tk), lambda b, i: (b, i, 0, 0)), tok(vw)],
        compiler_params=_cparams(("parallel", "parallel")),
        name="mla_pre",
    )(x, pre_g.reshape(1, d), w1, q_norm.reshape(1, -1), kv_norm.reshape(1, -1), wqt, wkn, wvt,
      cos_q, sin_q, cos_k, sin_k)

    tq = MLA_TQ
    npairs = H // 2
    qspec = lambda w_: pl.BlockSpec((1, w_, tq), lambda b, i: (b, 0, i))
    kspec = lambda w_: pl.BlockSpec((1, L, w_), lambda b, i: (b, 0, 0))
    rowspec = lambda w_: pl.BlockSpec((1, tq, w_), lambda b, i: (b, i, 0))
    return pl.pallas_call(
        _mla_attn_kernel,
        out_shape=jax.ShapeDtypeStruct(x.shape, x.dtype),
        grid=(bsz, L // tq),
        in_specs=[qspec(nope), qspec(rope), kspec(nope), kspec(LANES),
                  pl.BlockSpec((1, L // tk, vw, tk), lambda b, i: (b, 0, 0, 0)),
                  rowspec(vw), rowspec(d), _full(w_out.shape), _full((1, d))],
        out_specs=rowspec(d),
        scratch_shapes=[pltpu.VMEM((npairs, 2 * LANES, 2 * tq), BF16),
                        pltpu.VMEM((npairs, LANES + 16, 2 * tq), F32),
                        pltpu.VMEM((npairs, 1, 2 * tq), F32),
                        pltpu.VMEM((tq, vw), F32)],
        compiler_params=_cparams(("parallel", "arbitrary")),
        name="mla_attn",
    )(qn, qr, kn, kr, v, z, x, w_out.astype(BF16), post_g.reshape(1, d))


def _sgu_kernel(x_ref, g_ref, w_ref, lng_ref, lnb_ref, ws_ref, bs_ref, wo_ref, pg_ref, out_ref, s_ref):
    width = wo_ref.shape[0]
    tm = x_ref.shape[1]
    lane = lax.broadcasted_iota(jnp.int32, (1, LANES), 1)
    lo = lane < HALF
    x = x_ref[0]
    hb = _rms(x, g_ref[...]).astype(BF16)
    v = jax.nn.gelu(_dot(hb, w_ref[:, width:2 * width]))
    mu = jnp.mean(v, axis=-1, keepdims=True)
    vc = v - mu
    var = jnp.mean(vc * vc, axis=-1, keepdims=True)
    vb = (vc * lax.rsqrt(var + EPS) * lng_ref[...] + lnb_ref[...]).astype(BF16)
    group = SGU_STACK
    for c0 in range(0, tm // SGU_CHUNK, group):
        for jj in range(width // LANES):
            blk = jnp.concatenate([vb[c * SGU_CHUNK:(c + 1) * SGU_CHUNK, jj * LANES:(jj + 1) * LANES]
                                   for c in range(c0, c0 + group)], axis=1)
            r = _dot(ws_ref[jj], blk)
            for k in range(group):
                c = c0 + k
                s_ref[c * SGU_CHUNK:(c + 1) * SGU_CHUNK, jj * LANES:(jj + 1) * LANES] = (
                    jnp.where(lo, r[:SGU_CHUNK, k * LANES:(k + 1) * LANES],
                              r[SGU_CHUNK:, k * LANES:(k + 1) * LANES]) + bs_ref[jj])
    u = jax.nn.gelu(_dot(hb, w_ref[:, :width]))
    z = _dot(hb, w_ref[:, 2 * width:])
    o = u * s_ref[...] * jax.nn.silu(z)
    r = _dot(o.astype(BF16), wo_ref[...])
    out_ref[0] = x + _rms(r, pg_ref[...])


def _sgu_layer(x, pre_g, post_g, w_in, ln_g, ln_b, w_s, b_s, w_out):
    bsz, L, d = x.shape
    width = w_out.shape[0]
    T = SGU_CHUNK
    gd = width // SGU_GROUPS
    tril = jnp.tril(jnp.ones((T, T), dtype=bool))
    ws = jnp.where(tril[None], w_s, 0.0).reshape(SGU_GROUPS // 2, 2 * T, T).astype(BF16)
    bs = jnp.repeat(b_s.astype(F32).T, gd, axis=1)
    bs = bs.reshape(T, width // LANES, LANES).transpose(1, 0, 2)
    tm = 1024
    return pl.pallas_call(
        _sgu_kernel,
        out_shape=jax.ShapeDtypeStruct(x.shape, x.dtype),
        grid=(bsz, L // tm),
        in_specs=[pl.BlockSpec((1, tm, d), lambda b, i: (b, i, 0)),
                  _full((1, d)), _full(w_in.shape), _full((1, width)), _full((1, width)),
                  _full(ws.shape), _full(bs.shape), _full(w_out.shape), _full((1, d))],
        out_specs=pl.BlockSpec((1, tm, d), lambda b, i: (b, i, 0)),
        scratch_shapes=[pltpu.VMEM((tm, width), F32)],
        compiler_params=_cparams(("parallel", "parallel")),
        name="sgu",
    )(x, pre_g.reshape(1, d), w_in.astype(BF16), ln_g.reshape(1, width), ln_b.reshape(1, width),
      ws, bs, w_out.astype(BF16), post_g.reshape(1, d))


def kernel(x, pre_norm, post_norm, rel_bias, a_w_in, a_lam_re, a_lam_im, a_log_dt, a_b_re, a_b_im, a_c_re, a_c_im, a_d, a_w_glu, a_b_glu, a_w_out, b_w_in, b_sinks, b_w_out, c_w_in, c_q_norm, c_kv_norm, c_w_uq, c_w_ukv, c_w_out, d_w_in, d_ln_g, d_ln_b, d_w_s, d_b_s, d_w_out):
    depth = pre_norm.shape[0]
    for i in range(depth):
        kind, j = i % 4, i // 4
        if kind == 0:
            x = _s5_layer(x, pre_norm[i], post_norm[i], a_w_in[j], a_lam_re[j], a_lam_im[j], a_log_dt[j],
                          a_b_re[j], a_b_im[j], a_c_re[j], a_c_im[j], a_d[j], a_w_glu[j], a_b_glu[j],
                          a_w_out[j])
        elif kind == 1:
            x = _swa_layer(x, pre_norm[i], post_norm[i], b_w_in[j], b_sinks[j], b_w_out[j], rel_bias)
        elif kind == 2:
            x = _mla_layer(x, pre_norm[i], post_norm[i], c_w_in[j], c_q_norm[j], c_kv_norm[j], c_w_uq[j],
                           c_w_ukv[j], c_w_out[j])
        else:
            x = _sgu_layer(x, pre_norm[i], post_norm[i], d_w_in[j], d_ln_g[j], d_ln_b[j], d_w_s[j],
                           d_b_s[j], d_w_out[j])
    return x
```

```python
import functools
import math

import jax
import jax.numpy as jnp
import numpy as np
from jax import lax
from jax.experimental import pallas as pl
from jax.experimental.pallas import tpu as pltpu

F32 = jnp.float32
BF16 = jnp.bfloat16

D_MODEL = 1024
EPS = 1e-6
NEG_INF = -1e30
LANES = 128
HALF = LANES // 2

SSM_GROUP = 16
SSM_STATE = 64
S5_CH_BLOCK = LANES
S5_GROUPS_PER_BLOCK = S5_CH_BLOCK // SSM_GROUP
S5_STATE_BLOCK = S5_GROUPS_PER_BLOCK * SSM_STATE
S5_T = 64

HEAD_DIM = 64
SWA_HEADS = 16
SWA_KV_HEADS = 2
SWA_GROUP = SWA_HEADS // SWA_KV_HEADS
WINDOW = 128
SWA_WINDOWS_PER_STEP = 4
SWA_LOOKAHEAD = 2
SWA_UNIT_HEADS = 8
SWA_PV_DELAY = 1
REL_BUCKETS = 32
REL_MAX_DIST = 128

MLA_HEADS = 16
MLA_NOPE = 64
MLA_ROPE = 32
MLA_V = 64
MLA_KV_RANK = 256
MLA_Q_RANK = 768
ROPE_BASE = 10000.0
MLA_TQ = 256
MLA_TK = 256
MLA_LOOKAHEAD = 4
MLA_PV_DELAY = 2

SGU_CHUNK = 128
SGU_GROUPS = 16
SGU_STACK = 4

VMEM_LIMIT = 56 * 1024 * 1024


def _cparams(sem):
    return pltpu.CompilerParams(dimension_semantics=sem, vmem_limit_bytes=VMEM_LIMIT)


def _rms(x, g):
    return x * lax.rsqrt(jnp.mean(x * x, axis=-1, keepdims=True) + EPS) * g


def _dot(a, b):
    return jnp.dot(a, b, preferred_element_type=F32)


def _dot_nt(a, b):
    return lax.dot_general(a, b, (((1,), (1,)), ((), ())), preferred_element_type=F32)


def _full(shape):
    n = len(shape)
    return pl.BlockSpec(shape, lambda *_: (0,) * n, pipeline_mode=pl.Buffered(1))


def _s5_kernel(x_ref, g_ref, w_ref, perm_ref, permt_ref, bb_ref, cc_ref, ar_ref, ai_ref, d_ref,
               wg_ref, bg_ref, wo_ref, pg_ref, out_ref, u_ref, z_ref, y_ref, s_ref, carry_ref, *, tt):
    bsz = x_ref.shape[0]
    width = wg_ref.shape[0]
    rows = bsz * tt
    nblk = bb_ref.shape[0]
    sb = S5_STATE_BLOCK

    @pl.when(pl.program_id(0) == 0)
    def _():
        carry_ref[...] = jnp.zeros_like(carry_ref)

    x = x_ref[...].reshape(rows, x_ref.shape[2])
    hb = _rms(x, g_ref[...]).astype(BF16)
    hb = _dot(perm_ref[...], hb).astype(BF16)
    u_ref[...] = _dot(hb, w_ref[:, :width])
    z_ref[...] = _dot(hb, w_ref[:, width:])

    nbuf = s_ref.shape[0]

    def project_in(i):
        s_ref[i % nbuf] = _dot(u_ref[:, i * LANES:(i + 1) * LANES].astype(BF16), bb_ref[i])

    def project_out(i):
        ub = u_ref[:, i * LANES:(i + 1) * LANES]
        y = _dot(s_ref[i % nbuf].astype(BF16), cc_ref[i]) + d_ref[:, i * LANES:(i + 1) * LANES] * ub
        y_ref[:, i * LANES:(i + 1) * LANES] = jax.nn.gelu(y)

    project_in(0)
    for i in range(nblk):
        if i + 1 < nblk:
            project_in(i + 1)
        buf = s_ref.at[i % nbuf]
        ar = ar_ref[i]
        ai = ai_ref[i]
        sr = carry_ref[i, :, 0:sb]
        si = carry_ref[i, :, sb:2 * sb]
        for t in range(tt):
            r0 = t * bsz
            nr = ar * sr - ai * si + buf[r0:r0 + bsz, 0:sb]
            ni = ar * si + ai * sr + buf[r0:r0 + bsz, sb:2 * sb]
            buf[r0:r0 + bsz, 0:sb] = nr
            buf[r0:r0 + bsz, sb:2 * sb] = ni
            sr, si = nr, ni
        carry_ref[i, :, 0:sb] = sr
        carry_ref[i, :, sb:2 * sb] = si
        project_out(i)

    y = y_ref[...]
    gate = jax.nn.sigmoid(_dot(y.astype(BF16), wg_ref[...]) + bg_ref[...])
    o = y * gate * jax.nn.silu(z_ref[...])
    ob = _dot(permt_ref[...], o.astype(BF16)).astype(BF16)
    r = _dot(ob, wo_ref[...])
    out_ref[...] = (x + _rms(r, pg_ref[...])).reshape(out_ref.shape)


def _s5_discretize(lam_re, lam_im, log_dt, b_re, b_im):
    dt = jnp.exp(log_dt)[:, None]
    mag = jnp.exp(lam_re * dt)
    ab_re = mag * jnp.cos(lam_im * dt)
    ab_im = mag * jnp.sin(lam_im * dt)
    den = lam_re * lam_re + lam_im * lam_im
    nr = ab_re - 1.0
    f_re = (nr * lam_re + ab_im * lam_im) / den
    f_im = (ab_im * lam_re - nr * lam_im) / den
    bb_re = f_re[..., None] * b_re - f_im[..., None] * b_im
    bb_im = f_re[..., None] * b_im + f_im[..., None] * b_re
    return ab_re, ab_im, bb_re, bb_im


def _s5_layer(x, pre_g, post_g, w_in, lam_re, lam_im, log_dt, b_re, b_im, c_re, c_im, d_skip,
              w_glu, b_glu, w_out):
    bsz, L, d = x.shape
    width = w_in.shape[1] // 2
    nblk = width // S5_CH_BLOCK
    gpb = S5_GROUPS_PER_BLOCK
    tt = S5_T
    rows = bsz * tt

    src = (np.arange(rows) % bsz) * tt + np.arange(rows) // bsz
    perm_np = np.zeros((rows, rows), np.float32)
    perm_np[np.arange(rows), src] = 1.0
    perm = jnp.asarray(perm_np, BF16)
    perm_t = jnp.asarray(perm_np.T, BF16)

    ab_re, ab_im, bb_re, bb_im = _s5_discretize(lam_re, lam_im, log_dt, b_re, b_im)
    eye = jnp.eye(gpb, dtype=F32)

    def pack_b(bb):
        t = bb.reshape(nblk, gpb, SSM_STATE, SSM_GROUP)
        return jnp.einsum('igph,gk->ikhgp', t, eye).reshape(nblk, S5_CH_BLOCK, S5_STATE_BLOCK)

    def pack_c(cc):
        t = cc.reshape(nblk, gpb, SSM_GROUP, SSM_STATE)
        return jnp.einsum('ighp,gk->igpkh', t, eye).reshape(nblk, S5_STATE_BLOCK, S5_CH_BLOCK)

    bb = jnp.concatenate([pack_b(bb_re), pack_b(bb_im)], axis=2).astype(BF16)
    cc = jnp.concatenate([pack_c(c_re), -pack_c(c_im)], axis=1).astype(BF16)
    ar = jnp.broadcast_to(ab_re.reshape(nblk, 1, S5_STATE_BLOCK), (nblk, bsz, S5_STATE_BLOCK))
    ai = jnp.broadcast_to(ab_im.reshape(nblk, 1, S5_STATE_BLOCK), (nblk, bsz, S5_STATE_BLOCK))

    xspec = pl.BlockSpec((bsz, tt, d), lambda i: (0, i, 0))
    return pl.pallas_call(
        functools.partial(_s5_kernel, tt=tt),
        out_shape=jax.ShapeDtypeStruct(x.shape, x.dtype),
        grid=(L // tt,),
        in_specs=[xspec, _full((1, d)), _full(w_in.shape), _full(perm.shape), _full(perm_t.shape),
                  _full(bb.shape), _full(cc.shape), _full(ar.shape), _full(ai.shape), _full((1, width)),
                  _full(w_glu.shape), _full((1, width)), _full(w_out.shape), _full((1, d))],
        out_specs=xspec,
        scratch_shapes=[pltpu.VMEM((rows, width), F32),
                        pltpu.VMEM((rows, width), F32),
                        pltpu.VMEM((rows, width), F32),
                        pltpu.VMEM((2, rows, 2 * S5_STATE_BLOCK), F32),
                        pltpu.VMEM((nblk, bsz, 2 * S5_STATE_BLOCK), F32)],
        compiler_params=_cparams(("arbitrary",)),
        name="s5_layer",
    )(x, pre_g.reshape(1, d), w_in.astype(BF16), perm, perm_t, bb, cc, ar, ai, d_skip.reshape(1, width),
      w_glu.astype(BF16), b_glu.reshape(1, width), w_out.astype(BF16), post_g.reshape(1, d))


def _swa_bias(rel_bias):
    W = WINDOW
    n = 4 * W
    dist = 2 * W - jnp.arange(n)
    valid = jnp.logical_and(dist >= 0, dist < W)
    dpos = jnp.maximum(dist, 0)
    max_exact = REL_BUCKETS // 2
    dist_f = jnp.maximum(dpos, 1).astype(F32)
    large = max_exact + (jnp.log(dist_f / max_exact) / math.log(REL_MAX_DIST / max_exact)
                         * (REL_BUCKETS - max_exact)).astype(jnp.int32)
    large = jnp.minimum(large, REL_BUCKETS - 1)
    bucket = jnp.where(dpos < max_exact, dpos, large)
    vec = jnp.where(valid[:, None], rel_bias[bucket].astype(F32), NEG_INF).T
    skew = jnp.tile(vec, (1, W))[:, :W * (n - 1)].reshape(vec.shape[0], W, n - 1)
    return skew[:, :, W:3 * W]


def _swa_pre_kernel(x_ref, g_ref, wqt_ref, wk_ref, wvt_ref, wz_ref, qt_ref, k_ref, vt_ref, z_ref, *, scale):
    hb = _rms(x_ref[0], g_ref[...]).astype(BF16)
    qt_ref[0] = (_dot_nt(wqt_ref[...], hb) * scale).astype(BF16)
    k_ref[0] = _dot(hb, wk_ref[...]).astype(BF16)
    vt_ref[0] = _dot_nt(wvt_ref[...], hb).astype(BF16)
    z_ref[0] = _dot(hb, wz_ref[...])


def _swa_kernel(qt_ref, kp_ref, kc_ref, vtp_ref, vtc_ref, bias_ref, sink_ref, z_ref, x_ref, wo_ref, g_ref,
                out_ref, ot_ref):
    W = WINDOW
    nwin = qt_ref.shape[2] // W
    step = pl.program_id(1)
    kall = jnp.concatenate([kp_ref[0], kc_ref[0]], axis=0)
    vtall = jnp.concatenate([vtp_ref[0], vtc_ref[0]], axis=1)
    nsub = SWA_UNIT_HEADS
    zq = jnp.zeros((HEAD_DIM, nsub * W), BF16)
    ones = jnp.ones((16, 2 * W), BF16)
    units = [(w, h, c) for w in range(nwin) for h in range(SWA_KV_HEADS) for c in range(SWA_GROUP // nsub)]

    def scores(w, h, c):
        hd0 = h * SWA_GROUP + c * nsub
        qh = jnp.concatenate([qt_ref[0, (hd0 + g) * HEAD_DIM:(hd0 + g + 1) * HEAD_DIM, w * W:(w + 1) * W]
                              for g in range(nsub)], axis=1)
        qz = jnp.concatenate([qh, zq] if h == 0 else [zq, qh], axis=0)
        return _dot(kall[w * W:(w + 2) * W], qz)

    pending = [scores(*u) for u in units[:SWA_LOOKAHEAD]]
    late = []

    def flush():
        (w, h, c), p, tail = late.pop(0)
        vones = jnp.concatenate([vtall[h * HEAD_DIM:(h + 1) * HEAD_DIM, w * W:(w + 2) * W], ones], axis=0)
        o = _dot(vones, p)
        oh = o[:HEAD_DIM] * (1.0 / (o[HEAD_DIM:HEAD_DIM + 1] + tail))
        for g in range(nsub):
            hd = h * SWA_GROUP + c * nsub + g
            ot_ref[hd * HEAD_DIM:(hd + 1) * HEAD_DIM, w * W:(w + 1) * W] = oh[:, g * W:(g + 1) * W]

    for idx, (w, h, c) in enumerate(units):
        raw = pending.pop(0)
        if idx + SWA_LOOKAHEAD < len(units):
            pending.append(scores(*units[idx + SWA_LOOKAHEAD]))
        cols = slice(c * nsub * W, (c + 1) * nsub * W)
        variant = (step == 0).astype(jnp.int32) if w == 0 else 0
        s = raw + bias_ref[variant, h, :, cols]
        sink = sink_ref[h, :, cols]
        m = jnp.maximum(jnp.max(s, axis=0, keepdims=True), sink)
        if len(late) == SWA_PV_DELAY:
            flush()
        late.append(((w, h, c), jnp.exp2(s - m).astype(BF16), jnp.exp2(sink - m)))
    while late:
        flush()
    gated = ot_ref[...].T * jax.nn.silu(z_ref[0])
    r = _dot(gated.astype(BF16), wo_ref[...])
    out_ref[0] = x_ref[0] + _rms(r, g_ref[...])


def _swa_layer(x, pre_g, post_g, w_in, sinks, w_out, rel_bias):
    bsz, L, d = x.shape
    width = SWA_HEADS * HEAD_DIM
    kvw = SWA_KV_HEADS * HEAD_DIM
    W = WINDOW
    nb = L // W
    log2e = math.log2(math.e)
    tm = 1024
    tok = lambda w_: pl.BlockSpec((1, tm, w_), lambda b, i: (b, i, 0))
    tokt = lambda w_: pl.BlockSpec((1, w_, tm), lambda b, i: (b, 0, i))
    wb = w_in.astype(BF16)
    wqt = wb[:, :width].T
    wk = wb[:, width:width + kvw]
    wvt = wb[:, width + kvw:width + 2 * kvw].T
    wz = wb[:, width + 2 * kvw:]
    qt, k, vt, z = pl.pallas_call(
        functools.partial(_swa_pre_kernel, scale=HEAD_DIM ** -0.5 * log2e),
        out_shape=[jax.ShapeDtypeStruct((bsz, width, L), BF16),
                   jax.ShapeDtypeStruct((bsz, L, kvw), BF16),
                   jax.ShapeDtypeStruct((bsz, kvw, L), BF16),
                   jax.ShapeDtypeStruct((bsz, L, width), F32)],
        grid=(bsz, L // tm),
        in_specs=[tok(d), _full((1, d)), _full(wqt.shape), _full(wk.shape), _full(wvt.shape), _full(wz.shape)],
        out_specs=[tokt(width), tok(kvw), tokt(kvw), tok(width)],
        compiler_params=_cparams(("parallel", "parallel")),
        name="swa_pre",
    )(x, pre_g.reshape(1, d), wqt, wk, wvt, wz)

    bias = jnp.transpose(_swa_bias(rel_bias.astype(F32) * log2e), (0, 2, 1))
    has_prev = (jnp.arange(2 * W) >= W)[None, :, None]
    variants = [bias,
                jnp.where(has_prev, bias, NEG_INF)]
    bias_t = jnp.stack([v.reshape(SWA_KV_HEADS, SWA_GROUP, 2 * W, W).transpose(0, 2, 1, 3)
                        .reshape(SWA_KV_HEADS, 2 * W, SWA_GROUP * W) for v in variants])
    sink = jnp.repeat(sinks.astype(F32) * log2e, W).reshape(SWA_KV_HEADS, 1, SWA_GROUP * W)

    nwin = SWA_WINDOWS_PER_STEP
    tq = nwin * W
    prev = lambda n: jnp.maximum(n * nwin - 1, 0)
    return pl.pallas_call(
        _swa_kernel,
        out_shape=jax.ShapeDtypeStruct(x.shape, x.dtype),
        grid=(bsz, L // tq),
        in_specs=[pl.BlockSpec((1, width, tq), lambda b, n: (b, 0, n)),
                  pl.BlockSpec((1, W, kvw), lambda b, n: (b, prev(n), 0)),
                  pl.BlockSpec((1, tq, kvw), lambda b, n: (b, n, 0)),
                  pl.BlockSpec((1, kvw, W), lambda b, n: (b, 0, prev(n))),
                  pl.BlockSpec((1, kvw, tq), lambda b, n: (b, 0, n)),
                  _full(bias_t.shape), _full(sink.shape),
                  pl.BlockSpec((1, tq, width), lambda b, n: (b, n, 0)),
                  pl.BlockSpec((1, tq, d), lambda b, n: (b, n, 0)),
                  _full(w_out.shape), _full((1, d))],
        out_specs=pl.BlockSpec((1, tq, d), lambda b, n: (b, n, 0)),
        scratch_shapes=[pltpu.VMEM((width, tq), F32)],
        compiler_params=_cparams(("parallel", "parallel")),
        name="swa_attn",
    )(qt, k, k, vt, vt, bias_t, sink, z, x, w_out.astype(BF16), post_g.reshape(1, d))


def _mla_pre_kernel(x_ref, g_ref, w_ref, qn_ref, kvn_ref, wq_ref, wkv_ref, wvt_ref, cq_ref, sq_ref, ck_ref, sk_ref,
                    oqn_ref, oqr_ref, okn_ref, okr_ref, ov_ref, oz_ref, *, scale):
    nope = MLA_HEADS * MLA_NOPE
    rope = MLA_HEADS * MLA_ROPE
    vw = MLA_HEADS * MLA_V
    hb = _rms(x_ref[0], g_ref[...]).astype(BF16)
    o1 = MLA_Q_RANK
    o2 = o1 + MLA_KV_RANK
    o3 = o2 + vw
    cq = _dot(hb, w_ref[:, :o1])
    ckv = _dot(hb, w_ref[:, o1:o2])
    oz_ref[0] = _dot(hb, w_ref[:, o2:o3])
    kr = _dot(hb, w_ref[:, o3:o3 + LANES])
    krs = _dot(hb, w_ref[:, o3 + LANES:o3 + 2 * LANES])
    okr_ref[0] = (kr * ck_ref[...] + krs * sk_ref[...]).astype(BF16)
    cqb = _rms(cq, qn_ref[...]).astype(BF16)
    oqn_ref[0] = (_dot_nt(wq_ref[:nope], cqb) * scale).astype(BF16)
    qr = _dot_nt(wq_ref[nope:nope + rope], cqb)
    hr = MLA_ROPE // 2
    qrs = jnp.concatenate([qr[h * MLA_ROPE + o:h * MLA_ROPE + o + hr]
                           for h in range(MLA_HEADS) for o in (hr, 0)], axis=0)
    oqr_ref[0] = ((qr * cq_ref[...] + qrs * sq_ref[...]) * scale).astype(BF16)
    ckb = _rms(ckv, kvn_ref[...]).astype(BF16)
    okn_ref[0] = _dot(ckb, wkv_ref[:, :nope]).astype(BF16)
    vt = _dot_nt(wvt_ref[...], ckb).astype(BF16)
    tk = ov_ref.shape[3]
    for c in range(ov_ref.shape[1]):
        ov_ref[0, c] = vt[:, c * tk:(c + 1) * tk]


def _mla_attn_kernel(qn_ref, qr_ref, kn_ref, kr_ref, v_ref, z_ref, x_ref, wo_ref, g_ref, out_ref,
                     qs_ref, acc_ref, m_ref, o_ref):
    tq = qn_ref.shape[2]
    tk = v_ref.shape[3]
    npairs = MLA_HEADS // 2
    i = pl.program_id(1)
    krow = lax.broadcasted_iota(jnp.int32, (tk, 2 * tq), 0)
    qcol = lax.broadcasted_iota(jnp.int32, (tk, 2 * tq), 1)
    causal = krow <= jnp.where(qcol >= tq, qcol - tq, qcol)

    zn = jnp.zeros((MLA_NOPE, tq), BF16)
    zr = jnp.zeros((LANES - MLA_ROPE, tq), BF16)
    for p in range(npairs):
        qn = qn_ref[0, p * LANES:(p + 1) * LANES, :]
        r0 = 2 * p * MLA_ROPE
        c0 = jnp.concatenate([qn[:MLA_NOPE], zn, qr_ref[0, r0:r0 + MLA_ROPE, :], zr], axis=0)
        c1 = jnp.concatenate([zn, qn[MLA_NOPE:], qr_ref[0, r0 + MLA_ROPE:r0 + 2 * MLA_ROPE, :], zr], axis=0)
        qs_ref[p] = jnp.concatenate([c0, c1], axis=1)

    m_ref[...] = jnp.full(m_ref.shape, NEG_INF, F32)
    acc_ref[...] = jnp.zeros(acc_ref.shape, F32)

    def kv_steps(blocks, nb=1):
        units = [(j, masked, p) for j, masked in blocks for p in range(npairs)]
        tkk = nb * tk
        ones = jnp.ones((acc_ref.shape[1] - LANES, tkk), BF16)

        def scores(j, p):
            ks = pl.multiple_of(j * tkk, tkk)
            kc = jnp.concatenate([kn_ref[0, pl.ds(ks, tkk), p * LANES:(p + 1) * LANES],
                                  kr_ref[0, pl.ds(ks, tkk), :]], axis=1)
            return [_dot(kc, qs_ref[p, :, c * 2 * LANES:(c + 1) * 2 * LANES]) for c in range(tq // LANES)]

        pending = [scores(j, p) for j, _, p in units[:MLA_LOOKAHEAD]]
        late = []

        def flush():
            jj, pp, alpha, pr = late.pop(0)
            vt = jnp.concatenate([v_ref[0, jj * nb + r, pp * LANES:(pp + 1) * LANES, :] for r in range(nb)], axis=1)
            acc_ref[pp] = alpha * acc_ref[pp] + _dot(jnp.concatenate([vt, ones], axis=0), pr)

        for idx, (j, masked, p) in enumerate(units):
            s = pending.pop(0)
            if idx + MLA_LOOKAHEAD < len(units):
                nxt = units[idx + MLA_LOOKAHEAD]
                pending.append(scores(nxt[0], nxt[2]))
            probs, alphas = [], []
            for c in range(2 * tq // LANES):
                sc = s[c // 2][:, (c % 2) * LANES:(c % 2 + 1) * LANES]
                if masked:
                    sc = jnp.where(causal[:, c * LANES:(c + 1) * LANES], sc, NEG_INF)
                m_prev = m_ref[p, :, c * LANES:(c + 1) * LANES]
                m_new = jnp.maximum(m_prev, jnp.max(sc, axis=0, keepdims=True))
                alphas.append(jnp.exp2(m_prev - m_new))
                probs.append(jnp.exp2(sc - m_new).astype(BF16))
                m_ref[p, :, c * LANES:(c + 1) * LANES] = m_new
            if len(late) == MLA_PV_DELAY:
                flush()
            late.append((j, p, jnp.concatenate(alphas, axis=1), jnp.concatenate(probs, axis=1)))
        while late:
            flush()

    def body(jj, c):
        kv_steps([(jj, False)], nb=2)
        return c

    lax.fori_loop(0, i // 2, body, 0)

    @pl.when(i % 2 == 1)
    def _():
        kv_steps([(i - 1, False), (i, True)])

    @pl.when(i % 2 == 0)
    def _():
        kv_steps([(i, True)])
    for p in range(npairs):
        a = acc_ref[p]
        a = a[:LANES] * (1.0 / a[LANES:LANES + 1])
        ot = jnp.concatenate([a[:MLA_V, :tq], a[MLA_V:, tq:]], axis=0)
        o_ref[:, p * LANES:(p + 1) * LANES] = ot.T
    gated = o_ref[...] * jax.nn.silu(z_ref[0])
    r = _dot(gated.astype(BF16), wo_ref[...])
    out_ref[0] = x_ref[0] + _rms(r, g_ref[...])


def _mla_layer(x, pre_g, post_g, w_in, q_norm, kv_norm, w_uq, w_ukv, w_out):
    bsz, L, d = x.shape
    H = MLA_HEADS
    dq = MLA_NOPE + MLA_ROPE
    nope = H * MLA_NOPE
    rope = H * MLA_ROPE
    vw = H * MLA_V
    half = MLA_ROPE // 2
    o_kr = MLA_Q_RANK + MLA_KV_RANK
    o_z = o_kr + MLA_ROPE
    wb = w_in.astype(BF16)
    w_kr = wb[:, o_kr:o_z]
    w_krs = jnp.concatenate([w_kr[:, half:], w_kr[:, :half]], axis=1)
    reps = LANES // MLA_ROPE
    w1 = jnp.concatenate([wb[:, :o_kr], wb[:, o_z:]] + [w_kr] * reps + [w_krs] * reps, axis=1)
    wq3 = w_uq.astype(BF16).reshape(MLA_Q_RANK, H, dq)
    wqt = jnp.concatenate([wq3[:, :, :MLA_NOPE].reshape(MLA_Q_RANK, nope),
                           wq3[:, :, MLA_NOPE:].reshape(MLA_Q_RANK, rope)], axis=1).T
    wkv3 = w_ukv.astype(BF16).reshape(MLA_KV_RANK, H, MLA_NOPE + MLA_V)
    wkn = wkv3[:, :, :MLA_NOPE].reshape(MLA_KV_RANK, nope)
    wvt = wkv3[:, :, MLA_NOPE:].reshape(MLA_KV_RANK, vw).T
    inv = ROPE_BASE ** (-jnp.arange(0, MLA_ROPE, 2, dtype=F32) / MLA_ROPE)
    ang = jnp.arange(L, dtype=F32)[:, None] * inv[None, :]
    cos, sin = jnp.cos(ang), jnp.sin(ang)
    cos32 = jnp.concatenate([cos, cos], axis=1)
    sin32 = jnp.concatenate([-sin, sin], axis=1)
    cos_k, sin_k = jnp.tile(cos32, (1, LANES // MLA_ROPE)), jnp.tile(sin32, (1, LANES // MLA_ROPE))
    cos_q, sin_q = jnp.tile(cos32, (1, H)).T, jnp.tile(sin32, (1, H)).T

    tm = 1024
    tk = MLA_TK
    tok = lambda w_: pl.BlockSpec((1, tm, w_), lambda b, i: (b, i, 0))
    tokt = lambda w_: pl.BlockSpec((1, w_, tm), lambda b, i: (b, 0, i))
    scale = dq ** -0.5 * math.log2(math.e)
    qn, qr, kn, kr, v, z = pl.pallas_call(
        functools.partial(_mla_pre_kernel, scale=scale),
        out_shape=[jax.ShapeDtypeStruct((bsz, nope, L), BF16),
                   jax.ShapeDtypeStruct((bsz, rope, L), BF16),
                   jax.ShapeDtypeStruct((bsz, L, nope), BF16),
                   jax.ShapeDtypeStruct((bsz, L, LANES), BF16),
                   jax.ShapeDtypeStruct((bsz, L // tk, vw, tk), BF16),
                   jax.ShapeDtypeStruct((bsz, L, vw), F32)],
        grid=(bsz, L // tm),
        in_specs=[tok(d), _full((1, d)), _full(w1.shape), _full((1, MLA_Q_RANK)), _full((1, MLA_KV_RANK)),
                  _full(wqt.shape), _full(wkn.shape), _full(wvt.shape),
                  pl.BlockSpec((rope, tm), lambda b, i: (0, i)), pl.BlockSpec((rope, tm), lambda b, i: (0, i)),
                  pl.BlockSpec((tm, LANES), lambda b, i: (i, 0)), pl.BlockSpec((tm, LANES), lambda b, i: (i, 0))],
        out_specs=[tokt(nope), tokt(rope), tok(nope), tok(LANES),
                   pl.BlockSpec((1, tm // tk, vw, tk), lambda b, i: (b, i, 0, 0)), tok(vw)],
        compiler_params=_cparams(("parallel", "parallel")),
        name="mla_pre",
    )(x, pre_g.reshape(1, d), w1, q_norm.reshape(1, -1), kv_norm.reshape(1, -1), wqt, wkn, wvt,
      cos_q, sin_q, cos_k, sin_k)

    tq = MLA_TQ
    npairs = H // 2
    qspec = lambda w_: pl.BlockSpec((1, w_, tq), lambda b, i: (b, 0, i))
    kspec = lambda w_: pl.BlockSpec((1, L, w_), lambda b, i: (b, 0, 0))
    rowspec = lambda w_: pl.BlockSpec((1, tq, w_), lambda b, i: (b, i, 0))
    return pl.pallas_call(
        _mla_attn_kernel,
        out_shape=jax.ShapeDtypeStruct(x.shape, x.dtype),
        grid=(bsz, L // tq),
        in_specs=[qspec(nope), qspec(rope), kspec(nope), kspec(LANES),
                  pl.BlockSpec((1, L // tk, vw, tk), lambda b, i: (b, 0, 0, 0)),
                  rowspec(vw), rowspec(d), _full(w_out.shape), _full((1, d))],
        out_specs=rowspec(d),
        scratch_shapes=[pltpu.VMEM((npairs, 2 * LANES, 2 * tq), BF16),
                        pltpu.VMEM((npairs, LANES + 16, 2 * tq), F32),
                        pltpu.VMEM((npairs, 1, 2 * tq), F32),
                        pltpu.VMEM((tq, vw), F32)],
        compiler_params=_cparams(("parallel", "arbitrary")),
        name="mla_attn",
    )(qn, qr, kn, kr, v, z, x, w_out.astype(BF16), post_g.reshape(1, d))


def _sgu_kernel(x_ref, g_ref, w_ref, lng_ref, lnb_ref, ws_ref, bs_ref, wo_ref, pg_ref, out_ref, s_ref):
    width = wo_ref.shape[0]
    tm = x_ref.shape[1]
    lane = lax.broadcasted_iota(jnp.int32, (1, LANES), 1)
    lo = lane < HALF
    x = x_ref[0]
    hb = _rms(x, g_ref[...]).astype(BF16)
    v = jax.nn.gelu(_dot(hb, w_ref[:, width:2 * width]))
    mu = jnp.mean(v, axis=-1, keepdims=True)
    vc = v - mu
    var = jnp.mean(vc * vc, axis=-1, keepdims=True)
    vb = (vc * lax.rsqrt(var + EPS) * lng_ref[...] + lnb_ref[...]).astype(BF16)
    group = SGU_STACK
    for c0 in range(0, tm // SGU_CHUNK, group):
        for jj in range(width // LANES):
            blk = jnp.concatenate([vb[c * SGU_CHUNK:(c + 1) * SGU_CHUNK, jj * LANES:(jj + 1) * LANES]
                                   for c in range(c0, c0 + group)], axis=1)
            r = _dot(ws_ref[jj], blk)
            for k in range(group):
                c = c0 + k
                s_ref[c * SGU_CHUNK:(c + 1) * SGU_CHUNK, jj * LANES:(jj + 1) * LANES] = (
                    jnp.where(lo, r[:SGU_CHUNK, k * LANES:(k + 1) * LANES],
                              r[SGU_CHUNK:, k * LANES:(k + 1) * LANES]) + bs_ref[jj])
    u = jax.nn.gelu(_dot(hb, w_ref[:, :width]))
    z = _dot(hb, w_ref[:, 2 * width:])
    o = u * s_ref[...] * jax.nn.silu(z)
    r = _dot(o.astype(BF16), wo_ref[...])
    out_ref[0] = x + _rms(r, pg_ref[...])


def _sgu_layer(x, pre_g, post_g, w_in, ln_g, ln_b, w_s, b_s, w_out):
    bsz, L, d = x.shape
    width = w_out.shape[0]
    T = SGU_CHUNK
    gd = width // SGU_GROUPS
    tril = jnp.tril(jnp.ones((T, T), dtype=bool))
    ws = jnp.where(tril[None], w_s, 0.0).reshape(SGU_GROUPS // 2, 2 * T, T).astype(BF16)
    bs = jnp.repeat(b_s.astype(F32).T, gd, axis=1)
    bs = bs.reshape(T, width // LANES, LANES).transpose(1, 0, 2)
    tm = 1024
    return pl.pallas_call(
        _sgu_kernel,
        out_shape=jax.ShapeDtypeStruct(x.shape, x.dtype),
        grid=(bsz, L // tm),
        in_specs=[pl.BlockSpec((1, tm, d), lambda b, i: (b, i, 0)),
                  _full((1, d)), _full(w_in.shape), _full((1, width)), _full((1, width)),
                  _full(ws.shape), _full(bs.shape), _full(w_out.shape), _full((1, d))],
        out_specs=pl.BlockSpec((1, tm, d), lambda b, i: (b, i, 0)),
        scratch_shapes=[pltpu.VMEM((tm, width), F32)],
        compiler_params=_cparams(("parallel", "parallel")),
        name="sgu",
    )(x, pre_g.reshape(1, d), w_in.astype(BF16), ln_g.reshape(1, width), ln_b.reshape(1, width),
      ws, bs, w_out.astype(BF16), post_g.reshape(1, d))


def kernel(x, pre_norm, post_norm, rel_bias, a_w_in, a_lam_re, a_lam_im, a_log_dt, a_b_re, a_b_im, a_c_re, a_c_im, a_d, a_w_glu, a_b_glu, a_w_out, b_w_in, b_sinks, b_w_out, c_w_in, c_q_norm, c_kv_norm, c_w_uq, c_w_ukv, c_w_out, d_w_in, d_ln_g, d_ln_b, d_w_s, d_b_s, d_w_out):
    depth = pre_norm.shape[0]
    for i in range(depth):
        kind, j = i % 4, i // 4
        if kind == 0:
            x = _s5_layer(x, pre_norm[i], post_norm[i], a_w_in[j], a_lam_re[j], a_lam_im[j], a_log_dt[j],
                          a_b_re[j], a_b_im[j], a_c_re[j], a_c_im[j], a_d[j], a_w_glu[j], a_b_glu[j],
                          a_w_out[j])
        elif kind == 1:
            x = _swa_layer(x, pre_norm[i], post_norm[i], b_w_in[j], b_sinks[j], b_w_out[j], rel_bias)
        elif kind == 2:
            x = _mla_layer(x, pre_norm[i], post_norm[i], c_w_in[j], c_q_norm[j], c_kv_norm[j], c_w_uq[j],
                           c_w_ukv[j], c_w_out[j])
        else:
            x = _sgu_layer(x, pre_norm[i], post_norm[i], d_w_in[j], d_ln_g[j], d_ln_b[j], d_w_s[j],
                           d_b_s[j], d_w_out[j])
    return x
```

```python
import functools
import math

import jax
import jax.numpy as jnp
import numpy as np
from jax import lax
from jax.experimental import pallas as pl
from jax.experimental.pallas import tpu as pltpu

F32 = jnp.float32
BF16 = jnp.bfloat16

D_MODEL = 1024
EPS = 1e-6
NEG_INF = -1e30
LANES = 128
HALF = LANES // 2

SSM_GROUP = 16
SSM_STATE = 64
S5_CH_BLOCK = LANES
S5_GROUPS_PER_BLOCK = S5_CH_BLOCK // SSM_GROUP
S5_STATE_BLOCK = S5_GROUPS_PER_BLOCK * SSM_STATE
S5_T = 64

HEAD_DIM = 64
SWA_HEADS = 16
SWA_KV_HEADS = 2
SWA_GROUP = SWA_HEADS // SWA_KV_HEADS
WINDOW = 128
SWA_WINDOWS_PER_STEP = 4
SWA_LOOKAHEAD = 2
SWA_UNIT_HEADS = 8
SWA_PV_DELAY = 1
REL_BUCKETS = 32
REL_MAX_DIST = 128

MLA_HEADS = 16
MLA_NOPE = 64
MLA_ROPE = 32
MLA_V = 64
MLA_KV_RANK = 256
MLA_Q_RANK = 768
ROPE_BASE = 10000.0
MLA_TQ = 256
MLA_TK = 256
MLA_LOOKAHEAD = 4
MLA_PV_DELAY = 2

SGU_CHUNK = 128
SGU_GROUPS = 16
SGU_STACK = 4

VMEM_LIMIT = 56 * 1024 * 1024


def _cparams(sem):
    return pltpu.CompilerParams(dimension_semantics=sem, vmem_limit_bytes=VMEM_LIMIT)


def _rms(x, g):
    return x * lax.rsqrt(jnp.mean(x * x, axis=-1, keepdims=True) + EPS) * g


def _dot(a, b):
    return jnp.dot(a, b, preferred_element_type=F32)


def _dot_nt(a, b):
    return lax.dot_general(a, b, (((1,), (1,)), ((), ())), preferred_element_type=F32)


def _full(shape):
    n = len(shape)
    return pl.BlockSpec(shape, lambda *_: (0,) * n, pipeline_mode=pl.Buffered(1))


def _s5_kernel(x_ref, g_ref, w_ref, perm_ref, permt_ref, bb_ref, cc_ref, ar_ref, ai_ref, d_ref,
               wg_ref, bg_ref, wo_ref, pg_ref, out_ref, u_ref, z_ref, y_ref, s_ref, carry_ref, *, tt):
    bsz = x_ref.shape[0]
    width = wg_ref.shape[0]
    rows = bsz * tt
    nblk = bb_ref.shape[0]
    sb = S5_STATE_BLOCK

    @pl.when(pl.program_id(0) == 0)
    def _():
        carry_ref[...] = jnp.zeros_like(carry_ref)

    x = x_ref[...].reshape(rows, x_ref.shape[2])
    hb = _rms(x, g_ref[...]).astype(BF16)
    hb = _dot(perm_ref[...], hb).astype(BF16)
    u_ref[...] = _dot(hb, w_ref[:, :width].astype(BF16))
    z_ref[...] = _dot(hb, w_ref[:, width:].astype(BF16))

    nbuf = s_ref.shape[0]

    def project_in(i):
        s_ref[i % nbuf] = _dot(u_ref[:, i * LANES:(i + 1) * LANES].astype(BF16), bb_ref[i])

    def project_out(i):
        ub = u_ref[:, i * LANES:(i + 1) * LANES]
        y = _dot(s_ref[i % nbuf].astype(BF16), cc_ref[i]) + d_ref[:, i * LANES:(i + 1) * LANES] * ub
        y_ref[:, i * LANES:(i + 1) * LANES] = jax.nn.gelu(y)

    project_in(0)
    for i in range(nblk):
        if i + 1 < nblk:
            project_in(i + 1)
        buf = s_ref.at[i % nbuf]
        ar = ar_ref[i]
        ai = ai_ref[i]
        sr = carry_ref[i, :, 0:sb]
        si = carry_ref[i, :, sb:2 * sb]
        for t in range(tt):
            r0 = t * bsz
            nr = ar * sr - ai * si + buf[r0:r0 + bsz, 0:sb]
            ni = ar * si + ai * sr + buf[r0:r0 + bsz, sb:2 * sb]
            buf[r0:r0 + bsz, 0:sb] = nr
            buf[r0:r0 + bsz, sb:2 * sb] = ni
            sr, si = nr, ni
        carry_ref[i, :, 0:sb] = sr
        carry_ref[i, :, sb:2 * sb] = si
        project_out(i)

    y = y_ref[...]
    gate = jax.nn.sigmoid(_dot(y.astype(BF16), wg_ref[...].astype(BF16)) + bg_ref[...])
    o = y * gate * jax.nn.silu(z_ref[...])
    ob = _dot(permt_ref[...], o.astype(BF16)).astype(BF16)
    r = _dot(ob, wo_ref[...].astype(BF16))
    out_ref[...] = (x + _rms(r, pg_ref[...])).reshape(out_ref.shape)


def _s5_discretize(lam_re, lam_im, log_dt, b_re, b_im):
    dt = jnp.exp(log_dt)[:, None]
    mag = jnp.exp(lam_re * dt)
    ab_re = mag * jnp.cos(lam_im * dt)
    ab_im = mag * jnp.sin(lam_im * dt)
    den = lam_re * lam_re + lam_im * lam_im
    nr = ab_re - 1.0
    f_re = (nr * lam_re + ab_im * lam_im) / den
    f_im = (ab_im * lam_re - nr * lam_im) / den
    bb_re = f_re[..., None] * b_re - f_im[..., None] * b_im
    bb_im = f_re[..., None] * b_im + f_im[..., None] * b_re
    return ab_re, ab_im, bb_re, bb_im


def _s5_layer(x, pre_g, post_g, w_in, lam_re, lam_im, log_dt, b_re, b_im, c_re, c_im, d_skip,
              w_glu, b_glu, w_out):
    bsz, L, d = x.shape
    width = w_in.shape[1] // 2
    nblk = width // S5_CH_BLOCK
    gpb = S5_GROUPS_PER_BLOCK
    tt = S5_T
    rows = bsz * tt

    src = (np.arange(rows) % bsz) * tt + np.arange(rows) // bsz
    perm_np = np.zeros((rows, rows), np.float32)
    perm_np[np.arange(rows), src] = 1.0
    perm = jnp.asarray(perm_np, BF16)
    perm_t = jnp.asarray(perm_np.T, BF16)

    ab_re, ab_im, bb_re, bb_im = _s5_discretize(lam_re, lam_im, log_dt, b_re, b_im)
    eye = jnp.eye(gpb, dtype=F32)

    def pack_b(bb):
        t = bb.reshape(nblk, gpb, SSM_STATE, SSM_GROUP)
        return jnp.einsum('igph,gk->ikhgp', t, eye).reshape(nblk, S5_CH_BLOCK, S5_STATE_BLOCK)

    def pack_c(cc):
        t = cc.reshape(nblk, gpb, SSM_GROUP, SSM_STATE)
        return jnp.einsum('ighp,gk->igpkh', t, eye).reshape(nblk, S5_STATE_BLOCK, S5_CH_BLOCK)

    bb = jnp.concatenate([pack_b(bb_re), pack_b(bb_im)], axis=2).astype(BF16)
    cc = jnp.concatenate([pack_c(c_re), -pack_c(c_im)], axis=1).astype(BF16)
    ar = jnp.broadcast_to(ab_re.reshape(nblk, 1, S5_STATE_BLOCK), (nblk, bsz, S5_STATE_BLOCK))
    ai = jnp.broadcast_to(ab_im.reshape(nblk, 1, S5_STATE_BLOCK), (nblk, bsz, S5_STATE_BLOCK))

    xspec = pl.BlockSpec((bsz, tt, d), lambda i: (0, i, 0))
    return pl.pallas_call(
        functools.partial(_s5_kernel, tt=tt),
        out_shape=jax.ShapeDtypeStruct(x.shape, x.dtype),
        grid=(L // tt,),
        in_specs=[xspec, _full((1, d)), _full(w_in.shape), _full(perm.shape), _full(perm_t.shape),
                  _full(bb.shape), _full(cc.shape), _full(ar.shape), _full(ai.shape), _full((1, width)),
                  _full(w_glu.shape), _full((1, width)), _full(w_out.shape), _full((1, d))],
        out_specs=xspec,
        scratch_shapes=[pltpu.VMEM((rows, width), F32),
                        pltpu.VMEM((rows, width), F32),
                        pltpu.VMEM((rows, width), F32),
                        pltpu.VMEM((2, rows, 2 * S5_STATE_BLOCK), F32),
                        pltpu.VMEM((nblk, bsz, 2 * S5_STATE_BLOCK), F32)],
        compiler_params=_cparams(("arbitrary",)),
        name="s5_layer",
    )(x, pre_g.reshape(1, d), w_in, perm, perm_t, bb, cc, ar, ai, d_skip.reshape(1, width),
      w_glu, b_glu.reshape(1, width), w_out, post_g.reshape(1, d))


def _swa_bias(rel_bias):
    W = WINDOW
    n = 4 * W
    dist = 2 * W - jnp.arange(n)
    valid = jnp.logical_and(dist >= 0, dist < W)
    dpos = jnp.maximum(dist, 0)
    max_exact = REL_BUCKETS // 2
    dist_f = jnp.maximum(dpos, 1).astype(F32)
    large = max_exact + (jnp.log(dist_f / max_exact) / math.log(REL_MAX_DIST / max_exact)
                         * (REL_BUCKETS - max_exact)).astype(jnp.int32)
    large = jnp.minimum(large, REL_BUCKETS - 1)
    bucket = jnp.where(dpos < max_exact, dpos, large)
    vec = jnp.where(valid[:, None], rel_bias[bucket].astype(F32), NEG_INF).T
    skew = jnp.tile(vec, (1, W))[:, :W * (n - 1)].reshape(vec.shape[0], W, n - 1)
    return skew[:, :, W:3 * W]


def _swa_pre_kernel(x_ref, g_ref, wqt_ref, wk_ref, wvt_ref, wz_ref, qt_ref, k_ref, vt_ref, z_ref, *, scale):
    hb = _rms(x_ref[0], g_ref[...]).astype(BF16)
    qt_ref[0] = (_dot_nt(wqt_ref[...], hb) * scale).astype(BF16)
    k_ref[0] = _dot(hb, wk_ref[...]).astype(BF16)
    vt_ref[0] = _dot_nt(wvt_ref[...], hb).astype(BF16)
    z_ref[0] = _dot(hb, wz_ref[...])


def _swa_kernel(qt_ref, kp_ref, kc_ref, vtp_ref, vtc_ref, bias_ref, sink_ref, z_ref, x_ref, wo_ref, g_ref,
                out_ref, ot_ref):
    W = WINDOW
    nwin = qt_ref.shape[2] // W
    step = pl.program_id(1)
    kall = jnp.concatenate([kp_ref[0], kc_ref[0]], axis=0)
    vtall = jnp.concatenate([vtp_ref[0], vtc_ref[0]], axis=1)
    nsub = SWA_UNIT_HEADS
    zq = jnp.zeros((HEAD_DIM, nsub * W), BF16)
    ones = jnp.ones((16, 2 * W), BF16)
    units = [(w, h, c) for w in range(nwin) for h in range(SWA_KV_HEADS) for c in range(SWA_GROUP // nsub)]

    def scores(w, h, c):
        hd0 = h * SWA_GROUP + c * nsub
        qh = jnp.concatenate([qt_ref[0, (hd0 + g) * HEAD_DIM:(hd0 + g + 1) * HEAD_DIM, w * W:(w + 1) * W]
                              for g in range(nsub)], axis=1)
        qz = jnp.concatenate([qh, zq] if h == 0 else [zq, qh], axis=0)
        return _dot(kall[w * W:(w + 2) * W], qz)

    pending = [scores(*u) for u in units[:SWA_LOOKAHEAD]]
    late = []

    def flush():
        (w, h, c), p, tail = late.pop(0)
        vones = jnp.concatenate([vtall[h * HEAD_DIM:(h + 1) * HEAD_DIM, w * W:(w + 2) * W], ones], axis=0)
        o = _dot(vones, p)
        oh = o[:HEAD_DIM] * (1.0 / (o[HEAD_DIM:HEAD_DIM + 1] + tail))
        for g in range(nsub):
            hd = h * SWA_GROUP + c * nsub + g
            ot_ref[hd * HEAD_DIM:(hd + 1) * HEAD_DIM, w * W:(w + 1) * W] = oh[:, g * W:(g + 1) * W]

    for idx, (w, h, c) in enumerate(units):
        raw = pending.pop(0)
        if idx + SWA_LOOKAHEAD < len(units):
            pending.append(scores(*units[idx + SWA_LOOKAHEAD]))
        cols = slice(c * nsub * W, (c + 1) * nsub * W)
        variant = (step == 0).astype(jnp.int32) if w == 0 else 0
        s = raw + bias_ref[variant, h, :, cols]
        sink = sink_ref[h, :, cols]
        m = jnp.maximum(jnp.max(s, axis=0, keepdims=True), sink)
        if len(late) == SWA_PV_DELAY:
            flush()
        late.append(((w, h, c), jnp.exp2(s - m).astype(BF16), jnp.exp2(sink - m)))
    while late:
        flush()
    gated = ot_ref[...].T * jax.nn.silu(z_ref[0])
    r = _dot(gated.astype(BF16), wo_ref[...].astype(BF16))
    out_ref[0] = x_ref[0] + _rms(r, g_ref[...])


def _swa_layer(x, pre_g, post_g, w_in, sinks, w_out, rel_bias):
    bsz, L, d = x.shape
    width = SWA_HEADS * HEAD_DIM
    kvw = SWA_KV_HEADS * HEAD_DIM
    W = WINDOW
    nb = L // W
    log2e = math.log2(math.e)
    tm = 1024
    tok = lambda w_: pl.BlockSpec((1, tm, w_), lambda b, i: (b, i, 0))
    tokt = lambda w_: pl.BlockSpec((1, w_, tm), lambda b, i: (b, 0, i))
    wb = w_in.astype(BF16)
    wqt = wb[:, :width].T
    wk = wb[:, width:width + kvw]
    wvt = wb[:, width + kvw:width + 2 * kvw].T
    wz = wb[:, width + 2 * kvw:]
    qt, k, vt, z = pl.pallas_call(
        functools.partial(_swa_pre_kernel, scale=HEAD_DIM ** -0.5 * log2e),
        out_shape=[jax.ShapeDtypeStruct((bsz, width, L), BF16),
                   jax.ShapeDtypeStruct((bsz, L, kvw), BF16),
                   jax.ShapeDtypeStruct((bsz, kvw, L), BF16),
                   jax.ShapeDtypeStruct((bsz, L, width), F32)],
        grid=(bsz, L // tm),
        in_specs=[tok(d), _full((1, d)), _full(wqt.shape), _full(wk.shape), _full(wvt.shape), _full(wz.shape)],
        out_specs=[tokt(width), tok(kvw), tokt(kvw), tok(width)],
        compiler_params=_cparams(("parallel", "parallel")),
        name="swa_pre",
    )(x, pre_g.reshape(1, d), wqt, wk, wvt, wz)

    bias = jnp.transpose(_swa_bias(rel_bias.astype(F32) * log2e), (0, 2, 1))
    has_prev = (jnp.arange(2 * W) >= W)[None, :, None]
    variants = [bias,
                jnp.where(has_prev, bias, NEG_INF)]
    bias_t = jnp.stack([v.reshape(SWA_KV_HEADS, SWA_GROUP, 2 * W, W).transpose(0, 2, 1, 3)
                        .reshape(SWA_KV_HEADS, 2 * W, SWA_GROUP * W) for v in variants])
    sink = jnp.repeat(sinks.astype(F32) * log2e, W).reshape(SWA_KV_HEADS, 1, SWA_GROUP * W)

    nwin = SWA_WINDOWS_PER_STEP
    tq = nwin * W
    prev = lambda n: jnp.maximum(n * nwin - 1, 0)
    return pl.pallas_call(
        _swa_kernel,
        out_shape=jax.ShapeDtypeStruct(x.shape, x.dtype),
        grid=(bsz, L // tq),
        in_specs=[pl.BlockSpec((1, width, tq), lambda b, n: (b, 0, n)),
                  pl.BlockSpec((1, W, kvw), lambda b, n: (b, prev(n), 0)),
                  pl.BlockSpec((1, tq, kvw), lambda b, n: (b, n, 0)),
                  pl.BlockSpec((1, kvw, W), lambda b, n: (b, 0, prev(n))),
                  pl.BlockSpec((1, kvw, tq), lambda b, n: (b, 0, n)),
                  _full(bias_t.shape), _full(sink.shape),
                  pl.BlockSpec((1, tq, width), lambda b, n: (b, n, 0)),
                  pl.BlockSpec((1, tq, d), lambda b, n: (b, n, 0)),
                  _full(w_out.shape), _full((1, d))],
        out_specs=pl.BlockSpec((1, tq, d), lambda b, n: (b, n, 0)),
        scratch_shapes=[pltpu.VMEM((width, tq), F32)],
        compiler_params=_cparams(("parallel", "parallel")),
        name="swa_attn",
    )(qt, k, k, vt, vt, bias_t, sink, z, x, w_out, post_g.reshape(1, d))


def _mla_pre_kernel(x_ref, g_ref, w_ref, qn_ref, kvn_ref, wq_ref, wkv_ref, wvt_ref, cq_ref, sq_ref, ck_ref, sk_ref,
                    oqn_ref, oqr_ref, okn_ref, okr_ref, ov_ref, oz_ref, *, scale):
    nope = MLA_HEADS * MLA_NOPE
    rope = MLA_HEADS * MLA_ROPE
    vw = MLA_HEADS * MLA_V
    hb = _rms(x_ref[0], g_ref[...]).astype(BF16)
    o1 = MLA_Q_RANK
    o2 = o1 + MLA_KV_RANK
    o3 = o2 + vw
    cq = _dot(hb, w_ref[:, :o1])
    ckv = _dot(hb, w_ref[:, o1:o2])
    oz_ref[0] = _dot(hb, w_ref[:, o2:o3])
    kr = _dot(hb, w_ref[:, o3:o3 + LANES])
    krs = _dot(hb, w_ref[:, o3 + LANES:o3 + 2 * LANES])
    okr_ref[0] = (kr * ck_ref[...] + krs * sk_ref[...]).astype(BF16)
    cqb = _rms(cq, qn_ref[...]).astype(BF16)
    oqn_ref[0] = (_dot_nt(wq_ref[:nope], cqb) * scale).astype(BF16)
    qr = _dot_nt(wq_ref[nope:nope + rope], cqb)
    hr = MLA_ROPE // 2
    qrs = jnp.concatenate([qr[h * MLA_ROPE + o:h * MLA_ROPE + o + hr]
                           for h in range(MLA_HEADS) for o in (hr, 0)], axis=0)
    oqr_ref[0] = ((qr * cq_ref[...] + qrs * sq_ref[...]) * scale).astype(BF16)
    ckb = _rms(ckv, kvn_ref[...]).astype(BF16)
    okn_ref[0] = _dot(ckb, wkv_ref[:, :nope]).astype(BF16)
    vt = _dot_nt(wvt_ref[...], ckb).astype(BF16)
    tk = ov_ref.shape[3]
    for c in range(ov_ref.shape[1]):
        ov_ref[0, c] = vt[:, c * tk:(c + 1) * tk]


def _mla_attn_kernel(qn_ref, qr_ref, kn_ref, kr_ref, v_ref, z_ref, x_ref, wo_ref, g_ref, out_ref,
                     qs_ref, acc_ref, m_ref, o_ref):
    tq = qn_ref.shape[2]
    tk = v_ref.shape[3]
    npairs = MLA_HEADS // 2
    i = pl.program_id(1)
    krow = lax.broadcasted_iota(jnp.int32, (tk, 2 * tq), 0)
    qcol = lax.broadcasted_iota(jnp.int32, (tk, 2 * tq), 1)
    causal = krow <= jnp.where(qcol >= tq, qcol - tq, qcol)

    zn = jnp.zeros((MLA_NOPE, tq), BF16)
    zr = jnp.zeros((LANES - MLA_ROPE, tq), BF16)
    for p in range(npairs):
        qn = qn_ref[0, p * LANES:(p + 1) * LANES, :]
        r0 = 2 * p * MLA_ROPE
        c0 = jnp.concatenate([qn[:MLA_NOPE], zn, qr_ref[0, r0:r0 + MLA_ROPE, :], zr], axis=0)
        c1 = jnp.concatenate([zn, qn[MLA_NOPE:], qr_ref[0, r0 + MLA_ROPE:r0 + 2 * MLA_ROPE, :], zr], axis=0)
        qs_ref[p] = jnp.concatenate([c0, c1], axis=1)

    m_ref[...] = jnp.full(m_ref.shape, NEG_INF, F32)
    acc_ref[...] = jnp.zeros(acc_ref.shape, F32)
    ones = jnp.ones((acc_ref.shape[1] - LANES, tk), BF16)

    def kv_steps(blocks):
        units = [(j, masked, p) for j, masked in blocks for p in range(npairs)]

        def scores(j, p):
            ks = pl.multiple_of(j * tk, tk)
            kc = jnp.concatenate([kn_ref[0, pl.ds(ks, tk), p * LANES:(p + 1) * LANES],
                                  kr_ref[0, pl.ds(ks, tk), :]], axis=1)
            return [_dot(kc, qs_ref[p, :, c * 2 * LANES:(c + 1) * 2 * LANES]) for c in range(tq // LANES)]

        pending = [scores(j, p) for j, _, p in units[:MLA_LOOKAHEAD]]
        late = []

        def flush():
            jj, pp, alpha, pr = late.pop(0)
            vones = jnp.concatenate([v_ref[0, jj, pp * LANES:(pp + 1) * LANES, :], ones], axis=0)
            acc_ref[pp] = alpha * acc_ref[pp] + _dot(vones, pr)

        for idx, (j, masked, p) in enumerate(units):
            s = pending.pop(0)
            if idx + MLA_LOOKAHEAD < len(units):
                nxt = units[idx + MLA_LOOKAHEAD]
                pending.append(scores(nxt[0], nxt[2]))
            probs, alphas = [], []
            for c in range(2 * tq // LANES):
                sc = s[c // 2][:, (c % 2) * LANES:(c % 2 + 1) * LANES]
                if masked:
                    sc = jnp.where(causal[:, c * LANES:(c + 1) * LANES], sc, NEG_INF)
                m_prev = m_ref[p, :, c * LANES:(c + 1) * LANES]
                m_new = jnp.maximum(m_prev, jnp.max(sc, axis=0, keepdims=True))
                alphas.append(jnp.exp2(m_prev - m_new))
                probs.append(jnp.exp2(sc - m_new).astype(BF16))
                m_ref[p, :, c * LANES:(c + 1) * LANES] = m_new
            if len(late) == MLA_PV_DELAY:
                flush()
            late.append((j, p, jnp.concatenate(alphas, axis=1), jnp.concatenate(probs, axis=1)))
        while late:
            flush()

    def body(jj, c):
        kv_steps([(2 * jj, False), (2 * jj + 1, False)])
        return c

    lax.fori_loop(0, i // 2, body, 0)

    @pl.when(i % 2 == 1)
    def _():
        kv_steps([(i - 1, False), (i, True)])

    @pl.when(i % 2 == 0)
    def _():
        kv_steps([(i, True)])
    for p in range(npairs):
        a = acc_ref[p]
        a = a[:LANES] * (1.0 / a[LANES:LANES + 1])
        ot = jnp.concatenate([a[:MLA_V, :tq], a[MLA_V:, tq:]], axis=0)
        o_ref[:, p * LANES:(p + 1) * LANES] = ot.T
    gated = o_ref[...] * jax.nn.silu(z_ref[0])
    r = _dot(gated.astype(BF16), wo_ref[...].astype(BF16))
    out_ref[0] = x_ref[0] + _rms(r, g_ref[...])


def _mla_layer(x, pre_g, post_g, w_in, q_norm, kv_norm, w_uq, w_ukv, w_out):
    bsz, L, d = x.shape
    H = MLA_HEADS
    dq = MLA_NOPE + MLA_ROPE
    nope = H * MLA_NOPE
    rope = H * MLA_ROPE
    vw = H * MLA_V
    half = MLA_ROPE // 2
    o_kr = MLA_Q_RANK + MLA_KV_RANK
    o_z = o_kr + MLA_ROPE
    wb = w_in.astype(BF16)
    w_kr = wb[:, o_kr:o_z]
    w_krs = jnp.concatenate([w_kr[:, half:], w_kr[:, :half]], axis=1)
    reps = LANES // MLA_ROPE
    w1 = jnp.concatenate([wb[:, :o_kr], wb[:, o_z:]] + [w_kr] * reps + [w_krs] * reps, axis=1)
    wq3 = w_uq.astype(BF16).reshape(MLA_Q_RANK, H, dq)
    wqt = jnp.concatenate([wq3[:, :, :MLA_NOPE].reshape(MLA_Q_RANK, nope),
                           wq3[:, :, MLA_NOPE:].reshape(MLA_Q_RANK, rope)], axis=1).T
    wkv3 = w_ukv.astype(BF16).reshape(MLA_KV_RANK, H, MLA_NOPE + MLA_V)
    wkn = wkv3[:, :, :MLA_NOPE].reshape(MLA_KV_RANK, nope)
    wvt = wkv3[:, :, MLA_NOPE:].reshape(MLA_KV_RANK, vw).T
    inv = ROPE_BASE ** (-jnp.arange(0, MLA_ROPE, 2, dtype=F32) / MLA_ROPE)
    ang = jnp.arange(L, dtype=F32)[:, None] * inv[None, :]
    cos, sin = jnp.cos(ang), jnp.sin(ang)
    cos32 = jnp.concatenate([cos, cos], axis=1)
    sin32 = jnp.concatenate([-sin, sin], axis=1)
    cos_k, sin_k = jnp.tile(cos32, (1, LANES // MLA_ROPE)), jnp.tile(sin32, (1, LANES // MLA_ROPE))
    cos_q, sin_q = jnp.tile(cos32, (1, H)).T, jnp.tile(sin32, (1, H)).T

    tm = 1024
    tk = MLA_TK
    tok = lambda w_: pl.BlockSpec((1, tm, w_), lambda b, i: (b, i, 0))
    tokt = lambda w_: pl.BlockSpec((1, w_, tm), lambda b, i: (b, 0, i))
    scale = dq ** -0.5 * math.log2(math.e)
    qn, qr, kn, kr, v, z = pl.pallas_call(
        functools.partial(_mla_pre_kernel, scale=scale),
        out_shape=[jax.ShapeDtypeStruct((bsz, nope, L), BF16),
                   jax.ShapeDtypeStruct((bsz, rope, L), BF16),
                   jax.ShapeDtypeStruct((bsz, L, nope), BF16),
                   jax.ShapeDtypeStruct((bsz, L, LANES), BF16),
                   jax.ShapeDtypeStruct((bsz, L // tk, vw, tk), BF16),
                   jax.ShapeDtypeStruct((bsz, L, vw), F32)],
        grid=(bsz, L // tm),
        in_specs=[tok(d), _full((1, d)), _full(w1.shape), _full((1, MLA_Q_RANK)), _full((1, MLA_KV_RANK)),
                  _full(wqt.shape), _full(wkn.shape), _full(wvt.shape),
                  pl.BlockSpec((rope, tm), lambda b, i: (0, i)), pl.BlockSpec((rope, tm), lambda b, i: (0, i)),
                  pl.BlockSpec((tm, LANES), lambda b, i: (i, 0)), pl.BlockSpec((tm, LANES), lambda b, i: (i, 0))],
        out_specs=[tokt(nope), tokt(rope), tok(nope), tok(LANES),
                   pl.BlockSpec((1, tm // tk, vw, tk), lambda b, i: (b, i, 0, 0)), tok(vw)],
        compiler_params=_cparams(("parallel", "parallel")),
        name="mla_pre",
    )(x, pre_g.reshape(1, d), w1, q_norm.reshape(1, -1), kv_norm.reshape(1, -1), wqt, wkn, wvt,
      cos_q, sin_q, cos_k, sin_k)

    tq = MLA_TQ
    npairs = H // 2
    qspec = lambda w_: pl.BlockSpec((1, w_, tq), lambda b, i: (b, 0, i))
    kspec = lambda w_: pl.BlockSpec((1, L, w_), lambda b, i: (b, 0, 0))
    rowspec = lambda w_: pl.BlockSpec((1, tq, w_), lambda b, i: (b, i, 0))
    return pl.pallas_call(
        _mla_attn_kernel,
        out_shape=jax.ShapeDtypeStruct(x.shape, x.dtype),
        grid=(bsz, L // tq),
        in_specs=[qspec(nope), qspec(rope), kspec(nope), kspec(LANES),
                  pl.BlockSpec((1, L // tk, vw, tk), lambda b, i: (b, 0, 0, 0)),
                  rowspec(vw), rowspec(d), _full(w_out.shape), _full((1, d))],
        out_specs=rowspec(d),
        scratch_shapes=[pltpu.VMEM((npairs, 2 * LANES, 2 * tq), BF16),
                        pltpu.VMEM((npairs, LANES + 16, 2 * tq), F32),
                        pltpu.VMEM((npairs, 1, 2 * tq), F32),
                        pltpu.VMEM((tq, vw), F32)],
        compiler_params=_cparams(("parallel", "arbitrary")),
        name="mla_attn",
    )(qn, qr, kn, kr, v, z, x, w_out, post_g.reshape(1, d))


def _sgu_kernel(x_ref, g_ref, w_ref, lng_ref, lnb_ref, ws_ref, bs_ref, wo_ref, pg_ref, out_ref, s_ref):
    width = wo_ref.shape[0]
    tm = x_ref.shape[1]
    lane = lax.broadcasted_iota(jnp.int32, (1, LANES), 1)
    lo = lane < HALF
    x = x_ref[0]
    hb = _rms(x, g_ref[...]).astype(BF16)
    v = jax.nn.gelu(_dot(hb, w_ref[:, width:2 * width].astype(BF16)))
    mu = jnp.mean(v, axis=-1, keepdims=True)
    vc = v - mu
    var = jnp.mean(vc * vc, axis=-1, keepdims=True)
    vb = (vc * lax.rsqrt(var + EPS) * lng_ref[...] + lnb_ref[...]).astype(BF16)
    group = SGU_STACK
    for c0 in range(0, tm // SGU_CHUNK, group):
        for jj in range(width // LANES):
            blk = jnp.concatenate([vb[c * SGU_CHUNK:(c + 1) * SGU_CHUNK, jj * LANES:(jj + 1) * LANES]
                                   for c in range(c0, c0 + group)], axis=1)
            r = _dot(ws_ref[jj], blk)
            for k in range(group):
                c = c0 + k
                s_ref[c * SGU_CHUNK:(c + 1) * SGU_CHUNK, jj * LANES:(jj + 1) * LANES] = (
                    jnp.where(lo, r[:SGU_CHUNK, k * LANES:(k + 1) * LANES],
                              r[SGU_CHUNK:, k * LANES:(k + 1) * LANES]) + bs_ref[jj])
    u = jax.nn.gelu(_dot(hb, w_ref[:, :width].astype(BF16)))
    z = _dot(hb, w_ref[:, 2 * width:].astype(BF16))
    o = u * s_ref[...] * jax.nn.silu(z)
    r = _dot(o.astype(BF16), wo_ref[...].astype(BF16))
    out_ref[0] = x + _rms(r, pg_ref[...])


def _sgu_layer(x, pre_g, post_g, w_in, ln_g, ln_b, w_s, b_s, w_out):
    bsz, L, d = x.shape
    width = w_out.shape[0]
    T = SGU_CHUNK
    gd = width // SGU_GROUPS
    tril = jnp.tril(jnp.ones((T, T), dtype=bool))
    ws = jnp.where(tril[None], w_s, 0.0).reshape(SGU_GROUPS // 2, 2 * T, T).astype(BF16)
    bs = jnp.repeat(b_s.astype(F32).T, gd, axis=1)
    bs = bs.reshape(T, width // LANES, LANES).transpose(1, 0, 2)
    tm = 1024
    return pl.pallas_call(
        _sgu_kernel,
        out_shape=jax.ShapeDtypeStruct(x.shape, x.dtype),
        grid=(bsz, L // tm),
        in_specs=[pl.BlockSpec((1, tm, d), lambda b, i: (b, i, 0)),
                  _full((1, d)), _full(w_in.shape), _full((1, width)), _full((1, width)),
                  _full(ws.shape), _full(bs.shape), _full(w_out.shape), _full((1, d))],
        out_specs=pl.BlockSpec((1, tm, d), lambda b, i: (b, i, 0)),
        scratch_shapes=[pltpu.VMEM((tm, width), F32)],
        compiler_params=_cparams(("parallel", "parallel")),
        name="sgu",
    )(x, pre_g.reshape(1, d), w_in, ln_g.reshape(1, width), ln_b.reshape(1, width),
      ws, bs, w_out, post_g.reshape(1, d))


def kernel(x, pre_norm, post_norm, rel_bias, a_w_in, a_lam_re, a_lam_im, a_log_dt, a_b_re, a_b_im, a_c_re, a_c_im, a_d, a_w_glu, a_b_glu, a_w_out, b_w_in, b_sinks, b_w_out, c_w_in, c_q_norm, c_kv_norm, c_w_uq, c_w_ukv, c_w_out, d_w_in, d_ln_g, d_ln_b, d_w_s, d_b_s, d_w_out):
    depth = pre_norm.shape[0]
    for i in range(depth):
        kind, j = i % 4, i // 4
        if kind == 0:
            x = _s5_layer(x, pre_norm[i], post_norm[i], a_w_in[j], a_lam_re[j], a_lam_im[j], a_log_dt[j],
                          a_b_re[j], a_b_im[j], a_c_re[j], a_c_im[j], a_d[j], a_w_glu[j], a_b_glu[j],
                          a_w_out[j])
        elif kind == 1:
            x = _swa_layer(x, pre_norm[i], post_norm[i], b_w_in[j], b_sinks[j], b_w_out[j], rel_bias)
        elif kind == 2:
            x = _mla_layer(x, pre_norm[i], post_norm[i], c_w_in[j], c_q_norm[j], c_kv_norm[j], c_w_uq[j],
                           c_w_ukv[j], c_w_out[j])
        else:
            x = _sgu_layer(x, pre_norm[i], post_norm[i], d_w_in[j], d_ln_g[j], d_ln_b[j], d_w_s[j],
                           d_b_s[j], d_w_out[j])
    return x
```

```python
import functools
import math

import jax
import jax.numpy as jnp
import numpy as np
from jax import lax
from jax.experimental import pallas as pl
from jax.experimental.pallas import tpu as pltpu

F32 = jnp.float32
BF16 = jnp.bfloat16

D_MODEL = 1024
EPS = 1e-6
NEG_INF = -1e30
LANES = 128
HALF = LANES // 2

SSM_GROUP = 16
SSM_STATE = 64
S5_CH_BLOCK = LANES
S5_GROUPS_PER_BLOCK = S5_CH_BLOCK // SSM_GROUP
S5_STATE_BLOCK = S5_GROUPS_PER_BLOCK * SSM_STATE
S5_T = 64

HEAD_DIM = 64
SWA_HEADS = 16
SWA_KV_HEADS = 2
SWA_GROUP = SWA_HEADS // SWA_KV_HEADS
WINDOW = 128
SWA_WINDOWS_PER_STEP = 4
SWA_LOOKAHEAD = 2
SWA_UNIT_HEADS = 8
SWA_PV_DELAY = 1
REL_BUCKETS = 32
REL_MAX_DIST = 128

MLA_HEADS = 16
MLA_NOPE = 64
MLA_ROPE = 32
MLA_V = 64
MLA_KV_RANK = 256
MLA_Q_RANK = 768
ROPE_BASE = 10000.0
MLA_TQ = 256
MLA_TK = 256
MLA_LOOKAHEAD = 2
MLA_PV_DELAY = 2

SGU_CHUNK = 128
SGU_GROUPS = 16
SGU_STACK = 4

VMEM_LIMIT = 56 * 1024 * 1024


def _cparams(sem):
    return pltpu.CompilerParams(dimension_semantics=sem, vmem_limit_bytes=VMEM_LIMIT)


def _rms(x, g):
    return x * lax.rsqrt(jnp.mean(x * x, axis=-1, keepdims=True) + EPS) * g


def _dot(a, b):
    return jnp.dot(a, b, preferred_element_type=F32)


def _dot_nt(a, b):
    return lax.dot_general(a, b, (((1,), (1,)), ((), ())), preferred_element_type=F32)


def _full(shape):
    n = len(shape)
    return pl.BlockSpec(shape, lambda *_: (0,) * n, pipeline_mode=pl.Buffered(1))


def _s5_kernel(x_ref, g_ref, w_ref, perm_ref, permt_ref, bb_ref, cc_ref, ar_ref, ai_ref, d_ref,
               wg_ref, bg_ref, wo_ref, pg_ref, out_ref, u_ref, z_ref, y_ref, s_ref, carry_ref, *, tt):
    bsz = x_ref.shape[0]
    width = wg_ref.shape[0]
    rows = bsz * tt
    nblk = bb_ref.shape[0]
    sb = S5_STATE_BLOCK

    @pl.when(pl.program_id(0) == 0)
    def _():
        carry_ref[...] = jnp.zeros_like(carry_ref)

    x = x_ref[...].reshape(rows, x_ref.shape[2])
    hb = _rms(x, g_ref[...]).astype(BF16)
    hb = _dot(perm_ref[...], hb).astype(BF16)
    u_ref[...] = _dot(hb, w_ref[:, :width].astype(BF16))
    z_ref[...] = _dot(hb, w_ref[:, width:].astype(BF16))

    nbuf = s_ref.shape[0]

    def project_in(i):
        s_ref[i % nbuf] = _dot(u_ref[:, i * LANES:(i + 1) * LANES].astype(BF16), bb_ref[i])

    def project_out(i):
        ub = u_ref[:, i * LANES:(i + 1) * LANES]
        y = _dot(s_ref[i % nbuf].astype(BF16), cc_ref[i]) + d_ref[:, i * LANES:(i + 1) * LANES] * ub
        y_ref[:, i * LANES:(i + 1) * LANES] = jax.nn.gelu(y)

    project_in(0)
    for i in range(nblk):
        if i + 1 < nblk:
            project_in(i + 1)
        buf = s_ref.at[i % nbuf]
        ar = ar_ref[i]
        ai = ai_ref[i]
        sr = carry_ref[i, :, 0:sb]
        si = carry_ref[i, :, sb:2 * sb]
        for t in range(tt):
            r0 = t * bsz
            nr = ar * sr - ai * si + buf[r0:r0 + bsz, 0:sb]
            ni = ar * si + ai * sr + buf[r0:r0 + bsz, sb:2 * sb]
            buf[r0:r0 + bsz, 0:sb] = nr
            buf[r0:r0 + bsz, sb:2 * sb] = ni
            sr, si = nr, ni
        carry_ref[i, :, 0:sb] = sr
        carry_ref[i, :, sb:2 * sb] = si
        project_out(i)

    y = y_ref[...]
    gate = jax.nn.sigmoid(_dot(y.astype(BF16), wg_ref[...].astype(BF16)) + bg_ref[...])
    o = y * gate * jax.nn.silu(z_ref[...])
    ob = _dot(permt_ref[...], o.astype(BF16)).astype(BF16)
    r = _dot(ob, wo_ref[...].astype(BF16))
    out_ref[...] = (x + _rms(r, pg_ref[...])).reshape(out_ref.shape)


def _s5_discretize(lam_re, lam_im, log_dt, b_re, b_im):
    dt = jnp.exp(log_dt)[:, None]
    mag = jnp.exp(lam_re * dt)
    ab_re = mag * jnp.cos(lam_im * dt)
    ab_im = mag * jnp.sin(lam_im * dt)
    den = lam_re * lam_re + lam_im * lam_im
    nr = ab_re - 1.0
    f_re = (nr * lam_re + ab_im * lam_im) / den
    f_im = (ab_im * lam_re - nr * lam_im) / den
    bb_re = f_re[..., None] * b_re - f_im[..., None] * b_im
    bb_im = f_re[..., None] * b_im + f_im[..., None] * b_re
    return ab_re, ab_im, bb_re, bb_im


def _s5_layer(x, pre_g, post_g, w_in, lam_re, lam_im, log_dt, b_re, b_im, c_re, c_im, d_skip,
              w_glu, b_glu, w_out):
    bsz, L, d = x.shape
    width = w_in.shape[1] // 2
    nblk = width // S5_CH_BLOCK
    gpb = S5_GROUPS_PER_BLOCK
    tt = S5_T
    rows = bsz * tt

    src = (np.arange(rows) % bsz) * tt + np.arange(rows) // bsz
    perm_np = np.zeros((rows, rows), np.float32)
    perm_np[np.arange(rows), src] = 1.0
    perm = jnp.asarray(perm_np, BF16)
    perm_t = jnp.asarray(perm_np.T, BF16)

    ab_re, ab_im, bb_re, bb_im = _s5_discretize(lam_re, lam_im, log_dt, b_re, b_im)
    eye = jnp.eye(gpb, dtype=F32)

    def pack_b(bb):
        t = bb.reshape(nblk, gpb, SSM_STATE, SSM_GROUP)
        return jnp.einsum('igph,gk->ikhgp', t, eye).reshape(nblk, S5_CH_BLOCK, S5_STATE_BLOCK)

    def pack_c(cc):
        t = cc.reshape(nblk, gpb, SSM_GROUP, SSM_STATE)
        return jnp.einsum('ighp,gk->igpkh', t, eye).reshape(nblk, S5_STATE_BLOCK, S5_CH_BLOCK)

    bb = jnp.concatenate([pack_b(bb_re), pack_b(bb_im)], axis=2).astype(BF16)
    cc = jnp.concatenate([pack_c(c_re), -pack_c(c_im)], axis=1).astype(BF16)
    ar = jnp.broadcast_to(ab_re.reshape(nblk, 1, S5_STATE_BLOCK), (nblk, bsz, S5_STATE_BLOCK))
    ai = jnp.broadcast_to(ab_im.reshape(nblk, 1, S5_STATE_BLOCK), (nblk, bsz, S5_STATE_BLOCK))

    xspec = pl.BlockSpec((bsz, tt, d), lambda i: (0, i, 0))
    return pl.pallas_call(
        functools.partial(_s5_kernel, tt=tt),
        out_shape=jax.ShapeDtypeStruct(x.shape, x.dtype),
        grid=(L // tt,),
        in_specs=[xspec, _full((1, d)), _full(w_in.shape), _full(perm.shape), _full(perm_t.shape),
                  _full(bb.shape), _full(cc.shape), _full(ar.shape), _full(ai.shape), _full((1, width)),
                  _full(w_glu.shape), _full((1, width)), _full(w_out.shape), _full((1, d))],
        out_specs=xspec,
        scratch_shapes=[pltpu.VMEM((rows, width), F32),
                        pltpu.VMEM((rows, width), F32),
                        pltpu.VMEM((rows, width), F32),
                        pltpu.VMEM((2, rows, 2 * S5_STATE_BLOCK), F32),
                        pltpu.VMEM((nblk, bsz, 2 * S5_STATE_BLOCK), F32)],
        compiler_params=_cparams(("arbitrary",)),
        name="s5_layer",
    )(x, pre_g.reshape(1, d), w_in, perm, perm_t, bb, cc, ar, ai, d_skip.reshape(1, width),
      w_glu, b_glu.reshape(1, width), w_out, post_g.reshape(1, d))


def _swa_bias(rel_bias):
    W = WINDOW
    n = 4 * W
    dist = jnp.arange(n) - W
    valid = jnp.logical_and(dist >= 0, dist < W)
    dpos = jnp.maximum(dist, 0)
    max_exact = REL_BUCKETS // 2
    dist_f = jnp.maximum(dpos, 1).astype(F32)
    large = max_exact + (jnp.log(dist_f / max_exact) / math.log(REL_MAX_DIST / max_exact)
                         * (REL_BUCKETS - max_exact)).astype(jnp.int32)
    large = jnp.minimum(large, REL_BUCKETS - 1)
    bucket = jnp.where(dpos < max_exact, dpos, large)
    vec = jnp.where(valid[:, None], rel_bias[bucket].astype(F32), NEG_INF).T
    skew = jnp.tile(vec, (1, 2 * W))[:, :2 * W * (n - 1)].reshape(vec.shape[0], 2 * W, n - 1)
    return skew[:, :, 2 * W:3 * W]


def _swa_pre_kernel(x_ref, g_ref, wqt_ref, wk_ref, wvt_ref, wz_ref, qt_ref, k_ref, vt_ref, z_ref, *, scale):
    hb = _rms(x_ref[0], g_ref[...]).astype(BF16)
    qt_ref[0] = (_dot_nt(wqt_ref[...], hb) * scale).astype(BF16)
    k_ref[0] = _dot(hb, wk_ref[...]).astype(BF16)
    vt_ref[0] = _dot_nt(wvt_ref[...], hb).astype(BF16)
    z_ref[0] = _dot(hb, wz_ref[...])


def _swa_kernel(qt_ref, kp_ref, kc_ref, vtp_ref, vtc_ref, bias_ref, sink_ref, z_ref, x_ref, wo_ref, g_ref,
                out_ref, ot_ref):
    W = WINDOW
    nwin = qt_ref.shape[2] // W
    step = pl.program_id(1)
    kall = jnp.concatenate([kp_ref[0], kc_ref[0]], axis=0)
    vtall = jnp.concatenate([vtp_ref[0], vtc_ref[0]], axis=1)
    nsub = SWA_UNIT_HEADS
    zq = jnp.zeros((HEAD_DIM, nsub * W), BF16)
    ones = jnp.ones((16, 2 * W), BF16)
    units = [(w, h, c) for w in range(nwin) for h in range(SWA_KV_HEADS) for c in range(SWA_GROUP // nsub)]

    def scores(w, h, c):
        hd0 = h * SWA_GROUP + c * nsub
        qh = jnp.concatenate([qt_ref[0, (hd0 + g) * HEAD_DIM:(hd0 + g + 1) * HEAD_DIM, w * W:(w + 1) * W]
                              for g in range(nsub)], axis=1)
        qz = jnp.concatenate([qh, zq] if h == 0 else [zq, qh], axis=0)
        return _dot(kall[w * W:(w + 2) * W], qz)

    pending = [scores(*u) for u in units[:SWA_LOOKAHEAD]]
    late = []

    def flush():
        (w, h, c), p, tail = late.pop(0)
        vones = jnp.concatenate([vtall[h * HEAD_DIM:(h + 1) * HEAD_DIM, w * W:(w + 2) * W], ones], axis=0)
        o = _dot(vones, p)
        oh = o[:HEAD_DIM] * (1.0 / (o[HEAD_DIM:HEAD_DIM + 1] + tail))
        for g in range(nsub):
            hd = h * SWA_GROUP + c * nsub + g
            ot_ref[hd * HEAD_DIM:(hd + 1) * HEAD_DIM, w * W:(w + 1) * W] = oh[:, g * W:(g + 1) * W]

    for idx, (w, h, c) in enumerate(units):
        raw = pending.pop(0)
        if idx + SWA_LOOKAHEAD < len(units):
            pending.append(scores(*units[idx + SWA_LOOKAHEAD]))
        cols = slice(c * nsub * W, (c + 1) * nsub * W)
        variant = (step == 0).astype(jnp.int32) if w == 0 else 0
        s = raw + bias_ref[variant, h, :, cols]
        sink = sink_ref[h, :, cols]
        m = jnp.maximum(jnp.max(s, axis=0, keepdims=True), sink)
        if len(late) == SWA_PV_DELAY:
            flush()
        late.append(((w, h, c), jnp.exp2(s - m).astype(BF16), jnp.exp2(sink - m)))
    while late:
        flush()
    gated = ot_ref[...].T * jax.nn.silu(z_ref[0])
    r = _dot(gated.astype(BF16), wo_ref[...].astype(BF16))
    out_ref[0] = x_ref[0] + _rms(r, g_ref[...])


def _swa_layer(x, pre_g, post_g, w_in, sinks, w_out, rel_bias):
    bsz, L, d = x.shape
    width = SWA_HEADS * HEAD_DIM
    kvw = SWA_KV_HEADS * HEAD_DIM
    W = WINDOW
    nb = L // W
    log2e = math.log2(math.e)
    tm = 1024
    tok = lambda w_: pl.BlockSpec((1, tm, w_), lambda b, i: (b, i, 0))
    tokt = lambda w_: pl.BlockSpec((1, w_, tm), lambda b, i: (b, 0, i))
    wb = w_in.astype(BF16)
    wqt = wb[:, :width].T
    wk = wb[:, width:width + kvw]
    wvt = wb[:, width + kvw:width + 2 * kvw].T
    wz = wb[:, width + 2 * kvw:]
    qt, k, vt, z = pl.pallas_call(
        functools.partial(_swa_pre_kernel, scale=HEAD_DIM ** -0.5 * log2e),
        out_shape=[jax.ShapeDtypeStruct((bsz, width, L), BF16),
                   jax.ShapeDtypeStruct((bsz, L, kvw), BF16),
                   jax.ShapeDtypeStruct((bsz, kvw, L), BF16),
                   jax.ShapeDtypeStruct((bsz, L, width), F32)],
        grid=(bsz, L // tm),
        in_specs=[tok(d), _full((1, d)), _full(wqt.shape), _full(wk.shape), _full(wvt.shape), _full(wz.shape)],
        out_specs=[tokt(width), tok(kvw), tokt(kvw), tok(width)],
        compiler_params=_cparams(("parallel", "parallel")),
        name="swa_pre",
    )(x, pre_g.reshape(1, d), wqt, wk, wvt, wz)

    bias = _swa_bias(rel_bias.astype(F32) * log2e)
    has_prev = (jnp.arange(2 * W) >= W)[None, :, None]
    variants = [bias,
                jnp.where(has_prev, bias, NEG_INF)]
    bias_t = jnp.stack([v.reshape(SWA_KV_HEADS, SWA_GROUP, 2 * W, W).transpose(0, 2, 1, 3)
                        .reshape(SWA_KV_HEADS, 2 * W, SWA_GROUP * W) for v in variants])
    sink = jnp.repeat(sinks.astype(F32) * log2e, W).reshape(SWA_KV_HEADS, 1, SWA_GROUP * W)

    nwin = SWA_WINDOWS_PER_STEP
    tq = nwin * W
    prev = lambda n: jnp.maximum(n * nwin - 1, 0)
    return pl.pallas_call(
        _swa_kernel,
        out_shape=jax.ShapeDtypeStruct(x.shape, x.dtype),
        grid=(bsz, L // tq),
        in_specs=[pl.BlockSpec((1, width, tq), lambda b, n: (b, 0, n)),
                  pl.BlockSpec((1, W, kvw), lambda b, n: (b, prev(n), 0)),
                  pl.BlockSpec((1, tq, kvw), lambda b, n: (b, n, 0)),
                  pl.BlockSpec((1, kvw, W), lambda b, n: (b, 0, prev(n))),
                  pl.BlockSpec((1, kvw, tq), lambda b, n: (b, 0, n)),
                  _full(bias_t.shape), _full(sink.shape),
                  pl.BlockSpec((1, tq, width), lambda b, n: (b, n, 0)),
                  pl.BlockSpec((1, tq, d), lambda b, n: (b, n, 0)),
                  _full(w_out.shape), _full((1, d))],
        out_specs=pl.BlockSpec((1, tq, d), lambda b, n: (b, n, 0)),
        scratch_shapes=[pltpu.VMEM((width, tq), F32)],
        compiler_params=_cparams(("parallel", "parallel")),
        name="swa_attn",
    )(qt, k, k, vt, vt, bias_t, sink, z, x, w_out, post_g.reshape(1, d))


def _mla_pre_kernel(x_ref, g_ref, w_ref, qn_ref, kvn_ref, wq_ref, wkv_ref, wvt_ref, cq_ref, sq_ref, ck_ref, sk_ref,
                    oqn_ref, oqr_ref, okn_ref, okr_ref, ov_ref, oz_ref, *, scale):
    nope = MLA_HEADS * MLA_NOPE
    rope = MLA_HEADS * MLA_ROPE
    vw = MLA_HEADS * MLA_V
    hb = _rms(x_ref[0], g_ref[...]).astype(BF16)
    o1 = MLA_Q_RANK
    o2 = o1 + MLA_KV_RANK
    o3 = o2 + vw
    cq = _dot(hb, w_ref[:, :o1])
    ckv = _dot(hb, w_ref[:, o1:o2])
    oz_ref[0] = _dot(hb, w_ref[:, o2:o3])
    kr = _dot(hb, w_ref[:, o3:o3 + LANES])
    krs = _dot(hb, w_ref[:, o3 + LANES:o3 + 2 * LANES])
    okr_ref[0] = (kr * ck_ref[...] + krs * sk_ref[...]).astype(BF16)
    cqb = _rms(cq, qn_ref[...]).astype(BF16)
    oqn_ref[0] = (_dot_nt(wq_ref[:nope], cqb) * scale).astype(BF16)
    qr = _dot_nt(wq_ref[nope:nope + rope], cqb)
    hr = MLA_ROPE // 2
    qrs = jnp.concatenate([qr[h * MLA_ROPE + o:h * MLA_ROPE + o + hr]
                           for h in range(MLA_HEADS) for o in (hr, 0)], axis=0)
    oqr_ref[0] = ((qr * cq_ref[...] + qrs * sq_ref[...]) * scale).astype(BF16)
    ckb = _rms(ckv, kvn_ref[...]).astype(BF16)
    okn_ref[0] = _dot(ckb, wkv_ref[:, :nope]).astype(BF16)
    vt = _dot_nt(wvt_ref[...], ckb).astype(BF16)
    tk = ov_ref.shape[3]
    for c in range(ov_ref.shape[1]):
        ov_ref[0, c] = vt[:, c * tk:(c + 1) * tk]


def _mla_attn_kernel(qn_ref, qr_ref, kn_ref, kr_ref, v_ref, z_ref, x_ref, wo_ref, g_ref, out_ref,
                     qs_ref, acc_ref, m_ref, o_ref):
    tq = qn_ref.shape[2]
    tk = v_ref.shape[3]
    npairs = MLA_HEADS // 2
    i = pl.program_id(1)
    krow = lax.broadcasted_iota(jnp.int32, (tk, 2 * tq), 0)
    qcol = lax.broadcasted_iota(jnp.int32, (tk, 2 * tq), 1)
    causal = krow <= jnp.where(qcol >= tq, qcol - tq, qcol)

    zn = jnp.zeros((MLA_NOPE, tq), BF16)
    zr = jnp.zeros((LANES - MLA_ROPE, tq), BF16)
    for p in range(npairs):
        qn = qn_ref[0, p * LANES:(p + 1) * LANES, :]
        r0 = 2 * p * MLA_ROPE
        c0 = jnp.concatenate([qn[:MLA_NOPE], zn, qr_ref[0, r0:r0 + MLA_ROPE, :], zr], axis=0)
        c1 = jnp.concatenate([zn, qn[MLA_NOPE:], qr_ref[0, r0 + MLA_ROPE:r0 + 2 * MLA_ROPE, :], zr], axis=0)
        qs_ref[p] = jnp.concatenate([c0, c1], axis=1)

    m_ref[...] = jnp.full(m_ref.shape, NEG_INF, F32)
    acc_ref[...] = jnp.zeros(acc_ref.shape, F32)
    ones = jnp.ones((acc_ref.shape[1] - LANES, tk), BF16)

    def kv_steps(blocks):
        units = [(j, masked, p) for j, masked in blocks for p in range(npairs)]

        def scores(j, p):
            ks = pl.multiple_of(j * tk, tk)
            kc = jnp.concatenate([kn_ref[0, pl.ds(ks, tk), p * LANES:(p + 1) * LANES],
                                  kr_ref[0, pl.ds(ks, tk), :]], axis=1)
            return [_dot(kc, qs_ref[p, :, c * 2 * LANES:(c + 1) * 2 * LANES]) for c in range(tq // LANES)]

        pending = [scores(j, p) for j, _, p in units[:MLA_LOOKAHEAD]]
        late = []

        def flush():
            jj, pp, alpha, pr = late.pop(0)
            vones = jnp.concatenate([v_ref[0, jj, pp * LANES:(pp + 1) * LANES, :], ones], axis=0)
            acc_ref[pp] = alpha * acc_ref[pp] + _dot(vones, pr)

        for idx, (j, masked, p) in enumerate(units):
            s = pending.pop(0)
            if idx + MLA_LOOKAHEAD < len(units):
                nxt = units[idx + MLA_LOOKAHEAD]
                pending.append(scores(nxt[0], nxt[2]))
            probs, alphas = [], []
            for c in range(2 * tq // LANES):
                sc = s[c // 2][:, (c % 2) * LANES:(c % 2 + 1) * LANES]
                if masked:
                    sc = jnp.where(causal[:, c * LANES:(c + 1) * LANES], sc, NEG_INF)
                m_prev = m_ref[p, :, c * LANES:(c + 1) * LANES]
                m_new = jnp.maximum(m_prev, jnp.max(sc, axis=0, keepdims=True))
                alphas.append(jnp.exp2(m_prev - m_new))
                probs.append(jnp.exp2(sc - m_new).astype(BF16))
                m_ref[p, :, c * LANES:(c + 1) * LANES] = m_new
            if len(late) == MLA_PV_DELAY:
                flush()
            late.append((j, p, jnp.concatenate(alphas, axis=1), jnp.concatenate(probs, axis=1)))
        while late:
            flush()

    def body(jj, c):
        kv_steps([(2 * jj, False), (2 * jj + 1, False)])
        return c

    lax.fori_loop(0, i // 2, body, 0)

    @pl.when(i % 2 == 1)
    def _():
        kv_steps([(i - 1, False), (i, True)])

    @pl.when(i % 2 == 0)
    def _():
        kv_steps([(i, True)])
    for p in range(npairs):
        a = acc_ref[p]
        a = a[:LANES] * (1.0 / a[LANES:LANES + 1])
        ot = jnp.concatenate([a[:MLA_V, :tq], a[MLA_V:, tq:]], axis=0)
        o_ref[:, p * LANES:(p + 1) * LANES] = ot.T
    gated = o_ref[...] * jax.nn.silu(z_ref[0])
    r = _dot(gated.astype(BF16), wo_ref[...].astype(BF16))
    out_ref[0] = x_ref[0] + _rms(r, g_ref[...])


def _mla_layer(x, pre_g, post_g, w_in, q_norm, kv_norm, w_uq, w_ukv, w_out):
    bsz, L, d = x.shape
    H = MLA_HEADS
    dq = MLA_NOPE + MLA_ROPE
    nope = H * MLA_NOPE
    rope = H * MLA_ROPE
    vw = H * MLA_V
    half = MLA_ROPE // 2
    o_kr = MLA_Q_RANK + MLA_KV_RANK
    o_z = o_kr + MLA_ROPE
    wb = w_in.astype(BF16)
    w_kr = wb[:, o_kr:o_z]
    w_krs = jnp.concatenate([w_kr[:, half:], w_kr[:, :half]], axis=1)
    reps = LANES // MLA_ROPE
    w1 = jnp.concatenate([wb[:, :o_kr], wb[:, o_z:]] + [w_kr] * reps + [w_krs] * reps, axis=1)
    wq3 = w_uq.astype(BF16).reshape(MLA_Q_RANK, H, dq)
    wqt = jnp.concatenate([wq3[:, :, :MLA_NOPE].reshape(MLA_Q_RANK, nope),
                           wq3[:, :, MLA_NOPE:].reshape(MLA_Q_RANK, rope)], axis=1).T
    wkv3 = w_ukv.astype(BF16).reshape(MLA_KV_RANK, H, MLA_NOPE + MLA_V)
    wkn = wkv3[:, :, :MLA_NOPE].reshape(MLA_KV_RANK, nope)
    wvt = wkv3[:, :, MLA_NOPE:].reshape(MLA_KV_RANK, vw).T
    inv = ROPE_BASE ** (-jnp.arange(0, MLA_ROPE, 2, dtype=F32) / MLA_ROPE)
    ang = jnp.arange(L, dtype=F32)[:, None] * inv[None, :]
    cos, sin = jnp.cos(ang), jnp.sin(ang)
    cos32 = jnp.concatenate([cos, cos], axis=1)
    sin32 = jnp.concatenate([-sin, sin], axis=1)
    cos_k, sin_k = jnp.tile(cos32, (1, LANES // MLA_ROPE)), jnp.tile(sin32, (1, LANES // MLA_ROPE))
    cos_q, sin_q = jnp.tile(cos32, (1, H)).T, jnp.tile(sin32, (1, H)).T

    tm = 1024
    tk = MLA_TK
    tok = lambda w_: pl.BlockSpec((1, tm, w_), lambda b, i: (b, i, 0))
    tokt = lambda w_: pl.BlockSpec((1, w_, tm), lambda b, i: (b, 0, i))
    scale = dq ** -0.5 * math.log2(math.e)
    qn, qr, kn, kr, v, z = pl.pallas_call(
        functools.partial(_mla_pre_kernel, scale=scale),
        out_shape=[jax.ShapeDtypeStruct((bsz, nope, L), BF16),
                   jax.ShapeDtypeStruct((bsz, rope, L), BF16),
                   jax.ShapeDtypeStruct((bsz, L, nope), BF16),
                   jax.ShapeDtypeStruct((bsz, L, LANES), BF16),
                   jax.ShapeDtypeStruct((bsz, L // tk, vw, tk), BF16),
                   jax.ShapeDtypeStruct((bsz, L, vw), F32)],
        grid=(bsz, L // tm),
        in_specs=[tok(d), _full((1, d)), _full(w1.shape), _full((1, MLA_Q_RANK)), _full((1, MLA_KV_RANK)),
                  _full(wqt.shape), _full(wkn.shape), _full(wvt.shape),
                  pl.BlockSpec((rope, tm), lambda b, i: (0, i)), pl.BlockSpec((rope, tm), lambda b, i: (0, i)),
                  pl.BlockSpec((tm, LANES), lambda b, i: (i, 0)), pl.BlockSpec((tm, LANES), lambda b, i: (i, 0))],
        out_specs=[tokt(nope), tokt(rope), tok(nope), tok(LANES),
                   pl.BlockSpec((1, tm // tk, vw, tk), lambda b, i: (b, i, 0, 0)), tok(vw)],
        compiler_params=_cparams(("parallel", "parallel")),
        name="mla_pre",
    )(x, pre_g.reshape(1, d), w1, q_norm.reshape(1, -1), kv_norm.reshape(1, -1), wqt, wkn, wvt,
      cos_q, sin_q, cos_k, sin_k)

    tq = MLA_TQ
    npairs = H // 2
    qspec = lambda w_: pl.BlockSpec((1, w_, tq), lambda b, i: (b, 0, i))
    kspec = lambda w_: pl.BlockSpec((1, L, w_), lambda b, i: (b, 0, 0))
    rowspec = lambda w_: pl.BlockSpec((1, tq, w_), lambda b, i: (b, i, 0))
    return pl.pallas_call(
        _mla_attn_kernel,
        out_shape=jax.ShapeDtypeStruct(x.shape, x.dtype),
        grid=(bsz, L // tq),
        in_specs=[qspec(nope), qspec(rope), kspec(nope), kspec(LANES),
                  pl.BlockSpec((1, L // tk, vw, tk), lambda b, i: (b, 0, 0, 0)),
                  rowspec(vw), rowspec(d), _full(w_out.shape), _full((1, d))],
        out_specs=rowspec(d),
        scratch_shapes=[pltpu.VMEM((npairs, 2 * LANES, 2 * tq), BF16),
                        pltpu.VMEM((npairs, LANES + 16, 2 * tq), F32),
                        pltpu.VMEM((npairs, 1, 2 * tq), F32),
                        pltpu.VMEM((tq, vw), F32)],
        compiler_params=_cparams(("parallel", "arbitrary")),
        name="mla_attn",
    )(qn, qr, kn, kr, v, z, x, w_out, post_g.reshape(1, d))


def _sgu_kernel(x_ref, g_ref, w_ref, lng_ref, lnb_ref, ws_ref, bs_ref, wo_ref, pg_ref, out_ref, s_ref):
    width = wo_ref.shape[0]
    tm = x_ref.shape[1]
    lane = lax.broadcasted_iota(jnp.int32, (1, LANES), 1)
    lo = lane < HALF
    x = x_ref[0]
    hb = _rms(x, g_ref[...]).astype(BF16)
    v = jax.nn.gelu(_dot(hb, w_ref[:, width:2 * width].astype(BF16)))
    mu = jnp.mean(v, axis=-1, keepdims=True)
    vc = v - mu
    var = jnp.mean(vc * vc, axis=-1, keepdims=True)
    vb = (vc * lax.rsqrt(var + EPS) * lng_ref[...] + lnb_ref[...]).astype(BF16)
    group = SGU_STACK
    for c0 in range(0, tm // SGU_CHUNK, group):
        for jj in range(width // LANES):
            blk = jnp.concatenate([vb[c * SGU_CHUNK:(c + 1) * SGU_CHUNK, jj * LANES:(jj + 1) * LANES]
                                   for c in range(c0, c0 + group)], axis=1)
            r = _dot(ws_ref[jj], blk)
            for k in range(group):
                c = c0 + k
                s_ref[c * SGU_CHUNK:(c + 1) * SGU_CHUNK, jj * LANES:(jj + 1) * LANES] = (
                    jnp.where(lo, r[:SGU_CHUNK, k * LANES:(k + 1) * LANES],
                              r[SGU_CHUNK:, k * LANES:(k + 1) * LANES]) + bs_ref[jj])
    u = jax.nn.gelu(_dot(hb, w_ref[:, :width].astype(BF16)))
    z = _dot(hb, w_ref[:, 2 * width:].astype(BF16))
    o = u * s_ref[...] * jax.nn.silu(z)
    r = _dot(o.astype(BF16), wo_ref[...].astype(BF16))
    out_ref[0] = x + _rms(r, pg_ref[...])


def _sgu_layer(x, pre_g, post_g, w_in, ln_g, ln_b, w_s, b_s, w_out):
    bsz, L, d = x.shape
    width = w_out.shape[0]
    T = SGU_CHUNK
    gd = width // SGU_GROUPS
    tril = jnp.tril(jnp.ones((T, T), dtype=bool))
    ws = jnp.where(tril[None], w_s, 0.0).reshape(SGU_GROUPS // 2, 2 * T, T).astype(BF16)
    bs = jnp.repeat(b_s.astype(F32).T, gd, axis=1)
    bs = bs.reshape(T, width // LANES, LANES).transpose(1, 0, 2)
    tm = 1024
    return pl.pallas_call(
        _sgu_kernel,
        out_shape=jax.ShapeDtypeStruct(x.shape, x.dtype),
        grid=(bsz, L // tm),
        in_specs=[pl.BlockSpec((1, tm, d), lambda b, i: (b, i, 0)),
                  _full((1, d)), _full(w_in.shape), _full((1, width)), _full((1, width)),
                  _full(ws.shape), _full(bs.shape), _full(w_out.shape), _full((1, d))],
        out_specs=pl.BlockSpec((1, tm, d), lambda b, i: (b, i, 0)),
        scratch_shapes=[pltpu.VMEM((tm, width), F32)],
        compiler_params=_cparams(("parallel", "parallel")),
        name="sgu",
    )(x, pre_g.reshape(1, d), w_in, ln_g.reshape(1, width), ln_b.reshape(1, width),
      ws, bs, w_out, post_g.reshape(1, d))


def kernel(x, pre_norm, post_norm, rel_bias, a_w_in, a_lam_re, a_lam_im, a_log_dt, a_b_re, a_b_im, a_c_re, a_c_im, a_d, a_w_glu, a_b_glu, a_w_out, b_w_in, b_sinks, b_w_out, c_w_in, c_q_norm, c_kv_norm, c_w_uq, c_w_ukv, c_w_out, d_w_in, d_ln_g, d_ln_b, d_w_s, d_b_s, d_w_out):
    depth = pre_norm.shape[0]
    for i in range(depth):
        kind, j = i % 4, i // 4
        if kind == 0:
            x = _s5_layer(x, pre_norm[i], post_norm[i], a_w_in[j], a_lam_re[j], a_lam_im[j], a_log_dt[j],
                          a_b_re[j], a_b_im[j], a_c_re[j], a_c_im[j], a_d[j], a_w_glu[j], a_b_glu[j],
                          a_w_out[j])
        elif kind == 1:
            x = _swa_layer(x, pre_norm[i], post_norm[i], b_w_in[j], b_sinks[j], b_w_out[j], rel_bias)
        elif kind == 2:
            x = _mla_layer(x, pre_norm[i], post_norm[i], c_w_in[j], c_q_norm[j], c_kv_norm[j], c_w_uq[j],
                           c_w_ukv[j], c_w_out[j])
        else:
            x = _sgu_layer(x, pre_norm[i], post_norm[i], d_w_in[j], d_ln_g[j], d_ln_b[j], d_w_s[j],
                           d_b_s[j], d_w_out[j])
    return x
```

```python
import functools
import math

import jax
import jax.numpy as jnp
import numpy as np
from jax import lax
from jax.experimental import pallas as pl
from jax.experimental.pallas import tpu as pltpu

F32 = jnp.float32
BF16 = jnp.bfloat16

D_MODEL = 1024
EPS = 1e-6
NEG_INF = -1e30
LANES = 128
HALF = LANES // 2

SSM_GROUP = 16
SSM_STATE = 64
S5_CH_BLOCK = LANES
S5_GROUPS_PER_BLOCK = S5_CH_BLOCK // SSM_GROUP
S5_STATE_BLOCK = S5_GROUPS_PER_BLOCK * SSM_STATE
S5_T = 64

HEAD_DIM = 64
SWA_HEADS = 16
SWA_KV_HEADS = 2
SWA_GROUP = SWA_HEADS // SWA_KV_HEADS
WINDOW = 128
SWA_WINDOWS_PER_STEP = 4
SWA_LOOKAHEAD = 2
SWA_UNIT_HEADS = 8
SWA_PV_DELAY = 1
REL_BUCKETS = 32
REL_MAX_DIST = 128

MLA_HEADS = 16
MLA_NOPE = 64
MLA_ROPE = 32
MLA_V = 64
MLA_KV_RANK = 256
MLA_Q_RANK = 768
ROPE_BASE = 10000.0
MLA_TQ = 256
MLA_TK = 256
MLA_LOOKAHEAD = 6
MLA_PV_DELAY = 2

SGU_CHUNK = 128
SGU_GROUPS = 16
SGU_STACK = 4

VMEM_LIMIT = 56 * 1024 * 1024


def _cparams(sem):
    return pltpu.CompilerParams(dimension_semantics=sem, vmem_limit_bytes=VMEM_LIMIT)


def _rms(x, g):
    return x * lax.rsqrt(jnp.mean(x * x, axis=-1, keepdims=True) + EPS) * g


def _dot(a, b):
    return jnp.dot(a, b, preferred_element_type=F32)


def _dot_nt(a, b):
    return lax.dot_general(a, b, (((1,), (1,)), ((), ())), preferred_element_type=F32)


def _full(shape):
    n = len(shape)
    return pl.BlockSpec(shape, lambda *_: (0,) * n, pipeline_mode=pl.Buffered(1))


def _s5_kernel(x_ref, g_ref, w_ref, perm_ref, permt_ref, bb_ref, cc_ref, ar_ref, ai_ref, d_ref,
               wg_ref, bg_ref, wo_ref, pg_ref, out_ref, u_ref, z_ref, y_ref, s_ref, carry_ref, *, tt):
    bsz = x_ref.shape[0]
    width = wg_ref.shape[0]
    rows = bsz * tt
    nblk = bb_ref.shape[0]
    sb = S5_STATE_BLOCK

    @pl.when(pl.program_id(0) == 0)
    def _():
        carry_ref[...] = jnp.zeros_like(carry_ref)

    x = x_ref[...].reshape(rows, x_ref.shape[2])
    hb = _rms(x, g_ref[...]).astype(BF16)
    hb = _dot(perm_ref[...], hb).astype(BF16)
    u_ref[...] = _dot(hb, w_ref[:, :width].astype(BF16))
    z_ref[...] = _dot(hb, w_ref[:, width:].astype(BF16))

    nbuf = s_ref.shape[0]

    def project_in(i):
        s_ref[i % nbuf] = _dot(u_ref[:, i * LANES:(i + 1) * LANES].astype(BF16), bb_ref[i])

    def project_out(i):
        ub = u_ref[:, i * LANES:(i + 1) * LANES]
        y = _dot(s_ref[i % nbuf].astype(BF16), cc_ref[i]) + d_ref[:, i * LANES:(i + 1) * LANES] * ub
        y_ref[:, i * LANES:(i + 1) * LANES] = jax.nn.gelu(y)

    project_in(0)
    for i in range(nblk):
        if i + 1 < nblk:
            project_in(i + 1)
        buf = s_ref.at[i % nbuf]
        ar = ar_ref[i]
        ai = ai_ref[i]
        sr = carry_ref[i, :, 0:sb]
        si = carry_ref[i, :, sb:2 * sb]
        for t in range(tt):
            r0 = t * bsz
            nr = ar * sr - ai * si + buf[r0:r0 + bsz, 0:sb]
            ni = ar * si + ai * sr + buf[r0:r0 + bsz, sb:2 * sb]
            buf[r0:r0 + bsz, 0:sb] = nr
            buf[r0:r0 + bsz, sb:2 * sb] = ni
            sr, si = nr, ni
        carry_ref[i, :, 0:sb] = sr
        carry_ref[i, :, sb:2 * sb] = si
        project_out(i)

    y = y_ref[...]
    gate = jax.nn.sigmoid(_dot(y.astype(BF16), wg_ref[...].astype(BF16)) + bg_ref[...])
    o = y * gate * jax.nn.silu(z_ref[...])
    ob = _dot(permt_ref[...], o.astype(BF16)).astype(BF16)
    r = _dot(ob, wo_ref[...].astype(BF16))
    out_ref[...] = (x + _rms(r, pg_ref[...])).reshape(out_ref.shape)


def _s5_discretize(lam_re, lam_im, log_dt, b_re, b_im):
    dt = jnp.exp(log_dt)[:, None]
    mag = jnp.exp(lam_re * dt)
    ab_re = mag * jnp.cos(lam_im * dt)
    ab_im = mag * jnp.sin(lam_im * dt)
    den = lam_re * lam_re + lam_im * lam_im
    nr = ab_re - 1.0
    f_re = (nr * lam_re + ab_im * lam_im) / den
    f_im = (ab_im * lam_re - nr * lam_im) / den
    bb_re = f_re[..., None] * b_re - f_im[..., None] * b_im
    bb_im = f_re[..., None] * b_im + f_im[..., None] * b_re
    return ab_re, ab_im, bb_re, bb_im


def _s5_layer(x, pre_g, post_g, w_in, lam_re, lam_im, log_dt, b_re, b_im, c_re, c_im, d_skip,
              w_glu, b_glu, w_out):
    bsz, L, d = x.shape
    width = w_in.shape[1] // 2
    nblk = width // S5_CH_BLOCK
    gpb = S5_GROUPS_PER_BLOCK
    tt = S5_T
    rows = bsz * tt

    src = (np.arange(rows) % bsz) * tt + np.arange(rows) // bsz
    perm_np = np.zeros((rows, rows), np.float32)
    perm_np[np.arange(rows), src] = 1.0
    perm = jnp.asarray(perm_np, BF16)
    perm_t = jnp.asarray(perm_np.T, BF16)

    ab_re, ab_im, bb_re, bb_im = _s5_discretize(lam_re, lam_im, log_dt, b_re, b_im)
    eye = jnp.eye(gpb, dtype=F32)

    def pack_b(bb):
        t = bb.reshape(nblk, gpb, SSM_STATE, SSM_GROUP)
        return jnp.einsum('igph,gk->ikhgp', t, eye).reshape(nblk, S5_CH_BLOCK, S5_STATE_BLOCK)

    def pack_c(cc):
        t = cc.reshape(nblk, gpb, SSM_GROUP, SSM_STATE)
        return jnp.einsum('ighp,gk->igpkh', t, eye).reshape(nblk, S5_STATE_BLOCK, S5_CH_BLOCK)

    bb = jnp.concatenate([pack_b(bb_re), pack_b(bb_im)], axis=2).astype(BF16)
    cc = jnp.concatenate([pack_c(c_re), -pack_c(c_im)], axis=1).astype(BF16)
    ar = jnp.broadcast_to(ab_re.reshape(nblk, 1, S5_STATE_BLOCK), (nblk, bsz, S5_STATE_BLOCK))
    ai = jnp.broadcast_to(ab_im.reshape(nblk, 1, S5_STATE_BLOCK), (nblk, bsz, S5_STATE_BLOCK))

    xspec = pl.BlockSpec((bsz, tt, d), lambda i: (0, i, 0))
    return pl.pallas_call(
        functools.partial(_s5_kernel, tt=tt),
        out_shape=jax.ShapeDtypeStruct(x.shape, x.dtype),
        grid=(L // tt,),
        in_specs=[xspec, _full((1, d)), _full(w_in.shape), _full(perm.shape), _full(perm_t.shape),
                  _full(bb.shape), _full(cc.shape), _full(ar.shape), _full(ai.shape), _full((1, width)),
                  _full(w_glu.shape), _full((1, width)), _full(w_out.shape), _full((1, d))],
        out_specs=xspec,
        scratch_shapes=[pltpu.VMEM((rows, width), F32),
                        pltpu.VMEM((rows, width), F32),
                        pltpu.VMEM((rows, width), F32),
                        pltpu.VMEM((2, rows, 2 * S5_STATE_BLOCK), F32),
                        pltpu.VMEM((nblk, bsz, 2 * S5_STATE_BLOCK), F32)],
        compiler_params=_cparams(("arbitrary",)),
        name="s5_layer",
    )(x, pre_g.reshape(1, d), w_in, perm, perm_t, bb, cc, ar, ai, d_skip.reshape(1, width),
      w_glu, b_glu.reshape(1, width), w_out, post_g.reshape(1, d))


def _swa_bias(rel_bias):
    W = WINDOW
    n = 4 * W
    dist = 2 * W - jnp.arange(n)
    valid = jnp.logical_and(dist >= 0, dist < W)
    dpos = jnp.maximum(dist, 0)
    max_exact = REL_BUCKETS // 2
    dist_f = jnp.maximum(dpos, 1).astype(F32)
    large = max_exact + (jnp.log(dist_f / max_exact) / math.log(REL_MAX_DIST / max_exact)
                         * (REL_BUCKETS - max_exact)).astype(jnp.int32)
    large = jnp.minimum(large, REL_BUCKETS - 1)
    bucket = jnp.where(dpos < max_exact, dpos, large)
    vec = jnp.where(valid[:, None], rel_bias[bucket].astype(F32), NEG_INF).T
    skew = jnp.tile(vec, (1, W))[:, :W * (n - 1)].reshape(vec.shape[0], W, n - 1)
    return skew[:, :, W:3 * W]


def _swa_pre_kernel(x_ref, g_ref, wqt_ref, wk_ref, wvt_ref, wz_ref, qt_ref, k_ref, vt_ref, z_ref, *, scale):
    hb = _rms(x_ref[0], g_ref[...]).astype(BF16)
    qt_ref[0] = (_dot_nt(wqt_ref[...], hb) * scale).astype(BF16)
    k_ref[0] = _dot(hb, wk_ref[...]).astype(BF16)
    vt_ref[0] = _dot_nt(wvt_ref[...], hb).astype(BF16)
    z_ref[0] = _dot(hb, wz_ref[...])


def _swa_kernel(qt_ref, kp_ref, kc_ref, vtp_ref, vtc_ref, bias_ref, sink_ref, z_ref, x_ref, wo_ref, g_ref,
                out_ref, ot_ref):
    W = WINDOW
    nwin = qt_ref.shape[2] // W
    step = pl.program_id(1)
    kall = jnp.concatenate([kp_ref[0], kc_ref[0]], axis=0)
    vtall = jnp.concatenate([vtp_ref[0], vtc_ref[0]], axis=1)
    nsub = SWA_UNIT_HEADS
    zq = jnp.zeros((HEAD_DIM, nsub * W), BF16)
    ones = jnp.ones((16, 2 * W), BF16)
    units = [(w, h, c) for w in range(nwin) for h in range(SWA_KV_HEADS) for c in range(SWA_GROUP // nsub)]

    def scores(w, h, c):
        hd0 = h * SWA_GROUP + c * nsub
        qh = jnp.concatenate([qt_ref[0, (hd0 + g) * HEAD_DIM:(hd0 + g + 1) * HEAD_DIM, w * W:(w + 1) * W]
                              for g in range(nsub)], axis=1)
        qz = jnp.concatenate([qh, zq] if h == 0 else [zq, qh], axis=0)
        return _dot(kall[w * W:(w + 2) * W], qz)

    pending = [scores(*u) for u in units[:SWA_LOOKAHEAD]]
    late = []

    def flush():
        (w, h, c), p, tail = late.pop(0)
        vones = jnp.concatenate([vtall[h * HEAD_DIM:(h + 1) * HEAD_DIM, w * W:(w + 2) * W], ones], axis=0)
        o = _dot(vones, p)
        oh = o[:HEAD_DIM] * (1.0 / (o[HEAD_DIM:HEAD_DIM + 1] + tail))
        for g in range(nsub):
            hd = h * SWA_GROUP + c * nsub + g
            ot_ref[hd * HEAD_DIM:(hd + 1) * HEAD_DIM, w * W:(w + 1) * W] = oh[:, g * W:(g + 1) * W]

    for idx, (w, h, c) in enumerate(units):
        raw = pending.pop(0)
        if idx + SWA_LOOKAHEAD < len(units):
            pending.append(scores(*units[idx + SWA_LOOKAHEAD]))
        cols = slice(c * nsub * W, (c + 1) * nsub * W)
        variant = (step == 0).astype(jnp.int32) if w == 0 else 0
        s = raw + bias_ref[variant, h, :, cols]
        sink = sink_ref[h, :, cols]
        m = jnp.maximum(jnp.max(s, axis=0, keepdims=True), sink)
        if len(late) == SWA_PV_DELAY:
            flush()
        late.append(((w, h, c), jnp.exp2(s - m).astype(BF16), jnp.exp2(sink - m)))
    while late:
        flush()
    gated = ot_ref[...].T * jax.nn.silu(z_ref[0])
    r = _dot(gated.astype(BF16), wo_ref[...].astype(BF16))
    out_ref[0] = x_ref[0] + _rms(r, g_ref[...])


def _swa_layer(x, pre_g, post_g, w_in, sinks, w_out, rel_bias):
    bsz, L, d = x.shape
    width = SWA_HEADS * HEAD_DIM
    kvw = SWA_KV_HEADS * HEAD_DIM
    W = WINDOW
    nb = L // W
    log2e = math.log2(math.e)
    tm = 1024
    tok = lambda w_: pl.BlockSpec((1, tm, w_), lambda b, i: (b, i, 0))
    tokt = lambda w_: pl.BlockSpec((1, w_, tm), lambda b, i: (b, 0, i))
    wb = w_in.astype(BF16)
    wqt = wb[:, :width].T
    wk = wb[:, width:width + kvw]
    wvt = wb[:, width + kvw:width + 2 * kvw].T
    wz = wb[:, width + 2 * kvw:]
    qt, k, vt, z = pl.pallas_call(
        functools.partial(_swa_pre_kernel, scale=HEAD_DIM ** -0.5 * log2e),
        out_shape=[jax.ShapeDtypeStruct((bsz, width, L), BF16),
                   jax.ShapeDtypeStruct((bsz, L, kvw), BF16),
                   jax.ShapeDtypeStruct((bsz, kvw, L), BF16),
                   jax.ShapeDtypeStruct((bsz, L, width), F32)],
        grid=(bsz, L // tm),
        in_specs=[tok(d), _full((1, d)), _full(wqt.shape), _full(wk.shape), _full(wvt.shape), _full(wz.shape)],
        out_specs=[tokt(width), tok(kvw), tokt(kvw), tok(width)],
        compiler_params=_cparams(("parallel", "parallel")),
        name="swa_pre",
    )(x, pre_g.reshape(1, d), wqt, wk, wvt, wz)

    bias = jnp.transpose(_swa_bias(rel_bias.astype(F32) * log2e), (0, 2, 1))
    has_prev = (jnp.arange(2 * W) >= W)[None, :, None]
    variants = [bias,
                jnp.where(has_prev, bias, NEG_INF)]
    bias_t = jnp.stack([v.reshape(SWA_KV_HEADS, SWA_GROUP, 2 * W, W).transpose(0, 2, 1, 3)
                        .reshape(SWA_KV_HEADS, 2 * W, SWA_GROUP * W) for v in variants])
    sink = jnp.repeat(sinks.astype(F32) * log2e, W).reshape(SWA_KV_HEADS, 1, SWA_GROUP * W)

    nwin = SWA_WINDOWS_PER_STEP
    tq = nwin * W
    prev = lambda n: jnp.maximum(n * nwin - 1, 0)
    return pl.pallas_call(
        _swa_kernel,
        out_shape=jax.ShapeDtypeStruct(x.shape, x.dtype),
        grid=(bsz, L // tq),
        in_specs=[pl.BlockSpec((1, width, tq), lambda b, n: (b, 0, n)),
                  pl.BlockSpec((1, W, kvw), lambda b, n: (b, prev(n), 0)),
                  pl.BlockSpec((1, tq, kvw), lambda b, n: (b, n, 0)),
                  pl.BlockSpec((1, kvw, W), lambda b, n: (b, 0, prev(n))),
                  pl.BlockSpec((1, kvw, tq), lambda b, n: (b, 0, n)),
                  _full(bias_t.shape), _full(sink.shape),
                  pl.BlockSpec((1, tq, width), lambda b, n: (b, n, 0)),
                  pl.BlockSpec((1, tq, d), lambda b, n: (b, n, 0)),
                  _full(w_out.shape), _full((1, d))],
        out_specs=pl.BlockSpec((1, tq, d), lambda b, n: (b, n, 0)),
        scratch_shapes=[pltpu.VMEM((width, tq), F32)],
        compiler_params=_cparams(("parallel", "parallel")),
        name="swa_attn",
    )(qt, k, k, vt, vt, bias_t, sink, z, x, w_out, post_g.reshape(1, d))


def _mla_pre_kernel(x_ref, g_ref, w_ref, qn_ref, kvn_ref, wq_ref, wkv_ref, wvt_ref, cq_ref, sq_ref, ck_ref, sk_ref,
                    oqn_ref, oqr_ref, okn_ref, okr_ref, ov_ref, oz_ref, *, scale):
    nope = MLA_HEADS * MLA_NOPE
    rope = MLA_HEADS * MLA_ROPE
    vw = MLA_HEADS * MLA_V
    hb = _rms(x_ref[0], g_ref[...]).astype(BF16)
    o1 = MLA_Q_RANK
    o2 = o1 + MLA_KV_RANK
    o3 = o2 + vw
    cq = _dot(hb, w_ref[:, :o1])
    ckv = _dot(hb, w_ref[:, o1:o2])
    oz_ref[0] = _dot(hb, w_ref[:, o2:o3])
    kr = _dot(hb, w_ref[:, o3:o3 + LANES])
    krs = _dot(hb, w_ref[:, o3 + LANES:o3 + 2 * LANES])
    okr_ref[0] = (kr * ck_ref[...] + krs * sk_ref[...]).astype(BF16)
    cqb = _rms(cq, qn_ref[...]).astype(BF16)
    oqn_ref[0] = (_dot_nt(wq_ref[:nope], cqb) * scale).astype(BF16)
    qr = _dot_nt(wq_ref[nope:nope + rope], cqb)
    hr = MLA_ROPE // 2
    qrs = jnp.concatenate([qr[h * MLA_ROPE + o:h * MLA_ROPE + o + hr]
                           for h in range(MLA_HEADS) for o in (hr, 0)], axis=0)
    oqr_ref[0] = ((qr * cq_ref[...] + qrs * sq_ref[...]) * scale).astype(BF16)
    ckb = _rms(ckv, kvn_ref[...]).astype(BF16)
    okn_ref[0] = _dot(ckb, wkv_ref[:, :nope]).astype(BF16)
    vt = _dot_nt(wvt_ref[...], ckb).astype(BF16)
    tk = ov_ref.shape[3]
    for c in range(ov_ref.shape[1]):
        ov_ref[0, c] = vt[:, c * tk:(c + 1) * tk]


def _mla_attn_kernel(qn_ref, qr_ref, kn_ref, kr_ref, v_ref, z_ref, x_ref, wo_ref, g_ref, out_ref,
                     qs_ref, acc_ref, m_ref, o_ref):
    tq = qn_ref.shape[2]
    tk = v_ref.shape[3]
    npairs = MLA_HEADS // 2
    i = pl.program_id(1)
    krow = lax.broadcasted_iota(jnp.int32, (tk, 2 * tq), 0)
    qcol = lax.broadcasted_iota(jnp.int32, (tk, 2 * tq), 1)
    causal = krow <= jnp.where(qcol >= tq, qcol - tq, qcol)

    zn = jnp.zeros((MLA_NOPE, tq), BF16)
    zr = jnp.zeros((LANES - MLA_ROPE, tq), BF16)
    for p in range(npairs):
        qn = qn_ref[0, p * LANES:(p + 1) * LANES, :]
        r0 = 2 * p * MLA_ROPE
        c0 = jnp.concatenate([qn[:MLA_NOPE], zn, qr_ref[0, r0:r0 + MLA_ROPE, :], zr], axis=0)
        c1 = jnp.concatenate([zn, qn[MLA_NOPE:], qr_ref[0, r0 + MLA_ROPE:r0 + 2 * MLA_ROPE, :], zr], axis=0)
        qs_ref[p] = jnp.concatenate([c0, c1], axis=1)

    m_ref[...] = jnp.full(m_ref.shape, NEG_INF, F32)
    acc_ref[...] = jnp.zeros(acc_ref.shape, F32)
    ones = jnp.ones((acc_ref.shape[1] - LANES, tk), BF16)

    def kv_steps(blocks):
        units = [(j, masked, p) for j, masked in blocks for p in range(npairs)]

        def scores(j, p):
            ks = pl.multiple_of(j * tk, tk)
            kc = jnp.concatenate([kn_ref[0, pl.ds(ks, tk), p * LANES:(p + 1) * LANES],
                                  kr_ref[0, pl.ds(ks, tk), :]], axis=1)
            return [_dot(kc, qs_ref[p, :, c * 2 * LANES:(c + 1) * 2 * LANES]) for c in range(tq // LANES)]

        pending = [scores(j, p) for j, _, p in units[:MLA_LOOKAHEAD]]
        late = []

        def flush():
            jj, pp, alpha, pr = late.pop(0)
            vones = jnp.concatenate([v_ref[0, jj, pp * LANES:(pp + 1) * LANES, :], ones], axis=0)
            acc_ref[pp] = alpha * acc_ref[pp] + _dot(vones, pr)

        for idx, (j, masked, p) in enumerate(units):
            s = pending.pop(0)
            if idx + MLA_LOOKAHEAD < len(units):
                nxt = units[idx + MLA_LOOKAHEAD]
                pending.append(scores(nxt[0], nxt[2]))
            probs, alphas = [], []
            for c in range(2 * tq // LANES):
                sc = s[c // 2][:, (c % 2) * LANES:(c % 2 + 1) * LANES]
                if masked:
                    sc = jnp.where(causal[:, c * LANES:(c + 1) * LANES], sc, NEG_INF)
                m_prev = m_ref[p, :, c * LANES:(c + 1) * LANES]
                m_new = jnp.maximum(m_prev, jnp.max(sc, axis=0, keepdims=True))
                alphas.append(jnp.exp2(m_prev - m_new))
                probs.append(jnp.exp2(sc - m_new).astype(BF16))
                m_ref[p, :, c * LANES:(c + 1) * LANES] = m_new
            if len(late) == MLA_PV_DELAY:
                flush()
            late.append((j, p, jnp.concatenate(alphas, axis=1), jnp.concatenate(probs, axis=1)))
        while late:
            flush()

    def body(jj, c):
        kv_steps([(2 * jj, False), (2 * jj + 1, False)])
        return c

    lax.fori_loop(0, i // 2, body, 0)

    @pl.when(i % 2 == 1)
    def _():
        kv_steps([(i - 1, False), (i, True)])

    @pl.when(i % 2 == 0)
    def _():
        kv_steps([(i, True)])
    for p in range(npairs):
        a = acc_ref[p]
        a = a[:LANES] * (1.0 / a[LANES:LANES + 1])
        ot = jnp.concatenate([a[:MLA_V, :tq], a[MLA_V:, tq:]], axis=0)
        o_ref[:, p * LANES:(p + 1) * LANES] = ot.T
    gated = o_ref[...] * jax.nn.silu(z_ref[0])
    r = _dot(gated.astype(BF16), wo_ref[...].astype(BF16))
    out_ref[0] = x_ref[0] + _rms(r, g_ref[...])


def _mla_layer(x, pre_g, post_g, w_in, q_norm, kv_norm, w_uq, w_ukv, w_out):
    bsz, L, d = x.shape
    H = MLA_HEADS
    dq = MLA_NOPE + MLA_ROPE
    nope = H * MLA_NOPE
    rope = H * MLA_ROPE
    vw = H * MLA_V
    half = MLA_ROPE // 2
    o_kr = MLA_Q_RANK + MLA_KV_RANK
    o_z = o_kr + MLA_ROPE
    wb = w_in.astype(BF16)
    w_kr = wb[:, o_kr:o_z]
    w_krs = jnp.concatenate([w_kr[:, half:], w_kr[:, :half]], axis=1)
    reps = LANES // MLA_ROPE
    w1 = jnp.concatenate([wb[:, :o_kr], wb[:, o_z:]] + [w_kr] * reps + [w_krs] * reps, axis=1)
    wq3 = w_uq.astype(BF16).reshape(MLA_Q_RANK, H, dq)
    wqt = jnp.concatenate([wq3[:, :, :MLA_NOPE].reshape(MLA_Q_RANK, nope),
                           wq3[:, :, MLA_NOPE:].reshape(MLA_Q_RANK, rope)], axis=1).T
    wkv3 = w_ukv.astype(BF16).reshape(MLA_KV_RANK, H, MLA_NOPE + MLA_V)
    wkn = wkv3[:, :, :MLA_NOPE].reshape(MLA_KV_RANK, nope)
    wvt = wkv3[:, :, MLA_NOPE:].reshape(MLA_KV_RANK, vw).T
    inv = ROPE_BASE ** (-jnp.arange(0, MLA_ROPE, 2, dtype=F32) / MLA_ROPE)
    ang = jnp.arange(L, dtype=F32)[:, None] * inv[None, :]
    cos, sin = jnp.cos(ang), jnp.sin(ang)
    cos32 = jnp.concatenate([cos, cos], axis=1)
    sin32 = jnp.concatenate([-sin, sin], axis=1)
    cos_k, sin_k = jnp.tile(cos32, (1, LANES // MLA_ROPE)), jnp.tile(sin32, (1, LANES // MLA_ROPE))
    cos_q, sin_q = jnp.tile(cos32, (1, H)).T, jnp.tile(sin32, (1, H)).T

    tm = 1024
    tk = MLA_TK
    tok = lambda w_: pl.BlockSpec((1, tm, w_), lambda b, i: (b, i, 0))
    tokt = lambda w_: pl.BlockSpec((1, w_, tm), lambda b, i: (b, 0, i))
    scale = dq ** -0.5 * math.log2(math.e)
    qn, qr, kn, kr, v, z = pl.pallas_call(
        functools.partial(_mla_pre_kernel, scale=scale),
        out_shape=[jax.ShapeDtypeStruct((bsz, nope, L), BF16),
                   jax.ShapeDtypeStruct((bsz, rope, L), BF16),
                   jax.ShapeDtypeStruct((bsz, L, nope), BF16),
                   jax.ShapeDtypeStruct((bsz, L, LANES), BF16),
                   jax.ShapeDtypeStruct((bsz, L // tk, vw, tk), BF16),
                   jax.ShapeDtypeStruct((bsz, L, vw), F32)],
        grid=(bsz, L // tm),
        in_specs=[tok(d), _full((1, d)), _full(w1.shape), _full((1, MLA_Q_RANK)), _full((1, MLA_KV_RANK)),
                  _full(wqt.shape), _full(wkn.shape), _full(wvt.shape),
                  pl.BlockSpec((rope, tm), lambda b, i: (0, i)), pl.BlockSpec((rope, tm), lambda b, i: (0, i)),
                  pl.BlockSpec((tm, LANES), lambda b, i: (i, 0)), pl.BlockSpec((tm, LANES), lambda b, i: (i, 0))],
        out_specs=[tokt(nope), tokt(rope), tok(nope), tok(LANES),
                   pl.BlockSpec((1, tm // tk, vw, tk), lambda b, i: (b, i, 0, 0)), tok(vw)],
        compiler_params=_cparams(("parallel", "parallel")),
        name="mla_pre",
    )(x, pre_g.reshape(1, d), w1, q_norm.reshape(1, -1), kv_norm.reshape(1, -1), wqt, wkn, wvt,
      cos_q, sin_q, cos_k, sin_k)

    tq = MLA_TQ
    npairs = H // 2
    qspec = lambda w_: pl.BlockSpec((1, w_, tq), lambda b, i: (b, 0, i))
    kspec = lambda w_: pl.BlockSpec((1, L, w_), lambda b, i: (b, 0, 0))
    rowspec = lambda w_: pl.BlockSpec((1, tq, w_), lambda b, i: (b, i, 0))
    return pl.pallas_call(
        _mla_attn_kernel,
        out_shape=jax.ShapeDtypeStruct(x.shape, x.dtype),
        grid=(bsz, L // tq),
        in_specs=[qspec(nope), qspec(rope), kspec(nope), kspec(LANES),
                  pl.BlockSpec((1, L // tk, vw, tk), lambda b, i: (b, 0, 0, 0)),
                  rowspec(vw), rowspec(d), _full(w_out.shape), _full((1, d))],
        out_specs=rowspec(d),
        scratch_shapes=[pltpu.VMEM((npairs, 2 * LANES, 2 * tq), BF16),
                        pltpu.VMEM((npairs, LANES + 16, 2 * tq), F32),
                        pltpu.VMEM((npairs, 1, 2 * tq), F32),
                        pltpu.VMEM((tq, vw), F32)],
        compiler_params=_cparams(("parallel", "arbitrary")),
        name="mla_attn",
    )(qn, qr, kn, kr, v, z, x, w_out, post_g.reshape(1, d))


def _sgu_kernel(x_ref, g_ref, w_ref, lng_ref, lnb_ref, ws_ref, bs_ref, wo_ref, pg_ref, out_ref, s_ref):
    width = wo_ref.shape[0]
    tm = x_ref.shape[1]
    lane = lax.broadcasted_iota(jnp.int32, (1, LANES), 1)
    lo = lane < HALF
    x = x_ref[0]
    hb = _rms(x, g_ref[...]).astype(BF16)
    v = jax.nn.gelu(_dot(hb, w_ref[:, width:2 * width].astype(BF16)))
    mu = jnp.mean(v, axis=-1, keepdims=True)
    vc = v - mu
    var = jnp.mean(vc * vc, axis=-1, keepdims=True)
    vb = (vc * lax.rsqrt(var + EPS) * lng_ref[...] + lnb_ref[...]).astype(BF16)
    group = SGU_STACK
    for c0 in range(0, tm // SGU_CHUNK, group):
        for jj in range(width // LANES):
            blk = jnp.concatenate([vb[c * SGU_CHUNK:(c + 1) * SGU_CHUNK, jj * LANES:(jj + 1) * LANES]
                                   for c in range(c0, c0 + group)], axis=1)
            r = _dot(ws_ref[jj], blk)
            for k in range(group):
                c = c0 + k
                s_ref[c * SGU_CHUNK:(c + 1) * SGU_CHUNK, jj * LANES:(jj + 1) * LANES] = (
                    jnp.where(lo, r[:SGU_CHUNK, k * LANES:(k + 1) * LANES],
                              r[SGU_CHUNK:, k * LANES:(k + 1) * LANES]) + bs_ref[jj])
    u = jax.nn.gelu(_dot(hb, w_ref[:, :width].astype(BF16)))
    z = _dot(hb, w_ref[:, 2 * width:].astype(BF16))
    o = u * s_ref[...] * jax.nn.silu(z)
    r = _dot(o.astype(BF16), wo_ref[...].astype(BF16))
    out_ref[0] = x + _rms(r, pg_ref[...])


def _sgu_layer(x, pre_g, post_g, w_in, ln_g, ln_b, w_s, b_s, w_out):
    bsz, L, d = x.shape
    width = w_out.shape[0]
    T = SGU_CHUNK
    gd = width // SGU_GROUPS
    tril = jnp.tril(jnp.ones((T, T), dtype=bool))
    ws = jnp.where(tril[None], w_s, 0.0).reshape(SGU_GROUPS // 2, 2 * T, T).astype(BF16)
    bs = jnp.repeat(b_s.astype(F32).T, gd, axis=1)
    bs = bs.reshape(T, width // LANES, LANES).transpose(1, 0, 2)
    tm = 1024
    return pl.pallas_call(
        _sgu_kernel,
        out_shape=jax.ShapeDtypeStruct(x.shape, x.dtype),
        grid=(bsz, L // tm),
        in_specs=[pl.BlockSpec((1, tm, d), lambda b, i: (b, i, 0)),
                  _full((1, d)), _full(w_in.shape), _full((1, width)), _full((1, width)),
                  _full(ws.shape), _full(bs.shape), _full(w_out.shape), _full((1, d))],
        out_specs=pl.BlockSpec((1, tm, d), lambda b, i: (b, i, 0)),
        scratch_shapes=[pltpu.VMEM((tm, width), F32)],
        compiler_params=_cparams(("parallel", "parallel")),
        name="sgu",
    )(x, pre_g.reshape(1, d), w_in, ln_g.reshape(1, width), ln_b.reshape(1, width),
      ws, bs, w_out, post_g.reshape(1, d))


def kernel(x, pre_norm, post_norm, rel_bias, a_w_in, a_lam_re, a_lam_im, a_log_dt, a_b_re, a_b_im, a_c_re, a_c_im, a_d, a_w_glu, a_b_glu, a_w_out, b_w_in, b_sinks, b_w_out, c_w_in, c_q_norm, c_kv_norm, c_w_uq, c_w_ukv, c_w_out, d_w_in, d_ln_g, d_ln_b, d_w_s, d_b_s, d_w_out):
    depth = pre_norm.shape[0]
    for i in range(depth):
        kind, j = i % 4, i // 4
        if kind == 0:
            x = _s5_layer(x, pre_norm[i], post_norm[i], a_w_in[j], a_lam_re[j], a_lam_im[j], a_log_dt[j],
                          a_b_re[j], a_b_im[j], a_c_re[j], a_c_im[j], a_d[j], a_w_glu[j], a_b_glu[j],
                          a_w_out[j])
        elif kind == 1:
            x = _swa_layer(x, pre_norm[i], post_norm[i], b_w_in[j], b_sinks[j], b_w_out[j], rel_bias)
        elif kind == 2:
            x = _mla_layer(x, pre_norm[i], post_norm[i], c_w_in[j], c_q_norm[j], c_kv_norm[j], c_w_uq[j],
                           c_w_ukv[j], c_w_out[j])
        else:
            x = _sgu_layer(x, pre_norm[i], post_norm[i], d_w_in[j], d_ln_g[j], d_ln_b[j], d_w_s[j],
                           d_b_s[j], d_w_out[j])
    return x
```

```python
import functools
import math

import jax
import jax.numpy as jnp
import numpy as np
from jax import lax
from jax.experimental import pallas as pl
from jax.experimental.pallas import tpu as pltpu

F32 = jnp.float32
BF16 = jnp.bfloat16

D_MODEL = 1024
EPS = 1e-6
NEG_INF = -1e30
LANES = 128
HALF = LANES // 2

SSM_GROUP = 16
SSM_STATE = 64
S5_CH_BLOCK = LANES
S5_GROUPS_PER_BLOCK = S5_CH_BLOCK // SSM_GROUP
S5_STATE_BLOCK = S5_GROUPS_PER_BLOCK * SSM_STATE
S5_T = 64

HEAD_DIM = 64
SWA_HEADS = 16
SWA_KV_HEADS = 2
SWA_GROUP = SWA_HEADS // SWA_KV_HEADS
WINDOW = 128
SWA_WINDOWS_PER_STEP = 4
SWA_LOOKAHEAD = 2
SWA_UNIT_HEADS = 8
SWA_PV_DELAY = 1
REL_BUCKETS = 32
REL_MAX_DIST = 128

MLA_HEADS = 16
MLA_NOPE = 64
MLA_ROPE = 32
MLA_V = 64
MLA_KV_RANK = 256
MLA_Q_RANK = 768
ROPE_BASE = 10000.0
MLA_TQ = 256
MLA_TK = 256
MLA_LOOKAHEAD = 6
MLA_PV_DELAY = 2

SGU_CHUNK = 128
SGU_GROUPS = 16
SGU_STACK = 4

VMEM_LIMIT = 56 * 1024 * 1024


def _cparams(sem):
    return pltpu.CompilerParams(dimension_semantics=sem, vmem_limit_bytes=VMEM_LIMIT)


def _rms(x, g):
    return x * lax.rsqrt(jnp.mean(x * x, axis=-1, keepdims=True) + EPS) * g


def _dot(a, b):
    return jnp.dot(a, b, preferred_element_type=F32)


def _dot_nt(a, b):
    return lax.dot_general(a, b, (((1,), (1,)), ((), ())), preferred_element_type=F32)


def _full(shape):
    n = len(shape)
    return pl.BlockSpec(shape, lambda *_: (0,) * n, pipeline_mode=pl.Buffered(1))


def _s5_kernel(x_ref, g_ref, w_ref, perm_ref, permt_ref, bb_ref, cc_ref, ar_ref, ai_ref, d_ref,
               wg_ref, bg_ref, wo_ref, pg_ref, out_ref, u_ref, z_ref, y_ref, s_ref, carry_ref, *, tt):
    bsz = x_ref.shape[0]
    width = wg_ref.shape[0]
    rows = bsz * tt
    nblk = bb_ref.shape[0]
    sb = S5_STATE_BLOCK

    @pl.when(pl.program_id(0) == 0)
    def _():
        carry_ref[...] = jnp.zeros_like(carry_ref)

    x = x_ref[...].reshape(rows, x_ref.shape[2])
    hb = _rms(x, g_ref[...]).astype(BF16)
    hb = _dot(perm_ref[...], hb).astype(BF16)
    u_ref[...] = _dot(hb, w_ref[:, :width].astype(BF16))
    z_ref[...] = _dot(hb, w_ref[:, width:].astype(BF16))

    nbuf = s_ref.shape[0]

    def project_in(i):
        s_ref[i % nbuf] = _dot(u_ref[:, i * LANES:(i + 1) * LANES].astype(BF16), bb_ref[i])

    def project_out(i):
        ub = u_ref[:, i * LANES:(i + 1) * LANES]
        y = _dot(s_ref[i % nbuf].astype(BF16), cc_ref[i]) + d_ref[:, i * LANES:(i + 1) * LANES] * ub
        y_ref[:, i * LANES:(i + 1) * LANES] = jax.nn.gelu(y)

    project_in(0)
    for i in range(nblk):
        if i + 1 < nblk:
            project_in(i + 1)
        buf = s_ref.at[i % nbuf]
        ar = ar_ref[i]
        ai = ai_ref[i]
        sr = carry_ref[i, :, 0:sb]
        si = carry_ref[i, :, sb:2 * sb]
        for t in range(tt):
            r0 = t * bsz
            nr = ar * sr - ai * si + buf[r0:r0 + bsz, 0:sb]
            ni = ar * si + ai * sr + buf[r0:r0 + bsz, sb:2 * sb]
            buf[r0:r0 + bsz, 0:sb] = nr
            buf[r0:r0 + bsz, sb:2 * sb] = ni
            sr, si = nr, ni
        carry_ref[i, :, 0:sb] = sr
        carry_ref[i, :, sb:2 * sb] = si
        project_out(i)

    y = y_ref[...]
    gate = jax.nn.sigmoid(_dot(y.astype(BF16), wg_ref[...].astype(BF16)) + bg_ref[...])
    o = y * gate * jax.nn.silu(z_ref[...])
    ob = _dot(permt_ref[...], o.astype(BF16)).astype(BF16)
    r = _dot(ob, wo_ref[...].astype(BF16))
    out_ref[...] = (x + _rms(r, pg_ref[...])).reshape(out_ref.shape)


def _s5_discretize(lam_re, lam_im, log_dt, b_re, b_im):
    dt = jnp.exp(log_dt)[:, None]
    mag = jnp.exp(lam_re * dt)
    ab_re = mag * jnp.cos(lam_im * dt)
    ab_im = mag * jnp.sin(lam_im * dt)
    den = lam_re * lam_re + lam_im * lam_im
    nr = ab_re - 1.0
    f_re = (nr * lam_re + ab_im * lam_im) / den
    f_im = (ab_im * lam_re - nr * lam_im) / den
    bb_re = f_re[..., None] * b_re - f_im[..., None] * b_im
    bb_im = f_re[..., None] * b_im + f_im[..., None] * b_re
    return ab_re, ab_im, bb_re, bb_im


def _s5_layer(x, pre_g, post_g, w_in, lam_re, lam_im, log_dt, b_re, b_im, c_re, c_im, d_skip,
              w_glu, b_glu, w_out):
    bsz, L, d = x.shape
    width = w_in.shape[1] // 2
    nblk = width // S5_CH_BLOCK
    gpb = S5_GROUPS_PER_BLOCK
    tt = S5_T
    rows = bsz * tt

    src = (np.arange(rows) % bsz) * tt + np.arange(rows) // bsz
    perm_np = np.zeros((rows, rows), np.float32)
    perm_np[np.arange(rows), src] = 1.0
    perm = jnp.asarray(perm_np, BF16)
    perm_t = jnp.asarray(perm_np.T, BF16)

    ab_re, ab_im, bb_re, bb_im = _s5_discretize(lam_re, lam_im, log_dt, b_re, b_im)
    eye = jnp.eye(gpb, dtype=F32)

    def pack_b(bb):
        t = bb.reshape(nblk, gpb, SSM_STATE, SSM_GROUP)
        return jnp.einsum('igph,gk->ikhgp', t, eye).reshape(nblk, S5_CH_BLOCK, S5_STATE_BLOCK)

    def pack_c(cc):
        t = cc.reshape(nblk, gpb, SSM_GROUP, SSM_STATE)
        return jnp.einsum('ighp,gk->igpkh', t, eye).reshape(nblk, S5_STATE_BLOCK, S5_CH_BLOCK)

    bb = jnp.concatenate([pack_b(bb_re), pack_b(bb_im)], axis=2).astype(BF16)
    cc = jnp.concatenate([pack_c(c_re), -pack_c(c_im)], axis=1).astype(BF16)
    ar = jnp.broadcast_to(ab_re.reshape(nblk, 1, S5_STATE_BLOCK), (nblk, bsz, S5_STATE_BLOCK))
    ai = jnp.broadcast_to(ab_im.reshape(nblk, 1, S5_STATE_BLOCK), (nblk, bsz, S5_STATE_BLOCK))

    xspec = pl.BlockSpec((bsz, tt, d), lambda i: (0, i, 0))
    return pl.pallas_call(
        functools.partial(_s5_kernel, tt=tt),
        out_shape=jax.ShapeDtypeStruct(x.shape, x.dtype),
        grid=(L // tt,),
        in_specs=[xspec, _full((1, d)), _full(w_in.shape), _full(perm.shape), _full(perm_t.shape),
                  _full(bb.shape), _full(cc.shape), _full(ar.shape), _full(ai.shape), _full((1, width)),
                  _full(w_glu.shape), _full((1, width)), _full(w_out.shape), _full((1, d))],
        out_specs=xspec,
        scratch_shapes=[pltpu.VMEM((rows, width), F32),
                        pltpu.VMEM((rows, width), F32),
                        pltpu.VMEM((rows, width), F32),
                        pltpu.VMEM((2, rows, 2 * S5_STATE_BLOCK), F32),
                        pltpu.VMEM((nblk, bsz, 2 * S5_STATE_BLOCK), F32)],
        compiler_params=_cparams(("arbitrary",)),
        name="s5_layer",
    )(x, pre_g.reshape(1, d), w_in, perm, perm_t, bb, cc, ar, ai, d_skip.reshape(1, width),
      w_glu, b_glu.reshape(1, width), w_out, post_g.reshape(1, d))


def _swa_bias(rel_bias):
    W = WINDOW
    n = 4 * W
    dist = jnp.arange(n) - W
    valid = jnp.logical_and(dist >= 0, dist < W)
    dpos = jnp.maximum(dist, 0)
    max_exact = REL_BUCKETS // 2
    dist_f = jnp.maximum(dpos, 1).astype(F32)
    large = max_exact + (jnp.log(dist_f / max_exact) / math.log(REL_MAX_DIST / max_exact)
                         * (REL_BUCKETS - max_exact)).astype(jnp.int32)
    large = jnp.minimum(large, REL_BUCKETS - 1)
    bucket = jnp.where(dpos < max_exact, dpos, large)
    return jnp.where(valid[:, None], rel_bias[bucket].astype(F32), NEG_INF).T


def _swa_pre_kernel(x_ref, g_ref, wqt_ref, wk_ref, wvt_ref, wz_ref, qt_ref, k_ref, vt_ref, z_ref, *, scale):
    hb = _rms(x_ref[0], g_ref[...]).astype(BF16)
    qt_ref[0] = (_dot_nt(wqt_ref[...], hb) * scale).astype(BF16)
    k_ref[0] = _dot(hb, wk_ref[...]).astype(BF16)
    vt_ref[0] = _dot_nt(wvt_ref[...], hb).astype(BF16)
    z_ref[0] = _dot(hb, wz_ref[...])


def _swa_kernel(qt_ref, kp_ref, kc_ref, vtp_ref, vtc_ref, bvec_ref, sink_ref, z_ref, x_ref, wo_ref, g_ref,
                out_ref, ot_ref, bias_ref):
    W = WINDOW
    nwin = qt_ref.shape[2] // W
    step = pl.program_id(1)

    @pl.when(jnp.logical_and(pl.program_id(0) == 0, step == 0))
    def _():
        no_prev = lax.broadcasted_iota(jnp.int32, (2 * W, W), 0) < W
        for hd in range(SWA_HEADS):
            base = jnp.broadcast_to(bvec_ref[hd:hd + 1, :], (2 * W, bvec_ref.shape[1]))
            toep = pltpu.roll(base, 0, 1, stride=1, stride_axis=0)[:, 2 * W:3 * W]
            h, g = divmod(hd, SWA_GROUP)
            bias_ref[0, h, :, g * W:(g + 1) * W] = toep
            bias_ref[1, h, :, g * W:(g + 1) * W] = jnp.where(no_prev, NEG_INF, toep)
    kall = jnp.concatenate([kp_ref[0], kc_ref[0]], axis=0)
    vtall = jnp.concatenate([vtp_ref[0], vtc_ref[0]], axis=1)
    nsub = SWA_UNIT_HEADS
    zq = jnp.zeros((HEAD_DIM, nsub * W), BF16)
    ones = jnp.ones((16, 2 * W), BF16)
    units = [(w, h, c) for w in range(nwin) for h in range(SWA_KV_HEADS) for c in range(SWA_GROUP // nsub)]

    def scores(w, h, c):
        hd0 = h * SWA_GROUP + c * nsub
        qh = jnp.concatenate([qt_ref[0, (hd0 + g) * HEAD_DIM:(hd0 + g + 1) * HEAD_DIM, w * W:(w + 1) * W]
                              for g in range(nsub)], axis=1)
        qz = jnp.concatenate([qh, zq] if h == 0 else [zq, qh], axis=0)
        return _dot(kall[w * W:(w + 2) * W], qz)

    pending = [scores(*u) for u in units[:SWA_LOOKAHEAD]]
    late = []

    def flush():
        (w, h, c), p, tail = late.pop(0)
        vones = jnp.concatenate([vtall[h * HEAD_DIM:(h + 1) * HEAD_DIM, w * W:(w + 2) * W], ones], axis=0)
        o = _dot(vones, p)
        oh = o[:HEAD_DIM] * (1.0 / (o[HEAD_DIM:HEAD_DIM + 1] + tail))
        for g in range(nsub):
            hd = h * SWA_GROUP + c * nsub + g
            ot_ref[hd * HEAD_DIM:(hd + 1) * HEAD_DIM, w * W:(w + 1) * W] = oh[:, g * W:(g + 1) * W]

    for idx, (w, h, c) in enumerate(units):
        raw = pending.pop(0)
        if idx + SWA_LOOKAHEAD < len(units):
            pending.append(scores(*units[idx + SWA_LOOKAHEAD]))
        cols = slice(c * nsub * W, (c + 1) * nsub * W)
        variant = (step == 0).astype(jnp.int32) if w == 0 else 0
        s = raw + bias_ref[variant, h, :, cols]
        sink = sink_ref[h, :, cols]
        m = jnp.maximum(jnp.max(s, axis=0, keepdims=True), sink)
        if len(late) == SWA_PV_DELAY:
            flush()
        late.append(((w, h, c), jnp.exp2(s - m).astype(BF16), jnp.exp2(sink - m)))
    while late:
        flush()
    gated = ot_ref[...].T * jax.nn.silu(z_ref[0])
    r = _dot(gated.astype(BF16), wo_ref[...].astype(BF16))
    out_ref[0] = x_ref[0] + _rms(r, g_ref[...])


def _swa_layer(x, pre_g, post_g, w_in, sinks, w_out, rel_bias):
    bsz, L, d = x.shape
    width = SWA_HEADS * HEAD_DIM
    kvw = SWA_KV_HEADS * HEAD_DIM
    W = WINDOW
    nb = L // W
    log2e = math.log2(math.e)
    tm = 1024
    tok = lambda w_: pl.BlockSpec((1, tm, w_), lambda b, i: (b, i, 0))
    tokt = lambda w_: pl.BlockSpec((1, w_, tm), lambda b, i: (b, 0, i))
    wb = w_in.astype(BF16)
    wqt = wb[:, :width].T
    wk = wb[:, width:width + kvw]
    wvt = wb[:, width + kvw:width + 2 * kvw].T
    wz = wb[:, width + 2 * kvw:]
    qt, k, vt, z = pl.pallas_call(
        functools.partial(_swa_pre_kernel, scale=HEAD_DIM ** -0.5 * log2e),
        out_shape=[jax.ShapeDtypeStruct((bsz, width, L), BF16),
                   jax.ShapeDtypeStruct((bsz, L, kvw), BF16),
                   jax.ShapeDtypeStruct((bsz, kvw, L), BF16),
                   jax.ShapeDtypeStruct((bsz, L, width), F32)],
        grid=(bsz, L // tm),
        in_specs=[tok(d), _full((1, d)), _full(wqt.shape), _full(wk.shape), _full(wvt.shape), _full(wz.shape)],
        out_specs=[tokt(width), tok(kvw), tokt(kvw), tok(width)],
        compiler_params=_cparams(("parallel", "parallel")),
        name="swa_pre",
    )(x, pre_g.reshape(1, d), wqt, wk, wvt, wz)

    bvec = _swa_bias(rel_bias.astype(F32) * log2e)
    sink = jnp.repeat(sinks.astype(F32) * log2e, W).reshape(SWA_KV_HEADS, 1, SWA_GROUP * W)

    nwin = SWA_WINDOWS_PER_STEP
    tq = nwin * W
    prev = lambda n: jnp.maximum(n * nwin - 1, 0)
    return pl.pallas_call(
        _swa_kernel,
        out_shape=jax.ShapeDtypeStruct(x.shape, x.dtype),
        grid=(bsz, L // tq),
        in_specs=[pl.BlockSpec((1, width, tq), lambda b, n: (b, 0, n)),
                  pl.BlockSpec((1, W, kvw), lambda b, n: (b, prev(n), 0)),
                  pl.BlockSpec((1, tq, kvw), lambda b, n: (b, n, 0)),
                  pl.BlockSpec((1, kvw, W), lambda b, n: (b, 0, prev(n))),
                  pl.BlockSpec((1, kvw, tq), lambda b, n: (b, 0, n)),
                  _full(bvec.shape), _full(sink.shape),
                  pl.BlockSpec((1, tq, width), lambda b, n: (b, n, 0)),
                  pl.BlockSpec((1, tq, d), lambda b, n: (b, n, 0)),
                  _full(w_out.shape), _full((1, d))],
        out_specs=pl.BlockSpec((1, tq, d), lambda b, n: (b, n, 0)),
        scratch_shapes=[pltpu.VMEM((width, tq), F32),
                        pltpu.VMEM((2, SWA_KV_HEADS, 2 * W, SWA_GROUP * W), F32)],
        compiler_params=_cparams(("arbitrary", "arbitrary")),
        name="swa_attn",
    )(qt, k, k, vt, vt, bvec, sink, z, x, w_out, post_g.reshape(1, d))


def _mla_pre_kernel(x_ref, g_ref, w_ref, qn_ref, kvn_ref, wq_ref, wkv_ref, wvt_ref, cq_ref, sq_ref, ck_ref, sk_ref,
                    oqn_ref, oqr_ref, okn_ref, okr_ref, ov_ref, oz_ref, *, scale):
    nope = MLA_HEADS * MLA_NOPE
    rope = MLA_HEADS * MLA_ROPE
    vw = MLA_HEADS * MLA_V
    hb = _rms(x_ref[0], g_ref[...]).astype(BF16)
    o1 = MLA_Q_RANK
    o2 = o1 + MLA_KV_RANK
    o3 = o2 + vw
    cq = _dot(hb, w_ref[:, :o1])
    ckv = _dot(hb, w_ref[:, o1:o2])
    oz_ref[0] = _dot(hb, w_ref[:, o2:o3])
    kr = _dot(hb, w_ref[:, o3:o3 + LANES])
    krs = _dot(hb, w_ref[:, o3 + LANES:o3 + 2 * LANES])
    okr_ref[0] = (kr * ck_ref[...] + krs * sk_ref[...]).astype(BF16)
    cqb = _rms(cq, qn_ref[...]).astype(BF16)
    oqn_ref[0] = (_dot_nt(wq_ref[:nope], cqb) * scale).astype(BF16)
    qr = _dot_nt(wq_ref[nope:nope + rope], cqb)
    hr = MLA_ROPE // 2
    qrs = jnp.concatenate([qr[h * MLA_ROPE + o:h * MLA_ROPE + o + hr]
                           for h in range(MLA_HEADS) for o in (hr, 0)], axis=0)
    oqr_ref[0] = ((qr * cq_ref[...] + qrs * sq_ref[...]) * scale).astype(BF16)
    ckb = _rms(ckv, kvn_ref[...]).astype(BF16)
    okn_ref[0] = _dot(ckb, wkv_ref[:, :nope]).astype(BF16)
    vt = _dot_nt(wvt_ref[...], ckb).astype(BF16)
    tk = ov_ref.shape[3]
    for c in range(ov_ref.shape[1]):
        ov_ref[0, c] = vt[:, c * tk:(c + 1) * tk]


def _mla_attn_kernel(qn_ref, qr_ref, kn_ref, kr_ref, v_ref, z_ref, x_ref, wo_ref, g_ref, out_ref,
                     qs_ref, acc_ref, m_ref, o_ref):
    tq = qn_ref.shape[2]
    tk = v_ref.shape[3]
    npairs = MLA_HEADS // 2
    i = pl.program_id(1)
    krow = lax.broadcasted_iota(jnp.int32, (tk, 2 * tq), 0)
    qcol = lax.broadcasted_iota(jnp.int32, (tk, 2 * tq), 1)
    causal = krow <= jnp.where(qcol >= tq, qcol - tq, qcol)

    zn = jnp.zeros((MLA_NOPE, tq), BF16)
    zr = jnp.zeros((LANES - MLA_ROPE, tq), BF16)
    for p in range(npairs):
        qn = qn_ref[0, p * LANES:(p + 1) * LANES, :]
        r0 = 2 * p * MLA_ROPE
        c0 = jnp.concatenate([qn[:MLA_NOPE], zn, qr_ref[0, r0:r0 + MLA_ROPE, :], zr], axis=0)
        c1 = jnp.concatenate([zn, qn[MLA_NOPE:], qr_ref[0, r0 + MLA_ROPE:r0 + 2 * MLA_ROPE, :], zr], axis=0)
        qs_ref[p] = jnp.concatenate([c0, c1], axis=1)

    m_ref[...] = jnp.full(m_ref.shape, NEG_INF, F32)
    acc_ref[...] = jnp.zeros(acc_ref.shape, F32)
    ones = jnp.ones((acc_ref.shape[1] - LANES, tk), BF16)

    def kv_steps(blocks):
        units = [(j, masked, p) for j, masked in blocks for p in range(npairs)]

        def scores(j, p):
            ks = pl.multiple_of(j * tk, tk)
            kc = jnp.concatenate([kn_ref[0, pl.ds(ks, tk), p * LANES:(p + 1) * LANES],
                                  kr_ref[0, pl.ds(ks, tk), :]], axis=1)
            return [_dot(kc, qs_ref[p, :, c * 2 * LANES:(c + 1) * 2 * LANES]) for c in range(tq // LANES)]

        pending = [scores(j, p) for j, _, p in units[:MLA_LOOKAHEAD]]
        late = []

        def flush():
            jj, pp, alpha, pr = late.pop(0)
            vones = jnp.concatenate([v_ref[0, jj, pp * LANES:(pp + 1) * LANES, :], ones], axis=0)
            acc_ref[pp] = alpha * acc_ref[pp] + _dot(vones, pr)

        for idx, (j, masked, p) in enumerate(units):
            s = pending.pop(0)
            if idx + MLA_LOOKAHEAD < len(units):
                nxt = units[idx + MLA_LOOKAHEAD]
                pending.append(scores(nxt[0], nxt[2]))
            probs, alphas = [], []
            for c in range(2 * tq // LANES):
                sc = s[c // 2][:, (c % 2) * LANES:(c % 2 + 1) * LANES]
                if masked:
                    sc = jnp.where(causal[:, c * LANES:(c + 1) * LANES], sc, NEG_INF)
                m_prev = m_ref[p, :, c * LANES:(c + 1) * LANES]
                m_new = jnp.maximum(m_prev, jnp.max(sc, axis=0, keepdims=True))
                alphas.append(jnp.exp2(m_prev - m_new))
                probs.append(jnp.exp2(sc - m_new).astype(BF16))
                m_ref[p, :, c * LANES:(c + 1) * LANES] = m_new
            if len(late) == MLA_PV_DELAY:
                flush()
            late.append((j, p, jnp.concatenate(alphas, axis=1), jnp.concatenate(probs, axis=1)))
        while late:
            flush()

    def body(jj, c):
        kv_steps([(2 * jj, False), (2 * jj + 1, False)])
        return c

    lax.fori_loop(0, i // 2, body, 0)

    @pl.when(i % 2 == 1)
    def _():
        kv_steps([(i - 1, False), (i, True)])

    @pl.when(i % 2 == 0)
    def _():
        kv_steps([(i, True)])
    for p in range(npairs):
        a = acc_ref[p]
        a = a[:LANES] * (1.0 / a[LANES:LANES + 1])
        ot = jnp.concatenate([a[:MLA_V, :tq], a[MLA_V:, tq:]], axis=0)
        o_ref[:, p * LANES:(p + 1) * LANES] = ot.T
    gated = o_ref[...] * jax.nn.silu(z_ref[0])
    r = _dot(gated.astype(BF16), wo_ref[...].astype(BF16))
    out_ref[0] = x_ref[0] + _rms(r, g_ref[...])


def _mla_layer(x, pre_g, post_g, w_in, q_norm, kv_norm, w_uq, w_ukv, w_out):
    bsz, L, d = x.shape
    H = MLA_HEADS
    dq = MLA_NOPE + MLA_ROPE
    nope = H * MLA_NOPE
    rope = H * MLA_ROPE
    vw = H * MLA_V
    half = MLA_ROPE // 2
    o_kr = MLA_Q_RANK + MLA_KV_RANK
    o_z = o_kr + MLA_ROPE
    wb = w_in.astype(BF16)
    w_kr = wb[:, o_kr:o_z]
    w_krs = jnp.concatenate([w_kr[:, half:], w_kr[:, :half]], axis=1)
    reps = LANES // MLA_ROPE
    w1 = jnp.concatenate([wb[:, :o_kr], wb[:, o_z:]] + [w_kr] * reps + [w_krs] * reps, axis=1)
    wq3 = w_uq.astype(BF16).reshape(MLA_Q_RANK, H, dq)
    wqt = jnp.concatenate([wq3[:, :, :MLA_NOPE].reshape(MLA_Q_RANK, nope),
                           wq3[:, :, MLA_NOPE:].reshape(MLA_Q_RANK, rope)], axis=1).T
    wkv3 = w_ukv.astype(BF16).reshape(MLA_KV_RANK, H, MLA_NOPE + MLA_V)
    wkn = wkv3[:, :, :MLA_NOPE].reshape(MLA_KV_RANK, nope)
    wvt = wkv3[:, :, MLA_NOPE:].reshape(MLA_KV_RANK, vw).T
    inv = ROPE_BASE ** (-jnp.arange(0, MLA_ROPE, 2, dtype=F32) / MLA_ROPE)
    ang = jnp.arange(L, dtype=F32)[:, None] * inv[None, :]
    cos, sin = jnp.cos(ang), jnp.sin(ang)
    cos32 = jnp.concatenate([cos, cos], axis=1)
    sin32 = jnp.concatenate([-sin, sin], axis=1)
    cos_k, sin_k = jnp.tile(cos32, (1, LANES // MLA_ROPE)), jnp.tile(sin32, (1, LANES // MLA_ROPE))
    cos_q, sin_q = jnp.tile(cos32, (1, H)).T, jnp.tile(sin32, (1, H)).T

    tm = 1024
    tk = MLA_TK
    tok = lambda w_: pl.BlockSpec((1, tm, w_), lambda b, i: (b, i, 0))
    tokt = lambda w_: pl.BlockSpec((1, w_, tm), lambda b, i: (b, 0, i))
    scale = dq ** -0.5 * math.log2(math.e)
    qn, qr, kn, kr, v, z = pl.pallas_call(
        functools.partial(_mla_pre_kernel, scale=scale),
        out_shape=[jax.ShapeDtypeStruct((bsz, nope, L), BF16),
                   jax.ShapeDtypeStruct((bsz, rope, L), BF16),
                   jax.ShapeDtypeStruct((bsz, L, nope), BF16),
                   jax.ShapeDtypeStruct((bsz, L, LANES), BF16),
                   jax.ShapeDtypeStruct((bsz, L // tk, vw, tk), BF16),
                   jax.ShapeDtypeStruct((bsz, L, vw), F32)],
        grid=(bsz, L // tm),
        in_specs=[tok(d), _full((1, d)), _full(w1.shape), _full((1, MLA_Q_RANK)), _full((1, MLA_KV_RANK)),
                  _full(wqt.shape), _full(wkn.shape), _full(wvt.shape),
                  pl.BlockSpec((rope, tm), lambda b, i: (0, i)), pl.BlockSpec((rope, tm), lambda b, i: (0, i)),
                  pl.BlockSpec((tm, LANES), lambda b, i: (i, 0)), pl.BlockSpec((tm, LANES), lambda b, i: (i, 0))],
        out_specs=[tokt(nope), tokt(rope), tok(nope), tok(LANES),
                   pl.BlockSpec((1, tm // tk, vw, tk), lambda b, i: (b, i, 0, 0)), tok(vw)],
        compiler_params=_cparams(("parallel", "parallel")),
        name="mla_pre",
    )(x, pre_g.reshape(1, d), w1, q_norm.reshape(1, -1), kv_norm.reshape(1, -1), wqt, wkn, wvt,
      cos_q, sin_q, cos_k, sin_k)

    tq = MLA_TQ
    npairs = H // 2
    qspec = lambda w_: pl.BlockSpec((1, w_, tq), lambda b, i: (b, 0, i))
    kspec = lambda w_: pl.BlockSpec((1, L, w_), lambda b, i: (b, 0, 0))
    rowspec = lambda w_: pl.BlockSpec((1, tq, w_), lambda b, i: (b, i, 0))
    return pl.pallas_call(
        _mla_attn_kernel,
        out_shape=jax.ShapeDtypeStruct(x.shape, x.dtype),
        grid=(bsz, L // tq),
        in_specs=[qspec(nope), qspec(rope), kspec(nope), kspec(LANES),
                  pl.BlockSpec((1, L // tk, vw, tk), lambda b, i: (b, 0, 0, 0)),
                  rowspec(vw), rowspec(d), _full(w_out.shape), _full((1, d))],
        out_specs=rowspec(d),
        scratch_shapes=[pltpu.VMEM((npairs, 2 * LANES, 2 * tq), BF16),
                        pltpu.VMEM((npairs, LANES + 16, 2 * tq), F32),
                        pltpu.VMEM((npairs, 1, 2 * tq), F32),
                        pltpu.VMEM((tq, vw), F32)],
        compiler_params=_cparams(("parallel", "arbitrary")),
        name="mla_attn",
    )(qn, qr, kn, kr, v, z, x, w_out, post_g.reshape(1, d))


def _sgu_kernel(x_ref, g_ref, w_ref, lng_ref, lnb_ref, ws_ref, bs_ref, wo_ref, pg_ref, out_ref, s_ref):
    width = wo_ref.shape[0]
    tm = x_ref.shape[1]
    lane = lax.broadcasted_iota(jnp.int32, (1, LANES), 1)
    lo = lane < HALF
    x = x_ref[0]
    hb = _rms(x, g_ref[...]).astype(BF16)
    v = jax.nn.gelu(_dot(hb, w_ref[:, width:2 * width].astype(BF16)))
    mu = jnp.mean(v, axis=-1, keepdims=True)
    vc = v - mu
    var = jnp.mean(vc * vc, axis=-1, keepdims=True)
    vb = (vc * lax.rsqrt(var + EPS) * lng_ref[...] + lnb_ref[...]).astype(BF16)
    group = SGU_STACK
    for c0 in range(0, tm // SGU_CHUNK, group):
        for jj in range(width // LANES):
            blk = jnp.concatenate([vb[c * SGU_CHUNK:(c + 1) * SGU_CHUNK, jj * LANES:(jj + 1) * LANES]
                                   for c in range(c0, c0 + group)], axis=1)
            r = _dot(ws_ref[jj], blk)
            for k in range(group):
                c = c0 + k
                s_ref[c * SGU_CHUNK:(c + 1) * SGU_CHUNK, jj * LANES:(jj + 1) * LANES] = (
                    jnp.where(lo, r[:SGU_CHUNK, k * LANES:(k + 1) * LANES],
                              r[SGU_CHUNK:, k * LANES:(k + 1) * LANES]) + bs_ref[jj])
    u = jax.nn.gelu(_dot(hb, w_ref[:, :width].astype(BF16)))
    z = _dot(hb, w_ref[:, 2 * width:].astype(BF16))
    o = u * s_ref[...] * jax.nn.silu(z)
    r = _dot(o.astype(BF16), wo_ref[...].astype(BF16))
    out_ref[0] = x + _rms(r, pg_ref[...])


def _sgu_layer(x, pre_g, post_g, w_in, ln_g, ln_b, w_s, b_s, w_out):
    bsz, L, d = x.shape
    width = w_out.shape[0]
    T = SGU_CHUNK
    gd = width // SGU_GROUPS
    tril = jnp.tril(jnp.ones((T, T), dtype=bool))
    ws = jnp.where(tril[None], w_s, 0.0).reshape(SGU_GROUPS // 2, 2 * T, T).astype(BF16)
    bs = jnp.repeat(b_s.astype(F32).T, gd, axis=1)
    bs = bs.reshape(T, width // LANES, LANES).transpose(1, 0, 2)
    tm = 1024
    return pl.pallas_call(
        _sgu_kernel,
        out_shape=jax.ShapeDtypeStruct(x.shape, x.dtype),
        grid=(bsz, L // tm),
        in_specs=[pl.BlockSpec((1, tm, d), lambda b, i: (b, i, 0)),
                  _full((1, d)), _full(w_in.shape), _full((1, width)), _full((1, width)),
                  _full(ws.shape), _full(bs.shape), _full(w_out.shape), _full((1, d))],
        out_specs=pl.BlockSpec((1, tm, d), lambda b, i: (b, i, 0)),
        scratch_shapes=[pltpu.VMEM((tm, width), F32)],
        compiler_params=_cparams(("parallel", "parallel")),
        name="sgu",
    )(x, pre_g.reshape(1, d), w_in, ln_g.reshape(1, width), ln_b.reshape(1, width),
      ws, bs, w_out, post_g.reshape(1, d))


def kernel(x, pre_norm, post_norm, rel_bias, a_w_in, a_lam_re, a_lam_im, a_log_dt, a_b_re, a_b_im, a_c_re, a_c_im, a_d, a_w_glu, a_b_glu, a_w_out, b_w_in, b_sinks, b_w_out, c_w_in, c_q_norm, c_kv_norm, c_w_uq, c_w_ukv, c_w_out, d_w_in, d_ln_g, d_ln_b, d_w_s, d_b_s, d_w_out):
    depth = pre_norm.shape[0]
    for i in range(depth):
        kind, j = i % 4, i // 4
        if kind == 0:
            x = _s5_layer(x, pre_norm[i], post_norm[i], a_w_in[j], a_lam_re[j], a_lam_im[j], a_log_dt[j],
                          a_b_re[j], a_b_im[j], a_c_re[j], a_c_im[j], a_d[j], a_w_glu[j], a_b_glu[j],
                          a_w_out[j])
        elif kind == 1:
            x = _swa_layer(x, pre_norm[i], post_norm[i], b_w_in[j], b_sinks[j], b_w_out[j], rel_bias)
        elif kind == 2:
            x = _mla_layer(x, pre_norm[i], post_norm[i], c_w_in[j], c_q_norm[j], c_kv_norm[j], c_w_uq[j],
                           c_w_ukv[j], c_w_out[j])
        else:
            x = _sgu_layer(x, pre_norm[i], post_norm[i], d_w_in[j], d_ln_g[j], d_ln_b[j], d_w_s[j],
                           d_b_s[j], d_w_out[j])
    return x
```

```python
import functools
import math

import jax
import jax.numpy as jnp
import numpy as np
from jax import lax
from jax.experimental import pallas as pl
from jax.experimental.pallas import tpu as pltpu

F32 = jnp.float32
BF16 = jnp.bfloat16

D_MODEL = 1024
EPS = 1e-6
NEG_INF = -1e30
LANES = 128
HALF = LANES // 2

SSM_GROUP = 16
SSM_STATE = 64
S5_CH_BLOCK = LANES
S5_GROUPS_PER_BLOCK = S5_CH_BLOCK // SSM_GROUP
S5_STATE_BLOCK = S5_GROUPS_PER_BLOCK * SSM_STATE
S5_T = 64

HEAD_DIM = 64
SWA_HEADS = 16
SWA_KV_HEADS = 2
SWA_GROUP = SWA_HEADS // SWA_KV_HEADS
WINDOW = 128
SWA_WINDOWS_PER_STEP = 4
SWA_LOOKAHEAD = 2
SWA_UNIT_HEADS = 8
SWA_PV_DELAY = 1
REL_BUCKETS = 32
REL_MAX_DIST = 128

MLA_HEADS = 16
MLA_NOPE = 64
MLA_ROPE = 32
MLA_V = 64
MLA_KV_RANK = 256
MLA_Q_RANK = 768
ROPE_BASE = 10000.0
MLA_TQ = 256
MLA_TK = 256
MLA_LOOKAHEAD = 6
MLA_PV_DELAY = 2

SGU_CHUNK = 128
SGU_GROUPS = 16
SGU_STACK = 4

VMEM_LIMIT = 56 * 1024 * 1024


def _cparams(sem):
    return pltpu.CompilerParams(dimension_semantics=sem, vmem_limit_bytes=VMEM_LIMIT)


def _rms(x, g):
    return x * lax.rsqrt(jnp.mean(x * x, axis=-1, keepdims=True) + EPS) * g


def _dot(a, b):
    return jnp.dot(a, b, preferred_element_type=F32)


def _dot_nt(a, b):
    return lax.dot_general(a, b, (((1,), (1,)), ((), ())), preferred_element_type=F32)


def _full(shape):
    n = len(shape)
    return pl.BlockSpec(shape, lambda *_: (0,) * n, pipeline_mode=pl.Buffered(1))


def _s5_kernel(x_ref, g_ref, w_ref, perm_ref, permt_ref, bb_ref, cc_ref, ar_ref, ai_ref, d_ref,
               wg_ref, bg_ref, wo_ref, pg_ref, out_ref, u_ref, z_ref, y_ref, s_ref, carry_ref, *, tt):
    bsz = x_ref.shape[0]
    width = wg_ref.shape[0]
    rows = bsz * tt
    nblk = bb_ref.shape[0]
    sb = S5_STATE_BLOCK

    @pl.when(pl.program_id(0) == 0)
    def _():
        carry_ref[...] = jnp.zeros_like(carry_ref)

    x = x_ref[...].reshape(rows, x_ref.shape[2])
    hb = _rms(x, g_ref[...]).astype(BF16)
    hb = _dot(perm_ref[...], hb).astype(BF16)
    u_ref[...] = _dot(hb, w_ref[:, :width].astype(BF16))
    z_ref[...] = _dot(hb, w_ref[:, width:].astype(BF16))

    nbuf = s_ref.shape[0]

    def project_in(i):
        s_ref[i % nbuf] = _dot(u_ref[:, i * LANES:(i + 1) * LANES].astype(BF16), bb_ref[i])

    def project_out(i):
        ub = u_ref[:, i * LANES:(i + 1) * LANES]
        y = _dot(s_ref[i % nbuf].astype(BF16), cc_ref[i]) + d_ref[:, i * LANES:(i + 1) * LANES] * ub
        y_ref[:, i * LANES:(i + 1) * LANES] = jax.nn.gelu(y)

    project_in(0)
    for i in range(nblk):
        if i + 1 < nblk:
            project_in(i + 1)
        buf = s_ref.at[i % nbuf]
        ar = ar_ref[i]
        ai = ai_ref[i]
        sr = carry_ref[i, :, 0:sb]
        si = carry_ref[i, :, sb:2 * sb]
        for t in range(tt):
            r0 = t * bsz
            nr = ar * sr - ai * si + buf[r0:r0 + bsz, 0:sb]
            ni = ar * si + ai * sr + buf[r0:r0 + bsz, sb:2 * sb]
            buf[r0:r0 + bsz, 0:sb] = nr
            buf[r0:r0 + bsz, sb:2 * sb] = ni
            sr, si = nr, ni
        carry_ref[i, :, 0:sb] = sr
        carry_ref[i, :, sb:2 * sb] = si
        project_out(i)

    y = y_ref[...]
    gate = jax.nn.sigmoid(_dot(y.astype(BF16), wg_ref[...].astype(BF16)) + bg_ref[...])
    o = y * gate * jax.nn.silu(z_ref[...])
    ob = _dot(permt_ref[...], o.astype(BF16)).astype(BF16)
    r = _dot(ob, wo_ref[...].astype(BF16))
    out_ref[...] = (x + _rms(r, pg_ref[...])).reshape(out_ref.shape)


def _s5_discretize(lam_re, lam_im, log_dt, b_re, b_im):
    dt = jnp.exp(log_dt)[:, None]
    mag = jnp.exp(lam_re * dt)
    ab_re = mag * jnp.cos(lam_im * dt)
    ab_im = mag * jnp.sin(lam_im * dt)
    den = lam_re * lam_re + lam_im * lam_im
    nr = ab_re - 1.0
    f_re = (nr * lam_re + ab_im * lam_im) / den
    f_im = (ab_im * lam_re - nr * lam_im) / den
    bb_re = f_re[..., None] * b_re - f_im[..., None] * b_im
    bb_im = f_re[..., None] * b_im + f_im[..., None] * b_re
    return ab_re, ab_im, bb_re, bb_im


def _s5_layer(x, pre_g, post_g, w_in, lam_re, lam_im, log_dt, b_re, b_im, c_re, c_im, d_skip,
              w_glu, b_glu, w_out):
    bsz, L, d = x.shape
    width = w_in.shape[1] // 2
    nblk = width // S5_CH_BLOCK
    gpb = S5_GROUPS_PER_BLOCK
    tt = S5_T
    rows = bsz * tt

    src = (np.arange(rows) % bsz) * tt + np.arange(rows) // bsz
    perm_np = np.zeros((rows, rows), np.float32)
    perm_np[np.arange(rows), src] = 1.0
    perm = jnp.asarray(perm_np, BF16)
    perm_t = jnp.asarray(perm_np.T, BF16)

    ab_re, ab_im, bb_re, bb_im = _s5_discretize(lam_re, lam_im, log_dt, b_re, b_im)
    eye = jnp.eye(gpb, dtype=F32)

    def pack_b(bb):
        t = bb.reshape(nblk, gpb, SSM_STATE, SSM_GROUP)
        return jnp.einsum('igph,gk->ikhgp', t, eye).reshape(nblk, S5_CH_BLOCK, S5_STATE_BLOCK)

    def pack_c(cc):
        t = cc.reshape(nblk, gpb, SSM_GROUP, SSM_STATE)
        return jnp.einsum('ighp,gk->igpkh', t, eye).reshape(nblk, S5_STATE_BLOCK, S5_CH_BLOCK)

    bb = jnp.concatenate([pack_b(bb_re), pack_b(bb_im)], axis=2).astype(BF16)
    cc = jnp.concatenate([pack_c(c_re), -pack_c(c_im)], axis=1).astype(BF16)
    ar = jnp.broadcast_to(ab_re.reshape(nblk, 1, S5_STATE_BLOCK), (nblk, bsz, S5_STATE_BLOCK))
    ai = jnp.broadcast_to(ab_im.reshape(nblk, 1, S5_STATE_BLOCK), (nblk, bsz, S5_STATE_BLOCK))

    xspec = pl.BlockSpec((bsz, tt, d), lambda i: (0, i, 0))
    return pl.pallas_call(
        functools.partial(_s5_kernel, tt=tt),
        out_shape=jax.ShapeDtypeStruct(x.shape, x.dtype),
        grid=(L // tt,),
        in_specs=[xspec, _full((1, d)), _full(w_in.shape), _full(perm.shape), _full(perm_t.shape),
                  _full(bb.shape), _full(cc.shape), _full(ar.shape), _full(ai.shape), _full((1, width)),
                  _full(w_glu.shape), _full((1, width)), _full(w_out.shape), _full((1, d))],
        out_specs=xspec,
        scratch_shapes=[pltpu.VMEM((rows, width), F32),
                        pltpu.VMEM((rows, width), F32),
                        pltpu.VMEM((rows, width), F32),
                        pltpu.VMEM((2, rows, 2 * S5_STATE_BLOCK), F32),
                        pltpu.VMEM((nblk, bsz, 2 * S5_STATE_BLOCK), F32)],
        compiler_params=_cparams(("arbitrary",)),
        name="s5_layer",
    )(x, pre_g.reshape(1, d), w_in, perm, perm_t, bb, cc, ar, ai, d_skip.reshape(1, width),
      w_glu, b_glu.reshape(1, width), w_out, post_g.reshape(1, d))


def _swa_bias(rel_bias):
    W = WINDOW
    n = 4 * W
    dist = jnp.arange(n) - W
    valid = jnp.logical_and(dist >= 0, dist < W)
    dpos = jnp.maximum(dist, 0)
    max_exact = REL_BUCKETS // 2
    dist_f = jnp.maximum(dpos, 1).astype(F32)
    large = max_exact + (jnp.log(dist_f / max_exact) / math.log(REL_MAX_DIST / max_exact)
                         * (REL_BUCKETS - max_exact)).astype(jnp.int32)
    large = jnp.minimum(large, REL_BUCKETS - 1)
    bucket = jnp.where(dpos < max_exact, dpos, large)
    return jnp.where(valid[:, None], rel_bias[bucket].astype(F32), NEG_INF).T


def _swa_kernel(x_ref, pg_ref, wqt_ref, wk_ref, wvt_ref, wz_ref, bvec_ref, sink_ref, wo_ref, g_ref,
                out_ref, ot_ref, bias_ref, kprev_ref, vtprev_ref, *, scale):
    W = WINDOW
    nwin = x_ref.shape[1] // W
    step = pl.program_id(1)

    @pl.when(step == 0)
    def _():
        kprev_ref[...] = jnp.zeros_like(kprev_ref)
        vtprev_ref[...] = jnp.zeros_like(vtprev_ref)

    x = x_ref[0]
    hb = _rms(x, pg_ref[...]).astype(BF16)
    qt = (_dot_nt(wqt_ref[...], hb) * scale).astype(BF16)
    k = _dot(hb, wk_ref[...]).astype(BF16)
    vt = _dot_nt(wvt_ref[...], hb).astype(BF16)
    z = _dot(hb, wz_ref[...])

    @pl.when(jnp.logical_and(pl.program_id(0) == 0, step == 0))
    def _():
        no_prev = lax.broadcasted_iota(jnp.int32, (2 * W, W), 0) < W
        for hd in range(SWA_HEADS):
            base = jnp.broadcast_to(bvec_ref[hd:hd + 1, :], (2 * W, bvec_ref.shape[1]))
            toep = pltpu.roll(base, 0, 1, stride=1, stride_axis=0)[:, 2 * W:3 * W]
            h, g = divmod(hd, SWA_GROUP)
            bias_ref[0, h, :, g * W:(g + 1) * W] = toep
            bias_ref[1, h, :, g * W:(g + 1) * W] = jnp.where(no_prev, NEG_INF, toep)
    kall = jnp.concatenate([kprev_ref[...], k], axis=0)
    vtall = jnp.concatenate([vtprev_ref[...], vt], axis=1)
    kprev_ref[...] = k[(nwin - 1) * W:]
    vtprev_ref[...] = vt[:, (nwin - 1) * W:]
    nsub = SWA_UNIT_HEADS
    zq = jnp.zeros((HEAD_DIM, nsub * W), BF16)
    ones = jnp.ones((16, 2 * W), BF16)
    units = [(w, h, c) for w in range(nwin) for h in range(SWA_KV_HEADS) for c in range(SWA_GROUP // nsub)]

    def scores(w, h, c):
        hd0 = h * SWA_GROUP + c * nsub
        qh = jnp.concatenate([qt[(hd0 + g) * HEAD_DIM:(hd0 + g + 1) * HEAD_DIM, w * W:(w + 1) * W]
                              for g in range(nsub)], axis=1)
        qz = jnp.concatenate([qh, zq] if h == 0 else [zq, qh], axis=0)
        return _dot(kall[w * W:(w + 2) * W], qz)

    pending = [scores(*u) for u in units[:SWA_LOOKAHEAD]]
    late = []

    def flush():
        (w, h, c), p, tail = late.pop(0)
        vones = jnp.concatenate([vtall[h * HEAD_DIM:(h + 1) * HEAD_DIM, w * W:(w + 2) * W], ones], axis=0)
        o = _dot(vones, p)
        oh = o[:HEAD_DIM] * (1.0 / (o[HEAD_DIM:HEAD_DIM + 1] + tail))
        for g in range(nsub):
            hd = h * SWA_GROUP + c * nsub + g
            ot_ref[hd * HEAD_DIM:(hd + 1) * HEAD_DIM, w * W:(w + 1) * W] = oh[:, g * W:(g + 1) * W]

    for idx, (w, h, c) in enumerate(units):
        raw = pending.pop(0)
        if idx + SWA_LOOKAHEAD < len(units):
            pending.append(scores(*units[idx + SWA_LOOKAHEAD]))
        cols = slice(c * nsub * W, (c + 1) * nsub * W)
        variant = (step == 0).astype(jnp.int32) if w == 0 else 0
        s = raw + bias_ref[variant, h, :, cols]
        sink = sink_ref[h, :, cols]
        m = jnp.maximum(jnp.max(s, axis=0, keepdims=True), sink)
        if len(late) == SWA_PV_DELAY:
            flush()
        late.append(((w, h, c), jnp.exp2(s - m).astype(BF16), jnp.exp2(sink - m)))
    while late:
        flush()
    gated = ot_ref[...].T * jax.nn.silu(z)
    r = _dot(gated.astype(BF16), wo_ref[...].astype(BF16))
    out_ref[0] = x + _rms(r, g_ref[...])


def _swa_layer(x, pre_g, post_g, w_in, sinks, w_out, rel_bias):
    bsz, L, d = x.shape
    width = SWA_HEADS * HEAD_DIM
    kvw = SWA_KV_HEADS * HEAD_DIM
    W = WINDOW
    log2e = math.log2(math.e)
    wb = w_in.astype(BF16)
    wqt = wb[:, :width].T
    wk = wb[:, width:width + kvw]
    wvt = wb[:, width + kvw:width + 2 * kvw].T
    wz = wb[:, width + 2 * kvw:]
    bvec = _swa_bias(rel_bias.astype(F32) * log2e)
    sink = jnp.repeat(sinks.astype(F32) * log2e, W).reshape(SWA_KV_HEADS, 1, SWA_GROUP * W)

    tq = SWA_WINDOWS_PER_STEP * W
    xspec = pl.BlockSpec((1, tq, d), lambda b, n: (b, n, 0))
    return pl.pallas_call(
        functools.partial(_swa_kernel, scale=HEAD_DIM ** -0.5 * log2e),
        out_shape=jax.ShapeDtypeStruct(x.shape, x.dtype),
        grid=(bsz, L // tq),
        in_specs=[xspec, _full((1, d)), _full(wqt.shape), _full(wk.shape), _full(wvt.shape), _full(wz.shape),
                  _full(bvec.shape), _full(sink.shape), _full(w_out.shape), _full((1, d))],
        out_specs=xspec,
        scratch_shapes=[pltpu.VMEM((width, tq), F32),
                        pltpu.VMEM((2, SWA_KV_HEADS, 2 * W, SWA_GROUP * W), F32),
                        pltpu.VMEM((W, kvw), BF16),
                        pltpu.VMEM((kvw, W), BF16)],
        compiler_params=_cparams(("arbitrary", "arbitrary")),
        name="swa_layer",
    )(x, pre_g.reshape(1, d), wqt, wk, wvt, wz, bvec, sink, w_out, post_g.reshape(1, d))


def _mla_pre_kernel(x_ref, g_ref, w_ref, qn_ref, kvn_ref, wq_ref, wkv_ref, wvt_ref, cq_ref, sq_ref, ck_ref, sk_ref,
                    oqn_ref, oqr_ref, okn_ref, okr_ref, ov_ref, oz_ref, *, scale):
    nope = MLA_HEADS * MLA_NOPE
    rope = MLA_HEADS * MLA_ROPE
    vw = MLA_HEADS * MLA_V
    hb = _rms(x_ref[0], g_ref[...]).astype(BF16)
    o1 = MLA_Q_RANK
    o2 = o1 + MLA_KV_RANK
    o3 = o2 + vw
    cq = _dot(hb, w_ref[:, :o1])
    ckv = _dot(hb, w_ref[:, o1:o2])
    oz_ref[0] = _dot(hb, w_ref[:, o2:o3])
    kr = _dot(hb, w_ref[:, o3:o3 + LANES])
    krs = _dot(hb, w_ref[:, o3 + LANES:o3 + 2 * LANES])
    okr_ref[0] = (kr * ck_ref[...] + krs * sk_ref[...]).astype(BF16)
    cqb = _rms(cq, qn_ref[...]).astype(BF16)
    oqn_ref[0] = (_dot_nt(wq_ref[:nope], cqb) * scale).astype(BF16)
    qr = _dot_nt(wq_ref[nope:nope + rope], cqb)
    hr = MLA_ROPE // 2
    qrs = jnp.concatenate([qr[h * MLA_ROPE + o:h * MLA_ROPE + o + hr]
                           for h in range(MLA_HEADS) for o in (hr, 0)], axis=0)
    oqr_ref[0] = ((qr * cq_ref[...] + qrs * sq_ref[...]) * scale).astype(BF16)
    ckb = _rms(ckv, kvn_ref[...]).astype(BF16)
    okn_ref[0] = _dot(ckb, wkv_ref[:, :nope]).astype(BF16)
    vt = _dot_nt(wvt_ref[...], ckb).astype(BF16)
    tk = ov_ref.shape[3]
    for c in range(ov_ref.shape[1]):
        ov_ref[0, c] = vt[:, c * tk:(c + 1) * tk]


def _mla_attn_kernel(qn_ref, qr_ref, kn_ref, kr_ref, v_ref, z_ref, x_ref, wo_ref, g_ref, out_ref,
                     qs_ref, acc_ref, m_ref, o_ref):
    tq = qn_ref.shape[2]
    tk = v_ref.shape[3]
    npairs = MLA_HEADS // 2
    i = pl.program_id(1)
    krow = lax.broadcasted_iota(jnp.int32, (tk, 2 * tq), 0)
    qcol = lax.broadcasted_iota(jnp.int32, (tk, 2 * tq), 1)
    causal = krow <= jnp.where(qcol >= tq, qcol - tq, qcol)

    zn = jnp.zeros((MLA_NOPE, tq), BF16)
    zr = jnp.zeros((LANES - MLA_ROPE, tq), BF16)
    for p in range(npairs):
        qn = qn_ref[0, p * LANES:(p + 1) * LANES, :]
        r0 = 2 * p * MLA_ROPE
        c0 = jnp.concatenate([qn[:MLA_NOPE], zn, qr_ref[0, r0:r0 + MLA_ROPE, :], zr], axis=0)
        c1 = jnp.concatenate([zn, qn[MLA_NOPE:], qr_ref[0, r0 + MLA_ROPE:r0 + 2 * MLA_ROPE, :], zr], axis=0)
        qs_ref[p] = jnp.concatenate([c0, c1], axis=1)

    m_ref[...] = jnp.full(m_ref.shape, NEG_INF, F32)
    acc_ref[...] = jnp.zeros(acc_ref.shape, F32)
    ones = jnp.ones((acc_ref.shape[1] - LANES, tk), BF16)

    def kv_steps(blocks):
        units = [(j, masked, p) for j, masked in blocks for p in range(npairs)]

        def scores(j, p):
            ks = pl.multiple_of(j * tk, tk)
            kc = jnp.concatenate([kn_ref[0, pl.ds(ks, tk), p * LANES:(p + 1) * LANES],
                                  kr_ref[0, pl.ds(ks, tk), :]], axis=1)
            return [_dot(kc, qs_ref[p, :, c * 2 * LANES:(c + 1) * 2 * LANES]) for c in range(tq // LANES)]

        pending = [scores(j, p) for j, _, p in units[:MLA_LOOKAHEAD]]
        late = []

        def flush():
            jj, pp, alpha, pr = late.pop(0)
            vones = jnp.concatenate([v_ref[0, jj, pp * LANES:(pp + 1) * LANES, :], ones], axis=0)
            acc_ref[pp] = alpha * acc_ref[pp] + _dot(vones, pr)

        for idx, (j, masked, p) in enumerate(units):
            s = pending.pop(0)
            if idx + MLA_LOOKAHEAD < len(units):
                nxt = units[idx + MLA_LOOKAHEAD]
                pending.append(scores(nxt[0], nxt[2]))
            probs, alphas = [], []
            for c in range(2 * tq // LANES):
                sc = s[c // 2][:, (c % 2) * LANES:(c % 2 + 1) * LANES]
                if masked:
                    sc = jnp.where(causal[:, c * LANES:(c + 1) * LANES], sc, NEG_INF)
                m_prev = m_ref[p, :, c * LANES:(c + 1) * LANES]
                m_new = jnp.maximum(m_prev, jnp.max(sc, axis=0, keepdims=True))
                alphas.append(jnp.exp2(m_prev - m_new))
                probs.append(jnp.exp2(sc - m_new).astype(BF16))
                m_ref[p, :, c * LANES:(c + 1) * LANES] = m_new
            if len(late) == MLA_PV_DELAY:
                flush()
            late.append((j, p, jnp.concatenate(alphas, axis=1), jnp.concatenate(probs, axis=1)))
        while late:
            flush()

    def body(jj, c):
        kv_steps([(2 * jj, False), (2 * jj + 1, False)])
        return c

    lax.fori_loop(0, i // 2, body, 0)

    @pl.when(i % 2 == 1)
    def _():
        kv_steps([(i - 1, False), (i, True)])

    @pl.when(i % 2 == 0)
    def _():
        kv_steps([(i, True)])
    for p in range(npairs):
        a = acc_ref[p]
        a = a[:LANES] * (1.0 / a[LANES:LANES + 1])
        ot = jnp.concatenate([a[:MLA_V, :tq], a[MLA_V:, tq:]], axis=0)
        o_ref[:, p * LANES:(p + 1) * LANES] = ot.T
    gated = o_ref[...] * jax.nn.silu(z_ref[0])
    r = _dot(gated.astype(BF16), wo_ref[...].astype(BF16))
    out_ref[0] = x_ref[0] + _rms(r, g_ref[...])


def _mla_layer(x, pre_g, post_g, w_in, q_norm, kv_norm, w_uq, w_ukv, w_out):
    bsz, L, d = x.shape
    H = MLA_HEADS
    dq = MLA_NOPE + MLA_ROPE
    nope = H * MLA_NOPE
    rope = H * MLA_ROPE
    vw = H * MLA_V
    half = MLA_ROPE // 2
    o_kr = MLA_Q_RANK + MLA_KV_RANK
    o_z = o_kr + MLA_ROPE
    wb = w_in.astype(BF16)
    w_kr = wb[:, o_kr:o_z]
    w_krs = jnp.concatenate([w_kr[:, half:], w_kr[:, :half]], axis=1)
    reps = LANES // MLA_ROPE
    w1 = jnp.concatenate([wb[:, :o_kr], wb[:, o_z:]] + [w_kr] * reps + [w_krs] * reps, axis=1)
    wq3 = w_uq.astype(BF16).reshape(MLA_Q_RANK, H, dq)
    wqt = jnp.concatenate([wq3[:, :, :MLA_NOPE].reshape(MLA_Q_RANK, nope),
                           wq3[:, :, MLA_NOPE:].reshape(MLA_Q_RANK, rope)], axis=1).T
    wkv3 = w_ukv.astype(BF16).reshape(MLA_KV_RANK, H, MLA_NOPE + MLA_V)
    wkn = wkv3[:, :, :MLA_NOPE].reshape(MLA_KV_RANK, nope)
    wvt = wkv3[:, :, MLA_NOPE:].reshape(MLA_KV_RANK, vw).T
    inv = ROPE_BASE ** (-jnp.arange(0, MLA_ROPE, 2, dtype=F32) / MLA_ROPE)
    ang = jnp.arange(L, dtype=F32)[:, None] * inv[None, :]
    cos, sin = jnp.cos(ang), jnp.sin(ang)
    cos32 = jnp.concatenate([cos, cos], axis=1)
    sin32 = jnp.concatenate([-sin, sin], axis=1)
    cos_k, sin_k = jnp.tile(cos32, (1, LANES // MLA_ROPE)), jnp.tile(sin32, (1, LANES // MLA_ROPE))
    cos_q, sin_q = jnp.tile(cos32, (1, H)).T, jnp.tile(sin32, (1, H)).T

    tm = 1024
    tk = MLA_TK
    tok = lambda w_: pl.BlockSpec((1, tm, w_), lambda b, i: (b, i, 0))
    tokt = lambda w_: pl.BlockSpec((1, w_, tm), lambda b, i: (b, 0, i))
    scale = dq ** -0.5 * math.log2(math.e)
    qn, qr, kn, kr, v, z = pl.pallas_call(
        functools.partial(_mla_pre_kernel, scale=scale),
        out_shape=[jax.ShapeDtypeStruct((bsz, nope, L), BF16),
                   jax.ShapeDtypeStruct((bsz, rope, L), BF16),
                   jax.ShapeDtypeStruct((bsz, L, nope), BF16),
                   jax.ShapeDtypeStruct((bsz, L, LANES), BF16),
                   jax.ShapeDtypeStruct((bsz, L // tk, vw, tk), BF16),
                   jax.ShapeDtypeStruct((bsz, L, vw), F32)],
        grid=(bsz, L // tm),
        in_specs=[tok(d), _full((1, d)), _full(w1.shape), _full((1, MLA_Q_RANK)), _full((1, MLA_KV_RANK)),
                  _full(wqt.shape), _full(wkn.shape), _full(wvt.shape),
                  pl.BlockSpec((rope, tm), lambda b, i: (0, i)), pl.BlockSpec((rope, tm), lambda b, i: (0, i)),
                  pl.BlockSpec((tm, LANES), lambda b, i: (i, 0)), pl.BlockSpec((tm, LANES), lambda b, i: (i, 0))],
        out_specs=[tokt(nope), tokt(rope), tok(nope), tok(LANES),
                   pl.BlockSpec((1, tm // tk, vw, tk), lambda b, i: (b, i, 0, 0)), tok(vw)],
        compiler_params=_cparams(("parallel", "parallel")),
        name="mla_pre",
    )(x, pre_g.reshape(1, d), w1, q_norm.reshape(1, -1), kv_norm.reshape(1, -1), wqt, wkn, wvt,
      cos_q, sin_q, cos_k, sin_k)

    tq = MLA_TQ
    npairs = H // 2
    qspec = lambda w_: pl.BlockSpec((1, w_, tq), lambda b, i: (b, 0, i))
    kspec = lambda w_: pl.BlockSpec((1, L, w_), lambda b, i: (b, 0, 0))
    rowspec = lambda w_: pl.BlockSpec((1, tq, w_), lambda b, i: (b, i, 0))
    return pl.pallas_call(
        _mla_attn_kernel,
        out_shape=jax.ShapeDtypeStruct(x.shape, x.dtype),
        grid=(bsz, L // tq),
        in_specs=[qspec(nope), qspec(rope), kspec(nope), kspec(LANES),
                  pl.BlockSpec((1, L // tk, vw, tk), lambda b, i: (b, 0, 0, 0)),
                  rowspec(vw), rowspec(d), _full(w_out.shape), _full((1, d))],
        out_specs=rowspec(d),
        scratch_shapes=[pltpu.VMEM((npairs, 2 * LANES, 2 * tq), BF16),
                        pltpu.VMEM((npairs, LANES + 16, 2 * tq), F32),
                        pltpu.VMEM((npairs, 1, 2 * tq), F32),
                        pltpu.VMEM((tq, vw), F32)],
        compiler_params=_cparams(("parallel", "arbitrary")),
        name="mla_attn",
    )(qn, qr, kn, kr, v, z, x, w_out, post_g.reshape(1, d))


def _sgu_kernel(x_ref, g_ref, w_ref, lng_ref, lnb_ref, ws_ref, bs_ref, wo_ref, pg_ref, out_ref, s_ref):
    width = wo_ref.shape[0]
    tm = x_ref.shape[1]
    lane = lax.broadcasted_iota(jnp.int32, (1, LANES), 1)
    lo = lane < HALF
    x = x_ref[0]
    hb = _rms(x, g_ref[...]).astype(BF16)
    v = jax.nn.gelu(_dot(hb, w_ref[:, width:2 * width].astype(BF16)))
    mu = jnp.mean(v, axis=-1, keepdims=True)
    vc = v - mu
    var = jnp.mean(vc * vc, axis=-1, keepdims=True)
    vb = (vc * lax.rsqrt(var + EPS) * lng_ref[...] + lnb_ref[...]).astype(BF16)
    group = SGU_STACK
    for c0 in range(0, tm // SGU_CHUNK, group):
        for jj in range(width // LANES):
            blk = jnp.concatenate([vb[c * SGU_CHUNK:(c + 1) * SGU_CHUNK, jj * LANES:(jj + 1) * LANES]
                                   for c in range(c0, c0 + group)], axis=1)
            r = _dot(ws_ref[jj], blk)
            for k in range(group):
                c = c0 + k
                s_ref[c * SGU_CHUNK:(c + 1) * SGU_CHUNK, jj * LANES:(jj + 1) * LANES] = (
                    jnp.where(lo, r[:SGU_CHUNK, k * LANES:(k + 1) * LANES],
                              r[SGU_CHUNK:, k * LANES:(k + 1) * LANES]) + bs_ref[jj])
    u = jax.nn.gelu(_dot(hb, w_ref[:, :width].astype(BF16)))
    z = _dot(hb, w_ref[:, 2 * width:].astype(BF16))
    o = u * s_ref[...] * jax.nn.silu(z)
    r = _dot(o.astype(BF16), wo_ref[...].astype(BF16))
    out_ref[0] = x + _rms(r, pg_ref[...])


def _sgu_layer(x, pre_g, post_g, w_in, ln_g, ln_b, w_s, b_s, w_out):
    bsz, L, d = x.shape
    width = w_out.shape[0]
    T = SGU_CHUNK
    gd = width // SGU_GROUPS
    tril = jnp.tril(jnp.ones((T, T), dtype=bool))
    ws = jnp.where(tril[None], w_s, 0.0).reshape(SGU_GROUPS // 2, 2 * T, T).astype(BF16)
    bs = jnp.repeat(b_s.astype(F32).T, gd, axis=1)
    bs = bs.reshape(T, width // LANES, LANES).transpose(1, 0, 2)
    tm = 1024
    return pl.pallas_call(
        _sgu_kernel,
        out_shape=jax.ShapeDtypeStruct(x.shape, x.dtype),
        grid=(bsz, L // tm),
        in_specs=[pl.BlockSpec((1, tm, d), lambda b, i: (b, i, 0)),
                  _full((1, d)), _full(w_in.shape), _full((1, width)), _full((1, width)),
                  _full(ws.shape), _full(bs.shape), _full(w_out.shape), _full((1, d))],
        out_specs=pl.BlockSpec((1, tm, d), lambda b, i: (b, i, 0)),
        scratch_shapes=[pltpu.VMEM((tm, width), F32)],
        compiler_params=_cparams(("parallel", "parallel")),
        name="sgu",
    )(x, pre_g.reshape(1, d), w_in, ln_g.reshape(1, width), ln_b.reshape(1, width),
      ws, bs, w_out, post_g.reshape(1, d))


def kernel(x, pre_norm, post_norm, rel_bias, a_w_in, a_lam_re, a_lam_im, a_log_dt, a_b_re, a_b_im, a_c_re, a_c_im, a_d, a_w_glu, a_b_glu, a_w_out, b_w_in, b_sinks, b_w_out, c_w_in, c_q_norm, c_kv_norm, c_w_uq, c_w_ukv, c_w_out, d_w_in, d_ln_g, d_ln_b, d_w_s, d_b_s, d_w_out):
    depth = pre_norm.shape[0]
    for i in range(depth):
        kind, j = i % 4, i // 4
        if kind == 0:
            x = _s5_layer(x, pre_norm[i], post_norm[i], a_w_in[j], a_lam_re[j], a_lam_im[j], a_log_dt[j],
                          a_b_re[j], a_b_im[j], a_c_re[j], a_c_im[j], a_d[j], a_w_glu[j], a_b_glu[j],
                          a_w_out[j])
        elif kind == 1:
            x = _swa_layer(x, pre_norm[i], post_norm[i], b_w_in[j], b_sinks[j], b_w_out[j], rel_bias)
        elif kind == 2:
            x = _mla_layer(x, pre_norm[i], post_norm[i], c_w_in[j], c_q_norm[j], c_kv_norm[j], c_w_uq[j],
                           c_w_ukv[j], c_w_out[j])
        else:
            x = _sgu_layer(x, pre_norm[i], post_norm[i], d_w_in[j], d_ln_g[j], d_ln_b[j], d_w_s[j],
                           d_b_s[j], d_w_out[j])
    return x
```

```python
import functools
import math

import jax
import jax.numpy as jnp
import numpy as np
from jax import lax
from jax.experimental import pallas as pl
from jax.experimental.pallas import tpu as pltpu

F32 = jnp.float32
BF16 = jnp.bfloat16

EPS = 1e-6
NEG_INF = -1e30
LANES = 128
HALF = LANES // 2

SSM_GROUP = 16
SSM_STATE = 64
S5_CH_BLOCK = LANES
S5_GROUPS_PER_BLOCK = S5_CH_BLOCK // SSM_GROUP
S5_STATE_BLOCK = S5_GROUPS_PER_BLOCK * SSM_STATE
S5_T = 64

HEAD_DIM = 64
SWA_HEADS = 16
SWA_KV_HEADS = 2
SWA_GROUP = SWA_HEADS // SWA_KV_HEADS
WINDOW = 128
SWA_WINDOWS_PER_STEP = 4
SWA_UNIT_HEADS = 8
SWA_LOOKAHEAD = 2
SWA_PV_DELAY = 1
REL_BUCKETS = 32
REL_MAX_DIST = 128

MLA_HEADS = 16
MLA_NOPE = 64
MLA_ROPE = 32
MLA_V = 64
MLA_KV_RANK = 256
MLA_Q_RANK = 768
ROPE_BASE = 10000.0
MLA_TQ = 256
MLA_TK = 256
MLA_LOOKAHEAD = 6
MLA_PV_DELAY = 2

SGU_CHUNK = 128
SGU_GROUPS = 16
SGU_STACK = 4

ROW_TILE = 1024
VMEM_LIMIT = 56 * 1024 * 1024


def _cparams(sem):
    return pltpu.CompilerParams(dimension_semantics=sem, vmem_limit_bytes=VMEM_LIMIT)


def _rms(x, g):
    return x * lax.rsqrt(jnp.mean(x * x, axis=-1, keepdims=True) + EPS) * g


def _dot(a, b):
    return jnp.dot(a, b, preferred_element_type=F32)


def _dot_nt(a, b):
    return lax.dot_general(a, b, (((1,), (1,)), ((), ())), preferred_element_type=F32)


def _full(shape):
    n = len(shape)
    return pl.BlockSpec(shape, lambda *_: (0,) * n, pipeline_mode=pl.Buffered(1))


def _s5_kernel(x_ref, g_ref, w_ref, perm_ref, permt_ref, bb_ref, cc_ref, ar_ref, ai_ref, d_ref,
               wg_ref, bg_ref, wo_ref, pg_ref, out_ref, u_ref, z_ref, y_ref, s_ref, carry_ref, *, tt):
    bsz = x_ref.shape[0]
    width = wg_ref.shape[0]
    rows = bsz * tt
    nblk = bb_ref.shape[0]
    sb = S5_STATE_BLOCK

    @pl.when(pl.program_id(0) == 0)
    def _():
        carry_ref[...] = jnp.zeros_like(carry_ref)

    x = x_ref[...].reshape(rows, x_ref.shape[2])
    hb = _rms(x, g_ref[...]).astype(BF16)
    hb = _dot(perm_ref[...], hb).astype(BF16)
    u_ref[...] = _dot(hb, w_ref[:, :width].astype(BF16))
    z_ref[...] = _dot(hb, w_ref[:, width:].astype(BF16))

    nbuf = s_ref.shape[0]

    def project_in(i):
        s_ref[i % nbuf] = _dot(u_ref[:, i * LANES:(i + 1) * LANES].astype(BF16), bb_ref[i])

    def project_out(i):
        ub = u_ref[:, i * LANES:(i + 1) * LANES]
        y = _dot(s_ref[i % nbuf].astype(BF16), cc_ref[i]) + d_ref[:, i * LANES:(i + 1) * LANES] * ub
        y_ref[:, i * LANES:(i + 1) * LANES] = jax.nn.gelu(y)

    project_in(0)
    for i in range(nblk):
        if i + 1 < nblk:
            project_in(i + 1)
        buf = s_ref.at[i % nbuf]
        ar = ar_ref[i]
        ai = ai_ref[i]
        sr = carry_ref[i, :, 0:sb]
        si = carry_ref[i, :, sb:2 * sb]
        for t in range(tt):
            r0 = t * bsz
            nr = ar * sr - ai * si + buf[r0:r0 + bsz, 0:sb]
            ni = ar * si + ai * sr + buf[r0:r0 + bsz, sb:2 * sb]
            buf[r0:r0 + bsz, 0:sb] = nr
            buf[r0:r0 + bsz, sb:2 * sb] = ni
            sr, si = nr, ni
        carry_ref[i, :, 0:sb] = sr
        carry_ref[i, :, sb:2 * sb] = si
        project_out(i)

    y = y_ref[...]
    gate = jax.nn.sigmoid(_dot(y.astype(BF16), wg_ref[...].astype(BF16)) + bg_ref[...])
    o = y * gate * jax.nn.silu(z_ref[...])
    ob = _dot(permt_ref[...], o.astype(BF16)).astype(BF16)
    r = _dot(ob, wo_ref[...].astype(BF16))
    out_ref[...] = (x + _rms(r, pg_ref[...])).reshape(out_ref.shape)


def _s5_discretize(lam_re, lam_im, log_dt, b_re, b_im):
    dt = jnp.exp(log_dt)[:, None]
    mag = jnp.exp(lam_re * dt)
    ab_re = mag * jnp.cos(lam_im * dt)
    ab_im = mag * jnp.sin(lam_im * dt)
    den = lam_re * lam_re + lam_im * lam_im
    nr = ab_re - 1.0
    f_re = (nr * lam_re + ab_im * lam_im) / den
    f_im = (ab_im * lam_re - nr * lam_im) / den
    bb_re = f_re[..., None] * b_re - f_im[..., None] * b_im
    bb_im = f_re[..., None] * b_im + f_im[..., None] * b_re
    return ab_re, ab_im, bb_re, bb_im


def _s5_layer(x, pre_g, post_g, w_in, lam_re, lam_im, log_dt, b_re, b_im, c_re, c_im, d_skip,
              w_glu, b_glu, w_out):
    bsz, L, d = x.shape
    width = w_in.shape[1] // 2
    nblk = width // S5_CH_BLOCK
    gpb = S5_GROUPS_PER_BLOCK
    tt = S5_T
    rows = bsz * tt

    src = (np.arange(rows) % bsz) * tt + np.arange(rows) // bsz
    perm_np = np.zeros((rows, rows), np.float32)
    perm_np[np.arange(rows), src] = 1.0
    perm = jnp.asarray(perm_np, BF16)
    perm_t = jnp.asarray(perm_np.T, BF16)

    ab_re, ab_im, bb_re, bb_im = _s5_discretize(lam_re, lam_im, log_dt, b_re, b_im)
    eye = jnp.eye(gpb, dtype=F32)

    def pack_b(bb):
        t = bb.reshape(nblk, gpb, SSM_STATE, SSM_GROUP)
        return jnp.einsum('igph,gk->ikhgp', t, eye).reshape(nblk, S5_CH_BLOCK, S5_STATE_BLOCK)

    def pack_c(cc):
        t = cc.reshape(nblk, gpb, SSM_GROUP, SSM_STATE)
        return jnp.einsum('ighp,gk->igpkh', t, eye).reshape(nblk, S5_STATE_BLOCK, S5_CH_BLOCK)

    bb = jnp.concatenate([pack_b(bb_re), pack_b(bb_im)], axis=2).astype(BF16)
    cc = jnp.concatenate([pack_c(c_re), -pack_c(c_im)], axis=1).astype(BF16)
    ar = jnp.broadcast_to(ab_re.reshape(nblk, 1, S5_STATE_BLOCK), (nblk, bsz, S5_STATE_BLOCK))
    ai = jnp.broadcast_to(ab_im.reshape(nblk, 1, S5_STATE_BLOCK), (nblk, bsz, S5_STATE_BLOCK))

    xspec = pl.BlockSpec((bsz, tt, d), lambda i: (0, i, 0))
    return pl.pallas_call(
        functools.partial(_s5_kernel, tt=tt),
        out_shape=jax.ShapeDtypeStruct(x.shape, x.dtype),
        grid=(L // tt,),
        in_specs=[xspec, _full((1, d)), _full(w_in.shape), _full(perm.shape), _full(perm_t.shape),
                  _full(bb.shape), _full(cc.shape), _full(ar.shape), _full(ai.shape), _full((1, width)),
                  _full(w_glu.shape), _full((1, width)), _full(w_out.shape), _full((1, d))],
        out_specs=xspec,
        scratch_shapes=[pltpu.VMEM((rows, width), F32),
                        pltpu.VMEM((rows, width), F32),
                        pltpu.VMEM((rows, width), F32),
                        pltpu.VMEM((2, rows, 2 * S5_STATE_BLOCK), F32),
                        pltpu.VMEM((nblk, bsz, 2 * S5_STATE_BLOCK), F32)],
        compiler_params=_cparams(("arbitrary",)),
        name="s5_layer",
    )(x, pre_g.reshape(1, d), w_in, perm, perm_t, bb, cc, ar, ai, d_skip.reshape(1, width),
      w_glu, b_glu.reshape(1, width), w_out, post_g.reshape(1, d))


def _swa_bias(rel_bias):
    W = WINDOW
    n = 4 * W
    dist = jnp.arange(n) - W
    valid = jnp.logical_and(dist >= 0, dist < W)
    dpos = jnp.maximum(dist, 0)
    max_exact = REL_BUCKETS // 2
    dist_f = jnp.maximum(dpos, 1).astype(F32)
    large = max_exact + (jnp.log(dist_f / max_exact) / math.log(REL_MAX_DIST / max_exact)
                         * (REL_BUCKETS - max_exact)).astype(jnp.int32)
    large = jnp.minimum(large, REL_BUCKETS - 1)
    bucket = jnp.where(dpos < max_exact, dpos, large)
    return jnp.where(valid[:, None], rel_bias[bucket].astype(F32), NEG_INF).T


def _swa_kernel(x_ref, pg_ref, wqt_ref, wk_ref, wvt_ref, wz_ref, bvec_ref, sink_ref, wo_ref, g_ref,
                out_ref, ot_ref, bias_ref, kprev_ref, vtprev_ref, *, scale):
    W = WINDOW
    nwin = x_ref.shape[1] // W
    step = pl.program_id(1)

    @pl.when(step == 0)
    def _():
        kprev_ref[...] = jnp.zeros_like(kprev_ref)
        vtprev_ref[...] = jnp.zeros_like(vtprev_ref)

    x = x_ref[0]
    hb = _rms(x, pg_ref[...]).astype(BF16)
    qt = (_dot_nt(wqt_ref[...], hb) * scale).astype(BF16)
    k = _dot(hb, wk_ref[...]).astype(BF16)
    vt = _dot_nt(wvt_ref[...], hb).astype(BF16)
    z = _dot(hb, wz_ref[...])

    @pl.when(jnp.logical_and(pl.program_id(0) == 0, step == 0))
    def _():
        no_prev = lax.broadcasted_iota(jnp.int32, (2 * W, W), 0) < W
        for hd in range(SWA_HEADS):
            base = jnp.broadcast_to(bvec_ref[hd:hd + 1, :], (2 * W, bvec_ref.shape[1]))
            toep = pltpu.roll(base, 0, 1, stride=1, stride_axis=0)[:, 2 * W:3 * W]
            h, g = divmod(hd, SWA_GROUP)
            bias_ref[0, h, :, g * W:(g + 1) * W] = toep
            bias_ref[1, h, :, g * W:(g + 1) * W] = jnp.where(no_prev, NEG_INF, toep)
    kall = jnp.concatenate([kprev_ref[...], k], axis=0)
    vtall = jnp.concatenate([vtprev_ref[...], vt], axis=1)
    kprev_ref[...] = k[(nwin - 1) * W:]
    vtprev_ref[...] = vt[:, (nwin - 1) * W:]
    nsub = SWA_UNIT_HEADS
    zq = jnp.zeros((HEAD_DIM, nsub * W), BF16)
    ones = jnp.ones((16, 2 * W), BF16)
    units = [(w, h, c) for w in range(nwin) for h in range(SWA_KV_HEADS) for c in range(SWA_GROUP // nsub)]

    def scores(w, h, c):
        hd0 = h * SWA_GROUP + c * nsub
        qh = jnp.concatenate([qt[(hd0 + g) * HEAD_DIM:(hd0 + g + 1) * HEAD_DIM, w * W:(w + 1) * W]
                              for g in range(nsub)], axis=1)
        qz = jnp.concatenate([qh, zq] if h == 0 else [zq, qh], axis=0)
        return _dot(kall[w * W:(w + 2) * W], qz)

    pending = [scores(*u) for u in units[:SWA_LOOKAHEAD]]
    late = []

    def flush():
        (w, h, c), p, tail = late.pop(0)
        vones = jnp.concatenate([vtall[h * HEAD_DIM:(h + 1) * HEAD_DIM, w * W:(w + 2) * W], ones], axis=0)
        o = _dot(vones, p)
        oh = o[:HEAD_DIM] * (1.0 / (o[HEAD_DIM:HEAD_DIM + 1] + tail))
        for g in range(nsub):
            hd = h * SWA_GROUP + c * nsub + g
            ot_ref[hd * HEAD_DIM:(hd + 1) * HEAD_DIM, w * W:(w + 1) * W] = oh[:, g * W:(g + 1) * W]

    for idx, (w, h, c) in enumerate(units):
        raw = pending.pop(0)
        if idx + SWA_LOOKAHEAD < len(units):
            pending.append(scores(*units[idx + SWA_LOOKAHEAD]))
        cols = slice(c * nsub * W, (c + 1) * nsub * W)
        variant = (step == 0).astype(jnp.int32) if w == 0 else 0
        s = raw + bias_ref[variant, h, :, cols]
        sink = sink_ref[h, :, cols]
        m = jnp.maximum(jnp.max(s, axis=0, keepdims=True), sink)
        if len(late) == SWA_PV_DELAY:
            flush()
        late.append(((w, h, c), jnp.exp2(s - m).astype(BF16), jnp.exp2(sink - m)))
    while late:
        flush()
    gated = ot_ref[...].T * jax.nn.silu(z)
    r = _dot(gated.astype(BF16), wo_ref[...].astype(BF16))
    out_ref[0] = x + _rms(r, g_ref[...])


def _swa_layer(x, pre_g, post_g, w_in, sinks, w_out, rel_bias):
    bsz, L, d = x.shape
    width = SWA_HEADS * HEAD_DIM
    kvw = SWA_KV_HEADS * HEAD_DIM
    W = WINDOW
    log2e = math.log2(math.e)
    wb = w_in.astype(BF16)
    wqt = wb[:, :width].T
    wk = wb[:, width:width + kvw]
    wvt = wb[:, width + kvw:width + 2 * kvw].T
    wz = wb[:, width + 2 * kvw:]
    bvec = _swa_bias(rel_bias.astype(F32) * log2e)
    sink = jnp.repeat(sinks.astype(F32) * log2e, W).reshape(SWA_KV_HEADS, 1, SWA_GROUP * W)

    tq = SWA_WINDOWS_PER_STEP * W
    xspec = pl.BlockSpec((1, tq, d), lambda b, n: (b, n, 0))
    return pl.pallas_call(
        functools.partial(_swa_kernel, scale=HEAD_DIM ** -0.5 * log2e),
        out_shape=jax.ShapeDtypeStruct(x.shape, x.dtype),
        grid=(bsz, L // tq),
        in_specs=[xspec, _full((1, d)), _full(wqt.shape), _full(wk.shape), _full(wvt.shape), _full(wz.shape),
                  _full(bvec.shape), _full(sink.shape), _full(w_out.shape), _full((1, d))],
        out_specs=xspec,
        scratch_shapes=[pltpu.VMEM((width, tq), F32),
                        pltpu.VMEM((2, SWA_KV_HEADS, 2 * W, SWA_GROUP * W), F32),
                        pltpu.VMEM((W, kvw), BF16),
                        pltpu.VMEM((kvw, W), BF16)],
        compiler_params=_cparams(("arbitrary", "arbitrary")),
        name="swa_layer",
    )(x, pre_g.reshape(1, d), wqt, wk, wvt, wz, bvec, sink, w_out, post_g.reshape(1, d))


def _mla_pre_kernel(x_ref, g_ref, w_ref, qn_ref, kvn_ref, wq_ref, wkv_ref, wvt_ref, cq_ref, sq_ref, ck_ref, sk_ref,
                    oqn_ref, oqr_ref, okn_ref, okr_ref, ov_ref, oz_ref, *, scale):
    nope = MLA_HEADS * MLA_NOPE
    rope = MLA_HEADS * MLA_ROPE
    vw = MLA_HEADS * MLA_V
    hb = _rms(x_ref[0], g_ref[...]).astype(BF16)
    o1 = MLA_Q_RANK
    o2 = o1 + MLA_KV_RANK
    o3 = o2 + vw
    cq = _dot(hb, w_ref[:, :o1])
    ckv = _dot(hb, w_ref[:, o1:o2])
    oz_ref[0] = _dot(hb, w_ref[:, o2:o3])
    kr = _dot(hb, w_ref[:, o3:o3 + LANES])
    krs = _dot(hb, w_ref[:, o3 + LANES:o3 + 2 * LANES])
    okr_ref[0] = (kr * ck_ref[...] + krs * sk_ref[...]).astype(BF16)
    cqb = _rms(cq, qn_ref[...]).astype(BF16)
    oqn_ref[0] = (_dot_nt(wq_ref[:nope], cqb) * scale).astype(BF16)
    qr = _dot_nt(wq_ref[nope:nope + rope], cqb)
    hr = MLA_ROPE // 2
    qrs = jnp.concatenate([qr[h * MLA_ROPE + o:h * MLA_ROPE + o + hr]
                           for h in range(MLA_HEADS) for o in (hr, 0)], axis=0)
    oqr_ref[0] = ((qr * cq_ref[...] + qrs * sq_ref[...]) * scale).astype(BF16)
    ckb = _rms(ckv, kvn_ref[...]).astype(BF16)
    okn_ref[0] = _dot(ckb, wkv_ref[:, :nope]).astype(BF16)
    vt = _dot_nt(wvt_ref[...], ckb).astype(BF16)
    tk = ov_ref.shape[3]
    for c in range(ov_ref.shape[1]):
        ov_ref[0, c] = vt[:, c * tk:(c + 1) * tk]


def _mla_attn_kernel(qn_ref, qr_ref, kn_ref, kr_ref, v_ref, z_ref, x_ref, wo_ref, g_ref, out_ref,
                     qs_ref, acc_ref, m_ref, o_ref):
    tq = qn_ref.shape[2]
    tk = v_ref.shape[3]
    npairs = MLA_HEADS // 2
    i = pl.program_id(1)
    tri = (lax.broadcasted_iota(jnp.int32, (LANES, LANES), 0)
           <= lax.broadcasted_iota(jnp.int32, (LANES, LANES), 1))

    zn = jnp.zeros((MLA_NOPE, tq), BF16)
    zr = jnp.zeros((LANES - MLA_ROPE, tq), BF16)
    for p in range(npairs):
        qn = qn_ref[0, p * LANES:(p + 1) * LANES, :]
        r0 = 2 * p * MLA_ROPE
        c0 = jnp.concatenate([qn[:MLA_NOPE], zn, qr_ref[0, r0:r0 + MLA_ROPE, :], zr], axis=0)
        c1 = jnp.concatenate([zn, qn[MLA_NOPE:], qr_ref[0, r0 + MLA_ROPE:r0 + 2 * MLA_ROPE, :], zr], axis=0)
        qs_ref[p] = jnp.concatenate([c0, c1], axis=1)

    m_ref[...] = jnp.full(m_ref.shape, NEG_INF, F32)
    acc_ref[...] = jnp.zeros(acc_ref.shape, F32)
    ones = jnp.ones((acc_ref.shape[1] - LANES, tk), BF16)

    def kv_steps(blocks):
        units = [(j, masked, p) for j, masked in blocks for p in range(npairs)]

        def scores(j, masked, p):
            ks = pl.multiple_of(j * tk, tk)
            kc = jnp.concatenate([kn_ref[0, pl.ds(ks, tk), p * LANES:(p + 1) * LANES],
                                  kr_ref[0, pl.ds(ks, tk), :]], axis=1)
            if not masked:
                chunks = [_dot(kc, qs_ref[p, :, c * 2 * LANES:(c + 1) * 2 * LANES]) for c in range(tq // LANES)]
                return [chunks[c // 2][:, (c % 2) * LANES:(c % 2 + 1) * LANES] for c in range(2 * tq // LANES)]
            lo = [_dot(kc[:LANES], qs_ref[p, :, c * 2 * LANES:(c + 1) * 2 * LANES]) for c in range(tq // LANES)]
            late_q = jnp.concatenate([qs_ref[p, :, LANES:2 * LANES], qs_ref[p, :, 3 * LANES:4 * LANES]], axis=1)
            hi = _dot(kc[LANES:], late_q)
            strips = []
            for hd in range(2):
                strips.append(jnp.where(tri, lo[hd][:, :LANES], NEG_INF))
                strips.append(jnp.concatenate([lo[hd][:, LANES:],
                                               jnp.where(tri, hi[:, hd * LANES:(hd + 1) * LANES], NEG_INF)], axis=0))
            return strips

        pending = [scores(*u) for u in units[:MLA_LOOKAHEAD]]
        late = []

        def flush():
            jj, pp, alpha, pr = late.pop(0)
            vones = jnp.concatenate([v_ref[0, jj, pp * LANES:(pp + 1) * LANES, :], ones], axis=0)
            acc_ref[pp] = alpha * acc_ref[pp] + _dot(vones, pr)

        for idx, (j, masked, p) in enumerate(units):
            s = pending.pop(0)
            if idx + MLA_LOOKAHEAD < len(units):
                pending.append(scores(*units[idx + MLA_LOOKAHEAD]))
            probs, alphas = [], []
            for c, sc in enumerate(s):
                m_prev = m_ref[p, :, c * LANES:(c + 1) * LANES]
                m_new = jnp.maximum(m_prev, jnp.max(sc, axis=0, keepdims=True))
                alphas.append(jnp.exp2(m_prev - m_new))
                pr = jnp.exp2(sc - m_new).astype(BF16)
                if pr.shape[0] < tk:
                    pr = jnp.concatenate([pr, jnp.zeros((tk - pr.shape[0], LANES), BF16)], axis=0)
                probs.append(pr)
                m_ref[p, :, c * LANES:(c + 1) * LANES] = m_new
            if len(late) == MLA_PV_DELAY:
                flush()
            late.append((j, p, jnp.concatenate(alphas, axis=1), jnp.concatenate(probs, axis=1)))
        while late:
            flush()

    def body(jj, c):
        kv_steps([(2 * jj, False), (2 * jj + 1, False)])
        return c

    lax.fori_loop(0, i // 2, body, 0)

    @pl.when(i % 2 == 1)
    def _():
        kv_steps([(i - 1, False), (i, True)])

    @pl.when(i % 2 == 0)
    def _():
        kv_steps([(i, True)])
    for p in range(npairs):
        a = acc_ref[p]
        a = a[:LANES] * (1.0 / a[LANES:LANES + 1])
        ot = jnp.concatenate([a[:MLA_V, :tq], a[MLA_V:, tq:]], axis=0)
        o_ref[:, p * LANES:(p + 1) * LANES] = ot.T
    gated = o_ref[...] * jax.nn.silu(z_ref[0])
    r = _dot(gated.astype(BF16), wo_ref[...].astype(BF16))
    out_ref[0] = x_ref[0] + _rms(r, g_ref[...])


def _mla_layer(x, pre_g, post_g, w_in, q_norm, kv_norm, w_uq, w_ukv, w_out):
    bsz, L, d = x.shape
    H = MLA_HEADS
    dq = MLA_NOPE + MLA_ROPE
    nope = H * MLA_NOPE
    rope = H * MLA_ROPE
    vw = H * MLA_V
    half = MLA_ROPE // 2
    o_kr = MLA_Q_RANK + MLA_KV_RANK
    o_z = o_kr + MLA_ROPE
    wb = w_in.astype(BF16)
    w_kr = wb[:, o_kr:o_z]
    w_krs = jnp.concatenate([w_kr[:, half:], w_kr[:, :half]], axis=1)
    reps = LANES // MLA_ROPE
    w1 = jnp.concatenate([wb[:, :o_kr], wb[:, o_z:]] + [w_kr] * reps + [w_krs] * reps, axis=1)
    wq3 = w_uq.astype(BF16).reshape(MLA_Q_RANK, H, dq)
    wqt = jnp.concatenate([wq3[:, :, :MLA_NOPE].reshape(MLA_Q_RANK, nope),
                           wq3[:, :, MLA_NOPE:].reshape(MLA_Q_RANK, rope)], axis=1).T
    wkv3 = w_ukv.astype(BF16).reshape(MLA_KV_RANK, H, MLA_NOPE + MLA_V)
    wkn = wkv3[:, :, :MLA_NOPE].reshape(MLA_KV_RANK, nope)
    wvt = wkv3[:, :, MLA_NOPE:].reshape(MLA_KV_RANK, vw).T
    inv = ROPE_BASE ** (-jnp.arange(0, MLA_ROPE, 2, dtype=F32) / MLA_ROPE)
    ang = jnp.arange(L, dtype=F32)[:, None] * inv[None, :]
    cos, sin = jnp.cos(ang), jnp.sin(ang)
    cos32 = jnp.concatenate([cos, cos], axis=1)
    sin32 = jnp.concatenate([-sin, sin], axis=1)
    cos_k, sin_k = jnp.tile(cos32, (1, LANES // MLA_ROPE)), jnp.tile(sin32, (1, LANES // MLA_ROPE))
    cos_q, sin_q = jnp.tile(cos32, (1, H)).T, jnp.tile(sin32, (1, H)).T

    tm = ROW_TILE
    tk = MLA_TK
    tok = lambda w_: pl.BlockSpec((1, tm, w_), lambda b, i: (b, i, 0))
    tokt = lambda w_: pl.BlockSpec((1, w_, tm), lambda b, i: (b, 0, i))
    scale = dq ** -0.5 * math.log2(math.e)
    qn, qr, kn, kr, v, z = pl.pallas_call(
        functools.partial(_mla_pre_kernel, scale=scale),
        out_shape=[jax.ShapeDtypeStruct((bsz, nope, L), BF16),
                   jax.ShapeDtypeStruct((bsz, rope, L), BF16),
                   jax.ShapeDtypeStruct((bsz, L, nope), BF16),
                   jax.ShapeDtypeStruct((bsz, L, LANES), BF16),
                   jax.ShapeDtypeStruct((bsz, L // tk, vw, tk), BF16),
                   jax.ShapeDtypeStruct((bsz, L, vw), F32)],
        grid=(bsz, L // tm),
        in_specs=[tok(d), _full((1, d)), _full(w1.shape), _full((1, MLA_Q_RANK)), _full((1, MLA_KV_RANK)),
                  _full(wqt.shape), _full(wkn.shape), _full(wvt.shape),
                  pl.BlockSpec((rope, tm), lambda b, i: (0, i)), pl.BlockSpec((rope, tm), lambda b, i: (0, i)),
                  pl.BlockSpec((tm, LANES), lambda b, i: (i, 0)), pl.BlockSpec((tm, LANES), lambda b, i: (i, 0))],
        out_specs=[tokt(nope), tokt(rope), tok(nope), tok(LANES),
                   pl.BlockSpec((1, tm // tk, vw, tk), lambda b, i: (b, i, 0, 0)), tok(vw)],
        compiler_params=_cparams(("parallel", "parallel")),
        name="mla_pre",
    )(x, pre_g.reshape(1, d), w1, q_norm.reshape(1, -1), kv_norm.reshape(1, -1), wqt, wkn, wvt,
      cos_q, sin_q, cos_k, sin_k)

    tq = MLA_TQ
    assert MLA_TQ == MLA_TK == 2 * LANES, "the diagonal-block handling works on 128-query strips of a 256 block"
    npairs = H // 2
    qspec = lambda w_: pl.BlockSpec((1, w_, tq), lambda b, i: (b, 0, i))
    kspec = lambda w_: pl.BlockSpec((1, L, w_), lambda b, i: (b, 0, 0))
    rowspec = lambda w_: pl.BlockSpec((1, tq, w_), lambda b, i: (b, i, 0))
    return pl.pallas_call(
        _mla_attn_kernel,
        out_shape=jax.ShapeDtypeStruct(x.shape, x.dtype),
        grid=(bsz, L // tq),
        in_specs=[qspec(nope), qspec(rope), kspec(nope), kspec(LANES),
                  pl.BlockSpec((1, L // tk, vw, tk), lambda b, i: (b, 0, 0, 0)),
                  rowspec(vw), rowspec(d), _full(w_out.shape), _full((1, d))],
        out_specs=rowspec(d),
        scratch_shapes=[pltpu.VMEM((npairs, 2 * LANES, 2 * tq), BF16),
                        pltpu.VMEM((npairs, LANES + 16, 2 * tq), F32),
                        pltpu.VMEM((npairs, 1, 2 * tq), F32),
                        pltpu.VMEM((tq, vw), F32)],
        compiler_params=_cparams(("parallel", "arbitrary")),
        name="mla_attn",
    )(qn, qr, kn, kr, v, z, x, w_out, post_g.reshape(1, d))


def _sgu_kernel(x_ref, g_ref, w_ref, lng_ref, lnb_ref, ws_ref, bs_ref, wo_ref, pg_ref, out_ref, s_ref):
    width = wo_ref.shape[0]
    tm = x_ref.shape[1]
    lane = lax.broadcasted_iota(jnp.int32, (1, LANES), 1)
    lo = lane < HALF
    x = x_ref[0]
    hb = _rms(x, g_ref[...]).astype(BF16)
    v = jax.nn.gelu(_dot(hb, w_ref[:, width:2 * width].astype(BF16)))
    mu = jnp.mean(v, axis=-1, keepdims=True)
    vc = v - mu
    var = jnp.mean(vc * vc, axis=-1, keepdims=True)
    vb = (vc * lax.rsqrt(var + EPS) * lng_ref[...] + lnb_ref[...]).astype(BF16)
    group = SGU_STACK
    for c0 in range(0, tm // SGU_CHUNK, group):
        for jj in range(width // LANES):
            blk = jnp.concatenate([vb[c * SGU_CHUNK:(c + 1) * SGU_CHUNK, jj * LANES:(jj + 1) * LANES]
                                   for c in range(c0, c0 + group)], axis=1)
            r = _dot(ws_ref[jj], blk)
            for k in range(group):
                c = c0 + k
                s_ref[c * SGU_CHUNK:(c + 1) * SGU_CHUNK, jj * LANES:(jj + 1) * LANES] = (
                    jnp.where(lo, r[:SGU_CHUNK, k * LANES:(k + 1) * LANES],
                              r[SGU_CHUNK:, k * LANES:(k + 1) * LANES]) + bs_ref[jj])
    u = jax.nn.gelu(_dot(hb, w_ref[:, :width].astype(BF16)))
    z = _dot(hb, w_ref[:, 2 * width:].astype(BF16))
    o = u * s_ref[...] * jax.nn.silu(z)
    r = _dot(o.astype(BF16), wo_ref[...].astype(BF16))
    out_ref[0] = x + _rms(r, pg_ref[...])


def _sgu_layer(x, pre_g, post_g, w_in, ln_g, ln_b, w_s, b_s, w_out):
    bsz, L, d = x.shape
    width = w_out.shape[0]
    T = SGU_CHUNK
    gd = width // SGU_GROUPS
    tril = jnp.tril(jnp.ones((T, T), dtype=bool))
    ws = jnp.where(tril[None], w_s, 0.0).reshape(SGU_GROUPS // 2, 2 * T, T).astype(BF16)
    bs = jnp.repeat(b_s.astype(F32).T, gd, axis=1)
    bs = bs.reshape(T, width // LANES, LANES).transpose(1, 0, 2)
    tm = ROW_TILE
    return pl.pallas_call(
        _sgu_kernel,
        out_shape=jax.ShapeDtypeStruct(x.shape, x.dtype),
        grid=(bsz, L // tm),
        in_specs=[pl.BlockSpec((1, tm, d), lambda b, i: (b, i, 0)),
                  _full((1, d)), _full(w_in.shape), _full((1, width)), _full((1, width)),
                  _full(ws.shape), _full(bs.shape), _full(w_out.shape), _full((1, d))],
        out_specs=pl.BlockSpec((1, tm, d), lambda b, i: (b, i, 0)),
        scratch_shapes=[pltpu.VMEM((tm, width), F32)],
        compiler_params=_cparams(("parallel", "parallel")),
        name="sgu",
    )(x, pre_g.reshape(1, d), w_in, ln_g.reshape(1, width), ln_b.reshape(1, width),
      ws, bs, w_out, post_g.reshape(1, d))


def kernel(x, pre_norm, post_norm, rel_bias, a_w_in, a_lam_re, a_lam_im, a_log_dt, a_b_re, a_b_im, a_c_re, a_c_im, a_d, a_w_glu, a_b_glu, a_w_out, b_w_in, b_sinks, b_w_out, c_w_in, c_q_norm, c_kv_norm, c_w_uq, c_w_ukv, c_w_out, d_w_in, d_ln_g, d_ln_b, d_w_s, d_b_s, d_w_out):
    depth = pre_norm.shape[0]
    for i in range(depth):
        kind, j = i % 4, i // 4
        if kind == 0:
            x = _s5_layer(x, pre_norm[i], post_norm[i], a_w_in[j], a_lam_re[j], a_lam_im[j], a_log_dt[j],
                          a_b_re[j], a_b_im[j], a_c_re[j], a_c_im[j], a_d[j], a_w_glu[j], a_b_glu[j],
                          a_w_out[j])
        elif kind == 1:
            x = _swa_layer(x, pre_norm[i], post_norm[i], b_w_in[j], b_sinks[j], b_w_out[j], rel_bias)
        elif kind == 2:
            x = _mla_layer(x, pre_norm[i], post_norm[i], c_w_in[j], c_q_norm[j], c_kv_norm[j], c_w_uq[j],
                           c_w_ukv[j], c_w_out[j])
        else:
            x = _sgu_layer(x, pre_norm[i], post_norm[i], d_w_in[j], d_ln_g[j], d_ln_b[j], d_w_s[j],
                           d_b_s[j], d_w_out[j])
    return x
```

```python
import functools
import math

import jax
import jax.numpy as jnp
import numpy as np
from jax import lax
from jax.experimental import pallas as pl
from jax.experimental.pallas import tpu as pltpu

F32 = jnp.float32
BF16 = jnp.bfloat16

EPS = 1e-6
NEG_INF = -1e30
LANES = 128
HALF = LANES // 2

SSM_GROUP = 16
SSM_STATE = 64
S5_CH_BLOCK = LANES
S5_GROUPS_PER_BLOCK = S5_CH_BLOCK // SSM_GROUP
S5_STATE_BLOCK = S5_GROUPS_PER_BLOCK * SSM_STATE
S5_T = 64

HEAD_DIM = 64
SWA_HEADS = 16
SWA_KV_HEADS = 2
SWA_GROUP = SWA_HEADS // SWA_KV_HEADS
WINDOW = 128
SWA_WINDOWS_PER_STEP = 4
SWA_UNIT_HEADS = 8
SWA_LOOKAHEAD = 2
SWA_PV_DELAY = 1
REL_BUCKETS = 32
REL_MAX_DIST = 128

MLA_HEADS = 16
MLA_NOPE = 64
MLA_ROPE = 32
MLA_V = 64
MLA_KV_RANK = 256
MLA_Q_RANK = 768
ROPE_BASE = 10000.0
MLA_TQ = 256
MLA_TK = 256
MLA_LOOKAHEAD = 6
MLA_PV_DELAY = 2

SGU_CHUNK = 128
SGU_GROUPS = 16
SGU_STACK = 4

ROW_TILE = 1024
VMEM_LIMIT = 56 * 1024 * 1024


def _cparams(sem):
    return pltpu.CompilerParams(dimension_semantics=sem, vmem_limit_bytes=VMEM_LIMIT)


def _rms(x, g):
    return x * lax.rsqrt(jnp.mean(x * x, axis=-1, keepdims=True) + EPS) * g


def _dot(a, b):
    return jnp.dot(a, b, preferred_element_type=F32)


def _dot_nt(a, b):
    return lax.dot_general(a, b, (((1,), (1,)), ((), ())), preferred_element_type=F32)


def _full(shape):
    n = len(shape)
    return pl.BlockSpec(shape, lambda *_: (0,) * n, pipeline_mode=pl.Buffered(1))


def _s5_kernel(x_ref, g_ref, w_ref, bb_ref, cc_ref, ar_ref, ai_ref, d_ref,
               wg_ref, bg_ref, wo_ref, pg_ref, out_ref, u_ref, z_ref, y_ref, s_ref, carry_ref, *, tt):
    bsz = x_ref.shape[0]
    width = wg_ref.shape[0]
    rows = bsz * tt
    nblk = bb_ref.shape[0]
    sb = S5_STATE_BLOCK

    @pl.when(pl.program_id(0) == 0)
    def _():
        carry_ref[...] = jnp.zeros_like(carry_ref)

    x = x_ref[...].reshape(rows, x_ref.shape[2])
    h = _rms(x, g_ref[...])
    hb = pltpu.einshape("btd->(tb)d", h.reshape(bsz, tt, h.shape[1])).astype(BF16)
    u_ref[...] = _dot(hb, w_ref[:, :width].astype(BF16))
    z_ref[...] = _dot(hb, w_ref[:, width:].astype(BF16))

    nbuf = s_ref.shape[0]

    def project_in(i):
        s_ref[i % nbuf] = _dot(u_ref[:, i * LANES:(i + 1) * LANES].astype(BF16), bb_ref[i])

    def project_out(i):
        ub = u_ref[:, i * LANES:(i + 1) * LANES]
        y = _dot(s_ref[i % nbuf].astype(BF16), cc_ref[i]) + d_ref[:, i * LANES:(i + 1) * LANES] * ub
        y_ref[:, i * LANES:(i + 1) * LANES] = jax.nn.gelu(y)

    project_in(0)
    for i in range(nblk):
        if i + 1 < nblk:
            project_in(i + 1)
        buf = s_ref.at[i % nbuf]
        ar = ar_ref[i]
        ai = ai_ref[i]
        sr = carry_ref[i, :, 0:sb]
        si = carry_ref[i, :, sb:2 * sb]
        for t in range(tt):
            r0 = t * bsz
            nr = ar * sr - ai * si + buf[r0:r0 + bsz, 0:sb]
            ni = ar * si + ai * sr + buf[r0:r0 + bsz, sb:2 * sb]
            buf[r0:r0 + bsz, 0:sb] = nr
            buf[r0:r0 + bsz, sb:2 * sb] = ni
            sr, si = nr, ni
        carry_ref[i, :, 0:sb] = sr
        carry_ref[i, :, sb:2 * sb] = si
        project_out(i)

    y = y_ref[...]
    gate = jax.nn.sigmoid(_dot(y.astype(BF16), wg_ref[...].astype(BF16)) + bg_ref[...])
    o = y * gate * jax.nn.silu(z_ref[...])
    ob = pltpu.einshape("tbd->(bt)d", o.reshape(tt, bsz, o.shape[1])).astype(BF16)
    r = _dot(ob, wo_ref[...].astype(BF16))
    out_ref[...] = (x + _rms(r, pg_ref[...])).reshape(out_ref.shape)


def _s5_discretize(lam_re, lam_im, log_dt, b_re, b_im):
    dt = jnp.exp(log_dt)[:, None]
    mag = jnp.exp(lam_re * dt)
    ab_re = mag * jnp.cos(lam_im * dt)
    ab_im = mag * jnp.sin(lam_im * dt)
    den = lam_re * lam_re + lam_im * lam_im
    nr = ab_re - 1.0
    f_re = (nr * lam_re + ab_im * lam_im) / den
    f_im = (ab_im * lam_re - nr * lam_im) / den
    bb_re = f_re[..., None] * b_re - f_im[..., None] * b_im
    bb_im = f_re[..., None] * b_im + f_im[..., None] * b_re
    return ab_re, ab_im, bb_re, bb_im


def _s5_layer(x, pre_g, post_g, w_in, lam_re, lam_im, log_dt, b_re, b_im, c_re, c_im, d_skip,
              w_glu, b_glu, w_out):
    bsz, L, d = x.shape
    width = w_in.shape[1] // 2
    nblk = width // S5_CH_BLOCK
    gpb = S5_GROUPS_PER_BLOCK
    tt = S5_T
    rows = bsz * tt

    ab_re, ab_im, bb_re, bb_im = _s5_discretize(lam_re, lam_im, log_dt, b_re, b_im)
    eye = jnp.eye(gpb, dtype=F32)

    def pack_b(bb):
        t = bb.reshape(nblk, gpb, SSM_STATE, SSM_GROUP)
        return jnp.einsum('igph,gk->ikhgp', t, eye).reshape(nblk, S5_CH_BLOCK, S5_STATE_BLOCK)

    def pack_c(cc):
        t = cc.reshape(nblk, gpb, SSM_GROUP, SSM_STATE)
        return jnp.einsum('ighp,gk->igpkh', t, eye).reshape(nblk, S5_STATE_BLOCK, S5_CH_BLOCK)

    bb = jnp.concatenate([pack_b(bb_re), pack_b(bb_im)], axis=2).astype(BF16)
    cc = jnp.concatenate([pack_c(c_re), -pack_c(c_im)], axis=1).astype(BF16)
    ar = jnp.broadcast_to(ab_re.reshape(nblk, 1, S5_STATE_BLOCK), (nblk, bsz, S5_STATE_BLOCK))
    ai = jnp.broadcast_to(ab_im.reshape(nblk, 1, S5_STATE_BLOCK), (nblk, bsz, S5_STATE_BLOCK))

    xspec = pl.BlockSpec((bsz, tt, d), lambda i: (0, i, 0))
    return pl.pallas_call(
        functools.partial(_s5_kernel, tt=tt),
        out_shape=jax.ShapeDtypeStruct(x.shape, x.dtype),
        grid=(L // tt,),
        in_specs=[xspec, _full((1, d)), _full(w_in.shape),
                  _full(bb.shape), _full(cc.shape), _full(ar.shape), _full(ai.shape), _full((1, width)),
                  _full(w_glu.shape), _full((1, width)), _full(w_out.shape), _full((1, d))],
        out_specs=xspec,
        scratch_shapes=[pltpu.VMEM((rows, width), F32),
                        pltpu.VMEM((rows, width), F32),
                        pltpu.VMEM((rows, width), F32),
                        pltpu.VMEM((2, rows, 2 * S5_STATE_BLOCK), F32),
                        pltpu.VMEM((nblk, bsz, 2 * S5_STATE_BLOCK), F32)],
        compiler_params=_cparams(("arbitrary",)),
        name="s5_layer",
    )(x, pre_g.reshape(1, d), w_in, bb, cc, ar, ai, d_skip.reshape(1, width),
      w_glu, b_glu.reshape(1, width), w_out, post_g.reshape(1, d))


def _swa_bias(rel_bias):
    W = WINDOW
    n = 4 * W
    dist = jnp.arange(n) - W
    valid = jnp.logical_and(dist >= 0, dist < W)
    dpos = jnp.maximum(dist, 0)
    max_exact = REL_BUCKETS // 2
    dist_f = jnp.maximum(dpos, 1).astype(F32)
    large = max_exact + (jnp.log(dist_f / max_exact) / math.log(REL_MAX_DIST / max_exact)
                         * (REL_BUCKETS - max_exact)).astype(jnp.int32)
    large = jnp.minimum(large, REL_BUCKETS - 1)
    bucket = jnp.where(dpos < max_exact, dpos, large)
    return jnp.where(valid[:, None], rel_bias[bucket].astype(F32), NEG_INF).T


def _swa_kernel(x_ref, pg_ref, wqt_ref, wk_ref, wvt_ref, wz_ref, bvec_ref, sink_ref, wo_ref, g_ref,
                out_ref, ot_ref, bias_ref, kprev_ref, vtprev_ref, *, scale):
    W = WINDOW
    nwin = x_ref.shape[1] // W
    step = pl.program_id(1)

    @pl.when(step == 0)
    def _():
        kprev_ref[...] = jnp.zeros_like(kprev_ref)
        vtprev_ref[...] = jnp.zeros_like(vtprev_ref)

    x = x_ref[0]
    hb = _rms(x, pg_ref[...]).astype(BF16)
    qt = (_dot_nt(wqt_ref[...], hb) * scale).astype(BF16)
    k = _dot(hb, wk_ref[...]).astype(BF16)
    vt = _dot_nt(wvt_ref[...], hb).astype(BF16)
    z = _dot(hb, wz_ref[...])

    @pl.when(jnp.logical_and(pl.program_id(0) == 0, step == 0))
    def _():
        no_prev = lax.broadcasted_iota(jnp.int32, (2 * W, W), 0) < W
        for hd in range(SWA_HEADS):
            base = jnp.broadcast_to(bvec_ref[hd:hd + 1, :], (2 * W, bvec_ref.shape[1]))
            toep = pltpu.roll(base, 0, 1, stride=1, stride_axis=0)[:, 2 * W:3 * W]
            h, g = divmod(hd, SWA_GROUP)
            bias_ref[0, h, :, g * W:(g + 1) * W] = toep
            bias_ref[1, h, :, g * W:(g + 1) * W] = jnp.where(no_prev, NEG_INF, toep)
    kall = jnp.concatenate([kprev_ref[...], k], axis=0)
    vtall = jnp.concatenate([vtprev_ref[...], vt], axis=1)
    kprev_ref[...] = k[(nwin - 1) * W:]
    vtprev_ref[...] = vt[:, (nwin - 1) * W:]
    nsub = SWA_UNIT_HEADS
    zq = jnp.zeros((HEAD_DIM, nsub * W), BF16)
    ones = jnp.ones((16, 2 * W), BF16)
    units = [(w, h, c) for w in range(nwin) for h in range(SWA_KV_HEADS) for c in range(SWA_GROUP // nsub)]

    def scores(w, h, c):
        hd0 = h * SWA_GROUP + c * nsub
        qh = jnp.concatenate([qt[(hd0 + g) * HEAD_DIM:(hd0 + g + 1) * HEAD_DIM, w * W:(w + 1) * W]
                              for g in range(nsub)], axis=1)
        qz = jnp.concatenate([qh, zq] if h == 0 else [zq, qh], axis=0)
        return _dot(kall[w * W:(w + 2) * W], qz)

    pending = [scores(*u) for u in units[:SWA_LOOKAHEAD]]
    late = []

    def flush():
        (w, h, c), p, tail = late.pop(0)
        vones = jnp.concatenate([vtall[h * HEAD_DIM:(h + 1) * HEAD_DIM, w * W:(w + 2) * W], ones], axis=0)
        o = _dot(vones, p)
        oh = o[:HEAD_DIM] * (1.0 / (o[HEAD_DIM:HEAD_DIM + 1] + tail))
        for g in range(nsub):
            hd = h * SWA_GROUP + c * nsub + g
            ot_ref[hd * HEAD_DIM:(hd + 1) * HEAD_DIM, w * W:(w + 1) * W] = oh[:, g * W:(g + 1) * W]

    for idx, (w, h, c) in enumerate(units):
        raw = pending.pop(0)
        if idx + SWA_LOOKAHEAD < len(units):
            pending.append(scores(*units[idx + SWA_LOOKAHEAD]))
        cols = slice(c * nsub * W, (c + 1) * nsub * W)
        variant = (step == 0).astype(jnp.int32) if w == 0 else 0
        s = raw + bias_ref[variant, h, :, cols]
        sink = sink_ref[h, :, cols]
        m = jnp.maximum(jnp.max(s, axis=0, keepdims=True), sink)
        if len(late) == SWA_PV_DELAY:
            flush()
        late.append(((w, h, c), jnp.exp2(s - m).astype(BF16), jnp.exp2(sink - m)))
    while late:
        flush()
    gated = ot_ref[...].T * jax.nn.silu(z)
    r = _dot(gated.astype(BF16), wo_ref[...].astype(BF16))
    out_ref[0] = x + _rms(r, g_ref[...])


def _swa_layer(x, pre_g, post_g, w_in, sinks, w_out, rel_bias):
    bsz, L, d = x.shape
    width = SWA_HEADS * HEAD_DIM
    kvw = SWA_KV_HEADS * HEAD_DIM
    W = WINDOW
    log2e = math.log2(math.e)
    wb = w_in.astype(BF16)
    wqt = wb[:, :width].T
    wk = wb[:, width:width + kvw]
    wvt = wb[:, width + kvw:width + 2 * kvw].T
    wz = wb[:, width + 2 * kvw:]
    bvec = _swa_bias(rel_bias.astype(F32) * log2e)
    sink = jnp.repeat(sinks.astype(F32) * log2e, W).reshape(SWA_KV_HEADS, 1, SWA_GROUP * W)

    tq = SWA_WINDOWS_PER_STEP * W
    xspec = pl.BlockSpec((1, tq, d), lambda b, n: (b, n, 0))
    return pl.pallas_call(
        functools.partial(_swa_kernel, scale=HEAD_DIM ** -0.5 * log2e),
        out_shape=jax.ShapeDtypeStruct(x.shape, x.dtype),
        grid=(bsz, L // tq),
        in_specs=[xspec, _full((1, d)), _full(wqt.shape), _full(wk.shape), _full(wvt.shape), _full(wz.shape),
                  _full(bvec.shape), _full(sink.shape), _full(w_out.shape), _full((1, d))],
        out_specs=xspec,
        scratch_shapes=[pltpu.VMEM((width, tq), F32),
                        pltpu.VMEM((2, SWA_KV_HEADS, 2 * W, SWA_GROUP * W), F32),
                        pltpu.VMEM((W, kvw), BF16),
                        pltpu.VMEM((kvw, W), BF16)],
        compiler_params=_cparams(("arbitrary", "arbitrary")),
        name="swa_layer",
    )(x, pre_g.reshape(1, d), wqt, wk, wvt, wz, bvec, sink, w_out, post_g.reshape(1, d))


def _mla_pre_kernel(x_ref, g_ref, w_ref, qn_ref, kvn_ref, wq_ref, wkv_ref, wvt_ref, cq_ref, sq_ref, ck_ref, sk_ref,
                    oqn_ref, oqr_ref, okn_ref, okr_ref, ov_ref, oz_ref, *, scale):
    nope = MLA_HEADS * MLA_NOPE
    rope = MLA_HEADS * MLA_ROPE
    vw = MLA_HEADS * MLA_V
    hb = _rms(x_ref[0], g_ref[...]).astype(BF16)
    o1 = MLA_Q_RANK
    o2 = o1 + MLA_KV_RANK
    o3 = o2 + vw
    cq = _dot(hb, w_ref[:, :o1])
    ckv = _dot(hb, w_ref[:, o1:o2])
    oz_ref[0] = _dot(hb, w_ref[:, o2:o3])
    kr = _dot(hb, w_ref[:, o3:o3 + LANES])
    krs = _dot(hb, w_ref[:, o3 + LANES:o3 + 2 * LANES])
    okr_ref[0] = (kr * ck_ref[...] + krs * sk_ref[...]).astype(BF16)
    cqb = _rms(cq, qn_ref[...]).astype(BF16)
    oqn_ref[0] = (_dot_nt(wq_ref[:nope], cqb) * scale).astype(BF16)
    qr = _dot_nt(wq_ref[nope:nope + rope], cqb)
    hr = MLA_ROPE // 2
    qrs = jnp.concatenate([qr[h * MLA_ROPE + o:h * MLA_ROPE + o + hr]
                           for h in range(MLA_HEADS) for o in (hr, 0)], axis=0)
    oqr_ref[0] = ((qr * cq_ref[...] + qrs * sq_ref[...]) * scale).astype(BF16)
    ckb = _rms(ckv, kvn_ref[...]).astype(BF16)
    okn_ref[0] = _dot(ckb, wkv_ref[:, :nope]).astype(BF16)
    vt = _dot_nt(wvt_ref[...], ckb).astype(BF16)
    tk = ov_ref.shape[3]
    for c in range(ov_ref.shape[1]):
        ov_ref[0, c] = vt[:, c * tk:(c + 1) * tk]


def _mla_attn_kernel(qn_ref, qr_ref, kn_ref, kr_ref, v_ref, z_ref, x_ref, wo_ref, g_ref, out_ref,
                     qs_ref, acc_ref, m_ref, o_ref):
    tq = qn_ref.shape[2]
    tk = v_ref.shape[3]
    npairs = MLA_HEADS // 2
    i = pl.program_id(1)
    tri = (lax.broadcasted_iota(jnp.int32, (LANES, LANES), 0)
           <= lax.broadcasted_iota(jnp.int32, (LANES, LANES), 1))

    zn = jnp.zeros((MLA_NOPE, tq), BF16)
    zr = jnp.zeros((LANES - MLA_ROPE, tq), BF16)
    for p in range(npairs):
        qn = qn_ref[0, p * LANES:(p + 1) * LANES, :]
        r0 = 2 * p * MLA_ROPE
        c0 = jnp.concatenate([qn[:MLA_NOPE], zn, qr_ref[0, r0:r0 + MLA_ROPE, :], zr], axis=0)
        c1 = jnp.concatenate([zn, qn[MLA_NOPE:], qr_ref[0, r0 + MLA_ROPE:r0 + 2 * MLA_ROPE, :], zr], axis=0)
        qs_ref[p] = jnp.concatenate([c0, c1], axis=1)

    m_ref[...] = jnp.full(m_ref.shape, NEG_INF, F32)
    acc_ref[...] = jnp.zeros(acc_ref.shape, F32)
    ones = jnp.ones((acc_ref.shape[1] - LANES, tk), BF16)

    def kv_steps(blocks):
        units = [(j, masked, p) for j, masked in blocks for p in range(npairs)]

        def scores(j, masked, p):
            ks = pl.multiple_of(j * tk, tk)
            kc = jnp.concatenate([kn_ref[0, pl.ds(ks, tk), p * LANES:(p + 1) * LANES],
                                  kr_ref[0, pl.ds(ks, tk), :]], axis=1)
            if not masked:
                chunks = [_dot(kc, qs_ref[p, :, c * 2 * LANES:(c + 1) * 2 * LANES]) for c in range(tq // LANES)]
                return [chunks[c // 2][:, (c % 2) * LANES:(c % 2 + 1) * LANES] for c in range(2 * tq // LANES)]
            lo = [_dot(kc[:LANES], qs_ref[p, :, c * 2 * LANES:(c + 1) * 2 * LANES]) for c in range(tq // LANES)]
            late_q = jnp.concatenate([qs_ref[p, :, LANES:2 * LANES], qs_ref[p, :, 3 * LANES:4 * LANES]], axis=1)
            hi = _dot(kc[LANES:], late_q)
            strips = []
            for hd in range(2):
                strips.append(jnp.where(tri, lo[hd][:, :LANES], NEG_INF))
                strips.append(jnp.concatenate([lo[hd][:, LANES:],
                                               jnp.where(tri, hi[:, hd * LANES:(hd + 1) * LANES], NEG_INF)], axis=0))
            return strips

        pending = [scores(*u) for u in units[:MLA_LOOKAHEAD]]
        late = []

        def flush():
            jj, pp, alpha, pr = late.pop(0)
            vones = jnp.concatenate([v_ref[0, jj, pp * LANES:(pp + 1) * LANES, :], ones], axis=0)
            acc_ref[pp] = alpha * acc_ref[pp] + _dot(vones, pr)

        for idx, (j, masked, p) in enumerate(units):
            s = pending.pop(0)
            if idx + MLA_LOOKAHEAD < len(units):
                pending.append(scores(*units[idx + MLA_LOOKAHEAD]))
            probs, alphas = [], []
            for c, sc in enumerate(s):
                m_prev = m_ref[p, :, c * LANES:(c + 1) * LANES]
                m_new = jnp.maximum(m_prev, jnp.max(sc, axis=0, keepdims=True))
                alphas.append(jnp.exp2(m_prev - m_new))
                pr = jnp.exp2(sc - m_new).astype(BF16)
                if pr.shape[0] < tk:
                    pr = jnp.concatenate([pr, jnp.zeros((tk - pr.shape[0], LANES), BF16)], axis=0)
                probs.append(pr)
                m_ref[p, :, c * LANES:(c + 1) * LANES] = m_new
            if len(late) == MLA_PV_DELAY:
                flush()
            late.append((j, p, jnp.concatenate(alphas, axis=1), jnp.concatenate(probs, axis=1)))
        while late:
            flush()

    def body(jj, c):
        kv_steps([(2 * jj, False), (2 * jj + 1, False)])
        return c

    lax.fori_loop(0, i // 2, body, 0)

    @pl.when(i % 2 == 1)
    def _():
        kv_steps([(i - 1, False), (i, True)])

    @pl.when(i % 2 == 0)
    def _():
        kv_steps([(i, True)])
    for p in range(npairs):
        a = acc_ref[p]
        a = a[:LANES] * (1.0 / a[LANES:LANES + 1])
        ot = jnp.concatenate([a[:MLA_V, :tq], a[MLA_V:, tq:]], axis=0)
        o_ref[:, p * LANES:(p + 1) * LANES] = ot.T
    gated = o_ref[...] * jax.nn.silu(z_ref[0])
    r = _dot(gated.astype(BF16), wo_ref[...].astype(BF16))
    out_ref[0] = x_ref[0] + _rms(r, g_ref[...])


def _mla_layer(x, pre_g, post_g, w_in, q_norm, kv_norm, w_uq, w_ukv, w_out):
    bsz, L, d = x.shape
    H = MLA_HEADS
    dq = MLA_NOPE + MLA_ROPE
    nope = H * MLA_NOPE
    rope = H * MLA_ROPE
    vw = H * MLA_V
    half = MLA_ROPE // 2
    o_kr = MLA_Q_RANK + MLA_KV_RANK
    o_z = o_kr + MLA_ROPE
    wb = w_in.astype(BF16)
    w_kr = wb[:, o_kr:o_z]
    w_krs = jnp.concatenate([w_kr[:, half:], w_kr[:, :half]], axis=1)
    reps = LANES // MLA_ROPE
    w1 = jnp.concatenate([wb[:, :o_kr], wb[:, o_z:]] + [w_kr] * reps + [w_krs] * reps, axis=1)
    wq3 = w_uq.astype(BF16).reshape(MLA_Q_RANK, H, dq)
    wqt = jnp.concatenate([wq3[:, :, :MLA_NOPE].reshape(MLA_Q_RANK, nope),
                           wq3[:, :, MLA_NOPE:].reshape(MLA_Q_RANK, rope)], axis=1).T
    wkv3 = w_ukv.astype(BF16).reshape(MLA_KV_RANK, H, MLA_NOPE + MLA_V)
    wkn = wkv3[:, :, :MLA_NOPE].reshape(MLA_KV_RANK, nope)
    wvt = wkv3[:, :, MLA_NOPE:].reshape(MLA_KV_RANK, vw).T
    inv = ROPE_BASE ** (-jnp.arange(0, MLA_ROPE, 2, dtype=F32) / MLA_ROPE)
    ang = jnp.arange(L, dtype=F32)[:, None] * inv[None, :]
    cos, sin = jnp.cos(ang), jnp.sin(ang)
    cos32 = jnp.concatenate([cos, cos], axis=1)
    sin32 = jnp.concatenate([-sin, sin], axis=1)
    cos_k, sin_k = jnp.tile(cos32, (1, LANES // MLA_ROPE)), jnp.tile(sin32, (1, LANES // MLA_ROPE))
    cos_q, sin_q = jnp.tile(cos32, (1, H)).T, jnp.tile(sin32, (1, H)).T

    tm = ROW_TILE
    tk = MLA_TK
    tok = lambda w_: pl.BlockSpec((1, tm, w_), lambda b, i: (b, i, 0))
    tokt = lambda w_: pl.BlockSpec((1, w_, tm), lambda b, i: (b, 0, i))
    scale = dq ** -0.5 * math.log2(math.e)
    qn, qr, kn, kr, v, z = pl.pallas_call(
        functools.partial(_mla_pre_kernel, scale=scale),
        out_shape=[jax.ShapeDtypeStruct((bsz, nope, L), BF16),
                   jax.ShapeDtypeStruct((bsz, rope, L), BF16),
                   jax.ShapeDtypeStruct((bsz, L, nope), BF16),
                   jax.ShapeDtypeStruct((bsz, L, LANES), BF16),
                   jax.ShapeDtypeStruct((bsz, L // tk, vw, tk), BF16),
                   jax.ShapeDtypeStruct((bsz, L, vw), F32)],
        grid=(bsz, L // tm),
        in_specs=[tok(d), _full((1, d)), _full(w1.shape), _full((1, MLA_Q_RANK)), _full((1, MLA_KV_RANK)),
                  _full(wqt.shape), _full(wkn.shape), _full(wvt.shape),
                  pl.BlockSpec((rope, tm), lambda b, i: (0, i)), pl.BlockSpec((rope, tm), lambda b, i: (0, i)),
                  pl.BlockSpec((tm, LANES), lambda b, i: (i, 0)), pl.BlockSpec((tm, LANES), lambda b, i: (i, 0))],
        out_specs=[tokt(nope), tokt(rope), tok(nope), tok(LANES),
                   pl.BlockSpec((1, tm // tk, vw, tk), lambda b, i: (b, i, 0, 0)), tok(vw)],
        compiler_params=_cparams(("parallel", "parallel")),
        name="mla_pre",
    )(x, pre_g.reshape(1, d), w1, q_norm.reshape(1, -1), kv_norm.reshape(1, -1), wqt, wkn, wvt,
      cos_q, sin_q, cos_k, sin_k)

    tq = MLA_TQ
    assert MLA_TQ == MLA_TK == 2 * LANES, "the diagonal-block handling works on 128-query strips of a 256 block"
    npairs = H // 2
    qspec = lambda w_: pl.BlockSpec((1, w_, tq), lambda b, i: (b, 0, i))
    kspec = lambda w_: pl.BlockSpec((1, L, w_), lambda b, i: (b, 0, 0))
    rowspec = lambda w_: pl.BlockSpec((1, tq, w_), lambda b, i: (b, i, 0))
    return pl.pallas_call(
        _mla_attn_kernel,
        out_shape=jax.ShapeDtypeStruct(x.shape, x.dtype),
        grid=(bsz, L // tq),
        in_specs=[qspec(nope), qspec(rope), kspec(nope), kspec(LANES),
                  pl.BlockSpec((1, L // tk, vw, tk), lambda b, i: (b, 0, 0, 0)),
                  rowspec(vw), rowspec(d), _full(w_out.shape), _full((1, d))],
        out_specs=rowspec(d),
        scratch_shapes=[pltpu.VMEM((npairs, 2 * LANES, 2 * tq), BF16),
                        pltpu.VMEM((npairs, LANES + 16, 2 * tq), F32),
                        pltpu.VMEM((npairs, 1, 2 * tq), F32),
                        pltpu.VMEM((tq, vw), F32)],
        compiler_params=_cparams(("parallel", "arbitrary")),
        name="mla_attn",
    )(qn, qr, kn, kr, v, z, x, w_out, post_g.reshape(1, d))


def _sgu_kernel(x_ref, g_ref, w_ref, lng_ref, lnb_ref, ws_ref, bs_ref, wo_ref, pg_ref, out_ref, s_ref):
    width = wo_ref.shape[0]
    tm = x_ref.shape[1]
    lane = lax.broadcasted_iota(jnp.int32, (1, LANES), 1)
    lo = lane < HALF
    x = x_ref[0]
    hb = _rms(x, g_ref[...]).astype(BF16)
    v = jax.nn.gelu(_dot(hb, w_ref[:, width:2 * width].astype(BF16)))
    mu = jnp.mean(v, axis=-1, keepdims=True)
    vc = v - mu
    var = jnp.mean(vc * vc, axis=-1, keepdims=True)
    vb = (vc * lax.rsqrt(var + EPS) * lng_ref[...] + lnb_ref[...]).astype(BF16)
    group = SGU_STACK
    for c0 in range(0, tm // SGU_CHUNK, group):
        for jj in range(width // LANES):
            blk = jnp.concatenate([vb[c * SGU_CHUNK:(c + 1) * SGU_CHUNK, jj * LANES:(jj + 1) * LANES]
                                   for c in range(c0, c0 + group)], axis=1)
            r = _dot(ws_ref[jj], blk)
            for k in range(group):
                c = c0 + k
                s_ref[c * SGU_CHUNK:(c + 1) * SGU_CHUNK, jj * LANES:(jj + 1) * LANES] = (
                    jnp.where(lo, r[:SGU_CHUNK, k * LANES:(k + 1) * LANES],
                              r[SGU_CHUNK:, k * LANES:(k + 1) * LANES]) + bs_ref[jj])
    u = jax.nn.gelu(_dot(hb, w_ref[:, :width].astype(BF16)))
    z = _dot(hb, w_ref[:, 2 * width:].astype(BF16))
    o = u * s_ref[...] * jax.nn.silu(z)
    r = _dot(o.astype(BF16), wo_ref[...].astype(BF16))
    out_ref[0] = x + _rms(r, pg_ref[...])


def _sgu_layer(x, pre_g, post_g, w_in, ln_g, ln_b, w_s, b_s, w_out):
    bsz, L, d = x.shape
    width = w_out.shape[0]
    T = SGU_CHUNK
    gd = width // SGU_GROUPS
    tril = jnp.tril(jnp.ones((T, T), dtype=bool))
    ws = jnp.where(tril[None], w_s, 0.0).reshape(SGU_GROUPS // 2, 2 * T, T).astype(BF16)
    bs = jnp.repeat(b_s.astype(F32).T, gd, axis=1)
    bs = bs.reshape(T, width // LANES, LANES).transpose(1, 0, 2)
    tm = ROW_TILE
    return pl.pallas_call(
        _sgu_kernel,
        out_shape=jax.ShapeDtypeStruct(x.shape, x.dtype),
        grid=(bsz, L // tm),
        in_specs=[pl.BlockSpec((1, tm, d), lambda b, i: (b, i, 0)),
                  _full((1, d)), _full(w_in.shape), _full((1, width)), _full((1, width)),
                  _full(ws.shape), _full(bs.shape), _full(w_out.shape), _full((1, d))],
        out_specs=pl.BlockSpec((1, tm, d), lambda b, i: (b, i, 0)),
        scratch_shapes=[pltpu.VMEM((tm, width), F32)],
        compiler_params=_cparams(("parallel", "parallel")),
        name="sgu",
    )(x, pre_g.reshape(1, d), w_in, ln_g.reshape(1, width), ln_b.reshape(1, width),
      ws, bs, w_out, post_g.reshape(1, d))


def kernel(x, pre_norm, post_norm, rel_bias, a_w_in, a_lam_re, a_lam_im, a_log_dt, a_b_re, a_b_im, a_c_re, a_c_im, a_d, a_w_glu, a_b_glu, a_w_out, b_w_in, b_sinks, b_w_out, c_w_in, c_q_norm, c_kv_norm, c_w_uq, c_w_ukv, c_w_out, d_w_in, d_ln_g, d_ln_b, d_w_s, d_b_s, d_w_out):
    depth = pre_norm.shape[0]
    for i in range(depth):
        kind, j = i % 4, i // 4
        if kind == 0:
            x = _s5_layer(x, pre_norm[i], post_norm[i], a_w_in[j], a_lam_re[j], a_lam_im[j], a_log_dt[j],
                          a_b_re[j], a_b_im[j], a_c_re[j], a_c_im[j], a_d[j], a_w_glu[j], a_b_glu[j],
                          a_w_out[j])
        elif kind == 1:
            x = _swa_layer(x, pre_norm[i], post_norm[i], b_w_in[j], b_sinks[j], b_w_out[j], rel_bias)
        elif kind == 2:
            x = _mla_layer(x, pre_norm[i], post_norm[i], c_w_in[j], c_q_norm[j], c_kv_norm[j], c_w_uq[j],
                           c_w_ukv[j], c_w_out[j])
        else:
            x = _sgu_layer(x, pre_norm[i], post_norm[i], d_w_in[j], d_ln_g[j], d_ln_b[j], d_w_s[j],
                           d_b_s[j], d_w_out[j])
    return x
```

```python
import functools
import math

import jax
import jax.numpy as jnp
import numpy as np
from jax import lax
from jax.experimental import pallas as pl
from jax.experimental.pallas import tpu as pltpu

F32 = jnp.float32
BF16 = jnp.bfloat16

EPS = 1e-6
NEG_INF = -1e30
LANES = 128
HALF = LANES // 2

SSM_GROUP = 16
SSM_STATE = 64
S5_CH_BLOCK = LANES
S5_GROUPS_PER_BLOCK = S5_CH_BLOCK // SSM_GROUP
S5_STATE_BLOCK = S5_GROUPS_PER_BLOCK * SSM_STATE
S5_T = 64

HEAD_DIM = 64
SWA_HEADS = 16
SWA_KV_HEADS = 2
SWA_GROUP = SWA_HEADS // SWA_KV_HEADS
WINDOW = 128
SWA_WINDOWS_PER_STEP = 4
SWA_UNIT_HEADS = 8
SWA_LOOKAHEAD = 2
SWA_PV_DELAY = 1
REL_BUCKETS = 32
REL_MAX_DIST = 128

MLA_HEADS = 16
MLA_NOPE = 64
MLA_ROPE = 32
MLA_V = 64
MLA_KV_RANK = 256
MLA_Q_RANK = 768
ROPE_BASE = 10000.0
MLA_TQ = 256
MLA_TK = 256
MLA_LOOKAHEAD = 6
MLA_PV_DELAY = 2

SGU_CHUNK = 128
SGU_GROUPS = 16
SGU_STACK = 4

ROW_TILE = 1024
VMEM_LIMIT = 56 * 1024 * 1024


def _cparams(sem):
    return pltpu.CompilerParams(dimension_semantics=sem, vmem_limit_bytes=VMEM_LIMIT)


def _rms(x, g):
    return x * lax.rsqrt(jnp.mean(x * x, axis=-1, keepdims=True) + EPS) * g


def _dot(a, b):
    return jnp.dot(a, b, preferred_element_type=F32)


def _dot_nt(a, b):
    return lax.dot_general(a, b, (((1,), (1,)), ((), ())), preferred_element_type=F32)


def _full(shape):
    n = len(shape)
    return pl.BlockSpec(shape, lambda *_: (0,) * n, pipeline_mode=pl.Buffered(1))


def _s5_kernel(x_ref, g_ref, w_ref, bb_ref, cc_ref, ar_ref, ai_ref, d_ref,
               wg_ref, bg_ref, wo_ref, pg_ref, out_ref, u_ref, z_ref, y_ref, s_ref, carry_ref, *, tt):
    bsz = x_ref.shape[0]
    width = wg_ref.shape[0]
    rows = bsz * tt
    nblk = bb_ref.shape[0]
    sb = S5_STATE_BLOCK

    @pl.when(pl.program_id(0) == 0)
    def _():
        carry_ref[...] = jnp.zeros_like(carry_ref)

    x = x_ref[...].reshape(rows, x_ref.shape[2])
    h = _rms(x, g_ref[...])
    hb = jnp.swapaxes(h.reshape(bsz, tt, h.shape[1]), 0, 1).reshape(rows, h.shape[1]).astype(BF16)
    u_ref[...] = _dot(hb, w_ref[:, :width].astype(BF16))
    z_ref[...] = _dot(hb, w_ref[:, width:].astype(BF16))

    nbuf = s_ref.shape[0]

    def project_in(i):
        s_ref[i % nbuf] = _dot(u_ref[:, i * LANES:(i + 1) * LANES].astype(BF16), bb_ref[i])

    def project_out(i):
        ub = u_ref[:, i * LANES:(i + 1) * LANES]
        y = _dot(s_ref[i % nbuf].astype(BF16), cc_ref[i]) + d_ref[:, i * LANES:(i + 1) * LANES] * ub
        y_ref[:, i * LANES:(i + 1) * LANES] = jax.nn.gelu(y)

    project_in(0)
    for i in range(nblk):
        if i + 1 < nblk:
            project_in(i + 1)
        buf = s_ref.at[i % nbuf]
        ar = ar_ref[i]
        ai = ai_ref[i]
        sr = carry_ref[i, :, 0:sb]
        si = carry_ref[i, :, sb:2 * sb]
        for t in range(tt):
            r0 = t * bsz
            nr = ar * sr - ai * si + buf[r0:r0 + bsz, 0:sb]
            ni = ar * si + ai * sr + buf[r0:r0 + bsz, sb:2 * sb]
            buf[r0:r0 + bsz, 0:sb] = nr
            buf[r0:r0 + bsz, sb:2 * sb] = ni
            sr, si = nr, ni
        carry_ref[i, :, 0:sb] = sr
        carry_ref[i, :, sb:2 * sb] = si
        project_out(i)

    y = y_ref[...]
    gate = jax.nn.sigmoid(_dot(y.astype(BF16), wg_ref[...].astype(BF16)) + bg_ref[...])
    o = y * gate * jax.nn.silu(z_ref[...])
    ob = jnp.swapaxes(o.reshape(tt, bsz, o.shape[1]), 0, 1).reshape(rows, o.shape[1]).astype(BF16)
    r = _dot(ob, wo_ref[...].astype(BF16))
    out_ref[...] = (x + _rms(r, pg_ref[...])).reshape(out_ref.shape)


def _s5_discretize(lam_re, lam_im, log_dt, b_re, b_im):
    dt = jnp.exp(log_dt)[:, None]
    mag = jnp.exp(lam_re * dt)
    ab_re = mag * jnp.cos(lam_im * dt)
    ab_im = mag * jnp.sin(lam_im * dt)
    den = lam_re * lam_re + lam_im * lam_im
    nr = ab_re - 1.0
    f_re = (nr * lam_re + ab_im * lam_im) / den
    f_im = (ab_im * lam_re - nr * lam_im) / den
    bb_re = f_re[..., None] * b_re - f_im[..., None] * b_im
    bb_im = f_re[..., None] * b_im + f_im[..., None] * b_re
    return ab_re, ab_im, bb_re, bb_im


def _s5_layer(x, pre_g, post_g, w_in, lam_re, lam_im, log_dt, b_re, b_im, c_re, c_im, d_skip,
              w_glu, b_glu, w_out):
    bsz, L, d = x.shape
    width = w_in.shape[1] // 2
    nblk = width // S5_CH_BLOCK
    gpb = S5_GROUPS_PER_BLOCK
    tt = S5_T
    rows = bsz * tt

    ab_re, ab_im, bb_re, bb_im = _s5_discretize(lam_re, lam_im, log_dt, b_re, b_im)
    eye = jnp.eye(gpb, dtype=F32)

    def pack_b(bb):
        t = bb.reshape(nblk, gpb, SSM_STATE, SSM_GROUP)
        return jnp.einsum('igph,gk->ikhgp', t, eye).reshape(nblk, S5_CH_BLOCK, S5_STATE_BLOCK)

    def pack_c(cc):
        t = cc.reshape(nblk, gpb, SSM_GROUP, SSM_STATE)
        return jnp.einsum('ighp,gk->igpkh', t, eye).reshape(nblk, S5_STATE_BLOCK, S5_CH_BLOCK)

    bb = jnp.concatenate([pack_b(bb_re), pack_b(bb_im)], axis=2).astype(BF16)
    cc = jnp.concatenate([pack_c(c_re), -pack_c(c_im)], axis=1).astype(BF16)
    ar = jnp.broadcast_to(ab_re.reshape(nblk, 1, S5_STATE_BLOCK), (nblk, bsz, S5_STATE_BLOCK))
    ai = jnp.broadcast_to(ab_im.reshape(nblk, 1, S5_STATE_BLOCK), (nblk, bsz, S5_STATE_BLOCK))

    xspec = pl.BlockSpec((bsz, tt, d), lambda i: (0, i, 0))
    return pl.pallas_call(
        functools.partial(_s5_kernel, tt=tt),
        out_shape=jax.ShapeDtypeStruct(x.shape, x.dtype),
        grid=(L // tt,),
        in_specs=[xspec, _full((1, d)), _full(w_in.shape),
                  _full(bb.shape), _full(cc.shape), _full(ar.shape), _full(ai.shape), _full((1, width)),
                  _full(w_glu.shape), _full((1, width)), _full(w_out.shape), _full((1, d))],
        out_specs=xspec,
        scratch_shapes=[pltpu.VMEM((rows, width), F32),
                        pltpu.VMEM((rows, width), F32),
                        pltpu.VMEM((rows, width), F32),
                        pltpu.VMEM((2, rows, 2 * S5_STATE_BLOCK), F32),
                        pltpu.VMEM((nblk, bsz, 2 * S5_STATE_BLOCK), F32)],
        compiler_params=_cparams(("arbitrary",)),
        name="s5_layer",
    )(x, pre_g.reshape(1, d), w_in, bb, cc, ar, ai, d_skip.reshape(1, width),
      w_glu, b_glu.reshape(1, width), w_out, post_g.reshape(1, d))


def _swa_bias(rel_bias):
    W = WINDOW
    n = 4 * W
    dist = jnp.arange(n) - W
    valid = jnp.logical_and(dist >= 0, dist < W)
    dpos = jnp.maximum(dist, 0)
    max_exact = REL_BUCKETS // 2
    dist_f = jnp.maximum(dpos, 1).astype(F32)
    large = max_exact + (jnp.log(dist_f / max_exact) / math.log(REL_MAX_DIST / max_exact)
                         * (REL_BUCKETS - max_exact)).astype(jnp.int32)
    large = jnp.minimum(large, REL_BUCKETS - 1)
    bucket = jnp.where(dpos < max_exact, dpos, large)
    return jnp.where(valid[:, None], rel_bias[bucket].astype(F32), NEG_INF).T


def _swa_kernel(x_ref, pg_ref, wqt_ref, wk_ref, wvt_ref, wz_ref, bvec_ref, sink_ref, wo_ref, g_ref,
                out_ref, ot_ref, bias_ref, kprev_ref, vtprev_ref, *, scale):
    W = WINDOW
    nwin = x_ref.shape[1] // W
    step = pl.program_id(1)

    @pl.when(step == 0)
    def _():
        kprev_ref[...] = jnp.zeros_like(kprev_ref)
        vtprev_ref[...] = jnp.zeros_like(vtprev_ref)

    x = x_ref[0]
    hb = _rms(x, pg_ref[...]).astype(BF16)
    qt = (_dot_nt(wqt_ref[...], hb) * scale).astype(BF16)
    k = _dot(hb, wk_ref[...]).astype(BF16)
    vt = _dot_nt(wvt_ref[...], hb).astype(BF16)
    z = _dot(hb, wz_ref[...])

    @pl.when(jnp.logical_and(pl.program_id(0) == 0, step == 0))
    def _():
        no_prev = lax.broadcasted_iota(jnp.int32, (2 * W, W), 0) < W
        for hd in range(SWA_HEADS):
            base = jnp.broadcast_to(bvec_ref[hd:hd + 1, :], (2 * W, bvec_ref.shape[1]))
            toep = pltpu.roll(base, 0, 1, stride=1, stride_axis=0)[:, 2 * W:3 * W]
            h, g = divmod(hd, SWA_GROUP)
            bias_ref[0, h, :, g * W:(g + 1) * W] = toep
            bias_ref[1, h, :, g * W:(g + 1) * W] = jnp.where(no_prev, NEG_INF, toep)
    kall = jnp.concatenate([kprev_ref[...], k], axis=0)
    vtall = jnp.concatenate([vtprev_ref[...], vt], axis=1)
    kprev_ref[...] = k[(nwin - 1) * W:]
    vtprev_ref[...] = vt[:, (nwin - 1) * W:]
    nsub = SWA_UNIT_HEADS
    zq = jnp.zeros((HEAD_DIM, nsub * W), BF16)
    ones = jnp.ones((16, 2 * W), BF16)
    units = [(w, h, c) for w in range(nwin) for h in range(SWA_KV_HEADS) for c in range(SWA_GROUP // nsub)]

    def scores(w, h, c):
        hd0 = h * SWA_GROUP + c * nsub
        qh = jnp.concatenate([qt[(hd0 + g) * HEAD_DIM:(hd0 + g + 1) * HEAD_DIM, w * W:(w + 1) * W]
                              for g in range(nsub)], axis=1)
        qz = jnp.concatenate([qh, zq] if h == 0 else [zq, qh], axis=0)
        return _dot(kall[w * W:(w + 2) * W], qz)

    pending = [scores(*u) for u in units[:SWA_LOOKAHEAD]]
    late = []

    def flush():
        (w, h, c), p, tail = late.pop(0)
        vones = jnp.concatenate([vtall[h * HEAD_DIM:(h + 1) * HEAD_DIM, w * W:(w + 2) * W], ones], axis=0)
        o = _dot(vones, p)
        oh = o[:HEAD_DIM] * (1.0 / (o[HEAD_DIM:HEAD_DIM + 1] + tail))
        for g in range(nsub):
            hd = h * SWA_GROUP + c * nsub + g
            ot_ref[hd * HEAD_DIM:(hd + 1) * HEAD_DIM, w * W:(w + 1) * W] = oh[:, g * W:(g + 1) * W]

    for idx, (w, h, c) in enumerate(units):
        raw = pending.pop(0)
        if idx + SWA_LOOKAHEAD < len(units):
            pending.append(scores(*units[idx + SWA_LOOKAHEAD]))
        cols = slice(c * nsub * W, (c + 1) * nsub * W)
        variant = (step == 0).astype(jnp.int32) if w == 0 else 0
        s = raw + bias_ref[variant, h, :, cols]
        sink = sink_ref[h, :, cols]
        m = jnp.maximum(jnp.max(s, axis=0, keepdims=True), sink)
        if len(late) == SWA_PV_DELAY:
            flush()
        late.append(((w, h, c), jnp.exp2(s - m).astype(BF16), jnp.exp2(sink - m)))
    while late:
        flush()
    gated = ot_ref[...].T * jax.nn.silu(z)
    r = _dot(gated.astype(BF16), wo_ref[...].astype(BF16))
    out_ref[0] = x + _rms(r, g_ref[...])


def _swa_layer(x, pre_g, post_g, w_in, sinks, w_out, rel_bias):
    bsz, L, d = x.shape
    width = SWA_HEADS * HEAD_DIM
    kvw = SWA_KV_HEADS * HEAD_DIM
    W = WINDOW
    log2e = math.log2(math.e)
    wb = w_in.astype(BF16)
    wqt = wb[:, :width].T
    wk = wb[:, width:width + kvw]
    wvt = wb[:, width + kvw:width + 2 * kvw].T
    wz = wb[:, width + 2 * kvw:]
    bvec = _swa_bias(rel_bias.astype(F32) * log2e)
    sink = jnp.repeat(sinks.astype(F32) * log2e, W).reshape(SWA_KV_HEADS, 1, SWA_GROUP * W)

    tq = SWA_WINDOWS_PER_STEP * W
    xspec = pl.BlockSpec((1, tq, d), lambda b, n: (b, n, 0))
    return pl.pallas_call(
        functools.partial(_swa_kernel, scale=HEAD_DIM ** -0.5 * log2e),
        out_shape=jax.ShapeDtypeStruct(x.shape, x.dtype),
        grid=(bsz, L // tq),
        in_specs=[xspec, _full((1, d)), _full(wqt.shape), _full(wk.shape), _full(wvt.shape), _full(wz.shape),
                  _full(bvec.shape), _full(sink.shape), _full(w_out.shape), _full((1, d))],
        out_specs=xspec,
        scratch_shapes=[pltpu.VMEM((width, tq), F32),
                        pltpu.VMEM((2, SWA_KV_HEADS, 2 * W, SWA_GROUP * W), F32),
                        pltpu.VMEM((W, kvw), BF16),
                        pltpu.VMEM((kvw, W), BF16)],
        compiler_params=_cparams(("arbitrary", "arbitrary")),
        name="swa_layer",
    )(x, pre_g.reshape(1, d), wqt, wk, wvt, wz, bvec, sink, w_out, post_g.reshape(1, d))


def _mla_pre_kernel(x_ref, g_ref, w_ref, qn_ref, kvn_ref, wq_ref, wkv_ref, wvt_ref, cq_ref, sq_ref, ck_ref, sk_ref,
                    oqn_ref, oqr_ref, okn_ref, okr_ref, ov_ref, oz_ref, *, scale):
    nope = MLA_HEADS * MLA_NOPE
    rope = MLA_HEADS * MLA_ROPE
    vw = MLA_HEADS * MLA_V
    hb = _rms(x_ref[0], g_ref[...]).astype(BF16)
    o1 = MLA_Q_RANK
    o2 = o1 + MLA_KV_RANK
    o3 = o2 + vw
    cq = _dot(hb, w_ref[:, :o1])
    ckv = _dot(hb, w_ref[:, o1:o2])
    oz_ref[0] = _dot(hb, w_ref[:, o2:o3])
    kr = _dot(hb, w_ref[:, o3:o3 + LANES])
    krs = _dot(hb, w_ref[:, o3 + LANES:o3 + 2 * LANES])
    okr_ref[0] = (kr * ck_ref[...] + krs * sk_ref[...]).astype(BF16)
    cqb = _rms(cq, qn_ref[...]).astype(BF16)
    oqn_ref[0] = (_dot_nt(wq_ref[:nope], cqb) * scale).astype(BF16)
    qr = _dot_nt(wq_ref[nope:nope + rope], cqb)
    hr = MLA_ROPE // 2
    qrs = jnp.concatenate([qr[h * MLA_ROPE + o:h * MLA_ROPE + o + hr]
                           for h in range(MLA_HEADS) for o in (hr, 0)], axis=0)
    oqr_ref[0] = ((qr * cq_ref[...] + qrs * sq_ref[...]) * scale).astype(BF16)
    ckb = _rms(ckv, kvn_ref[...]).astype(BF16)
    okn_ref[0] = _dot(ckb, wkv_ref[:, :nope]).astype(BF16)
    vt = _dot_nt(wvt_ref[...], ckb).astype(BF16)
    tk = ov_ref.shape[3]
    for c in range(ov_ref.shape[1]):
        ov_ref[0, c] = vt[:, c * tk:(c + 1) * tk]


def _mla_attn_kernel(qn_ref, qr_ref, kn_ref, kr_ref, v_ref, z_ref, x_ref, wo_ref, g_ref, out_ref,
                     qs_ref, acc_ref, m_ref, o_ref):
    tq = qn_ref.shape[2]
    tk = v_ref.shape[3]
    npairs = MLA_HEADS // 2
    i = pl.program_id(1)
    tri = (lax.broadcasted_iota(jnp.int32, (LANES, LANES), 0)
           <= lax.broadcasted_iota(jnp.int32, (LANES, LANES), 1))

    zn = jnp.zeros((MLA_NOPE, tq), BF16)
    zr = jnp.zeros((LANES - MLA_ROPE, tq), BF16)
    for p in range(npairs):
        qn = qn_ref[0, p * LANES:(p + 1) * LANES, :]
        r0 = 2 * p * MLA_ROPE
        c0 = jnp.concatenate([qn[:MLA_NOPE], zn, qr_ref[0, r0:r0 + MLA_ROPE, :], zr], axis=0)
        c1 = jnp.concatenate([zn, qn[MLA_NOPE:], qr_ref[0, r0 + MLA_ROPE:r0 + 2 * MLA_ROPE, :], zr], axis=0)
        qs_ref[p] = jnp.concatenate([c0, c1], axis=1)

    m_ref[...] = jnp.full(m_ref.shape, NEG_INF, F32)
    acc_ref[...] = jnp.zeros(acc_ref.shape, F32)
    ones = jnp.ones((acc_ref.shape[1] - LANES, tk), BF16)

    def kv_steps(blocks):
        units = [(j, masked, p) for j, masked in blocks for p in range(npairs)]

        def scores(j, masked, p):
            ks = pl.multiple_of(j * tk, tk)
            kc = jnp.concatenate([kn_ref[0, pl.ds(ks, tk), p * LANES:(p + 1) * LANES],
                                  kr_ref[0, pl.ds(ks, tk), :]], axis=1)
            if not masked:
                chunks = [_dot(kc, qs_ref[p, :, c * 2 * LANES:(c + 1) * 2 * LANES]) for c in range(tq // LANES)]
                return [chunks[c // 2][:, (c % 2) * LANES:(c % 2 + 1) * LANES] for c in range(2 * tq // LANES)]
            lo = [_dot(kc[:LANES], qs_ref[p, :, c * 2 * LANES:(c + 1) * 2 * LANES]) for c in range(tq // LANES)]
            late_q = jnp.concatenate([qs_ref[p, :, LANES:2 * LANES], qs_ref[p, :, 3 * LANES:4 * LANES]], axis=1)
            hi = _dot(kc[LANES:], late_q)
            strips = []
            for hd in range(2):
                strips.append(jnp.where(tri, lo[hd][:, :LANES], NEG_INF))
                strips.append(jnp.concatenate([lo[hd][:, LANES:],
                                               jnp.where(tri, hi[:, hd * LANES:(hd + 1) * LANES], NEG_INF)], axis=0))
            return strips

        pending = [scores(*u) for u in units[:MLA_LOOKAHEAD]]
        late = []

        def flush():
            jj, pp, alpha, pr = late.pop(0)
            vones = jnp.concatenate([v_ref[0, jj, pp * LANES:(pp + 1) * LANES, :], ones], axis=0)
            acc_ref[pp] = alpha * acc_ref[pp] + _dot(vones, pr)

        for idx, (j, masked, p) in enumerate(units):
            s = pending.pop(0)
            if idx + MLA_LOOKAHEAD < len(units):
                pending.append(scores(*units[idx + MLA_LOOKAHEAD]))
            probs, alphas = [], []
            for c, sc in enumerate(s):
                m_prev = m_ref[p, :, c * LANES:(c + 1) * LANES]
                m_new = jnp.maximum(m_prev, jnp.max(sc, axis=0, keepdims=True))
                alphas.append(jnp.exp2(m_prev - m_new))
                pr = jnp.exp2(sc - m_new).astype(BF16)
                if pr.shape[0] < tk:
                    pr = jnp.concatenate([pr, jnp.zeros((tk - pr.shape[0], LANES), BF16)], axis=0)
                probs.append(pr)
                m_ref[p, :, c * LANES:(c + 1) * LANES] = m_new
            if len(late) == MLA_PV_DELAY:
                flush()
            late.append((j, p, jnp.concatenate(alphas, axis=1), jnp.concatenate(probs, axis=1)))
        while late:
            flush()

    def body(jj, c):
        kv_steps([(2 * jj, False), (2 * jj + 1, False)])
        return c

    lax.fori_loop(0, i // 2, body, 0)

    @pl.when(i % 2 == 1)
    def _():
        kv_steps([(i - 1, False), (i, True)])

    @pl.when(i % 2 == 0)
    def _():
        kv_steps([(i, True)])
    for p in range(npairs):
        a = acc_ref[p]
        a = a[:LANES] * (1.0 / a[LANES:LANES + 1])
        ot = jnp.concatenate([a[:MLA_V, :tq], a[MLA_V:, tq:]], axis=0)
        o_ref[:, p * LANES:(p + 1) * LANES] = ot.T
    gated = o_ref[...] * jax.nn.silu(z_ref[0])
    r = _dot(gated.astype(BF16), wo_ref[...].astype(BF16))
    out_ref[0] = x_ref[0] + _rms(r, g_ref[...])


def _mla_layer(x, pre_g, post_g, w_in, q_norm, kv_norm, w_uq, w_ukv, w_out):
    bsz, L, d = x.shape
    H = MLA_HEADS
    dq = MLA_NOPE + MLA_ROPE
    nope = H * MLA_NOPE
    rope = H * MLA_ROPE
    vw = H * MLA_V
    half = MLA_ROPE // 2
    o_kr = MLA_Q_RANK + MLA_KV_RANK
    o_z = o_kr + MLA_ROPE
    wb = w_in.astype(BF16)
    w_kr = wb[:, o_kr:o_z]
    w_krs = jnp.concatenate([w_kr[:, half:], w_kr[:, :half]], axis=1)
    reps = LANES // MLA_ROPE
    w1 = jnp.concatenate([wb[:, :o_kr], wb[:, o_z:]] + [w_kr] * reps + [w_krs] * reps, axis=1)
    wq3 = w_uq.astype(BF16).reshape(MLA_Q_RANK, H, dq)
    wqt = jnp.concatenate([wq3[:, :, :MLA_NOPE].reshape(MLA_Q_RANK, nope),
                           wq3[:, :, MLA_NOPE:].reshape(MLA_Q_RANK, rope)], axis=1).T
    wkv3 = w_ukv.astype(BF16).reshape(MLA_KV_RANK, H, MLA_NOPE + MLA_V)
    wkn = wkv3[:, :, :MLA_NOPE].reshape(MLA_KV_RANK, nope)
    wvt = wkv3[:, :, MLA_NOPE:].reshape(MLA_KV_RANK, vw).T
    inv = ROPE_BASE ** (-jnp.arange(0, MLA_ROPE, 2, dtype=F32) / MLA_ROPE)
    ang = jnp.arange(L, dtype=F32)[:, None] * inv[None, :]
    cos, sin = jnp.cos(ang), jnp.sin(ang)
    cos32 = jnp.concatenate([cos, cos], axis=1)
    sin32 = jnp.concatenate([-sin, sin], axis=1)
    cos_k, sin_k = jnp.tile(cos32, (1, LANES // MLA_ROPE)), jnp.tile(sin32, (1, LANES // MLA_ROPE))
    cos_q, sin_q = jnp.tile(cos32, (1, H)).T, jnp.tile(sin32, (1, H)).T

    tm = ROW_TILE
    tk = MLA_TK
    tok = lambda w_: pl.BlockSpec((1, tm, w_), lambda b, i: (b, i, 0))
    tokt = lambda w_: pl.BlockSpec((1, w_, tm), lambda b, i: (b, 0, i))
    scale = dq ** -0.5 * math.log2(math.e)
    qn, qr, kn, kr, v, z = pl.pallas_call(
        functools.partial(_mla_pre_kernel, scale=scale),
        out_shape=[jax.ShapeDtypeStruct((bsz, nope, L), BF16),
                   jax.ShapeDtypeStruct((bsz, rope, L), BF16),
                   jax.ShapeDtypeStruct((bsz, L, nope), BF16),
                   jax.ShapeDtypeStruct((bsz, L, LANES), BF16),
                   jax.ShapeDtypeStruct((bsz, L // tk, vw, tk), BF16),
                   jax.ShapeDtypeStruct((bsz, L, vw), F32)],
        grid=(bsz, L // tm),
        in_specs=[tok(d), _full((1, d)), _full(w1.shape), _full((1, MLA_Q_RANK)), _full((1, MLA_KV_RANK)),
                  _full(wqt.shape), _full(wkn.shape), _full(wvt.shape),
                  pl.BlockSpec((rope, tm), lambda b, i: (0, i)), pl.BlockSpec((rope, tm), lambda b, i: (0, i)),
                  pl.BlockSpec((tm, LANES), lambda b, i: (i, 0)), pl.BlockSpec((tm, LANES), lambda b, i: (i, 0))],
        out_specs=[tokt(nope), tokt(rope), tok(nope), tok(LANES),
                   pl.BlockSpec((1, tm // tk, vw, tk), lambda b, i: (b, i, 0, 0)), tok(vw)],
        compiler_params=_cparams(("parallel", "parallel")),
        name="mla_pre",
    )(x, pre_g.reshape(1, d), w1, q_norm.reshape(1, -1), kv_norm.reshape(1, -1), wqt, wkn, wvt,
      cos_q, sin_q, cos_k, sin_k)

    tq = MLA_TQ
    assert MLA_TQ == MLA_TK == 2 * LANES, "the diagonal-block handling works on 128-query strips of a 256 block"
    npairs = H // 2
    qspec = lambda w_: pl.BlockSpec((1, w_, tq), lambda b, i: (b, 0, i))
    kspec = lambda w_: pl.BlockSpec((1, L, w_), lambda b, i: (b, 0, 0))
    rowspec = lambda w_: pl.BlockSpec((1, tq, w_), lambda b, i: (b, i, 0))
    return pl.pallas_call(
        _mla_attn_kernel,
        out_shape=jax.ShapeDtypeStruct(x.shape, x.dtype),
        grid=(bsz, L // tq),
        in_specs=[qspec(nope), qspec(rope), kspec(nope), kspec(LANES),
                  pl.BlockSpec((1, L // tk, vw, tk), lambda b, i: (b, 0, 0, 0)),
                  rowspec(vw), rowspec(d), _full(w_out.shape), _full((1, d))],
        out_specs=rowspec(d),
        scratch_shapes=[pltpu.VMEM((npairs, 2 * LANES, 2 * tq), BF16),
                        pltpu.VMEM((npairs, LANES + 16, 2 * tq), F32),
                        pltpu.VMEM((npairs, 1, 2 * tq), F32),
                        pltpu.VMEM((tq, vw), F32)],
        compiler_params=_cparams(("parallel", "arbitrary")),
        name="mla_attn",
    )(qn, qr, kn, kr, v, z, x, w_out, post_g.reshape(1, d))


def _sgu_kernel(x_ref, g_ref, w_ref, lng_ref, lnb_ref, ws_ref, bs_ref, wo_ref, pg_ref, out_ref, s_ref):
    width = wo_ref.shape[0]
    tm = x_ref.shape[1]
    lane = lax.broadcasted_iota(jnp.int32, (1, LANES), 1)
    lo = lane < HALF
    x = x_ref[0]
    hb = _rms(x, g_ref[...]).astype(BF16)
    v = jax.nn.gelu(_dot(hb, w_ref[:, width:2 * width].astype(BF16)))
    mu = jnp.mean(v, axis=-1, keepdims=True)
    vc = v - mu
    var = jnp.mean(vc * vc, axis=-1, keepdims=True)
    vb = (vc * lax.rsqrt(var + EPS) * lng_ref[...] + lnb_ref[...]).astype(BF16)
    group = SGU_STACK
    for c0 in range(0, tm // SGU_CHUNK, group):
        for jj in range(width // LANES):
            blk = jnp.concatenate([vb[c * SGU_CHUNK:(c + 1) * SGU_CHUNK, jj * LANES:(jj + 1) * LANES]
                                   for c in range(c0, c0 + group)], axis=1)
            r = _dot(ws_ref[jj], blk)
            for k in range(group):
                c = c0 + k
                s_ref[c * SGU_CHUNK:(c + 1) * SGU_CHUNK, jj * LANES:(jj + 1) * LANES] = (
                    jnp.where(lo, r[:SGU_CHUNK, k * LANES:(k + 1) * LANES],
                              r[SGU_CHUNK:, k * LANES:(k + 1) * LANES]) + bs_ref[jj])
    u = jax.nn.gelu(_dot(hb, w_ref[:, :width].astype(BF16)))
    z = _dot(hb, w_ref[:, 2 * width:].astype(BF16))
    o = u * s_ref[...] * jax.nn.silu(z)
    r = _dot(o.astype(BF16), wo_ref[...].astype(BF16))
    out_ref[0] = x + _rms(r, pg_ref[...])


def _sgu_layer(x, pre_g, post_g, w_in, ln_g, ln_b, w_s, b_s, w_out):
    bsz, L, d = x.shape
    width = w_out.shape[0]
    T = SGU_CHUNK
    gd = width // SGU_GROUPS
    tril = jnp.tril(jnp.ones((T, T), dtype=bool))
    ws = jnp.where(tril[None], w_s, 0.0).reshape(SGU_GROUPS // 2, 2 * T, T).astype(BF16)
    bs = jnp.repeat(b_s.astype(F32).T, gd, axis=1)
    bs = bs.reshape(T, width // LANES, LANES).transpose(1, 0, 2)
    tm = ROW_TILE
    return pl.pallas_call(
        _sgu_kernel,
        out_shape=jax.ShapeDtypeStruct(x.shape, x.dtype),
        grid=(bsz, L // tm),
        in_specs=[pl.BlockSpec((1, tm, d), lambda b, i: (b, i, 0)),
                  _full((1, d)), _full(w_in.shape), _full((1, width)), _full((1, width)),
                  _full(ws.shape), _full(bs.shape), _full(w_out.shape), _full((1, d))],
        out_specs=pl.BlockSpec((1, tm, d), lambda b, i: (b, i, 0)),
        scratch_shapes=[pltpu.VMEM((tm, width), F32)],
        compiler_params=_cparams(("parallel", "parallel")),
        name="sgu",
    )(x, pre_g.reshape(1, d), w_in, ln_g.reshape(1, width), ln_b.reshape(1, width),
      ws, bs, w_out, post_g.reshape(1, d))


def kernel(x, pre_norm, post_norm, rel_bias, a_w_in, a_lam_re, a_lam_im, a_log_dt, a_b_re, a_b_im, a_c_re, a_c_im, a_d, a_w_glu, a_b_glu, a_w_out, b_w_in, b_sinks, b_w_out, c_w_in, c_q_norm, c_kv_norm, c_w_uq, c_w_ukv, c_w_out, d_w_in, d_ln_g, d_ln_b, d_w_s, d_b_s, d_w_out):
    depth = pre_norm.shape[0]
    for i in range(depth):
        kind, j = i % 4, i // 4
        if kind == 0:
            x = _s5_layer(x, pre_norm[i], post_norm[i], a_w_in[j], a_lam_re[j], a_lam_im[j], a_log_dt[j],
                          a_b_re[j], a_b_im[j], a_c_re[j], a_c_im[j], a_d[j], a_w_glu[j], a_b_glu[j],
                          a_w_out[j])
        elif kind == 1:
            x = _swa_layer(x, pre_norm[i], post_norm[i], b_w_in[j], b_sinks[j], b_w_out[j], rel_bias)
        elif kind == 2:
            x = _mla_layer(x, pre_norm[i], post_norm[i], c_w_in[j], c_q_norm[j], c_kv_norm[j], c_w_uq[j],
                           c_w_ukv[j], c_w_out[j])
        else:
            x = _sgu_layer(x, pre_norm[i], post_norm[i], d_w_in[j], d_ln_g[j], d_ln_b[j], d_w_s[j],
                           d_b_s[j], d_w_out[j])
    return x
```

```python
import functools
import math

import jax
import jax.numpy as jnp
import numpy as np
from jax import lax
from jax.experimental import pallas as pl
from jax.experimental.pallas import tpu as pltpu

F32 = jnp.float32
BF16 = jnp.bfloat16

EPS = 1e-6
NEG_INF = -1e30
LANES = 128
HALF = LANES // 2

SSM_GROUP = 16
SSM_STATE = 64
S5_CH_BLOCK = LANES
S5_GROUPS_PER_BLOCK = S5_CH_BLOCK // SSM_GROUP
S5_STATE_BLOCK = S5_GROUPS_PER_BLOCK * SSM_STATE
S5_T = 64

HEAD_DIM = 64
SWA_HEADS = 16
SWA_KV_HEADS = 2
SWA_GROUP = SWA_HEADS // SWA_KV_HEADS
WINDOW = 128
SWA_WINDOWS_PER_STEP = 8
SWA_UNIT_HEADS = 8
SWA_LOOKAHEAD = 2
SWA_PV_DELAY = 1
REL_BUCKETS = 32
REL_MAX_DIST = 128

MLA_HEADS = 16
MLA_NOPE = 64
MLA_ROPE = 32
MLA_V = 64
MLA_KV_RANK = 256
MLA_Q_RANK = 768
ROPE_BASE = 10000.0
MLA_TQ = 256
MLA_TK = 256
MLA_LOOKAHEAD = 6
MLA_PV_DELAY = 2

SGU_CHUNK = 128
SGU_GROUPS = 16
SGU_STACK = 4

ROW_TILE = 1024
VMEM_LIMIT = 56 * 1024 * 1024


def _cparams(sem):
    return pltpu.CompilerParams(dimension_semantics=sem, vmem_limit_bytes=VMEM_LIMIT)


def _rms(x, g):
    return x * lax.rsqrt(jnp.mean(x * x, axis=-1, keepdims=True) + EPS) * g


def _dot(a, b):
    return jnp.dot(a, b, preferred_element_type=F32)


def _dot_nt(a, b):
    return lax.dot_general(a, b, (((1,), (1,)), ((), ())), preferred_element_type=F32)


def _full(shape):
    n = len(shape)
    return pl.BlockSpec(shape, lambda *_: (0,) * n, pipeline_mode=pl.Buffered(1))


def _s5_kernel(x_ref, g_ref, w_ref, bb_ref, cc_ref, ar_ref, ai_ref, d_ref,
               wg_ref, bg_ref, wo_ref, pg_ref, out_ref, u_ref, z_ref, y_ref, s_ref, carry_ref, *, tt):
    bsz = x_ref.shape[0]
    width = wg_ref.shape[0]
    rows = bsz * tt
    nblk = bb_ref.shape[0]
    sb = S5_STATE_BLOCK

    @pl.when(pl.program_id(0) == 0)
    def _():
        carry_ref[...] = jnp.zeros_like(carry_ref)

    x = x_ref[...].reshape(rows, x_ref.shape[2])
    h = _rms(x, g_ref[...])
    hb = jnp.swapaxes(h.reshape(bsz, tt, h.shape[1]), 0, 1).reshape(rows, h.shape[1]).astype(BF16)
    u_ref[...] = _dot(hb, w_ref[:, :width].astype(BF16))
    z_ref[...] = _dot(hb, w_ref[:, width:].astype(BF16))

    nbuf = s_ref.shape[0]

    def project_in(i):
        s_ref[i % nbuf] = _dot(u_ref[:, i * LANES:(i + 1) * LANES].astype(BF16), bb_ref[i])

    def project_out(i):
        ub = u_ref[:, i * LANES:(i + 1) * LANES]
        y = _dot(s_ref[i % nbuf].astype(BF16), cc_ref[i]) + d_ref[:, i * LANES:(i + 1) * LANES] * ub
        y_ref[:, i * LANES:(i + 1) * LANES] = jax.nn.gelu(y)

    project_in(0)
    for i in range(nblk):
        if i + 1 < nblk:
            project_in(i + 1)
        buf = s_ref.at[i % nbuf]
        ar = ar_ref[i]
        ai = ai_ref[i]
        sr = carry_ref[i, :, 0:sb]
        si = carry_ref[i, :, sb:2 * sb]
        for t in range(tt):
            r0 = t * bsz
            nr = ar * sr - ai * si + buf[r0:r0 + bsz, 0:sb]
            ni = ar * si + ai * sr + buf[r0:r0 + bsz, sb:2 * sb]
            buf[r0:r0 + bsz, 0:sb] = nr
            buf[r0:r0 + bsz, sb:2 * sb] = ni
            sr, si = nr, ni
        carry_ref[i, :, 0:sb] = sr
        carry_ref[i, :, sb:2 * sb] = si
        project_out(i)

    y = y_ref[...]
    gate = jax.nn.sigmoid(_dot(y.astype(BF16), wg_ref[...].astype(BF16)) + bg_ref[...])
    o = y * gate * jax.nn.silu(z_ref[...])
    ob = jnp.swapaxes(o.reshape(tt, bsz, o.shape[1]), 0, 1).reshape(rows, o.shape[1]).astype(BF16)
    r = _dot(ob, wo_ref[...].astype(BF16))
    out_ref[...] = (x + _rms(r, pg_ref[...])).reshape(out_ref.shape)


def _s5_discretize(lam_re, lam_im, log_dt, b_re, b_im):
    dt = jnp.exp(log_dt)[:, None]
    mag = jnp.exp(lam_re * dt)
    ab_re = mag * jnp.cos(lam_im * dt)
    ab_im = mag * jnp.sin(lam_im * dt)
    den = lam_re * lam_re + lam_im * lam_im
    nr = ab_re - 1.0
    f_re = (nr * lam_re + ab_im * lam_im) / den
    f_im = (ab_im * lam_re - nr * lam_im) / den
    bb_re = f_re[..., None] * b_re - f_im[..., None] * b_im
    bb_im = f_re[..., None] * b_im + f_im[..., None] * b_re
    return ab_re, ab_im, bb_re, bb_im


def _s5_layer(x, pre_g, post_g, w_in, lam_re, lam_im, log_dt, b_re, b_im, c_re, c_im, d_skip,
              w_glu, b_glu, w_out):
    bsz, L, d = x.shape
    width = w_in.shape[1] // 2
    nblk = width // S5_CH_BLOCK
    gpb = S5_GROUPS_PER_BLOCK
    tt = S5_T
    rows = bsz * tt

    ab_re, ab_im, bb_re, bb_im = _s5_discretize(lam_re, lam_im, log_dt, b_re, b_im)
    eye = jnp.eye(gpb, dtype=F32)

    def pack_b(bb):
        t = bb.reshape(nblk, gpb, SSM_STATE, SSM_GROUP)
        return jnp.einsum('igph,gk->ikhgp', t, eye).reshape(nblk, S5_CH_BLOCK, S5_STATE_BLOCK)

    def pack_c(cc):
        t = cc.reshape(nblk, gpb, SSM_GROUP, SSM_STATE)
        return jnp.einsum('ighp,gk->igpkh', t, eye).reshape(nblk, S5_STATE_BLOCK, S5_CH_BLOCK)

    bb = jnp.concatenate([pack_b(bb_re), pack_b(bb_im)], axis=2).astype(BF16)
    cc = jnp.concatenate([pack_c(c_re), -pack_c(c_im)], axis=1).astype(BF16)
    ar = jnp.broadcast_to(ab_re.reshape(nblk, 1, S5_STATE_BLOCK), (nblk, bsz, S5_STATE_BLOCK))
    ai = jnp.broadcast_to(ab_im.reshape(nblk, 1, S5_STATE_BLOCK), (nblk, bsz, S5_STATE_BLOCK))

    xspec = pl.BlockSpec((bsz, tt, d), lambda i: (0, i, 0))
    return pl.pallas_call(
        functools.partial(_s5_kernel, tt=tt),
        out_shape=jax.ShapeDtypeStruct(x.shape, x.dtype),
        grid=(L // tt,),
        in_specs=[xspec, _full((1, d)), _full(w_in.shape),
                  _full(bb.shape), _full(cc.shape), _full(ar.shape), _full(ai.shape), _full((1, width)),
                  _full(w_glu.shape), _full((1, width)), _full(w_out.shape), _full((1, d))],
        out_specs=xspec,
        scratch_shapes=[pltpu.VMEM((rows, width), F32),
                        pltpu.VMEM((rows, width), F32),
                        pltpu.VMEM((rows, width), F32),
                        pltpu.VMEM((2, rows, 2 * S5_STATE_BLOCK), F32),
                        pltpu.VMEM((nblk, bsz, 2 * S5_STATE_BLOCK), F32)],
        compiler_params=_cparams(("arbitrary",)),
        name="s5_layer",
    )(x, pre_g.reshape(1, d), w_in, bb, cc, ar, ai, d_skip.reshape(1, width),
      w_glu, b_glu.reshape(1, width), w_out, post_g.reshape(1, d))


def _swa_bias(rel_bias):
    W = WINDOW
    n = 4 * W
    dist = jnp.arange(n) - W
    valid = jnp.logical_and(dist >= 0, dist < W)
    dpos = jnp.maximum(dist, 0)
    max_exact = REL_BUCKETS // 2
    dist_f = jnp.maximum(dpos, 1).astype(F32)
    large = max_exact + (jnp.log(dist_f / max_exact) / math.log(REL_MAX_DIST / max_exact)
                         * (REL_BUCKETS - max_exact)).astype(jnp.int32)
    large = jnp.minimum(large, REL_BUCKETS - 1)
    bucket = jnp.where(dpos < max_exact, dpos, large)
    return jnp.where(valid[:, None], rel_bias[bucket].astype(F32), NEG_INF).T


def _swa_kernel(x_ref, pg_ref, wqt_ref, wk_ref, wvt_ref, wz_ref, bvec_ref, sink_ref, wo_ref, g_ref,
                out_ref, ot_ref, bias_ref, kprev_ref, vtprev_ref, *, scale):
    W = WINDOW
    nwin = x_ref.shape[1] // W
    step = pl.program_id(1)

    @pl.when(step == 0)
    def _():
        kprev_ref[...] = jnp.zeros_like(kprev_ref)
        vtprev_ref[...] = jnp.zeros_like(vtprev_ref)

    x = x_ref[0]
    hb = _rms(x, pg_ref[...]).astype(BF16)
    qt = (_dot_nt(wqt_ref[...], hb) * scale).astype(BF16)
    k = _dot(hb, wk_ref[...]).astype(BF16)
    vt = _dot_nt(wvt_ref[...], hb).astype(BF16)
    z = _dot(hb, wz_ref[...])

    @pl.when(jnp.logical_and(pl.program_id(0) == 0, step == 0))
    def _():
        no_prev = lax.broadcasted_iota(jnp.int32, (2 * W, W), 0) < W
        for hd in range(SWA_HEADS):
            base = jnp.broadcast_to(bvec_ref[hd:hd + 1, :], (2 * W, bvec_ref.shape[1]))
            toep = pltpu.roll(base, 0, 1, stride=1, stride_axis=0)[:, 2 * W:3 * W]
            h, g = divmod(hd, SWA_GROUP)
            bias_ref[0, h, :, g * W:(g + 1) * W] = toep
            bias_ref[1, h, :, g * W:(g + 1) * W] = jnp.where(no_prev, NEG_INF, toep)
    kall = jnp.concatenate([kprev_ref[...], k], axis=0)
    vtall = jnp.concatenate([vtprev_ref[...], vt], axis=1)
    kprev_ref[...] = k[(nwin - 1) * W:]
    vtprev_ref[...] = vt[:, (nwin - 1) * W:]
    nsub = SWA_UNIT_HEADS
    zq = jnp.zeros((HEAD_DIM, nsub * W), BF16)
    ones = jnp.ones((16, 2 * W), BF16)
    units = [(w, h, c) for w in range(nwin) for h in range(SWA_KV_HEADS) for c in range(SWA_GROUP // nsub)]

    def scores(w, h, c):
        hd0 = h * SWA_GROUP + c * nsub
        qh = jnp.concatenate([qt[(hd0 + g) * HEAD_DIM:(hd0 + g + 1) * HEAD_DIM, w * W:(w + 1) * W]
                              for g in range(nsub)], axis=1)
        qz = jnp.concatenate([qh, zq] if h == 0 else [zq, qh], axis=0)
        return _dot(kall[w * W:(w + 2) * W], qz)

    pending = [scores(*u) for u in units[:SWA_LOOKAHEAD]]
    late = []

    def flush():
        (w, h, c), p, tail = late.pop(0)
        vones = jnp.concatenate([vtall[h * HEAD_DIM:(h + 1) * HEAD_DIM, w * W:(w + 2) * W], ones], axis=0)
        o = _dot(vones, p)
        oh = o[:HEAD_DIM] * (1.0 / (o[HEAD_DIM:HEAD_DIM + 1] + tail))
        for g in range(nsub):
            hd = h * SWA_GROUP + c * nsub + g
            ot_ref[hd * HEAD_DIM:(hd + 1) * HEAD_DIM, w * W:(w + 1) * W] = oh[:, g * W:(g + 1) * W]

    for idx, (w, h, c) in enumerate(units):
        raw = pending.pop(0)
        if idx + SWA_LOOKAHEAD < len(units):
            pending.append(scores(*units[idx + SWA_LOOKAHEAD]))
        cols = slice(c * nsub * W, (c + 1) * nsub * W)
        variant = (step == 0).astype(jnp.int32) if w == 0 else 0
        s = raw + bias_ref[variant, h, :, cols]
        sink = sink_ref[h, :, cols]
        m = jnp.maximum(jnp.max(s, axis=0, keepdims=True), sink)
        if len(late) == SWA_PV_DELAY:
            flush()
        late.append(((w, h, c), jnp.exp2(s - m).astype(BF16), jnp.exp2(sink - m)))
    while late:
        flush()
    gated = ot_ref[...].T * jax.nn.silu(z)
    r = _dot(gated.astype(BF16), wo_ref[...].astype(BF16))
    out_ref[0] = x + _rms(r, g_ref[...])


def _swa_layer(x, pre_g, post_g, w_in, sinks, w_out, rel_bias):
    bsz, L, d = x.shape
    width = SWA_HEADS * HEAD_DIM
    kvw = SWA_KV_HEADS * HEAD_DIM
    W = WINDOW
    log2e = math.log2(math.e)
    wb = w_in.astype(BF16)
    wqt = wb[:, :width].T
    wk = wb[:, width:width + kvw]
    wvt = wb[:, width + kvw:width + 2 * kvw].T
    wz = wb[:, width + 2 * kvw:]
    bvec = _swa_bias(rel_bias.astype(F32) * log2e)
    sink = jnp.repeat(sinks.astype(F32) * log2e, W).reshape(SWA_KV_HEADS, 1, SWA_GROUP * W)

    tq = SWA_WINDOWS_PER_STEP * W
    xspec = pl.BlockSpec((1, tq, d), lambda b, n: (b, n, 0))
    return pl.pallas_call(
        functools.partial(_swa_kernel, scale=HEAD_DIM ** -0.5 * log2e),
        out_shape=jax.ShapeDtypeStruct(x.shape, x.dtype),
        grid=(bsz, L // tq),
        in_specs=[xspec, _full((1, d)), _full(wqt.shape), _full(wk.shape), _full(wvt.shape), _full(wz.shape),
                  _full(bvec.shape), _full(sink.shape), _full(w_out.shape), _full((1, d))],
        out_specs=xspec,
        scratch_shapes=[pltpu.VMEM((width, tq), F32),
                        pltpu.VMEM((2, SWA_KV_HEADS, 2 * W, SWA_GROUP * W), F32),
                        pltpu.VMEM((W, kvw), BF16),
                        pltpu.VMEM((kvw, W), BF16)],
        compiler_params=_cparams(("arbitrary", "arbitrary")),
        name="swa_layer",
    )(x, pre_g.reshape(1, d), wqt, wk, wvt, wz, bvec, sink, w_out, post_g.reshape(1, d))


def _mla_pre_kernel(x_ref, g_ref, w_ref, qn_ref, kvn_ref, wq_ref, wkv_ref, wvt_ref, cq_ref, sq_ref, ck_ref, sk_ref,
                    oqn_ref, oqr_ref, okn_ref, okr_ref, ov_ref, oz_ref, *, scale):
    nope = MLA_HEADS * MLA_NOPE
    rope = MLA_HEADS * MLA_ROPE
    vw = MLA_HEADS * MLA_V
    hb = _rms(x_ref[0], g_ref[...]).astype(BF16)
    o1 = MLA_Q_RANK
    o2 = o1 + MLA_KV_RANK
    o3 = o2 + vw
    cq = _dot(hb, w_ref[:, :o1])
    ckv = _dot(hb, w_ref[:, o1:o2])
    oz_ref[0] = _dot(hb, w_ref[:, o2:o3])
    kr = _dot(hb, w_ref[:, o3:o3 + LANES])
    krs = _dot(hb, w_ref[:, o3 + LANES:o3 + 2 * LANES])
    okr_ref[0] = (kr * ck_ref[...] + krs * sk_ref[...]).astype(BF16)
    cqb = _rms(cq, qn_ref[...]).astype(BF16)
    oqn_ref[0] = (_dot_nt(wq_ref[:nope], cqb) * scale).astype(BF16)
    qr = _dot_nt(wq_ref[nope:nope + rope], cqb)
    hr = MLA_ROPE // 2
    qrs = jnp.concatenate([qr[h * MLA_ROPE + o:h * MLA_ROPE + o + hr]
                           for h in range(MLA_HEADS) for o in (hr, 0)], axis=0)
    oqr_ref[0] = ((qr * cq_ref[...] + qrs * sq_ref[...]) * scale).astype(BF16)
    ckb = _rms(ckv, kvn_ref[...]).astype(BF16)
    okn_ref[0] = _dot(ckb, wkv_ref[:, :nope]).astype(BF16)
    vt = _dot_nt(wvt_ref[...], ckb).astype(BF16)
    tk = ov_ref.shape[3]
    for c in range(ov_ref.shape[1]):
        ov_ref[0, c] = vt[:, c * tk:(c + 1) * tk]


def _mla_attn_kernel(qn_ref, qr_ref, kn_ref, kr_ref, v_ref, z_ref, x_ref, wo_ref, g_ref, out_ref,
                     qs_ref, acc_ref, m_ref, o_ref):
    tq = qn_ref.shape[2]
    tk = v_ref.shape[3]
    npairs = MLA_HEADS // 2
    i = pl.program_id(1)
    tri = (lax.broadcasted_iota(jnp.int32, (LANES, LANES), 0)
           <= lax.broadcasted_iota(jnp.int32, (LANES, LANES), 1))

    zn = jnp.zeros((MLA_NOPE, tq), BF16)
    zr = jnp.zeros((LANES - MLA_ROPE, tq), BF16)
    for p in range(npairs):
        qn = qn_ref[0, p * LANES:(p + 1) * LANES, :]
        r0 = 2 * p * MLA_ROPE
        c0 = jnp.concatenate([qn[:MLA_NOPE], zn, qr_ref[0, r0:r0 + MLA_ROPE, :], zr], axis=0)
        c1 = jnp.concatenate([zn, qn[MLA_NOPE:], qr_ref[0, r0 + MLA_ROPE:r0 + 2 * MLA_ROPE, :], zr], axis=0)
        qs_ref[p] = jnp.concatenate([c0, c1], axis=1)

    m_ref[...] = jnp.full(m_ref.shape, NEG_INF, F32)
    acc_ref[...] = jnp.zeros(acc_ref.shape, F32)
    ones = jnp.ones((acc_ref.shape[1] - LANES, tk), BF16)

    def kv_steps(blocks):
        units = [(j, masked, p) for j, masked in blocks for p in range(npairs)]

        def scores(j, masked, p):
            ks = pl.multiple_of(j * tk, tk)
            kc = jnp.concatenate([kn_ref[0, pl.ds(ks, tk), p * LANES:(p + 1) * LANES],
                                  kr_ref[0, pl.ds(ks, tk), :]], axis=1)
            if not masked:
                chunks = [_dot(kc, qs_ref[p, :, c * 2 * LANES:(c + 1) * 2 * LANES]) for c in range(tq // LANES)]
                return [chunks[c // 2][:, (c % 2) * LANES:(c % 2 + 1) * LANES] for c in range(2 * tq // LANES)]
            lo = [_dot(kc[:LANES], qs_ref[p, :, c * 2 * LANES:(c + 1) * 2 * LANES]) for c in range(tq // LANES)]
            late_q = jnp.concatenate([qs_ref[p, :, LANES:2 * LANES], qs_ref[p, :, 3 * LANES:4 * LANES]], axis=1)
            hi = _dot(kc[LANES:], late_q)
            strips = []
            for hd in range(2):
                strips.append(jnp.where(tri, lo[hd][:, :LANES], NEG_INF))
                strips.append(jnp.concatenate([lo[hd][:, LANES:],
                                               jnp.where(tri, hi[:, hd * LANES:(hd + 1) * LANES], NEG_INF)], axis=0))
            return strips

        pending = [scores(*u) for u in units[:MLA_LOOKAHEAD]]
        late = []

        def flush():
            jj, pp, alpha, pr = late.pop(0)
            vones = jnp.concatenate([v_ref[0, jj, pp * LANES:(pp + 1) * LANES, :], ones], axis=0)
            acc_ref[pp] = alpha * acc_ref[pp] + _dot(vones, pr)

        for idx, (j, masked, p) in enumerate(units):
            s = pending.pop(0)
            if idx + MLA_LOOKAHEAD < len(units):
                pending.append(scores(*units[idx + MLA_LOOKAHEAD]))
            probs, alphas = [], []
            for c, sc in enumerate(s):
                m_prev = m_ref[p, :, c * LANES:(c + 1) * LANES]
                m_new = jnp.maximum(m_prev, jnp.max(sc, axis=0, keepdims=True))
                alphas.append(jnp.exp2(m_prev - m_new))
                pr = jnp.exp2(sc - m_new).astype(BF16)
                if pr.shape[0] < tk:
                    pr = jnp.concatenate([pr, jnp.zeros((tk - pr.shape[0], LANES), BF16)], axis=0)
                probs.append(pr)
                m_ref[p, :, c * LANES:(c + 1) * LANES] = m_new
            if len(late) == MLA_PV_DELAY:
                flush()
            late.append((j, p, jnp.concatenate(alphas, axis=1), jnp.concatenate(probs, axis=1)))
        while late:
            flush()

    def body(jj, c):
        kv_steps([(2 * jj, False), (2 * jj + 1, False)])
        return c

    lax.fori_loop(0, i // 2, body, 0)

    @pl.when(i % 2 == 1)
    def _():
        kv_steps([(i - 1, False), (i, True)])

    @pl.when(i % 2 == 0)
    def _():
        kv_steps([(i, True)])
    for p in range(npairs):
        a = acc_ref[p]
        a = a[:LANES] * (1.0 / a[LANES:LANES + 1])
        ot = jnp.concatenate([a[:MLA_V, :tq], a[MLA_V:, tq:]], axis=0)
        o_ref[:, p * LANES:(p + 1) * LANES] = ot.T
    gated = o_ref[...] * jax.nn.silu(z_ref[0])
    r = _dot(gated.astype(BF16), wo_ref[...].astype(BF16))
    out_ref[0] = x_ref[0] + _rms(r, g_ref[...])


def _mla_layer(x, pre_g, post_g, w_in, q_norm, kv_norm, w_uq, w_ukv, w_out):
    bsz, L, d = x.shape
    H = MLA_HEADS
    dq = MLA_NOPE + MLA_ROPE
    nope = H * MLA_NOPE
    rope = H * MLA_ROPE
    vw = H * MLA_V
    half = MLA_ROPE // 2
    o_kr = MLA_Q_RANK + MLA_KV_RANK
    o_z = o_kr + MLA_ROPE
    wb = w_in.astype(BF16)
    w_kr = wb[:, o_kr:o_z]
    w_krs = jnp.concatenate([w_kr[:, half:], w_kr[:, :half]], axis=1)
    reps = LANES // MLA_ROPE
    w1 = jnp.concatenate([wb[:, :o_kr], wb[:, o_z:]] + [w_kr] * reps + [w_krs] * reps, axis=1)
    wq3 = w_uq.astype(BF16).reshape(MLA_Q_RANK, H, dq)
    wqt = jnp.concatenate([wq3[:, :, :MLA_NOPE].reshape(MLA_Q_RANK, nope),
                           wq3[:, :, MLA_NOPE:].reshape(MLA_Q_RANK, rope)], axis=1).T
    wkv3 = w_ukv.astype(BF16).reshape(MLA_KV_RANK, H, MLA_NOPE + MLA_V)
    wkn = wkv3[:, :, :MLA_NOPE].reshape(MLA_KV_RANK, nope)
    wvt = wkv3[:, :, MLA_NOPE:].reshape(MLA_KV_RANK, vw).T
    inv = ROPE_BASE ** (-jnp.arange(0, MLA_ROPE, 2, dtype=F32) / MLA_ROPE)
    ang = jnp.arange(L, dtype=F32)[:, None] * inv[None, :]
    cos, sin = jnp.cos(ang), jnp.sin(ang)
    cos32 = jnp.concatenate([cos, cos], axis=1)
    sin32 = jnp.concatenate([-sin, sin], axis=1)
    cos_k, sin_k = jnp.tile(cos32, (1, LANES // MLA_ROPE)), jnp.tile(sin32, (1, LANES // MLA_ROPE))
    cos_q, sin_q = jnp.tile(cos32, (1, H)).T, jnp.tile(sin32, (1, H)).T

    tm = ROW_TILE
    tk = MLA_TK
    tok = lambda w_: pl.BlockSpec((1, tm, w_), lambda b, i: (b, i, 0))
    tokt = lambda w_: pl.BlockSpec((1, w_, tm), lambda b, i: (b, 0, i))
    scale = dq ** -0.5 * math.log2(math.e)
    qn, qr, kn, kr, v, z = pl.pallas_call(
        functools.partial(_mla_pre_kernel, scale=scale),
        out_shape=[jax.ShapeDtypeStruct((bsz, nope, L), BF16),
                   jax.ShapeDtypeStruct((bsz, rope, L), BF16),
                   jax.ShapeDtypeStruct((bsz, L, nope), BF16),
                   jax.ShapeDtypeStruct((bsz, L, LANES), BF16),
                   jax.ShapeDtypeStruct((bsz, L // tk, vw, tk), BF16),
                   jax.ShapeDtypeStruct((bsz, L, vw), F32)],
        grid=(bsz, L // tm),
        in_specs=[tok(d), _full((1, d)), _full(w1.shape), _full((1, MLA_Q_RANK)), _full((1, MLA_KV_RANK)),
                  _full(wqt.shape), _full(wkn.shape), _full(wvt.shape),
                  pl.BlockSpec((rope, tm), lambda b, i: (0, i)), pl.BlockSpec((rope, tm), lambda b, i: (0, i)),
                  pl.BlockSpec((tm, LANES), lambda b, i: (i, 0)), pl.BlockSpec((tm, LANES), lambda b, i: (i, 0))],
        out_specs=[tokt(nope), tokt(rope), tok(nope), tok(LANES),
                   pl.BlockSpec((1, tm // tk, vw, tk), lambda b, i: (b, i, 0, 0)), tok(vw)],
        compiler_params=_cparams(("parallel", "parallel")),
        name="mla_pre",
    )(x, pre_g.reshape(1, d), w1, q_norm.reshape(1, -1), kv_norm.reshape(1, -1), wqt, wkn, wvt,
      cos_q, sin_q, cos_k, sin_k)

    tq = MLA_TQ
    assert MLA_TQ == MLA_TK == 2 * LANES, "the diagonal-block handling works on 128-query strips of a 256 block"
    npairs = H // 2
    qspec = lambda w_: pl.BlockSpec((1, w_, tq), lambda b, i: (b, 0, i))
    kspec = lambda w_: pl.BlockSpec((1, L, w_), lambda b, i: (b, 0, 0))
    rowspec = lambda w_: pl.BlockSpec((1, tq, w_), lambda b, i: (b, i, 0))
    return pl.pallas_call(
        _mla_attn_kernel,
        out_shape=jax.ShapeDtypeStruct(x.shape, x.dtype),
        grid=(bsz, L // tq),
        in_specs=[qspec(nope), qspec(rope), kspec(nope), kspec(LANES),
                  pl.BlockSpec((1, L // tk, vw, tk), lambda b, i: (b, 0, 0, 0)),
                  rowspec(vw), rowspec(d), _full(w_out.shape), _full((1, d))],
        out_specs=rowspec(d),
        scratch_shapes=[pltpu.VMEM((npairs, 2 * LANES, 2 * tq), BF16),
                        pltpu.VMEM((npairs, LANES + 16, 2 * tq), F32),
                        pltpu.VMEM((npairs, 1, 2 * tq), F32),
                        pltpu.VMEM((tq, vw), F32)],
        compiler_params=_cparams(("parallel", "arbitrary")),
        name="mla_attn",
    )(qn, qr, kn, kr, v, z, x, w_out, post_g.reshape(1, d))


def _sgu_kernel(x_ref, g_ref, w_ref, lng_ref, lnb_ref, ws_ref, bs_ref, wo_ref, pg_ref, out_ref, s_ref):
    width = wo_ref.shape[0]
    tm = x_ref.shape[1]
    lane = lax.broadcasted_iota(jnp.int32, (1, LANES), 1)
    lo = lane < HALF
    x = x_ref[0]
    hb = _rms(x, g_ref[...]).astype(BF16)
    v = jax.nn.gelu(_dot(hb, w_ref[:, width:2 * width].astype(BF16)))
    mu = jnp.mean(v, axis=-1, keepdims=True)
    vc = v - mu
    var = jnp.mean(vc * vc, axis=-1, keepdims=True)
    vb = (vc * lax.rsqrt(var + EPS) * lng_ref[...] + lnb_ref[...]).astype(BF16)
    group = SGU_STACK
    for c0 in range(0, tm // SGU_CHUNK, group):
        for jj in range(width // LANES):
            blk = jnp.concatenate([vb[c * SGU_CHUNK:(c + 1) * SGU_CHUNK, jj * LANES:(jj + 1) * LANES]
                                   for c in range(c0, c0 + group)], axis=1)
            r = _dot(ws_ref[jj], blk)
            for k in range(group):
                c = c0 + k
                s_ref[c * SGU_CHUNK:(c + 1) * SGU_CHUNK, jj * LANES:(jj + 1) * LANES] = (
                    jnp.where(lo, r[:SGU_CHUNK, k * LANES:(k + 1) * LANES],
                              r[SGU_CHUNK:, k * LANES:(k + 1) * LANES]) + bs_ref[jj])
    u = jax.nn.gelu(_dot(hb, w_ref[:, :width].astype(BF16)))
    z = _dot(hb, w_ref[:, 2 * width:].astype(BF16))
    o = u * s_ref[...] * jax.nn.silu(z)
    r = _dot(o.astype(BF16), wo_ref[...].astype(BF16))
    out_ref[0] = x + _rms(r, pg_ref[...])


def _sgu_layer(x, pre_g, post_g, w_in, ln_g, ln_b, w_s, b_s, w_out):
    bsz, L, d = x.shape
    width = w_out.shape[0]
    T = SGU_CHUNK
    gd = width // SGU_GROUPS
    tril = jnp.tril(jnp.ones((T, T), dtype=bool))
    ws = jnp.where(tril[None], w_s, 0.0).reshape(SGU_GROUPS // 2, 2 * T, T).astype(BF16)
    bs = jnp.repeat(b_s.astype(F32).T, gd, axis=1)
    bs = bs.reshape(T, width // LANES, LANES).transpose(1, 0, 2)
    tm = ROW_TILE
    return pl.pallas_call(
        _sgu_kernel,
        out_shape=jax.ShapeDtypeStruct(x.shape, x.dtype),
        grid=(bsz, L // tm),
        in_specs=[pl.BlockSpec((1, tm, d), lambda b, i: (b, i, 0)),
                  _full((1, d)), _full(w_in.shape), _full((1, width)), _full((1, width)),
                  _full(ws.shape), _full(bs.shape), _full(w_out.shape), _full((1, d))],
        out_specs=pl.BlockSpec((1, tm, d), lambda b, i: (b, i, 0)),
        scratch_shapes=[pltpu.VMEM((tm, width), F32)],
        compiler_params=_cparams(("parallel", "parallel")),
        name="sgu",
    )(x, pre_g.reshape(1, d), w_in, ln_g.reshape(1, width), ln_b.reshape(1, width),
      ws, bs, w_out, post_g.reshape(1, d))


def kernel(x, pre_norm, post_norm, rel_bias, a_w_in, a_lam_re, a_lam_im, a_log_dt, a_b_re, a_b_im, a_c_re, a_c_im, a_d, a_w_glu, a_b_glu, a_w_out, b_w_in, b_sinks, b_w_out, c_w_in, c_q_norm, c_kv_norm, c_w_uq, c_w_ukv, c_w_out, d_w_in, d_ln_g, d_ln_b, d_w_s, d_b_s, d_w_out):
    depth = pre_norm.shape[0]
    for i in range(depth):
        kind, j = i % 4, i // 4
        if kind == 0:
            x = _s5_layer(x, pre_norm[i], post_norm[i], a_w_in[j], a_lam_re[j], a_lam_im[j], a_log_dt[j],
                          a_b_re[j], a_b_im[j], a_c_re[j], a_c_im[j], a_d[j], a_w_glu[j], a_b_glu[j],
                          a_w_out[j])
        elif kind == 1:
            x = _swa_layer(x, pre_norm[i], post_norm[i], b_w_in[j], b_sinks[j], b_w_out[j], rel_bias)
        elif kind == 2:
            x = _mla_layer(x, pre_norm[i], post_norm[i], c_w_in[j], c_q_norm[j], c_kv_norm[j], c_w_uq[j],
                           c_w_ukv[j], c_w_out[j])
        else:
            x = _sgu_layer(x, pre_norm[i], post_norm[i], d_w_in[j], d_ln_g[j], d_ln_b[j], d_w_s[j],
                           d_b_s[j], d_w_out[j])
    return x
```

```python
import functools
import math

import jax
import jax.numpy as jnp
from jax import lax
from jax.experimental import pallas as pl
from jax.experimental.pallas import tpu as pltpu

F32 = jnp.float32
BF16 = jnp.bfloat16

EPS = 1e-6
NEG_INF = -1e30
LANES = 128
HALF = LANES // 2

SSM_GROUP = 16
SSM_STATE = 64
S5_CH_BLOCK = LANES
S5_GROUPS_PER_BLOCK = S5_CH_BLOCK // SSM_GROUP
S5_STATE_BLOCK = S5_GROUPS_PER_BLOCK * SSM_STATE
S5_T = 64

HEAD_DIM = 64
SWA_HEADS = 16
SWA_KV_HEADS = 2
SWA_GROUP = SWA_HEADS // SWA_KV_HEADS
WINDOW = 128
SWA_WINDOWS_PER_STEP = 8
SWA_UNIT_HEADS = 8
SWA_LOOKAHEAD = 2
SWA_PV_DELAY = 1
REL_BUCKETS = 32
REL_MAX_DIST = 128

MLA_HEADS = 16
MLA_NOPE = 64
MLA_ROPE = 32
MLA_V = 64
MLA_KV_RANK = 256
MLA_Q_RANK = 768
ROPE_BASE = 10000.0
MLA_TQ = 256
MLA_TK = 256
MLA_LOOKAHEAD = 6
MLA_PV_DELAY = 2

SGU_CHUNK = 128
SGU_GROUPS = 16
SGU_STACK = 4

ROW_TILE = 1024
VMEM_LIMIT = 56 * 1024 * 1024


def _cparams(sem):
    return pltpu.CompilerParams(dimension_semantics=sem, vmem_limit_bytes=VMEM_LIMIT)


def _rms(x, g):
    return x * lax.rsqrt(jnp.mean(x * x, axis=-1, keepdims=True) + EPS) * g


def _dot(a, b):
    return jnp.dot(a, b, preferred_element_type=F32)


def _dot_nt(a, b):
    return lax.dot_general(a, b, (((1,), (1,)), ((), ())), preferred_element_type=F32)


def _full(shape):
    n = len(shape)
    return pl.BlockSpec(shape, lambda *_: (0,) * n, pipeline_mode=pl.Buffered(1))


def _s5_kernel(x_ref, g_ref, w_ref, bb_ref, cc_ref, ar_ref, ai_ref, d_ref,
               wg_ref, bg_ref, wo_ref, pg_ref, out_ref, u_ref, z_ref, y_ref, s_ref, carry_ref, *, tt):
    bsz = x_ref.shape[0]
    width = wg_ref.shape[0]
    rows = bsz * tt
    nblk = bb_ref.shape[0]
    sb = S5_STATE_BLOCK

    @pl.when(pl.program_id(0) == 0)
    def _():
        carry_ref[...] = jnp.zeros_like(carry_ref)

    x = x_ref[...].reshape(rows, x_ref.shape[2])
    h = _rms(x, g_ref[...])
    hb = jnp.swapaxes(h.reshape(bsz, tt, h.shape[1]), 0, 1).reshape(rows, h.shape[1]).astype(BF16)
    u_ref[...] = _dot(hb, w_ref[:, :width].astype(BF16))
    z_ref[...] = _dot(hb, w_ref[:, width:].astype(BF16))

    nbuf = s_ref.shape[0]

    def project_in(i):
        s_ref[i % nbuf] = _dot(u_ref[:, i * LANES:(i + 1) * LANES].astype(BF16), bb_ref[i])

    def project_out(i):
        ub = u_ref[:, i * LANES:(i + 1) * LANES]
        y = _dot(s_ref[i % nbuf].astype(BF16), cc_ref[i]) + d_ref[:, i * LANES:(i + 1) * LANES] * ub
        y_ref[:, i * LANES:(i + 1) * LANES] = jax.nn.gelu(y)

    project_in(0)
    for i in range(nblk):
        if i + 1 < nblk:
            project_in(i + 1)
        buf = s_ref.at[i % nbuf]
        ar = ar_ref[i]
        ai = ai_ref[i]
        sr = carry_ref[i, :, 0:sb]
        si = carry_ref[i, :, sb:2 * sb]
        for t in range(tt):
            r0 = t * bsz
            nr = ar * sr - ai * si + buf[r0:r0 + bsz, 0:sb]
            ni = ar * si + ai * sr + buf[r0:r0 + bsz, sb:2 * sb]
            buf[r0:r0 + bsz, 0:sb] = nr
            buf[r0:r0 + bsz, sb:2 * sb] = ni
            sr, si = nr, ni
        carry_ref[i, :, 0:sb] = sr
        carry_ref[i, :, sb:2 * sb] = si
        project_out(i)

    y = y_ref[...]
    gate = jax.nn.sigmoid(_dot(y.astype(BF16), wg_ref[...].astype(BF16)) + bg_ref[...])
    o = y * gate * jax.nn.silu(z_ref[...])
    ob = jnp.swapaxes(o.reshape(tt, bsz, o.shape[1]), 0, 1).reshape(rows, o.shape[1]).astype(BF16)
    r = _dot(ob, wo_ref[...].astype(BF16))
    out_ref[...] = (x + _rms(r, pg_ref[...])).reshape(out_ref.shape)


def _s5_discretize(lam_re, lam_im, log_dt, b_re, b_im):
    dt = jnp.exp(log_dt)[:, None]
    mag = jnp.exp(lam_re * dt)
    ab_re = mag * jnp.cos(lam_im * dt)
    ab_im = mag * jnp.sin(lam_im * dt)
    den = lam_re * lam_re + lam_im * lam_im
    nr = ab_re - 1.0
    f_re = (nr * lam_re + ab_im * lam_im) / den
    f_im = (ab_im * lam_re - nr * lam_im) / den
    bb_re = f_re[..., None] * b_re - f_im[..., None] * b_im
    bb_im = f_re[..., None] * b_im + f_im[..., None] * b_re
    return ab_re, ab_im, bb_re, bb_im


def _s5_layer(x, pre_g, post_g, w_in, lam_re, lam_im, log_dt, b_re, b_im, c_re, c_im, d_skip,
              w_glu, b_glu, w_out):
    bsz, L, d = x.shape
    width = w_in.shape[1] // 2
    nblk = width // S5_CH_BLOCK
    gpb = S5_GROUPS_PER_BLOCK
    tt = S5_T
    rows = bsz * tt

    ab_re, ab_im, bb_re, bb_im = _s5_discretize(lam_re, lam_im, log_dt, b_re, b_im)
    eye = jnp.eye(gpb, dtype=F32)

    def pack_b(bb):
        t = bb.reshape(nblk, gpb, SSM_STATE, SSM_GROUP)
        return jnp.einsum('igph,gk->ikhgp', t, eye).reshape(nblk, S5_CH_BLOCK, S5_STATE_BLOCK)

    def pack_c(cc):
        t = cc.reshape(nblk, gpb, SSM_GROUP, SSM_STATE)
        return jnp.einsum('ighp,gk->igpkh', t, eye).reshape(nblk, S5_STATE_BLOCK, S5_CH_BLOCK)

    bb = jnp.concatenate([pack_b(bb_re), pack_b(bb_im)], axis=2).astype(BF16)
    cc = jnp.concatenate([pack_c(c_re), -pack_c(c_im)], axis=1).astype(BF16)
    ar = jnp.broadcast_to(ab_re.reshape(nblk, 1, S5_STATE_BLOCK), (nblk, bsz, S5_STATE_BLOCK))
    ai = jnp.broadcast_to(ab_im.reshape(nblk, 1, S5_STATE_BLOCK), (nblk, bsz, S5_STATE_BLOCK))

    xspec = pl.BlockSpec((bsz, tt, d), lambda i: (0, i, 0))
    return pl.pallas_call(
        functools.partial(_s5_kernel, tt=tt),
        out_shape=jax.ShapeDtypeStruct(x.shape, x.dtype),
        grid=(L // tt,),
        in_specs=[xspec, _full((1, d)), _full(w_in.shape),
                  _full(bb.shape), _full(cc.shape), _full(ar.shape), _full(ai.shape), _full((1, width)),
                  _full(w_glu.shape), _full((1, width)), _full(w_out.shape), _full((1, d))],
        out_specs=xspec,
        scratch_shapes=[pltpu.VMEM((rows, width), F32),
                        pltpu.VMEM((rows, width), F32),
                        pltpu.VMEM((rows, width), F32),
                        pltpu.VMEM((2, rows, 2 * S5_STATE_BLOCK), F32),
                        pltpu.VMEM((nblk, bsz, 2 * S5_STATE_BLOCK), F32)],
        compiler_params=_cparams(("arbitrary",)),
        name="s5_layer",
    )(x, pre_g.reshape(1, d), w_in, bb, cc, ar, ai, d_skip.reshape(1, width),
      w_glu, b_glu.reshape(1, width), w_out, post_g.reshape(1, d))


def _swa_bias(rel_bias):
    W = WINDOW
    n = 4 * W
    dist = jnp.arange(n) - W
    valid = jnp.logical_and(dist >= 0, dist < W)
    dpos = jnp.maximum(dist, 0)
    max_exact = REL_BUCKETS // 2
    dist_f = jnp.maximum(dpos, 1).astype(F32)
    large = max_exact + (jnp.log(dist_f / max_exact) / math.log(REL_MAX_DIST / max_exact)
                         * (REL_BUCKETS - max_exact)).astype(jnp.int32)
    large = jnp.minimum(large, REL_BUCKETS - 1)
    bucket = jnp.where(dpos < max_exact, dpos, large)
    return jnp.where(valid[:, None], rel_bias[bucket].astype(F32), NEG_INF).T


def _swa_kernel(x_ref, pg_ref, wqt_ref, wk_ref, wvt_ref, wz_ref, bvec_ref, sink_ref, wo_ref, g_ref,
                out_ref, ot_ref, bias_ref, kprev_ref, vtprev_ref, *, scale):
    W = WINDOW
    nwin = x_ref.shape[1] // W
    step = pl.program_id(1)

    @pl.when(step == 0)
    def _():
        kprev_ref[...] = jnp.zeros_like(kprev_ref)
        vtprev_ref[...] = jnp.zeros_like(vtprev_ref)

    x = x_ref[0]
    hb = _rms(x, pg_ref[...]).astype(BF16)
    qt = (_dot_nt(wqt_ref[...], hb) * scale).astype(BF16)
    k = _dot(hb, wk_ref[...]).astype(BF16)
    vt = _dot_nt(wvt_ref[...], hb).astype(BF16)
    z = _dot(hb, wz_ref[...])

    @pl.when(jnp.logical_and(pl.program_id(0) == 0, step == 0))
    def _():
        no_prev = lax.broadcasted_iota(jnp.int32, (2 * W, W), 0) < W
        for hd in range(SWA_HEADS):
            base = jnp.broadcast_to(bvec_ref[hd:hd + 1, :], (2 * W, bvec_ref.shape[1]))
            toep = pltpu.roll(base, 0, 1, stride=1, stride_axis=0)[:, 2 * W:3 * W]
            h, g = divmod(hd, SWA_GROUP)
            bias_ref[0, h, :, g * W:(g + 1) * W] = toep
            bias_ref[1, h, :, g * W:(g + 1) * W] = jnp.where(no_prev, NEG_INF, toep)
    kall = jnp.concatenate([kprev_ref[...], k], axis=0)
    vtall = jnp.concatenate([vtprev_ref[...], vt], axis=1)
    kprev_ref[...] = k[(nwin - 1) * W:]
    vtprev_ref[...] = vt[:, (nwin - 1) * W:]
    nsub = SWA_UNIT_HEADS
    zq = jnp.zeros((HEAD_DIM, nsub * W), BF16)
    ones = jnp.ones((16, 2 * W), BF16)
    units = [(w, h, c) for w in range(nwin) for h in range(SWA_KV_HEADS) for c in range(SWA_GROUP // nsub)]

    def scores(w, h, c):
        hd0 = h * SWA_GROUP + c * nsub
        qh = jnp.concatenate([qt[(hd0 + g) * HEAD_DIM:(hd0 + g + 1) * HEAD_DIM, w * W:(w + 1) * W]
                              for g in range(nsub)], axis=1)
        qz = jnp.concatenate([qh, zq] if h == 0 else [zq, qh], axis=0)
        return _dot(kall[w * W:(w + 2) * W], qz)

    pending = [scores(*u) for u in units[:SWA_LOOKAHEAD]]
    late = []

    def flush():
        (w, h, c), p, tail = late.pop(0)
        vones = jnp.concatenate([vtall[h * HEAD_DIM:(h + 1) * HEAD_DIM, w * W:(w + 2) * W], ones], axis=0)
        o = _dot(vones, p)
        oh = o[:HEAD_DIM] * (1.0 / (o[HEAD_DIM:HEAD_DIM + 1] + tail))
        for g in range(nsub):
            hd = h * SWA_GROUP + c * nsub + g
            ot_ref[hd * HEAD_DIM:(hd + 1) * HEAD_DIM, w * W:(w + 1) * W] = oh[:, g * W:(g + 1) * W]

    for idx, (w, h, c) in enumerate(units):
        raw = pending.pop(0)
        if idx + SWA_LOOKAHEAD < len(units):
            pending.append(scores(*units[idx + SWA_LOOKAHEAD]))
        cols = slice(c * nsub * W, (c + 1) * nsub * W)
        variant = (step == 0).astype(jnp.int32) if w == 0 else 0
        s = raw + bias_ref[variant, h, :, cols]
        sink = sink_ref[h, :, cols]
        m = jnp.maximum(jnp.max(s, axis=0, keepdims=True), sink)
        if len(late) == SWA_PV_DELAY:
            flush()
        late.append(((w, h, c), jnp.exp2(s - m).astype(BF16), jnp.exp2(sink - m)))
    while late:
        flush()
    gated = ot_ref[...].T * jax.nn.silu(z)
    r = _dot(gated.astype(BF16), wo_ref[...].astype(BF16))
    out_ref[0] = x + _rms(r, g_ref[...])


def _swa_layer(x, pre_g, post_g, w_in, sinks, w_out, rel_bias):
    bsz, L, d = x.shape
    width = SWA_HEADS * HEAD_DIM
    kvw = SWA_KV_HEADS * HEAD_DIM
    W = WINDOW
    log2e = math.log2(math.e)
    wb = w_in.astype(BF16)
    wqt = wb[:, :width].T
    wk = wb[:, width:width + kvw]
    wvt = wb[:, width + kvw:width + 2 * kvw].T
    wz = wb[:, width + 2 * kvw:]
    bvec = _swa_bias(rel_bias.astype(F32) * log2e)
    sink = jnp.repeat(sinks.astype(F32) * log2e, W).reshape(SWA_KV_HEADS, 1, SWA_GROUP * W)

    tq = SWA_WINDOWS_PER_STEP * W
    xspec = pl.BlockSpec((1, tq, d), lambda b, n: (b, n, 0))
    return pl.pallas_call(
        functools.partial(_swa_kernel, scale=HEAD_DIM ** -0.5 * log2e),
        out_shape=jax.ShapeDtypeStruct(x.shape, x.dtype),
        grid=(bsz, L // tq),
        in_specs=[xspec, _full((1, d)), _full(wqt.shape), _full(wk.shape), _full(wvt.shape), _full(wz.shape),
                  _full(bvec.shape), _full(sink.shape), _full(w_out.shape), _full((1, d))],
        out_specs=xspec,
        scratch_shapes=[pltpu.VMEM((width, tq), F32),
                        pltpu.VMEM((2, SWA_KV_HEADS, 2 * W, SWA_GROUP * W), F32),
                        pltpu.VMEM((W, kvw), BF16),
                        pltpu.VMEM((kvw, W), BF16)],
        compiler_params=_cparams(("arbitrary", "arbitrary")),
        name="swa_layer",
    )(x, pre_g.reshape(1, d), wqt, wk, wvt, wz, bvec, sink, w_out, post_g.reshape(1, d))


def _mla_pre_kernel(x_ref, g_ref, w_ref, qn_ref, kvn_ref, wq_ref, wkv_ref, wvt_ref, cq_ref, sq_ref, ck_ref, sk_ref,
                    oqn_ref, oqr_ref, okn_ref, okr_ref, ov_ref, oz_ref, *, scale):
    nope = MLA_HEADS * MLA_NOPE
    rope = MLA_HEADS * MLA_ROPE
    vw = MLA_HEADS * MLA_V
    hb = _rms(x_ref[0], g_ref[...]).astype(BF16)
    o1 = MLA_Q_RANK
    o2 = o1 + MLA_KV_RANK
    o3 = o2 + vw
    cq = _dot(hb, w_ref[:, :o1])
    ckv = _dot(hb, w_ref[:, o1:o2])
    oz_ref[0] = _dot(hb, w_ref[:, o2:o3])
    kr = _dot(hb, w_ref[:, o3:o3 + LANES])
    krs = _dot(hb, w_ref[:, o3 + LANES:o3 + 2 * LANES])
    okr_ref[0] = (kr * ck_ref[...] + krs * sk_ref[...]).astype(BF16)
    cqb = _rms(cq, qn_ref[...]).astype(BF16)
    oqn_ref[0] = (_dot_nt(wq_ref[:nope], cqb) * scale).astype(BF16)
    qr = _dot_nt(wq_ref[nope:nope + rope], cqb)
    hr = MLA_ROPE // 2
    qrs = jnp.concatenate([qr[h * MLA_ROPE + o:h * MLA_ROPE + o + hr]
                           for h in range(MLA_HEADS) for o in (hr, 0)], axis=0)
    oqr_ref[0] = ((qr * cq_ref[...] + qrs * sq_ref[...]) * scale).astype(BF16)
    ckb = _rms(ckv, kvn_ref[...]).astype(BF16)
    okn_ref[0] = _dot(ckb, wkv_ref[:, :nope]).astype(BF16)
    vt = _dot_nt(wvt_ref[...], ckb).astype(BF16)
    tk = ov_ref.shape[3]
    for c in range(ov_ref.shape[1]):
        ov_ref[0, c] = vt[:, c * tk:(c + 1) * tk]


def _mla_attn_kernel(qn_ref, qr_ref, kn_ref, kr_ref, v_ref, z_ref, x_ref, wo_ref, g_ref, out_ref,
                     qs_ref, acc_ref, m_ref, o_ref):
    tq = qn_ref.shape[2]
    tk = v_ref.shape[3]
    npairs = MLA_HEADS // 2
    i = pl.program_id(1)
    tri = (lax.broadcasted_iota(jnp.int32, (LANES, LANES), 0)
           <= lax.broadcasted_iota(jnp.int32, (LANES, LANES), 1))

    zn = jnp.zeros((MLA_NOPE, tq), BF16)
    zr = jnp.zeros((LANES - MLA_ROPE, tq), BF16)
    for p in range(npairs):
        qn = qn_ref[0, p * LANES:(p + 1) * LANES, :]
        r0 = 2 * p * MLA_ROPE
        c0 = jnp.concatenate([qn[:MLA_NOPE], zn, qr_ref[0, r0:r0 + MLA_ROPE, :], zr], axis=0)
        c1 = jnp.concatenate([zn, qn[MLA_NOPE:], qr_ref[0, r0 + MLA_ROPE:r0 + 2 * MLA_ROPE, :], zr], axis=0)
        qs_ref[p] = jnp.concatenate([c0, c1], axis=1)

    m_ref[...] = jnp.full(m_ref.shape, NEG_INF, F32)
    acc_ref[...] = jnp.zeros(acc_ref.shape, F32)
    ones = jnp.ones((acc_ref.shape[1] - LANES, tk), BF16)

    def kv_steps(blocks):
        units = [(j, masked, p) for j, masked in blocks for p in range(npairs)]

        def scores(j, masked, p):
            ks = pl.multiple_of(j * tk, tk)
            kc = jnp.concatenate([kn_ref[0, pl.ds(ks, tk), p * LANES:(p + 1) * LANES],
                                  kr_ref[0, pl.ds(ks, tk), :]], axis=1)
            if not masked:
                chunks = [_dot(kc, qs_ref[p, :, c * 2 * LANES:(c + 1) * 2 * LANES]) for c in range(tq // LANES)]
                return [chunks[c // 2][:, (c % 2) * LANES:(c % 2 + 1) * LANES] for c in range(2 * tq // LANES)]
            lo = [_dot(kc[:LANES], qs_ref[p, :, c * 2 * LANES:(c + 1) * 2 * LANES]) for c in range(tq // LANES)]
            late_q = jnp.concatenate([qs_ref[p, :, LANES:2 * LANES], qs_ref[p, :, 3 * LANES:4 * LANES]], axis=1)
            hi = _dot(kc[LANES:], late_q)
            strips = []
            for hd in range(2):
                strips.append(jnp.where(tri, lo[hd][:, :LANES], NEG_INF))
                strips.append(jnp.concatenate([lo[hd][:, LANES:],
                                               jnp.where(tri, hi[:, hd * LANES:(hd + 1) * LANES], NEG_INF)], axis=0))
            return strips

        pending = [scores(*u) for u in units[:MLA_LOOKAHEAD]]
        late = []

        def flush():
            jj, pp, alpha, pr = late.pop(0)
            vones = jnp.concatenate([v_ref[0, jj, pp * LANES:(pp + 1) * LANES, :], ones], axis=0)
            acc_ref[pp] = alpha * acc_ref[pp] + _dot(vones, pr)

        for idx, (j, masked, p) in enumerate(units):
            s = pending.pop(0)
            if idx + MLA_LOOKAHEAD < len(units):
                pending.append(scores(*units[idx + MLA_LOOKAHEAD]))
            probs, alphas = [], []
            for c, sc in enumerate(s):
                m_prev = m_ref[p, :, c * LANES:(c + 1) * LANES]
                m_new = jnp.maximum(m_prev, jnp.max(sc, axis=0, keepdims=True))
                alphas.append(jnp.exp2(m_prev - m_new))
                pr = jnp.exp2(sc - m_new).astype(BF16)
                if pr.shape[0] < tk:
                    pr = jnp.concatenate([pr, jnp.zeros((tk - pr.shape[0], LANES), BF16)], axis=0)
                probs.append(pr)
                m_ref[p, :, c * LANES:(c + 1) * LANES] = m_new
            if len(late) == MLA_PV_DELAY:
                flush()
            late.append((j, p, jnp.concatenate(alphas, axis=1), jnp.concatenate(probs, axis=1)))
        while late:
            flush()

    def body(jj, c):
        kv_steps([(2 * jj, False), (2 * jj + 1, False)])
        return c

    lax.fori_loop(0, i // 2, body, 0)

    @pl.when(i % 2 == 1)
    def _():
        kv_steps([(i - 1, False), (i, True)])

    @pl.when(i % 2 == 0)
    def _():
        kv_steps([(i, True)])
    for p in range(npairs):
        a = acc_ref[p]
        a = a[:LANES] * (1.0 / a[LANES:LANES + 1])
        ot = jnp.concatenate([a[:MLA_V, :tq], a[MLA_V:, tq:]], axis=0)
        o_ref[:, p * LANES:(p + 1) * LANES] = ot.T
    gated = o_ref[...] * jax.nn.silu(z_ref[0])
    r = _dot(gated.astype(BF16), wo_ref[...].astype(BF16))
    out_ref[0] = x_ref[0] + _rms(r, g_ref[...])


def _mla_layer(x, pre_g, post_g, w_in, q_norm, kv_norm, w_uq, w_ukv, w_out):
    bsz, L, d = x.shape
    H = MLA_HEADS
    dq = MLA_NOPE + MLA_ROPE
    nope = H * MLA_NOPE
    rope = H * MLA_ROPE
    vw = H * MLA_V
    half = MLA_ROPE // 2
    o_kr = MLA_Q_RANK + MLA_KV_RANK
    o_z = o_kr + MLA_ROPE
    wb = w_in.astype(BF16)
    w_kr = wb[:, o_kr:o_z]
    w_krs = jnp.concatenate([w_kr[:, half:], w_kr[:, :half]], axis=1)
    reps = LANES // MLA_ROPE
    w1 = jnp.concatenate([wb[:, :o_kr], wb[:, o_z:]] + [w_kr] * reps + [w_krs] * reps, axis=1)
    wq3 = w_uq.astype(BF16).reshape(MLA_Q_RANK, H, dq)
    wqt = jnp.concatenate([wq3[:, :, :MLA_NOPE].reshape(MLA_Q_RANK, nope),
                           wq3[:, :, MLA_NOPE:].reshape(MLA_Q_RANK, rope)], axis=1).T
    wkv3 = w_ukv.astype(BF16).reshape(MLA_KV_RANK, H, MLA_NOPE + MLA_V)
    wkn = wkv3[:, :, :MLA_NOPE].reshape(MLA_KV_RANK, nope)
    wvt = wkv3[:, :, MLA_NOPE:].reshape(MLA_KV_RANK, vw).T
    inv = ROPE_BASE ** (-jnp.arange(0, MLA_ROPE, 2, dtype=F32) / MLA_ROPE)
    ang = jnp.arange(L, dtype=F32)[:, None] * inv[None, :]
    cos, sin = jnp.cos(ang), jnp.sin(ang)
    cos32 = jnp.concatenate([cos, cos], axis=1)
    sin32 = jnp.concatenate([-sin, sin], axis=1)
    cos_k, sin_k = jnp.tile(cos32, (1, LANES // MLA_ROPE)), jnp.tile(sin32, (1, LANES // MLA_ROPE))
    cos_q, sin_q = jnp.tile(cos32, (1, H)).T, jnp.tile(sin32, (1, H)).T

    tm = ROW_TILE
    tk = MLA_TK
    tok = lambda w_: pl.BlockSpec((1, tm, w_), lambda b, i: (b, i, 0))
    tokt = lambda w_: pl.BlockSpec((1, w_, tm), lambda b, i: (b, 0, i))
    scale = dq ** -0.5 * math.log2(math.e)
    qn, qr, kn, kr, v, z = pl.pallas_call(
        functools.partial(_mla_pre_kernel, scale=scale),
        out_shape=[jax.ShapeDtypeStruct((bsz, nope, L), BF16),
                   jax.ShapeDtypeStruct((bsz, rope, L), BF16),
                   jax.ShapeDtypeStruct((bsz, L, nope), BF16),
                   jax.ShapeDtypeStruct((bsz, L, LANES), BF16),
                   jax.ShapeDtypeStruct((bsz, L // tk, vw, tk), BF16),
                   jax.ShapeDtypeStruct((bsz, L, vw), F32)],
        grid=(bsz, L // tm),
        in_specs=[tok(d), _full((1, d)), _full(w1.shape), _full((1, MLA_Q_RANK)), _full((1, MLA_KV_RANK)),
                  _full(wqt.shape), _full(wkn.shape), _full(wvt.shape),
                  pl.BlockSpec((rope, tm), lambda b, i: (0, i)), pl.BlockSpec((rope, tm), lambda b, i: (0, i)),
                  pl.BlockSpec((tm, LANES), lambda b, i: (i, 0)), pl.BlockSpec((tm, LANES), lambda b, i: (i, 0))],
        out_specs=[tokt(nope), tokt(rope), tok(nope), tok(LANES),
                   pl.BlockSpec((1, tm // tk, vw, tk), lambda b, i: (b, i, 0, 0)), tok(vw)],
        compiler_params=_cparams(("parallel", "parallel")),
        name="mla_pre",
    )(x, pre_g.reshape(1, d), w1, q_norm.reshape(1, -1), kv_norm.reshape(1, -1), wqt, wkn, wvt,
      cos_q, sin_q, cos_k, sin_k)

    tq = MLA_TQ
    assert MLA_TQ == MLA_TK == 2 * LANES, "the diagonal-block handling works on 128-query strips of a 256 block"
    npairs = H // 2
    qspec = lambda w_: pl.BlockSpec((1, w_, tq), lambda b, i: (b, 0, i))
    kspec = lambda w_: pl.BlockSpec((1, L, w_), lambda b, i: (b, 0, 0))
    rowspec = lambda w_: pl.BlockSpec((1, tq, w_), lambda b, i: (b, i, 0))
    return pl.pallas_call(
        _mla_attn_kernel,
        out_shape=jax.ShapeDtypeStruct(x.shape, x.dtype),
        grid=(bsz, L // tq),
        in_specs=[qspec(nope), qspec(rope), kspec(nope), kspec(LANES),
                  pl.BlockSpec((1, L // tk, vw, tk), lambda b, i: (b, 0, 0, 0)),
                  rowspec(vw), rowspec(d), _full(w_out.shape), _full((1, d))],
        out_specs=rowspec(d),
        scratch_shapes=[pltpu.VMEM((npairs, 2 * LANES, 2 * tq), BF16),
                        pltpu.VMEM((npairs, LANES + 16, 2 * tq), F32),
                        pltpu.VMEM((npairs, 1, 2 * tq), F32),
                        pltpu.VMEM((tq, vw), F32)],
        compiler_params=_cparams(("parallel", "arbitrary")),
        name="mla_attn",
    )(qn, qr, kn, kr, v, z, x, w_out, post_g.reshape(1, d))


def _sgu_kernel(x_ref, g_ref, w_ref, lng_ref, lnb_ref, ws_ref, bs_ref, wo_ref, pg_ref, out_ref, s_ref):
    width = wo_ref.shape[0]
    tm = x_ref.shape[1]
    lane = lax.broadcasted_iota(jnp.int32, (1, LANES), 1)
    lo = lane < HALF
    x = x_ref[0]
    hb = _rms(x, g_ref[...]).astype(BF16)
    v = jax.nn.gelu(_dot(hb, w_ref[:, width:2 * width].astype(BF16)))
    mu = jnp.mean(v, axis=-1, keepdims=True)
    vc = v - mu
    var = jnp.mean(vc * vc, axis=-1, keepdims=True)
    vb = (vc * lax.rsqrt(var + EPS) * lng_ref[...] + lnb_ref[...]).astype(BF16)
    group = SGU_STACK
    for c0 in range(0, tm // SGU_CHUNK, group):
        for jj in range(width // LANES):
            blk = jnp.concatenate([vb[c * SGU_CHUNK:(c + 1) * SGU_CHUNK, jj * LANES:(jj + 1) * LANES]
                                   for c in range(c0, c0 + group)], axis=1)
            r = _dot(ws_ref[jj], blk)
            for k in range(group):
                c = c0 + k
                s_ref[c * SGU_CHUNK:(c + 1) * SGU_CHUNK, jj * LANES:(jj + 1) * LANES] = (
                    jnp.where(lo, r[:SGU_CHUNK, k * LANES:(k + 1) * LANES],
                              r[SGU_CHUNK:, k * LANES:(k + 1) * LANES]) + bs_ref[jj])
    u = jax.nn.gelu(_dot(hb, w_ref[:, :width].astype(BF16)))
    z = _dot(hb, w_ref[:, 2 * width:].astype(BF16))
    o = u * s_ref[...] * jax.nn.silu(z)
    r = _dot(o.astype(BF16), wo_ref[...].astype(BF16))
    out_ref[0] = x + _rms(r, pg_ref[...])


def _sgu_layer(x, pre_g, post_g, w_in, ln_g, ln_b, w_s, b_s, w_out):
    bsz, L, d = x.shape
    width = w_out.shape[0]
    T = SGU_CHUNK
    gd = width // SGU_GROUPS
    tril = jnp.tril(jnp.ones((T, T), dtype=bool))
    ws = jnp.where(tril[None], w_s, 0.0).reshape(SGU_GROUPS // 2, 2 * T, T).astype(BF16)
    bs = jnp.repeat(b_s.astype(F32).T, gd, axis=1)
    bs = bs.reshape(T, width // LANES, LANES).transpose(1, 0, 2)
    tm = ROW_TILE
    return pl.pallas_call(
        _sgu_kernel,
        out_shape=jax.ShapeDtypeStruct(x.shape, x.dtype),
        grid=(bsz, L // tm),
        in_specs=[pl.BlockSpec((1, tm, d), lambda b, i: (b, i, 0)),
                  _full((1, d)), _full(w_in.shape), _full((1, width)), _full((1, width)),
                  _full(ws.shape), _full(bs.shape), _full(w_out.shape), _full((1, d))],
        out_specs=pl.BlockSpec((1, tm, d), lambda b, i: (b, i, 0)),
        scratch_shapes=[pltpu.VMEM((tm, width), F32)],
        compiler_params=_cparams(("parallel", "parallel")),
        name="sgu",
    )(x, pre_g.reshape(1, d), w_in, ln_g.reshape(1, width), ln_b.reshape(1, width),
      ws, bs, w_out, post_g.reshape(1, d))


def kernel(x, pre_norm, post_norm, rel_bias, a_w_in, a_lam_re, a_lam_im, a_log_dt, a_b_re, a_b_im, a_c_re, a_c_im, a_d, a_w_glu, a_b_glu, a_w_out, b_w_in, b_sinks, b_w_out, c_w_in, c_q_norm, c_kv_norm, c_w_uq, c_w_ukv, c_w_out, d_w_in, d_ln_g, d_ln_b, d_w_s, d_b_s, d_w_out):
    depth = pre_norm.shape[0]
    for i in range(depth):
        kind, j = i % 4, i // 4
        if kind == 0:
            x = _s5_layer(x, pre_norm[i], post_norm[i], a_w_in[j], a_lam_re[j], a_lam_im[j], a_log_dt[j],
                          a_b_re[j], a_b_im[j], a_c_re[j], a_c_im[j], a_d[j], a_w_glu[j], a_b_glu[j],
                          a_w_out[j])
        elif kind == 1:
            x = _swa_layer(x, pre_norm[i], post_norm[i], b_w_in[j], b_sinks[j], b_w_out[j], rel_bias)
        elif kind == 2:
            x = _mla_layer(x, pre_norm[i], post_norm[i], c_w_in[j], c_q_norm[j], c_kv_norm[j], c_w_uq[j],
                           c_w_ukv[j], c_w_out[j])
        else:
            x = _sgu_layer(x, pre_norm[i], post_norm[i], d_w_in[j], d_ln_g[j], d_ln_b[j], d_w_s[j],
                           d_b_s[j], d_w_out[j])
    return x
```

```python
import functools
import math

import jax
import jax.numpy as jnp
from jax import lax
from jax.experimental import pallas as pl
from jax.experimental.pallas import tpu as pltpu

F32 = jnp.float32
BF16 = jnp.bfloat16

EPS = 1e-6
NEG_INF = -1e30
LANES = 128
HALF = LANES // 2

SSM_GROUP = 16
SSM_STATE = 64
S5_CH_BLOCK = LANES
S5_GROUPS_PER_BLOCK = S5_CH_BLOCK // SSM_GROUP
S5_STATE_BLOCK = S5_GROUPS_PER_BLOCK * SSM_STATE
S5_T = 64

HEAD_DIM = 64
SWA_HEADS = 16
SWA_KV_HEADS = 2
SWA_GROUP = SWA_HEADS // SWA_KV_HEADS
WINDOW = 128
SWA_WINDOWS_PER_STEP = 8
SWA_UNIT_HEADS = 8
SWA_LOOKAHEAD = 2
SWA_PV_DELAY = 1
REL_BUCKETS = 32
REL_MAX_DIST = 128

MLA_HEADS = 16
MLA_NOPE = 64
MLA_ROPE = 32
MLA_V = 64
MLA_KV_RANK = 256
MLA_Q_RANK = 768
ROPE_BASE = 10000.0
MLA_TQ = 256
MLA_TK = 256
MLA_LOOKAHEAD = 6
MLA_PV_DELAY = 3

SGU_CHUNK = 128
SGU_GROUPS = 16
SGU_STACK = 4

ROW_TILE = 1024
VMEM_LIMIT = 56 * 1024 * 1024


def _cparams(sem):
    return pltpu.CompilerParams(dimension_semantics=sem, vmem_limit_bytes=VMEM_LIMIT)


def _rms(x, g):
    return x * lax.rsqrt(jnp.mean(x * x, axis=-1, keepdims=True) + EPS) * g


def _dot(a, b):
    return jnp.dot(a, b, preferred_element_type=F32)


def _dot_nt(a, b):
    return lax.dot_general(a, b, (((1,), (1,)), ((), ())), preferred_element_type=F32)


def _full(shape):
    n = len(shape)
    return pl.BlockSpec(shape, lambda *_: (0,) * n, pipeline_mode=pl.Buffered(1))


def _s5_kernel(x_ref, g_ref, w_ref, bb_ref, cc_ref, ar_ref, ai_ref, d_ref,
               wg_ref, bg_ref, wo_ref, pg_ref, out_ref, u_ref, z_ref, y_ref, s_ref, carry_ref, *, tt):
    bsz = x_ref.shape[0]
    width = wg_ref.shape[0]
    rows = bsz * tt
    nblk = bb_ref.shape[0]
    sb = S5_STATE_BLOCK

    @pl.when(pl.program_id(0) == 0)
    def _():
        carry_ref[...] = jnp.zeros_like(carry_ref)

    x = x_ref[...].reshape(rows, x_ref.shape[2])
    h = _rms(x, g_ref[...])
    hb = jnp.swapaxes(h.reshape(bsz, tt, h.shape[1]), 0, 1).reshape(rows, h.shape[1]).astype(BF16)
    u_ref[...] = _dot(hb, w_ref[:, :width].astype(BF16))
    z_ref[...] = _dot(hb, w_ref[:, width:].astype(BF16))

    nbuf = s_ref.shape[0]

    def project_in(i):
        s_ref[i % nbuf] = _dot(u_ref[:, i * LANES:(i + 1) * LANES].astype(BF16), bb_ref[i])

    def project_out(i):
        ub = u_ref[:, i * LANES:(i + 1) * LANES]
        y = _dot(s_ref[i % nbuf].astype(BF16), cc_ref[i]) + d_ref[:, i * LANES:(i + 1) * LANES] * ub
        y_ref[:, i * LANES:(i + 1) * LANES] = jax.nn.gelu(y)

    project_in(0)
    for i in range(nblk):
        if i + 1 < nblk:
            project_in(i + 1)
        buf = s_ref.at[i % nbuf]
        ar = ar_ref[i]
        ai = ai_ref[i]
        sr = carry_ref[i, :, 0:sb]
        si = carry_ref[i, :, sb:2 * sb]
        for t in range(tt):
            r0 = t * bsz
            nr = ar * sr - ai * si + buf[r0:r0 + bsz, 0:sb]
            ni = ar * si + ai * sr + buf[r0:r0 + bsz, sb:2 * sb]
            buf[r0:r0 + bsz, 0:sb] = nr
            buf[r0:r0 + bsz, sb:2 * sb] = ni
            sr, si = nr, ni
        carry_ref[i, :, 0:sb] = sr
        carry_ref[i, :, sb:2 * sb] = si
        project_out(i)

    y = y_ref[...]
    gate = jax.nn.sigmoid(_dot(y.astype(BF16), wg_ref[...].astype(BF16)) + bg_ref[...])
    o = y * gate * jax.nn.silu(z_ref[...])
    ob = jnp.swapaxes(o.reshape(tt, bsz, o.shape[1]), 0, 1).reshape(rows, o.shape[1]).astype(BF16)
    r = _dot(ob, wo_ref[...].astype(BF16))
    out_ref[...] = (x + _rms(r, pg_ref[...])).reshape(out_ref.shape)


def _s5_discretize(lam_re, lam_im, log_dt, b_re, b_im):
    dt = jnp.exp(log_dt)[:, None]
    mag = jnp.exp(lam_re * dt)
    ab_re = mag * jnp.cos(lam_im * dt)
    ab_im = mag * jnp.sin(lam_im * dt)
    den = lam_re * lam_re + lam_im * lam_im
    nr = ab_re - 1.0
    f_re = (nr * lam_re + ab_im * lam_im) / den
    f_im = (ab_im * lam_re - nr * lam_im) / den
    bb_re = f_re[..., None] * b_re - f_im[..., None] * b_im
    bb_im = f_re[..., None] * b_im + f_im[..., None] * b_re
    return ab_re, ab_im, bb_re, bb_im


def _s5_layer(x, pre_g, post_g, w_in, lam_re, lam_im, log_dt, b_re, b_im, c_re, c_im, d_skip,
              w_glu, b_glu, w_out):
    bsz, L, d = x.shape
    width = w_in.shape[1] // 2
    nblk = width // S5_CH_BLOCK
    gpb = S5_GROUPS_PER_BLOCK
    tt = S5_T
    rows = bsz * tt

    ab_re, ab_im, bb_re, bb_im = _s5_discretize(lam_re, lam_im, log_dt, b_re, b_im)
    eye = jnp.eye(gpb, dtype=F32)

    def pack_b(bb):
        t = bb.reshape(nblk, gpb, SSM_STATE, SSM_GROUP)
        return jnp.einsum('igph,gk->ikhgp', t, eye).reshape(nblk, S5_CH_BLOCK, S5_STATE_BLOCK)

    def pack_c(cc):
        t = cc.reshape(nblk, gpb, SSM_GROUP, SSM_STATE)
        return jnp.einsum('ighp,gk->igpkh', t, eye).reshape(nblk, S5_STATE_BLOCK, S5_CH_BLOCK)

    bb = jnp.concatenate([pack_b(bb_re), pack_b(bb_im)], axis=2).astype(BF16)
    cc = jnp.concatenate([pack_c(c_re), -pack_c(c_im)], axis=1).astype(BF16)
    ar = jnp.broadcast_to(ab_re.reshape(nblk, 1, S5_STATE_BLOCK), (nblk, bsz, S5_STATE_BLOCK))
    ai = jnp.broadcast_to(ab_im.reshape(nblk, 1, S5_STATE_BLOCK), (nblk, bsz, S5_STATE_BLOCK))

    xspec = pl.BlockSpec((bsz, tt, d), lambda i: (0, i, 0))
    return pl.pallas_call(
        functools.partial(_s5_kernel, tt=tt),
        out_shape=jax.ShapeDtypeStruct(x.shape, x.dtype),
        grid=(L // tt,),
        in_specs=[xspec, _full((1, d)), _full(w_in.shape),
                  _full(bb.shape), _full(cc.shape), _full(ar.shape), _full(ai.shape), _full((1, width)),
                  _full(w_glu.shape), _full((1, width)), _full(w_out.shape), _full((1, d))],
        out_specs=xspec,
        scratch_shapes=[pltpu.VMEM((rows, width), F32),
                        pltpu.VMEM((rows, width), F32),
                        pltpu.VMEM((rows, width), F32),
                        pltpu.VMEM((2, rows, 2 * S5_STATE_BLOCK), F32),
                        pltpu.VMEM((nblk, bsz, 2 * S5_STATE_BLOCK), F32)],
        compiler_params=_cparams(("arbitrary",)),
        name="s5_layer",
    )(x, pre_g.reshape(1, d), w_in, bb, cc, ar, ai, d_skip.reshape(1, width),
      w_glu, b_glu.reshape(1, width), w_out, post_g.reshape(1, d))


def _swa_bias(rel_bias):
    W = WINDOW
    n = 4 * W
    dist = jnp.arange(n) - W
    valid = jnp.logical_and(dist >= 0, dist < W)
    dpos = jnp.maximum(dist, 0)
    max_exact = REL_BUCKETS // 2
    dist_f = jnp.maximum(dpos, 1).astype(F32)
    large = max_exact + (jnp.log(dist_f / max_exact) / math.log(REL_MAX_DIST / max_exact)
                         * (REL_BUCKETS - max_exact)).astype(jnp.int32)
    large = jnp.minimum(large, REL_BUCKETS - 1)
    bucket = jnp.where(dpos < max_exact, dpos, large)
    return jnp.where(valid[:, None], rel_bias[bucket].astype(F32), NEG_INF).T


def _swa_kernel(x_ref, pg_ref, wqt_ref, wk_ref, wvt_ref, wz_ref, bvec_ref, sink_ref, wo_ref, g_ref,
                out_ref, ot_ref, bias_ref, kprev_ref, vtprev_ref, *, scale):
    W = WINDOW
    nwin = x_ref.shape[1] // W
    step = pl.program_id(1)

    @pl.when(step == 0)
    def _():
        kprev_ref[...] = jnp.zeros_like(kprev_ref)
        vtprev_ref[...] = jnp.zeros_like(vtprev_ref)

    x = x_ref[0]
    hb = _rms(x, pg_ref[...]).astype(BF16)
    qt = (_dot_nt(wqt_ref[...], hb) * scale).astype(BF16)
    k = _dot(hb, wk_ref[...]).astype(BF16)
    vt = _dot_nt(wvt_ref[...], hb).astype(BF16)
    z = _dot(hb, wz_ref[...])

    @pl.when(jnp.logical_and(pl.program_id(0) == 0, step == 0))
    def _():
        no_prev = lax.broadcasted_iota(jnp.int32, (2 * W, W), 0) < W
        for hd in range(SWA_HEADS):
            base = jnp.broadcast_to(bvec_ref[hd:hd + 1, :], (2 * W, bvec_ref.shape[1]))
            toep = pltpu.roll(base, 0, 1, stride=1, stride_axis=0)[:, 2 * W:3 * W]
            h, g = divmod(hd, SWA_GROUP)
            bias_ref[0, h, :, g * W:(g + 1) * W] = toep
            bias_ref[1, h, :, g * W:(g + 1) * W] = jnp.where(no_prev, NEG_INF, toep)
    kall = jnp.concatenate([kprev_ref[...], k], axis=0)
    vtall = jnp.concatenate([vtprev_ref[...], vt], axis=1)
    kprev_ref[...] = k[(nwin - 1) * W:]
    vtprev_ref[...] = vt[:, (nwin - 1) * W:]
    nsub = SWA_UNIT_HEADS
    zq = jnp.zeros((HEAD_DIM, nsub * W), BF16)
    ones = jnp.ones((16, 2 * W), BF16)
    units = [(w, h, c) for w in range(nwin) for h in range(SWA_KV_HEADS) for c in range(SWA_GROUP // nsub)]

    def scores(w, h, c):
        hd0 = h * SWA_GROUP + c * nsub
        qh = jnp.concatenate([qt[(hd0 + g) * HEAD_DIM:(hd0 + g + 1) * HEAD_DIM, w * W:(w + 1) * W]
                              for g in range(nsub)], axis=1)
        qz = jnp.concatenate([qh, zq] if h == 0 else [zq, qh], axis=0)
        return _dot(kall[w * W:(w + 2) * W], qz)

    pending = [scores(*u) for u in units[:SWA_LOOKAHEAD]]
    late = []

    def flush():
        (w, h, c), p, tail = late.pop(0)
        vones = jnp.concatenate([vtall[h * HEAD_DIM:(h + 1) * HEAD_DIM, w * W:(w + 2) * W], ones], axis=0)
        o = _dot(vones, p)
        oh = o[:HEAD_DIM] * (1.0 / (o[HEAD_DIM:HEAD_DIM + 1] + tail))
        for g in range(nsub):
            hd = h * SWA_GROUP + c * nsub + g
            ot_ref[hd * HEAD_DIM:(hd + 1) * HEAD_DIM, w * W:(w + 1) * W] = oh[:, g * W:(g + 1) * W]

    for idx, (w, h, c) in enumerate(units):
        raw = pending.pop(0)
        if idx + SWA_LOOKAHEAD < len(units):
            pending.append(scores(*units[idx + SWA_LOOKAHEAD]))
        cols = slice(c * nsub * W, (c + 1) * nsub * W)
        variant = (step == 0).astype(jnp.int32) if w == 0 else 0
        s = raw + bias_ref[variant, h, :, cols]
        sink = sink_ref[h, :, cols]
        m = jnp.maximum(jnp.max(s, axis=0, keepdims=True), sink)
        if len(late) == SWA_PV_DELAY:
            flush()
        late.append(((w, h, c), jnp.exp2(s - m).astype(BF16), jnp.exp2(sink - m)))
    while late:
        flush()
    gated = ot_ref[...].T * jax.nn.silu(z)
    r = _dot(gated.astype(BF16), wo_ref[...].astype(BF16))
    out_ref[0] = x + _rms(r, g_ref[...])


def _swa_layer(x, pre_g, post_g, w_in, sinks, w_out, rel_bias):
    bsz, L, d = x.shape
    width = SWA_HEADS * HEAD_DIM
    kvw = SWA_KV_HEADS * HEAD_DIM
    W = WINDOW
    log2e = math.log2(math.e)
    wb = w_in.astype(BF16)
    wqt = wb[:, :width].T
    wk = wb[:, width:width + kvw]
    wvt = wb[:, width + kvw:width + 2 * kvw].T
    wz = wb[:, width + 2 * kvw:]
    bvec = _swa_bias(rel_bias.astype(F32) * log2e)
    sink = jnp.repeat(sinks.astype(F32) * log2e, W).reshape(SWA_KV_HEADS, 1, SWA_GROUP * W)

    tq = SWA_WINDOWS_PER_STEP * W
    xspec = pl.BlockSpec((1, tq, d), lambda b, n: (b, n, 0))
    return pl.pallas_call(
        functools.partial(_swa_kernel, scale=HEAD_DIM ** -0.5 * log2e),
        out_shape=jax.ShapeDtypeStruct(x.shape, x.dtype),
        grid=(bsz, L // tq),
        in_specs=[xspec, _full((1, d)), _full(wqt.shape), _full(wk.shape), _full(wvt.shape), _full(wz.shape),
                  _full(bvec.shape), _full(sink.shape), _full(w_out.shape), _full((1, d))],
        out_specs=xspec,
        scratch_shapes=[pltpu.VMEM((width, tq), F32),
                        pltpu.VMEM((2, SWA_KV_HEADS, 2 * W, SWA_GROUP * W), F32),
                        pltpu.VMEM((W, kvw), BF16),
                        pltpu.VMEM((kvw, W), BF16)],
        compiler_params=_cparams(("arbitrary", "arbitrary")),
        name="swa_layer",
    )(x, pre_g.reshape(1, d), wqt, wk, wvt, wz, bvec, sink, w_out, post_g.reshape(1, d))


def _mla_pre_kernel(x_ref, g_ref, w_ref, qn_ref, kvn_ref, wq_ref, wkv_ref, wvt_ref, cq_ref, sq_ref, ck_ref, sk_ref,
                    oqn_ref, oqr_ref, okn_ref, okr_ref, ov_ref, oz_ref, *, scale):
    nope = MLA_HEADS * MLA_NOPE
    rope = MLA_HEADS * MLA_ROPE
    vw = MLA_HEADS * MLA_V
    hb = _rms(x_ref[0], g_ref[...]).astype(BF16)
    o1 = MLA_Q_RANK
    o2 = o1 + MLA_KV_RANK
    o3 = o2 + vw
    cq = _dot(hb, w_ref[:, :o1])
    ckv = _dot(hb, w_ref[:, o1:o2])
    oz_ref[0] = _dot(hb, w_ref[:, o2:o3])
    kr = _dot(hb, w_ref[:, o3:o3 + LANES])
    krs = _dot(hb, w_ref[:, o3 + LANES:o3 + 2 * LANES])
    okr_ref[0] = (kr * ck_ref[...] + krs * sk_ref[...]).astype(BF16)
    cqb = _rms(cq, qn_ref[...]).astype(BF16)
    oqn_ref[0] = (_dot_nt(wq_ref[:nope], cqb) * scale).astype(BF16)
    qr = _dot_nt(wq_ref[nope:nope + rope], cqb)
    hr = MLA_ROPE // 2
    qrs = jnp.concatenate([qr[h * MLA_ROPE + o:h * MLA_ROPE + o + hr]
                           for h in range(MLA_HEADS) for o in (hr, 0)], axis=0)
    oqr_ref[0] = ((qr * cq_ref[...] + qrs * sq_ref[...]) * scale).astype(BF16)
    ckb = _rms(ckv, kvn_ref[...]).astype(BF16)
    okn_ref[0] = _dot(ckb, wkv_ref[:, :nope]).astype(BF16)
    vt = _dot_nt(wvt_ref[...], ckb).astype(BF16)
    tk = ov_ref.shape[3]
    for c in range(ov_ref.shape[1]):
        ov_ref[0, c] = vt[:, c * tk:(c + 1) * tk]


def _mla_attn_kernel(qn_ref, qr_ref, kn_ref, kr_ref, v_ref, z_ref, x_ref, wo_ref, g_ref, out_ref,
                     qs_ref, acc_ref, m_ref, o_ref):
    tq = qn_ref.shape[2]
    tk = v_ref.shape[3]
    npairs = MLA_HEADS // 2
    i = pl.program_id(1)
    tri = (lax.broadcasted_iota(jnp.int32, (LANES, LANES), 0)
           <= lax.broadcasted_iota(jnp.int32, (LANES, LANES), 1))

    zn = jnp.zeros((MLA_NOPE, tq), BF16)
    zr = jnp.zeros((LANES - MLA_ROPE, tq), BF16)
    for p in range(npairs):
        qn = qn_ref[0, p * LANES:(p + 1) * LANES, :]
        r0 = 2 * p * MLA_ROPE
        c0 = jnp.concatenate([qn[:MLA_NOPE], zn, qr_ref[0, r0:r0 + MLA_ROPE, :], zr], axis=0)
        c1 = jnp.concatenate([zn, qn[MLA_NOPE:], qr_ref[0, r0 + MLA_ROPE:r0 + 2 * MLA_ROPE, :], zr], axis=0)
        qs_ref[p] = jnp.concatenate([c0, c1], axis=1)

    m_ref[...] = jnp.full(m_ref.shape, NEG_INF, F32)
    acc_ref[...] = jnp.zeros(acc_ref.shape, F32)
    ones = jnp.ones((acc_ref.shape[1] - LANES, tk), BF16)

    def kv_steps(blocks):
        units = [(j, masked, p) for j, masked in blocks for p in range(npairs)]

        def scores(j, masked, p):
            ks = pl.multiple_of(j * tk, tk)
            kc = jnp.concatenate([kn_ref[0, pl.ds(ks, tk), p * LANES:(p + 1) * LANES],
                                  kr_ref[0, pl.ds(ks, tk), :]], axis=1)
            if not masked:
                chunks = [_dot(kc, qs_ref[p, :, c * 2 * LANES:(c + 1) * 2 * LANES]) for c in range(tq // LANES)]
                return [chunks[c // 2][:, (c % 2) * LANES:(c % 2 + 1) * LANES] for c in range(2 * tq // LANES)]
            lo = [_dot(kc[:LANES], qs_ref[p, :, c * 2 * LANES:(c + 1) * 2 * LANES]) for c in range(tq // LANES)]
            late_q = jnp.concatenate([qs_ref[p, :, LANES:2 * LANES], qs_ref[p, :, 3 * LANES:4 * LANES]], axis=1)
            hi = _dot(kc[LANES:], late_q)
            strips = []
            for hd in range(2):
                strips.append(jnp.where(tri, lo[hd][:, :LANES], NEG_INF))
                strips.append(jnp.concatenate([lo[hd][:, LANES:],
                                               jnp.where(tri, hi[:, hd * LANES:(hd + 1) * LANES], NEG_INF)], axis=0))
            return strips

        pending = [scores(*u) for u in units[:MLA_LOOKAHEAD]]
        late = []

        def flush():
            jj, pp, alpha, pr = late.pop(0)
            vones = jnp.concatenate([v_ref[0, jj, pp * LANES:(pp + 1) * LANES, :], ones], axis=0)
            acc_ref[pp] = alpha * acc_ref[pp] + _dot(vones, pr)

        for idx, (j, masked, p) in enumerate(units):
            s = pending.pop(0)
            if idx + MLA_LOOKAHEAD < len(units):
                pending.append(scores(*units[idx + MLA_LOOKAHEAD]))
            probs, alphas = [], []
            for c, sc in enumerate(s):
                m_prev = m_ref[p, :, c * LANES:(c + 1) * LANES]
                m_new = jnp.maximum(m_prev, jnp.max(sc, axis=0, keepdims=True))
                alphas.append(jnp.exp2(m_prev - m_new))
                pr = jnp.exp2(sc - m_new).astype(BF16)
                if pr.shape[0] < tk:
                    pr = jnp.concatenate([pr, jnp.zeros((tk - pr.shape[0], LANES), BF16)], axis=0)
                probs.append(pr)
                m_ref[p, :, c * LANES:(c + 1) * LANES] = m_new
            if len(late) == MLA_PV_DELAY:
                flush()
            late.append((j, p, jnp.concatenate(alphas, axis=1), jnp.concatenate(probs, axis=1)))
        while late:
            flush()

    def body(jj, c):
        kv_steps([(2 * jj, False), (2 * jj + 1, False)])
        return c

    lax.fori_loop(0, i // 2, body, 0)

    @pl.when(i % 2 == 1)
    def _():
        kv_steps([(i - 1, False), (i, True)])

    @pl.when(i % 2 == 0)
    def _():
        kv_steps([(i, True)])
    for p in range(npairs):
        a = acc_ref[p]
        a = a[:LANES] * (1.0 / a[LANES:LANES + 1])
        ot = jnp.concatenate([a[:MLA_V, :tq], a[MLA_V:, tq:]], axis=0)
        o_ref[:, p * LANES:(p + 1) * LANES] = ot.T
    gated = o_ref[...] * jax.nn.silu(z_ref[0])
    r = _dot(gated.astype(BF16), wo_ref[...].astype(BF16))
    out_ref[0] = x_ref[0] + _rms(r, g_ref[...])


def _mla_layer(x, pre_g, post_g, w_in, q_norm, kv_norm, w_uq, w_ukv, w_out):
    bsz, L, d = x.shape
    H = MLA_HEADS
    dq = MLA_NOPE + MLA_ROPE
    nope = H * MLA_NOPE
    rope = H * MLA_ROPE
    vw = H * MLA_V
    half = MLA_ROPE // 2
    o_kr = MLA_Q_RANK + MLA_KV_RANK
    o_z = o_kr + MLA_ROPE
    wb = w_in.astype(BF16)
    w_kr = wb[:, o_kr:o_z]
    w_krs = jnp.concatenate([w_kr[:, half:], w_kr[:, :half]], axis=1)
    reps = LANES // MLA_ROPE
    w1 = jnp.concatenate([wb[:, :o_kr], wb[:, o_z:]] + [w_kr] * reps + [w_krs] * reps, axis=1)
    wq3 = w_uq.astype(BF16).reshape(MLA_Q_RANK, H, dq)
    wqt = jnp.concatenate([wq3[:, :, :MLA_NOPE].reshape(MLA_Q_RANK, nope),
                           wq3[:, :, MLA_NOPE:].reshape(MLA_Q_RANK, rope)], axis=1).T
    wkv3 = w_ukv.astype(BF16).reshape(MLA_KV_RANK, H, MLA_NOPE + MLA_V)
    wkn = wkv3[:, :, :MLA_NOPE].reshape(MLA_KV_RANK, nope)
    wvt = wkv3[:, :, MLA_NOPE:].reshape(MLA_KV_RANK, vw).T
    inv = ROPE_BASE ** (-jnp.arange(0, MLA_ROPE, 2, dtype=F32) / MLA_ROPE)
    ang = jnp.arange(L, dtype=F32)[:, None] * inv[None, :]
    cos, sin = jnp.cos(ang), jnp.sin(ang)
    cos32 = jnp.concatenate([cos, cos], axis=1)
    sin32 = jnp.concatenate([-sin, sin], axis=1)
    cos_k, sin_k = jnp.tile(cos32, (1, LANES // MLA_ROPE)), jnp.tile(sin32, (1, LANES // MLA_ROPE))
    cos_q, sin_q = jnp.tile(cos32, (1, H)).T, jnp.tile(sin32, (1, H)).T

    tm = ROW_TILE
    tk = MLA_TK
    tok = lambda w_: pl.BlockSpec((1, tm, w_), lambda b, i: (b, i, 0))
    tokt = lambda w_: pl.BlockSpec((1, w_, tm), lambda b, i: (b, 0, i))
    scale = dq ** -0.5 * math.log2(math.e)
    qn, qr, kn, kr, v, z = pl.pallas_call(
        functools.partial(_mla_pre_kernel, scale=scale),
        out_shape=[jax.ShapeDtypeStruct((bsz, nope, L), BF16),
                   jax.ShapeDtypeStruct((bsz, rope, L), BF16),
                   jax.ShapeDtypeStruct((bsz, L, nope), BF16),
                   jax.ShapeDtypeStruct((bsz, L, LANES), BF16),
                   jax.ShapeDtypeStruct((bsz, L // tk, vw, tk), BF16),
                   jax.ShapeDtypeStruct((bsz, L, vw), F32)],
        grid=(bsz, L // tm),
        in_specs=[tok(d), _full((1, d)), _full(w1.shape), _full((1, MLA_Q_RANK)), _full((1, MLA_KV_RANK)),
                  _full(wqt.shape), _full(wkn.shape), _full(wvt.shape),
                  pl.BlockSpec((rope, tm), lambda b, i: (0, i)), pl.BlockSpec((rope, tm), lambda b, i: (0, i)),
                  pl.BlockSpec((tm, LANES), lambda b, i: (i, 0)), pl.BlockSpec((tm, LANES), lambda b, i: (i, 0))],
        out_specs=[tokt(nope), tokt(rope), tok(nope), tok(LANES),
                   pl.BlockSpec((1, tm // tk, vw, tk), lambda b, i: (b, i, 0, 0)), tok(vw)],
        compiler_params=_cparams(("parallel", "parallel")),
        name="mla_pre",
    )(x, pre_g.reshape(1, d), w1, q_norm.reshape(1, -1), kv_norm.reshape(1, -1), wqt, wkn, wvt,
      cos_q, sin_q, cos_k, sin_k)

    tq = MLA_TQ
    assert MLA_TQ == MLA_TK == 2 * LANES, "the diagonal-block handling works on 128-query strips of a 256 block"
    npairs = H // 2
    qspec = lambda w_: pl.BlockSpec((1, w_, tq), lambda b, i: (b, 0, i))
    kspec = lambda w_: pl.BlockSpec((1, L, w_), lambda b, i: (b, 0, 0))
    rowspec = lambda w_: pl.BlockSpec((1, tq, w_), lambda b, i: (b, i, 0))
    return pl.pallas_call(
        _mla_attn_kernel,
        out_shape=jax.ShapeDtypeStruct(x.shape, x.dtype),
        grid=(bsz, L // tq),
        in_specs=[qspec(nope), qspec(rope), kspec(nope), kspec(LANES),
                  pl.BlockSpec((1, L // tk, vw, tk), lambda b, i: (b, 0, 0, 0)),
                  rowspec(vw), rowspec(d), _full(w_out.shape), _full((1, d))],
        out_specs=rowspec(d),
        scratch_shapes=[pltpu.VMEM((npairs, 2 * LANES, 2 * tq), BF16),
                        pltpu.VMEM((npairs, LANES + 16, 2 * tq), F32),
                        pltpu.VMEM((npairs, 1, 2 * tq), F32),
                        pltpu.VMEM((tq, vw), F32)],
        compiler_params=_cparams(("parallel", "arbitrary")),
        name="mla_attn",
    )(qn, qr, kn, kr, v, z, x, w_out, post_g.reshape(1, d))


def _sgu_kernel(x_ref, g_ref, w_ref, lng_ref, lnb_ref, ws_ref, bs_ref, wo_ref, pg_ref, out_ref, s_ref):
    width = wo_ref.shape[0]
    tm = x_ref.shape[1]
    lane = lax.broadcasted_iota(jnp.int32, (1, LANES), 1)
    lo = lane < HALF
    x = x_ref[0]
    hb = _rms(x, g_ref[...]).astype(BF16)
    v = jax.nn.gelu(_dot(hb, w_ref[:, width:2 * width].astype(BF16)))
    mu = jnp.mean(v, axis=-1, keepdims=True)
    vc = v - mu
    var = jnp.mean(vc * vc, axis=-1, keepdims=True)
    vb = (vc * lax.rsqrt(var + EPS) * lng_ref[...] + lnb_ref[...]).astype(BF16)
    group = SGU_STACK
    for c0 in range(0, tm // SGU_CHUNK, group):
        for jj in range(width // LANES):
            blk = jnp.concatenate([vb[c * SGU_CHUNK:(c + 1) * SGU_CHUNK, jj * LANES:(jj + 1) * LANES]
                                   for c in range(c0, c0 + group)], axis=1)
            r = _dot(ws_ref[jj], blk)
            for k in range(group):
                c = c0 + k
                s_ref[c * SGU_CHUNK:(c + 1) * SGU_CHUNK, jj * LANES:(jj + 1) * LANES] = (
                    jnp.where(lo, r[:SGU_CHUNK, k * LANES:(k + 1) * LANES],
                              r[SGU_CHUNK:, k * LANES:(k + 1) * LANES]) + bs_ref[jj])
    u = jax.nn.gelu(_dot(hb, w_ref[:, :width].astype(BF16)))
    z = _dot(hb, w_ref[:, 2 * width:].astype(BF16))
    o = u * s_ref[...] * jax.nn.silu(z)
    r = _dot(o.astype(BF16), wo_ref[...].astype(BF16))
    out_ref[0] = x + _rms(r, pg_ref[...])


def _sgu_layer(x, pre_g, post_g, w_in, ln_g, ln_b, w_s, b_s, w_out):
    bsz, L, d = x.shape
    width = w_out.shape[0]
    T = SGU_CHUNK
    gd = width // SGU_GROUPS
    tril = jnp.tril(jnp.ones((T, T), dtype=bool))
    ws = jnp.where(tril[None], w_s, 0.0).reshape(SGU_GROUPS // 2, 2 * T, T).astype(BF16)
    bs = jnp.repeat(b_s.astype(F32).T, gd, axis=1)
    bs = bs.reshape(T, width // LANES, LANES).transpose(1, 0, 2)
    tm = ROW_TILE
    return pl.pallas_call(
        _sgu_kernel,
        out_shape=jax.ShapeDtypeStruct(x.shape, x.dtype),
        grid=(bsz, L // tm),
        in_specs=[pl.BlockSpec((1, tm, d), lambda b, i: (b, i, 0)),
                  _full((1, d)), _full(w_in.shape), _full((1, width)), _full((1, width)),
                  _full(ws.shape), _full(bs.shape), _full(w_out.shape), _full((1, d))],
        out_specs=pl.BlockSpec((1, tm, d), lambda b, i: (b, i, 0)),
        scratch_shapes=[pltpu.VMEM((tm, width), F32)],
        compiler_params=_cparams(("parallel", "parallel")),
        name="sgu",
    )(x, pre_g.reshape(1, d), w_in, ln_g.reshape(1, width), ln_b.reshape(1, width),
      ws, bs, w_out, post_g.reshape(1, d))


def kernel(x, pre_norm, post_norm, rel_bias, a_w_in, a_lam_re, a_lam_im, a_log_dt, a_b_re, a_b_im, a_c_re, a_c_im, a_d, a_w_glu, a_b_glu, a_w_out, b_w_in, b_sinks, b_w_out, c_w_in, c_q_norm, c_kv_norm, c_w_uq, c_w_ukv, c_w_out, d_w_in, d_ln_g, d_ln_b, d_w_s, d_b_s, d_w_out):
    depth = pre_norm.shape[0]
    for i in range(depth):
        kind, j = i % 4, i // 4
        if kind == 0:
            x = _s5_layer(x, pre_norm[i], post_norm[i], a_w_in[j], a_lam_re[j], a_lam_im[j], a_log_dt[j],
                          a_b_re[j], a_b_im[j], a_c_re[j], a_c_im[j], a_d[j], a_w_glu[j], a_b_glu[j],
                          a_w_out[j])
        elif kind == 1:
            x = _swa_layer(x, pre_norm[i], post_norm[i], b_w_in[j], b_sinks[j], b_w_out[j], rel_bias)
        elif kind == 2:
            x = _mla_layer(x, pre_norm[i], post_norm[i], c_w_in[j], c_q_norm[j], c_kv_norm[j], c_w_uq[j],
                           c_w_ukv[j], c_w_out[j])
        else:
            x = _sgu_layer(x, pre_norm[i], post_norm[i], d_w_in[j], d_ln_g[j], d_ln_b[j], d_w_s[j],
                           d_b_s[j], d_w_out[j])
    return x
```

```python
import functools
import math

import jax
import jax.numpy as jnp
from jax import lax
from jax.experimental import pallas as pl
from jax.experimental.pallas import tpu as pltpu

F32 = jnp.float32
BF16 = jnp.bfloat16

EPS = 1e-6
NEG_INF = -1e30
LANES = 128
HALF = LANES // 2

SSM_GROUP = 16
SSM_STATE = 64
S5_CH_BLOCK = LANES
S5_GROUPS_PER_BLOCK = S5_CH_BLOCK // SSM_GROUP
S5_STATE_BLOCK = S5_GROUPS_PER_BLOCK * SSM_STATE
S5_T = 64

HEAD_DIM = 64
SWA_HEADS = 16
SWA_KV_HEADS = 2
SWA_GROUP = SWA_HEADS // SWA_KV_HEADS
WINDOW = 128
SWA_WINDOWS_PER_STEP = 8
SWA_UNIT_HEADS = 8
SWA_LOOKAHEAD = 2
SWA_PV_DELAY = 1
REL_BUCKETS = 32
REL_MAX_DIST = 128

MLA_HEADS = 16
MLA_NOPE = 64
MLA_ROPE = 32
MLA_V = 64
MLA_KV_RANK = 256
MLA_Q_RANK = 768
ROPE_BASE = 10000.0
MLA_TQ = 256
MLA_TK = 256
MLA_LOOKAHEAD = 8
MLA_PV_DELAY = 3

SGU_CHUNK = 128
SGU_GROUPS = 16
SGU_STACK = 4

ROW_TILE = 1024
VMEM_LIMIT = 56 * 1024 * 1024


def _cparams(sem):
    return pltpu.CompilerParams(dimension_semantics=sem, vmem_limit_bytes=VMEM_LIMIT)


def _rms(x, g):
    return x * lax.rsqrt(jnp.mean(x * x, axis=-1, keepdims=True) + EPS) * g


def _dot(a, b):
    return jnp.dot(a, b, preferred_element_type=F32)


def _dot_nt(a, b):
    return lax.dot_general(a, b, (((1,), (1,)), ((), ())), preferred_element_type=F32)


def _full(shape):
    n = len(shape)
    return pl.BlockSpec(shape, lambda *_: (0,) * n, pipeline_mode=pl.Buffered(1))


def _s5_kernel(x_ref, g_ref, w_ref, bb_ref, cc_ref, ar_ref, ai_ref, d_ref,
               wg_ref, bg_ref, wo_ref, pg_ref, out_ref, u_ref, z_ref, y_ref, s_ref, carry_ref, *, tt):
    bsz = x_ref.shape[0]
    width = wg_ref.shape[0]
    rows = bsz * tt
    nblk = bb_ref.shape[0]
    sb = S5_STATE_BLOCK

    @pl.when(pl.program_id(0) == 0)
    def _():
        carry_ref[...] = jnp.zeros_like(carry_ref)

    x = x_ref[...].reshape(rows, x_ref.shape[2])
    h = _rms(x, g_ref[...])
    hb = jnp.swapaxes(h.reshape(bsz, tt, h.shape[1]), 0, 1).reshape(rows, h.shape[1]).astype(BF16)
    u_ref[...] = _dot(hb, w_ref[:, :width].astype(BF16))
    z_ref[...] = _dot(hb, w_ref[:, width:].astype(BF16))

    nbuf = s_ref.shape[0]

    def project_in(i):
        s_ref[i % nbuf] = _dot(u_ref[:, i * LANES:(i + 1) * LANES].astype(BF16), bb_ref[i])

    def project_out(i):
        ub = u_ref[:, i * LANES:(i + 1) * LANES]
        y = _dot(s_ref[i % nbuf].astype(BF16), cc_ref[i]) + d_ref[:, i * LANES:(i + 1) * LANES] * ub
        y_ref[:, i * LANES:(i + 1) * LANES] = jax.nn.gelu(y)

    project_in(0)
    for i in range(nblk):
        if i + 1 < nblk:
            project_in(i + 1)
        buf = s_ref.at[i % nbuf]
        ar = ar_ref[i]
        ai = ai_ref[i]
        sr = carry_ref[i, :, 0:sb]
        si = carry_ref[i, :, sb:2 * sb]
        for t in range(tt):
            r0 = t * bsz
            nr = ar * sr - ai * si + buf[r0:r0 + bsz, 0:sb]
            ni = ar * si + ai * sr + buf[r0:r0 + bsz, sb:2 * sb]
            buf[r0:r0 + bsz, 0:sb] = nr
            buf[r0:r0 + bsz, sb:2 * sb] = ni
            sr, si = nr, ni
        carry_ref[i, :, 0:sb] = sr
        carry_ref[i, :, sb:2 * sb] = si
        project_out(i)

    y = y_ref[...]
    gate = jax.nn.sigmoid(_dot(y.astype(BF16), wg_ref[...].astype(BF16)) + bg_ref[...])
    o = y * gate * jax.nn.silu(z_ref[...])
    ob = jnp.swapaxes(o.reshape(tt, bsz, o.shape[1]), 0, 1).reshape(rows, o.shape[1]).astype(BF16)
    r = _dot(ob, wo_ref[...].astype(BF16))
    out_ref[...] = (x + _rms(r, pg_ref[...])).reshape(out_ref.shape)


def _s5_discretize(lam_re, lam_im, log_dt, b_re, b_im):
    dt = jnp.exp(log_dt)[:, None]
    mag = jnp.exp(lam_re * dt)
    ab_re = mag * jnp.cos(lam_im * dt)
    ab_im = mag * jnp.sin(lam_im * dt)
    den = lam_re * lam_re + lam_im * lam_im
    nr = ab_re - 1.0
    f_re = (nr * lam_re + ab_im * lam_im) / den
    f_im = (ab_im * lam_re - nr * lam_im) / den
    bb_re = f_re[..., None] * b_re - f_im[..., None] * b_im
    bb_im = f_re[..., None] * b_im + f_im[..., None] * b_re
    return ab_re, ab_im, bb_re, bb_im


def _s5_layer(x, pre_g, post_g, w_in, lam_re, lam_im, log_dt, b_re, b_im, c_re, c_im, d_skip,
              w_glu, b_glu, w_out):
    bsz, L, d = x.shape
    width = w_in.shape[1] // 2
    nblk = width // S5_CH_BLOCK
    gpb = S5_GROUPS_PER_BLOCK
    tt = S5_T
    rows = bsz * tt

    ab_re, ab_im, bb_re, bb_im = _s5_discretize(lam_re, lam_im, log_dt, b_re, b_im)
    eye = jnp.eye(gpb, dtype=F32)

    def pack_b(bb):
        t = bb.reshape(nblk, gpb, SSM_STATE, SSM_GROUP)
        return jnp.einsum('igph,gk->ikhgp', t, eye).reshape(nblk, S5_CH_BLOCK, S5_STATE_BLOCK)

    def pack_c(cc):
        t = cc.reshape(nblk, gpb, SSM_GROUP, SSM_STATE)
        return jnp.einsum('ighp,gk->igpkh', t, eye).reshape(nblk, S5_STATE_BLOCK, S5_CH_BLOCK)

    bb = jnp.concatenate([pack_b(bb_re), pack_b(bb_im)], axis=2).astype(BF16)
    cc = jnp.concatenate([pack_c(c_re), -pack_c(c_im)], axis=1).astype(BF16)
    ar = jnp.broadcast_to(ab_re.reshape(nblk, 1, S5_STATE_BLOCK), (nblk, bsz, S5_STATE_BLOCK))
    ai = jnp.broadcast_to(ab_im.reshape(nblk, 1, S5_STATE_BLOCK), (nblk, bsz, S5_STATE_BLOCK))

    xspec = pl.BlockSpec((bsz, tt, d), lambda i: (0, i, 0))
    return pl.pallas_call(
        functools.partial(_s5_kernel, tt=tt),
        out_shape=jax.ShapeDtypeStruct(x.shape, x.dtype),
        grid=(L // tt,),
        in_specs=[xspec, _full((1, d)), _full(w_in.shape),
                  _full(bb.shape), _full(cc.shape), _full(ar.shape), _full(ai.shape), _full((1, width)),
                  _full(w_glu.shape), _full((1, width)), _full(w_out.shape), _full((1, d))],
        out_specs=xspec,
        scratch_shapes=[pltpu.VMEM((rows, width), F32),
                        pltpu.VMEM((rows, width), F32),
                        pltpu.VMEM((rows, width), F32),
                        pltpu.VMEM((2, rows, 2 * S5_STATE_BLOCK), F32),
                        pltpu.VMEM((nblk, bsz, 2 * S5_STATE_BLOCK), F32)],
        compiler_params=_cparams(("arbitrary",)),
        name="s5_layer",
    )(x, pre_g.reshape(1, d), w_in, bb, cc, ar, ai, d_skip.reshape(1, width),
      w_glu, b_glu.reshape(1, width), w_out, post_g.reshape(1, d))


def _swa_bias(rel_bias):
    W = WINDOW
    n = 4 * W
    dist = jnp.arange(n) - W
    valid = jnp.logical_and(dist >= 0, dist < W)
    dpos = jnp.maximum(dist, 0)
    max_exact = REL_BUCKETS // 2
    dist_f = jnp.maximum(dpos, 1).astype(F32)
    large = max_exact + (jnp.log(dist_f / max_exact) / math.log(REL_MAX_DIST / max_exact)
                         * (REL_BUCKETS - max_exact)).astype(jnp.int32)
    large = jnp.minimum(large, REL_BUCKETS - 1)
    bucket = jnp.where(dpos < max_exact, dpos, large)
    return jnp.where(valid[:, None], rel_bias[bucket].astype(F32), NEG_INF).T


def _swa_kernel(x_ref, pg_ref, wqt_ref, wk_ref, wvt_ref, wz_ref, bvec_ref, sink_ref, wo_ref, g_ref,
                out_ref, ot_ref, bias_ref, kprev_ref, vtprev_ref, *, scale):
    W = WINDOW
    nwin = x_ref.shape[1] // W
    step = pl.program_id(1)

    @pl.when(step == 0)
    def _():
        kprev_ref[...] = jnp.zeros_like(kprev_ref)
        vtprev_ref[...] = jnp.zeros_like(vtprev_ref)

    x = x_ref[0]
    hb = _rms(x, pg_ref[...]).astype(BF16)
    qt = (_dot_nt(wqt_ref[...], hb) * scale).astype(BF16)
    k = _dot(hb, wk_ref[...]).astype(BF16)
    vt = _dot_nt(wvt_ref[...], hb).astype(BF16)
    z = _dot(hb, wz_ref[...])

    @pl.when(jnp.logical_and(pl.program_id(0) == 0, step == 0))
    def _():
        no_prev = lax.broadcasted_iota(jnp.int32, (2 * W, W), 0) < W
        for hd in range(SWA_HEADS):
            base = jnp.broadcast_to(bvec_ref[hd:hd + 1, :], (2 * W, bvec_ref.shape[1]))
            toep = pltpu.roll(base, 0, 1, stride=1, stride_axis=0)[:, 2 * W:3 * W]
            h, g = divmod(hd, SWA_GROUP)
            bias_ref[0, h, :, g * W:(g + 1) * W] = toep
            bias_ref[1, h, :, g * W:(g + 1) * W] = jnp.where(no_prev, NEG_INF, toep)
    kall = jnp.concatenate([kprev_ref[...], k], axis=0)
    vtall = jnp.concatenate([vtprev_ref[...], vt], axis=1)
    kprev_ref[...] = k[(nwin - 1) * W:]
    vtprev_ref[...] = vt[:, (nwin - 1) * W:]
    nsub = SWA_UNIT_HEADS
    zq = jnp.zeros((HEAD_DIM, nsub * W), BF16)
    ones = jnp.ones((16, 2 * W), BF16)
    units = [(w, h, c) for w in range(nwin) for h in range(SWA_KV_HEADS) for c in range(SWA_GROUP // nsub)]

    def scores(w, h, c):
        hd0 = h * SWA_GROUP + c * nsub
        qh = jnp.concatenate([qt[(hd0 + g) * HEAD_DIM:(hd0 + g + 1) * HEAD_DIM, w * W:(w + 1) * W]
                              for g in range(nsub)], axis=1)
        qz = jnp.concatenate([qh, zq] if h == 0 else [zq, qh], axis=0)
        return _dot(kall[w * W:(w + 2) * W], qz)

    pending = [scores(*u) for u in units[:SWA_LOOKAHEAD]]
    late = []

    def flush():
        (w, h, c), p, tail = late.pop(0)
        vones = jnp.concatenate([vtall[h * HEAD_DIM:(h + 1) * HEAD_DIM, w * W:(w + 2) * W], ones], axis=0)
        o = _dot(vones, p)
        oh = o[:HEAD_DIM] * (1.0 / (o[HEAD_DIM:HEAD_DIM + 1] + tail))
        for g in range(nsub):
            hd = h * SWA_GROUP + c * nsub + g
            ot_ref[hd * HEAD_DIM:(hd + 1) * HEAD_DIM, w * W:(w + 1) * W] = oh[:, g * W:(g + 1) * W]

    for idx, (w, h, c) in enumerate(units):
        raw = pending.pop(0)
        if idx + SWA_LOOKAHEAD < len(units):
            pending.append(scores(*units[idx + SWA_LOOKAHEAD]))
        cols = slice(c * nsub * W, (c + 1) * nsub * W)
        variant = (step == 0).astype(jnp.int32) if w == 0 else 0
        s = raw + bias_ref[variant, h, :, cols]
        sink = sink_ref[h, :, cols]
        m = jnp.maximum(jnp.max(s, axis=0, keepdims=True), sink)
        if len(late) == SWA_PV_DELAY:
            flush()
        late.append(((w, h, c), jnp.exp2(s - m).astype(BF16), jnp.exp2(sink - m)))
    while late:
        flush()
    gated = ot_ref[...].T * jax.nn.silu(z)
    r = _dot(gated.astype(BF16), wo_ref[...].astype(BF16))
    out_ref[0] = x + _rms(r, g_ref[...])


def _swa_layer(x, pre_g, post_g, w_in, sinks, w_out, rel_bias):
    bsz, L, d = x.shape
    width = SWA_HEADS * HEAD_DIM
    kvw = SWA_KV_HEADS * HEAD_DIM
    W = WINDOW
    log2e = math.log2(math.e)
    wb = w_in.astype(BF16)
    wqt = wb[:, :width].T
    wk = wb[:, width:width + kvw]
    wvt = wb[:, width + kvw:width + 2 * kvw].T
    wz = wb[:, width + 2 * kvw:]
    bvec = _swa_bias(rel_bias.astype(F32) * log2e)
    sink = jnp.repeat(sinks.astype(F32) * log2e, W).reshape(SWA_KV_HEADS, 1, SWA_GROUP * W)

    tq = SWA_WINDOWS_PER_STEP * W
    xspec = pl.BlockSpec((1, tq, d), lambda b, n: (b, n, 0))
    return pl.pallas_call(
        functools.partial(_swa_kernel, scale=HEAD_DIM ** -0.5 * log2e),
        out_shape=jax.ShapeDtypeStruct(x.shape, x.dtype),
        grid=(bsz, L // tq),
        in_specs=[xspec, _full((1, d)), _full(wqt.shape), _full(wk.shape), _full(wvt.shape), _full(wz.shape),
                  _full(bvec.shape), _full(sink.shape), _full(w_out.shape), _full((1, d))],
        out_specs=xspec,
        scratch_shapes=[pltpu.VMEM((width, tq), F32),
                        pltpu.VMEM((2, SWA_KV_HEADS, 2 * W, SWA_GROUP * W), F32),
                        pltpu.VMEM((W, kvw), BF16),
                        pltpu.VMEM((kvw, W), BF16)],
        compiler_params=_cparams(("arbitrary", "arbitrary")),
        name="swa_layer",
    )(x, pre_g.reshape(1, d), wqt, wk, wvt, wz, bvec, sink, w_out, post_g.reshape(1, d))


def _mla_pre_kernel(x_ref, g_ref, w_ref, qn_ref, kvn_ref, wq_ref, wkv_ref, wvt_ref, cq_ref, sq_ref, ck_ref, sk_ref,
                    oqn_ref, oqr_ref, okn_ref, okr_ref, ov_ref, oz_ref, *, scale):
    nope = MLA_HEADS * MLA_NOPE
    rope = MLA_HEADS * MLA_ROPE
    vw = MLA_HEADS * MLA_V
    hb = _rms(x_ref[0], g_ref[...]).astype(BF16)
    o1 = MLA_Q_RANK
    o2 = o1 + MLA_KV_RANK
    o3 = o2 + vw
    cq = _dot(hb, w_ref[:, :o1])
    ckv = _dot(hb, w_ref[:, o1:o2])
    oz_ref[0] = _dot(hb, w_ref[:, o2:o3])
    kr = _dot(hb, w_ref[:, o3:o3 + LANES])
    krs = _dot(hb, w_ref[:, o3 + LANES:o3 + 2 * LANES])
    okr_ref[0] = (kr * ck_ref[...] + krs * sk_ref[...]).astype(BF16)
    cqb = _rms(cq, qn_ref[...]).astype(BF16)
    oqn_ref[0] = (_dot_nt(wq_ref[:nope], cqb) * scale).astype(BF16)
    qr = _dot_nt(wq_ref[nope:nope + rope], cqb)
    hr = MLA_ROPE // 2
    qrs = jnp.concatenate([qr[h * MLA_ROPE + o:h * MLA_ROPE + o + hr]
                           for h in range(MLA_HEADS) for o in (hr, 0)], axis=0)
    oqr_ref[0] = ((qr * cq_ref[...] + qrs * sq_ref[...]) * scale).astype(BF16)
    ckb = _rms(ckv, kvn_ref[...]).astype(BF16)
    okn_ref[0] = _dot(ckb, wkv_ref[:, :nope]).astype(BF16)
    vt = _dot_nt(wvt_ref[...], ckb).astype(BF16)
    tk = ov_ref.shape[3]
    for c in range(ov_ref.shape[1]):
        ov_ref[0, c] = vt[:, c * tk:(c + 1) * tk]


def _mla_attn_kernel(qn_ref, qr_ref, kn_ref, kr_ref, v_ref, z_ref, x_ref, wo_ref, g_ref, out_ref,
                     qs_ref, acc_ref, m_ref, o_ref):
    tq = qn_ref.shape[2]
    tk = v_ref.shape[3]
    npairs = MLA_HEADS // 2
    i = pl.program_id(1)
    tri = (lax.broadcasted_iota(jnp.int32, (LANES, LANES), 0)
           <= lax.broadcasted_iota(jnp.int32, (LANES, LANES), 1))

    zn = jnp.zeros((MLA_NOPE, tq), BF16)
    zr = jnp.zeros((LANES - MLA_ROPE, tq), BF16)
    for p in range(npairs):
        qn = qn_ref[0, p * LANES:(p + 1) * LANES, :]
        r0 = 2 * p * MLA_ROPE
        c0 = jnp.concatenate([qn[:MLA_NOPE], zn, qr_ref[0, r0:r0 + MLA_ROPE, :], zr], axis=0)
        c1 = jnp.concatenate([zn, qn[MLA_NOPE:], qr_ref[0, r0 + MLA_ROPE:r0 + 2 * MLA_ROPE, :], zr], axis=0)
        qs_ref[p] = jnp.concatenate([c0, c1], axis=1)

    m_ref[...] = jnp.full(m_ref.shape, NEG_INF, F32)
    acc_ref[...] = jnp.zeros(acc_ref.shape, F32)
    ones = jnp.ones((acc_ref.shape[1] - LANES, tk), BF16)

    def kv_steps(blocks):
        units = [(j, masked, p) for j, masked in blocks for p in range(npairs)]

        def scores(j, masked, p):
            ks = pl.multiple_of(j * tk, tk)
            kc = jnp.concatenate([kn_ref[0, pl.ds(ks, tk), p * LANES:(p + 1) * LANES],
                                  kr_ref[0, pl.ds(ks, tk), :]], axis=1)
            if not masked:
                chunks = [_dot(kc, qs_ref[p, :, c * 2 * LANES:(c + 1) * 2 * LANES]) for c in range(tq // LANES)]
                return [chunks[c // 2][:, (c % 2) * LANES:(c % 2 + 1) * LANES] for c in range(2 * tq // LANES)]
            lo = [_dot(kc[:LANES], qs_ref[p, :, c * 2 * LANES:(c + 1) * 2 * LANES]) for c in range(tq // LANES)]
            late_q = jnp.concatenate([qs_ref[p, :, LANES:2 * LANES], qs_ref[p, :, 3 * LANES:4 * LANES]], axis=1)
            hi = _dot(kc[LANES:], late_q)
            strips = []
            for hd in range(2):
                strips.append(jnp.where(tri, lo[hd][:, :LANES], NEG_INF))
                strips.append(jnp.concatenate([lo[hd][:, LANES:],
                                               jnp.where(tri, hi[:, hd * LANES:(hd + 1) * LANES], NEG_INF)], axis=0))
            return strips

        pending = [scores(*u) for u in units[:MLA_LOOKAHEAD]]
        late = []

        def flush():
            jj, pp, alpha, pr = late.pop(0)
            vones = jnp.concatenate([v_ref[0, jj, pp * LANES:(pp + 1) * LANES, :], ones], axis=0)
            acc_ref[pp] = alpha * acc_ref[pp] + _dot(vones, pr)

        for idx, (j, masked, p) in enumerate(units):
            s = pending.pop(0)
            if idx + MLA_LOOKAHEAD < len(units):
                pending.append(scores(*units[idx + MLA_LOOKAHEAD]))
            probs, alphas = [], []
            for c, sc in enumerate(s):
                m_prev = m_ref[p, :, c * LANES:(c + 1) * LANES]
                m_new = jnp.maximum(m_prev, jnp.max(sc, axis=0, keepdims=True))
                alphas.append(jnp.exp2(m_prev - m_new))
                pr = jnp.exp2(sc - m_new).astype(BF16)
                if pr.shape[0] < tk:
                    pr = jnp.concatenate([pr, jnp.zeros((tk - pr.shape[0], LANES), BF16)], axis=0)
                probs.append(pr)
                m_ref[p, :, c * LANES:(c + 1) * LANES] = m_new
            if len(late) == MLA_PV_DELAY:
                flush()
            late.append((j, p, jnp.concatenate(alphas, axis=1), jnp.concatenate(probs, axis=1)))
        while late:
            flush()

    def body(jj, c):
        kv_steps([(2 * jj, False), (2 * jj + 1, False)])
        return c

    lax.fori_loop(0, i // 2, body, 0)

    @pl.when(i % 2 == 1)
    def _():
        kv_steps([(i - 1, False), (i, True)])

    @pl.when(i % 2 == 0)
    def _():
        kv_steps([(i, True)])
    for p in range(npairs):
        a = acc_ref[p]
        a = a[:LANES] * (1.0 / a[LANES:LANES + 1])
        ot = jnp.concatenate([a[:MLA_V, :tq], a[MLA_V:, tq:]], axis=0)
        o_ref[:, p * LANES:(p + 1) * LANES] = ot.T
    gated = o_ref[...] * jax.nn.silu(z_ref[0])
    r = _dot(gated.astype(BF16), wo_ref[...].astype(BF16))
    out_ref[0] = x_ref[0] + _rms(r, g_ref[...])


def _mla_layer(x, pre_g, post_g, w_in, q_norm, kv_norm, w_uq, w_ukv, w_out):
    bsz, L, d = x.shape
    H = MLA_HEADS
    dq = MLA_NOPE + MLA_ROPE
    nope = H * MLA_NOPE
    rope = H * MLA_ROPE
    vw = H * MLA_V
    half = MLA_ROPE // 2
    o_kr = MLA_Q_RANK + MLA_KV_RANK
    o_z = o_kr + MLA_ROPE
    wb = w_in.astype(BF16)
    w_kr = wb[:, o_kr:o_z]
    w_krs = jnp.concatenate([w_kr[:, half:], w_kr[:, :half]], axis=1)
    reps = LANES // MLA_ROPE
    w1 = jnp.concatenate([wb[:, :o_kr], wb[:, o_z:]] + [w_kr] * reps + [w_krs] * reps, axis=1)
    wq3 = w_uq.astype(BF16).reshape(MLA_Q_RANK, H, dq)
    wqt = jnp.concatenate([wq3[:, :, :MLA_NOPE].reshape(MLA_Q_RANK, nope),
                           wq3[:, :, MLA_NOPE:].reshape(MLA_Q_RANK, rope)], axis=1).T
    wkv3 = w_ukv.astype(BF16).reshape(MLA_KV_RANK, H, MLA_NOPE + MLA_V)
    wkn = wkv3[:, :, :MLA_NOPE].reshape(MLA_KV_RANK, nope)
    wvt = wkv3[:, :, MLA_NOPE:].reshape(MLA_KV_RANK, vw).T
    inv = ROPE_BASE ** (-jnp.arange(0, MLA_ROPE, 2, dtype=F32) / MLA_ROPE)
    ang = jnp.arange(L, dtype=F32)[:, None] * inv[None, :]
    cos, sin = jnp.cos(ang), jnp.sin(ang)
    cos32 = jnp.concatenate([cos, cos], axis=1)
    sin32 = jnp.concatenate([-sin, sin], axis=1)
    cos_k, sin_k = jnp.tile(cos32, (1, LANES // MLA_ROPE)), jnp.tile(sin32, (1, LANES // MLA_ROPE))
    cos_q, sin_q = jnp.tile(cos32, (1, H)).T, jnp.tile(sin32, (1, H)).T

    tm = ROW_TILE
    tk = MLA_TK
    tok = lambda w_: pl.BlockSpec((1, tm, w_), lambda b, i: (b, i, 0))
    tokt = lambda w_: pl.BlockSpec((1, w_, tm), lambda b, i: (b, 0, i))
    scale = dq ** -0.5 * math.log2(math.e)
    qn, qr, kn, kr, v, z = pl.pallas_call(
        functools.partial(_mla_pre_kernel, scale=scale),
        out_shape=[jax.ShapeDtypeStruct((bsz, nope, L), BF16),
                   jax.ShapeDtypeStruct((bsz, rope, L), BF16),
                   jax.ShapeDtypeStruct((bsz, L, nope), BF16),
                   jax.ShapeDtypeStruct((bsz, L, LANES), BF16),
                   jax.ShapeDtypeStruct((bsz, L // tk, vw, tk), BF16),
                   jax.ShapeDtypeStruct((bsz, L, vw), F32)],
        grid=(bsz, L // tm),
        in_specs=[tok(d), _full((1, d)), _full(w1.shape), _full((1, MLA_Q_RANK)), _full((1, MLA_KV_RANK)),
                  _full(wqt.shape), _full(wkn.shape), _full(wvt.shape),
                  pl.BlockSpec((rope, tm), lambda b, i: (0, i)), pl.BlockSpec((rope, tm), lambda b, i: (0, i)),
                  pl.BlockSpec((tm, LANES), lambda b, i: (i, 0)), pl.BlockSpec((tm, LANES), lambda b, i: (i, 0))],
        out_specs=[tokt(nope), tokt(rope), tok(nope), tok(LANES),
                   pl.BlockSpec((1, tm // tk, vw, tk), lambda b, i: (b, i, 0, 0)), tok(vw)],
        compiler_params=_cparams(("parallel", "parallel")),
        name="mla_pre",
    )(x, pre_g.reshape(1, d), w1, q_norm.reshape(1, -1), kv_norm.reshape(1, -1), wqt, wkn, wvt,
      cos_q, sin_q, cos_k, sin_k)

    tq = MLA_TQ
    assert MLA_TQ == MLA_TK == 2 * LANES, "the diagonal-block handling works on 128-query strips of a 256 block"
    npairs = H // 2
    qspec = lambda w_: pl.BlockSpec((1, w_, tq), lambda b, i: (b, 0, i))
    kspec = lambda w_: pl.BlockSpec((1, L, w_), lambda b, i: (b, 0, 0))
    rowspec = lambda w_: pl.BlockSpec((1, tq, w_), lambda b, i: (b, i, 0))
    return pl.pallas_call(
        _mla_attn_kernel,
        out_shape=jax.ShapeDtypeStruct(x.shape, x.dtype),
        grid=(bsz, L // tq),
        in_specs=[qspec(nope), qspec(rope), kspec(nope), kspec(LANES),
                  pl.BlockSpec((1, L // tk, vw, tk), lambda b, i: (b, 0, 0, 0)),
                  rowspec(vw), rowspec(d), _full(w_out.shape), _full((1, d))],
        out_specs=rowspec(d),
        scratch_shapes=[pltpu.VMEM((npairs, 2 * LANES, 2 * tq), BF16),
                        pltpu.VMEM((npairs, LANES + 16, 2 * tq), F32),
                        pltpu.VMEM((npairs, 1, 2 * tq), F32),
                        pltpu.VMEM((tq, vw), F32)],
        compiler_params=_cparams(("parallel", "arbitrary")),
        name="mla_attn",
    )(qn, qr, kn, kr, v, z, x, w_out, post_g.reshape(1, d))


def _sgu_kernel(x_ref, g_ref, w_ref, lng_ref, lnb_ref, ws_ref, bs_ref, wo_ref, pg_ref, out_ref, s_ref):
    width = wo_ref.shape[0]
    tm = x_ref.shape[1]
    lane = lax.broadcasted_iota(jnp.int32, (1, LANES), 1)
    lo = lane < HALF
    x = x_ref[0]
    hb = _rms(x, g_ref[...]).astype(BF16)
    v = jax.nn.gelu(_dot(hb, w_ref[:, width:2 * width].astype(BF16)))
    mu = jnp.mean(v, axis=-1, keepdims=True)
    vc = v - mu
    var = jnp.mean(vc * vc, axis=-1, keepdims=True)
    vb = (vc * lax.rsqrt(var + EPS) * lng_ref[...] + lnb_ref[...]).astype(BF16)
    group = SGU_STACK
    for c0 in range(0, tm // SGU_CHUNK, group):
        for jj in range(width // LANES):
            blk = jnp.concatenate([vb[c * SGU_CHUNK:(c + 1) * SGU_CHUNK, jj * LANES:(jj + 1) * LANES]
                                   for c in range(c0, c0 + group)], axis=1)
            r = _dot(ws_ref[jj], blk)
            for k in range(group):
                c = c0 + k
                s_ref[c * SGU_CHUNK:(c + 1) * SGU_CHUNK, jj * LANES:(jj + 1) * LANES] = (
                    jnp.where(lo, r[:SGU_CHUNK, k * LANES:(k + 1) * LANES],
                              r[SGU_CHUNK:, k * LANES:(k + 1) * LANES]) + bs_ref[jj])
    u = jax.nn.gelu(_dot(hb, w_ref[:, :width].astype(BF16)))
    z = _dot(hb, w_ref[:, 2 * width:].astype(BF16))
    o = u * s_ref[...] * jax.nn.silu(z)
    r = _dot(o.astype(BF16), wo_ref[...].astype(BF16))
    out_ref[0] = x + _rms(r, pg_ref[...])


def _sgu_layer(x, pre_g, post_g, w_in, ln_g, ln_b, w_s, b_s, w_out):
    bsz, L, d = x.shape
    width = w_out.shape[0]
    T = SGU_CHUNK
    gd = width // SGU_GROUPS
    tril = jnp.tril(jnp.ones((T, T), dtype=bool))
    ws = jnp.where(tril[None], w_s, 0.0).reshape(SGU_GROUPS // 2, 2 * T, T).astype(BF16)
    bs = jnp.repeat(b_s.astype(F32).T, gd, axis=1)
    bs = bs.reshape(T, width // LANES, LANES).transpose(1, 0, 2)
    tm = ROW_TILE
    return pl.pallas_call(
        _sgu_kernel,
        out_shape=jax.ShapeDtypeStruct(x.shape, x.dtype),
        grid=(bsz, L // tm),
        in_specs=[pl.BlockSpec((1, tm, d), lambda b, i: (b, i, 0)),
                  _full((1, d)), _full(w_in.shape), _full((1, width)), _full((1, width)),
                  _full(ws.shape), _full(bs.shape), _full(w_out.shape), _full((1, d))],
        out_specs=pl.BlockSpec((1, tm, d), lambda b, i: (b, i, 0)),
        scratch_shapes=[pltpu.VMEM((tm, width), F32)],
        compiler_params=_cparams(("parallel", "parallel")),
        name="sgu",
    )(x, pre_g.reshape(1, d), w_in, ln_g.reshape(1, width), ln_b.reshape(1, width),
      ws, bs, w_out, post_g.reshape(1, d))


def kernel(x, pre_norm, post_norm, rel_bias, a_w_in, a_lam_re, a_lam_im, a_log_dt, a_b_re, a_b_im, a_c_re, a_c_im, a_d, a_w_glu, a_b_glu, a_w_out, b_w_in, b_sinks, b_w_out, c_w_in, c_q_norm, c_kv_norm, c_w_uq, c_w_ukv, c_w_out, d_w_in, d_ln_g, d_ln_b, d_w_s, d_b_s, d_w_out):
    depth = pre_norm.shape[0]
    for i in range(depth):
        kind, j = i % 4, i // 4
        if kind == 0:
            x = _s5_layer(x, pre_norm[i], post_norm[i], a_w_in[j], a_lam_re[j], a_lam_im[j], a_log_dt[j],
                          a_b_re[j], a_b_im[j], a_c_re[j], a_c_im[j], a_d[j], a_w_glu[j], a_b_glu[j],
                          a_w_out[j])
        elif kind == 1:
            x = _swa_layer(x, pre_norm[i], post_norm[i], b_w_in[j], b_sinks[j], b_w_out[j], rel_bias)
        elif kind == 2:
            x = _mla_layer(x, pre_norm[i], post_norm[i], c_w_in[j], c_q_norm[j], c_kv_norm[j], c_w_uq[j],
                           c_w_ukv[j], c_w_out[j])
        else:
            x = _sgu_layer(x, pre_norm[i], post_norm[i], d_w_in[j], d_ln_g[j], d_ln_b[j], d_w_s[j],
                           d_b_s[j], d_w_out[j])
    return x
```

```python
import functools
import math

import jax
import jax.numpy as jnp
from jax import lax
from jax.experimental import pallas as pl
from jax.experimental.pallas import tpu as pltpu

F32 = jnp.float32
BF16 = jnp.bfloat16

EPS = 1e-6
NEG_INF = -1e30
LANES = 128
HALF = LANES // 2

SSM_GROUP = 16
SSM_STATE = 64
S5_CH_BLOCK = LANES
S5_GROUPS_PER_BLOCK = S5_CH_BLOCK // SSM_GROUP
S5_STATE_BLOCK = S5_GROUPS_PER_BLOCK * SSM_STATE
S5_T = 64

HEAD_DIM = 64
SWA_HEADS = 16
SWA_KV_HEADS = 2
SWA_GROUP = SWA_HEADS // SWA_KV_HEADS
WINDOW = 128
SWA_WINDOWS_PER_STEP = 8
SWA_UNIT_HEADS = 8
SWA_LOOKAHEAD = 2
SWA_PV_DELAY = 1
REL_BUCKETS = 32
REL_MAX_DIST = 128

MLA_HEADS = 16
MLA_NOPE = 64
MLA_ROPE = 32
MLA_V = 64
MLA_KV_RANK = 256
MLA_Q_RANK = 768
ROPE_BASE = 10000.0
MLA_TQ = 256
MLA_TK = 256
MLA_LOOKAHEAD = 12
MLA_PV_DELAY = 4

SGU_CHUNK = 128
SGU_GROUPS = 16
SGU_STACK = 4

ROW_TILE = 1024
VMEM_LIMIT = 56 * 1024 * 1024


def _cparams(sem):
    return pltpu.CompilerParams(dimension_semantics=sem, vmem_limit_bytes=VMEM_LIMIT)


def _rms(x, g):
    return x * lax.rsqrt(jnp.mean(x * x, axis=-1, keepdims=True) + EPS) * g


def _dot(a, b):
    return jnp.dot(a, b, preferred_element_type=F32)


def _dot_nt(a, b):
    return lax.dot_general(a, b, (((1,), (1,)), ((), ())), preferred_element_type=F32)


def _full(shape):
    n = len(shape)
    return pl.BlockSpec(shape, lambda *_: (0,) * n, pipeline_mode=pl.Buffered(1))


def _s5_kernel(x_ref, g_ref, w_ref, bb_ref, cc_ref, ar_ref, ai_ref, d_ref,
               wg_ref, bg_ref, wo_ref, pg_ref, out_ref, u_ref, z_ref, y_ref, s_ref, carry_ref, *, tt):
    bsz = x_ref.shape[0]
    width = wg_ref.shape[0]
    rows = bsz * tt
    nblk = bb_ref.shape[0]
    sb = S5_STATE_BLOCK

    @pl.when(pl.program_id(0) == 0)
    def _():
        carry_ref[...] = jnp.zeros_like(carry_ref)

    x = x_ref[...].reshape(rows, x_ref.shape[2])
    h = _rms(x, g_ref[...])
    hb = jnp.swapaxes(h.reshape(bsz, tt, h.shape[1]), 0, 1).reshape(rows, h.shape[1]).astype(BF16)
    u_ref[...] = _dot(hb, w_ref[:, :width].astype(BF16))
    z_ref[...] = _dot(hb, w_ref[:, width:].astype(BF16))

    nbuf = s_ref.shape[0]

    def project_in(i):
        s_ref[i % nbuf] = _dot(u_ref[:, i * LANES:(i + 1) * LANES].astype(BF16), bb_ref[i])

    def project_out(i):
        ub = u_ref[:, i * LANES:(i + 1) * LANES]
        y = _dot(s_ref[i % nbuf].astype(BF16), cc_ref[i]) + d_ref[:, i * LANES:(i + 1) * LANES] * ub
        y_ref[:, i * LANES:(i + 1) * LANES] = jax.nn.gelu(y)

    project_in(0)
    for i in range(nblk):
        if i + 1 < nblk:
            project_in(i + 1)
        buf = s_ref.at[i % nbuf]
        ar = ar_ref[i]
        ai = ai_ref[i]
        sr = carry_ref[i, :, 0:sb]
        si = carry_ref[i, :, sb:2 * sb]
        for t in range(tt):
            r0 = t * bsz
            nr = ar * sr - ai * si + buf[r0:r0 + bsz, 0:sb]
            ni = ar * si + ai * sr + buf[r0:r0 + bsz, sb:2 * sb]
            buf[r0:r0 + bsz, 0:sb] = nr
            buf[r0:r0 + bsz, sb:2 * sb] = ni
            sr, si = nr, ni
        carry_ref[i, :, 0:sb] = sr
        carry_ref[i, :, sb:2 * sb] = si
        project_out(i)

    y = y_ref[...]
    gate = jax.nn.sigmoid(_dot(y.astype(BF16), wg_ref[...].astype(BF16)) + bg_ref[...])
    o = y * gate * jax.nn.silu(z_ref[...])
    ob = jnp.swapaxes(o.reshape(tt, bsz, o.shape[1]), 0, 1).reshape(rows, o.shape[1]).astype(BF16)
    r = _dot(ob, wo_ref[...].astype(BF16))
    out_ref[...] = (x + _rms(r, pg_ref[...])).reshape(out_ref.shape)


def _s5_discretize(lam_re, lam_im, log_dt, b_re, b_im):
    dt = jnp.exp(log_dt)[:, None]
    mag = jnp.exp(lam_re * dt)
    ab_re = mag * jnp.cos(lam_im * dt)
    ab_im = mag * jnp.sin(lam_im * dt)
    den = lam_re * lam_re + lam_im * lam_im
    nr = ab_re - 1.0
    f_re = (nr * lam_re + ab_im * lam_im) / den
    f_im = (ab_im * lam_re - nr * lam_im) / den
    bb_re = f_re[..., None] * b_re - f_im[..., None] * b_im
    bb_im = f_re[..., None] * b_im + f_im[..., None] * b_re
    return ab_re, ab_im, bb_re, bb_im


def _s5_layer(x, pre_g, post_g, w_in, lam_re, lam_im, log_dt, b_re, b_im, c_re, c_im, d_skip,
              w_glu, b_glu, w_out):
    bsz, L, d = x.shape
    width = w_in.shape[1] // 2
    nblk = width // S5_CH_BLOCK
    gpb = S5_GROUPS_PER_BLOCK
    tt = S5_T
    rows = bsz * tt

    ab_re, ab_im, bb_re, bb_im = _s5_discretize(lam_re, lam_im, log_dt, b_re, b_im)
    eye = jnp.eye(gpb, dtype=F32)

    def pack_b(bb):
        t = bb.reshape(nblk, gpb, SSM_STATE, SSM_GROUP)
        return jnp.einsum('igph,gk->ikhgp', t, eye).reshape(nblk, S5_CH_BLOCK, S5_STATE_BLOCK)

    def pack_c(cc):
        t = cc.reshape(nblk, gpb, SSM_GROUP, SSM_STATE)
        return jnp.einsum('ighp,gk->igpkh', t, eye).reshape(nblk, S5_STATE_BLOCK, S5_CH_BLOCK)

    bb = jnp.concatenate([pack_b(bb_re), pack_b(bb_im)], axis=2).astype(BF16)
    cc = jnp.concatenate([pack_c(c_re), -pack_c(c_im)], axis=1).astype(BF16)
    ar = jnp.broadcast_to(ab_re.reshape(nblk, 1, S5_STATE_BLOCK), (nblk, bsz, S5_STATE_BLOCK))
    ai = jnp.broadcast_to(ab_im.reshape(nblk, 1, S5_STATE_BLOCK), (nblk, bsz, S5_STATE_BLOCK))

    xspec = pl.BlockSpec((bsz, tt, d), lambda i: (0, i, 0))
    return pl.pallas_call(
        functools.partial(_s5_kernel, tt=tt),
        out_shape=jax.ShapeDtypeStruct(x.shape, x.dtype),
        grid=(L // tt,),
        in_specs=[xspec, _full((1, d)), _full(w_in.shape),
                  _full(bb.shape), _full(cc.shape), _full(ar.shape), _full(ai.shape), _full((1, width)),
                  _full(w_glu.shape), _full((1, width)), _full(w_out.shape), _full((1, d))],
        out_specs=xspec,
        scratch_shapes=[pltpu.VMEM((rows, width), F32),
                        pltpu.VMEM((rows, width), F32),
                        pltpu.VMEM((rows, width), F32),
                        pltpu.VMEM((2, rows, 2 * S5_STATE_BLOCK), F32),
                        pltpu.VMEM((nblk, bsz, 2 * S5_STATE_BLOCK), F32)],
        compiler_params=_cparams(("arbitrary",)),
        name="s5_layer",
    )(x, pre_g.reshape(1, d), w_in, bb, cc, ar, ai, d_skip.reshape(1, width),
      w_glu, b_glu.reshape(1, width), w_out, post_g.reshape(1, d))


def _swa_bias(rel_bias):
    W = WINDOW
    n = 4 * W
    dist = jnp.arange(n) - W
    valid = jnp.logical_and(dist >= 0, dist < W)
    dpos = jnp.maximum(dist, 0)
    max_exact = REL_BUCKETS // 2
    dist_f = jnp.maximum(dpos, 1).astype(F32)
    large = max_exact + (jnp.log(dist_f / max_exact) / math.log(REL_MAX_DIST / max_exact)
                         * (REL_BUCKETS - max_exact)).astype(jnp.int32)
    large = jnp.minimum(large, REL_BUCKETS - 1)
    bucket = jnp.where(dpos < max_exact, dpos, large)
    return jnp.where(valid[:, None], rel_bias[bucket].astype(F32), NEG_INF).T


def _swa_kernel(x_ref, pg_ref, wqt_ref, wk_ref, wvt_ref, wz_ref, bvec_ref, sink_ref, wo_ref, g_ref,
                out_ref, ot_ref, bias_ref, kprev_ref, vtprev_ref, *, scale):
    W = WINDOW
    nwin = x_ref.shape[1] // W
    step = pl.program_id(1)

    @pl.when(step == 0)
    def _():
        kprev_ref[...] = jnp.zeros_like(kprev_ref)
        vtprev_ref[...] = jnp.zeros_like(vtprev_ref)

    x = x_ref[0]
    hb = _rms(x, pg_ref[...]).astype(BF16)
    qt = (_dot_nt(wqt_ref[...], hb) * scale).astype(BF16)
    k = _dot(hb, wk_ref[...]).astype(BF16)
    vt = _dot_nt(wvt_ref[...], hb).astype(BF16)
    z = _dot(hb, wz_ref[...])

    @pl.when(jnp.logical_and(pl.program_id(0) == 0, step == 0))
    def _():
        no_prev = lax.broadcasted_iota(jnp.int32, (2 * W, W), 0) < W
        for hd in range(SWA_HEADS):
            base = jnp.broadcast_to(bvec_ref[hd:hd + 1, :], (2 * W, bvec_ref.shape[1]))
            toep = pltpu.roll(base, 0, 1, stride=1, stride_axis=0)[:, 2 * W:3 * W]
            h, g = divmod(hd, SWA_GROUP)
            bias_ref[0, h, :, g * W:(g + 1) * W] = toep
            bias_ref[1, h, :, g * W:(g + 1) * W] = jnp.where(no_prev, NEG_INF, toep)
    kall = jnp.concatenate([kprev_ref[...], k], axis=0)
    vtall = jnp.concatenate([vtprev_ref[...], vt], axis=1)
    kprev_ref[...] = k[(nwin - 1) * W:]
    vtprev_ref[...] = vt[:, (nwin - 1) * W:]
    nsub = SWA_UNIT_HEADS
    zq = jnp.zeros((HEAD_DIM, nsub * W), BF16)
    ones = jnp.ones((16, 2 * W), BF16)
    units = [(w, h, c) for w in range(nwin) for h in range(SWA_KV_HEADS) for c in range(SWA_GROUP // nsub)]

    def scores(w, h, c):
        hd0 = h * SWA_GROUP + c * nsub
        qh = jnp.concatenate([qt[(hd0 + g) * HEAD_DIM:(hd0 + g + 1) * HEAD_DIM, w * W:(w + 1) * W]
                              for g in range(nsub)], axis=1)
        qz = jnp.concatenate([qh, zq] if h == 0 else [zq, qh], axis=0)
        return _dot(kall[w * W:(w + 2) * W], qz)

    pending = [scores(*u) for u in units[:SWA_LOOKAHEAD]]
    late = []

    def flush():
        (w, h, c), p, tail = late.pop(0)
        vones = jnp.concatenate([vtall[h * HEAD_DIM:(h + 1) * HEAD_DIM, w * W:(w + 2) * W], ones], axis=0)
        o = _dot(vones, p)
        oh = o[:HEAD_DIM] * (1.0 / (o[HEAD_DIM:HEAD_DIM + 1] + tail))
        for g in range(nsub):
            hd = h * SWA_GROUP + c * nsub + g
            ot_ref[hd * HEAD_DIM:(hd + 1) * HEAD_DIM, w * W:(w + 1) * W] = oh[:, g * W:(g + 1) * W]

    for idx, (w, h, c) in enumerate(units):
        raw = pending.pop(0)
        if idx + SWA_LOOKAHEAD < len(units):
            pending.append(scores(*units[idx + SWA_LOOKAHEAD]))
        cols = slice(c * nsub * W, (c + 1) * nsub * W)
        variant = (step == 0).astype(jnp.int32) if w == 0 else 0
        s = raw + bias_ref[variant, h, :, cols]
        sink = sink_ref[h, :, cols]
        m = jnp.maximum(jnp.max(s, axis=0, keepdims=True), sink)
        if len(late) == SWA_PV_DELAY:
            flush()
        late.append(((w, h, c), jnp.exp2(s - m).astype(BF16), jnp.exp2(sink - m)))
    while late:
        flush()
    gated = ot_ref[...].T * jax.nn.silu(z)
    r = _dot(gated.astype(BF16), wo_ref[...].astype(BF16))
    out_ref[0] = x + _rms(r, g_ref[...])


def _swa_layer(x, pre_g, post_g, w_in, sinks, w_out, rel_bias):
    bsz, L, d = x.shape
    width = SWA_HEADS * HEAD_DIM
    kvw = SWA_KV_HEADS * HEAD_DIM
    W = WINDOW
    log2e = math.log2(math.e)
    wb = w_in.astype(BF16)
    wqt = wb[:, :width].T
    wk = wb[:, width:width + kvw]
    wvt = wb[:, width + kvw:width + 2 * kvw].T
    wz = wb[:, width + 2 * kvw:]
    bvec = _swa_bias(rel_bias.astype(F32) * log2e)
    sink = jnp.repeat(sinks.astype(F32) * log2e, W).reshape(SWA_KV_HEADS, 1, SWA_GROUP * W)

    tq = SWA_WINDOWS_PER_STEP * W
    xspec = pl.BlockSpec((1, tq, d), lambda b, n: (b, n, 0))
    return pl.pallas_call(
        functools.partial(_swa_kernel, scale=HEAD_DIM ** -0.5 * log2e),
        out_shape=jax.ShapeDtypeStruct(x.shape, x.dtype),
        grid=(bsz, L // tq),
        in_specs=[xspec, _full((1, d)), _full(wqt.shape), _full(wk.shape), _full(wvt.shape), _full(wz.shape),
                  _full(bvec.shape), _full(sink.shape), _full(w_out.shape), _full((1, d))],
        out_specs=xspec,
        scratch_shapes=[pltpu.VMEM((width, tq), F32),
                        pltpu.VMEM((2, SWA_KV_HEADS, 2 * W, SWA_GROUP * W), F32),
                        pltpu.VMEM((W, kvw), BF16),
                        pltpu.VMEM((kvw, W), BF16)],
        compiler_params=_cparams(("arbitrary", "arbitrary")),
        name="swa_layer",
    )(x, pre_g.reshape(1, d), wqt, wk, wvt, wz, bvec, sink, w_out, post_g.reshape(1, d))


def _mla_pre_kernel(x_ref, g_ref, w_ref, qn_ref, kvn_ref, wq_ref, wkv_ref, wvt_ref, cq_ref, sq_ref, ck_ref, sk_ref,
                    oqn_ref, oqr_ref, okn_ref, okr_ref, ov_ref, oz_ref, *, scale):
    nope = MLA_HEADS * MLA_NOPE
    rope = MLA_HEADS * MLA_ROPE
    vw = MLA_HEADS * MLA_V
    hb = _rms(x_ref[0], g_ref[...]).astype(BF16)
    o1 = MLA_Q_RANK
    o2 = o1 + MLA_KV_RANK
    o3 = o2 + vw
    cq = _dot(hb, w_ref[:, :o1])
    ckv = _dot(hb, w_ref[:, o1:o2])
    oz_ref[0] = _dot(hb, w_ref[:, o2:o3])
    kr = _dot(hb, w_ref[:, o3:o3 + LANES])
    krs = _dot(hb, w_ref[:, o3 + LANES:o3 + 2 * LANES])
    okr_ref[0] = (kr * ck_ref[...] + krs * sk_ref[...]).astype(BF16)
    cqb = _rms(cq, qn_ref[...]).astype(BF16)
    oqn_ref[0] = (_dot_nt(wq_ref[:nope], cqb) * scale).astype(BF16)
    qr = _dot_nt(wq_ref[nope:nope + rope], cqb)
    hr = MLA_ROPE // 2
    qrs = jnp.concatenate([qr[h * MLA_ROPE + o:h * MLA_ROPE + o + hr]
                           for h in range(MLA_HEADS) for o in (hr, 0)], axis=0)
    oqr_ref[0] = ((qr * cq_ref[...] + qrs * sq_ref[...]) * scale).astype(BF16)
    ckb = _rms(ckv, kvn_ref[...]).astype(BF16)
    okn_ref[0] = _dot(ckb, wkv_ref[:, :nope]).astype(BF16)
    vt = _dot_nt(wvt_ref[...], ckb).astype(BF16)
    tk = ov_ref.shape[3]
    for c in range(ov_ref.shape[1]):
        ov_ref[0, c] = vt[:, c * tk:(c + 1) * tk]


def _mla_attn_kernel(qn_ref, qr_ref, kn_ref, kr_ref, v_ref, z_ref, x_ref, wo_ref, g_ref, out_ref,
                     qs_ref, acc_ref, m_ref, o_ref):
    tq = qn_ref.shape[2]
    tk = v_ref.shape[3]
    npairs = MLA_HEADS // 2
    i = pl.program_id(1)
    tri = (lax.broadcasted_iota(jnp.int32, (LANES, LANES), 0)
           <= lax.broadcasted_iota(jnp.int32, (LANES, LANES), 1))

    zn = jnp.zeros((MLA_NOPE, tq), BF16)
    zr = jnp.zeros((LANES - MLA_ROPE, tq), BF16)
    for p in range(npairs):
        qn = qn_ref[0, p * LANES:(p + 1) * LANES, :]
        r0 = 2 * p * MLA_ROPE
        c0 = jnp.concatenate([qn[:MLA_NOPE], zn, qr_ref[0, r0:r0 + MLA_ROPE, :], zr], axis=0)
        c1 = jnp.concatenate([zn, qn[MLA_NOPE:], qr_ref[0, r0 + MLA_ROPE:r0 + 2 * MLA_ROPE, :], zr], axis=0)
        qs_ref[p] = jnp.concatenate([c0, c1], axis=1)

    m_ref[...] = jnp.full(m_ref.shape, NEG_INF, F32)
    acc_ref[...] = jnp.zeros(acc_ref.shape, F32)
    ones = jnp.ones((acc_ref.shape[1] - LANES, tk), BF16)

    def kv_steps(blocks):
        units = [(j, masked, p) for j, masked in blocks for p in range(npairs)]

        def scores(j, masked, p):
            ks = pl.multiple_of(j * tk, tk)
            kc = jnp.concatenate([kn_ref[0, pl.ds(ks, tk), p * LANES:(p + 1) * LANES],
                                  kr_ref[0, pl.ds(ks, tk), :]], axis=1)
            if not masked:
                chunks = [_dot(kc, qs_ref[p, :, c * 2 * LANES:(c + 1) * 2 * LANES]) for c in range(tq // LANES)]
                return [chunks[c // 2][:, (c % 2) * LANES:(c % 2 + 1) * LANES] for c in range(2 * tq // LANES)]
            lo = [_dot(kc[:LANES], qs_ref[p, :, c * 2 * LANES:(c + 1) * 2 * LANES]) for c in range(tq // LANES)]
            late_q = jnp.concatenate([qs_ref[p, :, LANES:2 * LANES], qs_ref[p, :, 3 * LANES:4 * LANES]], axis=1)
            hi = _dot(kc[LANES:], late_q)
            strips = []
            for hd in range(2):
                strips.append(jnp.where(tri, lo[hd][:, :LANES], NEG_INF))
                strips.append(jnp.concatenate([lo[hd][:, LANES:],
                                               jnp.where(tri, hi[:, hd * LANES:(hd + 1) * LANES], NEG_INF)], axis=0))
            return strips

        pending = [scores(*u) for u in units[:MLA_LOOKAHEAD]]
        late = []

        def flush():
            jj, pp, alpha, pr = late.pop(0)
            vones = jnp.concatenate([v_ref[0, jj, pp * LANES:(pp + 1) * LANES, :], ones], axis=0)
            acc_ref[pp] = alpha * acc_ref[pp] + _dot(vones, pr)

        for idx, (j, masked, p) in enumerate(units):
            s = pending.pop(0)
            if idx + MLA_LOOKAHEAD < len(units):
                pending.append(scores(*units[idx + MLA_LOOKAHEAD]))
            probs, alphas = [], []
            for c, sc in enumerate(s):
                m_prev = m_ref[p, :, c * LANES:(c + 1) * LANES]
                m_new = jnp.maximum(m_prev, jnp.max(sc, axis=0, keepdims=True))
                alphas.append(jnp.exp2(m_prev - m_new))
                pr = jnp.exp2(sc - m_new).astype(BF16)
                if pr.shape[0] < tk:
                    pr = jnp.concatenate([pr, jnp.zeros((tk - pr.shape[0], LANES), BF16)], axis=0)
                probs.append(pr)
                m_ref[p, :, c * LANES:(c + 1) * LANES] = m_new
            if len(late) == MLA_PV_DELAY:
                flush()
            late.append((j, p, jnp.concatenate(alphas, axis=1), jnp.concatenate(probs, axis=1)))
        while late:
            flush()

    def body(jj, c):
        kv_steps([(2 * jj, False), (2 * jj + 1, False)])
        return c

    lax.fori_loop(0, i // 2, body, 0)

    @pl.when(i % 2 == 1)
    def _():
        kv_steps([(i - 1, False), (i, True)])

    @pl.when(i % 2 == 0)
    def _():
        kv_steps([(i, True)])
    for p in range(npairs):
        a = acc_ref[p]
        a = a[:LANES] * (1.0 / a[LANES:LANES + 1])
        ot = jnp.concatenate([a[:MLA_V, :tq], a[MLA_V:, tq:]], axis=0)
        o_ref[:, p * LANES:(p + 1) * LANES] = ot.T
    gated = o_ref[...] * jax.nn.silu(z_ref[0])
    r = _dot(gated.astype(BF16), wo_ref[...].astype(BF16))
    out_ref[0] = x_ref[0] + _rms(r, g_ref[...])


def _mla_layer(x, pre_g, post_g, w_in, q_norm, kv_norm, w_uq, w_ukv, w_out):
    bsz, L, d = x.shape
    H = MLA_HEADS
    dq = MLA_NOPE + MLA_ROPE
    nope = H * MLA_NOPE
    rope = H * MLA_ROPE
    vw = H * MLA_V
    half = MLA_ROPE // 2
    o_kr = MLA_Q_RANK + MLA_KV_RANK
    o_z = o_kr + MLA_ROPE
    wb = w_in.astype(BF16)
    w_kr = wb[:, o_kr:o_z]
    w_krs = jnp.concatenate([w_kr[:, half:], w_kr[:, :half]], axis=1)
    reps = LANES // MLA_ROPE
    w1 = jnp.concatenate([wb[:, :o_kr], wb[:, o_z:]] + [w_kr] * reps + [w_krs] * reps, axis=1)
    wq3 = w_uq.astype(BF16).reshape(MLA_Q_RANK, H, dq)
    wqt = jnp.concatenate([wq3[:, :, :MLA_NOPE].reshape(MLA_Q_RANK, nope),
                           wq3[:, :, MLA_NOPE:].reshape(MLA_Q_RANK, rope)], axis=1).T
    wkv3 = w_ukv.astype(BF16).reshape(MLA_KV_RANK, H, MLA_NOPE + MLA_V)
    wkn = wkv3[:, :, :MLA_NOPE].reshape(MLA_KV_RANK, nope)
    wvt = wkv3[:, :, MLA_NOPE:].reshape(MLA_KV_RANK, vw).T
    inv = ROPE_BASE ** (-jnp.arange(0, MLA_ROPE, 2, dtype=F32) / MLA_ROPE)
    ang = jnp.arange(L, dtype=F32)[:, None] * inv[None, :]
    cos, sin = jnp.cos(ang), jnp.sin(ang)
    cos32 = jnp.concatenate([cos, cos], axis=1)
    sin32 = jnp.concatenate([-sin, sin], axis=1)
    cos_k, sin_k = jnp.tile(cos32, (1, LANES // MLA_ROPE)), jnp.tile(sin32, (1, LANES // MLA_ROPE))
    cos_q, sin_q = jnp.tile(cos32, (1, H)).T, jnp.tile(sin32, (1, H)).T

    tm = ROW_TILE
    tk = MLA_TK
    tok = lambda w_: pl.BlockSpec((1, tm, w_), lambda b, i: (b, i, 0))
    tokt = lambda w_: pl.BlockSpec((1, w_, tm), lambda b, i: (b, 0, i))
    scale = dq ** -0.5 * math.log2(math.e)
    qn, qr, kn, kr, v, z = pl.pallas_call(
        functools.partial(_mla_pre_kernel, scale=scale),
        out_shape=[jax.ShapeDtypeStruct((bsz, nope, L), BF16),
                   jax.ShapeDtypeStruct((bsz, rope, L), BF16),
                   jax.ShapeDtypeStruct((bsz, L, nope), BF16),
                   jax.ShapeDtypeStruct((bsz, L, LANES), BF16),
                   jax.ShapeDtypeStruct((bsz, L // tk, vw, tk), BF16),
                   jax.ShapeDtypeStruct((bsz, L, vw), F32)],
        grid=(bsz, L // tm),
        in_specs=[tok(d), _full((1, d)), _full(w1.shape), _full((1, MLA_Q_RANK)), _full((1, MLA_KV_RANK)),
                  _full(wqt.shape), _full(wkn.shape), _full(wvt.shape),
                  pl.BlockSpec((rope, tm), lambda b, i: (0, i)), pl.BlockSpec((rope, tm), lambda b, i: (0, i)),
                  pl.BlockSpec((tm, LANES), lambda b, i: (i, 0)), pl.BlockSpec((tm, LANES), lambda b, i: (i, 0))],
        out_specs=[tokt(nope), tokt(rope), tok(nope), tok(LANES),
                   pl.BlockSpec((1, tm // tk, vw, tk), lambda b, i: (b, i, 0, 0)), tok(vw)],
        compiler_params=_cparams(("parallel", "parallel")),
        name="mla_pre",
    )(x, pre_g.reshape(1, d), w1, q_norm.reshape(1, -1), kv_norm.reshape(1, -1), wqt, wkn, wvt,
      cos_q, sin_q, cos_k, sin_k)

    tq = MLA_TQ
    assert MLA_TQ == MLA_TK == 2 * LANES, "the diagonal-block handling works on 128-query strips of a 256 block"
    npairs = H // 2
    qspec = lambda w_: pl.BlockSpec((1, w_, tq), lambda b, i: (b, 0, i))
    kspec = lambda w_: pl.BlockSpec((1, L, w_), lambda b, i: (b, 0, 0))
    rowspec = lambda w_: pl.BlockSpec((1, tq, w_), lambda b, i: (b, i, 0))
    return pl.pallas_call(
        _mla_attn_kernel,
        out_shape=jax.ShapeDtypeStruct(x.shape, x.dtype),
        grid=(bsz, L // tq),
        in_specs=[qspec(nope), qspec(rope), kspec(nope), kspec(LANES),
                  pl.BlockSpec((1, L // tk, vw, tk), lambda b, i: (b, 0, 0, 0)),
                  rowspec(vw), rowspec(d), _full(w_out.shape), _full((1, d))],
        out_specs=rowspec(d),
        scratch_shapes=[pltpu.VMEM((npairs, 2 * LANES, 2 * tq), BF16),
                        pltpu.VMEM((npairs, LANES + 16, 2 * tq), F32),
                        pltpu.VMEM((npairs, 1, 2 * tq), F32),
                        pltpu.VMEM((tq, vw), F32)],
        compiler_params=_cparams(("parallel", "arbitrary")),
        name="mla_attn",
    )(qn, qr, kn, kr, v, z, x, w_out, post_g.reshape(1, d))


def _sgu_kernel(x_ref, g_ref, w_ref, lng_ref, lnb_ref, ws_ref, bs_ref, wo_ref, pg_ref, out_ref, s_ref):
    width = wo_ref.shape[0]
    tm = x_ref.shape[1]
    lane = lax.broadcasted_iota(jnp.int32, (1, LANES), 1)
    lo = lane < HALF
    x = x_ref[0]
    hb = _rms(x, g_ref[...]).astype(BF16)
    v = jax.nn.gelu(_dot(hb, w_ref[:, width:2 * width].astype(BF16)))
    mu = jnp.mean(v, axis=-1, keepdims=True)
    vc = v - mu
    var = jnp.mean(vc * vc, axis=-1, keepdims=True)
    vb = (vc * lax.rsqrt(var + EPS) * lng_ref[...] + lnb_ref[...]).astype(BF16)
    group = SGU_STACK
    for c0 in range(0, tm // SGU_CHUNK, group):
        for jj in range(width // LANES):
            blk = jnp.concatenate([vb[c * SGU_CHUNK:(c + 1) * SGU_CHUNK, jj * LANES:(jj + 1) * LANES]
                                   for c in range(c0, c0 + group)], axis=1)
            r = _dot(ws_ref[jj], blk)
            for k in range(group):
                c = c0 + k
                s_ref[c * SGU_CHUNK:(c + 1) * SGU_CHUNK, jj * LANES:(jj + 1) * LANES] = (
                    jnp.where(lo, r[:SGU_CHUNK, k * LANES:(k + 1) * LANES],
                              r[SGU_CHUNK:, k * LANES:(k + 1) * LANES]) + bs_ref[jj])
    u = jax.nn.gelu(_dot(hb, w_ref[:, :width].astype(BF16)))
    z = _dot(hb, w_ref[:, 2 * width:].astype(BF16))
    o = u * s_ref[...] * jax.nn.silu(z)
    r = _dot(o.astype(BF16), wo_ref[...].astype(BF16))
    out_ref[0] = x + _rms(r, pg_ref[...])


def _sgu_layer(x, pre_g, post_g, w_in, ln_g, ln_b, w_s, b_s, w_out):
    bsz, L, d = x.shape
    width = w_out.shape[0]
    T = SGU_CHUNK
    gd = width // SGU_GROUPS
    tril = jnp.tril(jnp.ones((T, T), dtype=bool))
    ws = jnp.where(tril[None], w_s, 0.0).reshape(SGU_GROUPS // 2, 2 * T, T).astype(BF16)
    bs = jnp.repeat(b_s.astype(F32).T, gd, axis=1)
    bs = bs.reshape(T, width // LANES, LANES).transpose(1, 0, 2)
    tm = ROW_TILE
    return pl.pallas_call(
        _sgu_kernel,
        out_shape=jax.ShapeDtypeStruct(x.shape, x.dtype),
        grid=(bsz, L // tm),
        in_specs=[pl.BlockSpec((1, tm, d), lambda b, i: (b, i, 0)),
                  _full((1, d)), _full(w_in.shape), _full((1, width)), _full((1, width)),
                  _full(ws.shape), _full(bs.shape), _full(w_out.shape), _full((1, d))],
        out_specs=pl.BlockSpec((1, tm, d), lambda b, i: (b, i, 0)),
        scratch_shapes=[pltpu.VMEM((tm, width), F32)],
        compiler_params=_cparams(("parallel", "parallel")),
        name="sgu",
    )(x, pre_g.reshape(1, d), w_in, ln_g.reshape(1, width), ln_b.reshape(1, width),
      ws, bs, w_out, post_g.reshape(1, d))


def kernel(x, pre_norm, post_norm, rel_bias, a_w_in, a_lam_re, a_lam_im, a_log_dt, a_b_re, a_b_im, a_c_re, a_c_im, a_d, a_w_glu, a_b_glu, a_w_out, b_w_in, b_sinks, b_w_out, c_w_in, c_q_norm, c_kv_norm, c_w_uq, c_w_ukv, c_w_out, d_w_in, d_ln_g, d_ln_b, d_w_s, d_b_s, d_w_out):
    depth = pre_norm.shape[0]
    for i in range(depth):
        kind, j = i % 4, i // 4
        if kind == 0:
            x = _s5_layer(x, pre_norm[i], post_norm[i], a_w_in[j], a_lam_re[j], a_lam_im[j], a_log_dt[j],
                          a_b_re[j], a_b_im[j], a_c_re[j], a_c_im[j], a_d[j], a_w_glu[j], a_b_glu[j],
                          a_w_out[j])
        elif kind == 1:
            x = _swa_layer(x, pre_norm[i], post_norm[i], b_w_in[j], b_sinks[j], b_w_out[j], rel_bias)
        elif kind == 2:
            x = _mla_layer(x, pre_norm[i], post_norm[i], c_w_in[j], c_q_norm[j], c_kv_norm[j], c_w_uq[j],
                           c_w_ukv[j], c_w_out[j])
        else:
            x = _sgu_layer(x, pre_norm[i], post_norm[i], d_w_in[j], d_ln_g[j], d_ln_b[j], d_w_s[j],
                           d_b_s[j], d_w_out[j])
    return x
```

```python
import functools
import math

import jax
import jax.numpy as jnp
from jax import lax
from jax.experimental import pallas as pl
from jax.experimental.pallas import tpu as pltpu

F32 = jnp.float32
BF16 = jnp.bfloat16

EPS = 1e-6
NEG_INF = -1e30
LANES = 128
HALF = LANES // 2

SSM_GROUP = 16
SSM_STATE = 64
S5_CH_BLOCK = LANES
S5_GROUPS_PER_BLOCK = S5_CH_BLOCK // SSM_GROUP
S5_STATE_BLOCK = S5_GROUPS_PER_BLOCK * SSM_STATE
S5_T = 64

HEAD_DIM = 64
SWA_HEADS = 16
SWA_KV_HEADS = 2
SWA_GROUP = SWA_HEADS // SWA_KV_HEADS
WINDOW = 128
SWA_WINDOWS_PER_STEP = 8
SWA_UNIT_HEADS = 8
SWA_LOOKAHEAD = 2
SWA_PV_DELAY = 1
REL_BUCKETS = 32
REL_MAX_DIST = 128

MLA_HEADS = 16
MLA_NOPE = 64
MLA_ROPE = 32
MLA_V = 64
MLA_KV_RANK = 256
MLA_Q_RANK = 768
ROPE_BASE = 10000.0
MLA_TQ = 256
MLA_TK = 256
MLA_LOOKAHEAD = 16
MLA_PV_DELAY = 6

SGU_CHUNK = 128
SGU_GROUPS = 16
SGU_STACK = 4

ROW_TILE = 1024
VMEM_LIMIT = 56 * 1024 * 1024


def _cparams(sem):
    return pltpu.CompilerParams(dimension_semantics=sem, vmem_limit_bytes=VMEM_LIMIT)


def _rms(x, g):
    return x * lax.rsqrt(jnp.mean(x * x, axis=-1, keepdims=True) + EPS) * g


def _dot(a, b):
    return jnp.dot(a, b, preferred_element_type=F32)


def _dot_nt(a, b):
    return lax.dot_general(a, b, (((1,), (1,)), ((), ())), preferred_element_type=F32)


def _full(shape):
    n = len(shape)
    return pl.BlockSpec(shape, lambda *_: (0,) * n, pipeline_mode=pl.Buffered(1))


def _s5_kernel(x_ref, g_ref, w_ref, bb_ref, cc_ref, ar_ref, ai_ref, d_ref,
               wg_ref, bg_ref, wo_ref, pg_ref, out_ref, u_ref, z_ref, y_ref, s_ref, carry_ref, *, tt):
    bsz = x_ref.shape[0]
    width = wg_ref.shape[0]
    rows = bsz * tt
    nblk = bb_ref.shape[0]
    sb = S5_STATE_BLOCK

    @pl.when(pl.program_id(0) == 0)
    def _():
        carry_ref[...] = jnp.zeros_like(carry_ref)

    x = x_ref[...].reshape(rows, x_ref.shape[2])
    h = _rms(x, g_ref[...])
    hb = jnp.swapaxes(h.reshape(bsz, tt, h.shape[1]), 0, 1).reshape(rows, h.shape[1]).astype(BF16)
    u_ref[...] = _dot(hb, w_ref[:, :width].astype(BF16))
    z_ref[...] = _dot(hb, w_ref[:, width:].astype(BF16))

    nbuf = s_ref.shape[0]

    def project_in(i):
        s_ref[i % nbuf] = _dot(u_ref[:, i * LANES:(i + 1) * LANES].astype(BF16), bb_ref[i])

    def project_out(i):
        ub = u_ref[:, i * LANES:(i + 1) * LANES]
        y = _dot(s_ref[i % nbuf].astype(BF16), cc_ref[i]) + d_ref[:, i * LANES:(i + 1) * LANES] * ub
        y_ref[:, i * LANES:(i + 1) * LANES] = jax.nn.gelu(y)

    project_in(0)
    for i in range(nblk):
        if i + 1 < nblk:
            project_in(i + 1)
        buf = s_ref.at[i % nbuf]
        ar = ar_ref[i]
        ai = ai_ref[i]
        sr = carry_ref[i, :, 0:sb]
        si = carry_ref[i, :, sb:2 * sb]
        for t in range(tt):
            r0 = t * bsz
            nr = ar * sr - ai * si + buf[r0:r0 + bsz, 0:sb]
            ni = ar * si + ai * sr + buf[r0:r0 + bsz, sb:2 * sb]
            buf[r0:r0 + bsz, 0:sb] = nr
            buf[r0:r0 + bsz, sb:2 * sb] = ni
            sr, si = nr, ni
        carry_ref[i, :, 0:sb] = sr
        carry_ref[i, :, sb:2 * sb] = si
        project_out(i)

    y = y_ref[...]
    gate = jax.nn.sigmoid(_dot(y.astype(BF16), wg_ref[...].astype(BF16)) + bg_ref[...])
    o = y * gate * jax.nn.silu(z_ref[...])
    ob = jnp.swapaxes(o.reshape(tt, bsz, o.shape[1]), 0, 1).reshape(rows, o.shape[1]).astype(BF16)
    r = _dot(ob, wo_ref[...].astype(BF16))
    out_ref[...] = (x + _rms(r, pg_ref[...])).reshape(out_ref.shape)


def _s5_discretize(lam_re, lam_im, log_dt, b_re, b_im):
    dt = jnp.exp(log_dt)[:, None]
    mag = jnp.exp(lam_re * dt)
    ab_re = mag * jnp.cos(lam_im * dt)
    ab_im = mag * jnp.sin(lam_im * dt)
    den = lam_re * lam_re + lam_im * lam_im
    nr = ab_re - 1.0
    f_re = (nr * lam_re + ab_im * lam_im) / den
    f_im = (ab_im * lam_re - nr * lam_im) / den
    bb_re = f_re[..., None] * b_re - f_im[..., None] * b_im
    bb_im = f_re[..., None] * b_im + f_im[..., None] * b_re
    return ab_re, ab_im, bb_re, bb_im


def _s5_layer(x, pre_g, post_g, w_in, lam_re, lam_im, log_dt, b_re, b_im, c_re, c_im, d_skip,
              w_glu, b_glu, w_out):
    bsz, L, d = x.shape
    width = w_in.shape[1] // 2
    nblk = width // S5_CH_BLOCK
    gpb = S5_GROUPS_PER_BLOCK
    tt = S5_T
    rows = bsz * tt

    ab_re, ab_im, bb_re, bb_im = _s5_discretize(lam_re, lam_im, log_dt, b_re, b_im)
    eye = jnp.eye(gpb, dtype=F32)

    def pack_b(bb):
        t = bb.reshape(nblk, gpb, SSM_STATE, SSM_GROUP)
        return jnp.einsum('igph,gk->ikhgp', t, eye).reshape(nblk, S5_CH_BLOCK, S5_STATE_BLOCK)

    def pack_c(cc):
        t = cc.reshape(nblk, gpb, SSM_GROUP, SSM_STATE)
        return jnp.einsum('ighp,gk->igpkh', t, eye).reshape(nblk, S5_STATE_BLOCK, S5_CH_BLOCK)

    bb = jnp.concatenate([pack_b(bb_re), pack_b(bb_im)], axis=2).astype(BF16)
    cc = jnp.concatenate([pack_c(c_re), -pack_c(c_im)], axis=1).astype(BF16)
    ar = jnp.broadcast_to(ab_re.reshape(nblk, 1, S5_STATE_BLOCK), (nblk, bsz, S5_STATE_BLOCK))
    ai = jnp.broadcast_to(ab_im.reshape(nblk, 1, S5_STATE_BLOCK), (nblk, bsz, S5_STATE_BLOCK))

    xspec = pl.BlockSpec((bsz, tt, d), lambda i: (0, i, 0))
    return pl.pallas_call(
        functools.partial(_s5_kernel, tt=tt),
        out_shape=jax.ShapeDtypeStruct(x.shape, x.dtype),
        grid=(L // tt,),
        in_specs=[xspec, _full((1, d)), _full(w_in.shape),
                  _full(bb.shape), _full(cc.shape), _full(ar.shape), _full(ai.shape), _full((1, width)),
                  _full(w_glu.shape), _full((1, width)), _full(w_out.shape), _full((1, d))],
        out_specs=xspec,
        scratch_shapes=[pltpu.VMEM((rows, width), F32),
                        pltpu.VMEM((rows, width), F32),
                        pltpu.VMEM((rows, width), F32),
                        pltpu.VMEM((2, rows, 2 * S5_STATE_BLOCK), F32),
                        pltpu.VMEM((nblk, bsz, 2 * S5_STATE_BLOCK), F32)],
        compiler_params=_cparams(("arbitrary",)),
        name="s5_layer",
    )(x, pre_g.reshape(1, d), w_in, bb, cc, ar, ai, d_skip.reshape(1, width),
      w_glu, b_glu.reshape(1, width), w_out, post_g.reshape(1, d))


def _swa_bias(rel_bias):
    W = WINDOW
    n = 4 * W
    dist = jnp.arange(n) - W
    valid = jnp.logical_and(dist >= 0, dist < W)
    dpos = jnp.maximum(dist, 0)
    max_exact = REL_BUCKETS // 2
    dist_f = jnp.maximum(dpos, 1).astype(F32)
    large = max_exact + (jnp.log(dist_f / max_exact) / math.log(REL_MAX_DIST / max_exact)
                         * (REL_BUCKETS - max_exact)).astype(jnp.int32)
    large = jnp.minimum(large, REL_BUCKETS - 1)
    bucket = jnp.where(dpos < max_exact, dpos, large)
    return jnp.where(valid[:, None], rel_bias[bucket].astype(F32), NEG_INF).T


def _swa_kernel(x_ref, pg_ref, wqt_ref, wk_ref, wvt_ref, wz_ref, bvec_ref, sink_ref, wo_ref, g_ref,
                out_ref, ot_ref, bias_ref, kprev_ref, vtprev_ref, *, scale):
    W = WINDOW
    nwin = x_ref.shape[1] // W
    step = pl.program_id(1)

    @pl.when(step == 0)
    def _():
        kprev_ref[...] = jnp.zeros_like(kprev_ref)
        vtprev_ref[...] = jnp.zeros_like(vtprev_ref)

    x = x_ref[0]
    hb = _rms(x, pg_ref[...]).astype(BF16)
    qt = (_dot_nt(wqt_ref[...], hb) * scale).astype(BF16)
    k = _dot(hb, wk_ref[...]).astype(BF16)
    vt = _dot_nt(wvt_ref[...], hb).astype(BF16)
    z = _dot(hb, wz_ref[...])

    @pl.when(jnp.logical_and(pl.program_id(0) == 0, step == 0))
    def _():
        no_prev = lax.broadcasted_iota(jnp.int32, (2 * W, W), 0) < W
        for hd in range(SWA_HEADS):
            base = jnp.broadcast_to(bvec_ref[hd:hd + 1, :], (2 * W, bvec_ref.shape[1]))
            toep = pltpu.roll(base, 0, 1, stride=1, stride_axis=0)[:, 2 * W:3 * W]
            h, g = divmod(hd, SWA_GROUP)
            bias_ref[0, h, :, g * W:(g + 1) * W] = toep
            bias_ref[1, h, :, g * W:(g + 1) * W] = jnp.where(no_prev, NEG_INF, toep)
    kall = jnp.concatenate([kprev_ref[...], k], axis=0)
    vtall = jnp.concatenate([vtprev_ref[...], vt], axis=1)
    kprev_ref[...] = k[(nwin - 1) * W:]
    vtprev_ref[...] = vt[:, (nwin - 1) * W:]
    nsub = SWA_UNIT_HEADS
    zq = jnp.zeros((HEAD_DIM, nsub * W), BF16)
    ones = jnp.ones((16, 2 * W), BF16)
    units = [(w, h, c) for w in range(nwin) for h in range(SWA_KV_HEADS) for c in range(SWA_GROUP // nsub)]

    def scores(w, h, c):
        hd0 = h * SWA_GROUP + c * nsub
        qh = jnp.concatenate([qt[(hd0 + g) * HEAD_DIM:(hd0 + g + 1) * HEAD_DIM, w * W:(w + 1) * W]
                              for g in range(nsub)], axis=1)
        qz = jnp.concatenate([qh, zq] if h == 0 else [zq, qh], axis=0)
        return _dot(kall[w * W:(w + 2) * W], qz)

    pending = [scores(*u) for u in units[:SWA_LOOKAHEAD]]
    late = []

    def flush():
        (w, h, c), p, tail = late.pop(0)
        vones = jnp.concatenate([vtall[h * HEAD_DIM:(h + 1) * HEAD_DIM, w * W:(w + 2) * W], ones], axis=0)
        o = _dot(vones, p)
        oh = o[:HEAD_DIM] * (1.0 / (o[HEAD_DIM:HEAD_DIM + 1] + tail))
        for g in range(nsub):
            hd = h * SWA_GROUP + c * nsub + g
            ot_ref[hd * HEAD_DIM:(hd + 1) * HEAD_DIM, w * W:(w + 1) * W] = oh[:, g * W:(g + 1) * W]

    for idx, (w, h, c) in enumerate(units):
        raw = pending.pop(0)
        if idx + SWA_LOOKAHEAD < len(units):
            pending.append(scores(*units[idx + SWA_LOOKAHEAD]))
        cols = slice(c * nsub * W, (c + 1) * nsub * W)
        variant = (step == 0).astype(jnp.int32) if w == 0 else 0
        s = raw + bias_ref[variant, h, :, cols]
        sink = sink_ref[h, :, cols]
        m = jnp.maximum(jnp.max(s, axis=0, keepdims=True), sink)
        if len(late) == SWA_PV_DELAY:
            flush()
        late.append(((w, h, c), jnp.exp2(s - m).astype(BF16), jnp.exp2(sink - m)))
    while late:
        flush()
    gated = ot_ref[...].T * jax.nn.silu(z)
    r = _dot(gated.astype(BF16), wo_ref[...].astype(BF16))
    out_ref[0] = x + _rms(r, g_ref[...])


def _swa_layer(x, pre_g, post_g, w_in, sinks, w_out, rel_bias):
    bsz, L, d = x.shape
    width = SWA_HEADS * HEAD_DIM
    kvw = SWA_KV_HEADS * HEAD_DIM
    W = WINDOW
    log2e = math.log2(math.e)
    wb = w_in.astype(BF16)
    wqt = wb[:, :width].T
    wk = wb[:, width:width + kvw]
    wvt = wb[:, width + kvw:width + 2 * kvw].T
    wz = wb[:, width + 2 * kvw:]
    bvec = _swa_bias(rel_bias.astype(F32) * log2e)
    sink = jnp.repeat(sinks.astype(F32) * log2e, W).reshape(SWA_KV_HEADS, 1, SWA_GROUP * W)

    tq = SWA_WINDOWS_PER_STEP * W
    xspec = pl.BlockSpec((1, tq, d), lambda b, n: (b, n, 0))
    return pl.pallas_call(
        functools.partial(_swa_kernel, scale=HEAD_DIM ** -0.5 * log2e),
        out_shape=jax.ShapeDtypeStruct(x.shape, x.dtype),
        grid=(bsz, L // tq),
        in_specs=[xspec, _full((1, d)), _full(wqt.shape), _full(wk.shape), _full(wvt.shape), _full(wz.shape),
                  _full(bvec.shape), _full(sink.shape), _full(w_out.shape), _full((1, d))],
        out_specs=xspec,
        scratch_shapes=[pltpu.VMEM((width, tq), F32),
                        pltpu.VMEM((2, SWA_KV_HEADS, 2 * W, SWA_GROUP * W), F32),
                        pltpu.VMEM((W, kvw), BF16),
                        pltpu.VMEM((kvw, W), BF16)],
        compiler_params=_cparams(("arbitrary", "arbitrary")),
        name="swa_layer",
    )(x, pre_g.reshape(1, d), wqt, wk, wvt, wz, bvec, sink, w_out, post_g.reshape(1, d))


def _mla_pre_kernel(x_ref, g_ref, w_ref, qn_ref, kvn_ref, wq_ref, wkv_ref, wvt_ref, cq_ref, sq_ref, ck_ref, sk_ref,
                    oqn_ref, oqr_ref, okn_ref, okr_ref, ov_ref, oz_ref, *, scale):
    nope = MLA_HEADS * MLA_NOPE
    rope = MLA_HEADS * MLA_ROPE
    vw = MLA_HEADS * MLA_V
    hb = _rms(x_ref[0], g_ref[...]).astype(BF16)
    o1 = MLA_Q_RANK
    o2 = o1 + MLA_KV_RANK
    o3 = o2 + vw
    cq = _dot(hb, w_ref[:, :o1])
    ckv = _dot(hb, w_ref[:, o1:o2])
    oz_ref[0] = _dot(hb, w_ref[:, o2:o3])
    kr = _dot(hb, w_ref[:, o3:o3 + LANES])
    krs = _dot(hb, w_ref[:, o3 + LANES:o3 + 2 * LANES])
    okr_ref[0] = (kr * ck_ref[...] + krs * sk_ref[...]).astype(BF16)
    cqb = _rms(cq, qn_ref[...]).astype(BF16)
    oqn_ref[0] = (_dot_nt(wq_ref[:nope], cqb) * scale).astype(BF16)
    qr = _dot_nt(wq_ref[nope:nope + rope], cqb)
    hr = MLA_ROPE // 2
    qrs = jnp.concatenate([qr[h * MLA_ROPE + o:h * MLA_ROPE + o + hr]
                           for h in range(MLA_HEADS) for o in (hr, 0)], axis=0)
    oqr_ref[0] = ((qr * cq_ref[...] + qrs * sq_ref[...]) * scale).astype(BF16)
    ckb = _rms(ckv, kvn_ref[...]).astype(BF16)
    okn_ref[0] = _dot(ckb, wkv_ref[:, :nope]).astype(BF16)
    vt = _dot_nt(wvt_ref[...], ckb).astype(BF16)
    tk = ov_ref.shape[3]
    for c in range(ov_ref.shape[1]):
        ov_ref[0, c] = vt[:, c * tk:(c + 1) * tk]


def _mla_attn_kernel(qn_ref, qr_ref, kn_ref, kr_ref, v_ref, z_ref, x_ref, wo_ref, g_ref, out_ref,
                     qs_ref, acc_ref, m_ref, o_ref):
    tq = qn_ref.shape[2]
    tk = v_ref.shape[3]
    npairs = MLA_HEADS // 2
    i = pl.program_id(1)
    tri = (lax.broadcasted_iota(jnp.int32, (LANES, LANES), 0)
           <= lax.broadcasted_iota(jnp.int32, (LANES, LANES), 1))

    zn = jnp.zeros((MLA_NOPE, tq), BF16)
    zr = jnp.zeros((LANES - MLA_ROPE, tq), BF16)
    for p in range(npairs):
        qn = qn_ref[0, p * LANES:(p + 1) * LANES, :]
        r0 = 2 * p * MLA_ROPE
        c0 = jnp.concatenate([qn[:MLA_NOPE], zn, qr_ref[0, r0:r0 + MLA_ROPE, :], zr], axis=0)
        c1 = jnp.concatenate([zn, qn[MLA_NOPE:], qr_ref[0, r0 + MLA_ROPE:r0 + 2 * MLA_ROPE, :], zr], axis=0)
        qs_ref[p] = jnp.concatenate([c0, c1], axis=1)

    m_ref[...] = jnp.full(m_ref.shape, NEG_INF, F32)
    acc_ref[...] = jnp.zeros(acc_ref.shape, F32)
    ones = jnp.ones((acc_ref.shape[1] - LANES, tk), BF16)

    def kv_steps(blocks):
        units = [(j, masked, p) for j, masked in blocks for p in range(npairs)]

        def scores(j, masked, p):
            ks = pl.multiple_of(j * tk, tk)
            kc = jnp.concatenate([kn_ref[0, pl.ds(ks, tk), p * LANES:(p + 1) * LANES],
                                  kr_ref[0, pl.ds(ks, tk), :]], axis=1)
            if not masked:
                chunks = [_dot(kc, qs_ref[p, :, c * 2 * LANES:(c + 1) * 2 * LANES]) for c in range(tq // LANES)]
                return [chunks[c // 2][:, (c % 2) * LANES:(c % 2 + 1) * LANES] for c in range(2 * tq // LANES)]
            lo = [_dot(kc[:LANES], qs_ref[p, :, c * 2 * LANES:(c + 1) * 2 * LANES]) for c in range(tq // LANES)]
            late_q = jnp.concatenate([qs_ref[p, :, LANES:2 * LANES], qs_ref[p, :, 3 * LANES:4 * LANES]], axis=1)
            hi = _dot(kc[LANES:], late_q)
            strips = []
            for hd in range(2):
                strips.append(jnp.where(tri, lo[hd][:, :LANES], NEG_INF))
                strips.append(jnp.concatenate([lo[hd][:, LANES:],
                                               jnp.where(tri, hi[:, hd * LANES:(hd + 1) * LANES], NEG_INF)], axis=0))
            return strips

        pending = [scores(*u) for u in units[:MLA_LOOKAHEAD]]
        late = []

        def flush():
            jj, pp, alpha, pr = late.pop(0)
            vones = jnp.concatenate([v_ref[0, jj, pp * LANES:(pp + 1) * LANES, :], ones], axis=0)
            acc_ref[pp] = alpha * acc_ref[pp] + _dot(vones, pr)

        for idx, (j, masked, p) in enumerate(units):
            s = pending.pop(0)
            if idx + MLA_LOOKAHEAD < len(units):
                pending.append(scores(*units[idx + MLA_LOOKAHEAD]))
            probs, alphas = [], []
            for c, sc in enumerate(s):
                m_prev = m_ref[p, :, c * LANES:(c + 1) * LANES]
                m_new = jnp.maximum(m_prev, jnp.max(sc, axis=0, keepdims=True))
                alphas.append(jnp.exp2(m_prev - m_new))
                pr = jnp.exp2(sc - m_new).astype(BF16)
                if pr.shape[0] < tk:
                    pr = jnp.concatenate([pr, jnp.zeros((tk - pr.shape[0], LANES), BF16)], axis=0)
                probs.append(pr)
                m_ref[p, :, c * LANES:(c + 1) * LANES] = m_new
            if len(late) == MLA_PV_DELAY:
                flush()
            late.append((j, p, jnp.concatenate(alphas, axis=1), jnp.concatenate(probs, axis=1)))
        while late:
            flush()

    def body(jj, c):
        kv_steps([(2 * jj, False), (2 * jj + 1, False)])
        return c

    lax.fori_loop(0, i // 2, body, 0)

    @pl.when(i % 2 == 1)
    def _():
        kv_steps([(i - 1, False), (i, True)])

    @pl.when(i % 2 == 0)
    def _():
        kv_steps([(i, True)])
    for p in range(npairs):
        a = acc_ref[p]
        a = a[:LANES] * (1.0 / a[LANES:LANES + 1])
        ot = jnp.concatenate([a[:MLA_V, :tq], a[MLA_V:, tq:]], axis=0)
        o_ref[:, p * LANES:(p + 1) * LANES] = ot.T
    gated = o_ref[...] * jax.nn.silu(z_ref[0])
    r = _dot(gated.astype(BF16), wo_ref[...].astype(BF16))
    out_ref[0] = x_ref[0] + _rms(r, g_ref[...])


def _mla_layer(x, pre_g, post_g, w_in, q_norm, kv_norm, w_uq, w_ukv, w_out):
    bsz, L, d = x.shape
    H = MLA_HEADS
    dq = MLA_NOPE + MLA_ROPE
    nope = H * MLA_NOPE
    rope = H * MLA_ROPE
    vw = H * MLA_V
    half = MLA_ROPE // 2
    o_kr = MLA_Q_RANK + MLA_KV_RANK
    o_z = o_kr + MLA_ROPE
    wb = w_in.astype(BF16)
    w_kr = wb[:, o_kr:o_z]
    w_krs = jnp.concatenate([w_kr[:, half:], w_kr[:, :half]], axis=1)
    reps = LANES // MLA_ROPE
    w1 = jnp.concatenate([wb[:, :o_kr], wb[:, o_z:]] + [w_kr] * reps + [w_krs] * reps, axis=1)
    wq3 = w_uq.astype(BF16).reshape(MLA_Q_RANK, H, dq)
    wqt = jnp.concatenate([wq3[:, :, :MLA_NOPE].reshape(MLA_Q_RANK, nope),
                           wq3[:, :, MLA_NOPE:].reshape(MLA_Q_RANK, rope)], axis=1).T
    wkv3 = w_ukv.astype(BF16).reshape(MLA_KV_RANK, H, MLA_NOPE + MLA_V)
    wkn = wkv3[:, :, :MLA_NOPE].reshape(MLA_KV_RANK, nope)
    wvt = wkv3[:, :, MLA_NOPE:].reshape(MLA_KV_RANK, vw).T
    inv = ROPE_BASE ** (-jnp.arange(0, MLA_ROPE, 2, dtype=F32) / MLA_ROPE)
    ang = jnp.arange(L, dtype=F32)[:, None] * inv[None, :]
    cos, sin = jnp.cos(ang), jnp.sin(ang)
    cos32 = jnp.concatenate([cos, cos], axis=1)
    sin32 = jnp.concatenate([-sin, sin], axis=1)
    cos_k, sin_k = jnp.tile(cos32, (1, LANES // MLA_ROPE)), jnp.tile(sin32, (1, LANES // MLA_ROPE))
    cos_q, sin_q = jnp.tile(cos32, (1, H)).T, jnp.tile(sin32, (1, H)).T

    tm = ROW_TILE
    tk = MLA_TK
    tok = lambda w_: pl.BlockSpec((1, tm, w_), lambda b, i: (b, i, 0))
    tokt = lambda w_: pl.BlockSpec((1, w_, tm), lambda b, i: (b, 0, i))
    scale = dq ** -0.5 * math.log2(math.e)
    qn, qr, kn, kr, v, z = pl.pallas_call(
        functools.partial(_mla_pre_kernel, scale=scale),
        out_shape=[jax.ShapeDtypeStruct((bsz, nope, L), BF16),
                   jax.ShapeDtypeStruct((bsz, rope, L), BF16),
                   jax.ShapeDtypeStruct((bsz, L, nope), BF16),
                   jax.ShapeDtypeStruct((bsz, L, LANES), BF16),
                   jax.ShapeDtypeStruct((bsz, L // tk, vw, tk), BF16),
                   jax.ShapeDtypeStruct((bsz, L, vw), F32)],
        grid=(bsz, L // tm),
        in_specs=[tok(d), _full((1, d)), _full(w1.shape), _full((1, MLA_Q_RANK)), _full((1, MLA_KV_RANK)),
                  _full(wqt.shape), _full(wkn.shape), _full(wvt.shape),
                  pl.BlockSpec((rope, tm), lambda b, i: (0, i)), pl.BlockSpec((rope, tm), lambda b, i: (0, i)),
                  pl.BlockSpec((tm, LANES), lambda b, i: (i, 0)), pl.BlockSpec((tm, LANES), lambda b, i: (i, 0))],
        out_specs=[tokt(nope), tokt(rope), tok(nope), tok(LANES),
                   pl.BlockSpec((1, tm // tk, vw, tk), lambda b, i: (b, i, 0, 0)), tok(vw)],
        compiler_params=_cparams(("parallel", "parallel")),
        name="mla_pre",
    )(x, pre_g.reshape(1, d), w1, q_norm.reshape(1, -1), kv_norm.reshape(1, -1), wqt, wkn, wvt,
      cos_q, sin_q, cos_k, sin_k)

    tq = MLA_TQ
    assert MLA_TQ == MLA_TK == 2 * LANES, "the diagonal-block handling works on 128-query strips of a 256 block"
    npairs = H // 2
    qspec = lambda w_: pl.BlockSpec((1, w_, tq), lambda b, i: (b, 0, i))
    kspec = lambda w_: pl.BlockSpec((1, L, w_), lambda b, i: (b, 0, 0))
    rowspec = lambda w_: pl.BlockSpec((1, tq, w_), lambda b, i: (b, i, 0))
    return pl.pallas_call(
        _mla_attn_kernel,
        out_shape=jax.ShapeDtypeStruct(x.shape, x.dtype),
        grid=(bsz, L // tq),
        in_specs=[qspec(nope), qspec(rope), kspec(nope), kspec(LANES),
                  pl.BlockSpec((1, L // tk, vw, tk), lambda b, i: (b, 0, 0, 0)),
                  rowspec(vw), rowspec(d), _full(w_out.shape), _full((1, d))],
        out_specs=rowspec(d),
        scratch_shapes=[pltpu.VMEM((npairs, 2 * LANES, 2 * tq), BF16),
                        pltpu.VMEM((npairs, LANES + 16, 2 * tq), F32),
                        pltpu.VMEM((npairs, 1, 2 * tq), F32),
                        pltpu.VMEM((tq, vw), F32)],
        compiler_params=_cparams(("parallel", "arbitrary")),
        name="mla_attn",
    )(qn, qr, kn, kr, v, z, x, w_out, post_g.reshape(1, d))


def _sgu_kernel(x_ref, g_ref, w_ref, lng_ref, lnb_ref, ws_ref, bs_ref, wo_ref, pg_ref, out_ref, s_ref):
    width = wo_ref.shape[0]
    tm = x_ref.shape[1]
    lane = lax.broadcasted_iota(jnp.int32, (1, LANES), 1)
    lo = lane < HALF
    x = x_ref[0]
    hb = _rms(x, g_ref[...]).astype(BF16)
    v = jax.nn.gelu(_dot(hb, w_ref[:, width:2 * width].astype(BF16)))
    mu = jnp.mean(v, axis=-1, keepdims=True)
    vc = v - mu
    var = jnp.mean(vc * vc, axis=-1, keepdims=True)
    vb = (vc * lax.rsqrt(var + EPS) * lng_ref[...] + lnb_ref[...]).astype(BF16)
    group = SGU_STACK
    for c0 in range(0, tm // SGU_CHUNK, group):
        for jj in range(width // LANES):
            blk = jnp.concatenate([vb[c * SGU_CHUNK:(c + 1) * SGU_CHUNK, jj * LANES:(jj + 1) * LANES]
                                   for c in range(c0, c0 + group)], axis=1)
            r = _dot(ws_ref[jj], blk)
            for k in range(group):
                c = c0 + k
                s_ref[c * SGU_CHUNK:(c + 1) * SGU_CHUNK, jj * LANES:(jj + 1) * LANES] = (
                    jnp.where(lo, r[:SGU_CHUNK, k * LANES:(k + 1) * LANES],
                              r[SGU_CHUNK:, k * LANES:(k + 1) * LANES]) + bs_ref[jj])
    u = jax.nn.gelu(_dot(hb, w_ref[:, :width].astype(BF16)))
    z = _dot(hb, w_ref[:, 2 * width:].astype(BF16))
    o = u * s_ref[...] * jax.nn.silu(z)
    r = _dot(o.astype(BF16), wo_ref[...].astype(BF16))
    out_ref[0] = x + _rms(r, pg_ref[...])


def _sgu_layer(x, pre_g, post_g, w_in, ln_g, ln_b, w_s, b_s, w_out):
    bsz, L, d = x.shape
    width = w_out.shape[0]
    T = SGU_CHUNK
    gd = width // SGU_GROUPS
    tril = jnp.tril(jnp.ones((T, T), dtype=bool))
    ws = jnp.where(tril[None], w_s, 0.0).reshape(SGU_GROUPS // 2, 2 * T, T).astype(BF16)
    bs = jnp.repeat(b_s.astype(F32).T, gd, axis=1)
    bs = bs.reshape(T, width // LANES, LANES).transpose(1, 0, 2)
    tm = ROW_TILE
    return pl.pallas_call(
        _sgu_kernel,
        out_shape=jax.ShapeDtypeStruct(x.shape, x.dtype),
        grid=(bsz, L // tm),
        in_specs=[pl.BlockSpec((1, tm, d), lambda b, i: (b, i, 0)),
                  _full((1, d)), _full(w_in.shape), _full((1, width)), _full((1, width)),
                  _full(ws.shape), _full(bs.shape), _full(w_out.shape), _full((1, d))],
        out_specs=pl.BlockSpec((1, tm, d), lambda b, i: (b, i, 0)),
        scratch_shapes=[pltpu.VMEM((tm, width), F32)],
        compiler_params=_cparams(("parallel", "parallel")),
        name="sgu",
    )(x, pre_g.reshape(1, d), w_in, ln_g.reshape(1, width), ln_b.reshape(1, width),
      ws, bs, w_out, post_g.reshape(1, d))


def kernel(x, pre_norm, post_norm, rel_bias, a_w_in, a_lam_re, a_lam_im, a_log_dt, a_b_re, a_b_im, a_c_re, a_c_im, a_d, a_w_glu, a_b_glu, a_w_out, b_w_in, b_sinks, b_w_out, c_w_in, c_q_norm, c_kv_norm, c_w_uq, c_w_ukv, c_w_out, d_w_in, d_ln_g, d_ln_b, d_w_s, d_b_s, d_w_out):
    depth = pre_norm.shape[0]
    for i in range(depth):
        kind, j = i % 4, i // 4
        if kind == 0:
            x = _s5_layer(x, pre_norm[i], post_norm[i], a_w_in[j], a_lam_re[j], a_lam_im[j], a_log_dt[j],
                          a_b_re[j], a_b_im[j], a_c_re[j], a_c_im[j], a_d[j], a_w_glu[j], a_b_glu[j],
                          a_w_out[j])
        elif kind == 1:
            x = _swa_layer(x, pre_norm[i], post_norm[i], b_w_in[j], b_sinks[j], b_w_out[j], rel_bias)
        elif kind == 2:
            x = _mla_layer(x, pre_norm[i], post_norm[i], c_w_in[j], c_q_norm[j], c_kv_norm[j], c_w_uq[j],
                           c_w_ukv[j], c_w_out[j])
        else:
            x = _sgu_layer(x, pre_norm[i], post_norm[i], d_w_in[j], d_ln_g[j], d_ln_b[j], d_w_s[j],
                           d_b_s[j], d_w_out[j])
    return x
```

```python
import functools
import math

import jax
import jax.numpy as jnp
from jax import lax
from jax.experimental import pallas as pl
from jax.experimental.pallas import tpu as pltpu

F32 = jnp.float32
BF16 = jnp.bfloat16

EPS = 1e-6
NEG_INF = -1e30
LANES = 128
HALF = LANES // 2

SSM_GROUP = 16
SSM_STATE = 64
S5_CH_BLOCK = LANES
S5_GROUPS_PER_BLOCK = S5_CH_BLOCK // SSM_GROUP
S5_STATE_BLOCK = S5_GROUPS_PER_BLOCK * SSM_STATE
S5_T = 64

HEAD_DIM = 64
SWA_HEADS = 16
SWA_KV_HEADS = 2
SWA_GROUP = SWA_HEADS // SWA_KV_HEADS
WINDOW = 128
SWA_WINDOWS_PER_STEP = 8
SWA_UNIT_HEADS = 8
SWA_LOOKAHEAD = 4
SWA_PV_DELAY = 2
REL_BUCKETS = 32
REL_MAX_DIST = 128

MLA_HEADS = 16
MLA_NOPE = 64
MLA_ROPE = 32
MLA_V = 64
MLA_KV_RANK = 256
MLA_Q_RANK = 768
ROPE_BASE = 10000.0
MLA_TQ = 256
MLA_TK = 256
MLA_LOOKAHEAD = 12
MLA_PV_DELAY = 4

SGU_CHUNK = 128
SGU_GROUPS = 16
SGU_STACK = 4

ROW_TILE = 1024
VMEM_LIMIT = 56 * 1024 * 1024


def _cparams(sem):
    return pltpu.CompilerParams(dimension_semantics=sem, vmem_limit_bytes=VMEM_LIMIT)


def _rms(x, g):
    return x * lax.rsqrt(jnp.mean(x * x, axis=-1, keepdims=True) + EPS) * g


def _dot(a, b):
    return jnp.dot(a, b, preferred_element_type=F32)


def _dot_nt(a, b):
    return lax.dot_general(a, b, (((1,), (1,)), ((), ())), preferred_element_type=F32)


def _full(shape):
    n = len(shape)
    return pl.BlockSpec(shape, lambda *_: (0,) * n, pipeline_mode=pl.Buffered(1))


def _s5_kernel(x_ref, g_ref, w_ref, bb_ref, cc_ref, ar_ref, ai_ref, d_ref,
               wg_ref, bg_ref, wo_ref, pg_ref, out_ref, u_ref, z_ref, y_ref, s_ref, carry_ref, *, tt):
    bsz = x_ref.shape[0]
    width = wg_ref.shape[0]
    rows = bsz * tt
    nblk = bb_ref.shape[0]
    sb = S5_STATE_BLOCK

    @pl.when(pl.program_id(0) == 0)
    def _():
        carry_ref[...] = jnp.zeros_like(carry_ref)

    x = x_ref[...].reshape(rows, x_ref.shape[2])
    h = _rms(x, g_ref[...])
    hb = jnp.swapaxes(h.reshape(bsz, tt, h.shape[1]), 0, 1).reshape(rows, h.shape[1]).astype(BF16)
    u_ref[...] = _dot(hb, w_ref[:, :width].astype(BF16))
    z_ref[...] = _dot(hb, w_ref[:, width:].astype(BF16))

    nbuf = s_ref.shape[0]

    def project_in(i):
        s_ref[i % nbuf] = _dot(u_ref[:, i * LANES:(i + 1) * LANES].astype(BF16), bb_ref[i])

    def project_out(i):
        ub = u_ref[:, i * LANES:(i + 1) * LANES]
        y = _dot(s_ref[i % nbuf].astype(BF16), cc_ref[i]) + d_ref[:, i * LANES:(i + 1) * LANES] * ub
        y_ref[:, i * LANES:(i + 1) * LANES] = jax.nn.gelu(y)

    project_in(0)
    for i in range(nblk):
        if i + 1 < nblk:
            project_in(i + 1)
        buf = s_ref.at[i % nbuf]
        ar = ar_ref[i]
        ai = ai_ref[i]
        sr = carry_ref[i, :, 0:sb]
        si = carry_ref[i, :, sb:2 * sb]
        for t in range(tt):
            r0 = t * bsz
            nr = ar * sr - ai * si + buf[r0:r0 + bsz, 0:sb]
            ni = ar * si + ai * sr + buf[r0:r0 + bsz, sb:2 * sb]
            buf[r0:r0 + bsz, 0:sb] = nr
            buf[r0:r0 + bsz, sb:2 * sb] = ni
            sr, si = nr, ni
        carry_ref[i, :, 0:sb] = sr
        carry_ref[i, :, sb:2 * sb] = si
        project_out(i)

    y = y_ref[...]
    gate = jax.nn.sigmoid(_dot(y.astype(BF16), wg_ref[...].astype(BF16)) + bg_ref[...])
    o = y * gate * jax.nn.silu(z_ref[...])
    ob = jnp.swapaxes(o.reshape(tt, bsz, o.shape[1]), 0, 1).reshape(rows, o.shape[1]).astype(BF16)
    r = _dot(ob, wo_ref[...].astype(BF16))
    out_ref[...] = (x + _rms(r, pg_ref[...])).reshape(out_ref.shape)


def _s5_discretize(lam_re, lam_im, log_dt, b_re, b_im):
    dt = jnp.exp(log_dt)[:, None]
    mag = jnp.exp(lam_re * dt)
    ab_re = mag * jnp.cos(lam_im * dt)
    ab_im = mag * jnp.sin(lam_im * dt)
    den = lam_re * lam_re + lam_im * lam_im
    nr = ab_re - 1.0
    f_re = (nr * lam_re + ab_im * lam_im) / den
    f_im = (ab_im * lam_re - nr * lam_im) / den
    bb_re = f_re[..., None] * b_re - f_im[..., None] * b_im
    bb_im = f_re[..., None] * b_im + f_im[..., None] * b_re
    return ab_re, ab_im, bb_re, bb_im


def _s5_layer(x, pre_g, post_g, w_in, lam_re, lam_im, log_dt, b_re, b_im, c_re, c_im, d_skip,
              w_glu, b_glu, w_out):
    bsz, L, d = x.shape
    width = w_in.shape[1] // 2
    nblk = width // S5_CH_BLOCK
    gpb = S5_GROUPS_PER_BLOCK
    tt = S5_T
    rows = bsz * tt

    ab_re, ab_im, bb_re, bb_im = _s5_discretize(lam_re, lam_im, log_dt, b_re, b_im)
    eye = jnp.eye(gpb, dtype=F32)

    def pack_b(bb):
        t = bb.reshape(nblk, gpb, SSM_STATE, SSM_GROUP)
        return jnp.einsum('igph,gk->ikhgp', t, eye).reshape(nblk, S5_CH_BLOCK, S5_STATE_BLOCK)

    def pack_c(cc):
        t = cc.reshape(nblk, gpb, SSM_GROUP, SSM_STATE)
        return jnp.einsum('ighp,gk->igpkh', t, eye).reshape(nblk, S5_STATE_BLOCK, S5_CH_BLOCK)

    bb = jnp.concatenate([pack_b(bb_re), pack_b(bb_im)], axis=2).astype(BF16)
    cc = jnp.concatenate([pack_c(c_re), -pack_c(c_im)], axis=1).astype(BF16)
    ar = jnp.broadcast_to(ab_re.reshape(nblk, 1, S5_STATE_BLOCK), (nblk, bsz, S5_STATE_BLOCK))
    ai = jnp.broadcast_to(ab_im.reshape(nblk, 1, S5_STATE_BLOCK), (nblk, bsz, S5_STATE_BLOCK))

    xspec = pl.BlockSpec((bsz, tt, d), lambda i: (0, i, 0))
    return pl.pallas_call(
        functools.partial(_s5_kernel, tt=tt),
        out_shape=jax.ShapeDtypeStruct(x.shape, x.dtype),
        grid=(L // tt,),
        in_specs=[xspec, _full((1, d)), _full(w_in.shape),
                  _full(bb.shape), _full(cc.shape), _full(ar.shape), _full(ai.shape), _full((1, width)),
                  _full(w_glu.shape), _full((1, width)), _full(w_out.shape), _full((1, d))],
        out_specs=xspec,
        scratch_shapes=[pltpu.VMEM((rows, width), F32),
                        pltpu.VMEM((rows, width), F32),
                        pltpu.VMEM((rows, width), F32),
                        pltpu.VMEM((2, rows, 2 * S5_STATE_BLOCK), F32),
                        pltpu.VMEM((nblk, bsz, 2 * S5_STATE_BLOCK), F32)],
        compiler_params=_cparams(("arbitrary",)),
        name="s5_layer",
    )(x, pre_g.reshape(1, d), w_in, bb, cc, ar, ai, d_skip.reshape(1, width),
      w_glu, b_glu.reshape(1, width), w_out, post_g.reshape(1, d))


def _swa_bias(rel_bias):
    W = WINDOW
    n = 4 * W
    dist = jnp.arange(n) - W
    valid = jnp.logical_and(dist >= 0, dist < W)
    dpos = jnp.maximum(dist, 0)
    max_exact = REL_BUCKETS // 2
    dist_f = jnp.maximum(dpos, 1).astype(F32)
    large = max_exact + (jnp.log(dist_f / max_exact) / math.log(REL_MAX_DIST / max_exact)
                         * (REL_BUCKETS - max_exact)).astype(jnp.int32)
    large = jnp.minimum(large, REL_BUCKETS - 1)
    bucket = jnp.where(dpos < max_exact, dpos, large)
    return jnp.where(valid[:, None], rel_bias[bucket].astype(F32), NEG_INF).T


def _swa_kernel(x_ref, pg_ref, wqt_ref, wk_ref, wvt_ref, wz_ref, bvec_ref, sink_ref, wo_ref, g_ref,
                out_ref, ot_ref, bias_ref, kprev_ref, vtprev_ref, *, scale):
    W = WINDOW
    nwin = x_ref.shape[1] // W
    step = pl.program_id(1)

    @pl.when(step == 0)
    def _():
        kprev_ref[...] = jnp.zeros_like(kprev_ref)
        vtprev_ref[...] = jnp.zeros_like(vtprev_ref)

    x = x_ref[0]
    hb = _rms(x, pg_ref[...]).astype(BF16)
    qt = (_dot_nt(wqt_ref[...], hb) * scale).astype(BF16)
    k = _dot(hb, wk_ref[...]).astype(BF16)
    vt = _dot_nt(wvt_ref[...], hb).astype(BF16)
    z = _dot(hb, wz_ref[...])

    @pl.when(jnp.logical_and(pl.program_id(0) == 0, step == 0))
    def _():
        no_prev = lax.broadcasted_iota(jnp.int32, (2 * W, W), 0) < W
        for hd in range(SWA_HEADS):
            base = jnp.broadcast_to(bvec_ref[hd:hd + 1, :], (2 * W, bvec_ref.shape[1]))
            toep = pltpu.roll(base, 0, 1, stride=1, stride_axis=0)[:, 2 * W:3 * W]
            h, g = divmod(hd, SWA_GROUP)
            bias_ref[0, h, :, g * W:(g + 1) * W] = toep
            bias_ref[1, h, :, g * W:(g + 1) * W] = jnp.where(no_prev, NEG_INF, toep)
    kall = jnp.concatenate([kprev_ref[...], k], axis=0)
    vtall = jnp.concatenate([vtprev_ref[...], vt], axis=1)
    kprev_ref[...] = k[(nwin - 1) * W:]
    vtprev_ref[...] = vt[:, (nwin - 1) * W:]
    nsub = SWA_UNIT_HEADS
    zq = jnp.zeros((HEAD_DIM, nsub * W), BF16)
    ones = jnp.ones((16, 2 * W), BF16)
    units = [(w, h, c) for w in range(nwin) for h in range(SWA_KV_HEADS) for c in range(SWA_GROUP // nsub)]

    def scores(w, h, c):
        hd0 = h * SWA_GROUP + c * nsub
        qh = jnp.concatenate([qt[(hd0 + g) * HEAD_DIM:(hd0 + g + 1) * HEAD_DIM, w * W:(w + 1) * W]
                              for g in range(nsub)], axis=1)
        qz = jnp.concatenate([qh, zq] if h == 0 else [zq, qh], axis=0)
        return _dot(kall[w * W:(w + 2) * W], qz)

    pending = [scores(*u) for u in units[:SWA_LOOKAHEAD]]
    late = []

    def flush():
        (w, h, c), p, tail = late.pop(0)
        vones = jnp.concatenate([vtall[h * HEAD_DIM:(h + 1) * HEAD_DIM, w * W:(w + 2) * W], ones], axis=0)
        o = _dot(vones, p)
        oh = o[:HEAD_DIM] * (1.0 / (o[HEAD_DIM:HEAD_DIM + 1] + tail))
        for g in range(nsub):
            hd = h * SWA_GROUP + c * nsub + g
            ot_ref[hd * HEAD_DIM:(hd + 1) * HEAD_DIM, w * W:(w + 1) * W] = oh[:, g * W:(g + 1) * W]

    for idx, (w, h, c) in enumerate(units):
        raw = pending.pop(0)
        if idx + SWA_LOOKAHEAD < len(units):
            pending.append(scores(*units[idx + SWA_LOOKAHEAD]))
        cols = slice(c * nsub * W, (c + 1) * nsub * W)
        variant = (step == 0).astype(jnp.int32) if w == 0 else 0
        s = raw + bias_ref[variant, h, :, cols]
        sink = sink_ref[h, :, cols]
        m = jnp.maximum(jnp.max(s, axis=0, keepdims=True), sink)
        if len(late) == SWA_PV_DELAY:
            flush()
        late.append(((w, h, c), jnp.exp2(s - m).astype(BF16), jnp.exp2(sink - m)))
    while late:
        flush()
    gated = ot_ref[...].T * jax.nn.silu(z)
    r = _dot(gated.astype(BF16), wo_ref[...].astype(BF16))
    out_ref[0] = x + _rms(r, g_ref[...])


def _swa_layer(x, pre_g, post_g, w_in, sinks, w_out, rel_bias):
    bsz, L, d = x.shape
    width = SWA_HEADS * HEAD_DIM
    kvw = SWA_KV_HEADS * HEAD_DIM
    W = WINDOW
    log2e = math.log2(math.e)
    wb = w_in.astype(BF16)
    wqt = wb[:, :width].T
    wk = wb[:, width:width + kvw]
    wvt = wb[:, width + kvw:width + 2 * kvw].T
    wz = wb[:, width + 2 * kvw:]
    bvec = _swa_bias(rel_bias.astype(F32) * log2e)
    sink = jnp.repeat(sinks.astype(F32) * log2e, W).reshape(SWA_KV_HEADS, 1, SWA_GROUP * W)

    tq = SWA_WINDOWS_PER_STEP * W
    xspec = pl.BlockSpec((1, tq, d), lambda b, n: (b, n, 0))
    return pl.pallas_call(
        functools.partial(_swa_kernel, scale=HEAD_DIM ** -0.5 * log2e),
        out_shape=jax.ShapeDtypeStruct(x.shape, x.dtype),
        grid=(bsz, L // tq),
        in_specs=[xspec, _full((1, d)), _full(wqt.shape), _full(wk.shape), _full(wvt.shape), _full(wz.shape),
                  _full(bvec.shape), _full(sink.shape), _full(w_out.shape), _full((1, d))],
        out_specs=xspec,
        scratch_shapes=[pltpu.VMEM((width, tq), F32),
                        pltpu.VMEM((2, SWA_KV_HEADS, 2 * W, SWA_GROUP * W), F32),
                        pltpu.VMEM((W, kvw), BF16),
                        pltpu.VMEM((kvw, W), BF16)],
        compiler_params=_cparams(("arbitrary", "arbitrary")),
        name="swa_layer",
    )(x, pre_g.reshape(1, d), wqt, wk, wvt, wz, bvec, sink, w_out, post_g.reshape(1, d))


def _mla_pre_kernel(x_ref, g_ref, w_ref, qn_ref, kvn_ref, wq_ref, wkv_ref, wvt_ref, cq_ref, sq_ref, ck_ref, sk_ref,
                    oqn_ref, oqr_ref, okn_ref, okr_ref, ov_ref, oz_ref, *, scale):
    nope = MLA_HEADS * MLA_NOPE
    rope = MLA_HEADS * MLA_ROPE
    vw = MLA_HEADS * MLA_V
    hb = _rms(x_ref[0], g_ref[...]).astype(BF16)
    o1 = MLA_Q_RANK
    o2 = o1 + MLA_KV_RANK
    o3 = o2 + vw
    cq = _dot(hb, w_ref[:, :o1])
    ckv = _dot(hb, w_ref[:, o1:o2])
    oz_ref[0] = _dot(hb, w_ref[:, o2:o3])
    kr = _dot(hb, w_ref[:, o3:o3 + LANES])
    krs = _dot(hb, w_ref[:, o3 + LANES:o3 + 2 * LANES])
    okr_ref[0] = (kr * ck_ref[...] + krs * sk_ref[...]).astype(BF16)
    cqb = _rms(cq, qn_ref[...]).astype(BF16)
    oqn_ref[0] = (_dot_nt(wq_ref[:nope], cqb) * scale).astype(BF16)
    qr = _dot_nt(wq_ref[nope:nope + rope], cqb)
    hr = MLA_ROPE // 2
    qrs = jnp.concatenate([qr[h * MLA_ROPE + o:h * MLA_ROPE + o + hr]
                           for h in range(MLA_HEADS) for o in (hr, 0)], axis=0)
    oqr_ref[0] = ((qr * cq_ref[...] + qrs * sq_ref[...]) * scale).astype(BF16)
    ckb = _rms(ckv, kvn_ref[...]).astype(BF16)
    okn_ref[0] = _dot(ckb, wkv_ref[:, :nope]).astype(BF16)
    vt = _dot_nt(wvt_ref[...], ckb).astype(BF16)
    tk = ov_ref.shape[3]
    for c in range(ov_ref.shape[1]):
        ov_ref[0, c] = vt[:, c * tk:(c + 1) * tk]


def _mla_attn_kernel(qn_ref, qr_ref, kn_ref, kr_ref, v_ref, z_ref, x_ref, wo_ref, g_ref, out_ref,
                     qs_ref, acc_ref, m_ref, o_ref):
    tq = qn_ref.shape[2]
    tk = v_ref.shape[3]
    npairs = MLA_HEADS // 2
    i = pl.program_id(1)
    tri = (lax.broadcasted_iota(jnp.int32, (LANES, LANES), 0)
           <= lax.broadcasted_iota(jnp.int32, (LANES, LANES), 1))

    zn = jnp.zeros((MLA_NOPE, tq), BF16)
    zr = jnp.zeros((LANES - MLA_ROPE, tq), BF16)
    for p in range(npairs):
        qn = qn_ref[0, p * LANES:(p + 1) * LANES, :]
        r0 = 2 * p * MLA_ROPE
        c0 = jnp.concatenate([qn[:MLA_NOPE], zn, qr_ref[0, r0:r0 + MLA_ROPE, :], zr], axis=0)
        c1 = jnp.concatenate([zn, qn[MLA_NOPE:], qr_ref[0, r0 + MLA_ROPE:r0 + 2 * MLA_ROPE, :], zr], axis=0)
        qs_ref[p] = jnp.concatenate([c0, c1], axis=1)

    m_ref[...] = jnp.full(m_ref.shape, NEG_INF, F32)
    acc_ref[...] = jnp.zeros(acc_ref.shape, F32)
    ones = jnp.ones((acc_ref.shape[1] - LANES, tk), BF16)

    def kv_steps(blocks):
        units = [(j, masked, p) for j, masked in blocks for p in range(npairs)]

        def scores(j, masked, p):
            ks = pl.multiple_of(j * tk, tk)
            kc = jnp.concatenate([kn_ref[0, pl.ds(ks, tk), p * LANES:(p + 1) * LANES],
                                  kr_ref[0, pl.ds(ks, tk), :]], axis=1)
            if not masked:
                chunks = [_dot(kc, qs_ref[p, :, c * 2 * LANES:(c + 1) * 2 * LANES]) for c in range(tq // LANES)]
                return [chunks[c // 2][:, (c % 2) * LANES:(c % 2 + 1) * LANES] for c in range(2 * tq // LANES)]
            lo = [_dot(kc[:LANES], qs_ref[p, :, c * 2 * LANES:(c + 1) * 2 * LANES]) for c in range(tq // LANES)]
            late_q = jnp.concatenate([qs_ref[p, :, LANES:2 * LANES], qs_ref[p, :, 3 * LANES:4 * LANES]], axis=1)
            hi = _dot(kc[LANES:], late_q)
            strips = []
            for hd in range(2):
                strips.append(jnp.where(tri, lo[hd][:, :LANES], NEG_INF))
                strips.append(jnp.concatenate([lo[hd][:, LANES:],
                                               jnp.where(tri, hi[:, hd * LANES:(hd + 1) * LANES], NEG_INF)], axis=0))
            return strips

        pending = [scores(*u) for u in units[:MLA_LOOKAHEAD]]
        late = []

        def flush():
            jj, pp, alpha, pr = late.pop(0)
            vones = jnp.concatenate([v_ref[0, jj, pp * LANES:(pp + 1) * LANES, :], ones], axis=0)
            acc_ref[pp] = alpha * acc_ref[pp] + _dot(vones, pr)

        for idx, (j, masked, p) in enumerate(units):
            s = pending.pop(0)
            if idx + MLA_LOOKAHEAD < len(units):
                pending.append(scores(*units[idx + MLA_LOOKAHEAD]))
            probs, alphas = [], []
            for c, sc in enumerate(s):
                m_prev = m_ref[p, :, c * LANES:(c + 1) * LANES]
                m_new = jnp.maximum(m_prev, jnp.max(sc, axis=0, keepdims=True))
                alphas.append(jnp.exp2(m_prev - m_new))
                pr = jnp.exp2(sc - m_new).astype(BF16)
                if pr.shape[0] < tk:
                    pr = jnp.concatenate([pr, jnp.zeros((tk - pr.shape[0], LANES), BF16)], axis=0)
                probs.append(pr)
                m_ref[p, :, c * LANES:(c + 1) * LANES] = m_new
            if len(late) == MLA_PV_DELAY:
                flush()
            late.append((j, p, jnp.concatenate(alphas, axis=1), jnp.concatenate(probs, axis=1)))
        while late:
            flush()

    def body(jj, c):
        kv_steps([(2 * jj, False), (2 * jj + 1, False)])
        return c

    lax.fori_loop(0, i // 2, body, 0)

    @pl.when(i % 2 == 1)
    def _():
        kv_steps([(i - 1, False), (i, True)])

    @pl.when(i % 2 == 0)
    def _():
        kv_steps([(i, True)])
    for p in range(npairs):
        a = acc_ref[p]
        a = a[:LANES] * (1.0 / a[LANES:LANES + 1])
        ot = jnp.concatenate([a[:MLA_V, :tq], a[MLA_V:, tq:]], axis=0)
        o_ref[:, p * LANES:(p + 1) * LANES] = ot.T
    gated = o_ref[...] * jax.nn.silu(z_ref[0])
    r = _dot(gated.astype(BF16), wo_ref[...].astype(BF16))
    out_ref[0] = x_ref[0] + _rms(r, g_ref[...])


def _mla_layer(x, pre_g, post_g, w_in, q_norm, kv_norm, w_uq, w_ukv, w_out):
    bsz, L, d = x.shape
    H = MLA_HEADS
    dq = MLA_NOPE + MLA_ROPE
    nope = H * MLA_NOPE
    rope = H * MLA_ROPE
    vw = H * MLA_V
    half = MLA_ROPE // 2
    o_kr = MLA_Q_RANK + MLA_KV_RANK
    o_z = o_kr + MLA_ROPE
    wb = w_in.astype(BF16)
    w_kr = wb[:, o_kr:o_z]
    w_krs = jnp.concatenate([w_kr[:, half:], w_kr[:, :half]], axis=1)
    reps = LANES // MLA_ROPE
    w1 = jnp.concatenate([wb[:, :o_kr], wb[:, o_z:]] + [w_kr] * reps + [w_krs] * reps, axis=1)
    wq3 = w_uq.astype(BF16).reshape(MLA_Q_RANK, H, dq)
    wqt = jnp.concatenate([wq3[:, :, :MLA_NOPE].reshape(MLA_Q_RANK, nope),
                           wq3[:, :, MLA_NOPE:].reshape(MLA_Q_RANK, rope)], axis=1).T
    wkv3 = w_ukv.astype(BF16).reshape(MLA_KV_RANK, H, MLA_NOPE + MLA_V)
    wkn = wkv3[:, :, :MLA_NOPE].reshape(MLA_KV_RANK, nope)
    wvt = wkv3[:, :, MLA_NOPE:].reshape(MLA_KV_RANK, vw).T
    inv = ROPE_BASE ** (-jnp.arange(0, MLA_ROPE, 2, dtype=F32) / MLA_ROPE)
    ang = jnp.arange(L, dtype=F32)[:, None] * inv[None, :]
    cos, sin = jnp.cos(ang), jnp.sin(ang)
    cos32 = jnp.concatenate([cos, cos], axis=1)
    sin32 = jnp.concatenate([-sin, sin], axis=1)
    cos_k, sin_k = jnp.tile(cos32, (1, LANES // MLA_ROPE)), jnp.tile(sin32, (1, LANES // MLA_ROPE))
    cos_q, sin_q = jnp.tile(cos32, (1, H)).T, jnp.tile(sin32, (1, H)).T

    tm = ROW_TILE
    tk = MLA_TK
    tok = lambda w_: pl.BlockSpec((1, tm, w_), lambda b, i: (b, i, 0))
    tokt = lambda w_: pl.BlockSpec((1, w_, tm), lambda b, i: (b, 0, i))
    scale = dq ** -0.5 * math.log2(math.e)
    qn, qr, kn, kr, v, z = pl.pallas_call(
        functools.partial(_mla_pre_kernel, scale=scale),
        out_shape=[jax.ShapeDtypeStruct((bsz, nope, L), BF16),
                   jax.ShapeDtypeStruct((bsz, rope, L), BF16),
                   jax.ShapeDtypeStruct((bsz, L, nope), BF16),
                   jax.ShapeDtypeStruct((bsz, L, LANES), BF16),
                   jax.ShapeDtypeStruct((bsz, L // tk, vw, tk), BF16),
                   jax.ShapeDtypeStruct((bsz, L, vw), F32)],
        grid=(bsz, L // tm),
        in_specs=[tok(d), _full((1, d)), _full(w1.shape), _full((1, MLA_Q_RANK)), _full((1, MLA_KV_RANK)),
                  _full(wqt.shape), _full(wkn.shape), _full(wvt.shape),
                  pl.BlockSpec((rope, tm), lambda b, i: (0, i)), pl.BlockSpec((rope, tm), lambda b, i: (0, i)),
                  pl.BlockSpec((tm, LANES), lambda b, i: (i, 0)), pl.BlockSpec((tm, LANES), lambda b, i: (i, 0))],
        out_specs=[tokt(nope), tokt(rope), tok(nope), tok(LANES),
                   pl.BlockSpec((1, tm // tk, vw, tk), lambda b, i: (b, i, 0, 0)), tok(vw)],
        compiler_params=_cparams(("parallel", "parallel")),
        name="mla_pre",
    )(x, pre_g.reshape(1, d), w1, q_norm.reshape(1, -1), kv_norm.reshape(1, -1), wqt, wkn, wvt,
      cos_q, sin_q, cos_k, sin_k)

    tq = MLA_TQ
    assert MLA_TQ == MLA_TK == 2 * LANES, "the diagonal-block handling works on 128-query strips of a 256 block"
    npairs = H // 2
    qspec = lambda w_: pl.BlockSpec((1, w_, tq), lambda b, i: (b, 0, i))
    kspec = lambda w_: pl.BlockSpec((1, L, w_), lambda b, i: (b, 0, 0))
    rowspec = lambda w_: pl.BlockSpec((1, tq, w_), lambda b, i: (b, i, 0))
    return pl.pallas_call(
        _mla_attn_kernel,
        out_shape=jax.ShapeDtypeStruct(x.shape, x.dtype),
        grid=(bsz, L // tq),
        in_specs=[qspec(nope), qspec(rope), kspec(nope), kspec(LANES),
                  pl.BlockSpec((1, L // tk, vw, tk), lambda b, i: (b, 0, 0, 0)),
                  rowspec(vw), rowspec(d), _full(w_out.shape), _full((1, d))],
        out_specs=rowspec(d),
        scratch_shapes=[pltpu.VMEM((npairs, 2 * LANES, 2 * tq), BF16),
                        pltpu.VMEM((npairs, LANES + 16, 2 * tq), F32),
                        pltpu.VMEM((npairs, 1, 2 * tq), F32),
                        pltpu.VMEM((tq, vw), F32)],
        compiler_params=_cparams(("parallel", "arbitrary")),
        name="mla_attn",
    )(qn, qr, kn, kr, v, z, x, w_out, post_g.reshape(1, d))


def _sgu_kernel(x_ref, g_ref, w_ref, lng_ref, lnb_ref, ws_ref, bs_ref, wo_ref, pg_ref, out_ref, s_ref):
    width = wo_ref.shape[0]
    tm = x_ref.shape[1]
    lane = lax.broadcasted_iota(jnp.int32, (1, LANES), 1)
    lo = lane < HALF
    x = x_ref[0]
    hb = _rms(x, g_ref[...]).astype(BF16)
    v = jax.nn.gelu(_dot(hb, w_ref[:, width:2 * width].astype(BF16)))
    mu = jnp.mean(v, axis=-1, keepdims=True)
    vc = v - mu
    var = jnp.mean(vc * vc, axis=-1, keepdims=True)
    vb = (vc * lax.rsqrt(var + EPS) * lng_ref[...] + lnb_ref[...]).astype(BF16)
    group = SGU_STACK
    for c0 in range(0, tm // SGU_CHUNK, group):
        for jj in range(width // LANES):
            blk = jnp.concatenate([vb[c * SGU_CHUNK:(c + 1) * SGU_CHUNK, jj * LANES:(jj + 1) * LANES]
                                   for c in range(c0, c0 + group)], axis=1)
            r = _dot(ws_ref[jj], blk)
            for k in range(group):
                c = c0 + k
                s_ref[c * SGU_CHUNK:(c + 1) * SGU_CHUNK, jj * LANES:(jj + 1) * LANES] = (
                    jnp.where(lo, r[:SGU_CHUNK, k * LANES:(k + 1) * LANES],
                              r[SGU_CHUNK:, k * LANES:(k + 1) * LANES]) + bs_ref[jj])
    u = jax.nn.gelu(_dot(hb, w_ref[:, :width].astype(BF16)))
    z = _dot(hb, w_ref[:, 2 * width:].astype(BF16))
    o = u * s_ref[...] * jax.nn.silu(z)
    r = _dot(o.astype(BF16), wo_ref[...].astype(BF16))
    out_ref[0] = x + _rms(r, pg_ref[...])


def _sgu_layer(x, pre_g, post_g, w_in, ln_g, ln_b, w_s, b_s, w_out):
    bsz, L, d = x.shape
    width = w_out.shape[0]
    T = SGU_CHUNK
    gd = width // SGU_GROUPS
    tril = jnp.tril(jnp.ones((T, T), dtype=bool))
    ws = jnp.where(tril[None], w_s, 0.0).reshape(SGU_GROUPS // 2, 2 * T, T).astype(BF16)
    bs = jnp.repeat(b_s.astype(F32).T, gd, axis=1)
    bs = bs.reshape(T, width // LANES, LANES).transpose(1, 0, 2)
    tm = ROW_TILE
    return pl.pallas_call(
        _sgu_kernel,
        out_shape=jax.ShapeDtypeStruct(x.shape, x.dtype),
        grid=(bsz, L // tm),
        in_specs=[pl.BlockSpec((1, tm, d), lambda b, i: (b, i, 0)),
                  _full((1, d)), _full(w_in.shape), _full((1, width)), _full((1, width)),
                  _full(ws.shape), _full(bs.shape), _full(w_out.shape), _full((1, d))],
        out_specs=pl.BlockSpec((1, tm, d), lambda b, i: (b, i, 0)),
        scratch_shapes=[pltpu.VMEM((tm, width), F32)],
        compiler_params=_cparams(("parallel", "parallel")),
        name="sgu",
    )(x, pre_g.reshape(1, d), w_in, ln_g.reshape(1, width), ln_b.reshape(1, width),
      ws, bs, w_out, post_g.reshape(1, d))


def kernel(x, pre_norm, post_norm, rel_bias, a_w_in, a_lam_re, a_lam_im, a_log_dt, a_b_re, a_b_im, a_c_re, a_c_im, a_d, a_w_glu, a_b_glu, a_w_out, b_w_in, b_sinks, b_w_out, c_w_in, c_q_norm, c_kv_norm, c_w_uq, c_w_ukv, c_w_out, d_w_in, d_ln_g, d_ln_b, d_w_s, d_b_s, d_w_out):
    depth = pre_norm.shape[0]
    for i in range(depth):
        kind, j = i % 4, i // 4
        if kind == 0:
            x = _s5_layer(x, pre_norm[i], post_norm[i], a_w_in[j], a_lam_re[j], a_lam_im[j], a_log_dt[j],
                          a_b_re[j], a_b_im[j], a_c_re[j], a_c_im[j], a_d[j], a_w_glu[j], a_b_glu[j],
                          a_w_out[j])
        elif kind == 1:
            x = _swa_layer(x, pre_norm[i], post_norm[i], b_w_in[j], b_sinks[j], b_w_out[j], rel_bias)
        elif kind == 2:
            x = _mla_layer(x, pre_norm[i], post_norm[i], c_w_in[j], c_q_norm[j], c_kv_norm[j], c_w_uq[j],
                           c_w_ukv[j], c_w_out[j])
        else:
            x = _sgu_layer(x, pre_norm[i], post_norm[i], d_w_in[j], d_ln_g[j], d_ln_b[j], d_w_s[j],
                           d_b_s[j], d_w_out[j])
    return x
```
